```python
import math
import jax
import jax.numpy as jnp
from jax import lax
import numpy as np

D_MODEL = 1024
BATCH = 2
SEQ = 8192
DEPTH = 2
DEC_BATCH = 32
DEC_SEQ = 1
PAST_LEN = 8192
PAGE_SIZE = 128

EPS = 1e-6
N_MOD = 6
GDN_HEADS = 4
GDN_DK = 128
GDN_DV = 128
GDN_CONV = 4
GDN_CHUNK = 64
GDN_CONV_CH = GDN_HEADS * (2 * GDN_DK + GDN_DV)
NSA_HEADS = 8
NSA_KV_GROUPS = 2
NSA_HPG = NSA_HEADS // NSA_KV_GROUPS
NSA_DH = 64
NSA_BLOCK = 64
NSA_TOPN = 16
NSA_LOCAL = 2
NSA_WINDOW = 512
NSA_QBLOCK = 128
NSA_FORCE = 1.0e4
REL_BUCKETS = 32
REL_MAX_DIST = 2048
RNN_WIDTH = D_MODEL
RNN_BLOCKS = 8
RNN_BW = RNN_WIDTH // RNN_BLOCKS
RNN_CONV = 4
RG_C = 8.0
FFN_DIM = 2816
N_EXPERTS = 8
TOP_K = 2
EXPERT_DIM = 3584
MOE_BLOCK = 128
P0_SIZES = (GDN_CONV_CH, GDN_HEADS * GDN_DV, GDN_HEADS, GDN_HEADS, NSA_HEADS * NSA_DH, 6 * NSA_KV_GROUPS * NSA_DH, 3 * NSA_HEADS)
P0_WIDTH = sum(P0_SIZES)
MIX_WIDTH = GDN_HEADS * GDN_DV + NSA_HEADS * NSA_DH

kernel_name = 'hybrid_gdn_nsa_rglru_moe_step'


def _split_last(a, sizes):
    out, start = [], 0
    for n in sizes:
        out.append(a[..., start:start + n])
        start += n
    return out


def rms_norm(x, w):
    xf = x.astype(jnp.float32)
    return xf * lax.rsqrt(jnp.mean(xf * xf, axis=-1, keepdims=True) + EPS) * w.astype(jnp.float32)


def _l2norm(a):
    return a * lax.rsqrt(jnp.sum(a * a, axis=-1, keepdims=True) + EPS)


def adaln_params(c, w, b):
    m = jax.nn.silu(c.astype(jnp.float32)) @ w.astype(jnp.float32) + b.astype(jnp.float32)
    m = m.reshape(c.shape[0], N_MOD, 1, D_MODEL)
    return [m[:, i] for i in range(N_MOD)]


def modulate(x, gain, shift, scale):
    return (rms_norm(x, gain) * (1.0 + scale) + shift).astype(x.dtype)


def causal_conv(x, buf, w, b=None):
    t = x.shape[1]
    xc = jnp.concatenate([buf.astype(x.dtype), x], axis=1)
    y = xc[:, 0:t] * w[0]
    for j in range(1, w.shape[0]):
        y = y + xc[:, j:j + t] * w[j]
    if b is not None:
        y = y + b
    return y, xc[:, t:]


def swiglu(h, wg, wu, wd):
    return (jax.nn.silu(h @ wg) * (h @ wu)) @ wd


def gdn_core(q, k, v, log_a, beta, s0):
    bsz, t, nh, _ = q.shape
    dv = v.shape[-1]
    c = GDN_CHUNK
    tp = -(-t // c) * c

    def chunked(a):
        a = jnp.pad(a, [(0, 0), (0, tp - t)] + [(0, 0)] * (a.ndim - 2))
        a = a.reshape((bsz, tp // c, c) + a.shape[2:])
        return jnp.swapaxes(a, 2, 3)

    qc, kc, vc, la, bt = (chunked(a) for a in (q, k, v, log_a, beta))
    g = jnp.cumsum(la, axis=-1)
    incl = jnp.tril(jnp.ones((c, c), bool))
    strict = jnp.tril(jnp.ones((c, c), bool), -1)
    gam = jnp.where(incl, jnp.exp(jnp.where(incl, g[..., :, None] - g[..., None, :], 0.0)), 0.0)
    kk = jnp.einsum('bnhid,bnhjd->bnhij', kc, kc)
    m = jnp.where(strict, bt[..., :, None] * gam * kk, 0.0) + jnp.eye(c, dtype=jnp.float32)
    rhs = jnp.concatenate([bt[..., None] * vc, (bt * jnp.exp(g))[..., None] * kc], axis=-1)
    sol = lax.linalg.triangular_solve(m, rhs, left_side=True, lower=True, unit_diagonal=True)
    vb, w = sol[..., :dv], sol[..., dv:]
    aqk = jnp.einsum('bnhid,bnhjd->bnhij', qc, kc) * gam
    qg = qc * jnp.exp(g)[..., None]
    kd = kc * jnp.exp(g[..., -1:] - g)[..., None]
    gc = jnp.exp(g[..., -1])

    def step(s, xs):
        vb_n, w_n, aqk_n, qg_n, kd_n, gc_n = xs
        u = vb_n - jnp.einsum('bhcd,bhde->bhce', w_n, s)
        o = jnp.einsum('bhcd,bhde->bhce', qg_n, s) + jnp.einsum('bhij,bhje->bhie', aqk_n, u)
        s = gc_n[..., None, None] * s + jnp.einsum('bhcd,bhce->bhde', kd_n, u)
        return s, o

    xs = tuple(jnp.moveaxis(a, 1, 0) for a in (vb, w, aqk, qg, kd, gc))
    s_fin, o = lax.scan(step, s0, xs)
    o = jnp.transpose(o, (1, 0, 3, 2, 4)).reshape(bsz, tp, nh, dv)[:, :t]
    return o, s_fin


def gdn_mixer(qkv_raw, z, a_raw, b_raw, conv_buf, s0, conv_w, a_log, dt_bias, norm_w):
    f32 = jnp.float32
    bsz, t, _ = qkv_raw.shape
    y, new_buf = causal_conv(qkv_raw, conv_buf, conv_w)
    y = jax.nn.silu(y.astype(f32))
    q, k, v = _split_last(y, (GDN_HEADS * GDN_DK, GDN_HEADS * GDN_DK, GDN_HEADS * GDN_DV))
    q = _l2norm(q.reshape(bsz, t, GDN_HEADS, GDN_DK)) * GDN_DK ** -0.5
    k = _l2norm(k.reshape(bsz, t, GDN_HEADS, GDN_DK))
    v = v.reshape(bsz, t, GDN_HEADS, GDN_DV)
    log_a = -jnp.exp(a_log.astype(f32)) * jax.nn.softplus(a_raw.astype(f32) + dt_bias.astype(f32))
    beta = jax.nn.sigmoid(b_raw.astype(f32))
    o, s_fin = gdn_core(q, k, v, log_a, beta, s0.astype(f32))
    o = o * lax.rsqrt(jnp.mean(o * o, axis=-1, keepdims=True) + EPS) * norm_w.astype(f32)
    o = o * jax.nn.silu(z.astype(f32).reshape(bsz, t, GDN_HEADS, GDN_DV))
    return o.reshape(bsz, t, GDN_HEADS * GDN_DV), new_buf, s_fin


def t5_bucket(dist):
    n = jnp.maximum(dist, 0)
    exact = REL_BUCKETS // 2
    nf = jnp.maximum(n, exact).astype(jnp.float32)
    large = exact + (jnp.log(nf / exact) / math.log(REL_MAX_DIST / exact) * (REL_BUCKETS - exact)).astype(jnp.int32)
    return jnp.where(n < exact, n, jnp.minimum(large, REL_BUCKETS - 1))


def head_bias(rel_bias, bucket):
    tq, nk = bucket.shape
    b = rel_bias.astype(jnp.float32)[bucket].reshape(tq, nk, NSA_KV_GROUPS, NSA_HPG)
    return jnp.transpose(b, (0, 2, 3, 1))


def masked_softmax(s, mask):
    s = jnp.where(mask, s.astype(jnp.float32), -1e30)
    m = jnp.max(s, axis=-1, keepdims=True)
    p = jnp.where(mask, jnp.exp(s - m), 0.0)
    return p / jnp.maximum(jnp.sum(p, axis=-1, keepdims=True), 1e-30)


def nsa_compress_and_block(kv4):
    bsz, length = kv4.shape[:2]
    lp = -(-length // NSA_BLOCK) * NSA_BLOCK
    kv4 = jnp.pad(kv4, ((0, 0), (0, lp - length), (0, 0), (0, 0), (0, 0)))
    blocks = kv4.reshape(bsz, lp // NSA_BLOCK, NSA_BLOCK, 4, NSA_KV_GROUPS, NSA_DH)
    means = jnp.mean(blocks[:, :, :, :2], axis=2)
    sel = jnp.transpose(blocks[:, :, :, 2:], (0, 3, 4, 1, 2, 5))
    return means[:, :, 0], means[:, :, 1], sel[:, 0], sel[:, 1]


def nsa_block(q, gates, qpos, kc, vc, kb, vb, kw, vw, kwpos, rel_bias):
    bsz, tq = q.shape[:2]
    nblk = kc.shape[1]
    qg = q.reshape(bsz, tq, NSA_KV_GROUPS, NSA_HPG, NSA_DH)
    cend = (jnp.arange(nblk) + 1) * NSA_BLOCK - 1
    dist_c = qpos[:, None] - cend[None, :]
    s_c = jnp.einsum('bqghd,bngd->bqghn', qg, kc) + head_bias(rel_bias, t5_bucket(dist_c))[None]
    p_c = masked_softmax(s_c, (dist_c >= 0)[None, :, None, None, :])
    o_c = jnp.einsum('bqghn,bngd->bqghd', p_c, vc)
    blk = jnp.arange(nblk)[None, :]
    cur = (qpos // NSA_BLOCK)[:, None]
    valid = blk <= cur
    forced = valid & ((blk == 0) | (blk > cur - NSA_LOCAL))
    imp = jnp.where(forced[None, :, None, :], NSA_FORCE,
                    jnp.where(valid[None, :, None, :], jnp.sum(p_c, axis=3), -1.0))
    n_sel = min(NSA_TOPN, nblk)
    _, sel = lax.top_k(imp, n_sel)
    bi = jnp.arange(bsz)[:, None, None, None]
    gi = jnp.arange(NSA_KV_GROUPS)[None, None, :, None]
    nk = n_sel * NSA_BLOCK
    ks = kb[bi, gi, sel].reshape(bsz, tq, NSA_KV_GROUPS, nk, NSA_DH)
    vs = vb[bi, gi, sel].reshape(bsz, tq, NSA_KV_GROUPS, nk, NSA_DH)
    kpos_s = (sel[..., None] * NSA_BLOCK + jnp.arange(NSA_BLOCK)).reshape(bsz, tq, NSA_KV_GROUPS, nk)
    dist_s = qpos[None, :, None, None] - kpos_s
    bias_s = rel_bias.astype(jnp.float32).reshape(REL_BUCKETS, NSA_KV_GROUPS, NSA_HPG)[t5_bucket(dist_s), gi]
    s_s = jnp.einsum('bqghd,bqgkd->bqghk', qg, ks) + jnp.moveaxis(bias_s, -1, 3)
    p_s = masked_softmax(s_s, (dist_s >= 0)[:, :, :, None, :])
    o_s = jnp.einsum('bqghk,bqgkd->bqghd', p_s, vs)
    dist_w = qpos[:, None] - kwpos[None, :]
    mask_w = (kwpos[None, :] >= 0) & (dist_w >= 0) & (dist_w < NSA_WINDOW)
    s_w = jnp.einsum('bqghd,bkgd->bqghk', qg, kw) + head_bias(rel_bias, t5_bucket(dist_w))[None]
    p_w = masked_softmax(s_w, mask_w[None, :, None, None, :])
    o_w = jnp.einsum('bqghk,bkgd->bqghd', p_w, vw)
    g = gates.reshape(bsz, tq, NSA_KV_GROUPS, NSA_HPG, 3)
    o = g[..., 0:1] * o_c + g[..., 1:2] * o_s + g[..., 2:3] * o_w
    return o.reshape(bsz, tq, NSA_HEADS * NSA_DH)


def nsa_prompt(q, gates, kv, rel_bias):
    f32 = jnp.float32
    bsz, s = q.shape[:2]
    kvf = kv.astype(f32)
    kc, vc, kb, vb = nsa_compress_and_block(kvf[:, :, :4])
    pad = ((0, 0), (NSA_WINDOW, 0), (0, 0), (0, 0))
    kw = jnp.pad(kvf[:, :, 4], pad)
    vw = jnp.pad(kvf[:, :, 5], pad)
    span = NSA_WINDOW + NSA_QBLOCK

    def body(i):
        s0 = i * NSA_QBLOCK
        qi = lax.dynamic_slice_in_dim(q, s0, NSA_QBLOCK, axis=1)
        gti = lax.dynamic_slice_in_dim(gates, s0, NSA_QBLOCK, axis=1)
        kwi = lax.dynamic_slice_in_dim(kw, s0, span, axis=1)
        vwi = lax.dynamic_slice_in_dim(vw, s0, span, axis=1)
        qpos = s0 + jnp.arange(NSA_QBLOCK)
        kwpos = s0 - NSA_WINDOW + jnp.arange(span)
        return nsa_block(qi, gti, qpos, kc, vc, kb, vb, kwi, vwi, kwpos, rel_bias)

    o = lax.map(body, jnp.arange(s // NSA_QBLOCK))
    return jnp.transpose(o, (1, 0, 2, 3)).reshape(bsz, s, NSA_HEADS * NSA_DH)


def nsa_sample(q, gates, kv_new, cache_kv, cache_win, page_table, rel_bias):
    f32 = jnp.float32
    db, t = q.shape[:2]
    past = page_table.shape[1] * cache_kv.shape[1]
    past_kv = cache_kv[page_table].reshape((db, past) + cache_kv.shape[2:])
    full = jnp.concatenate([past_kv.astype(f32), kv_new[:, :, :4].astype(f32)], axis=1)
    kc, vc, kb, vb = nsa_compress_and_block(full)
    wb = cache_win.shape[1]
    win = jnp.concatenate([cache_win, kv_new[:, :, 4:].astype(cache_win.dtype)], axis=1)
    winf = win.astype(f32)
    kwpos = past - wb + jnp.arange(wb + t)
    qpos = past + jnp.arange(t)
    o = nsa_block(q, gates, qpos, kc, vc, kb, vb, winf[:, :, 0], winf[:, :, 1], kwpos, rel_bias)
    return o, win[:, t:]


def rglru_mixer(xr, conv_buf, h0, conv_w, conv_b, wa, ba, wx, bx, lam):
    f32 = jnp.float32
    bsz, t, _ = xr.shape
    xc, new_buf = causal_conv(xr, conv_buf, conv_w, conv_b)
    xf = xc.astype(f32)
    xb = xf.reshape(bsz, t, RNN_BLOCKS, RNN_BW)
    r = jax.nn.sigmoid(jnp.einsum('btnd,nde->btne', xb, wa.astype(f32)).reshape(bsz, t, RNN_WIDTH) + ba.astype(f32))
    i = jax.nn.sigmoid(jnp.einsum('btnd,nde->btne', xb, wx.astype(f32)).reshape(bsz, t, RNN_WIDTH) + bx.astype(f32))
    log_a = -RG_C * r * jax.nn.softplus(-lam.astype(f32))
    a = jnp.exp(log_a)
    b = jnp.sqrt(jnp.maximum(-jnp.expm1(2.0 * log_a), 0.0)) * (i * xf)
    b = b.at[:, 0].add(a[:, 0] * h0.astype(f32))

    def combine(left, right):
        return left[0] * right[0], right[0] * left[1] + right[1]

    _, h = lax.associative_scan(combine, (a, b), axis=1)
    return h, new_buf, h[:, -1]


def moe_swiglu(h, router, w1, w3, w2):
    f32 = jnp.float32
    n, d = h.shape
    logits = (h @ router).astype(f32)
    top_val, top_idx = lax.top_k(logits, TOP_K)
    gate = jax.nn.softmax(top_val, axis=-1)
    n_assign = n * TOP_K
    e_flat = top_idx.reshape(n_assign)
    tok_flat = jnp.repeat(jnp.arange(n, dtype=jnp.int32), TOP_K)
    w_flat = gate.reshape(n_assign)
    order = jnp.argsort(e_flat, stable=True)
    e_sorted = e_flat[order]
    counts = jnp.zeros((N_EXPERTS,), jnp.int32).at[e_flat].add(1)
    padded = (counts + MOE_BLOCK - 1) // MOE_BLOCK * MOE_BLOCK
    start = jnp.cumsum(counts) - counts
    pend = jnp.cumsum(padded)
    pstart = pend - padded
    dest = pstart[e_sorted] + jnp.arange(n_assign) - start[e_sorted]
    n_blocks = -(-n_assign // MOE_BLOCK) + N_EXPERTS
    rows = n_blocks * MOE_BLOCK
    row_tok = jnp.zeros((rows,), jnp.int32).at[dest].set(tok_flat[order])
    row_w = jnp.zeros((rows,), f32).at[dest].set(w_flat[order])
    blk_e = jnp.minimum(jnp.searchsorted(pend, jnp.arange(n_blocks) * MOE_BLOCK, side='right'), N_EXPERTS - 1)
    xs = h[row_tok].reshape(n_blocks, MOE_BLOCK, d)

    def expert_block(args):
        xb, e = args
        return (jax.nn.silu(xb @ w1[e]) * (xb @ w3[e])) @ w2[e]

    yb = lax.map(expert_block, (xs, blk_e)).reshape(rows, d)
    y = jnp.zeros((n, d), f32).at[row_tok].add(yb.astype(f32) * row_w[:, None])
    return y.astype(h.dtype)


def trunk(x, c, nsa_mix, st, p):
    bsz, t, _ = x.shape
    f32 = jnp.float32
    new = {}
    for layer in range(DEPTH):
        mod = adaln_params(c, p['w_ada'][layer], p['b_ada'][layer])
        h = modulate(x, p['norm_mix'][layer], mod[0], mod[1])
        if layer % 2 == 0:
            qkv_raw, z, a_raw, b_raw, q_b, kv_b, g_b = _split_last(h @ p['w_in0'], P0_SIZES)
            o_a, new['gdn_conv'], new['gdn'] = gdn_mixer(qkv_raw, z, a_raw, b_raw, st['gdn_conv'], st['gdn'],
                                                         p['gdn_conv_w'], p['gdn_a_log'], p['gdn_dt_bias'], p['gdn_norm_w'])
            q_b = q_b.astype(f32).reshape(bsz, t, NSA_HEADS, NSA_DH) * NSA_DH ** -0.5
            g_b = jax.nn.sigmoid(g_b.astype(f32)).reshape(bsz, t, NSA_HEADS, 3)
            kv_b = kv_b.reshape(bsz, t, 6, NSA_KV_GROUPS, NSA_DH)
            o_b, new['nsa_win'] = nsa_mix(q_b, g_b, kv_b)
            new['nsa_kv'] = kv_b[:, :, :4]
            mix = jnp.concatenate([o_a, o_b], axis=-1).astype(x.dtype) @ p['w_out0']
            x = x + (mod[2] * mix).astype(x.dtype)
            h = modulate(x, p['norm_ffn'][layer], mod[3], mod[4])
            y = swiglu(h, p['ffn_w_gate'], p['ffn_w_up'], p['ffn_w_down'])
            x = x + (mod[5] * y).astype(x.dtype)
        else:
            gate_br, rec_br = _split_last(h @ p['w_in1'], (RNN_WIDTH, RNN_WIDTH))
            hr, new['lru_conv'], new['lru'] = rglru_mixer(rec_br, st['lru_conv'], st['lru'], p['lru_conv_w'], p['lru_conv_b'],
                                                          p['lru_wa'], p['lru_ba'], p['lru_wx'], p['lru_bx'], p['lru_lambda'])
            y = (jax.nn.gelu(gate_br.astype(f32)) * hr).astype(x.dtype) @ p['w_out1']
            x = x + (mod[2] * y).astype(x.dtype)
            h = modulate(x, p['norm_ffn'][layer], mod[3], mod[4])
            y = moe_swiglu(h.reshape(bsz * t, D_MODEL), p['moe_router'], p['moe_w1'], p['moe_w3'], p['moe_w2'])
            x = x + (mod[5] * y.reshape(bsz, t, D_MODEL)).astype(x.dtype)
    return rms_norm(x, p['norm_final']).astype(x.dtype), new


def setup_inputs(seed: int = 0) -> dict:
    key = jax.random.key(seed)
    ks = list(jax.random.split(key, 48))
    f32 = jnp.float32

    def nrm(shape, scale=1.0):
        return scale * jax.random.normal(ks.pop(), shape, f32)

    n_pages = PAST_LEN // PAGE_SIZE
    n_pool = (DEC_BATCH * n_pages * 5 + 3) // 4
    win_buf = min(NSA_WINDOW, PAST_LEN)
    page_table = jax.random.permutation(ks.pop(), n_pool)[:DEC_BATCH * n_pages].reshape(DEC_BATCH, n_pages).astype(jnp.int32)
    a_init = jax.random.uniform(ks.pop(), (GDN_HEADS,), f32, 1.0, 16.0)
    dt = jnp.exp(jax.random.uniform(ks.pop(), (GDN_HEADS,), f32, math.log(1e-3), math.log(1e-1)))
    a_target = jax.random.uniform(ks.pop(), (RNN_WIDTH,), f32, 0.9, 0.999)
    sig = a_target ** (1.0 / RG_C)
    d_in = D_MODEL ** -0.5
    return {
        'x_prompt': nrm((BATCH, SEQ, D_MODEL)),
        'x_sample': nrm((DEC_BATCH, DEC_SEQ, D_MODEL)),
        'c_prompt': nrm((BATCH, D_MODEL)),
        'c_sample': nrm((DEC_BATCH, D_MODEL)),
        'cache_nsa_kv': nrm((n_pool, PAGE_SIZE, 4, NSA_KV_GROUPS, NSA_DH)),
        'cache_nsa_win': nrm((DEC_BATCH, win_buf, 2, NSA_KV_GROUPS, NSA_DH)),
        'state_gdn': nrm((DEC_BATCH, GDN_HEADS, GDN_DK, GDN_DV), 0.3),
        'state_gdn_conv': nrm((DEC_BATCH, GDN_CONV - 1, GDN_CONV_CH)),
        'state_lru': nrm((DEC_BATCH, RNN_WIDTH), 0.5),
        'state_lru_conv': nrm((DEC_BATCH, RNN_CONV - 1, RNN_WIDTH)),
        'page_table': page_table,
        'rel_bias': nrm((REL_BUCKETS, NSA_HEADS), 0.3),
        'w_ada': nrm((DEPTH, D_MODEL, N_MOD * D_MODEL), 0.5 * d_in),
        'b_ada': nrm((DEPTH, N_MOD * D_MODEL), 0.02),
        'norm_mix': 1.0 + nrm((DEPTH, D_MODEL), 0.02),
        'norm_ffn': 1.0 + nrm((DEPTH, D_MODEL), 0.02),
        'norm_final': 1.0 + nrm((D_MODEL,), 0.02),
        'w_in0': nrm((D_MODEL, P0_WIDTH), d_in),
        'gdn_conv_w': nrm((GDN_CONV, GDN_CONV_CH), 0.5),
        'gdn_a_log': jnp.log(a_init),
        'gdn_dt_bias': dt + jnp.log(-jnp.expm1(-dt)),
        'gdn_norm_w': 1.0 + nrm((GDN_DV,), 0.02),
        'w_out0': nrm((MIX_WIDTH, D_MODEL), MIX_WIDTH ** -0.5),
        'ffn_w_gate': nrm((D_MODEL, FFN_DIM), d_in),
        'ffn_w_up': nrm((D_MODEL, FFN_DIM), d_in),
        'ffn_w_down': nrm((FFN_DIM, D_MODEL), FFN_DIM ** -0.5),
        'w_in1': nrm((D_MODEL, 2 * RNN_WIDTH), d_in),
        'lru_conv_w': nrm((RNN_CONV, RNN_WIDTH), 0.5),
        'lru_conv_b': nrm((RNN_WIDTH,), 0.02),
        'lru_wa': nrm((RNN_BLOCKS, RNN_BW, RNN_BW), RNN_BW ** -0.5),
        'lru_ba': nrm((RNN_WIDTH,), 0.02),
        'lru_wx': nrm((RNN_BLOCKS, RNN_BW, RNN_BW), RNN_BW ** -0.5),
        'lru_bx': nrm((RNN_WIDTH,), 0.02),
        'lru_lambda': jnp.log(sig) - jnp.log1p(-sig),
        'w_out1': nrm((RNN_WIDTH, D_MODEL), RNN_WIDTH ** -0.5),
        'moe_router': nrm((D_MODEL, N_EXPERTS), d_in),
        'moe_w1': nrm((N_EXPERTS, D_MODEL, EXPERT_DIM), d_in),
        'moe_w3': nrm((N_EXPERTS, D_MODEL, EXPERT_DIM), d_in),
        'moe_w2': nrm((N_EXPERTS, EXPERT_DIM, D_MODEL), EXPERT_DIM ** -0.5),
    }


def reference(x_prompt, x_sample, c_prompt, c_sample, cache_nsa_kv, cache_nsa_win, state_gdn, state_gdn_conv,
              state_lru, state_lru_conv, page_table, rel_bias, w_ada, b_ada, norm_mix, norm_ffn, norm_final,
              w_in0, gdn_conv_w, gdn_a_log, gdn_dt_bias, gdn_norm_w, w_out0, ffn_w_gate, ffn_w_up, ffn_w_down,
              w_in1, lru_conv_w, lru_conv_b, lru_wa, lru_ba, lru_wx, lru_bx, lru_lambda, w_out1,
              moe_router, moe_w1, moe_w3, moe_w2):
    p = {'w_ada': w_ada, 'b_ada': b_ada, 'norm_mix': norm_mix, 'norm_ffn': norm_ffn, 'norm_final': norm_final,
         'w_in0': w_in0, 'gdn_conv_w': gdn_conv_w, 'gdn_a_log': gdn_a_log, 'gdn_dt_bias': gdn_dt_bias,
         'gdn_norm_w': gdn_norm_w, 'w_out0': w_out0, 'ffn_w_gate': ffn_w_gate, 'ffn_w_up': ffn_w_up,
         'ffn_w_down': ffn_w_down, 'w_in1': w_in1, 'lru_conv_w': lru_conv_w, 'lru_conv_b': lru_conv_b,
         'lru_wa': lru_wa, 'lru_ba': lru_ba, 'lru_wx': lru_wx, 'lru_bx': lru_bx, 'lru_lambda': lru_lambda,
         'w_out1': w_out1, 'moe_router': moe_router, 'moe_w1': moe_w1, 'moe_w3': moe_w3, 'moe_w2': moe_w2}
    bsz = x_prompt.shape[0]
    st_p = {'gdn_conv': jnp.zeros((bsz, GDN_CONV - 1, GDN_CONV_CH), x_prompt.dtype),
            'gdn': jnp.zeros((bsz, GDN_HEADS, GDN_DK, GDN_DV), jnp.float32),
            'lru_conv': jnp.zeros((bsz, RNN_CONV - 1, RNN_WIDTH), x_prompt.dtype),
            'lru': jnp.zeros((bsz, RNN_WIDTH), jnp.float32)}
    st_s = {'gdn_conv': state_gdn_conv, 'gdn': state_gdn, 'lru_conv': state_lru_conv, 'lru': state_lru}

    def prompt_nsa(q, g, kv):
        keep = min(NSA_WINDOW, kv.shape[1])
        return nsa_prompt(q, g, kv, rel_bias), kv[:, kv.shape[1] - keep:, 4:]

    def sample_nsa(q, g, kv):
        return nsa_sample(q, g, kv, cache_nsa_kv, cache_nsa_win, page_table, rel_bias)

    y_prompt, pn = trunk(x_prompt, c_prompt, prompt_nsa, st_p, p)
    y_sample, sn = trunk(x_sample, c_sample, sample_nsa, st_s, p)
    return (y_prompt, y_sample,
            pn['nsa_kv'], pn['nsa_win'], pn['gdn'], pn['gdn_conv'], pn['lru'], pn['lru_conv'],
            sn['nsa_kv'], sn['nsa_win'], sn['gdn'], sn['gdn_conv'], sn['lru'], sn['lru_conv'])
```

```python
import functools
import math

import numpy as np
import jax
import jax.numpy as jnp
from jax import lax
from jax.experimental import pallas as pl
from jax.experimental.pallas import tpu as pltpu

f32 = jnp.float32
bf16 = jnp.bfloat16
i32 = jnp.int32

D_MODEL = 1024
EPS = 1e-6
N_MOD = 6
GDN_HEADS = 4
GDN_DK = 128
GDN_DV = 128
GDN_CONV = 4
GDN_CHUNK = 64
GDN_CONV_CH = GDN_HEADS * (2 * GDN_DK + GDN_DV)
NSA_HEADS = 8
NSA_KV_GROUPS = 2
NSA_HPG = NSA_HEADS // NSA_KV_GROUPS
NSA_DH = 64
NSA_BLOCK = 64
NSA_TOPN = 16
NSA_LOCAL = 2
NSA_WINDOW = 512
NSA_QBLOCK = 128
NSA_FORCE = 1.0e4
REL_BUCKETS = 32
REL_MAX_DIST = 2048
RNN_WIDTH = D_MODEL
RNN_BLOCKS = 8
RNN_BW = RNN_WIDTH // RNN_BLOCKS
RNN_CONV = 4
RG_C = 8.0
FFN_DIM = 2816
N_EXPERTS = 8
TOP_K = 2
EXPERT_DIM = 3584

NEG = -1e30
PAIR = 2 * NSA_BLOCK
NEAR_PAIRS = 13
MISC_W = 128
VMEM_LIMIT = 56 * 1024 * 1024


def _cparams(sem, vmem=VMEM_LIMIT):
    return pltpu.CompilerParams(dimension_semantics=sem, vmem_limit_bytes=vmem)


def _mm(a, b):
    return jnp.dot(a.astype(bf16), b.astype(bf16), preferred_element_type=f32)


def _mm_nt(a, b):
    return lax.dot_general(a.astype(bf16), b.astype(bf16), (((1,), (1,)), ((), ())),
                           preferred_element_type=f32)


def _split2(a):
    hi = a.astype(bf16)
    lo = (a - hi.astype(f32)).astype(bf16)
    return hi, lo


def _mm3(a, b):
    ah, al = _split2(a)
    bh, bl = _split2(b)
    return (jnp.dot(ah, bh, preferred_element_type=f32) + jnp.dot(ah, bl, preferred_element_type=f32)
            + jnp.dot(al, bh, preferred_element_type=f32))


def _mm3_nt(a, b):
    ah, al = _split2(a)
    bh, bl = _split2(b)
    dn = (((1,), (1,)), ((), ()))
    return (lax.dot_general(ah, bh, dn, preferred_element_type=f32)
            + lax.dot_general(ah, bl, dn, preferred_element_type=f32)
            + lax.dot_general(al, bh, dn, preferred_element_type=f32))


def _mm_01(m01, a):
    hi = a.astype(bf16)
    r1 = a - hi.astype(f32)
    mid = r1.astype(bf16)
    lo = (r1 - mid.astype(f32)).astype(bf16)
    return (jnp.dot(m01, hi, preferred_element_type=f32) + jnp.dot(m01, mid, preferred_element_type=f32)
            + jnp.dot(m01, lo, preferred_element_type=f32))


def _dotp(a, b, prec):
    return _mm3(a, b) if prec == 3 else _mm(a, b)


def _silu(x):
    return x * jax.nn.sigmoid(x)


def _softplus(x):
    return jnp.maximum(x, 0.0) + jnp.log1p(jnp.exp(-jnp.abs(x)))


def _modulate(x, gain, shift, scale):
    r = lax.rsqrt(jnp.mean(x * x, axis=-1, keepdims=True) + EPS)
    return x * r * gain * (1.0 + scale) + shift


def _ada_kernel(c_ref, w_ref, b_ref, o_ref):
    o_ref[0] = _mm3(_silu(c_ref[...]), w_ref[0]) + b_ref[0]


def _adaln(c_all, w_ada, b_ada):
    rows = c_all.shape[0]
    depth, d, n = w_ada.shape
    tn = 1536
    return pl.pallas_call(
        _ada_kernel,
        grid=(depth, n // tn),
        in_specs=[pl.BlockSpec((rows, d), lambda l, j: (0, 0)),
                  pl.BlockSpec((1, d, tn), lambda l, j: (l, 0, j)),
                  pl.BlockSpec((1, 1, tn), lambda l, j: (l, 0, j))],
        out_specs=pl.BlockSpec((1, rows, tn), lambda l, j: (l, 0, j)),
        out_shape=jax.ShapeDtypeStruct((depth, rows, n), f32),
        compiler_params=_cparams(("arbitrary", "arbitrary")),
        name="adaln",
    )(c_all, w_ada, b_ada.reshape(depth, 1, n))


def _mod_spec(mod, tm):
    r = mod.shape[1]
    if r == 1:
        return pl.BlockSpec((1, 1, mod.shape[2]), lambda b, i: (b, 0, 0))
    return pl.BlockSpec((1, tm, mod.shape[2]), lambda b, i: (b, i, 0))


def _modmm_kernel(x_ref, gain_ref, shift_ref, scale_ref, w_ref, *o_refs, splits, prec, emit_h):
    h = _modulate(x_ref[0], gain_ref[...], shift_ref[0], scale_ref[0])
    if emit_h:
        o_refs[-1][0] = h.astype(o_refs[-1].dtype)
    hh = _split2(h) if prec == 3 else h.astype(bf16)
    for o_ref, (a, b) in zip(o_refs, splits):
        w = w_ref[:, a:b]
        if prec == 3:
            wh, wl = _split2(w)
            acc = (jnp.dot(hh[0], wh, preferred_element_type=f32) + jnp.dot(hh[0], wl, preferred_element_type=f32)
                   + jnp.dot(hh[1], wh, preferred_element_type=f32))
        else:
            acc = jnp.dot(hh, w, preferred_element_type=f32)
        o_ref[0] = acc


def _mod_matmul(x, gain, shift, scale, w, splits, *, tm, prec, emit_h=None):
    bsz, t, d = x.shape
    out_shape = [jax.ShapeDtypeStruct((bsz, t, b - a), f32) for a, b in splits]
    out_specs = [pl.BlockSpec((1, tm, b - a), lambda bi, i: (bi, i, 0)) for a, b in splits]
    if emit_h is not None:
        out_shape.append(jax.ShapeDtypeStruct((bsz, t, d), emit_h))
        out_specs.append(pl.BlockSpec((1, tm, d), lambda bi, i: (bi, i, 0)))
    return pl.pallas_call(
        functools.partial(_modmm_kernel, splits=tuple(splits), prec=prec, emit_h=emit_h is not None),
        grid=(bsz, t // tm),
        in_specs=[pl.BlockSpec((1, tm, d), lambda bi, i: (bi, i, 0)),
                  pl.BlockSpec((1, d), lambda bi, i: (0, 0)),
                  _mod_spec(shift, tm), _mod_spec(scale, tm),
                  pl.BlockSpec(w.shape, lambda bi, i: (0, 0))],
        out_specs=out_specs,
        out_shape=out_shape,
        compiler_params=_cparams(("parallel", "parallel")),
        name="mod_matmul",
    )(x, gain.reshape(1, d), shift, scale, w)


def _projres_kernel(*refs, n_lhs, ksplits, prec, final_norm):
    lhs = refs[:n_lhs]
    w_ref, x_ref, gate_ref = refs[n_lhs:n_lhs + 3]
    o_ref = refs[-1]
    acc = None
    for l_ref, (a, b) in zip(lhs, ksplits):
        part = _dotp(l_ref[0], w_ref[a:b, :], prec)
        acc = part if acc is None else acc + part
    y = x_ref[0] + gate_ref[0] * acc
    if final_norm:
        nw_ref = refs[n_lhs + 3]
        y = y * lax.rsqrt(jnp.mean(y * y, axis=-1, keepdims=True) + EPS) * nw_ref[...]
    o_ref[0] = y


def _proj_residual(lhs_list, w, x, gate, *, tm, prec, norm_w=None):
    bsz, t, d = x.shape
    ksplits, k0 = [], 0
    for l in lhs_list:
        ksplits.append((k0, k0 + l.shape[-1]))
        k0 += l.shape[-1]
    in_specs = [pl.BlockSpec((1, tm, l.shape[-1]), lambda bi, i: (bi, i, 0)) for l in lhs_list]
    in_specs += [pl.BlockSpec(w.shape, lambda bi, i: (0, 0)),
                 pl.BlockSpec((1, tm, d), lambda bi, i: (bi, i, 0)),
                 _mod_spec(gate, tm)]
    args = list(lhs_list) + [w, x, gate]
    if norm_w is not None:
        in_specs.append(pl.BlockSpec((1, d), lambda bi, i: (0, 0)))
        args.append(norm_w.reshape(1, d))
    return pl.pallas_call(
        functools.partial(_projres_kernel, n_lhs=len(lhs_list), ksplits=tuple(ksplits), prec=prec,
                          final_norm=norm_w is not None),
        grid=(bsz, t // tm),
        in_specs=in_specs,
        out_specs=pl.BlockSpec((1, tm, d), lambda bi, i: (bi, i, 0)),
        out_shape=jax.ShapeDtypeStruct((bsz, t, d), f32),
        compiler_params=_cparams(("parallel", "parallel")),
        name="proj_residual",
    )(*args)


def _ffn_kernel(x_ref, gain_ref, shift_ref, scale_ref, gate_ref, wg_ref, wu_ref, wd_ref, o_ref,
                h_scr, acc_scr, *, prec):
    f = pl.program_id(2)

    @pl.when(f == 0)
    def _():
        h_scr[...] = _modulate(x_ref[0], gain_ref[...], shift_ref[0], scale_ref[0]).astype(h_scr.dtype)
        acc_scr[...] = jnp.zeros_like(acc_scr)

    h = h_scr[...]
    act = _silu(_dotp(h, wg_ref[...], prec)) * _dotp(h, wu_ref[...], prec)
    acc_scr[...] += _dotp(act, wd_ref[...], prec)

    @pl.when(f == pl.num_programs(2) - 1)
    def _():
        o_ref[0] = x_ref[0] + gate_ref[0] * acc_scr[...]


def _ffn(x, gain, shift, scale, gate, wg, wu, wd, *, tm, tf, prec):
    bsz, t, d = x.shape
    fdim = wg.shape[1]
    mod_specs = []
    for mod in (shift, scale, gate):
        if mod.shape[1] == 1:
            mod_specs.append(pl.BlockSpec((1, 1, d), lambda b, i, f: (b, 0, 0)))
        else:
            mod_specs.append(pl.BlockSpec((1, tm, d), lambda b, i, f: (b, i, 0)))
    return pl.pallas_call(
        functools.partial(_ffn_kernel, prec=prec),
        grid=(bsz, t // tm, fdim // tf),
        in_specs=[pl.BlockSpec((1, tm, d), lambda b, i, f: (b, i, 0)),
                  pl.BlockSpec((1, d), lambda b, i, f: (0, 0)),
                  mod_specs[0], mod_specs[1], mod_specs[2],
                  pl.BlockSpec((d, tf), lambda b, i, f: (0, f)),
                  pl.BlockSpec((d, tf), lambda b, i, f: (0, f)),
                  pl.BlockSpec((tf, d), lambda b, i, f: (f, 0))],
        out_specs=pl.BlockSpec((1, tm, d), lambda b, i, f: (b, i, 0)),
        out_shape=jax.ShapeDtypeStruct((bsz, t, d), f32),
        scratch_shapes=[pltpu.VMEM((tm, d), f32 if prec == 3 else bf16), pltpu.VMEM((tm, d), f32)],
        compiler_params=_cparams(("parallel", "parallel", "arbitrary")),
        name="ffn",
    )(x, gain.reshape(1, d), shift, scale, gate, wg, wu, wd)


def _gdn_kernel(qkv_ref, z_ref, misc_ref, cw_ref, cinit_ref, hp_ref, nw_ref, s0_ref,
                o_ref, sfin_ref, xc_scr, s_scr, *, tb, n_valid):
    j = pl.program_id(1)
    c = GDN_CHUNK
    nh = GDN_HEADS

    @pl.when(j == 0)
    def _():
        xc_scr[0:8, :] = cinit_ref[0]
        s_scr[...] = s0_ref[0]

    x = qkv_ref[0]
    xc_scr[8:8 + tb, :] = x
    y = xc_scr[5:5 + tb, :] * cw_ref[0:1, :]
    y = y + xc_scr[6:6 + tb, :] * cw_ref[1:2, :]
    y = y + xc_scr[7:7 + tb, :] * cw_ref[2:3, :]
    y = y + x * cw_ref[3:4, :]
    xc_scr[0:8, :] = xc_scr[tb:tb + 8, :]
    y = _silu(y)

    misc = misc_ref[0]
    row = lax.broadcasted_iota(i32, (tb, MISC_W), 0) + j * tb
    live = row < n_valid
    log_a = jnp.where(live, hp_ref[0:1, :] * _softplus(misc + hp_ref[1:2, :]), 0.0)
    beta = jnp.where(live, jax.nn.sigmoid(misc), 0.0)

    r_i = lax.broadcasted_iota(i32, (tb, tb), 0)
    c_i = lax.broadcasted_iota(i32, (tb, tb), 1)
    sh = int(math.log2(c))
    ltri = (((r_i >> sh) == (c_i >> sh)) & (c_i <= r_i)).astype(bf16)
    g = _mm_01(ltri, log_a)
    g_t = g.T

    ri = lax.broadcasted_iota(i32, (c, c), 0)
    ci = lax.broadcasted_iota(i32, (c, c), 1)
    incl = ci <= ri
    strict = ci < ri
    eye = (ci == ri).astype(f32)

    for n in range(tb // c):
        r0 = n * c
        for h in range(nh):
            q = y[r0:r0 + c, h * GDN_DK:(h + 1) * GDN_DK]
            k = y[r0:r0 + c, nh * GDN_DK + h * GDN_DK:nh * GDN_DK + (h + 1) * GDN_DK]
            v = y[r0:r0 + c, 2 * nh * GDN_DK + h * GDN_DV:2 * nh * GDN_DK + (h + 1) * GDN_DV]
            q = q * lax.rsqrt(jnp.sum(q * q, axis=-1, keepdims=True) + EPS) * (GDN_DK ** -0.5)
            k = k * lax.rsqrt(jnp.sum(k * k, axis=-1, keepdims=True) + EPS)
            g_col = g[r0:r0 + c, h:h + 1]
            g_row = g_t[h:h + 1, r0:r0 + c]
            b_col = beta[r0:r0 + c, nh + h:nh + h + 1]
            gam = jnp.where(incl, jnp.exp(jnp.where(incl, g_col - g_row, 0.0)), 0.0)
            kk = _mm_nt(k, k)
            a = jnp.where(strict, b_col * gam * kk, 0.0)
            p = eye - a
            a_pow = a
            for _ in range(int(math.log2(c)) - 1):
                a_pow = _mm(a_pow, a_pow)
                p = p + _mm(p, a_pow)
            e_g = jnp.exp(g_col)
            rhs = jnp.concatenate([b_col * v, (b_col * e_g) * k], axis=-1)
            sol = _mm(p, rhs)
            vb, w = sol[:, :GDN_DV], sol[:, GDN_DV:]
            aqk = _mm_nt(q, k) * gam
            qg = q * e_g
            g_last = g_col[c - 1:c, :]
            kd = k * jnp.exp(g_last - g_col)
            gc = jnp.exp(g_last)
            s = s_scr[h]
            u = vb - _mm(w, s)
            o = _mm(qg, s) + _mm(aqk, u)
            s_scr[h] = gc * s + _mm(kd.T, u)
            o = o * lax.rsqrt(jnp.mean(o * o, axis=-1, keepdims=True) + EPS) * nw_ref[...]
            o = o * _silu(z_ref[0, r0:r0 + c, h * GDN_DV:(h + 1) * GDN_DV])
            o_ref[0, r0:r0 + c, h * GDN_DV:(h + 1) * GDN_DV] = o

    @pl.when(j == pl.num_programs(1) - 1)
    def _():
        sfin_ref[0] = s_scr[...]


def _gdn(qkv_raw, z, misc, conv_w, conv_buf, a_log, dt_bias, norm_w, s0, *, tb, n_valid):
    bsz, tp, ch = qkv_raw.shape
    cinit = jnp.concatenate([jnp.zeros((bsz, 5, ch), f32), conv_buf.astype(f32)], axis=1)
    hp = jnp.zeros((8, MISC_W), f32)
    hp = hp.at[0, :GDN_HEADS].set(-jnp.exp(a_log.astype(f32))).at[1, :GDN_HEADS].set(dt_bias.astype(f32))
    zw = GDN_HEADS * GDN_DV
    return pl.pallas_call(
        functools.partial(_gdn_kernel, tb=tb, n_valid=n_valid),
        grid=(bsz, tp // tb),
        in_specs=[pl.BlockSpec((1, tb, ch), lambda b, j: (b, j, 0)),
                  pl.BlockSpec((1, tb, zw), lambda b, j: (b, j, 0)),
                  pl.BlockSpec((1, tb, MISC_W), lambda b, j: (b, j, 0)),
                  pl.BlockSpec((GDN_CONV, ch), lambda b, j: (0, 0)),
                  pl.BlockSpec((1, 8, ch), lambda b, j: (b, 0, 0)),
                  pl.BlockSpec((8, MISC_W), lambda b, j: (0, 0)),
                  pl.BlockSpec((1, GDN_DV), lambda b, j: (0, 0)),
                  pl.BlockSpec((1, GDN_HEADS, GDN_DK, GDN_DV), lambda b, j: (b, 0, 0, 0))],
        out_specs=[pl.BlockSpec((1, tb, zw), lambda b, j: (b, j, 0)),
                   pl.BlockSpec((1, GDN_HEADS, GDN_DK, GDN_DV), lambda b, j: (b, 0, 0, 0))],
        out_shape=[jax.ShapeDtypeStruct((bsz, tp, zw), f32),
                   jax.ShapeDtypeStruct((bsz, GDN_HEADS, GDN_DK, GDN_DV), f32)],
        scratch_shapes=[pltpu.VMEM((tb + 8, ch), f32), pltpu.VMEM((GDN_HEADS, GDN_DK, GDN_DV), f32)],
        compiler_params=_cparams(("parallel", "arbitrary")),
        name="gdn",
    )(qkv_raw, z, misc, conv_w.astype(f32), cinit, hp, norm_w.reshape(1, GDN_DV).astype(f32), s0.astype(f32))


def _bucket_thresholds():
    exact = REL_BUCKETS // 2
    d = np.arange(0, 4 * REL_MAX_DIST, dtype=np.int64)

    def buckets(ft):
        nf = np.maximum(d, exact).astype(ft)
        large = exact + (np.log(nf / ft(exact)) / ft(math.log(REL_MAX_DIST / exact)) * ft(REL_BUCKETS - exact)).astype(np.int32)
        return np.where(d < exact, d, np.minimum(large, REL_BUCKETS - 1))

    b64, b32 = buckets(np.float64), buckets(np.float32)
    assert np.array_equal(b64, b32) and b64[-1] == REL_BUCKETS - 1 and np.all(np.diff(b64) >= 0)
    return [int(np.argmax(b64 >= k)) for k in range(1, REL_BUCKETS)]


_BUCKET_THR = _bucket_thresholds()
FAR_DIST = _BUCKET_THR[-1]


def _bias_kernel(rb_ref, d_ref, o_ref, *, slab, group_heads):
    rows = d_ref.shape[-2]
    for s in range(rows // slab):
        d = d_ref[0, 0, s * slab:(s + 1) * slab, :]
        h = pl.program_id(0) * group_heads + s
        val = jnp.full(d.shape, rb_ref[0, h], f32)
        for k in range(1, REL_BUCKETS):
            val = jnp.where(d >= _BUCKET_THR[k - 1], rb_ref[k, h], val)
        o_ref[0, 0, s * slab:(s + 1) * slab, :] = jnp.where(d < 0, NEG, val)


def _bias_table(rel_bias, dist, *, slab):
    g, nt, rows, cols = dist.shape
    return pl.pallas_call(
        functools.partial(_bias_kernel, slab=slab, group_heads=rows // slab),
        grid=(g, nt),
        in_specs=[pl.BlockSpec(memory_space=pltpu.SMEM),
                  pl.BlockSpec((1, 1, rows, cols), lambda a, b: (a, b, 0, 0))],
        out_specs=pl.BlockSpec((1, 1, rows, cols), lambda a, b: (a, b, 0, 0)),
        out_shape=jax.ShapeDtypeStruct(dist.shape, f32),
        compiler_params=_cparams(("arbitrary", "arbitrary")),
        name="bias_table",
    )(rel_bias.astype(f32), dist)


def _blockmean_kernel(x_ref, o_ref):
    x = x_ref[0]
    nb = x.shape[0] // NSA_BLOCK
    o_ref[0] = jnp.sum(x.reshape(nb, NSA_BLOCK, x.shape[1]), axis=1) * (1.0 / NSA_BLOCK)


def _block_means(kv4, *, tb):
    bsz, t, _ = kv4.shape
    w = 2 * NSA_KV_GROUPS * NSA_DH
    return pl.pallas_call(
        _blockmean_kernel,
        grid=(bsz, t // tb),
        in_specs=[pl.BlockSpec((1, tb, w), lambda b, i: (b, i, 0))],
        out_specs=pl.BlockSpec((1, tb // NSA_BLOCK, w), lambda b, i: (b, i, 0)),
        out_shape=jax.ShapeDtypeStruct((bsz, t // NSA_BLOCK, w), f32),
        compiler_params=_cparams(("parallel", "parallel")),
        name="nsa_block_means",
    )(kv4)


def _topn_mask(imp, n_sel):
    lane = lax.broadcasted_iota(i32, imp.shape, 1)
    sel = jnp.zeros(imp.shape, jnp.bool_)
    for _ in range(n_sel):
        mx = jnp.max(imp, axis=-1, keepdims=True)
        first = jnp.min(jnp.where(imp == mx, lane, imp.shape[1]), axis=-1, keepdims=True)
        hit = lane == first
        sel = sel | hit
        imp = jnp.where(hit, -jnp.inf, imp)
    return sel


def _nsa_prompt_kernel(q_ref, misc_ref, kcvc_ref, kaug_ref, v2_ref, kw_ref, vw_ref,
                       tabs_ref, tabw_ref, tabc_ref, c31_ref, o_ref):
    g = pl.program_id(1)
    i = pl.program_id(2)
    qb, dh, hpg = NSA_QBLOCK, NSA_DH, NSA_HPG
    rows = hpg * qb
    is_g0 = g == 0

    def ghalf(a):
        return jnp.where(is_g0, a[:, :dh], a[:, dh:])

    qt = q_ref[0] * (dh ** -0.5)
    qs = jnp.concatenate([qt[:, h * dh:(h + 1) * dh] for h in range(hpg)], axis=0)
    zero = jnp.zeros_like(qs)
    q2 = jnp.concatenate([jnp.where(is_g0, qs, zero), jnp.where(is_g0, zero, qs)], axis=1)

    kcvc = kcvc_ref[0]
    nblk = kcvc.shape[0]
    kc = ghalf(kcvc[:, :2 * dh])
    vc = ghalf(kcvc[:, 2 * dh:])
    s_c = _mm3_nt(qs, kc)
    lane = lax.broadcasted_iota(i32, (rows, nblk), 1)
    qrow = lax.broadcasted_iota(i32, (rows, nblk), 0) & (qb - 1)
    shift = (2 * i - 2 * NEAR_PAIRS + 1 + 4 * nblk) % nblk
    bias_c = pltpu.roll(tabc_ref[0], shift, 1)
    bias_c = jnp.where(lane < 2 * i - 2 * NEAR_PAIRS + 1, c31_ref[0], bias_c)
    readable = lane * NSA_BLOCK + (NSA_BLOCK - 1) <= i * qb + qrow
    s_c = jnp.where(readable, s_c + bias_c, NEG)
    m_c = jnp.max(s_c, axis=-1, keepdims=True)
    p_c = jnp.where(readable, jnp.exp(s_c - m_c), 0.0)
    p_c = p_c / jnp.maximum(jnp.sum(p_c, axis=-1, keepdims=True), 1e-30)
    o_c = _mm(p_c, vc)

    imp = p_c[0:qb]
    for h in range(1, hpg):
        imp = imp + p_c[h * qb:(h + 1) * qb]
    blk = lax.broadcasted_iota(i32, (qb, nblk), 1)
    cur = (i * qb + lax.broadcasted_iota(i32, (qb, nblk), 0)) >> int(math.log2(NSA_BLOCK))
    valid = blk <= cur
    forced = valid & ((blk == 0) | (blk > cur - NSA_LOCAL))
    imp = jnp.where(forced, NSA_FORCE, jnp.where(valid, imp, -1.0))
    sel = _topn_mask(imp, min(NSA_TOPN, nblk)) & valid
    msel = jnp.where(sel, 0.0, NEG).astype(bf16)
    qaug = jnp.concatenate([q2.astype(bf16), jnp.concatenate([msel] * hpg, axis=0)], axis=1)

    t_d = i // 2
    tk = 2 * PAIR

    def tile(k, carry):
        m, l, acc = carry
        t = t_d - k
        ks = pl.multiple_of(t * tk, tk)
        s = lax.dot_general(qaug, kaug_ref[0, pl.ds(ks, tk), :], (((1,), (1,)), ((), ())),
                            preferred_element_type=f32)
        i0 = jnp.clip(i - 2 * t, 0, NEAR_PAIRS)
        i1 = jnp.clip(i - 2 * t - 1, 0, NEAR_PAIRS)
        s = s + jnp.concatenate([tabs_ref[0, i0], tabs_ref[0, i1]], axis=1)
        m_new = jnp.maximum(m, jnp.max(s, axis=-1, keepdims=True))
        alpha = jnp.exp(m - m_new)
        p = jnp.exp(s - m_new)
        l = alpha * l + jnp.sum(p, axis=-1, keepdims=True)
        acc = alpha * acc + jnp.dot(p.astype(bf16), v2_ref[0, pl.ds(ks, tk), :], preferred_element_type=f32)
        return m_new, l, acc

    m0 = jnp.full((rows, 1), NEG, f32)
    l0 = jnp.zeros((rows, 1), f32)
    a0 = jnp.zeros((rows, 2 * dh), f32)
    _, l_s, acc_s = lax.fori_loop(0, t_d + 1, tile, (m0, l0, a0))
    o_s = ghalf(acc_s) / jnp.maximum(l_s, 1e-30)

    span = NSA_WINDOW + qb
    ws = pl.multiple_of(i * qb, qb)
    s_w = lax.dot_general(q2.astype(bf16), kw_ref[0, pl.ds(ws, span), :], (((1,), (1,)), ((), ())),
                          preferred_element_type=f32) + tabw_ref[0]
    kpos = lax.broadcasted_iota(i32, (rows, span), 1) + (i * qb - NSA_WINDOW)
    s_w = jnp.where(kpos >= 0, s_w, NEG)
    m_w = jnp.max(s_w, axis=-1, keepdims=True)
    p_w = jnp.exp(s_w - m_w)
    l_w = jnp.sum(p_w, axis=-1, keepdims=True)
    o_w = ghalf(jnp.dot(p_w.astype(bf16), vw_ref[0, pl.ds(ws, span), :], preferred_element_type=f32))
    o_w = o_w / jnp.maximum(l_w, 1e-30)

    gates = jax.nn.sigmoid(misc_ref[0])
    outs = []
    for h in range(hpg):
        r = slice(h * qb, (h + 1) * qb)
        acc = None
        for jb, branch in enumerate((o_c, o_s, o_w)):
            l0_ = 8 + 3 * h + jb
            l1_ = 8 + 3 * (hpg + h) + jb
            gcol = jnp.where(is_g0, gates[:, l0_:l0_ + 1], gates[:, l1_:l1_ + 1])
            term = gcol * branch[r]
            acc = term if acc is None else acc + term
        outs.append(acc)
    o_ref[0] = jnp.concatenate(outs, axis=1)


def _nsa_tables(rel_bias):
    qb, hpg = NSA_QBLOCK, NSA_HPG
    rows = hpg * qb
    q = (np.arange(rows) % qb)[:, None]
    c = np.arange(PAIR)[None, :]
    d_sel = [PAIR * idx + q - c for idx in range(NEAR_PAIRS + 1)]
    assert PAIR * NEAR_PAIRS - (PAIR - 1) >= FAR_DIST
    mm = 2 * NEAR_PAIRS - 1 - c
    d_cmp = np.maximum(np.where(mm >= -1, NSA_BLOCK * mm + q - (NSA_BLOCK - 1), 0), 0)
    assert NSA_BLOCK * (2 * NEAR_PAIRS) - (NSA_BLOCK - 1) >= FAR_DIST
    cw = np.arange(NSA_WINDOW + qb)[None, :]
    d_win = q + NSA_WINDOW - cw
    d_win = np.where((d_win >= 0) & (d_win < NSA_WINDOW), d_win, -1)
    n_win = (NSA_WINDOW + qb) // PAIR
    tiles = d_sel + [d_cmp] + [d_win[:, k * PAIR:(k + 1) * PAIR] for k in range(n_win)]
    dist = np.broadcast_to(np.stack(tiles)[None], (NSA_KV_GROUPS, len(tiles), rows, PAIR))
    tab = _bias_table(rel_bias, jnp.asarray(dist, i32), slab=qb)
    ns = NEAR_PAIRS + 1
    tabs = tab[:, :ns]
    tabc = tab[:, ns]
    tabw = jnp.concatenate([tab[:, ns + 1 + k] for k in range(n_win)], axis=-1)
    c31 = jnp.repeat(rel_bias.astype(f32)[REL_BUCKETS - 1].reshape(NSA_KV_GROUPS, hpg), qb, axis=1)
    return tabs, tabw, tabc, c31.reshape(NSA_KV_GROUPS, rows, 1)


def _nsa_prompt(q_b, misc, kv4, kvwin, rel_bias):
    bsz, t, _ = q_b.shape
    dh, qb = NSA_DH, NSA_QBLOCK
    nblk = t // NSA_BLOCK
    assert nblk == PAIR and t % (2 * PAIR) == 0
    kcvc = _block_means(kv4, tb=512)
    onehot = (jnp.arange(t, dtype=i32)[:, None] // NSA_BLOCK == jnp.arange(nblk, dtype=i32)[None, :]).astype(bf16)
    kaug = jnp.concatenate([kv4[:, :, 4 * dh:6 * dh].astype(bf16),
                            jnp.broadcast_to(onehot[None], (bsz, t, nblk))], axis=-1)
    v2 = kv4[:, :, 6 * dh:8 * dh].astype(bf16)
    pad = ((0, 0), (NSA_WINDOW, 0), (0, 0))
    kw = jnp.pad(kvwin[:, :, :2 * dh].astype(bf16), pad)
    vw = jnp.pad(kvwin[:, :, 2 * dh:].astype(bf16), pad)
    tabs, tabw, tabc, c31 = _nsa_tables(rel_bias)
    rows = NSA_HPG * qb
    gw = NSA_HPG * dh
    span = NSA_WINDOW + qb
    return pl.pallas_call(
        _nsa_prompt_kernel,
        grid=(bsz, NSA_KV_GROUPS, t // qb),
        in_specs=[pl.BlockSpec((1, qb, gw), lambda b, g, i: (b, i, g)),
                  pl.BlockSpec((1, qb, MISC_W), lambda b, g, i: (b, i, 0)),
                  pl.BlockSpec((1, nblk, 4 * dh), lambda b, g, i: (b, 0, 0)),
                  pl.BlockSpec((1, t, 2 * dh + nblk), lambda b, g, i: (b, 0, 0)),
                  pl.BlockSpec((1, t, 2 * dh), lambda b, g, i: (b, 0, 0)),
                  pl.BlockSpec((1, t + NSA_WINDOW, 2 * dh), lambda b, g, i: (b, 0, 0)),
                  pl.BlockSpec((1, t + NSA_WINDOW, 2 * dh), lambda b, g, i: (b, 0, 0)),
                  pl.BlockSpec((1, NEAR_PAIRS + 1, rows, PAIR), lambda b, g, i: (g, 0, 0, 0)),
                  pl.BlockSpec((1, rows, span), lambda b, g, i: (g, 0, 0)),
                  pl.BlockSpec((1, rows, PAIR), lambda b, g, i: (g, 0, 0)),
                  pl.BlockSpec((1, rows, 1), lambda b, g, i: (g, 0, 0))],
        out_specs=pl.BlockSpec((1, qb, gw), lambda b, g, i: (b, i, g)),
        out_shape=jax.ShapeDtypeStruct((bsz, t, NSA_HEADS * dh), f32),
        compiler_params=_cparams(("parallel", "parallel", "arbitrary")),
        name="nsa_prompt",
    )(q_b, misc, kcvc, kaug, v2, kw, vw, tabs, tabw, tabc, c31)


MEANS_PAGES = 8


def _nsa_s_means_kernel(pt_ref, *refs):
    del pt_ref
    x_refs, o_ref = refs[:-1], refs[-1]
    parts = []
    for x_ref in x_refs:
        x = x_ref[0]
        nb = x.shape[0] // NSA_BLOCK
        parts.append(jnp.sum(x.reshape(nb, NSA_BLOCK, x.shape[1]), axis=1) * (1.0 / NSA_BLOCK))
    o_ref[0] = jnp.concatenate(parts, axis=0)


def _nsa_s_means(cache2d, page_table):
    db, n_pages = page_table.shape
    page = cache2d.shape[1]
    w = 2 * NSA_KV_GROUPS * NSA_DH
    nblk = n_pages * page // NSA_BLOCK
    kp = MEANS_PAGES
    rows = kp * page // NSA_BLOCK
    assert n_pages % kp == 0 and rows % 8 == 0

    def page_spec(k):
        return pl.BlockSpec((1, page, w), lambda b, p, pt: (pt[b * n_pages + p * kp + k], 0, 0))

    return pl.pallas_call(
        _nsa_s_means_kernel,
        grid_spec=pltpu.PrefetchScalarGridSpec(
            num_scalar_prefetch=1,
            grid=(db, n_pages // kp),
            in_specs=[page_spec(k) for k in range(kp)],
            out_specs=pl.BlockSpec((1, rows, w), lambda b, p, pt: (b, p, 0))),
        out_shape=jax.ShapeDtypeStruct((db, nblk, w), f32),
        compiler_params=_cparams(("parallel", "parallel")),
        name="nsa_sample_means",
    )(page_table.reshape(-1), *([cache2d] * kp))


def _nsa_s_scores_kernel(q_ref, kcvc_ref, win_ref, kvn_ref, tc_ref, tw_ref, b0_ref, oc_ref, ow_ref, sel_ref):
    dh, hpg = NSA_DH, NSA_HPG
    q = q_ref[0] * (dh ** -0.5)
    kcvc = kcvc_ref[0]
    nblk = kcvc.shape[0]
    win = win_ref[0]
    kvn = kvn_ref[0]
    row = lax.broadcasted_iota(i32, (NSA_HEADS, 1), 0)
    lane = lax.broadcasted_iota(i32, (1, nblk), 1)
    n_sel = NSA_TOPN - 1
    for g in range(NSA_KV_GROUPS):
        in_g = (row >= g * hpg) & (row < (g + 1) * hpg)
        kc = kcvc[:, g * dh:(g + 1) * dh]
        vc = kcvc[:, (NSA_KV_GROUPS + g) * dh:(NSA_KV_GROUPS + g + 1) * dh]
        s_c = _mm3_nt(q, kc) + tc_ref[...]
        m_c = jnp.max(s_c, axis=-1, keepdims=True)
        p_c = jnp.exp(s_c - m_c)
        p_c = p_c / jnp.maximum(jnp.sum(p_c, axis=-1, keepdims=True), 1e-30)
        o_c = _mm(p_c, vc)
        imp = jnp.sum(jnp.where(in_g, p_c, 0.0), axis=0, keepdims=True)
        imp = jnp.where((lane == 0) | (lane > nblk - NSA_LOCAL), NSA_FORCE, imp)
        picks = jnp.zeros((1, nblk), i32)
        for it in range(n_sel):
            mx = jnp.max(imp, axis=-1, keepdims=True)
            first = jnp.min(jnp.where(imp == mx, lane, nblk), axis=-1, keepdims=True)
            picks = jnp.where(lane == it, first, picks)
            imp = jnp.where(lane == first, -jnp.inf, imp)
        sel_ref[0, g:g + 1, :] = picks
        kw = win[:, g * dh:(g + 1) * dh]
        vw = win[:, (NSA_KV_GROUPS + g) * dh:(NSA_KV_GROUPS + g + 1) * dh]
        s_w = _mm3_nt(q, kw) + tw_ref[...]
        s_n = jnp.sum(q * kvn[:, g * dh:(g + 1) * dh], axis=-1, keepdims=True) + b0_ref[...]
        m_w = jnp.maximum(jnp.max(s_w, axis=-1, keepdims=True), s_n)
        p_w = jnp.exp(s_w - m_w)
        p_n = jnp.exp(s_n - m_w)
        l_w = jnp.sum(p_w, axis=-1, keepdims=True) + p_n
        o_w = (_mm(p_w, vw) + p_n * kvn[:, (NSA_KV_GROUPS + g) * dh:(NSA_KV_GROUPS + g + 1) * dh]) / jnp.maximum(l_w, 1e-30)
        if g == 0:
            oc_ref[0] = o_c
            ow_ref[0] = o_w
        else:
            oc_ref[0] = jnp.where(in_g, o_c, oc_ref[0])
            ow_ref[0] = jnp.where(in_g, o_w, ow_ref[0])


def _nsa_s_sel_kernel(sel_ref, pt_ref, q_ref, k_ref, v_ref, tb_ref, kvn_ref, b0_ref, oc_ref, ow_ref, gt_ref,
                      o_ref, m_scr, l_scr, acc_scr):
    del sel_ref, pt_ref
    g = pl.program_id(1)
    j = pl.program_id(2)
    dh, hpg = NSA_DH, NSA_HPG
    is_g0 = g == 0
    q = q_ref[0] * (dh ** -0.5)
    kvn = kvn_ref[0]

    @pl.when(j == 0)
    def _():
        k_n = jnp.where(is_g0, kvn[:, 4 * dh:5 * dh], kvn[:, 5 * dh:6 * dh])
        v_n = jnp.where(is_g0, kvn[:, 6 * dh:7 * dh], kvn[:, 7 * dh:8 * dh])
        m_scr[...] = jnp.sum(q * k_n, axis=-1, keepdims=True) + b0_ref[...]
        l_scr[...] = jnp.ones_like(l_scr)
        acc_scr[...] = jnp.broadcast_to(v_n, acc_scr.shape)

    k = jnp.where(is_g0, k_ref[0][:, :dh], k_ref[0][:, dh:])
    v = jnp.where(is_g0, v_ref[0][:, :dh], v_ref[0][:, dh:])
    s = _mm3_nt(q, k) + tb_ref[0]
    m_old = m_scr[...]
    m_new = jnp.maximum(m_old, jnp.max(s, axis=-1, keepdims=True))
    alpha = jnp.exp(m_old - m_new)
    p = jnp.exp(s - m_new)
    l_scr[...] = alpha * l_scr[...] + jnp.sum(p, axis=-1, keepdims=True)
    acc_scr[...] = alpha * acc_scr[...] + _mm(p, v)
    m_scr[...] = m_new

    @pl.when(j == pl.num_programs(2) - 1)
    def _():
        o_s = acc_scr[...] / jnp.maximum(l_scr[...], 1e-30)
        gt = gt_ref[0]
        o = gt[:, 0:1] * oc_ref[0] + gt[:, 1:2] * o_s + gt[:, 2:3] * ow_ref[0]
        row = lax.broadcasted_iota(i32, o.shape, 0)
        in_g = (row >= g * hpg) & (row < (g + 1) * hpg)

        @pl.when(is_g0)
        def _():
            o_ref[0] = o

        @pl.when(jnp.logical_not(is_g0))
        def _():
            o_ref[0] = jnp.where(in_g, o, o_ref[0])


def _nsa_sample(q_b, gates, kv4_new, kvwin_new, cache_kv, cache_win, page_table, rel_bias):
    db = q_b.shape[0]
    dh = NSA_DH
    n_pool, page = cache_kv.shape[:2]
    n_pages = page_table.shape[1]
    past = n_pages * page
    nblk = past // NSA_BLOCK
    wb = cache_win.shape[1]
    assert nblk == PAIR and page % NSA_BLOCK == 0 and wb == NSA_WINDOW
    cache2d = cache_kv.reshape(n_pool, page, 8 * dh)
    kcvc = _nsa_s_means(cache2d, page_table)

    n = np.arange(nblk)
    d_cmp = past - (n * NSA_BLOCK + NSA_BLOCK - 1)
    jw = np.arange(wb)
    d_win = np.where(jw >= 1, wb - jw, -1)
    pos = np.arange(nblk * NSA_BLOCK)
    d_sel = past - pos
    width = nblk + wb + nblk * NSA_BLOCK
    dist = np.broadcast_to(np.concatenate([d_cmp, d_win, d_sel])[None, :], (NSA_HEADS, width))
    tab = _bias_table(rel_bias, jnp.asarray(dist[None, None], i32), slab=1)[0, 0]
    t_cmp, t_win = tab[:, :nblk], tab[:, nblk:nblk + wb]
    t_sel = jnp.transpose(tab[:, nblk + wb:].reshape(NSA_HEADS, nblk, NSA_BLOCK), (1, 0, 2))
    b0 = rel_bias.astype(f32)[0].reshape(NSA_HEADS, 1)

    q3 = q_b.reshape(db, NSA_HEADS, dh)
    win2d = cache_win.reshape(db, wb, 4 * dh).astype(f32)
    o_c, o_w, sel = pl.pallas_call(
        _nsa_s_scores_kernel,
        grid=(db,),
        in_specs=[pl.BlockSpec((1, NSA_HEADS, dh), lambda b: (b, 0, 0)),
                  pl.BlockSpec((1, nblk, 4 * dh), lambda b: (b, 0, 0)),
                  pl.BlockSpec((1, wb, 4 * dh), lambda b: (b, 0, 0)),
                  pl.BlockSpec((1, 1, 4 * dh), lambda b: (b, 0, 0)),
                  pl.BlockSpec((NSA_HEADS, nblk), lambda b: (0, 0)),
                  pl.BlockSpec((NSA_HEADS, wb), lambda b: (0, 0)),
                  pl.BlockSpec((NSA_HEADS, 1), lambda b: (0, 0))],
        out_specs=[pl.BlockSpec((1, NSA_HEADS, dh), lambda b: (b, 0, 0)),
                   pl.BlockSpec((1, NSA_HEADS, dh), lambda b: (b, 0, 0)),
                   pl.BlockSpec((1, NSA_KV_GROUPS, nblk), lambda b: (b, 0, 0))],
        out_shape=[jax.ShapeDtypeStruct((db, NSA_HEADS, dh), f32),
                   jax.ShapeDtypeStruct((db, NSA_HEADS, dh), f32),
                   jax.ShapeDtypeStruct((db, NSA_KV_GROUPS, nblk), i32)],
        compiler_params=_cparams(("parallel",)),
        name="nsa_sample_scores",
    )(q3, kcvc, win2d, kvwin_new.reshape(db, 1, 4 * dh), t_cmp, t_win, b0)

    n_sel = NSA_TOPN - 1
    sel_flat = sel[:, :, :n_sel].reshape(-1)
    halves = page // NSA_BLOCK
    cache_h = cache_kv.reshape(n_pool * halves, NSA_BLOCK, 8 * dh)

    def blk_of(b, g, j, sel_r, pt_r):
        nb = sel_r[(b * NSA_KV_GROUPS + g) * n_sel + j]
        return nb, pt_r[b * n_pages + nb // halves] * halves + nb % halves

    o = pl.pallas_call(
        _nsa_s_sel_kernel,
        grid_spec=pltpu.PrefetchScalarGridSpec(
            num_scalar_prefetch=2,
            grid=(db, NSA_KV_GROUPS, n_sel),
            in_specs=[pl.BlockSpec((1, NSA_HEADS, dh), lambda b, g, j, s, p: (b, 0, 0)),
                      pl.BlockSpec((1, NSA_BLOCK, 2 * dh), lambda b, g, j, s, p: (blk_of(b, g, j, s, p)[1], 0, 2)),
                      pl.BlockSpec((1, NSA_BLOCK, 2 * dh), lambda b, g, j, s, p: (blk_of(b, g, j, s, p)[1], 0, 3)),
                      pl.BlockSpec((1, NSA_HEADS, NSA_BLOCK), lambda b, g, j, s, p: (blk_of(b, g, j, s, p)[0], 0, 0)),
                      pl.BlockSpec((1, 1, 8 * dh), lambda b, g, j, s, p: (b, 0, 0)),
                      pl.BlockSpec((NSA_HEADS, 1), lambda b, g, j, s, p: (0, 0)),
                      pl.BlockSpec((1, NSA_HEADS, dh), lambda b, g, j, s, p: (b, 0, 0)),
                      pl.BlockSpec((1, NSA_HEADS, dh), lambda b, g, j, s, p: (b, 0, 0)),
                      pl.BlockSpec((1, NSA_HEADS, 3), lambda b, g, j, s, p: (b, 0, 0))],
            out_specs=pl.BlockSpec((1, NSA_HEADS, dh), lambda b, g, j, s, p: (b, 0, 0)),
            scratch_shapes=[pltpu.VMEM((NSA_HEADS, 1), f32), pltpu.VMEM((NSA_HEADS, 1), f32),
                            pltpu.VMEM((NSA_HEADS, dh), f32)]),
        out_shape=jax.ShapeDtypeStruct((db, NSA_HEADS, dh), f32),
        compiler_params=_cparams(("parallel", "arbitrary", "arbitrary")),
        name="nsa_sample_selected",
    )(sel_flat, page_table.reshape(-1), q3, cache_h, cache_h, t_sel, kv4_new.reshape(db, 1, 8 * dh), b0,
      o_c, o_w, gates)
    return o.reshape(db, NSA_HEADS * dh)


def _gelu_tanh(x):
    return 0.5 * x * (1.0 + jnp.tanh(math.sqrt(2.0 / math.pi) * (x + 0.044715 * (x * x * x))))


def _lru_gates(xc, wa_ref, wx_ref, ba_ref, bx_ref, lam_ref, prec):
    r_parts, i_parts = [], []
    for n in range(RNN_BLOCKS):
        xb = xc[:, n * RNN_BW:(n + 1) * RNN_BW]
        r_parts.append(_dotp(xb, wa_ref[n], prec))
        i_parts.append(_dotp(xb, wx_ref[n], prec))
    r = jax.nn.sigmoid(jnp.concatenate(r_parts, axis=1) + ba_ref[...])
    i = jax.nn.sigmoid(jnp.concatenate(i_parts, axis=1) + bx_ref[...])
    log_a = -RG_C * r * _softplus(-lam_ref[...])
    a = jnp.exp(log_a)
    t = jnp.tanh(log_a)
    b = jnp.sqrt(jnp.maximum(-2.0 * t / (1.0 - t), 0.0)) * (i * xc)
    return a, b


def _lru_kernel(rec_ref, gate_ref, cw_ref, cb_ref, wa_ref, wx_ref, ba_ref, bx_ref, lam_ref, cinit_ref, h0_ref,
                y_ref, hfin_ref, xc_scr, a_scr, b_scr, hs_scr, h_scr, *, tb):
    j = pl.program_id(1)

    @pl.when(j == 0)
    def _():
        xc_scr[0:8, :] = cinit_ref[0]
        h_scr[...] = h0_ref[0]

    x = rec_ref[0]
    xc_scr[8:8 + tb, :] = x
    xc = xc_scr[5:5 + tb, :] * cw_ref[0:1, :]
    xc = xc + xc_scr[6:6 + tb, :] * cw_ref[1:2, :]
    xc = xc + xc_scr[7:7 + tb, :] * cw_ref[2:3, :]
    xc = xc + x * cw_ref[3:4, :]
    xc = xc + cb_ref[...]
    xc_scr[0:8, :] = xc_scr[tb:tb + 8, :]
    a, b = _lru_gates(xc, wa_ref, wx_ref, ba_ref, bx_ref, lam_ref, 1)
    a_scr[...] = a
    b_scr[...] = b

    def step(t, h):
        h = a_scr[pl.ds(t, 1), :] * h + b_scr[pl.ds(t, 1), :]
        hs_scr[pl.ds(t, 1), :] = h
        return h

    h = lax.fori_loop(0, tb, step, h_scr[...], unroll=8)
    h_scr[...] = h
    y_ref[0] = _gelu_tanh(gate_ref[0]) * hs_scr[...]

    @pl.when(j == pl.num_programs(1) - 1)
    def _():
        hfin_ref[0] = h


def _lru_prompt(rec, gate, conv_w, conv_b, wa, wx, ba, bx, lam, *, tb):
    bsz, t, w = rec.shape
    row = lambda a: a.reshape(1, w).astype(f32)
    cinit = jnp.zeros((bsz, 8, w), f32)
    h0 = jnp.zeros((bsz, 1, w), f32)
    full = lambda shape: pl.BlockSpec(shape, lambda b, j: (0,) * len(shape))
    return pl.pallas_call(
        functools.partial(_lru_kernel, tb=tb),
        grid=(bsz, t // tb),
        in_specs=[pl.BlockSpec((1, tb, w), lambda b, j: (b, j, 0)),
                  pl.BlockSpec((1, tb, w), lambda b, j: (b, j, 0)),
                  full((RNN_CONV, w)), full((1, w)),
                  full((RNN_BLOCKS, RNN_BW, RNN_BW)), full((RNN_BLOCKS, RNN_BW, RNN_BW)),
                  full((1, w)), full((1, w)), full((1, w)),
                  pl.BlockSpec((1, 8, w), lambda b, j: (b, 0, 0)),
                  pl.BlockSpec((1, 1, w), lambda b, j: (b, 0, 0))],
        out_specs=[pl.BlockSpec((1, tb, w), lambda b, j: (b, j, 0)),
                   pl.BlockSpec((1, 1, w), lambda b, j: (b, 0, 0))],
        out_shape=[jax.ShapeDtypeStruct((bsz, t, w), f32), jax.ShapeDtypeStruct((bsz, 1, w), f32)],
        scratch_shapes=[pltpu.VMEM((tb + 8, w), f32), pltpu.VMEM((tb, w), f32), pltpu.VMEM((tb, w), f32),
                        pltpu.VMEM((tb, w), f32), pltpu.VMEM((1, w), f32)],
        compiler_params=_cparams(("parallel", "arbitrary")),
        name="rglru",
    )(rec, gate, conv_w.astype(f32), row(conv_b), wa.astype(f32), wx.astype(f32), row(ba), row(bx), row(lam),
      cinit, h0)


def _lru_step_kernel(rec_ref, gate_ref, b0_ref, b1_ref, b2_ref, cw_ref, cb_ref, wa_ref, wx_ref, ba_ref, bx_ref,
                     lam_ref, h0_ref, y_ref, h_ref):
    xc = b0_ref[...] * cw_ref[0:1, :]
    xc = xc + b1_ref[...] * cw_ref[1:2, :]
    xc = xc + b2_ref[...] * cw_ref[2:3, :]
    xc = xc + rec_ref[...] * cw_ref[3:4, :]
    xc = xc + cb_ref[...]
    a, b = _lru_gates(xc, wa_ref, wx_ref, ba_ref, bx_ref, lam_ref, 3)
    h = a * h0_ref[...] + b
    h_ref[...] = h
    y_ref[...] = _gelu_tanh(gate_ref[...]) * h


def _lru_sample(rec, gate, conv_buf, h0, conv_w, conv_b, wa, wx, ba, bx, lam):
    db, w = rec.shape
    row = lambda a: a.reshape(1, w).astype(f32)
    buf = conv_buf.astype(f32)
    return pl.pallas_call(
        _lru_step_kernel,
        out_shape=[jax.ShapeDtypeStruct((db, w), f32), jax.ShapeDtypeStruct((db, w), f32)],
        compiler_params=pltpu.CompilerParams(vmem_limit_bytes=VMEM_LIMIT),
        name="rglru_step",
    )(rec, gate, buf[:, 0], buf[:, 1], buf[:, 2], conv_w.astype(f32), row(conv_b), wa.astype(f32), wx.astype(f32),
      row(ba), row(bx), row(lam), h0.astype(f32))


def _top2_kernel(l_ref, e_ref, g_ref):
    lg = l_ref[...]
    lane = lax.broadcasted_iota(i32, lg.shape, 1)
    lg = jnp.where(lane < N_EXPERTS, lg, -jnp.inf)
    m1 = jnp.max(lg, axis=-1, keepdims=True)
    i1 = jnp.min(jnp.where(lg == m1, lane, lg.shape[1]), axis=-1, keepdims=True)
    lg2 = jnp.where(lane == i1, -jnp.inf, lg)
    m2 = jnp.max(lg2, axis=-1, keepdims=True)
    i2 = jnp.min(jnp.where(lg2 == m2, lane, lg.shape[1]), axis=-1, keepdims=True)
    e2 = jnp.exp(m2 - m1)
    den = 1.0 + e2
    e_ref[...] = jnp.where(lane == 0, i1, jnp.where(lane == 1, i2, 0))
    g_ref[...] = jnp.where(lane == 0, 1.0 / den, jnp.where(lane == 1, e2 / den, 0.0))


def _top2(logits, *, tm):
    n, w = logits.shape
    return pl.pallas_call(
        _top2_kernel,
        grid=(n // tm,),
        in_specs=[pl.BlockSpec((tm, w), lambda i: (i, 0))],
        out_specs=[pl.BlockSpec((tm, w), lambda i: (i, 0)), pl.BlockSpec((tm, w), lambda i: (i, 0))],
        out_shape=[jax.ShapeDtypeStruct((n, w), i32), jax.ShapeDtypeStruct((n, w), f32)],
        compiler_params=_cparams(("parallel",)),
        name="moe_top2",
    )(logits)


def _moe_kernel(be_ref, nu_ref, xs_ref, w1_ref, w3_ref, w2_ref, o_ref, acc_scr):
    del be_ref
    i = pl.program_id(0)
    f = pl.program_id(1)
    last = pl.num_programs(1) - 1
    used = i < nu_ref[0]

    @pl.when(used)
    def _():
        @pl.when(f == 0)
        def _():
            acc_scr[...] = jnp.zeros_like(acc_scr)

        x = xs_ref[...]
        act = _silu(_mm(x, w1_ref[0])) * _mm(x, w3_ref[0])
        acc_scr[...] += _mm(act, w2_ref[0])

        @pl.when(f == last)
        def _():
            o_ref[...] = acc_scr[...]

    @pl.when(jnp.logical_not(used) & (f == last))
    def _():
        o_ref[...] = jnp.zeros_like(o_ref)


def _moe_experts(xs, blk_e, n_used, w1, w3, w2, *, tm, tf):
    rows, d = xs.shape
    fdim = w1.shape[2]
    nf = fdim // tf

    def wcol(i, f, be, nu):
        return (be[i], 0, jnp.where(i < nu[0], f, nf - 1))

    def wrow(i, f, be, nu):
        return (be[i], jnp.where(i < nu[0], f, nf - 1), 0)

    return pl.pallas_call(
        _moe_kernel,
        grid_spec=pltpu.PrefetchScalarGridSpec(
            num_scalar_prefetch=2,
            grid=(rows // tm, nf),
            in_specs=[pl.BlockSpec((tm, d), lambda i, f, be, nu: (i, 0)),
                      pl.BlockSpec((1, d, tf), wcol),
                      pl.BlockSpec((1, d, tf), wcol),
                      pl.BlockSpec((1, tf, d), wrow)],
            out_specs=pl.BlockSpec((tm, d), lambda i, f, be, nu: (i, 0)),
            scratch_shapes=[pltpu.VMEM((tm, d), f32)]),
        out_shape=jax.ShapeDtypeStruct((rows, d), f32),
        compiler_params=_cparams(("arbitrary", "arbitrary")),
        name="moe_experts",
    )(blk_e, n_used, xs, w1, w3, w2)


def _combine_kernel(x_ref, gate_ref, y0_ref, y1_ref, g_ref, nw_ref, o_ref):
    gw = g_ref[0]
    y = y0_ref[0] * gw[:, 0:1] + y1_ref[0] * gw[:, 1:2]
    x = x_ref[0] + gate_ref[0] * y
    o_ref[0] = x * lax.rsqrt(jnp.mean(x * x, axis=-1, keepdims=True) + EPS) * nw_ref[...]


def _moe_combine(x, gate, y0, y1, gw, norm_w, *, tm):
    bsz, t, d = x.shape
    tok = lambda w: pl.BlockSpec((1, tm, w), lambda b, i: (b, i, 0))
    return pl.pallas_call(
        _combine_kernel,
        grid=(bsz, t // tm),
        in_specs=[tok(d), _mod_spec(gate, tm), tok(d), tok(d), tok(gw.shape[-1]),
                  pl.BlockSpec((1, d), lambda b, i: (0, 0))],
        out_specs=tok(d),
        out_shape=jax.ShapeDtypeStruct((bsz, t, d), f32),
        compiler_params=_cparams(("parallel", "parallel")),
        name="moe_combine",
    )(x, gate, y0, y1, gw, norm_w.reshape(1, d).astype(f32))


MOE_TM = 512
MOE_TF = 512


def _moe_dispatch(e_idx, n_tok):
    tm = MOE_TM
    n_assign = n_tok * TOP_K
    e_flat = e_idx.reshape(n_assign)
    onehot = (e_flat[:, None] == jnp.arange(N_EXPERTS, dtype=i32)[None, :]).astype(i32)
    rank = jnp.take_along_axis(jnp.cumsum(onehot, axis=0), e_flat[:, None], axis=1)[:, 0] - 1
    counts = jnp.sum(onehot, axis=0)
    padded = (counts + tm - 1) // tm * tm
    pend = jnp.cumsum(padded)
    pstart = pend - padded
    dest = pstart[e_flat] + rank
    n_blocks = -(-n_assign // tm) + N_EXPERTS
    rows = n_blocks * tm
    tok_flat = jnp.repeat(jnp.arange(n_tok, dtype=i32), TOP_K)
    row_tok = jnp.zeros((rows,), i32).at[dest].set(tok_flat)
    n_used = (pend[-1] // tm).astype(i32)
    blk = jnp.minimum(jnp.arange(n_blocks, dtype=i32), n_used - 1) * tm
    blk_e = jnp.minimum(jnp.searchsorted(pend, blk, side='right'), N_EXPERTS - 1).astype(i32)
    return dest.reshape(n_tok, TOP_K), row_tok, blk_e, n_used.reshape(1)


def _w_in0_layout(w_in0):
    c_qkv = GDN_CONV_CH
    c_z = c_qkv + GDN_HEADS * GDN_DV
    c_ab = c_z + 2 * GDN_HEADS
    c_q = c_ab + NSA_HEADS * NSA_DH
    c_kv = c_q + 6 * NSA_KV_GROUPS * NSA_DH
    c_g = c_kv + 3 * NSA_HEADS
    assert c_g == w_in0.shape[1]
    n_misc = 2 * GDN_HEADS + 3 * NSA_HEADS
    w = jnp.concatenate([w_in0[:, :c_z], w_in0[:, c_ab:c_kv], w_in0[:, c_z:c_ab], w_in0[:, c_kv:c_g],
                         jnp.zeros((w_in0.shape[0], MISC_W - n_misc), w_in0.dtype)], axis=1)
    widths = (GDN_CONV_CH, GDN_HEADS * GDN_DV, NSA_HEADS * NSA_DH, 4 * NSA_KV_GROUPS * NSA_DH,
              2 * NSA_KV_GROUPS * NSA_DH, MISC_W)
    splits, s = [], 0
    for wd in widths:
        splits.append((s, s + wd))
        s += wd
    return w, tuple(splits)


def kernel(x_prompt, x_sample, c_prompt, c_sample, cache_nsa_kv, cache_nsa_win, state_gdn, state_gdn_conv, state_lru, state_lru_conv, page_table, rel_bias, w_ada, b_ada, norm_mix, norm_ffn, norm_final, w_in0, gdn_conv_w, gdn_a_log, gdn_dt_bias, gdn_norm_w, w_out0, ffn_w_gate, ffn_w_up, ffn_w_down, w_in1, lru_conv_w, lru_conv_b, lru_wa, lru_ba, lru_wx, lru_bx, lru_lambda, w_out1, moe_router, moe_w1, moe_w3, moe_w2):
    bsz, seq, d = x_prompt.shape
    db = x_sample.shape[0]
    assert x_sample.shape[1] == 1
    dh = NSA_DH

    n_c = bsz + db
    n_c_pad = -(-n_c // 8) * 8
    c_all = jnp.concatenate([c_prompt, c_sample, jnp.zeros((n_c_pad - n_c, d), f32)], axis=0)
    mods = _adaln(c_all, w_ada, b_ada).reshape(2, n_c_pad, N_MOD, d)
    mod_p = [[mods[l, :bsz, k].reshape(bsz, 1, d) for k in range(N_MOD)] for l in range(2)]
    mod_s = [[mods[l, bsz:n_c, k].reshape(1, db, d) for k in range(N_MOD)] for l in range(2)]

    w0, splits0 = _w_in0_layout(w_in0)
    splits1 = ((0, RNN_WIDTH), (RNN_WIDTH, 2 * RNN_WIDTH))
    router = jnp.concatenate([moe_router, jnp.zeros((d, MISC_W - N_EXPERTS), f32)], axis=1)
    bf = lambda w: w.astype(bf16)

    tm = 512
    xp = x_prompt
    qkv, z, q_b, kv4, kvwin, misc = _mod_matmul(xp, norm_mix[0], mod_p[0][0], mod_p[0][1], bf(w0), splits0,
                                                tm=tm, prec=1)
    o_a, p_gdn = _gdn(qkv, z, misc, gdn_conv_w, jnp.zeros((bsz, GDN_CONV - 1, GDN_CONV_CH), f32), gdn_a_log,
                      gdn_dt_bias, gdn_norm_w, jnp.zeros((bsz, GDN_HEADS, GDN_DK, GDN_DV), f32), tb=256, n_valid=seq)
    p_gdn_conv = qkv[:, seq - (GDN_CONV - 1):]
    o_b = _nsa_prompt(q_b, misc, kv4, kvwin, rel_bias)
    p_nsa_kv = kv4.reshape(bsz, seq, 4, NSA_KV_GROUPS, dh)
    keep = min(NSA_WINDOW, seq)
    p_nsa_win = kvwin[:, seq - keep:].reshape(bsz, keep, 2, NSA_KV_GROUPS, dh)
    xp = _proj_residual([o_a, o_b], bf(w_out0), xp, mod_p[0][2], tm=tm, prec=1)
    xp = _ffn(xp, norm_ffn[0], mod_p[0][3], mod_p[0][4], mod_p[0][5], bf(ffn_w_gate), bf(ffn_w_up), bf(ffn_w_down),
              tm=tm, tf=FFN_DIM // 2, prec=1)
    gate_br, rec_br = _mod_matmul(xp, norm_mix[1], mod_p[1][0], mod_p[1][1], bf(w_in1), splits1, tm=tm, prec=1)
    y_in, p_lru = _lru_prompt(rec_br, gate_br, lru_conv_w, lru_conv_b, lru_wa, lru_wx, lru_ba, lru_bx, lru_lambda,
                              tb=256)
    p_lru_conv = rec_br[:, seq - (RNN_CONV - 1):]
    xp = _proj_residual([y_in], bf(w_out1), xp, mod_p[1][2], tm=tm, prec=1)
    logit_p, h_p = _mod_matmul(xp, norm_ffn[1], mod_p[1][3], mod_p[1][4], router, ((0, MISC_W),), tm=tm, prec=3,
                               emit_h=bf16)

    xs = x_sample.reshape(1, db, d)
    qkv_s, z_s, q_s, kv4_s, kvwin_s, misc_s = _mod_matmul(xs, norm_mix[0], mod_s[0][0], mod_s[0][1], w0, splits0,
                                                          tm=db, prec=3)
    c = GDN_CHUNK
    tpad = lambda a: jnp.pad(a.reshape(db, 1, a.shape[-1]), ((0, 0), (0, c - 1), (0, 0)))
    o_a_s, s_gdn = _gdn(tpad(qkv_s), tpad(z_s), tpad(misc_s), gdn_conv_w, state_gdn_conv, gdn_a_log, gdn_dt_bias,
                        gdn_norm_w, state_gdn, tb=c, n_valid=1)
    o_a_s = o_a_s[:, 0].reshape(1, db, GDN_HEADS * GDN_DV)
    s_gdn_conv = jnp.concatenate([state_gdn_conv[:, 1:], qkv_s.reshape(db, 1, GDN_CONV_CH)], axis=1)
    gates_s = jax.nn.sigmoid(misc_s[0, :, 2 * GDN_HEADS:2 * GDN_HEADS + 3 * NSA_HEADS]).reshape(db, NSA_HEADS, 3)
    o_b_s = _nsa_sample(q_s[0], gates_s, kv4_s[0], kvwin_s[0], cache_nsa_kv, cache_nsa_win, page_table, rel_bias)
    s_nsa_kv = kv4_s.reshape(db, 1, 4, NSA_KV_GROUPS, dh)
    s_nsa_win = jnp.concatenate([cache_nsa_win[:, 1:],
                                 kvwin_s.reshape(db, 1, 2, NSA_KV_GROUPS, dh).astype(cache_nsa_win.dtype)], axis=1)
    xs = _proj_residual([o_a_s, o_b_s.reshape(1, db, NSA_HEADS * dh)], w_out0, xs, mod_s[0][2], tm=db, prec=3)
    xs = _ffn(xs, norm_ffn[0], mod_s[0][3], mod_s[0][4], mod_s[0][5], ffn_w_gate, ffn_w_up, ffn_w_down,
              tm=db, tf=256, prec=3)
    gate_s, rec_s = _mod_matmul(xs, norm_mix[1], mod_s[1][0], mod_s[1][1], w_in1, splits1, tm=db, prec=3)
    y_in_s, s_lru = _lru_sample(rec_s[0], gate_s[0], state_lru_conv, state_lru, lru_conv_w, lru_conv_b, lru_wa, lru_wx,
                                lru_ba, lru_bx, lru_lambda)
    s_lru_conv = jnp.concatenate([state_lru_conv[:, 1:], rec_s.reshape(db, 1, RNN_WIDTH)], axis=1)
    xs = _proj_residual([y_in_s.reshape(1, db, RNN_WIDTH)], w_out1, xs, mod_s[1][2], tm=db, prec=3)
    logit_s, h_s = _mod_matmul(xs, norm_ffn[1], mod_s[1][3], mod_s[1][4], router, ((0, MISC_W),), tm=db, prec=3,
                               emit_h=bf16)

    n_p = bsz * seq
    n_tok = n_p + db
    logits = jnp.concatenate([logit_p.reshape(n_p, MISC_W), logit_s.reshape(db, MISC_W)], axis=0)
    h_all = jnp.concatenate([h_p.reshape(n_p, d), h_s.reshape(db, d)], axis=0)
    e_idx, gw = _top2(logits, tm=db)
    dest, row_tok, blk_e, n_used = _moe_dispatch(e_idx[:, :TOP_K], n_tok)
    yb = _moe_experts(h_all[row_tok], blk_e, n_used, moe_w1, moe_w3, moe_w2, tm=MOE_TM, tf=MOE_TF)
    y0, y1 = yb[dest[:, 0]], yb[dest[:, 1]]
    y_prompt = _moe_combine(xp, mod_p[1][5], y0[:n_p].reshape(bsz, seq, d), y1[:n_p].reshape(bsz, seq, d),
                            gw[:n_p].reshape(bsz, seq, MISC_W), norm_final, tm=tm)
    y_sample = _moe_combine(xs, mod_s[1][5], y0[n_p:].reshape(1, db, d), y1[n_p:].reshape(1, db, d),
                            gw[n_p:].reshape(1, db, MISC_W), norm_final, tm=db)

    return (y_prompt, y_sample.reshape(db, 1, d),
            p_nsa_kv, p_nsa_win, p_gdn, p_gdn_conv, p_lru.reshape(bsz, RNN_WIDTH), p_lru_conv,
            s_nsa_kv, s_nsa_win, s_gdn, s_gdn_conv, s_lru, s_lru_conv)
```

```python
import functools
import math

import numpy as np
import jax
import jax.numpy as jnp
from jax import lax
from jax.experimental import pallas as pl
from jax.experimental.pallas import tpu as pltpu

f32 = jnp.float32
bf16 = jnp.bfloat16
i32 = jnp.int32

D_MODEL = 1024
EPS = 1e-6
N_MOD = 6
GDN_HEADS = 4
GDN_DK = 128
GDN_DV = 128
GDN_CONV = 4
GDN_CHUNK = 64
GDN_CONV_CH = GDN_HEADS * (2 * GDN_DK + GDN_DV)
NSA_HEADS = 8
NSA_KV_GROUPS = 2
NSA_HPG = NSA_HEADS // NSA_KV_GROUPS
NSA_DH = 64
NSA_BLOCK = 64
NSA_TOPN = 16
NSA_LOCAL = 2
NSA_WINDOW = 512
NSA_QBLOCK = 128
NSA_FORCE = 1.0e4
REL_BUCKETS = 32
REL_MAX_DIST = 2048
RNN_WIDTH = D_MODEL
RNN_BLOCKS = 8
RNN_BW = RNN_WIDTH // RNN_BLOCKS
RNN_CONV = 4
RG_C = 8.0
FFN_DIM = 2816
N_EXPERTS = 8
TOP_K = 2
EXPERT_DIM = 3584

NEG = -1e30
PAIR = 2 * NSA_BLOCK
NEAR_PAIRS = 13
MISC_W = 128
VMEM_LIMIT = 56 * 1024 * 1024


def _cparams(sem, vmem=VMEM_LIMIT):
    return pltpu.CompilerParams(dimension_semantics=sem, vmem_limit_bytes=vmem)


def _mm(a, b):
    return jnp.dot(a.astype(bf16), b.astype(bf16), preferred_element_type=f32)


def _mm_nt(a, b):
    return lax.dot_general(a.astype(bf16), b.astype(bf16), (((1,), (1,)), ((), ())),
                           preferred_element_type=f32)


def _split2(a):
    hi = a.astype(bf16)
    lo = (a - hi.astype(f32)).astype(bf16)
    return hi, lo


def _mm3(a, b):
    ah, al = _split2(a)
    bh, bl = _split2(b)
    return (jnp.dot(ah, bh, preferred_element_type=f32) + jnp.dot(ah, bl, preferred_element_type=f32)
            + jnp.dot(al, bh, preferred_element_type=f32))


def _mm3_nt(a, b):
    ah, al = _split2(a)
    bh, bl = _split2(b)
    dn = (((1,), (1,)), ((), ()))
    return (lax.dot_general(ah, bh, dn, preferred_element_type=f32)
            + lax.dot_general(ah, bl, dn, preferred_element_type=f32)
            + lax.dot_general(al, bh, dn, preferred_element_type=f32))


def _mm_01(m01, a):
    hi = a.astype(bf16)
    r1 = a - hi.astype(f32)
    mid = r1.astype(bf16)
    lo = (r1 - mid.astype(f32)).astype(bf16)
    return (jnp.dot(m01, hi, preferred_element_type=f32) + jnp.dot(m01, mid, preferred_element_type=f32)
            + jnp.dot(m01, lo, preferred_element_type=f32))


def _dotp(a, b, prec):
    return _mm3(a, b) if prec == 3 else _mm(a, b)


def _silu(x):
    return x * jax.nn.sigmoid(x)


def _softplus(x):
    return jnp.maximum(x, 0.0) + jnp.log1p(jnp.exp(-jnp.abs(x)))


def _modulate(x, gain, shift, scale):
    r = lax.rsqrt(jnp.mean(x * x, axis=-1, keepdims=True) + EPS)
    return x * r * gain * (1.0 + scale) + shift


def _ada_kernel(c_ref, w_ref, b_ref, o_ref):
    o_ref[0] = _mm3(_silu(c_ref[...]), w_ref[0]) + b_ref[0]


def _adaln(c_all, w_ada, b_ada):
    rows = c_all.shape[0]
    depth, d, n = w_ada.shape
    tn = 1536
    return pl.pallas_call(
        _ada_kernel,
        grid=(depth, n // tn),
        in_specs=[pl.BlockSpec((rows, d), lambda l, j: (0, 0)),
                  pl.BlockSpec((1, d, tn), lambda l, j: (l, 0, j)),
                  pl.BlockSpec((1, 1, tn), lambda l, j: (l, 0, j))],
        out_specs=pl.BlockSpec((1, rows, tn), lambda l, j: (l, 0, j)),
        out_shape=jax.ShapeDtypeStruct((depth, rows, n), f32),
        compiler_params=_cparams(("arbitrary", "arbitrary")),
        name="adaln",
    )(c_all, w_ada, b_ada.reshape(depth, 1, n))


def _mod_spec(mod, tm):
    r = mod.shape[1]
    if r == 1:
        return pl.BlockSpec((1, 1, mod.shape[2]), lambda b, i: (b, 0, 0))
    return pl.BlockSpec((1, tm, mod.shape[2]), lambda b, i: (b, i, 0))


def _modmm_kernel(x_ref, gain_ref, shift_ref, scale_ref, w_ref, *o_refs, splits, prec, emit_h):
    h = _modulate(x_ref[0], gain_ref[...], shift_ref[0], scale_ref[0])
    if emit_h:
        o_refs[-1][0] = h.astype(o_refs[-1].dtype)
    hh = _split2(h) if prec == 3 else h.astype(bf16)
    for o_ref, (a, b) in zip(o_refs, splits):
        w = w_ref[:, a:b]
        if prec == 3:
            wh, wl = _split2(w)
            acc = (jnp.dot(hh[0], wh, preferred_element_type=f32) + jnp.dot(hh[0], wl, preferred_element_type=f32)
                   + jnp.dot(hh[1], wh, preferred_element_type=f32))
        else:
            acc = jnp.dot(hh, w, preferred_element_type=f32)
        o_ref[0] = acc


def _mod_matmul(x, gain, shift, scale, w, splits, *, tm, prec, emit_h=None):
    bsz, t, d = x.shape
    out_shape = [jax.ShapeDtypeStruct((bsz, t, b - a), f32) for a, b in splits]
    out_specs = [pl.BlockSpec((1, tm, b - a), lambda bi, i: (bi, i, 0)) for a, b in splits]
    if emit_h is not None:
        out_shape.append(jax.ShapeDtypeStruct((bsz, t, d), emit_h))
        out_specs.append(pl.BlockSpec((1, tm, d), lambda bi, i: (bi, i, 0)))
    return pl.pallas_call(
        functools.partial(_modmm_kernel, splits=tuple(splits), prec=prec, emit_h=emit_h is not None),
        grid=(bsz, t // tm),
        in_specs=[pl.BlockSpec((1, tm, d), lambda bi, i: (bi, i, 0)),
                  pl.BlockSpec((1, d), lambda bi, i: (0, 0)),
                  _mod_spec(shift, tm), _mod_spec(scale, tm),
                  pl.BlockSpec(w.shape, lambda bi, i: (0, 0))],
        out_specs=out_specs,
        out_shape=out_shape,
        compiler_params=_cparams(("parallel", "parallel")),
        name="mod_matmul",
    )(x, gain.reshape(1, d), shift, scale, w)


def _projres_kernel(*refs, n_lhs, ksplits, prec, final_norm):
    lhs = refs[:n_lhs]
    w_ref, x_ref, gate_ref = refs[n_lhs:n_lhs + 3]
    o_ref = refs[-1]
    acc = None
    for l_ref, (a, b) in zip(lhs, ksplits):
        part = _dotp(l_ref[0], w_ref[a:b, :], prec)
        acc = part if acc is None else acc + part
    y = x_ref[0] + gate_ref[0] * acc
    if final_norm:
        nw_ref = refs[n_lhs + 3]
        y = y * lax.rsqrt(jnp.mean(y * y, axis=-1, keepdims=True) + EPS) * nw_ref[...]
    o_ref[0] = y


def _proj_residual(lhs_list, w, x, gate, *, tm, prec, norm_w=None):
    bsz, t, d = x.shape
    ksplits, k0 = [], 0
    for l in lhs_list:
        ksplits.append((k0, k0 + l.shape[-1]))
        k0 += l.shape[-1]
    in_specs = [pl.BlockSpec((1, tm, l.shape[-1]), lambda bi, i: (bi, i, 0)) for l in lhs_list]
    in_specs += [pl.BlockSpec(w.shape, lambda bi, i: (0, 0)),
                 pl.BlockSpec((1, tm, d), lambda bi, i: (bi, i, 0)),
                 _mod_spec(gate, tm)]
    args = list(lhs_list) + [w, x, gate]
    if norm_w is not None:
        in_specs.append(pl.BlockSpec((1, d), lambda bi, i: (0, 0)))
        args.append(norm_w.reshape(1, d))
    return pl.pallas_call(
        functools.partial(_projres_kernel, n_lhs=len(lhs_list), ksplits=tuple(ksplits), prec=prec,
                          final_norm=norm_w is not None),
        grid=(bsz, t // tm),
        in_specs=in_specs,
        out_specs=pl.BlockSpec((1, tm, d), lambda bi, i: (bi, i, 0)),
        out_shape=jax.ShapeDtypeStruct((bsz, t, d), f32),
        compiler_params=_cparams(("parallel", "parallel")),
        name="proj_residual",
    )(*args)


def _ffn_kernel(x_ref, gain_ref, shift_ref, scale_ref, gate_ref, wg_ref, wu_ref, wd_ref, o_ref,
                h_scr, acc_scr, *, prec):
    f = pl.program_id(2)

    @pl.when(f == 0)
    def _():
        h_scr[...] = _modulate(x_ref[0], gain_ref[...], shift_ref[0], scale_ref[0]).astype(h_scr.dtype)
        acc_scr[...] = jnp.zeros_like(acc_scr)

    h = h_scr[...]
    act = _silu(_dotp(h, wg_ref[...], prec)) * _dotp(h, wu_ref[...], prec)
    acc_scr[...] += _dotp(act, wd_ref[...], prec)

    @pl.when(f == pl.num_programs(2) - 1)
    def _():
        o_ref[0] = x_ref[0] + gate_ref[0] * acc_scr[...]


def _ffn(x, gain, shift, scale, gate, wg, wu, wd, *, tm, tf, prec):
    bsz, t, d = x.shape
    fdim = wg.shape[1]
    mod_specs = []
    for mod in (shift, scale, gate):
        if mod.shape[1] == 1:
            mod_specs.append(pl.BlockSpec((1, 1, d), lambda b, i, f: (b, 0, 0)))
        else:
            mod_specs.append(pl.BlockSpec((1, tm, d), lambda b, i, f: (b, i, 0)))
    return pl.pallas_call(
        functools.partial(_ffn_kernel, prec=prec),
        grid=(bsz, t // tm, fdim // tf),
        in_specs=[pl.BlockSpec((1, tm, d), lambda b, i, f: (b, i, 0)),
                  pl.BlockSpec((1, d), lambda b, i, f: (0, 0)),
                  mod_specs[0], mod_specs[1], mod_specs[2],
                  pl.BlockSpec((d, tf), lambda b, i, f: (0, f)),
                  pl.BlockSpec((d, tf), lambda b, i, f: (0, f)),
                  pl.BlockSpec((tf, d), lambda b, i, f: (f, 0))],
        out_specs=pl.BlockSpec((1, tm, d), lambda b, i, f: (b, i, 0)),
        out_shape=jax.ShapeDtypeStruct((bsz, t, d), f32),
        scratch_shapes=[pltpu.VMEM((tm, d), f32 if prec == 3 else bf16), pltpu.VMEM((tm, d), f32)],
        compiler_params=_cparams(("parallel", "parallel", "arbitrary")),
        name="ffn",
    )(x, gain.reshape(1, d), shift, scale, gate, wg, wu, wd)


def _gdn_kernel(qkv_ref, z_ref, misc_ref, cw_ref, cinit_ref, hp_ref, nw_ref, s0_ref,
                o_ref, sfin_ref, xc_scr, s_scr, *, tb, n_valid):
    j = pl.program_id(1)
    c = GDN_CHUNK
    nh = GDN_HEADS

    @pl.when(j == 0)
    def _():
        xc_scr[0:8, :] = cinit_ref[0]
        s_scr[...] = s0_ref[0]

    x = qkv_ref[0]
    xc_scr[8:8 + tb, :] = x
    y = xc_scr[5:5 + tb, :] * cw_ref[0:1, :]
    y = y + xc_scr[6:6 + tb, :] * cw_ref[1:2, :]
    y = y + xc_scr[7:7 + tb, :] * cw_ref[2:3, :]
    y = y + x * cw_ref[3:4, :]
    xc_scr[0:8, :] = xc_scr[tb:tb + 8, :]
    y = _silu(y)

    misc = misc_ref[0]
    row = lax.broadcasted_iota(i32, (tb, MISC_W), 0) + j * tb
    live = row < n_valid
    log_a = jnp.where(live, hp_ref[0:1, :] * _softplus(misc + hp_ref[1:2, :]), 0.0)
    beta = jnp.where(live, jax.nn.sigmoid(misc), 0.0)

    r_i = lax.broadcasted_iota(i32, (tb, tb), 0)
    c_i = lax.broadcasted_iota(i32, (tb, tb), 1)
    sh = int(math.log2(c))
    ltri = (((r_i >> sh) == (c_i >> sh)) & (c_i <= r_i)).astype(bf16)
    g = _mm_01(ltri, log_a)
    g_t = g.T

    ri = lax.broadcasted_iota(i32, (c, c), 0)
    ci = lax.broadcasted_iota(i32, (c, c), 1)
    incl = ci <= ri
    strict = ci < ri
    eye = (ci == ri).astype(f32)
    quad = []
    for lvl in range(int(math.log2(c))):
        quad.append(((ri >> (lvl + 1)) == (ci >> (lvl + 1))) & (((ri >> lvl) & 1) == 1) & (((ci >> lvl) & 1) == 0))

    for n in range(tb // c):
        r0 = n * c
        for h in range(nh):
            q = y[r0:r0 + c, h * GDN_DK:(h + 1) * GDN_DK]
            k = y[r0:r0 + c, nh * GDN_DK + h * GDN_DK:nh * GDN_DK + (h + 1) * GDN_DK]
            v = y[r0:r0 + c, 2 * nh * GDN_DK + h * GDN_DV:2 * nh * GDN_DK + (h + 1) * GDN_DV]
            q = q * lax.rsqrt(jnp.sum(q * q, axis=-1, keepdims=True) + EPS) * (GDN_DK ** -0.5)
            k = k * lax.rsqrt(jnp.sum(k * k, axis=-1, keepdims=True) + EPS)
            g_col = g[r0:r0 + c, h:h + 1]
            g_row = g_t[h:h + 1, r0:r0 + c]
            b_col = beta[r0:r0 + c, nh + h:nh + h + 1]
            gam = jnp.where(incl, jnp.exp(jnp.where(incl, g_col - g_row, 0.0)), 0.0)
            kk = _mm_nt(k, k)
            a = jnp.where(strict, b_col * gam * kk, 0.0)
            p = eye - jnp.where(quad[0], a, 0.0)
            for lvl in range(1, len(quad)):
                lq = jnp.where(quad[lvl], a, 0.0).astype(bf16)
                ph = p.astype(bf16)
                m = jnp.dot(lq, ph, preferred_element_type=f32)
                p = p - jnp.dot(ph, m.astype(bf16), preferred_element_type=f32)
            e_g = jnp.exp(g_col)
            rhs = jnp.concatenate([b_col * v, (b_col * e_g) * k], axis=-1)
            sol = _mm(p, rhs)
            vb, w = sol[:, :GDN_DV], sol[:, GDN_DV:]
            aqk = _mm_nt(q, k) * gam
            qg = q * e_g
            g_last = g_col[c - 1:c, :]
            kd = k * jnp.exp(g_last - g_col)
            gc = jnp.exp(g_last)
            s = s_scr[h]
            u = vb - _mm(w, s)
            o = _mm(qg, s) + _mm(aqk, u)
            s_scr[h] = gc * s + _mm(kd.T, u)
            o = o * lax.rsqrt(jnp.mean(o * o, axis=-1, keepdims=True) + EPS) * nw_ref[...]
            o = o * _silu(z_ref[0, r0:r0 + c, h * GDN_DV:(h + 1) * GDN_DV])
            o_ref[0, r0:r0 + c, h * GDN_DV:(h + 1) * GDN_DV] = o

    @pl.when(j == pl.num_programs(1) - 1)
    def _():
        sfin_ref[0] = s_scr[...]


def _gdn(qkv_raw, z, misc, conv_w, conv_buf, a_log, dt_bias, norm_w, s0, *, tb, n_valid):
    bsz, tp, ch = qkv_raw.shape
    cinit = jnp.concatenate([jnp.zeros((bsz, 5, ch), f32), conv_buf.astype(f32)], axis=1)
    hp = jnp.zeros((8, MISC_W), f32)
    hp = hp.at[0, :GDN_HEADS].set(-jnp.exp(a_log.astype(f32))).at[1, :GDN_HEADS].set(dt_bias.astype(f32))
    zw = GDN_HEADS * GDN_DV
    return pl.pallas_call(
        functools.partial(_gdn_kernel, tb=tb, n_valid=n_valid),
        grid=(bsz, tp // tb),
        in_specs=[pl.BlockSpec((1, tb, ch), lambda b, j: (b, j, 0)),
                  pl.BlockSpec((1, tb, zw), lambda b, j: (b, j, 0)),
                  pl.BlockSpec((1, tb, MISC_W), lambda b, j: (b, j, 0)),
                  pl.BlockSpec((GDN_CONV, ch), lambda b, j: (0, 0)),
                  pl.BlockSpec((1, 8, ch), lambda b, j: (b, 0, 0)),
                  pl.BlockSpec((8, MISC_W), lambda b, j: (0, 0)),
                  pl.BlockSpec((1, GDN_DV), lambda b, j: (0, 0)),
                  pl.BlockSpec((1, GDN_HEADS, GDN_DK, GDN_DV), lambda b, j: (b, 0, 0, 0))],
        out_specs=[pl.BlockSpec((1, tb, zw), lambda b, j: (b, j, 0)),
                   pl.BlockSpec((1, GDN_HEADS, GDN_DK, GDN_DV), lambda b, j: (b, 0, 0, 0))],
        out_shape=[jax.ShapeDtypeStruct((bsz, tp, zw), f32),
                   jax.ShapeDtypeStruct((bsz, GDN_HEADS, GDN_DK, GDN_DV), f32)],
        scratch_shapes=[pltpu.VMEM((tb + 8, ch), f32), pltpu.VMEM((GDN_HEADS, GDN_DK, GDN_DV), f32)],
        compiler_params=_cparams(("parallel", "arbitrary")),
        name="gdn",
    )(qkv_raw, z, misc, conv_w.astype(f32), cinit, hp, norm_w.reshape(1, GDN_DV).astype(f32), s0.astype(f32))


def _bucket_thresholds():
    exact = REL_BUCKETS // 2
    d = np.arange(0, 4 * REL_MAX_DIST, dtype=np.int64)

    def buckets(ft):
        nf = np.maximum(d, exact).astype(ft)
        large = exact + (np.log(nf / ft(exact)) / ft(math.log(REL_MAX_DIST / exact)) * ft(REL_BUCKETS - exact)).astype(np.int32)
        return np.where(d < exact, d, np.minimum(large, REL_BUCKETS - 1))

    b64, b32 = buckets(np.float64), buckets(np.float32)
    assert np.array_equal(b64, b32) and b64[-1] == REL_BUCKETS - 1 and np.all(np.diff(b64) >= 0)
    return [int(np.argmax(b64 >= k)) for k in range(1, REL_BUCKETS)]


_BUCKET_THR = _bucket_thresholds()
FAR_DIST = _BUCKET_THR[-1]


def _bias_kernel(rb_ref, d_ref, o_ref, *, slab, group_heads):
    rows = d_ref.shape[-2]
    for s in range(rows // slab):
        d = d_ref[0, 0, s * slab:(s + 1) * slab, :]
        h = pl.program_id(0) * group_heads + s
        val = jnp.full(d.shape, rb_ref[0, h], f32)
        for k in range(1, REL_BUCKETS):
            val = jnp.where(d >= _BUCKET_THR[k - 1], rb_ref[k, h], val)
        o_ref[0, 0, s * slab:(s + 1) * slab, :] = jnp.where(d < 0, NEG, val)


def _bias_table(rel_bias, dist, *, slab):
    g, nt, rows, cols = dist.shape
    return pl.pallas_call(
        functools.partial(_bias_kernel, slab=slab, group_heads=rows // slab),
        grid=(g, nt),
        in_specs=[pl.BlockSpec(memory_space=pltpu.SMEM),
                  pl.BlockSpec((1, 1, rows, cols), lambda a, b: (a, b, 0, 0))],
        out_specs=pl.BlockSpec((1, 1, rows, cols), lambda a, b: (a, b, 0, 0)),
        out_shape=jax.ShapeDtypeStruct(dist.shape, f32),
        compiler_params=_cparams(("arbitrary", "arbitrary")),
        name="bias_table",
    )(rel_bias.astype(f32), dist)


def _blockmean_kernel(x_ref, o_ref):
    x = x_ref[0]
    nb = x.shape[0] // NSA_BLOCK
    o_ref[0] = jnp.sum(x.reshape(nb, NSA_BLOCK, x.shape[1]), axis=1) * (1.0 / NSA_BLOCK)


def _block_means(kv4, *, tb):
    bsz, t, _ = kv4.shape
    w = 2 * NSA_KV_GROUPS * NSA_DH
    return pl.pallas_call(
        _blockmean_kernel,
        grid=(bsz, t // tb),
        in_specs=[pl.BlockSpec((1, tb, w), lambda b, i: (b, i, 0))],
        out_specs=pl.BlockSpec((1, tb // NSA_BLOCK, w), lambda b, i: (b, i, 0)),
        out_shape=jax.ShapeDtypeStruct((bsz, t // NSA_BLOCK, w), f32),
        compiler_params=_cparams(("parallel", "parallel")),
        name="nsa_block_means",
    )(kv4)


def _topn_mask(imp, n_sel):
    lane = lax.broadcasted_iota(i32, imp.shape, 1)
    sel = jnp.zeros(imp.shape, jnp.bool_)
    for _ in range(n_sel):
        mx = jnp.max(imp, axis=-1, keepdims=True)
        first = jnp.min(jnp.where(imp == mx, lane, imp.shape[1]), axis=-1, keepdims=True)
        hit = lane == first
        sel = sel | hit
        imp = jnp.where(hit, -jnp.inf, imp)
    return sel


def _nsa_prompt_kernel(q_ref, misc_ref, kcvc_ref, kaug_ref, v2_ref, kw_ref, vw_ref,
                       tabs_ref, tabw_ref, tabc_ref, c31_ref, o_ref):
    g = pl.program_id(1)
    i = pl.program_id(2)
    qb, dh, hpg = NSA_QBLOCK, NSA_DH, NSA_HPG
    rows = hpg * qb
    is_g0 = g == 0

    def ghalf(a):
        return jnp.where(is_g0, a[:, :dh], a[:, dh:])

    qt = q_ref[0] * (dh ** -0.5)
    qs = jnp.concatenate([qt[:, h * dh:(h + 1) * dh] for h in range(hpg)], axis=0)
    zero = jnp.zeros_like(qs)
    q2 = jnp.concatenate([jnp.where(is_g0, qs, zero), jnp.where(is_g0, zero, qs)], axis=1)

    kcvc = kcvc_ref[0]
    nblk = kcvc.shape[0]
    kc = ghalf(kcvc[:, :2 * dh])
    vc = ghalf(kcvc[:, 2 * dh:])
    s_c = _mm3_nt(qs, kc)
    lane = lax.broadcasted_iota(i32, (rows, nblk), 1)
    qrow = lax.broadcasted_iota(i32, (rows, nblk), 0) & (qb - 1)
    shift = (2 * i - 2 * NEAR_PAIRS + 1 + 4 * nblk) % nblk
    bias_c = pltpu.roll(tabc_ref[0], shift, 1)
    bias_c = jnp.where(lane < 2 * i - 2 * NEAR_PAIRS + 1, c31_ref[0], bias_c)
    readable = lane * NSA_BLOCK + (NSA_BLOCK - 1) <= i * qb + qrow
    s_c = jnp.where(readable, s_c + bias_c, NEG)
    m_c = jnp.max(s_c, axis=-1, keepdims=True)
    p_c = jnp.where(readable, jnp.exp(s_c - m_c), 0.0)
    p_c = p_c / jnp.maximum(jnp.sum(p_c, axis=-1, keepdims=True), 1e-30)
    o_c = _mm(p_c, vc)

    imp = p_c[0:qb]
    for h in range(1, hpg):
        imp = imp + p_c[h * qb:(h + 1) * qb]
    blk = lax.broadcasted_iota(i32, (qb, nblk), 1)
    cur = (i * qb + lax.broadcasted_iota(i32, (qb, nblk), 0)) >> int(math.log2(NSA_BLOCK))
    valid = blk <= cur
    forced = valid & ((blk == 0) | (blk > cur - NSA_LOCAL))
    imp = jnp.where(forced, -jnp.inf, jnp.where(valid, imp, -1.0))
    sel = (forced | _topn_mask(imp, min(NSA_TOPN, nblk) - 1 - NSA_LOCAL)) & valid
    msel = jnp.where(sel, 0.0, NEG).astype(bf16)
    qaug = jnp.concatenate([q2.astype(bf16), jnp.concatenate([msel] * hpg, axis=0)], axis=1)

    t_d = i // 2
    tk = 2 * PAIR

    def tile(k, carry):
        m, l, acc = carry
        t = t_d - k
        ks = pl.multiple_of(t * tk, tk)
        s = lax.dot_general(qaug, kaug_ref[0, pl.ds(ks, tk), :], (((1,), (1,)), ((), ())),
                            preferred_element_type=f32)
        i0 = jnp.clip(i - 2 * t, 0, NEAR_PAIRS)
        i1 = jnp.clip(i - 2 * t - 1, 0, NEAR_PAIRS)
        s = s + jnp.concatenate([tabs_ref[0, i0], tabs_ref[0, i1]], axis=1)
        m_new = jnp.maximum(m, jnp.max(s, axis=-1, keepdims=True))
        alpha = jnp.exp(m - m_new)
        p = jnp.exp(s - m_new)
        l = alpha * l + jnp.sum(p, axis=-1, keepdims=True)
        acc = alpha * acc + jnp.dot(p.astype(bf16), v2_ref[0, pl.ds(ks, tk), :], preferred_element_type=f32)
        return m_new, l, acc

    m0 = jnp.full((rows, 1), NEG, f32)
    l0 = jnp.zeros((rows, 1), f32)
    a0 = jnp.zeros((rows, 2 * dh), f32)
    _, l_s, acc_s = lax.fori_loop(0, t_d + 1, tile, (m0, l0, a0))
    o_s = ghalf(acc_s) / jnp.maximum(l_s, 1e-30)

    span = NSA_WINDOW + qb
    ws = pl.multiple_of(i * qb, qb)
    s_w = lax.dot_general(q2.astype(bf16), kw_ref[0, pl.ds(ws, span), :], (((1,), (1,)), ((), ())),
                          preferred_element_type=f32) + tabw_ref[0]
    kpos = lax.broadcasted_iota(i32, (rows, span), 1) + (i * qb - NSA_WINDOW)
    s_w = jnp.where(kpos >= 0, s_w, NEG)
    m_w = jnp.max(s_w, axis=-1, keepdims=True)
    p_w = jnp.exp(s_w - m_w)
    l_w = jnp.sum(p_w, axis=-1, keepdims=True)
    o_w = ghalf(jnp.dot(p_w.astype(bf16), vw_ref[0, pl.ds(ws, span), :], preferred_element_type=f32))
    o_w = o_w / jnp.maximum(l_w, 1e-30)

    gates = jax.nn.sigmoid(misc_ref[0])
    outs = []
    for h in range(hpg):
        r = slice(h * qb, (h + 1) * qb)
        acc = None
        for jb, branch in enumerate((o_c, o_s, o_w)):
            l0_ = 8 + 3 * h + jb
            l1_ = 8 + 3 * (hpg + h) + jb
            gcol = jnp.where(is_g0, gates[:, l0_:l0_ + 1], gates[:, l1_:l1_ + 1])
            term = gcol * branch[r]
            acc = term if acc is None else acc + term
        outs.append(acc)
    o_ref[0] = jnp.concatenate(outs, axis=1)


def _nsa_tables(rel_bias):
    qb, hpg = NSA_QBLOCK, NSA_HPG
    rows = hpg * qb
    q = (np.arange(rows) % qb)[:, None]
    c = np.arange(PAIR)[None, :]
    d_sel = [PAIR * idx + q - c for idx in range(NEAR_PAIRS + 1)]
    assert PAIR * NEAR_PAIRS - (PAIR - 1) >= FAR_DIST
    mm = 2 * NEAR_PAIRS - 1 - c
    d_cmp = np.maximum(np.where(mm >= -1, NSA_BLOCK * mm + q - (NSA_BLOCK - 1), 0), 0)
    assert NSA_BLOCK * (2 * NEAR_PAIRS) - (NSA_BLOCK - 1) >= FAR_DIST
    cw = np.arange(NSA_WINDOW + qb)[None, :]
    d_win = q + NSA_WINDOW - cw
    d_win = np.where((d_win >= 0) & (d_win < NSA_WINDOW), d_win, -1)
    n_win = (NSA_WINDOW + qb) // PAIR
    tiles = d_sel + [d_cmp] + [d_win[:, k * PAIR:(k + 1) * PAIR] for k in range(n_win)]
    dist = np.broadcast_to(np.stack(tiles)[None], (NSA_KV_GROUPS, len(tiles), rows, PAIR))
    tab = _bias_table(rel_bias, jnp.asarray(dist, i32), slab=qb)
    ns = NEAR_PAIRS + 1
    tabs = tab[:, :ns]
    tabc = tab[:, ns]
    tabw = jnp.concatenate([tab[:, ns + 1 + k] for k in range(n_win)], axis=-1)
    c31 = jnp.repeat(rel_bias.astype(f32)[REL_BUCKETS - 1].reshape(NSA_KV_GROUPS, hpg), qb, axis=1)
    return tabs, tabw, tabc, c31.reshape(NSA_KV_GROUPS, rows, 1)


def _nsa_prompt(q_b, misc, kv4, kvwin, rel_bias):
    bsz, t, _ = q_b.shape
    dh, qb = NSA_DH, NSA_QBLOCK
    nblk = t // NSA_BLOCK
    assert nblk == PAIR and t % (2 * PAIR) == 0
    kcvc = _block_means(kv4, tb=512)
    onehot = (jnp.arange(t, dtype=i32)[:, None] // NSA_BLOCK == jnp.arange(nblk, dtype=i32)[None, :]).astype(bf16)
    kaug = jnp.concatenate([kv4[:, :, 4 * dh:6 * dh].astype(bf16),
                            jnp.broadcast_to(onehot[None], (bsz, t, nblk))], axis=-1)
    v2 = kv4[:, :, 6 * dh:8 * dh].astype(bf16)
    pad = ((0, 0), (NSA_WINDOW, 0), (0, 0))
    kw = jnp.pad(kvwin[:, :, :2 * dh].astype(bf16), pad)
    vw = jnp.pad(kvwin[:, :, 2 * dh:].astype(bf16), pad)
    tabs, tabw, tabc, c31 = _nsa_tables(rel_bias)
    rows = NSA_HPG * qb
    gw = NSA_HPG * dh
    span = NSA_WINDOW + qb
    return pl.pallas_call(
        _nsa_prompt_kernel,
        grid=(bsz, NSA_KV_GROUPS, t // qb),
        in_specs=[pl.BlockSpec((1, qb, gw), lambda b, g, i: (b, i, g)),
                  pl.BlockSpec((1, qb, MISC_W), lambda b, g, i: (b, i, 0)),
                  pl.BlockSpec((1, nblk, 4 * dh), lambda b, g, i: (b, 0, 0)),
                  pl.BlockSpec((1, t, 2 * dh + nblk), lambda b, g, i: (b, 0, 0)),
                  pl.BlockSpec((1, t, 2 * dh), lambda b, g, i: (b, 0, 0)),
                  pl.BlockSpec((1, t + NSA_WINDOW, 2 * dh), lambda b, g, i: (b, 0, 0)),
                  pl.BlockSpec((1, t + NSA_WINDOW, 2 * dh), lambda b, g, i: (b, 0, 0)),
                  pl.BlockSpec((1, NEAR_PAIRS + 1, rows, PAIR), lambda b, g, i: (g, 0, 0, 0)),
                  pl.BlockSpec((1, rows, span), lambda b, g, i: (g, 0, 0)),
                  pl.BlockSpec((1, rows, PAIR), lambda b, g, i: (g, 0, 0)),
                  pl.BlockSpec((1, rows, 1), lambda b, g, i: (g, 0, 0))],
        out_specs=pl.BlockSpec((1, qb, gw), lambda b, g, i: (b, i, g)),
        out_shape=jax.ShapeDtypeStruct((bsz, t, NSA_HEADS * dh), f32),
        compiler_params=_cparams(("parallel", "parallel", "arbitrary")),
        name="nsa_prompt",
    )(q_b, misc, kcvc, kaug, v2, kw, vw, tabs, tabw, tabc, c31)


MEANS_PAGES = 8


def _nsa_s_means_kernel(pt_ref, *refs):
    del pt_ref
    x_refs, o_ref = refs[:-1], refs[-1]
    rows = [[[] for _ in range(NSA_KV_GROUPS)] for _ in range(2)]
    for x_ref in x_refs:
        page = x_ref.shape[1]
        for j in range(2):
            for g in range(NSA_KV_GROUPS):
                x = x_ref[0, :, j, g, :]
                for n in range(page // NSA_BLOCK):
                    rows[j][g].append(jnp.sum(x[n * NSA_BLOCK:(n + 1) * NSA_BLOCK], axis=0, keepdims=True)
                                      * (1.0 / NSA_BLOCK))
    for j in range(2):
        for g in range(NSA_KV_GROUPS):
            o_ref[0, j * NSA_KV_GROUPS + g] = jnp.concatenate(rows[j][g], axis=0)


def _nsa_s_means(cache_kv, page_table):
    db, n_pages = page_table.shape
    page = cache_kv.shape[1]
    nblk = n_pages * page // NSA_BLOCK
    kp = MEANS_PAGES
    rows = kp * page // NSA_BLOCK
    assert n_pages % kp == 0 and rows % 8 == 0

    def page_spec(k):
        return pl.BlockSpec((1, page, 2, NSA_KV_GROUPS, NSA_DH),
                            lambda b, p, pt: (pt[b * n_pages + p * kp + k], 0, 0, 0, 0))

    return pl.pallas_call(
        _nsa_s_means_kernel,
        grid_spec=pltpu.PrefetchScalarGridSpec(
            num_scalar_prefetch=1,
            grid=(db, n_pages // kp),
            in_specs=[page_spec(k) for k in range(kp)],
            out_specs=pl.BlockSpec((1, 2 * NSA_KV_GROUPS, rows, NSA_DH), lambda b, p, pt: (b, 0, p, 0))),
        out_shape=jax.ShapeDtypeStruct((db, 2 * NSA_KV_GROUPS, nblk, NSA_DH), f32),
        compiler_params=_cparams(("parallel", "parallel")),
        name="nsa_sample_means",
    )(page_table.reshape(-1), *([cache_kv] * kp))


def _nsa_s_scores_kernel(q_ref, kcvc_ref, win_ref, kvn_ref, tc_ref, tw_ref, b0_ref, oc_ref, ow_ref, sel_ref):
    dh, hpg = NSA_DH, NSA_HPG
    q = q_ref[0] * (dh ** -0.5)
    nblk = kcvc_ref.shape[2]
    kvn = kvn_ref[0]
    row = lax.broadcasted_iota(i32, (NSA_HEADS, 1), 0)
    lane = lax.broadcasted_iota(i32, (1, nblk), 1)
    n_sel = NSA_TOPN - 1
    for g in range(NSA_KV_GROUPS):
        in_g = (row >= g * hpg) & (row < (g + 1) * hpg)
        kc = kcvc_ref[0, g]
        vc = kcvc_ref[0, NSA_KV_GROUPS + g]
        s_c = _mm3_nt(q, kc) + tc_ref[...]
        m_c = jnp.max(s_c, axis=-1, keepdims=True)
        p_c = jnp.exp(s_c - m_c)
        p_c = p_c / jnp.maximum(jnp.sum(p_c, axis=-1, keepdims=True), 1e-30)
        o_c = _mm(p_c, vc)
        imp = jnp.sum(jnp.where(in_g, p_c, 0.0), axis=0, keepdims=True)
        forced_blocks = [0] + [nblk - k for k in range(1, NSA_LOCAL)]
        imp = jnp.where((lane == 0) | (lane > nblk - NSA_LOCAL), -jnp.inf, imp)
        picks = jnp.zeros((1, nblk), i32)
        for it, fb in enumerate(forced_blocks):
            picks = jnp.where(lane == it, fb, picks)
        for it in range(len(forced_blocks), n_sel):
            mx = jnp.max(imp, axis=-1, keepdims=True)
            first = jnp.min(jnp.where(imp == mx, lane, nblk), axis=-1, keepdims=True)
            picks = jnp.where(lane == it, first, picks)
            imp = jnp.where(lane == first, -jnp.inf, imp)
        sel_ref[0, g:g + 1, :] = picks
        kw = win_ref[0, :, 0, g, :]
        vw = win_ref[0, :, 1, g, :]
        s_w = _mm3_nt(q, kw) + tw_ref[...]
        s_n = jnp.sum(q * kvn[:, g * dh:(g + 1) * dh], axis=-1, keepdims=True) + b0_ref[...]
        m_w = jnp.maximum(jnp.max(s_w, axis=-1, keepdims=True), s_n)
        p_w = jnp.exp(s_w - m_w)
        p_n = jnp.exp(s_n - m_w)
        l_w = jnp.sum(p_w, axis=-1, keepdims=True) + p_n
        o_w = (_mm(p_w, vw) + p_n * kvn[:, (NSA_KV_GROUPS + g) * dh:(NSA_KV_GROUPS + g + 1) * dh]) / jnp.maximum(l_w, 1e-30)
        if g == 0:
            oc_ref[0] = o_c
            ow_ref[0] = o_w
        else:
            oc_ref[0] = jnp.where(in_g, o_c, oc_ref[0])
            ow_ref[0] = jnp.where(in_g, o_w, ow_ref[0])


def _nsa_s_sel_kernel(sel_ref, pt_ref, q_ref, k_ref, v_ref, tb_ref, kvn_ref, b0_ref, oc_ref, ow_ref, gt_ref,
                      o_ref, m_scr, l_scr, acc_scr):
    del sel_ref, pt_ref
    g = pl.program_id(1)
    j = pl.program_id(2)
    dh, hpg = NSA_DH, NSA_HPG
    is_g0 = g == 0
    q = q_ref[0] * (dh ** -0.5)
    kvn = kvn_ref[0]

    @pl.when(j == 0)
    def _():
        k_n = jnp.where(is_g0, kvn[:, 4 * dh:5 * dh], kvn[:, 5 * dh:6 * dh])
        v_n = jnp.where(is_g0, kvn[:, 6 * dh:7 * dh], kvn[:, 7 * dh:8 * dh])
        m_scr[...] = jnp.sum(q * k_n, axis=-1, keepdims=True) + b0_ref[...]
        l_scr[...] = jnp.ones_like(l_scr)
        acc_scr[...] = jnp.broadcast_to(v_n, acc_scr.shape)

    k = jnp.where(is_g0, k_ref[0, :, 0, 0, :], k_ref[0, :, 0, 1, :])
    v = jnp.where(is_g0, v_ref[0, :, 0, 0, :], v_ref[0, :, 0, 1, :])
    s = _mm3_nt(q, k) + tb_ref[0]
    m_old = m_scr[...]
    m_new = jnp.maximum(m_old, jnp.max(s, axis=-1, keepdims=True))
    alpha = jnp.exp(m_old - m_new)
    p = jnp.exp(s - m_new)
    l_scr[...] = alpha * l_scr[...] + jnp.sum(p, axis=-1, keepdims=True)
    acc_scr[...] = alpha * acc_scr[...] + _mm(p, v)
    m_scr[...] = m_new

    @pl.when(j == pl.num_programs(2) - 1)
    def _():
        o_s = acc_scr[...] / jnp.maximum(l_scr[...], 1e-30)
        gt = gt_ref[0]
        o = gt[:, 0:1] * oc_ref[0] + gt[:, 1:2] * o_s + gt[:, 2:3] * ow_ref[0]
        row = lax.broadcasted_iota(i32, o.shape, 0)
        in_g = (row >= g * hpg) & (row < (g + 1) * hpg)

        @pl.when(is_g0)
        def _():
            o_ref[0] = o

        @pl.when(jnp.logical_not(is_g0))
        def _():
            o_ref[0] = jnp.where(in_g, o, o_ref[0])


def _nsa_sample(q_b, gates, kv4_new, kvwin_new, cache_kv, cache_win, page_table, rel_bias):
    db = q_b.shape[0]
    dh = NSA_DH
    n_pool, page = cache_kv.shape[:2]
    n_pages = page_table.shape[1]
    past = n_pages * page
    nblk = past // NSA_BLOCK
    wb = cache_win.shape[1]
    assert nblk == PAIR and page % NSA_BLOCK == 0 and wb == NSA_WINDOW
    kcvc = _nsa_s_means(cache_kv, page_table)

    n = np.arange(nblk)
    d_cmp = past - (n * NSA_BLOCK + NSA_BLOCK - 1)
    jw = np.arange(wb)
    d_win = np.where(jw >= 1, wb - jw, -1)
    pos = np.arange(nblk * NSA_BLOCK)
    d_sel = past - pos
    width = nblk + wb + nblk * NSA_BLOCK
    dist = np.broadcast_to(np.concatenate([d_cmp, d_win, d_sel])[None, :], (NSA_HEADS, width))
    tab = _bias_table(rel_bias, jnp.asarray(dist[None, None], i32), slab=1)[0, 0]
    t_cmp, t_win = tab[:, :nblk], tab[:, nblk:nblk + wb]
    t_sel = jnp.transpose(tab[:, nblk + wb:].reshape(NSA_HEADS, nblk, NSA_BLOCK), (1, 0, 2))
    b0 = rel_bias.astype(f32)[0].reshape(NSA_HEADS, 1)

    q3 = q_b.reshape(db, NSA_HEADS, dh)
    o_c, o_w, sel = pl.pallas_call(
        _nsa_s_scores_kernel,
        grid=(db,),
        in_specs=[pl.BlockSpec((1, NSA_HEADS, dh), lambda b: (b, 0, 0)),
                  pl.BlockSpec((1, 2 * NSA_KV_GROUPS, nblk, dh), lambda b: (b, 0, 0, 0)),
                  pl.BlockSpec((1, wb, 2, NSA_KV_GROUPS, dh), lambda b: (b, 0, 0, 0, 0)),
                  pl.BlockSpec((1, 1, 4 * dh), lambda b: (b, 0, 0)),
                  pl.BlockSpec((NSA_HEADS, nblk), lambda b: (0, 0)),
                  pl.BlockSpec((NSA_HEADS, wb), lambda b: (0, 0)),
                  pl.BlockSpec((NSA_HEADS, 1), lambda b: (0, 0))],
        out_specs=[pl.BlockSpec((1, NSA_HEADS, dh), lambda b: (b, 0, 0)),
                   pl.BlockSpec((1, NSA_HEADS, dh), lambda b: (b, 0, 0)),
                   pl.BlockSpec((1, NSA_KV_GROUPS, nblk), lambda b: (b, 0, 0))],
        out_shape=[jax.ShapeDtypeStruct((db, NSA_HEADS, dh), f32),
                   jax.ShapeDtypeStruct((db, NSA_HEADS, dh), f32),
                   jax.ShapeDtypeStruct((db, NSA_KV_GROUPS, nblk), i32)],
        compiler_params=_cparams(("parallel",)),
        name="nsa_sample_scores",
    )(q3, kcvc, cache_win.astype(f32), kvwin_new.reshape(db, 1, 4 * dh), t_cmp, t_win, b0)

    n_sel = NSA_TOPN - 1
    sel_flat = sel[:, :, :n_sel].reshape(-1)
    halves = page // NSA_BLOCK
    cache_h = cache_kv.reshape(n_pool * halves, NSA_BLOCK, 4, NSA_KV_GROUPS, dh)

    def blk_of(b, g, j, sel_r, pt_r):
        nb = sel_r[(b * NSA_KV_GROUPS + g) * n_sel + j]
        return nb, pt_r[b * n_pages + nb // halves] * halves + nb % halves

    o = pl.pallas_call(
        _nsa_s_sel_kernel,
        grid_spec=pltpu.PrefetchScalarGridSpec(
            num_scalar_prefetch=2,
            grid=(db, NSA_KV_GROUPS, n_sel),
            in_specs=[pl.BlockSpec((1, NSA_HEADS, dh), lambda b, g, j, s, p: (b, 0, 0)),
                      pl.BlockSpec((1, NSA_BLOCK, 1, NSA_KV_GROUPS, dh),
                                   lambda b, g, j, s, p: (blk_of(b, g, j, s, p)[1], 0, 2, 0, 0)),
                      pl.BlockSpec((1, NSA_BLOCK, 1, NSA_KV_GROUPS, dh),
                                   lambda b, g, j, s, p: (blk_of(b, g, j, s, p)[1], 0, 3, 0, 0)),
                      pl.BlockSpec((1, NSA_HEADS, NSA_BLOCK), lambda b, g, j, s, p: (blk_of(b, g, j, s, p)[0], 0, 0)),
                      pl.BlockSpec((1, 1, 8 * dh), lambda b, g, j, s, p: (b, 0, 0)),
                      pl.BlockSpec((NSA_HEADS, 1), lambda b, g, j, s, p: (0, 0)),
                      pl.BlockSpec((1, NSA_HEADS, dh), lambda b, g, j, s, p: (b, 0, 0)),
                      pl.BlockSpec((1, NSA_HEADS, dh), lambda b, g, j, s, p: (b, 0, 0)),
                      pl.BlockSpec((1, NSA_HEADS, 3), lambda b, g, j, s, p: (b, 0, 0))],
            out_specs=pl.BlockSpec((1, NSA_HEADS, dh), lambda b, g, j, s, p: (b, 0, 0)),
            scratch_shapes=[pltpu.VMEM((NSA_HEADS, 1), f32), pltpu.VMEM((NSA_HEADS, 1), f32),
                            pltpu.VMEM((NSA_HEADS, dh), f32)]),
        out_shape=jax.ShapeDtypeStruct((db, NSA_HEADS, dh), f32),
        compiler_params=_cparams(("parallel", "arbitrary", "arbitrary")),
        name="nsa_sample_selected",
    )(sel_flat, page_table.reshape(-1), q3, cache_h, cache_h, t_sel, kv4_new.reshape(db, 1, 8 * dh), b0,
      o_c, o_w, gates)
    return o.reshape(db, NSA_HEADS * dh)


def _gelu_tanh(x):
    return 0.5 * x * (1.0 + jnp.tanh(math.sqrt(2.0 / math.pi) * (x + 0.044715 * (x * x * x))))


def _lru_gates(xc, wa_ref, wx_ref, ba_ref, bx_ref, lam_ref, prec):
    r_parts, i_parts = [], []
    for n in range(RNN_BLOCKS):
        xb = xc[:, n * RNN_BW:(n + 1) * RNN_BW]
        r_parts.append(_dotp(xb, wa_ref[n], prec))
        i_parts.append(_dotp(xb, wx_ref[n], prec))
    r = jax.nn.sigmoid(jnp.concatenate(r_parts, axis=1) + ba_ref[...])
    i = jax.nn.sigmoid(jnp.concatenate(i_parts, axis=1) + bx_ref[...])
    log_a = -RG_C * r * _softplus(-lam_ref[...])
    a = jnp.exp(log_a)
    t = jnp.tanh(log_a)
    b = jnp.sqrt(jnp.maximum(-2.0 * t / (1.0 - t), 0.0)) * (i * xc)
    return a, b


def _lru_kernel(rec_ref, gate_ref, cw_ref, cb_ref, wa_ref, wx_ref, ba_ref, bx_ref, lam_ref, cinit_ref, h0_ref,
                y_ref, hfin_ref, xc_scr, a_scr, b_scr, hs_scr, h_scr, *, tb):
    j = pl.program_id(1)

    @pl.when(j == 0)
    def _():
        xc_scr[0:8, :] = cinit_ref[0]
        h_scr[...] = h0_ref[0]

    x = rec_ref[0]
    xc_scr[8:8 + tb, :] = x
    xc = xc_scr[5:5 + tb, :] * cw_ref[0:1, :]
    xc = xc + xc_scr[6:6 + tb, :] * cw_ref[1:2, :]
    xc = xc + xc_scr[7:7 + tb, :] * cw_ref[2:3, :]
    xc = xc + x * cw_ref[3:4, :]
    xc = xc + cb_ref[...]
    xc_scr[0:8, :] = xc_scr[tb:tb + 8, :]
    a, b = _lru_gates(xc, wa_ref, wx_ref, ba_ref, bx_ref, lam_ref, 1)
    a_scr[...] = a
    b_scr[...] = b

    def step(t, h):
        h = a_scr[pl.ds(t, 1), :] * h + b_scr[pl.ds(t, 1), :]
        hs_scr[pl.ds(t, 1), :] = h
        return h

    h = lax.fori_loop(0, tb, step, h_scr[...], unroll=8)
    h_scr[...] = h
    y_ref[0] = _gelu_tanh(gate_ref[0]) * hs_scr[...]

    @pl.when(j == pl.num_programs(1) - 1)
    def _():
        hfin_ref[0] = h


def _lru_prompt(rec, gate, conv_w, conv_b, wa, wx, ba, bx, lam, *, tb):
    bsz, t, w = rec.shape
    row = lambda a: a.reshape(1, w).astype(f32)
    cinit = jnp.zeros((bsz, 8, w), f32)
    h0 = jnp.zeros((bsz, 1, w), f32)
    full = lambda shape: pl.BlockSpec(shape, lambda b, j: (0,) * len(shape))
    return pl.pallas_call(
        functools.partial(_lru_kernel, tb=tb),
        grid=(bsz, t // tb),
        in_specs=[pl.BlockSpec((1, tb, w), lambda b, j: (b, j, 0)),
                  pl.BlockSpec((1, tb, w), lambda b, j: (b, j, 0)),
                  full((RNN_CONV, w)), full((1, w)),
                  full((RNN_BLOCKS, RNN_BW, RNN_BW)), full((RNN_BLOCKS, RNN_BW, RNN_BW)),
                  full((1, w)), full((1, w)), full((1, w)),
                  pl.BlockSpec((1, 8, w), lambda b, j: (b, 0, 0)),
                  pl.BlockSpec((1, 1, w), lambda b, j: (b, 0, 0))],
        out_specs=[pl.BlockSpec((1, tb, w), lambda b, j: (b, j, 0)),
                   pl.BlockSpec((1, 1, w), lambda b, j: (b, 0, 0))],
        out_shape=[jax.ShapeDtypeStruct((bsz, t, w), f32), jax.ShapeDtypeStruct((bsz, 1, w), f32)],
        scratch_shapes=[pltpu.VMEM((tb + 8, w), f32), pltpu.VMEM((tb, w), f32), pltpu.VMEM((tb, w), f32),
                        pltpu.VMEM((tb, w), f32), pltpu.VMEM((1, w), f32)],
        compiler_params=_cparams(("parallel", "arbitrary")),
        name="rglru",
    )(rec, gate, conv_w.astype(f32), row(conv_b), wa.astype(f32), wx.astype(f32), row(ba), row(bx), row(lam),
      cinit, h0)


def _lru_step_kernel(rec_ref, gate_ref, b0_ref, b1_ref, b2_ref, cw_ref, cb_ref, wa_ref, wx_ref, ba_ref, bx_ref,
                     lam_ref, h0_ref, y_ref, h_ref):
    xc = b0_ref[...] * cw_ref[0:1, :]
    xc = xc + b1_ref[...] * cw_ref[1:2, :]
    xc = xc + b2_ref[...] * cw_ref[2:3, :]
    xc = xc + rec_ref[...] * cw_ref[3:4, :]
    xc = xc + cb_ref[...]
    a, b = _lru_gates(xc, wa_ref, wx_ref, ba_ref, bx_ref, lam_ref, 3)
    h = a * h0_ref[...] + b
    h_ref[...] = h
    y_ref[...] = _gelu_tanh(gate_ref[...]) * h


def _lru_sample(rec, gate, conv_buf, h0, conv_w, conv_b, wa, wx, ba, bx, lam):
    db, w = rec.shape
    row = lambda a: a.reshape(1, w).astype(f32)
    buf = conv_buf.astype(f32)
    return pl.pallas_call(
        _lru_step_kernel,
        out_shape=[jax.ShapeDtypeStruct((db, w), f32), jax.ShapeDtypeStruct((db, w), f32)],
        compiler_params=pltpu.CompilerParams(vmem_limit_bytes=VMEM_LIMIT),
        name="rglru_step",
    )(rec, gate, buf[:, 0], buf[:, 1], buf[:, 2], conv_w.astype(f32), row(conv_b), wa.astype(f32), wx.astype(f32),
      row(ba), row(bx), row(lam), h0.astype(f32))


def _top2_kernel(l_ref, e_ref, g_ref):
    lg = l_ref[...]
    lane = lax.broadcasted_iota(i32, lg.shape, 1)
    lg = jnp.where(lane < N_EXPERTS, lg, -jnp.inf)
    m1 = jnp.max(lg, axis=-1, keepdims=True)
    i1 = jnp.min(jnp.where(lg == m1, lane, lg.shape[1]), axis=-1, keepdims=True)
    lg2 = jnp.where(lane == i1, -jnp.inf, lg)
    m2 = jnp.max(lg2, axis=-1, keepdims=True)
    i2 = jnp.min(jnp.where(lg2 == m2, lane, lg.shape[1]), axis=-1, keepdims=True)
    e2 = jnp.exp(m2 - m1)
    den = 1.0 + e2
    e_ref[...] = jnp.where(lane == 0, i1, jnp.where(lane == 1, i2, 0))
    g_ref[...] = jnp.where(lane == 0, 1.0 / den, jnp.where(lane == 1, e2 / den, 0.0))


def _top2(logits, *, tm):
    n, w = logits.shape
    return pl.pallas_call(
        _top2_kernel,
        grid=(n // tm,),
        in_specs=[pl.BlockSpec((tm, w), lambda i: (i, 0))],
        out_specs=[pl.BlockSpec((tm, w), lambda i: (i, 0)), pl.BlockSpec((tm, w), lambda i: (i, 0))],
        out_shape=[jax.ShapeDtypeStruct((n, w), i32), jax.ShapeDtypeStruct((n, w), f32)],
        compiler_params=_cparams(("parallel",)),
        name="moe_top2",
    )(logits)


def _moe_kernel(be_ref, nu_ref, xs_ref, w1_ref, w3_ref, w2_ref, o_ref, acc_scr):
    del be_ref
    i = pl.program_id(0)
    f = pl.program_id(1)
    last = pl.num_programs(1) - 1
    used = i < nu_ref[0]

    @pl.when(used)
    def _():
        @pl.when(f == 0)
        def _():
            acc_scr[...] = jnp.zeros_like(acc_scr)

        x = xs_ref[...]
        act = _silu(_mm(x, w1_ref[0])) * _mm(x, w3_ref[0])
        acc_scr[...] += _mm(act, w2_ref[0])

        @pl.when(f == last)
        def _():
            o_ref[...] = acc_scr[...]

    @pl.when(jnp.logical_not(used) & (f == last))
    def _():
        o_ref[...] = jnp.zeros_like(o_ref)


def _moe_experts(xs, blk_e, n_used, w1, w3, w2, *, tm, tf):
    rows, d = xs.shape
    fdim = w1.shape[2]
    nf = fdim // tf

    def wcol(i, f, be, nu):
        return (be[i], 0, jnp.where(i < nu[0], f, nf - 1))

    def wrow(i, f, be, nu):
        return (be[i], jnp.where(i < nu[0], f, nf - 1), 0)

    return pl.pallas_call(
        _moe_kernel,
        grid_spec=pltpu.PrefetchScalarGridSpec(
            num_scalar_prefetch=2,
            grid=(rows // tm, nf),
            in_specs=[pl.BlockSpec((tm, d), lambda i, f, be, nu: (i, 0)),
                      pl.BlockSpec((1, d, tf), wcol),
                      pl.BlockSpec((1, d, tf), wcol),
                      pl.BlockSpec((1, tf, d), wrow)],
            out_specs=pl.BlockSpec((tm, d), lambda i, f, be, nu: (i, 0)),
            scratch_shapes=[pltpu.VMEM((tm, d), f32)]),
        out_shape=jax.ShapeDtypeStruct((rows, d), f32),
        compiler_params=_cparams(("arbitrary", "arbitrary")),
        name="moe_experts",
    )(blk_e, n_used, xs, w1, w3, w2)


def _combine_kernel(x_ref, gate_ref, y0_ref, y1_ref, g_ref, nw_ref, o_ref):
    gw = g_ref[0]
    y = y0_ref[0] * gw[:, 0:1] + y1_ref[0] * gw[:, 1:2]
    x = x_ref[0] + gate_ref[0] * y
    o_ref[0] = x * lax.rsqrt(jnp.mean(x * x, axis=-1, keepdims=True) + EPS) * nw_ref[...]


def _moe_combine(x, gate, y0, y1, gw, norm_w, *, tm):
    bsz, t, d = x.shape
    tok = lambda w: pl.BlockSpec((1, tm, w), lambda b, i: (b, i, 0))
    return pl.pallas_call(
        _combine_kernel,
        grid=(bsz, t // tm),
        in_specs=[tok(d), _mod_spec(gate, tm), tok(d), tok(d), tok(gw.shape[-1]),
                  pl.BlockSpec((1, d), lambda b, i: (0, 0))],
        out_specs=tok(d),
        out_shape=jax.ShapeDtypeStruct((bsz, t, d), f32),
        compiler_params=_cparams(("parallel", "parallel")),
        name="moe_combine",
    )(x, gate, y0, y1, gw, norm_w.reshape(1, d).astype(f32))


MOE_TM = 512
MOE_TF = 512


def _moe_dispatch(e_idx, n_tok):
    tm = MOE_TM
    n_assign = n_tok * TOP_K
    e_flat = e_idx.reshape(n_assign)
    onehot = (e_flat[:, None] == jnp.arange(N_EXPERTS, dtype=i32)[None, :]).astype(i32)
    rank = jnp.take_along_axis(jnp.cumsum(onehot, axis=0), e_flat[:, None], axis=1)[:, 0] - 1
    counts = jnp.sum(onehot, axis=0)
    padded = (counts + tm - 1) // tm * tm
    pend = jnp.cumsum(padded)
    pstart = pend - padded
    dest = pstart[e_flat] + rank
    n_blocks = -(-n_assign // tm) + N_EXPERTS
    rows = n_blocks * tm
    tok_flat = jnp.repeat(jnp.arange(n_tok, dtype=i32), TOP_K)
    row_tok = jnp.zeros((rows,), i32).at[dest].set(tok_flat)
    n_used = (pend[-1] // tm).astype(i32)
    blk = jnp.minimum(jnp.arange(n_blocks, dtype=i32), n_used - 1) * tm
    blk_e = jnp.minimum(jnp.searchsorted(pend, blk, side='right'), N_EXPERTS - 1).astype(i32)
    return dest.reshape(n_tok, TOP_K), row_tok, blk_e, n_used.reshape(1)


def _w_in0_layout(w_in0):
    c_qkv = GDN_CONV_CH
    c_z = c_qkv + GDN_HEADS * GDN_DV
    c_ab = c_z + 2 * GDN_HEADS
    c_q = c_ab + NSA_HEADS * NSA_DH
    c_kv = c_q + 6 * NSA_KV_GROUPS * NSA_DH
    c_g = c_kv + 3 * NSA_HEADS
    assert c_g == w_in0.shape[1]
    n_misc = 2 * GDN_HEADS + 3 * NSA_HEADS
    w = jnp.concatenate([w_in0[:, :c_z], w_in0[:, c_ab:c_kv], w_in0[:, c_z:c_ab], w_in0[:, c_kv:c_g],
                         jnp.zeros((w_in0.shape[0], MISC_W - n_misc), w_in0.dtype)], axis=1)
    widths = (GDN_CONV_CH, GDN_HEADS * GDN_DV, NSA_HEADS * NSA_DH, 4 * NSA_KV_GROUPS * NSA_DH,
              2 * NSA_KV_GROUPS * NSA_DH, MISC_W)
    splits, s = [], 0
    for wd in widths:
        splits.append((s, s + wd))
        s += wd
    return w, tuple(splits)


def kernel(x_prompt, x_sample, c_prompt, c_sample, cache_nsa_kv, cache_nsa_win, state_gdn, state_gdn_conv, state_lru, state_lru_conv, page_table, rel_bias, w_ada, b_ada, norm_mix, norm_ffn, norm_final, w_in0, gdn_conv_w, gdn_a_log, gdn_dt_bias, gdn_norm_w, w_out0, ffn_w_gate, ffn_w_up, ffn_w_down, w_in1, lru_conv_w, lru_conv_b, lru_wa, lru_ba, lru_wx, lru_bx, lru_lambda, w_out1, moe_router, moe_w1, moe_w3, moe_w2):
    bsz, seq, d = x_prompt.shape
    db = x_sample.shape[0]
    assert x_sample.shape[1] == 1
    dh = NSA_DH

    n_c = bsz + db
    n_c_pad = -(-n_c // 8) * 8
    c_all = jnp.concatenate([c_prompt, c_sample, jnp.zeros((n_c_pad - n_c, d), f32)], axis=0)
    mods = _adaln(c_all, w_ada, b_ada).reshape(2, n_c_pad, N_MOD, d)
    mod_p = [[mods[l, :bsz, k].reshape(bsz, 1, d) for k in range(N_MOD)] for l in range(2)]
    mod_s = [[mods[l, bsz:n_c, k].reshape(1, db, d) for k in range(N_MOD)] for l in range(2)]

    w0, splits0 = _w_in0_layout(w_in0)
    splits1 = ((0, RNN_WIDTH), (RNN_WIDTH, 2 * RNN_WIDTH))
    router = jnp.concatenate([moe_router, jnp.zeros((d, MISC_W - N_EXPERTS), f32)], axis=1)
    bf = lambda w: w.astype(bf16)

    tm = 512
    xp = x_prompt
    qkv, z, q_b, kv4, kvwin, misc = _mod_matmul(xp, norm_mix[0], mod_p[0][0], mod_p[0][1], bf(w0), splits0,
                                                tm=tm, prec=1)
    o_a, p_gdn = _gdn(qkv, z, misc, gdn_conv_w, jnp.zeros((bsz, GDN_CONV - 1, GDN_CONV_CH), f32), gdn_a_log,
                      gdn_dt_bias, gdn_norm_w, jnp.zeros((bsz, GDN_HEADS, GDN_DK, GDN_DV), f32), tb=256, n_valid=seq)
    p_gdn_conv = qkv[:, seq - (GDN_CONV - 1):]
    o_b = _nsa_prompt(q_b, misc, kv4, kvwin, rel_bias)
    p_nsa_kv = kv4.reshape(bsz, seq, 4, NSA_KV_GROUPS, dh)
    keep = min(NSA_WINDOW, seq)
    p_nsa_win = kvwin[:, seq - keep:].reshape(bsz, keep, 2, NSA_KV_GROUPS, dh)
    xp = _proj_residual([o_a, o_b], bf(w_out0), xp, mod_p[0][2], tm=tm, prec=1)
    xp = _ffn(xp, norm_ffn[0], mod_p[0][3], mod_p[0][4], mod_p[0][5], bf(ffn_w_gate), bf(ffn_w_up), bf(ffn_w_down),
              tm=tm, tf=FFN_DIM // 2, prec=1)
    gate_br, rec_br = _mod_matmul(xp, norm_mix[1], mod_p[1][0], mod_p[1][1], bf(w_in1), splits1, tm=tm, prec=1)
    y_in, p_lru = _lru_prompt(rec_br, gate_br, lru_conv_w, lru_conv_b, lru_wa, lru_wx, lru_ba, lru_bx, lru_lambda,
                              tb=256)
    p_lru_conv = rec_br[:, seq - (RNN_CONV - 1):]
    xp = _proj_residual([y_in], bf(w_out1), xp, mod_p[1][2], tm=tm, prec=1)
    logit_p, h_p = _mod_matmul(xp, norm_ffn[1], mod_p[1][3], mod_p[1][4], router, ((0, MISC_W),), tm=tm, prec=3,
                               emit_h=bf16)

    xs = x_sample.reshape(1, db, d)
    qkv_s, z_s, q_s, kv4_s, kvwin_s, misc_s = _mod_matmul(xs, norm_mix[0], mod_s[0][0], mod_s[0][1], w0, splits0,
                                                          tm=db, prec=3)
    c = GDN_CHUNK
    tpad = lambda a: jnp.pad(a.reshape(db, 1, a.shape[-1]), ((0, 0), (0, c - 1), (0, 0)))
    o_a_s, s_gdn = _gdn(tpad(qkv_s), tpad(z_s), tpad(misc_s), gdn_conv_w, state_gdn_conv, gdn_a_log, gdn_dt_bias,
                        gdn_norm_w, state_gdn, tb=c, n_valid=1)
    o_a_s = o_a_s[:, 0].reshape(1, db, GDN_HEADS * GDN_DV)
    s_gdn_conv = jnp.concatenate([state_gdn_conv[:, 1:], qkv_s.reshape(db, 1, GDN_CONV_CH)], axis=1)
    gates_s = jax.nn.sigmoid(misc_s[0, :, 2 * GDN_HEADS:2 * GDN_HEADS + 3 * NSA_HEADS]).reshape(db, NSA_HEADS, 3)
    o_b_s = _nsa_sample(q_s[0], gates_s, kv4_s[0], kvwin_s[0], cache_nsa_kv, cache_nsa_win, page_table, rel_bias)
    s_nsa_kv = kv4_s.reshape(db, 1, 4, NSA_KV_GROUPS, dh)
    s_nsa_win = jnp.concatenate([cache_nsa_win[:, 1:],
                                 kvwin_s.reshape(db, 1, 2, NSA_KV_GROUPS, dh).astype(cache_nsa_win.dtype)], axis=1)
    xs = _proj_residual([o_a_s, o_b_s.reshape(1, db, NSA_HEADS * dh)], w_out0, xs, mod_s[0][2], tm=db, prec=3)
    xs = _ffn(xs, norm_ffn[0], mod_s[0][3], mod_s[0][4], mod_s[0][5], ffn_w_gate, ffn_w_up, ffn_w_down,
              tm=db, tf=256, prec=3)
    gate_s, rec_s = _mod_matmul(xs, norm_mix[1], mod_s[1][0], mod_s[1][1], w_in1, splits1, tm=db, prec=3)
    y_in_s, s_lru = _lru_sample(rec_s[0], gate_s[0], state_lru_conv, state_lru, lru_conv_w, lru_conv_b, lru_wa, lru_wx,
                                lru_ba, lru_bx, lru_lambda)
    s_lru_conv = jnp.concatenate([state_lru_conv[:, 1:], rec_s.reshape(db, 1, RNN_WIDTH)], axis=1)
    xs = _proj_residual([y_in_s.reshape(1, db, RNN_WIDTH)], w_out1, xs, mod_s[1][2], tm=db, prec=3)
    logit_s, h_s = _mod_matmul(xs, norm_ffn[1], mod_s[1][3], mod_s[1][4], router, ((0, MISC_W),), tm=db, prec=3,
                               emit_h=bf16)

    n_p = bsz * seq
    n_tok = n_p + db
    logits = jnp.concatenate([logit_p.reshape(n_p, MISC_W), logit_s.reshape(db, MISC_W)], axis=0)
    h_all = jnp.concatenate([h_p.reshape(n_p, d), h_s.reshape(db, d)], axis=0)
    e_idx, gw = _top2(logits, tm=max(t for t in range(8, 1025, 8) if n_tok % t == 0))
    dest, row_tok, blk_e, n_used = _moe_dispatch(e_idx[:, :TOP_K], n_tok)
    yb = _moe_experts(h_all[row_tok], blk_e, n_used, moe_w1, moe_w3, moe_w2, tm=MOE_TM, tf=MOE_TF)
    y0, y1 = yb[dest[:, 0]], yb[dest[:, 1]]
    y_prompt = _moe_combine(xp, mod_p[1][5], y0[:n_p].reshape(bsz, seq, d), y1[:n_p].reshape(bsz, seq, d),
                            gw[:n_p].reshape(bsz, seq, MISC_W), norm_final, tm=tm)
    y_sample = _moe_combine(xs, mod_s[1][5], y0[n_p:].reshape(1, db, d), y1[n_p:].reshape(1, db, d),
                            gw[n_p:].reshape(1, db, MISC_W), norm_final, tm=db)

    return (y_prompt, y_sample.reshape(db, 1, d),
            p_nsa_kv, p_nsa_win, p_gdn, p_gdn_conv, p_lru.reshape(bsz, RNN_WIDTH), p_lru_conv,
            s_nsa_kv, s_nsa_win, s_gdn, s_gdn_conv, s_lru, s_lru_conv)
```

```python
import functools
import math

import numpy as np
import jax
import jax.numpy as jnp
from jax import lax
from jax.experimental import pallas as pl
from jax.experimental.pallas import tpu as pltpu

f32 = jnp.float32
bf16 = jnp.bfloat16
i32 = jnp.int32

D_MODEL = 1024
EPS = 1e-6
N_MOD = 6
GDN_HEADS = 4
GDN_DK = 128
GDN_DV = 128
GDN_CONV = 4
GDN_CHUNK = 64
GDN_CONV_CH = GDN_HEADS * (2 * GDN_DK + GDN_DV)
NSA_HEADS = 8
NSA_KV_GROUPS = 2
NSA_HPG = NSA_HEADS // NSA_KV_GROUPS
NSA_DH = 64
NSA_BLOCK = 64
NSA_TOPN = 16
NSA_LOCAL = 2
NSA_WINDOW = 512
NSA_QBLOCK = 128
NSA_FORCE = 1.0e4
REL_BUCKETS = 32
REL_MAX_DIST = 2048
RNN_WIDTH = D_MODEL
RNN_BLOCKS = 8
RNN_BW = RNN_WIDTH // RNN_BLOCKS
RNN_CONV = 4
RG_C = 8.0
FFN_DIM = 2816
N_EXPERTS = 8
TOP_K = 2
EXPERT_DIM = 3584

NEG = -1e30
PAIR = 2 * NSA_BLOCK
NEAR_PAIRS = 13
MISC_W = 128
VMEM_LIMIT = 56 * 1024 * 1024


def _cparams(sem, vmem=VMEM_LIMIT):
    return pltpu.CompilerParams(dimension_semantics=sem, vmem_limit_bytes=vmem)


def _mm(a, b):
    return jnp.dot(a.astype(bf16), b.astype(bf16), preferred_element_type=f32)


def _mm_nt(a, b):
    return lax.dot_general(a.astype(bf16), b.astype(bf16), (((1,), (1,)), ((), ())),
                           preferred_element_type=f32)


def _split2(a):
    hi = a.astype(bf16)
    lo = (a - hi.astype(f32)).astype(bf16)
    return hi, lo


def _mm3(a, b):
    ah, al = _split2(a)
    bh, bl = _split2(b)
    return (jnp.dot(ah, bh, preferred_element_type=f32) + jnp.dot(ah, bl, preferred_element_type=f32)
            + jnp.dot(al, bh, preferred_element_type=f32))


def _mm3_nt(a, b):
    ah, al = _split2(a)
    bh, bl = _split2(b)
    dn = (((1,), (1,)), ((), ()))
    return (lax.dot_general(ah, bh, dn, preferred_element_type=f32)
            + lax.dot_general(ah, bl, dn, preferred_element_type=f32)
            + lax.dot_general(al, bh, dn, preferred_element_type=f32))


def _mm_01(m01, a):
    hi = a.astype(bf16)
    r1 = a - hi.astype(f32)
    mid = r1.astype(bf16)
    lo = (r1 - mid.astype(f32)).astype(bf16)
    return (jnp.dot(m01, hi, preferred_element_type=f32) + jnp.dot(m01, mid, preferred_element_type=f32)
            + jnp.dot(m01, lo, preferred_element_type=f32))


def _dotp(a, b, prec):
    return _mm3(a, b) if prec == 3 else _mm(a, b)


def _silu(x):
    return x * jax.nn.sigmoid(x)


def _softplus(x):
    return jnp.maximum(x, 0.0) + jnp.log1p(jnp.exp(-jnp.abs(x)))


def _modulate(x, gain, shift, scale):
    r = lax.rsqrt(jnp.mean(x * x, axis=-1, keepdims=True) + EPS)
    return x * r * gain * (1.0 + scale) + shift


def _ada_kernel(c_ref, w_ref, b_ref, o_ref):
    o_ref[0] = _mm3(_silu(c_ref[...]), w_ref[0]) + b_ref[0]


def _adaln(c_all, w_ada, b_ada):
    rows = c_all.shape[0]
    depth, d, n = w_ada.shape
    tn = 1536
    return pl.pallas_call(
        _ada_kernel,
        grid=(depth, n // tn),
        in_specs=[pl.BlockSpec((rows, d), lambda l, j: (0, 0)),
                  pl.BlockSpec((1, d, tn), lambda l, j: (l, 0, j)),
                  pl.BlockSpec((1, 1, tn), lambda l, j: (l, 0, j))],
        out_specs=pl.BlockSpec((1, rows, tn), lambda l, j: (l, 0, j)),
        out_shape=jax.ShapeDtypeStruct((depth, rows, n), f32),
        compiler_params=_cparams(("arbitrary", "arbitrary")),
        name="adaln",
    )(c_all, w_ada, b_ada.reshape(depth, 1, n))


def _mod_spec(mod, tm):
    r = mod.shape[1]
    if r == 1:
        return pl.BlockSpec((1, 1, mod.shape[2]), lambda b, i: (b, 0, 0))
    return pl.BlockSpec((1, tm, mod.shape[2]), lambda b, i: (b, i, 0))


def _modmm_kernel(x_ref, gain_ref, shift_ref, scale_ref, w_ref, *o_refs, splits, prec, emit_h):
    h = _modulate(x_ref[0], gain_ref[...], shift_ref[0], scale_ref[0])
    if emit_h:
        o_refs[-1][0] = h.astype(o_refs[-1].dtype)
    hh = _split2(h) if prec == 3 else h.astype(bf16)
    for o_ref, (a, b) in zip(o_refs, splits):
        w = w_ref[:, a:b]
        if prec == 3:
            wh, wl = _split2(w)
            acc = (jnp.dot(hh[0], wh, preferred_element_type=f32) + jnp.dot(hh[0], wl, preferred_element_type=f32)
                   + jnp.dot(hh[1], wh, preferred_element_type=f32))
        else:
            acc = jnp.dot(hh, w, preferred_element_type=f32)
        o_ref[0] = acc


def _mod_matmul(x, gain, shift, scale, w, splits, *, tm, prec, emit_h=None):
    bsz, t, d = x.shape
    out_shape = [jax.ShapeDtypeStruct((bsz, t, b - a), f32) for a, b in splits]
    out_specs = [pl.BlockSpec((1, tm, b - a), lambda bi, i: (bi, i, 0)) for a, b in splits]
    if emit_h is not None:
        out_shape.append(jax.ShapeDtypeStruct((bsz, t, d), emit_h))
        out_specs.append(pl.BlockSpec((1, tm, d), lambda bi, i: (bi, i, 0)))
    return pl.pallas_call(
        functools.partial(_modmm_kernel, splits=tuple(splits), prec=prec, emit_h=emit_h is not None),
        grid=(bsz, t // tm),
        in_specs=[pl.BlockSpec((1, tm, d), lambda bi, i: (bi, i, 0)),
                  pl.BlockSpec((1, d), lambda bi, i: (0, 0)),
                  _mod_spec(shift, tm), _mod_spec(scale, tm),
                  pl.BlockSpec(w.shape, lambda bi, i: (0, 0))],
        out_specs=out_specs,
        out_shape=out_shape,
        compiler_params=_cparams(("parallel", "parallel")),
        name="mod_matmul",
    )(x, gain.reshape(1, d), shift, scale, w)


def _projres_kernel(*refs, n_lhs, ksplits, prec, final_norm):
    lhs = refs[:n_lhs]
    w_ref, x_ref, gate_ref = refs[n_lhs:n_lhs + 3]
    o_ref = refs[-1]
    acc = None
    for l_ref, (a, b) in zip(lhs, ksplits):
        part = _dotp(l_ref[0], w_ref[a:b, :], prec)
        acc = part if acc is None else acc + part
    y = x_ref[0] + gate_ref[0] * acc
    if final_norm:
        nw_ref = refs[n_lhs + 3]
        y = y * lax.rsqrt(jnp.mean(y * y, axis=-1, keepdims=True) + EPS) * nw_ref[...]
    o_ref[0] = y


def _proj_residual(lhs_list, w, x, gate, *, tm, prec, norm_w=None):
    bsz, t, d = x.shape
    ksplits, k0 = [], 0
    for l in lhs_list:
        ksplits.append((k0, k0 + l.shape[-1]))
        k0 += l.shape[-1]
    in_specs = [pl.BlockSpec((1, tm, l.shape[-1]), lambda bi, i: (bi, i, 0)) for l in lhs_list]
    in_specs += [pl.BlockSpec(w.shape, lambda bi, i: (0, 0)),
                 pl.BlockSpec((1, tm, d), lambda bi, i: (bi, i, 0)),
                 _mod_spec(gate, tm)]
    args = list(lhs_list) + [w, x, gate]
    if norm_w is not None:
        in_specs.append(pl.BlockSpec((1, d), lambda bi, i: (0, 0)))
        args.append(norm_w.reshape(1, d))
    return pl.pallas_call(
        functools.partial(_projres_kernel, n_lhs=len(lhs_list), ksplits=tuple(ksplits), prec=prec,
                          final_norm=norm_w is not None),
        grid=(bsz, t // tm),
        in_specs=in_specs,
        out_specs=pl.BlockSpec((1, tm, d), lambda bi, i: (bi, i, 0)),
        out_shape=jax.ShapeDtypeStruct((bsz, t, d), f32),
        compiler_params=_cparams(("parallel", "parallel")),
        name="proj_residual",
    )(*args)


def _ffn_kernel(x_ref, gain_ref, shift_ref, scale_ref, gate_ref, wg_ref, wu_ref, wd_ref, o_ref,
                h_scr, acc_scr, *, prec):
    f = pl.program_id(2)

    @pl.when(f == 0)
    def _():
        h_scr[...] = _modulate(x_ref[0], gain_ref[...], shift_ref[0], scale_ref[0]).astype(h_scr.dtype)
        acc_scr[...] = jnp.zeros_like(acc_scr)

    h = h_scr[...]
    act = _silu(_dotp(h, wg_ref[...], prec)) * _dotp(h, wu_ref[...], prec)
    acc_scr[...] += _dotp(act, wd_ref[...], prec)

    @pl.when(f == pl.num_programs(2) - 1)
    def _():
        o_ref[0] = x_ref[0] + gate_ref[0] * acc_scr[...]


def _ffn(x, gain, shift, scale, gate, wg, wu, wd, *, tm, tf, prec):
    bsz, t, d = x.shape
    fdim = wg.shape[1]
    mod_specs = []
    for mod in (shift, scale, gate):
        if mod.shape[1] == 1:
            mod_specs.append(pl.BlockSpec((1, 1, d), lambda b, i, f: (b, 0, 0)))
        else:
            mod_specs.append(pl.BlockSpec((1, tm, d), lambda b, i, f: (b, i, 0)))
    return pl.pallas_call(
        functools.partial(_ffn_kernel, prec=prec),
        grid=(bsz, t // tm, fdim // tf),
        in_specs=[pl.BlockSpec((1, tm, d), lambda b, i, f: (b, i, 0)),
                  pl.BlockSpec((1, d), lambda b, i, f: (0, 0)),
                  mod_specs[0], mod_specs[1], mod_specs[2],
                  pl.BlockSpec((d, tf), lambda b, i, f: (0, f)),
                  pl.BlockSpec((d, tf), lambda b, i, f: (0, f)),
                  pl.BlockSpec((tf, d), lambda b, i, f: (f, 0))],
        out_specs=pl.BlockSpec((1, tm, d), lambda b, i, f: (b, i, 0)),
        out_shape=jax.ShapeDtypeStruct((bsz, t, d), f32),
        scratch_shapes=[pltpu.VMEM((tm, d), f32 if prec == 3 else bf16), pltpu.VMEM((tm, d), f32)],
        compiler_params=_cparams(("parallel", "parallel", "arbitrary")),
        name="ffn",
    )(x, gain.reshape(1, d), shift, scale, gate, wg, wu, wd)


def _gdn_kernel(qkv_ref, z_ref, misc_ref, cw_ref, cinit_ref, hp_ref, nw_ref, s0_ref,
                o_ref, sfin_ref, xc_scr, s_scr, *, tb, n_valid):
    j = pl.program_id(1)
    c = GDN_CHUNK
    nh = GDN_HEADS

    @pl.when(j == 0)
    def _():
        xc_scr[0:8, :] = cinit_ref[0]
        s_scr[...] = s0_ref[0]

    x = qkv_ref[0]
    xc_scr[8:8 + tb, :] = x
    y = xc_scr[5:5 + tb, :] * cw_ref[0:1, :]
    y = y + xc_scr[6:6 + tb, :] * cw_ref[1:2, :]
    y = y + xc_scr[7:7 + tb, :] * cw_ref[2:3, :]
    y = y + x * cw_ref[3:4, :]
    xc_scr[0:8, :] = xc_scr[tb:tb + 8, :]
    y = _silu(y)

    misc = misc_ref[0]
    row = lax.broadcasted_iota(i32, (tb, MISC_W), 0) + j * tb
    live = row < n_valid
    log_a = jnp.where(live, hp_ref[0:1, :] * _softplus(misc + hp_ref[1:2, :]), 0.0)
    beta = jnp.where(live, jax.nn.sigmoid(misc), 0.0)

    r_i = lax.broadcasted_iota(i32, (tb, tb), 0)
    c_i = lax.broadcasted_iota(i32, (tb, tb), 1)
    sh = int(math.log2(c))
    ltri = (((r_i >> sh) == (c_i >> sh)) & (c_i <= r_i)).astype(bf16)
    g = _mm_01(ltri, log_a)
    g_t = g.T

    ri = lax.broadcasted_iota(i32, (c, c), 0)
    ci = lax.broadcasted_iota(i32, (c, c), 1)
    incl = ci <= ri
    strict = ci < ri
    eye = (ci == ri).astype(f32)
    quad = []
    for lvl in range(int(math.log2(c))):
        quad.append(((ri >> (lvl + 1)) == (ci >> (lvl + 1))) & (((ri >> lvl) & 1) == 1) & (((ci >> lvl) & 1) == 0))

    nchunk = tb // c
    units = [(n, h) for n in range(nchunk) for h in range(nh)]

    def stack(fn):
        return jnp.stack([fn(n * c, h) for n, h in units], axis=0)

    def bmm(eq, a_, b_):
        return jnp.einsum(eq, a_.astype(bf16), b_.astype(bf16), preferred_element_type=f32)

    q = stack(lambda r0, h: y[r0:r0 + c, h * GDN_DK:(h + 1) * GDN_DK])
    k = stack(lambda r0, h: y[r0:r0 + c, (nh + h) * GDN_DK:(nh + h + 1) * GDN_DK])
    v = stack(lambda r0, h: y[r0:r0 + c, 2 * nh * GDN_DK + h * GDN_DV:2 * nh * GDN_DK + (h + 1) * GDN_DV])
    q = q * lax.rsqrt(jnp.sum(q * q, axis=-1, keepdims=True) + EPS) * (GDN_DK ** -0.5)
    k = k * lax.rsqrt(jnp.sum(k * k, axis=-1, keepdims=True) + EPS)
    g_col = stack(lambda r0, h: g[r0:r0 + c, h:h + 1])
    g_row = stack(lambda r0, h: g_t[h:h + 1, r0:r0 + c])
    b_col = stack(lambda r0, h: beta[r0:r0 + c, nh + h:nh + h + 1])
    gam = jnp.where(incl, jnp.exp(jnp.where(incl, g_col - g_row, 0.0)), 0.0)
    kk = bmm('uid,ujd->uij', k, k)
    a = jnp.where(strict, b_col * gam * kk, 0.0)
    p = eye - jnp.where(quad[0], a, 0.0)
    for lvl in range(1, len(quad)):
        m = bmm('uij,ujk->uik', jnp.where(quad[lvl], a, 0.0), p)
        p = p - bmm('uij,ujk->uik', p, m)
    e_g = jnp.exp(g_col)
    sol = bmm('uij,ujd->uid', p, jnp.concatenate([b_col * v, (b_col * e_g) * k], axis=-1))
    vb, w = sol[:, :, :GDN_DV], sol[:, :, GDN_DV:]
    aqk = bmm('uid,ujd->uij', q, k) * gam
    qg = q * e_g
    g_last = g_col[:, c - 1:c, :]
    kd_t = jnp.swapaxes(k * jnp.exp(g_last - g_col), 1, 2)
    gc = jnp.exp(g_last)

    s = s_scr[...]
    for n in range(nchunk):
        sl = slice(n * nh, (n + 1) * nh)
        u = vb[sl] - bmm('hcd,hde->hce', w[sl], s)
        o = bmm('hcd,hde->hce', qg[sl], s) + bmm('hij,hje->hie', aqk[sl], u)
        s = gc[sl] * s + bmm('hdc,hce->hde', kd_t[sl], u)
        o = o * lax.rsqrt(jnp.mean(o * o, axis=-1, keepdims=True) + EPS) * nw_ref[...]
        for h in range(nh):
            zs = z_ref[0, n * c:(n + 1) * c, h * GDN_DV:(h + 1) * GDN_DV]
            o_ref[0, n * c:(n + 1) * c, h * GDN_DV:(h + 1) * GDN_DV] = o[h] * _silu(zs)
    s_scr[...] = s

    @pl.when(j == pl.num_programs(1) - 1)
    def _():
        sfin_ref[0] = s_scr[...]


def _gdn(qkv_raw, z, misc, conv_w, conv_buf, a_log, dt_bias, norm_w, s0, *, tb, n_valid):
    bsz, tp, ch = qkv_raw.shape
    cinit = jnp.concatenate([jnp.zeros((bsz, 5, ch), f32), conv_buf.astype(f32)], axis=1)
    hp = jnp.zeros((8, MISC_W), f32)
    hp = hp.at[0, :GDN_HEADS].set(-jnp.exp(a_log.astype(f32))).at[1, :GDN_HEADS].set(dt_bias.astype(f32))
    zw = GDN_HEADS * GDN_DV
    return pl.pallas_call(
        functools.partial(_gdn_kernel, tb=tb, n_valid=n_valid),
        grid=(bsz, tp // tb),
        in_specs=[pl.BlockSpec((1, tb, ch), lambda b, j: (b, j, 0)),
                  pl.BlockSpec((1, tb, zw), lambda b, j: (b, j, 0)),
                  pl.BlockSpec((1, tb, MISC_W), lambda b, j: (b, j, 0)),
                  pl.BlockSpec((GDN_CONV, ch), lambda b, j: (0, 0)),
                  pl.BlockSpec((1, 8, ch), lambda b, j: (b, 0, 0)),
                  pl.BlockSpec((8, MISC_W), lambda b, j: (0, 0)),
                  pl.BlockSpec((1, GDN_DV), lambda b, j: (0, 0)),
                  pl.BlockSpec((1, GDN_HEADS, GDN_DK, GDN_DV), lambda b, j: (b, 0, 0, 0))],
        out_specs=[pl.BlockSpec((1, tb, zw), lambda b, j: (b, j, 0)),
                   pl.BlockSpec((1, GDN_HEADS, GDN_DK, GDN_DV), lambda b, j: (b, 0, 0, 0))],
        out_shape=[jax.ShapeDtypeStruct((bsz, tp, zw), f32),
                   jax.ShapeDtypeStruct((bsz, GDN_HEADS, GDN_DK, GDN_DV), f32)],
        scratch_shapes=[pltpu.VMEM((tb + 8, ch), f32), pltpu.VMEM((GDN_HEADS, GDN_DK, GDN_DV), f32)],
        compiler_params=_cparams(("parallel", "arbitrary")),
        name="gdn",
    )(qkv_raw, z, misc, conv_w.astype(f32), cinit, hp, norm_w.reshape(1, GDN_DV).astype(f32), s0.astype(f32))


def _bucket_thresholds():
    exact = REL_BUCKETS // 2
    d = np.arange(0, 4 * REL_MAX_DIST, dtype=np.int64)

    def buckets(ft):
        nf = np.maximum(d, exact).astype(ft)
        large = exact + (np.log(nf / ft(exact)) / ft(math.log(REL_MAX_DIST / exact)) * ft(REL_BUCKETS - exact)).astype(np.int32)
        return np.where(d < exact, d, np.minimum(large, REL_BUCKETS - 1))

    b64, b32 = buckets(np.float64), buckets(np.float32)
    assert np.array_equal(b64, b32) and b64[-1] == REL_BUCKETS - 1 and np.all(np.diff(b64) >= 0)
    return [int(np.argmax(b64 >= k)) for k in range(1, REL_BUCKETS)]


_BUCKET_THR = _bucket_thresholds()
FAR_DIST = _BUCKET_THR[-1]


def _bias_kernel(rb_ref, d_ref, o_ref, *, slab, group_heads):
    rows = d_ref.shape[-2]
    for s in range(rows // slab):
        d = d_ref[0, 0, s * slab:(s + 1) * slab, :]
        h = pl.program_id(0) * group_heads + s
        val = jnp.full(d.shape, rb_ref[0, h], f32)
        for k in range(1, REL_BUCKETS):
            val = jnp.where(d >= _BUCKET_THR[k - 1], rb_ref[k, h], val)
        o_ref[0, 0, s * slab:(s + 1) * slab, :] = jnp.where(d < 0, NEG, val)


def _bias_table(rel_bias, dist, *, slab):
    g, nt, rows, cols = dist.shape
    return pl.pallas_call(
        functools.partial(_bias_kernel, slab=slab, group_heads=rows // slab),
        grid=(g, nt),
        in_specs=[pl.BlockSpec(memory_space=pltpu.SMEM),
                  pl.BlockSpec((1, 1, rows, cols), lambda a, b: (a, b, 0, 0))],
        out_specs=pl.BlockSpec((1, 1, rows, cols), lambda a, b: (a, b, 0, 0)),
        out_shape=jax.ShapeDtypeStruct(dist.shape, f32),
        compiler_params=_cparams(("arbitrary", "arbitrary")),
        name="bias_table",
    )(rel_bias.astype(f32), dist)


def _blockmean_kernel(x_ref, o_ref):
    x = x_ref[0]
    nb = x.shape[0] // NSA_BLOCK
    o_ref[0] = jnp.sum(x.reshape(nb, NSA_BLOCK, x.shape[1]), axis=1) * (1.0 / NSA_BLOCK)


def _block_means(kv4, *, tb):
    bsz, t, _ = kv4.shape
    w = 2 * NSA_KV_GROUPS * NSA_DH
    return pl.pallas_call(
        _blockmean_kernel,
        grid=(bsz, t // tb),
        in_specs=[pl.BlockSpec((1, tb, w), lambda b, i: (b, i, 0))],
        out_specs=pl.BlockSpec((1, tb // NSA_BLOCK, w), lambda b, i: (b, i, 0)),
        out_shape=jax.ShapeDtypeStruct((bsz, t // NSA_BLOCK, w), f32),
        compiler_params=_cparams(("parallel", "parallel")),
        name="nsa_block_means",
    )(kv4)


def _topn_mask(imp, n_sel):
    lane = lax.broadcasted_iota(i32, imp.shape, 1)
    sel = jnp.zeros(imp.shape, jnp.bool_)
    for _ in range(n_sel):
        mx = jnp.max(imp, axis=-1, keepdims=True)
        first = jnp.min(jnp.where(imp == mx, lane, imp.shape[1]), axis=-1, keepdims=True)
        hit = lane == first
        sel = sel | hit
        imp = jnp.where(hit, -jnp.inf, imp)
    return sel


def _nsa_prompt_kernel(q_ref, misc_ref, kcvc_ref, kaug_ref, v2_ref, kw_ref, vw_ref,
                       tabs_ref, tabw_ref, tabc_ref, c31_ref, o_ref):
    g = pl.program_id(1)
    i = pl.program_id(2)
    qb, dh, hpg = NSA_QBLOCK, NSA_DH, NSA_HPG
    rows = hpg * qb
    is_g0 = g == 0

    def ghalf(a):
        return jnp.where(is_g0, a[:, :dh], a[:, dh:])

    qt = q_ref[0] * (dh ** -0.5)
    qs = jnp.concatenate([qt[:, h * dh:(h + 1) * dh] for h in range(hpg)], axis=0)
    zero = jnp.zeros_like(qs)
    q2 = jnp.concatenate([jnp.where(is_g0, qs, zero), jnp.where(is_g0, zero, qs)], axis=1)

    kcvc = kcvc_ref[0]
    nblk = kcvc.shape[0]
    kc = ghalf(kcvc[:, :2 * dh])
    vc = ghalf(kcvc[:, 2 * dh:])
    s_c = _mm3_nt(qs, kc)
    lane = lax.broadcasted_iota(i32, (rows, nblk), 1)
    qrow = lax.broadcasted_iota(i32, (rows, nblk), 0) & (qb - 1)
    shift = (2 * i - 2 * NEAR_PAIRS + 1 + 4 * nblk) % nblk
    bias_c = pltpu.roll(tabc_ref[0], shift, 1)
    bias_c = jnp.where(lane < 2 * i - 2 * NEAR_PAIRS + 1, c31_ref[0], bias_c)
    readable = lane * NSA_BLOCK + (NSA_BLOCK - 1) <= i * qb + qrow
    s_c = jnp.where(readable, s_c + bias_c, NEG)
    m_c = jnp.max(s_c, axis=-1, keepdims=True)
    p_c = jnp.where(readable, jnp.exp(s_c - m_c), 0.0)
    p_c = p_c / jnp.maximum(jnp.sum(p_c, axis=-1, keepdims=True), 1e-30)
    o_c = _mm(p_c, vc)

    imp = p_c[0:qb]
    for h in range(1, hpg):
        imp = imp + p_c[h * qb:(h + 1) * qb]
    blk = lax.broadcasted_iota(i32, (qb, nblk), 1)
    cur = (i * qb + lax.broadcasted_iota(i32, (qb, nblk), 0)) >> int(math.log2(NSA_BLOCK))
    valid = blk <= cur
    forced = valid & ((blk == 0) | (blk > cur - NSA_LOCAL))
    imp = jnp.where(forced, -jnp.inf, jnp.where(valid, imp, -1.0))
    sel = (forced | _topn_mask(imp, min(NSA_TOPN, nblk) - 1 - NSA_LOCAL)) & valid
    msel = jnp.where(sel, 0.0, NEG).astype(bf16)
    qaug = jnp.concatenate([q2.astype(bf16), jnp.concatenate([msel] * hpg, axis=0)], axis=1)

    t_d = i // 2
    tk = 2 * PAIR

    def tile(k, carry):
        m, l, acc = carry
        t = t_d - k
        ks = pl.multiple_of(t * tk, tk)
        s = lax.dot_general(qaug, kaug_ref[0, pl.ds(ks, tk), :], (((1,), (1,)), ((), ())),
                            preferred_element_type=f32)
        i0 = jnp.clip(i - 2 * t, 0, NEAR_PAIRS)
        i1 = jnp.clip(i - 2 * t - 1, 0, NEAR_PAIRS)
        s = s + jnp.concatenate([tabs_ref[0, i0], tabs_ref[0, i1]], axis=1)
        m_new = jnp.maximum(m, jnp.max(s, axis=-1, keepdims=True))
        alpha = jnp.exp(m - m_new)
        p = jnp.exp(s - m_new)
        l = alpha * l + jnp.sum(p, axis=-1, keepdims=True)
        acc = alpha * acc + jnp.dot(p.astype(bf16), v2_ref[0, pl.ds(ks, tk), :], preferred_element_type=f32)
        return m_new, l, acc

    m0 = jnp.full((rows, 1), NEG, f32)
    l0 = jnp.zeros((rows, 1), f32)
    a0 = jnp.zeros((rows, 2 * dh), f32)
    _, l_s, acc_s = lax.fori_loop(0, t_d + 1, tile, (m0, l0, a0))
    o_s = ghalf(acc_s) / jnp.maximum(l_s, 1e-30)

    span = NSA_WINDOW + qb
    ws = pl.multiple_of(i * qb, qb)
    s_w = lax.dot_general(q2.astype(bf16), kw_ref[0, pl.ds(ws, span), :], (((1,), (1,)), ((), ())),
                          preferred_element_type=f32) + tabw_ref[0]
    kpos = lax.broadcasted_iota(i32, (rows, span), 1) + (i * qb - NSA_WINDOW)
    s_w = jnp.where(kpos >= 0, s_w, NEG)
    m_w = jnp.max(s_w, axis=-1, keepdims=True)
    p_w = jnp.exp(s_w - m_w)
    l_w = jnp.sum(p_w, axis=-1, keepdims=True)
    o_w = ghalf(jnp.dot(p_w.astype(bf16), vw_ref[0, pl.ds(ws, span), :], preferred_element_type=f32))
    o_w = o_w / jnp.maximum(l_w, 1e-30)

    gates = jax.nn.sigmoid(misc_ref[0])
    outs = []
    for h in range(hpg):
        r = slice(h * qb, (h + 1) * qb)
        acc = None
        for jb, branch in enumerate((o_c, o_s, o_w)):
            l0_ = 8 + 3 * h + jb
            l1_ = 8 + 3 * (hpg + h) + jb
            gcol = jnp.where(is_g0, gates[:, l0_:l0_ + 1], gates[:, l1_:l1_ + 1])
            term = gcol * branch[r]
            acc = term if acc is None else acc + term
        outs.append(acc)
    o_ref[0] = jnp.concatenate(outs, axis=1)


def _nsa_tables(rel_bias):
    qb, hpg = NSA_QBLOCK, NSA_HPG
    rows = hpg * qb
    q = (np.arange(rows) % qb)[:, None]
    c = np.arange(PAIR)[None, :]
    d_sel = [PAIR * idx + q - c for idx in range(NEAR_PAIRS + 1)]
    assert PAIR * NEAR_PAIRS - (PAIR - 1) >= FAR_DIST
    mm = 2 * NEAR_PAIRS - 1 - c
    d_cmp = np.maximum(np.where(mm >= -1, NSA_BLOCK * mm + q - (NSA_BLOCK - 1), 0), 0)
    assert NSA_BLOCK * (2 * NEAR_PAIRS) - (NSA_BLOCK - 1) >= FAR_DIST
    cw = np.arange(NSA_WINDOW + qb)[None, :]
    d_win = q + NSA_WINDOW - cw
    d_win = np.where((d_win >= 0) & (d_win < NSA_WINDOW), d_win, -1)
    n_win = (NSA_WINDOW + qb) // PAIR
    tiles = d_sel + [d_cmp] + [d_win[:, k * PAIR:(k + 1) * PAIR] for k in range(n_win)]
    dist = np.broadcast_to(np.stack(tiles)[None], (NSA_KV_GROUPS, len(tiles), rows, PAIR))
    tab = _bias_table(rel_bias, jnp.asarray(dist, i32), slab=qb)
    ns = NEAR_PAIRS + 1
    tabs = tab[:, :ns]
    tabc = tab[:, ns]
    tabw = jnp.concatenate([tab[:, ns + 1 + k] for k in range(n_win)], axis=-1)
    c31 = jnp.repeat(rel_bias.astype(f32)[REL_BUCKETS - 1].reshape(NSA_KV_GROUPS, hpg), qb, axis=1)
    return tabs, tabw, tabc, c31.reshape(NSA_KV_GROUPS, rows, 1)


def _nsa_prompt(q_b, misc, kv4, kvwin, rel_bias):
    bsz, t, _ = q_b.shape
    dh, qb = NSA_DH, NSA_QBLOCK
    nblk = t // NSA_BLOCK
    assert nblk == PAIR and t % (2 * PAIR) == 0
    kcvc = _block_means(kv4, tb=512)
    onehot = (jnp.arange(t, dtype=i32)[:, None] // NSA_BLOCK == jnp.arange(nblk, dtype=i32)[None, :]).astype(bf16)
    kaug = jnp.concatenate([kv4[:, :, 4 * dh:6 * dh].astype(bf16),
                            jnp.broadcast_to(onehot[None], (bsz, t, nblk))], axis=-1)
    v2 = kv4[:, :, 6 * dh:8 * dh].astype(bf16)
    pad = ((0, 0), (NSA_WINDOW, 0), (0, 0))
    kw = jnp.pad(kvwin[:, :, :2 * dh].astype(bf16), pad)
    vw = jnp.pad(kvwin[:, :, 2 * dh:].astype(bf16), pad)
    tabs, tabw, tabc, c31 = _nsa_tables(rel_bias)
    rows = NSA_HPG * qb
    gw = NSA_HPG * dh
    span = NSA_WINDOW + qb
    return pl.pallas_call(
        _nsa_prompt_kernel,
        grid=(bsz, NSA_KV_GROUPS, t // qb),
        in_specs=[pl.BlockSpec((1, qb, gw), lambda b, g, i: (b, i, g)),
                  pl.BlockSpec((1, qb, MISC_W), lambda b, g, i: (b, i, 0)),
                  pl.BlockSpec((1, nblk, 4 * dh), lambda b, g, i: (b, 0, 0)),
                  pl.BlockSpec((1, t, 2 * dh + nblk), lambda b, g, i: (b, 0, 0)),
                  pl.BlockSpec((1, t, 2 * dh), lambda b, g, i: (b, 0, 0)),
                  pl.BlockSpec((1, t + NSA_WINDOW, 2 * dh), lambda b, g, i: (b, 0, 0)),
                  pl.BlockSpec((1, t + NSA_WINDOW, 2 * dh), lambda b, g, i: (b, 0, 0)),
                  pl.BlockSpec((1, NEAR_PAIRS + 1, rows, PAIR), lambda b, g, i: (g, 0, 0, 0)),
                  pl.BlockSpec((1, rows, span), lambda b, g, i: (g, 0, 0)),
                  pl.BlockSpec((1, rows, PAIR), lambda b, g, i: (g, 0, 0)),
                  pl.BlockSpec((1, rows, 1), lambda b, g, i: (g, 0, 0))],
        out_specs=pl.BlockSpec((1, qb, gw), lambda b, g, i: (b, i, g)),
        out_shape=jax.ShapeDtypeStruct((bsz, t, NSA_HEADS * dh), f32),
        compiler_params=_cparams(("parallel", "parallel", "arbitrary")),
        name="nsa_prompt",
    )(q_b, misc, kcvc, kaug, v2, kw, vw, tabs, tabw, tabc, c31)


MEANS_PAGES = 8


def _nsa_s_means_kernel(pt_ref, *refs):
    del pt_ref
    x_refs, o_ref = refs[:-1], refs[-1]
    rows = [[[] for _ in range(NSA_KV_GROUPS)] for _ in range(2)]
    for x_ref in x_refs:
        page = x_ref.shape[1]
        for j in range(2):
            for g in range(NSA_KV_GROUPS):
                x = x_ref[0, :, j, g, :]
                for n in range(page // NSA_BLOCK):
                    rows[j][g].append(jnp.sum(x[n * NSA_BLOCK:(n + 1) * NSA_BLOCK], axis=0, keepdims=True)
                                      * (1.0 / NSA_BLOCK))
    for j in range(2):
        for g in range(NSA_KV_GROUPS):
            o_ref[0, j * NSA_KV_GROUPS + g] = jnp.concatenate(rows[j][g], axis=0)


def _nsa_s_means(cache_kv, page_table):
    db, n_pages = page_table.shape
    page = cache_kv.shape[1]
    nblk = n_pages * page // NSA_BLOCK
    kp = MEANS_PAGES
    rows = kp * page // NSA_BLOCK
    assert n_pages % kp == 0 and rows % 8 == 0

    def page_spec(k):
        return pl.BlockSpec((1, page, 2, NSA_KV_GROUPS, NSA_DH),
                            lambda b, p, pt: (pt[b * n_pages + p * kp + k], 0, 0, 0, 0))

    return pl.pallas_call(
        _nsa_s_means_kernel,
        grid_spec=pltpu.PrefetchScalarGridSpec(
            num_scalar_prefetch=1,
            grid=(db, n_pages // kp),
            in_specs=[page_spec(k) for k in range(kp)],
            out_specs=pl.BlockSpec((1, 2 * NSA_KV_GROUPS, rows, NSA_DH), lambda b, p, pt: (b, 0, p, 0))),
        out_shape=jax.ShapeDtypeStruct((db, 2 * NSA_KV_GROUPS, nblk, NSA_DH), f32),
        compiler_params=_cparams(("parallel", "parallel")),
        name="nsa_sample_means",
    )(page_table.reshape(-1), *([cache_kv] * kp))


def _nsa_s_scores_kernel(q_ref, kcvc_ref, win_ref, kvn_ref, tc_ref, tw_ref, b0_ref, oc_ref, ow_ref, sel_ref):
    dh, hpg = NSA_DH, NSA_HPG
    q = q_ref[0] * (dh ** -0.5)
    nblk = kcvc_ref.shape[2]
    kvn = kvn_ref[0]
    row = lax.broadcasted_iota(i32, (NSA_HEADS, 1), 0)
    lane = lax.broadcasted_iota(i32, (1, nblk), 1)
    n_sel = NSA_TOPN - 1
    for g in range(NSA_KV_GROUPS):
        in_g = (row >= g * hpg) & (row < (g + 1) * hpg)
        kc = kcvc_ref[0, g]
        vc = kcvc_ref[0, NSA_KV_GROUPS + g]
        s_c = _mm3_nt(q, kc) + tc_ref[...]
        m_c = jnp.max(s_c, axis=-1, keepdims=True)
        p_c = jnp.exp(s_c - m_c)
        p_c = p_c / jnp.maximum(jnp.sum(p_c, axis=-1, keepdims=True), 1e-30)
        o_c = _mm(p_c, vc)
        imp = jnp.sum(jnp.where(in_g, p_c, 0.0), axis=0, keepdims=True)
        forced_blocks = [0] + [nblk - k for k in range(1, NSA_LOCAL)]
        imp = jnp.where((lane == 0) | (lane > nblk - NSA_LOCAL), -jnp.inf, imp)
        picks = jnp.zeros((1, nblk), i32)
        for it, fb in enumerate(forced_blocks):
            picks = jnp.where(lane == it, fb, picks)
        for it in range(len(forced_blocks), n_sel):
            mx = jnp.max(imp, axis=-1, keepdims=True)
            first = jnp.min(jnp.where(imp == mx, lane, nblk), axis=-1, keepdims=True)
            picks = jnp.where(lane == it, first, picks)
            imp = jnp.where(lane == first, -jnp.inf, imp)
        sel_ref[0, g:g + 1, :] = picks
        kw = win_ref[0, :, 0, g, :]
        vw = win_ref[0, :, 1, g, :]
        s_w = _mm3_nt(q, kw) + tw_ref[...]
        s_n = jnp.sum(q * kvn[:, g * dh:(g + 1) * dh], axis=-1, keepdims=True) + b0_ref[...]
        m_w = jnp.maximum(jnp.max(s_w, axis=-1, keepdims=True), s_n)
        p_w = jnp.exp(s_w - m_w)
        p_n = jnp.exp(s_n - m_w)
        l_w = jnp.sum(p_w, axis=-1, keepdims=True) + p_n
        o_w = (_mm(p_w, vw) + p_n * kvn[:, (NSA_KV_GROUPS + g) * dh:(NSA_KV_GROUPS + g + 1) * dh]) / jnp.maximum(l_w, 1e-30)
        if g == 0:
            oc_ref[0] = o_c
            ow_ref[0] = o_w
        else:
            oc_ref[0] = jnp.where(in_g, o_c, oc_ref[0])
            ow_ref[0] = jnp.where(in_g, o_w, ow_ref[0])


def _nsa_s_sel_kernel(sel_ref, pt_ref, q_ref, k_ref, v_ref, tb_ref, kvn_ref, b0_ref, oc_ref, ow_ref, gt_ref,
                      o_ref, m_scr, l_scr, acc_scr):
    del sel_ref, pt_ref
    g = pl.program_id(1)
    j = pl.program_id(2)
    dh, hpg = NSA_DH, NSA_HPG
    is_g0 = g == 0
    q = q_ref[0] * (dh ** -0.5)
    kvn = kvn_ref[0]

    @pl.when(j == 0)
    def _():
        k_n = jnp.where(is_g0, kvn[:, 4 * dh:5 * dh], kvn[:, 5 * dh:6 * dh])
        v_n = jnp.where(is_g0, kvn[:, 6 * dh:7 * dh], kvn[:, 7 * dh:8 * dh])
        m_scr[...] = jnp.sum(q * k_n, axis=-1, keepdims=True) + b0_ref[...]
        l_scr[...] = jnp.ones_like(l_scr)
        acc_scr[...] = jnp.broadcast_to(v_n, acc_scr.shape)

    k = jnp.where(is_g0, k_ref[0, :, 0, 0, :], k_ref[0, :, 0, 1, :])
    v = jnp.where(is_g0, v_ref[0, :, 0, 0, :], v_ref[0, :, 0, 1, :])
    s = _mm3_nt(q, k) + tb_ref[0]
    m_old = m_scr[...]
    m_new = jnp.maximum(m_old, jnp.max(s, axis=-1, keepdims=True))
    alpha = jnp.exp(m_old - m_new)
    p = jnp.exp(s - m_new)
    l_scr[...] = alpha * l_scr[...] + jnp.sum(p, axis=-1, keepdims=True)
    acc_scr[...] = alpha * acc_scr[...] + _mm(p, v)
    m_scr[...] = m_new

    @pl.when(j == pl.num_programs(2) - 1)
    def _():
        o_s = acc_scr[...] / jnp.maximum(l_scr[...], 1e-30)
        gt = gt_ref[0]
        o = gt[:, 0:1] * oc_ref[0] + gt[:, 1:2] * o_s + gt[:, 2:3] * ow_ref[0]
        row = lax.broadcasted_iota(i32, o.shape, 0)
        in_g = (row >= g * hpg) & (row < (g + 1) * hpg)

        @pl.when(is_g0)
        def _():
            o_ref[0] = o

        @pl.when(jnp.logical_not(is_g0))
        def _():
            o_ref[0] = jnp.where(in_g, o, o_ref[0])


def _nsa_sample(q_b, gates, kv4_new, kvwin_new, cache_kv, cache_win, page_table, rel_bias):
    db = q_b.shape[0]
    dh = NSA_DH
    n_pool, page = cache_kv.shape[:2]
    n_pages = page_table.shape[1]
    past = n_pages * page
    nblk = past // NSA_BLOCK
    wb = cache_win.shape[1]
    assert nblk == PAIR and page % NSA_BLOCK == 0 and wb == NSA_WINDOW
    kcvc = _nsa_s_means(cache_kv, page_table)

    n = np.arange(nblk)
    d_cmp = past - (n * NSA_BLOCK + NSA_BLOCK - 1)
    jw = np.arange(wb)
    d_win = np.where(jw >= 1, wb - jw, -1)
    pos = np.arange(nblk * NSA_BLOCK)
    d_sel = past - pos
    width = nblk + wb + nblk * NSA_BLOCK
    dist = np.broadcast_to(np.concatenate([d_cmp, d_win, d_sel])[None, :], (NSA_HEADS, width))
    tab = _bias_table(rel_bias, jnp.asarray(dist[None, None], i32), slab=1)[0, 0]
    t_cmp, t_win = tab[:, :nblk], tab[:, nblk:nblk + wb]
    t_sel = jnp.transpose(tab[:, nblk + wb:].reshape(NSA_HEADS, nblk, NSA_BLOCK), (1, 0, 2))
    b0 = rel_bias.astype(f32)[0].reshape(NSA_HEADS, 1)

    q3 = q_b.reshape(db, NSA_HEADS, dh)
    o_c, o_w, sel = pl.pallas_call(
        _nsa_s_scores_kernel,
        grid=(db,),
        in_specs=[pl.BlockSpec((1, NSA_HEADS, dh), lambda b: (b, 0, 0)),
                  pl.BlockSpec((1, 2 * NSA_KV_GROUPS, nblk, dh), lambda b: (b, 0, 0, 0)),
                  pl.BlockSpec((1, wb, 2, NSA_KV_GROUPS, dh), lambda b: (b, 0, 0, 0, 0)),
                  pl.BlockSpec((1, 1, 4 * dh), lambda b: (b, 0, 0)),
                  pl.BlockSpec((NSA_HEADS, nblk), lambda b: (0, 0)),
                  pl.BlockSpec((NSA_HEADS, wb), lambda b: (0, 0)),
                  pl.BlockSpec((NSA_HEADS, 1), lambda b: (0, 0))],
        out_specs=[pl.BlockSpec((1, NSA_HEADS, dh), lambda b: (b, 0, 0)),
                   pl.BlockSpec((1, NSA_HEADS, dh), lambda b: (b, 0, 0)),
                   pl.BlockSpec((1, NSA_KV_GROUPS, nblk), lambda b: (b, 0, 0))],
        out_shape=[jax.ShapeDtypeStruct((db, NSA_HEADS, dh), f32),
                   jax.ShapeDtypeStruct((db, NSA_HEADS, dh), f32),
                   jax.ShapeDtypeStruct((db, NSA_KV_GROUPS, nblk), i32)],
        compiler_params=_cparams(("parallel",)),
        name="nsa_sample_scores",
    )(q3, kcvc, cache_win.astype(f32), kvwin_new.reshape(db, 1, 4 * dh), t_cmp, t_win, b0)

    n_sel = NSA_TOPN - 1
    sel_flat = sel[:, :, :n_sel].reshape(-1)
    halves = page // NSA_BLOCK
    cache_h = cache_kv.reshape(n_pool * halves, NSA_BLOCK, 4, NSA_KV_GROUPS, dh)

    def blk_of(b, g, j, sel_r, pt_r):
        nb = sel_r[(b * NSA_KV_GROUPS + g) * n_sel + j]
        return nb, pt_r[b * n_pages + nb // halves] * halves + nb % halves

    o = pl.pallas_call(
        _nsa_s_sel_kernel,
        grid_spec=pltpu.PrefetchScalarGridSpec(
            num_scalar_prefetch=2,
            grid=(db, NSA_KV_GROUPS, n_sel),
            in_specs=[pl.BlockSpec((1, NSA_HEADS, dh), lambda b, g, j, s, p: (b, 0, 0)),
                      pl.BlockSpec((1, NSA_BLOCK, 1, NSA_KV_GROUPS, dh),
                                   lambda b, g, j, s, p: (blk_of(b, g, j, s, p)[1], 0, 2, 0, 0)),
                      pl.BlockSpec((1, NSA_BLOCK, 1, NSA_KV_GROUPS, dh),
                                   lambda b, g, j, s, p: (blk_of(b, g, j, s, p)[1], 0, 3, 0, 0)),
                      pl.BlockSpec((1, NSA_HEADS, NSA_BLOCK), lambda b, g, j, s, p: (blk_of(b, g, j, s, p)[0], 0, 0)),
                      pl.BlockSpec((1, 1, 8 * dh), lambda b, g, j, s, p: (b, 0, 0)),
                      pl.BlockSpec((NSA_HEADS, 1), lambda b, g, j, s, p: (0, 0)),
                      pl.BlockSpec((1, NSA_HEADS, dh), lambda b, g, j, s, p: (b, 0, 0)),
                      pl.BlockSpec((1, NSA_HEADS, dh), lambda b, g, j, s, p: (b, 0, 0)),
                      pl.BlockSpec((1, NSA_HEADS, 3), lambda b, g, j, s, p: (b, 0, 0))],
            out_specs=pl.BlockSpec((1, NSA_HEADS, dh), lambda b, g, j, s, p: (b, 0, 0)),
            scratch_shapes=[pltpu.VMEM((NSA_HEADS, 1), f32), pltpu.VMEM((NSA_HEADS, 1), f32),
                            pltpu.VMEM((NSA_HEADS, dh), f32)]),
        out_shape=jax.ShapeDtypeStruct((db, NSA_HEADS, dh), f32),
        compiler_params=_cparams(("parallel", "arbitrary", "arbitrary")),
        name="nsa_sample_selected",
    )(sel_flat, page_table.reshape(-1), q3, cache_h, cache_h, t_sel, kv4_new.reshape(db, 1, 8 * dh), b0,
      o_c, o_w, gates)
    return o.reshape(db, NSA_HEADS * dh)


def _nsa_st_means_kernel(pt_ref, *refs):
    del pt_ref
    x_refs, o_ref = refs[:-1], refs[-1]
    p = pl.program_id(1)

    @pl.when(p == 0)
    def _():
        o_ref[...] = jnp.zeros_like(o_ref)

    page = x_refs[0].shape[-1]
    per_page = page // NSA_BLOCK
    lane = lax.broadcasted_iota(i32, (NSA_DH, page), 1)
    out_lane = lax.broadcasted_iota(i32, (NSA_DH, o_ref.shape[-1]), 1)
    for j in range(2):
        for g in range(NSA_KV_GROUPS):
            acc = o_ref[0, j * NSA_KV_GROUPS + g]
            for k, x_ref in enumerate(x_refs):
                x = x_ref[0, j, g]
                for n in range(per_page):
                    in_blk = (lane >= n * NSA_BLOCK) & (lane < (n + 1) * NSA_BLOCK)
                    mean = jnp.sum(jnp.where(in_blk, x, 0.0), axis=1, keepdims=True) * (1.0 / NSA_BLOCK)
                    col = (p * len(x_refs) + k) * per_page + n
                    acc = jnp.where(out_lane == col, mean, acc)
            o_ref[0, j * NSA_KV_GROUPS + g] = acc


def _nsa_st_means(cache_t, page_table):
    db, n_pages = page_table.shape
    page = cache_t.shape[-1]
    nblk = n_pages * page // NSA_BLOCK
    kp = MEANS_PAGES
    assert n_pages % kp == 0

    def page_spec(k):
        return pl.BlockSpec((1, 2, NSA_KV_GROUPS, NSA_DH, page),
                            lambda b, p, pt: (pt[b * n_pages + p * kp + k], 0, 0, 0, 0))

    return pl.pallas_call(
        _nsa_st_means_kernel,
        grid_spec=pltpu.PrefetchScalarGridSpec(
            num_scalar_prefetch=1,
            grid=(db, n_pages // kp),
            in_specs=[page_spec(k) for k in range(kp)],
            out_specs=pl.BlockSpec((1, 2 * NSA_KV_GROUPS, NSA_DH, nblk), lambda b, p, pt: (b, 0, 0, 0))),
        out_shape=jax.ShapeDtypeStruct((db, 2 * NSA_KV_GROUPS, NSA_DH, nblk), f32),
        compiler_params=_cparams(("parallel", "arbitrary")),
        name="nsa_sample_means",
    )(page_table.reshape(-1), *([cache_t] * kp))


def _nsa_st_scores_kernel(q_ref, kcvc_ref, win_ref, kvn_ref, tc_ref, tw_ref, b0_ref, oc_ref, ow_ref, sel_ref):
    dh, hpg = NSA_DH, NSA_HPG
    q = q_ref[0] * (dh ** -0.5)
    nblk = kcvc_ref.shape[-1]
    kvn = kvn_ref[0]
    row = lax.broadcasted_iota(i32, (NSA_HEADS, 1), 0)
    lane = lax.broadcasted_iota(i32, (1, nblk), 1)
    n_sel = NSA_TOPN - 1
    for g in range(NSA_KV_GROUPS):
        in_g = (row >= g * hpg) & (row < (g + 1) * hpg)
        s_c = _mm3(q, kcvc_ref[0, g]) + tc_ref[...]
        m_c = jnp.max(s_c, axis=-1, keepdims=True)
        p_c = jnp.exp(s_c - m_c)
        p_c = p_c / jnp.maximum(jnp.sum(p_c, axis=-1, keepdims=True), 1e-30)
        o_c = _mm_nt(p_c, kcvc_ref[0, NSA_KV_GROUPS + g])
        imp = jnp.sum(jnp.where(in_g, p_c, 0.0), axis=0, keepdims=True)
        forced_blocks = [0] + [nblk - k for k in range(1, NSA_LOCAL)]
        imp = jnp.where((lane == 0) | (lane > nblk - NSA_LOCAL), -jnp.inf, imp)
        picks = jnp.zeros((1, nblk), i32)
        for it, fb in enumerate(forced_blocks):
            picks = jnp.where(lane == it, fb, picks)
        for it in range(len(forced_blocks), n_sel):
            mx = jnp.max(imp, axis=-1, keepdims=True)
            first = jnp.min(jnp.where(imp == mx, lane, nblk), axis=-1, keepdims=True)
            picks = jnp.where(lane == it, first, picks)
            imp = jnp.where(lane == first, -jnp.inf, imp)
        sel_ref[0, g:g + 1, :] = picks
        s_w = _mm3(q, win_ref[0, 0, g]) + tw_ref[...]
        s_n = jnp.sum(q * kvn[:, g * dh:(g + 1) * dh], axis=-1, keepdims=True) + b0_ref[...]
        m_w = jnp.maximum(jnp.max(s_w, axis=-1, keepdims=True), s_n)
        p_w = jnp.exp(s_w - m_w)
        p_n = jnp.exp(s_n - m_w)
        l_w = jnp.sum(p_w, axis=-1, keepdims=True) + p_n
        v_n = kvn[:, (NSA_KV_GROUPS + g) * dh:(NSA_KV_GROUPS + g + 1) * dh]
        o_w = (_mm_nt(p_w, win_ref[0, 1, g]) + p_n * v_n) / jnp.maximum(l_w, 1e-30)
        if g == 0:
            oc_ref[0] = o_c
            ow_ref[0] = o_w
        else:
            oc_ref[0] = jnp.where(in_g, o_c, oc_ref[0])
            ow_ref[0] = jnp.where(in_g, o_w, ow_ref[0])


def _nsa_st_sel_kernel(sel_ref, pt_ref, q_ref, rb_ref, kvn_ref, oc_ref, ow_ref, gt_ref, *refs, n_sel, past):
    del pt_ref
    k_refs, v_refs, o_ref = refs[:n_sel], refs[n_sel:2 * n_sel], refs[2 * n_sel]
    b = pl.program_id(0)
    g = pl.program_id(1)
    dh, hpg = NSA_DH, NSA_HPG
    is_g0 = g == 0
    page = k_refs[0].shape[-1]
    per_page = page // NSA_BLOCK
    q = q_ref[0] * (dh ** -0.5)
    kvn = kvn_ref[0]
    lane = lax.broadcasted_iota(i32, (NSA_HEADS, page), 1)
    s_parts, d_parts = [], []
    for j in range(n_sel):
        nb = sel_ref[(b * NSA_KV_GROUPS + g) * n_sel + j]
        in_blk = (lane >> int(math.log2(NSA_BLOCK))) == (nb % per_page)
        d_parts.append(jnp.where(in_blk, past - ((nb // per_page) * page + lane), -1))
        s_parts.append(_mm3(q, k_refs[j][0, 0, 0]))
    d = jnp.concatenate(d_parts, axis=1)
    bias = jnp.broadcast_to(rb_ref[0], d.shape)
    for k in range(1, REL_BUCKETS):
        bias = jnp.where(d >= _BUCKET_THR[k - 1], rb_ref[k], bias)
    s = jnp.concatenate(s_parts, axis=1) + jnp.where(d < 0, NEG, bias)
    k_n = jnp.where(is_g0, kvn[:, 4 * dh:5 * dh], kvn[:, 5 * dh:6 * dh])
    v_n = jnp.where(is_g0, kvn[:, 6 * dh:7 * dh], kvn[:, 7 * dh:8 * dh])
    s_n = jnp.sum(q * k_n, axis=-1, keepdims=True) + rb_ref[0]
    m = jnp.maximum(jnp.max(s, axis=-1, keepdims=True), s_n)
    p = jnp.exp(s - m)
    p_n = jnp.exp(s_n - m)
    l = jnp.sum(p, axis=-1, keepdims=True) + p_n
    acc = p_n * v_n
    for j in range(n_sel):
        acc = acc + _mm_nt(p[:, j * page:(j + 1) * page], v_refs[j][0, 0, 0])
    o_s = acc / jnp.maximum(l, 1e-30)
    gt = gt_ref[0]
    o = gt[:, 0:1] * oc_ref[0] + gt[:, 1:2] * o_s + gt[:, 2:3] * ow_ref[0]
    row = lax.broadcasted_iota(i32, o.shape, 0)
    in_g = (row >= g * hpg) & (row < (g + 1) * hpg)

    @pl.when(is_g0)
    def _():
        o_ref[0] = o

    @pl.when(jnp.logical_not(is_g0))
    def _():
        o_ref[0] = jnp.where(in_g, o, o_ref[0])


def _nsa_sample_t(q_b, gates, kv4_new, kvwin_new, cache_kv, cache_win, page_table, rel_bias):
    db = q_b.shape[0]
    dh = NSA_DH
    n_pool, page = cache_kv.shape[:2]
    n_pages = page_table.shape[1]
    past = n_pages * page
    nblk = past // NSA_BLOCK
    wb = cache_win.shape[1]
    assert nblk == PAIR and page % NSA_BLOCK == 0 and wb == NSA_WINDOW and nblk > NSA_LOCAL
    cache_t = jnp.transpose(cache_kv, (0, 2, 3, 4, 1)).astype(f32)
    win_t = jnp.transpose(cache_win, (0, 2, 3, 4, 1)).astype(f32)
    kcvc = _nsa_st_means(cache_t, page_table)

    n = np.arange(nblk)
    d_cmp = past - (n * NSA_BLOCK + NSA_BLOCK - 1)
    jw = np.arange(wb)
    d_win = np.where(jw >= 1, wb - jw, -1)
    dist = np.broadcast_to(np.concatenate([d_cmp, d_win])[None, :], (NSA_HEADS, nblk + wb))
    tab = _bias_table(rel_bias, jnp.asarray(dist[None, None], i32), slab=1)[0, 0]
    t_cmp, t_win = tab[:, :nblk], tab[:, nblk:]
    rb = rel_bias.astype(f32)
    b0 = rb[0].reshape(NSA_HEADS, 1)

    q3 = q_b.reshape(db, NSA_HEADS, dh)
    o_c, o_w, sel = pl.pallas_call(
        _nsa_st_scores_kernel,
        grid=(db,),
        in_specs=[pl.BlockSpec((1, NSA_HEADS, dh), lambda b: (b, 0, 0)),
                  pl.BlockSpec((1, 2 * NSA_KV_GROUPS, dh, nblk), lambda b: (b, 0, 0, 0)),
                  pl.BlockSpec((1, 2, NSA_KV_GROUPS, dh, wb), lambda b: (b, 0, 0, 0, 0)),
                  pl.BlockSpec((1, 1, 4 * dh), lambda b: (b, 0, 0)),
                  pl.BlockSpec((NSA_HEADS, nblk), lambda b: (0, 0)),
                  pl.BlockSpec((NSA_HEADS, wb), lambda b: (0, 0)),
                  pl.BlockSpec((NSA_HEADS, 1), lambda b: (0, 0))],
        out_specs=[pl.BlockSpec((1, NSA_HEADS, dh), lambda b: (b, 0, 0)),
                   pl.BlockSpec((1, NSA_HEADS, dh), lambda b: (b, 0, 0)),
                   pl.BlockSpec((1, NSA_KV_GROUPS, nblk), lambda b: (b, 0, 0))],
        out_shape=[jax.ShapeDtypeStruct((db, NSA_HEADS, dh), f32),
                   jax.ShapeDtypeStruct((db, NSA_HEADS, dh), f32),
                   jax.ShapeDtypeStruct((db, NSA_KV_GROUPS, nblk), i32)],
        compiler_params=_cparams(("parallel",)),
        name="nsa_sample_scores",
    )(q3, kcvc, win_t, kvwin_new.reshape(db, 1, 4 * dh), t_cmp, t_win, b0)

    n_sel = NSA_TOPN - 1
    sel_flat = sel[:, :, :n_sel].reshape(-1)
    per_page = page // NSA_BLOCK

    def page_spec(j, plane):
        def imap(b, g, s, p):
            nb = s[(b * NSA_KV_GROUPS + g) * n_sel + j]
            return (p[b * n_pages + nb // per_page], plane, g, 0, 0)
        return pl.BlockSpec((1, 1, 1, dh, page), imap)

    const = lambda shape: pl.BlockSpec(shape, lambda b, g, s, p: (0,) * len(shape))
    per_b = lambda shape: pl.BlockSpec((1,) + shape, lambda b, g, s, p: (b,) + (0,) * len(shape))
    o = pl.pallas_call(
        functools.partial(_nsa_st_sel_kernel, n_sel=n_sel, past=past),
        grid_spec=pltpu.PrefetchScalarGridSpec(
            num_scalar_prefetch=2,
            grid=(db, NSA_KV_GROUPS),
            in_specs=[per_b((NSA_HEADS, dh)), const((REL_BUCKETS, NSA_HEADS, 1)), per_b((1, 8 * dh)),
                      per_b((NSA_HEADS, dh)), per_b((NSA_HEADS, dh)), per_b((NSA_HEADS, 3))]
                     + [page_spec(j, 2) for j in range(n_sel)] + [page_spec(j, 3) for j in range(n_sel)],
            out_specs=per_b((NSA_HEADS, dh))),
        out_shape=jax.ShapeDtypeStruct((db, NSA_HEADS, dh), f32),
        compiler_params=_cparams(("parallel", "arbitrary")),
        name="nsa_sample_selected",
    )(sel_flat, page_table.reshape(-1), q3, rb.reshape(REL_BUCKETS, NSA_HEADS, 1), kv4_new.reshape(db, 1, 8 * dh),
      o_c, o_w, gates, *([cache_t] * (2 * n_sel)))
    return o.reshape(db, NSA_HEADS * dh)


def _gelu_tanh(x):
    return 0.5 * x * (1.0 + jnp.tanh(math.sqrt(2.0 / math.pi) * (x + 0.044715 * (x * x * x))))


def _lru_gates(xc, wa_ref, wx_ref, ba_ref, bx_ref, lam_ref, prec):
    r_parts, i_parts = [], []
    for n in range(RNN_BLOCKS):
        xb = xc[:, n * RNN_BW:(n + 1) * RNN_BW]
        r_parts.append(_dotp(xb, wa_ref[n], prec))
        i_parts.append(_dotp(xb, wx_ref[n], prec))
    r = jax.nn.sigmoid(jnp.concatenate(r_parts, axis=1) + ba_ref[...])
    i = jax.nn.sigmoid(jnp.concatenate(i_parts, axis=1) + bx_ref[...])
    log_a = -RG_C * r * _softplus(-lam_ref[...])
    a = jnp.exp(log_a)
    t = jnp.tanh(log_a)
    b = jnp.sqrt(jnp.maximum(-2.0 * t / (1.0 - t), 0.0)) * (i * xc)
    return a, b


def _lru_kernel(rec_ref, gate_ref, cw_ref, cb_ref, wa_ref, wx_ref, ba_ref, bx_ref, lam_ref, cinit_ref, h0_ref,
                y_ref, hfin_ref, xc_scr, a_scr, b_scr, hs_scr, h_scr, *, tb):
    j = pl.program_id(1)

    @pl.when(j == 0)
    def _():
        xc_scr[0:8, :] = cinit_ref[0]
        h_scr[...] = h0_ref[0]

    x = rec_ref[0]
    xc_scr[8:8 + tb, :] = x
    xc = xc_scr[5:5 + tb, :] * cw_ref[0:1, :]
    xc = xc + xc_scr[6:6 + tb, :] * cw_ref[1:2, :]
    xc = xc + xc_scr[7:7 + tb, :] * cw_ref[2:3, :]
    xc = xc + x * cw_ref[3:4, :]
    xc = xc + cb_ref[...]
    xc_scr[0:8, :] = xc_scr[tb:tb + 8, :]
    a, b = _lru_gates(xc, wa_ref, wx_ref, ba_ref, bx_ref, lam_ref, 1)
    a_scr[...] = a
    b_scr[...] = b

    def step(t, h):
        h = a_scr[pl.ds(t, 1), :] * h + b_scr[pl.ds(t, 1), :]
        hs_scr[pl.ds(t, 1), :] = h
        return h

    h = lax.fori_loop(0, tb, step, h_scr[...], unroll=8)
    h_scr[...] = h
    y_ref[0] = _gelu_tanh(gate_ref[0]) * hs_scr[...]

    @pl.when(j == pl.num_programs(1) - 1)
    def _():
        hfin_ref[0] = h


def _lru_prompt(rec, gate, conv_w, conv_b, wa, wx, ba, bx, lam, *, tb):
    bsz, t, w = rec.shape
    row = lambda a: a.reshape(1, w).astype(f32)
    cinit = jnp.zeros((bsz, 8, w), f32)
    h0 = jnp.zeros((bsz, 1, w), f32)
    full = lambda shape: pl.BlockSpec(shape, lambda b, j: (0,) * len(shape))
    return pl.pallas_call(
        functools.partial(_lru_kernel, tb=tb),
        grid=(bsz, t // tb),
        in_specs=[pl.BlockSpec((1, tb, w), lambda b, j: (b, j, 0)),
                  pl.BlockSpec((1, tb, w), lambda b, j: (b, j, 0)),
                  full((RNN_CONV, w)), full((1, w)),
                  full((RNN_BLOCKS, RNN_BW, RNN_BW)), full((RNN_BLOCKS, RNN_BW, RNN_BW)),
                  full((1, w)), full((1, w)), full((1, w)),
                  pl.BlockSpec((1, 8, w), lambda b, j: (b, 0, 0)),
                  pl.BlockSpec((1, 1, w), lambda b, j: (b, 0, 0))],
        out_specs=[pl.BlockSpec((1, tb, w), lambda b, j: (b, j, 0)),
                   pl.BlockSpec((1, 1, w), lambda b, j: (b, 0, 0))],
        out_shape=[jax.ShapeDtypeStruct((bsz, t, w), f32), jax.ShapeDtypeStruct((bsz, 1, w), f32)],
        scratch_shapes=[pltpu.VMEM((tb + 8, w), f32), pltpu.VMEM((tb, w), f32), pltpu.VMEM((tb, w), f32),
                        pltpu.VMEM((tb, w), f32), pltpu.VMEM((1, w), f32)],
        compiler_params=_cparams(("parallel", "arbitrary")),
        name="rglru",
    )(rec, gate, conv_w.astype(f32), row(conv_b), wa.astype(f32), wx.astype(f32), row(ba), row(bx), row(lam),
      cinit, h0)


def _lru_step_kernel(rec_ref, gate_ref, b0_ref, b1_ref, b2_ref, cw_ref, cb_ref, wa_ref, wx_ref, ba_ref, bx_ref,
                     lam_ref, h0_ref, y_ref, h_ref):
    xc = b0_ref[...] * cw_ref[0:1, :]
    xc = xc + b1_ref[...] * cw_ref[1:2, :]
    xc = xc + b2_ref[...] * cw_ref[2:3, :]
    xc = xc + rec_ref[...] * cw_ref[3:4, :]
    xc = xc + cb_ref[...]
    a, b = _lru_gates(xc, wa_ref, wx_ref, ba_ref, bx_ref, lam_ref, 3)
    h = a * h0_ref[...] + b
    h_ref[...] = h
    y_ref[...] = _gelu_tanh(gate_ref[...]) * h


def _lru_sample(rec, gate, conv_buf, h0, conv_w, conv_b, wa, wx, ba, bx, lam):
    db, w = rec.shape
    row = lambda a: a.reshape(1, w).astype(f32)
    buf = conv_buf.astype(f32)
    return pl.pallas_call(
        _lru_step_kernel,
        out_shape=[jax.ShapeDtypeStruct((db, w), f32), jax.ShapeDtypeStruct((db, w), f32)],
        compiler_params=pltpu.CompilerParams(vmem_limit_bytes=VMEM_LIMIT),
        name="rglru_step",
    )(rec, gate, buf[:, 0], buf[:, 1], buf[:, 2], conv_w.astype(f32), row(conv_b), wa.astype(f32), wx.astype(f32),
      row(ba), row(bx), row(lam), h0.astype(f32))


def _top2_kernel(l_ref, e_ref, g_ref):
    lg = l_ref[...]
    lane = lax.broadcasted_iota(i32, lg.shape, 1)
    lg = jnp.where(lane < N_EXPERTS, lg, -jnp.inf)
    m1 = jnp.max(lg, axis=-1, keepdims=True)
    i1 = jnp.min(jnp.where(lg == m1, lane, lg.shape[1]), axis=-1, keepdims=True)
    lg2 = jnp.where(lane == i1, -jnp.inf, lg)
    m2 = jnp.max(lg2, axis=-1, keepdims=True)
    i2 = jnp.min(jnp.where(lg2 == m2, lane, lg.shape[1]), axis=-1, keepdims=True)
    e2 = jnp.exp(m2 - m1)
    den = 1.0 + e2
    e_ref[...] = jnp.where(lane == 0, i1, jnp.where(lane == 1, i2, 0))
    g_ref[...] = jnp.where(lane == 0, 1.0 / den, jnp.where(lane == 1, e2 / den, 0.0))


def _top2(logits, *, tm):
    n, w = logits.shape
    return pl.pallas_call(
        _top2_kernel,
        grid=(n // tm,),
        in_specs=[pl.BlockSpec((tm, w), lambda i: (i, 0))],
        out_specs=[pl.BlockSpec((tm, w), lambda i: (i, 0)), pl.BlockSpec((tm, w), lambda i: (i, 0))],
        out_shape=[jax.ShapeDtypeStruct((n, w), i32), jax.ShapeDtypeStruct((n, w), f32)],
        compiler_params=_cparams(("parallel",)),
        name="moe_top2",
    )(logits)


def _moe_kernel(be_ref, nu_ref, xs_ref, w1_ref, w3_ref, w2_ref, o_ref, acc_scr):
    del be_ref
    i = pl.program_id(0)
    f = pl.program_id(1)
    last = pl.num_programs(1) - 1
    used = i < nu_ref[0]

    @pl.when(used)
    def _():
        @pl.when(f == 0)
        def _():
            acc_scr[...] = jnp.zeros_like(acc_scr)

        x = xs_ref[...]
        act = _silu(_mm(x, w1_ref[0])) * _mm(x, w3_ref[0])
        acc_scr[...] += _mm(act, w2_ref[0])

        @pl.when(f == last)
        def _():
            o_ref[...] = acc_scr[...]

    @pl.when(jnp.logical_not(used) & (f == last))
    def _():
        o_ref[...] = jnp.zeros_like(o_ref)


def _moe_experts(xs, blk_e, n_used, w1, w3, w2, *, tm, tf):
    rows, d = xs.shape
    fdim = w1.shape[2]
    nf = fdim // tf

    def wcol(i, f, be, nu):
        return (be[i], 0, jnp.where(i < nu[0], f, nf - 1))

    def wrow(i, f, be, nu):
        return (be[i], jnp.where(i < nu[0], f, nf - 1), 0)

    return pl.pallas_call(
        _moe_kernel,
        grid_spec=pltpu.PrefetchScalarGridSpec(
            num_scalar_prefetch=2,
            grid=(rows // tm, nf),
            in_specs=[pl.BlockSpec((tm, d), lambda i, f, be, nu: (i, 0)),
                      pl.BlockSpec((1, d, tf), wcol),
                      pl.BlockSpec((1, d, tf), wcol),
                      pl.BlockSpec((1, tf, d), wrow)],
            out_specs=pl.BlockSpec((tm, d), lambda i, f, be, nu: (i, 0)),
            scratch_shapes=[pltpu.VMEM((tm, d), f32)]),
        out_shape=jax.ShapeDtypeStruct((rows, d), f32),
        compiler_params=_cparams(("arbitrary", "arbitrary")),
        name="moe_experts",
    )(blk_e, n_used, xs, w1, w3, w2)


def _combine_kernel(x_ref, gate_ref, y0_ref, y1_ref, g_ref, nw_ref, o_ref):
    gw = g_ref[0]
    y = y0_ref[0] * gw[:, 0:1] + y1_ref[0] * gw[:, 1:2]
    x = x_ref[0] + gate_ref[0] * y
    o_ref[0] = x * lax.rsqrt(jnp.mean(x * x, axis=-1, keepdims=True) + EPS) * nw_ref[...]


def _moe_combine(x, gate, y0, y1, gw, norm_w, *, tm):
    bsz, t, d = x.shape
    tok = lambda w: pl.BlockSpec((1, tm, w), lambda b, i: (b, i, 0))
    return pl.pallas_call(
        _combine_kernel,
        grid=(bsz, t // tm),
        in_specs=[tok(d), _mod_spec(gate, tm), tok(d), tok(d), tok(gw.shape[-1]),
                  pl.BlockSpec((1, d), lambda b, i: (0, 0))],
        out_specs=tok(d),
        out_shape=jax.ShapeDtypeStruct((bsz, t, d), f32),
        compiler_params=_cparams(("parallel", "parallel")),
        name="moe_combine",
    )(x, gate, y0, y1, gw, norm_w.reshape(1, d).astype(f32))


MOE_TM = 512
MOE_TF = 512


def _moe_dispatch(e_idx, n_tok):
    tm = MOE_TM
    n_assign = n_tok * TOP_K
    e_flat = e_idx.reshape(n_assign)
    onehot = (e_flat[:, None] == jnp.arange(N_EXPERTS, dtype=i32)[None, :]).astype(i32)
    rank = jnp.take_along_axis(jnp.cumsum(onehot, axis=0), e_flat[:, None], axis=1)[:, 0] - 1
    counts = jnp.sum(onehot, axis=0)
    padded = (counts + tm - 1) // tm * tm
    pend = jnp.cumsum(padded)
    pstart = pend - padded
    dest = pstart[e_flat] + rank
    n_blocks = -(-n_assign // tm) + N_EXPERTS
    rows = n_blocks * tm
    tok_flat = jnp.repeat(jnp.arange(n_tok, dtype=i32), TOP_K)
    row_tok = jnp.zeros((rows,), i32).at[dest].set(tok_flat)
    n_used = (pend[-1] // tm).astype(i32)
    blk = jnp.minimum(jnp.arange(n_blocks, dtype=i32), n_used - 1) * tm
    blk_e = jnp.minimum(jnp.searchsorted(pend, blk, side='right'), N_EXPERTS - 1).astype(i32)
    return dest.reshape(n_tok, TOP_K), row_tok, blk_e, n_used.reshape(1)


def _w_in0_layout(w_in0):
    c_qkv = GDN_CONV_CH
    c_z = c_qkv + GDN_HEADS * GDN_DV
    c_ab = c_z + 2 * GDN_HEADS
    c_q = c_ab + NSA_HEADS * NSA_DH
    c_kv = c_q + 6 * NSA_KV_GROUPS * NSA_DH
    c_g = c_kv + 3 * NSA_HEADS
    assert c_g == w_in0.shape[1]
    n_misc = 2 * GDN_HEADS + 3 * NSA_HEADS
    w = jnp.concatenate([w_in0[:, :c_z], w_in0[:, c_ab:c_kv], w_in0[:, c_z:c_ab], w_in0[:, c_kv:c_g],
                         jnp.zeros((w_in0.shape[0], MISC_W - n_misc), w_in0.dtype)], axis=1)
    widths = (GDN_CONV_CH, GDN_HEADS * GDN_DV, NSA_HEADS * NSA_DH, 4 * NSA_KV_GROUPS * NSA_DH,
              2 * NSA_KV_GROUPS * NSA_DH, MISC_W)
    splits, s = [], 0
    for wd in widths:
        splits.append((s, s + wd))
        s += wd
    return w, tuple(splits)


def kernel(x_prompt, x_sample, c_prompt, c_sample, cache_nsa_kv, cache_nsa_win, state_gdn, state_gdn_conv, state_lru, state_lru_conv, page_table, rel_bias, w_ada, b_ada, norm_mix, norm_ffn, norm_final, w_in0, gdn_conv_w, gdn_a_log, gdn_dt_bias, gdn_norm_w, w_out0, ffn_w_gate, ffn_w_up, ffn_w_down, w_in1, lru_conv_w, lru_conv_b, lru_wa, lru_ba, lru_wx, lru_bx, lru_lambda, w_out1, moe_router, moe_w1, moe_w3, moe_w2):
    bsz, seq, d = x_prompt.shape
    db = x_sample.shape[0]
    assert x_sample.shape[1] == 1
    dh = NSA_DH

    n_c = bsz + db
    n_c_pad = -(-n_c // 8) * 8
    c_all = jnp.concatenate([c_prompt, c_sample, jnp.zeros((n_c_pad - n_c, d), f32)], axis=0)
    mods = _adaln(c_all, w_ada, b_ada).reshape(2, n_c_pad, N_MOD, d)
    mod_p = [[mods[l, :bsz, k].reshape(bsz, 1, d) for k in range(N_MOD)] for l in range(2)]
    mod_s = [[mods[l, bsz:n_c, k].reshape(1, db, d) for k in range(N_MOD)] for l in range(2)]

    w0, splits0 = _w_in0_layout(w_in0)
    splits1 = ((0, RNN_WIDTH), (RNN_WIDTH, 2 * RNN_WIDTH))
    router = jnp.concatenate([moe_router, jnp.zeros((d, MISC_W - N_EXPERTS), f32)], axis=1)
    bf = lambda w: w.astype(bf16)

    tm = 512
    xp = x_prompt
    qkv, z, q_b, kv4, kvwin, misc = _mod_matmul(xp, norm_mix[0], mod_p[0][0], mod_p[0][1], bf(w0), splits0,
                                                tm=tm, prec=1)
    o_a, p_gdn = _gdn(qkv, z, misc, gdn_conv_w, jnp.zeros((bsz, GDN_CONV - 1, GDN_CONV_CH), f32), gdn_a_log,
                      gdn_dt_bias, gdn_norm_w, jnp.zeros((bsz, GDN_HEADS, GDN_DK, GDN_DV), f32), tb=256, n_valid=seq)
    p_gdn_conv = qkv[:, seq - (GDN_CONV - 1):]
    o_b = _nsa_prompt(q_b, misc, kv4, kvwin, rel_bias)
    p_nsa_kv = kv4.reshape(bsz, seq, 4, NSA_KV_GROUPS, dh)
    keep = min(NSA_WINDOW, seq)
    p_nsa_win = kvwin[:, seq - keep:].reshape(bsz, keep, 2, NSA_KV_GROUPS, dh)
    xp = _proj_residual([o_a, o_b], bf(w_out0), xp, mod_p[0][2], tm=tm, prec=1)
    xp = _ffn(xp, norm_ffn[0], mod_p[0][3], mod_p[0][4], mod_p[0][5], bf(ffn_w_gate), bf(ffn_w_up), bf(ffn_w_down),
              tm=tm, tf=FFN_DIM // 2, prec=1)
    gate_br, rec_br = _mod_matmul(xp, norm_mix[1], mod_p[1][0], mod_p[1][1], bf(w_in1), splits1, tm=tm, prec=1)
    y_in, p_lru = _lru_prompt(rec_br, gate_br, lru_conv_w, lru_conv_b, lru_wa, lru_wx, lru_ba, lru_bx, lru_lambda,
                              tb=256)
    p_lru_conv = rec_br[:, seq - (RNN_CONV - 1):]
    xp = _proj_residual([y_in], bf(w_out1), xp, mod_p[1][2], tm=tm, prec=1)
    logit_p, h_p = _mod_matmul(xp, norm_ffn[1], mod_p[1][3], mod_p[1][4], router, ((0, MISC_W),), tm=tm, prec=3,
                               emit_h=bf16)

    xs = x_sample.reshape(1, db, d)
    qkv_s, z_s, q_s, kv4_s, kvwin_s, misc_s = _mod_matmul(xs, norm_mix[0], mod_s[0][0], mod_s[0][1], w0, splits0,
                                                          tm=db, prec=3)
    c = GDN_CHUNK
    tpad = lambda a: jnp.pad(a.reshape(db, 1, a.shape[-1]), ((0, 0), (0, c - 1), (0, 0)))
    o_a_s, s_gdn = _gdn(tpad(qkv_s), tpad(z_s), tpad(misc_s), gdn_conv_w, state_gdn_conv, gdn_a_log, gdn_dt_bias,
                        gdn_norm_w, state_gdn, tb=c, n_valid=1)
    o_a_s = o_a_s[:, 0].reshape(1, db, GDN_HEADS * GDN_DV)
    s_gdn_conv = jnp.concatenate([state_gdn_conv[:, 1:], qkv_s.reshape(db, 1, GDN_CONV_CH)], axis=1)
    gates_s = jax.nn.sigmoid(misc_s[0, :, 2 * GDN_HEADS:2 * GDN_HEADS + 3 * NSA_HEADS]).reshape(db, NSA_HEADS, 3)
    o_b_s = _nsa_sample_t(q_s[0], gates_s, kv4_s[0], kvwin_s[0], cache_nsa_kv, cache_nsa_win, page_table, rel_bias)
    s_nsa_kv = kv4_s.reshape(db, 1, 4, NSA_KV_GROUPS, dh)
    s_nsa_win = jnp.concatenate([cache_nsa_win[:, 1:],
                                 kvwin_s.reshape(db, 1, 2, NSA_KV_GROUPS, dh).astype(cache_nsa_win.dtype)], axis=1)
    xs = _proj_residual([o_a_s, o_b_s.reshape(1, db, NSA_HEADS * dh)], w_out0, xs, mod_s[0][2], tm=db, prec=3)
    xs = _ffn(xs, norm_ffn[0], mod_s[0][3], mod_s[0][4], mod_s[0][5], ffn_w_gate, ffn_w_up, ffn_w_down,
              tm=db, tf=256, prec=3)
    gate_s, rec_s = _mod_matmul(xs, norm_mix[1], mod_s[1][0], mod_s[1][1], w_in1, splits1, tm=db, prec=3)
    y_in_s, s_lru = _lru_sample(rec_s[0], gate_s[0], state_lru_conv, state_lru, lru_conv_w, lru_conv_b, lru_wa, lru_wx,
                                lru_ba, lru_bx, lru_lambda)
    s_lru_conv = jnp.concatenate([state_lru_conv[:, 1:], rec_s.reshape(db, 1, RNN_WIDTH)], axis=1)
    xs = _proj_residual([y_in_s.reshape(1, db, RNN_WIDTH)], w_out1, xs, mod_s[1][2], tm=db, prec=3)
    logit_s, h_s = _mod_matmul(xs, norm_ffn[1], mod_s[1][3], mod_s[1][4], router, ((0, MISC_W),), tm=db, prec=3,
                               emit_h=bf16)

    n_p = bsz * seq
    n_tok = n_p + db
    logits = jnp.concatenate([logit_p.reshape(n_p, MISC_W), logit_s.reshape(db, MISC_W)], axis=0)
    h_all = jnp.concatenate([h_p.reshape(n_p, d), h_s.reshape(db, d)], axis=0)
    e_idx, gw = _top2(logits, tm=max(t for t in range(8, 1025, 8) if n_tok % t == 0))
    dest, row_tok, blk_e, n_used = _moe_dispatch(e_idx[:, :TOP_K], n_tok)
    yb = _moe_experts(h_all[row_tok], blk_e, n_used, moe_w1, moe_w3, moe_w2, tm=MOE_TM, tf=MOE_TF)
    y0, y1 = yb[dest[:, 0]], yb[dest[:, 1]]
    y_prompt = _moe_combine(xp, mod_p[1][5], y0[:n_p].reshape(bsz, seq, d), y1[:n_p].reshape(bsz, seq, d),
                            gw[:n_p].reshape(bsz, seq, MISC_W), norm_final, tm=tm)
    y_sample = _moe_combine(xs, mod_s[1][5], y0[n_p:].reshape(1, db, d), y1[n_p:].reshape(1, db, d),
                            gw[n_p:].reshape(1, db, MISC_W), norm_final, tm=db)

    return (y_prompt, y_sample.reshape(db, 1, d),
            p_nsa_kv, p_nsa_win, p_gdn, p_gdn_conv, p_lru.reshape(bsz, RNN_WIDTH), p_lru_conv,
            s_nsa_kv, s_nsa_win, s_gdn, s_gdn_conv, s_lru, s_lru_conv)
```

```python
import functools
import math

import numpy as np
import jax
import jax.numpy as jnp
from jax import lax
from jax.experimental import pallas as pl
from jax.experimental.pallas import tpu as pltpu

f32 = jnp.float32
bf16 = jnp.bfloat16
i32 = jnp.int32

D_MODEL = 1024
EPS = 1e-6
N_MOD = 6
GDN_HEADS = 4
GDN_DK = 128
GDN_DV = 128
GDN_CONV = 4
GDN_CHUNK = 64
GDN_CONV_CH = GDN_HEADS * (2 * GDN_DK + GDN_DV)
NSA_HEADS = 8
NSA_KV_GROUPS = 2
NSA_HPG = NSA_HEADS // NSA_KV_GROUPS
NSA_DH = 64
NSA_BLOCK = 64
NSA_TOPN = 16
NSA_LOCAL = 2
NSA_WINDOW = 512
NSA_QBLOCK = 128
NSA_FORCE = 1.0e4
REL_BUCKETS = 32
REL_MAX_DIST = 2048
RNN_WIDTH = D_MODEL
RNN_BLOCKS = 8
RNN_BW = RNN_WIDTH // RNN_BLOCKS
RNN_CONV = 4
RG_C = 8.0
FFN_DIM = 2816
N_EXPERTS = 8
TOP_K = 2
EXPERT_DIM = 3584

NEG = -1e30
PAIR = 2 * NSA_BLOCK
NEAR_PAIRS = 13
MISC_W = 128
VMEM_LIMIT = 56 * 1024 * 1024


def _cparams(sem, vmem=VMEM_LIMIT):
    return pltpu.CompilerParams(dimension_semantics=sem, vmem_limit_bytes=vmem)


def _mm(a, b):
    return jnp.dot(a.astype(bf16), b.astype(bf16), preferred_element_type=f32)


def _mm_nt(a, b):
    return lax.dot_general(a.astype(bf16), b.astype(bf16), (((1,), (1,)), ((), ())),
                           preferred_element_type=f32)


def _split2(a):
    hi = a.astype(bf16)
    lo = (a - hi.astype(f32)).astype(bf16)
    return hi, lo


def _mm3(a, b):
    ah, al = _split2(a)
    bh, bl = _split2(b)
    return (jnp.dot(ah, bh, preferred_element_type=f32) + jnp.dot(ah, bl, preferred_element_type=f32)
            + jnp.dot(al, bh, preferred_element_type=f32))


def _mm3_nt(a, b):
    ah, al = _split2(a)
    bh, bl = _split2(b)
    dn = (((1,), (1,)), ((), ()))
    return (lax.dot_general(ah, bh, dn, preferred_element_type=f32)
            + lax.dot_general(ah, bl, dn, preferred_element_type=f32)
            + lax.dot_general(al, bh, dn, preferred_element_type=f32))


def _mm_01(m01, a):
    hi = a.astype(bf16)
    r1 = a - hi.astype(f32)
    mid = r1.astype(bf16)
    lo = (r1 - mid.astype(f32)).astype(bf16)
    return (jnp.dot(m01, hi, preferred_element_type=f32) + jnp.dot(m01, mid, preferred_element_type=f32)
            + jnp.dot(m01, lo, preferred_element_type=f32))


def _dotp(a, b, prec):
    return _mm3(a, b) if prec == 3 else _mm(a, b)


def _silu(x):
    return x * jax.nn.sigmoid(x)


def _softplus(x):
    return jnp.maximum(x, 0.0) + jnp.log1p(jnp.exp(-jnp.abs(x)))


def _modulate(x, gain, shift, scale):
    r = lax.rsqrt(jnp.mean(x * x, axis=-1, keepdims=True) + EPS)
    return x * r * gain * (1.0 + scale) + shift


def _ada_kernel(c_ref, w_ref, b_ref, o_ref):
    o_ref[0] = _mm3(_silu(c_ref[...]), w_ref[0]) + b_ref[0]


def _adaln(c_all, w_ada, b_ada):
    rows = c_all.shape[0]
    depth, d, n = w_ada.shape
    tn = 1536
    return pl.pallas_call(
        _ada_kernel,
        grid=(depth, n // tn),
        in_specs=[pl.BlockSpec((rows, d), lambda l, j: (0, 0)),
                  pl.BlockSpec((1, d, tn), lambda l, j: (l, 0, j)),
                  pl.BlockSpec((1, 1, tn), lambda l, j: (l, 0, j))],
        out_specs=pl.BlockSpec((1, rows, tn), lambda l, j: (l, 0, j)),
        out_shape=jax.ShapeDtypeStruct((depth, rows, n), f32),
        compiler_params=_cparams(("arbitrary", "arbitrary")),
        name="adaln",
    )(c_all, w_ada, b_ada.reshape(depth, 1, n))


def _mod_spec(mod, tm):
    r = mod.shape[1]
    if r == 1:
        return pl.BlockSpec((1, 1, mod.shape[2]), lambda b, i: (b, 0, 0))
    return pl.BlockSpec((1, tm, mod.shape[2]), lambda b, i: (b, i, 0))


def _modmm_kernel(x_ref, gain_ref, shift_ref, scale_ref, w_ref, *o_refs, splits, prec, emit_h):
    h = _modulate(x_ref[0], gain_ref[...], shift_ref[0], scale_ref[0])
    if emit_h:
        o_refs[-1][0] = h.astype(o_refs[-1].dtype)
    hh = _split2(h) if prec == 3 else h.astype(bf16)
    for o_ref, (a, b) in zip(o_refs, splits):
        w = w_ref[:, a:b]
        if prec == 3:
            wh, wl = _split2(w)
            acc = (jnp.dot(hh[0], wh, preferred_element_type=f32) + jnp.dot(hh[0], wl, preferred_element_type=f32)
                   + jnp.dot(hh[1], wh, preferred_element_type=f32))
        else:
            acc = jnp.dot(hh, w, preferred_element_type=f32)
        o_ref[0] = acc


def _mod_matmul(x, gain, shift, scale, w, splits, *, tm, prec, emit_h=None):
    bsz, t, d = x.shape
    out_shape = [jax.ShapeDtypeStruct((bsz, t, b - a), f32) for a, b in splits]
    out_specs = [pl.BlockSpec((1, tm, b - a), lambda bi, i: (bi, i, 0)) for a, b in splits]
    if emit_h is not None:
        out_shape.append(jax.ShapeDtypeStruct((bsz, t, d), emit_h))
        out_specs.append(pl.BlockSpec((1, tm, d), lambda bi, i: (bi, i, 0)))
    return pl.pallas_call(
        functools.partial(_modmm_kernel, splits=tuple(splits), prec=prec, emit_h=emit_h is not None),
        grid=(bsz, t // tm),
        in_specs=[pl.BlockSpec((1, tm, d), lambda bi, i: (bi, i, 0)),
                  pl.BlockSpec((1, d), lambda bi, i: (0, 0)),
                  _mod_spec(shift, tm), _mod_spec(scale, tm),
                  pl.BlockSpec(w.shape, lambda bi, i: (0, 0))],
        out_specs=out_specs,
        out_shape=out_shape,
        compiler_params=_cparams(("parallel", "parallel")),
        name="mod_matmul",
    )(x, gain.reshape(1, d), shift, scale, w)


def _projres_kernel(*refs, n_lhs, ksplits, prec, final_norm):
    lhs = refs[:n_lhs]
    w_ref, x_ref, gate_ref = refs[n_lhs:n_lhs + 3]
    o_ref = refs[-1]
    acc = None
    for l_ref, (a, b) in zip(lhs, ksplits):
        part = _dotp(l_ref[0], w_ref[a:b, :], prec)
        acc = part if acc is None else acc + part
    y = x_ref[0] + gate_ref[0] * acc
    if final_norm:
        nw_ref = refs[n_lhs + 3]
        y = y * lax.rsqrt(jnp.mean(y * y, axis=-1, keepdims=True) + EPS) * nw_ref[...]
    o_ref[0] = y


def _proj_residual(lhs_list, w, x, gate, *, tm, prec, norm_w=None):
    bsz, t, d = x.shape
    ksplits, k0 = [], 0
    for l in lhs_list:
        ksplits.append((k0, k0 + l.shape[-1]))
        k0 += l.shape[-1]
    in_specs = [pl.BlockSpec((1, tm, l.shape[-1]), lambda bi, i: (bi, i, 0)) for l in lhs_list]
    in_specs += [pl.BlockSpec(w.shape, lambda bi, i: (0, 0)),
                 pl.BlockSpec((1, tm, d), lambda bi, i: (bi, i, 0)),
                 _mod_spec(gate, tm)]
    args = list(lhs_list) + [w, x, gate]
    if norm_w is not None:
        in_specs.append(pl.BlockSpec((1, d), lambda bi, i: (0, 0)))
        args.append(norm_w.reshape(1, d))
    return pl.pallas_call(
        functools.partial(_projres_kernel, n_lhs=len(lhs_list), ksplits=tuple(ksplits), prec=prec,
                          final_norm=norm_w is not None),
        grid=(bsz, t // tm),
        in_specs=in_specs,
        out_specs=pl.BlockSpec((1, tm, d), lambda bi, i: (bi, i, 0)),
        out_shape=jax.ShapeDtypeStruct((bsz, t, d), f32),
        compiler_params=_cparams(("parallel", "parallel")),
        name="proj_residual",
    )(*args)


def _ffn_kernel(x_ref, gain_ref, shift_ref, scale_ref, gate_ref, wg_ref, wu_ref, wd_ref, o_ref,
                h_scr, acc_scr, *, prec):
    f = pl.program_id(2)

    @pl.when(f == 0)
    def _():
        h_scr[...] = _modulate(x_ref[0], gain_ref[...], shift_ref[0], scale_ref[0]).astype(h_scr.dtype)
        acc_scr[...] = jnp.zeros_like(acc_scr)

    h = h_scr[...]
    act = _silu(_dotp(h, wg_ref[...], prec)) * _dotp(h, wu_ref[...], prec)
    acc_scr[...] += _dotp(act, wd_ref[...], prec)

    @pl.when(f == pl.num_programs(2) - 1)
    def _():
        o_ref[0] = x_ref[0] + gate_ref[0] * acc_scr[...]


def _ffn(x, gain, shift, scale, gate, wg, wu, wd, *, tm, tf, prec):
    bsz, t, d = x.shape
    fdim = wg.shape[1]
    mod_specs = []
    for mod in (shift, scale, gate):
        if mod.shape[1] == 1:
            mod_specs.append(pl.BlockSpec((1, 1, d), lambda b, i, f: (b, 0, 0)))
        else:
            mod_specs.append(pl.BlockSpec((1, tm, d), lambda b, i, f: (b, i, 0)))
    return pl.pallas_call(
        functools.partial(_ffn_kernel, prec=prec),
        grid=(bsz, t // tm, fdim // tf),
        in_specs=[pl.BlockSpec((1, tm, d), lambda b, i, f: (b, i, 0)),
                  pl.BlockSpec((1, d), lambda b, i, f: (0, 0)),
                  mod_specs[0], mod_specs[1], mod_specs[2],
                  pl.BlockSpec((d, tf), lambda b, i, f: (0, f)),
                  pl.BlockSpec((d, tf), lambda b, i, f: (0, f)),
                  pl.BlockSpec((tf, d), lambda b, i, f: (f, 0))],
        out_specs=pl.BlockSpec((1, tm, d), lambda b, i, f: (b, i, 0)),
        out_shape=jax.ShapeDtypeStruct((bsz, t, d), f32),
        scratch_shapes=[pltpu.VMEM((tm, d), f32 if prec == 3 else bf16), pltpu.VMEM((tm, d), f32)],
        compiler_params=_cparams(("parallel", "parallel", "arbitrary")),
        name="ffn",
    )(x, gain.reshape(1, d), shift, scale, gate, wg, wu, wd)


def _gdn_kernel(qkv_ref, z_ref, misc_ref, cw_ref, cinit_ref, hp_ref, nw_ref, s0_ref,
                o_ref, sfin_ref, xc_scr, s_scr, *, tb, n_valid):
    j = pl.program_id(1)
    c = GDN_CHUNK
    nh = GDN_HEADS

    @pl.when(j == 0)
    def _():
        xc_scr[0:8, :] = cinit_ref[0]
        s_scr[...] = s0_ref[0]

    x = qkv_ref[0]
    xc_scr[8:8 + tb, :] = x
    y = xc_scr[5:5 + tb, :] * cw_ref[0:1, :]
    y = y + xc_scr[6:6 + tb, :] * cw_ref[1:2, :]
    y = y + xc_scr[7:7 + tb, :] * cw_ref[2:3, :]
    y = y + x * cw_ref[3:4, :]
    xc_scr[0:8, :] = xc_scr[tb:tb + 8, :]
    y = _silu(y)

    misc = misc_ref[0]
    row = lax.broadcasted_iota(i32, (tb, MISC_W), 0) + j * tb
    live = row < n_valid
    log_a = jnp.where(live, hp_ref[0:1, :] * _softplus(misc + hp_ref[1:2, :]), 0.0)
    beta = jnp.where(live, jax.nn.sigmoid(misc), 0.0)

    r_i = lax.broadcasted_iota(i32, (tb, tb), 0)
    c_i = lax.broadcasted_iota(i32, (tb, tb), 1)
    sh = int(math.log2(c))
    ltri = (((r_i >> sh) == (c_i >> sh)) & (c_i <= r_i)).astype(bf16)
    g = _mm_01(ltri, log_a)
    g_t = g.T

    ri = lax.broadcasted_iota(i32, (c, c), 0)
    ci = lax.broadcasted_iota(i32, (c, c), 1)
    incl = ci <= ri
    strict = ci < ri
    eye = (ci == ri).astype(f32)
    quad = []
    for lvl in range(int(math.log2(c))):
        quad.append(((ri >> (lvl + 1)) == (ci >> (lvl + 1))) & (((ri >> lvl) & 1) == 1) & (((ci >> lvl) & 1) == 0))

    nchunk = tb // c
    units = [(n, h) for n in range(nchunk) for h in range(nh)]

    def stack(fn):
        return jnp.stack([fn(n * c, h) for n, h in units], axis=0)

    def bmm(eq, a_, b_):
        return jnp.einsum(eq, a_.astype(bf16), b_.astype(bf16), preferred_element_type=f32)

    q = stack(lambda r0, h: y[r0:r0 + c, h * GDN_DK:(h + 1) * GDN_DK])
    k = stack(lambda r0, h: y[r0:r0 + c, (nh + h) * GDN_DK:(nh + h + 1) * GDN_DK])
    v = stack(lambda r0, h: y[r0:r0 + c, 2 * nh * GDN_DK + h * GDN_DV:2 * nh * GDN_DK + (h + 1) * GDN_DV])
    q = q * lax.rsqrt(jnp.sum(q * q, axis=-1, keepdims=True) + EPS) * (GDN_DK ** -0.5)
    k = k * lax.rsqrt(jnp.sum(k * k, axis=-1, keepdims=True) + EPS)
    g_col = stack(lambda r0, h: g[r0:r0 + c, h:h + 1])
    g_row = stack(lambda r0, h: g_t[h:h + 1, r0:r0 + c])
    b_col = stack(lambda r0, h: beta[r0:r0 + c, nh + h:nh + h + 1])
    gam = jnp.where(incl, jnp.exp(jnp.where(incl, g_col - g_row, 0.0)), 0.0)
    kk = bmm('uid,ujd->uij', k, k)
    a = jnp.where(strict, b_col * gam * kk, 0.0)
    p = eye - jnp.where(quad[0], a, 0.0)
    for lvl in range(1, len(quad)):
        m = bmm('uij,ujk->uik', jnp.where(quad[lvl], a, 0.0), p)
        p = p - bmm('uij,ujk->uik', p, m)
    e_g = jnp.exp(g_col)
    sol = bmm('uij,ujd->uid', p, jnp.concatenate([b_col * v, (b_col * e_g) * k], axis=-1))
    vb, w = sol[:, :, :GDN_DV], sol[:, :, GDN_DV:]
    aqk = bmm('uid,ujd->uij', q, k) * gam
    qg = q * e_g
    g_last = g_col[:, c - 1:c, :]
    kd_t = jnp.swapaxes(k * jnp.exp(g_last - g_col), 1, 2)
    gc = jnp.exp(g_last)

    s = s_scr[...]
    for n in range(nchunk):
        sl = slice(n * nh, (n + 1) * nh)
        u = vb[sl] - bmm('hcd,hde->hce', w[sl], s)
        o = bmm('hcd,hde->hce', qg[sl], s) + bmm('hij,hje->hie', aqk[sl], u)
        s = gc[sl] * s + bmm('hdc,hce->hde', kd_t[sl], u)
        o = o * lax.rsqrt(jnp.mean(o * o, axis=-1, keepdims=True) + EPS) * nw_ref[...]
        for h in range(nh):
            zs = z_ref[0, n * c:(n + 1) * c, h * GDN_DV:(h + 1) * GDN_DV]
            o_ref[0, n * c:(n + 1) * c, h * GDN_DV:(h + 1) * GDN_DV] = o[h] * _silu(zs)
    s_scr[...] = s

    @pl.when(j == pl.num_programs(1) - 1)
    def _():
        sfin_ref[0] = s_scr[...]


def _gdn(qkv_raw, z, misc, conv_w, conv_buf, a_log, dt_bias, norm_w, s0, *, tb, n_valid):
    bsz, tp, ch = qkv_raw.shape
    cinit = jnp.concatenate([jnp.zeros((bsz, 5, ch), f32), conv_buf.astype(f32)], axis=1)
    hp = jnp.zeros((8, MISC_W), f32)
    hp = hp.at[0, :GDN_HEADS].set(-jnp.exp(a_log.astype(f32))).at[1, :GDN_HEADS].set(dt_bias.astype(f32))
    zw = GDN_HEADS * GDN_DV
    return pl.pallas_call(
        functools.partial(_gdn_kernel, tb=tb, n_valid=n_valid),
        grid=(bsz, tp // tb),
        in_specs=[pl.BlockSpec((1, tb, ch), lambda b, j: (b, j, 0)),
                  pl.BlockSpec((1, tb, zw), lambda b, j: (b, j, 0)),
                  pl.BlockSpec((1, tb, MISC_W), lambda b, j: (b, j, 0)),
                  pl.BlockSpec((GDN_CONV, ch), lambda b, j: (0, 0)),
                  pl.BlockSpec((1, 8, ch), lambda b, j: (b, 0, 0)),
                  pl.BlockSpec((8, MISC_W), lambda b, j: (0, 0)),
                  pl.BlockSpec((1, GDN_DV), lambda b, j: (0, 0)),
                  pl.BlockSpec((1, GDN_HEADS, GDN_DK, GDN_DV), lambda b, j: (b, 0, 0, 0))],
        out_specs=[pl.BlockSpec((1, tb, zw), lambda b, j: (b, j, 0)),
                   pl.BlockSpec((1, GDN_HEADS, GDN_DK, GDN_DV), lambda b, j: (b, 0, 0, 0))],
        out_shape=[jax.ShapeDtypeStruct((bsz, tp, zw), f32),
                   jax.ShapeDtypeStruct((bsz, GDN_HEADS, GDN_DK, GDN_DV), f32)],
        scratch_shapes=[pltpu.VMEM((tb + 8, ch), f32), pltpu.VMEM((GDN_HEADS, GDN_DK, GDN_DV), f32)],
        compiler_params=_cparams(("parallel", "arbitrary")),
        name="gdn",
    )(qkv_raw, z, misc, conv_w.astype(f32), cinit, hp, norm_w.reshape(1, GDN_DV).astype(f32), s0.astype(f32))


def _bucket_thresholds():
    exact = REL_BUCKETS // 2
    d = np.arange(0, 4 * REL_MAX_DIST, dtype=np.int64)

    def buckets(ft):
        nf = np.maximum(d, exact).astype(ft)
        large = exact + (np.log(nf / ft(exact)) / ft(math.log(REL_MAX_DIST / exact)) * ft(REL_BUCKETS - exact)).astype(np.int32)
        return np.where(d < exact, d, np.minimum(large, REL_BUCKETS - 1))

    b64, b32 = buckets(np.float64), buckets(np.float32)
    assert np.array_equal(b64, b32) and b64[-1] == REL_BUCKETS - 1 and np.all(np.diff(b64) >= 0)
    return [int(np.argmax(b64 >= k)) for k in range(1, REL_BUCKETS)]


_BUCKET_THR = _bucket_thresholds()
FAR_DIST = _BUCKET_THR[-1]


def _bias_kernel(rb_ref, d_ref, o_ref, *, slab, group_heads):
    rows = d_ref.shape[-2]
    for s in range(rows // slab):
        d = d_ref[0, 0, s * slab:(s + 1) * slab, :]
        h = pl.program_id(0) * group_heads + s
        val = jnp.full(d.shape, rb_ref[0, h], f32)
        for k in range(1, REL_BUCKETS):
            val = jnp.where(d >= _BUCKET_THR[k - 1], rb_ref[k, h], val)
        o_ref[0, 0, s * slab:(s + 1) * slab, :] = jnp.where(d < 0, NEG, val)


def _bias_table(rel_bias, dist, *, slab):
    g, nt, rows, cols = dist.shape
    return pl.pallas_call(
        functools.partial(_bias_kernel, slab=slab, group_heads=rows // slab),
        grid=(g, nt),
        in_specs=[pl.BlockSpec(memory_space=pltpu.SMEM),
                  pl.BlockSpec((1, 1, rows, cols), lambda a, b: (a, b, 0, 0))],
        out_specs=pl.BlockSpec((1, 1, rows, cols), lambda a, b: (a, b, 0, 0)),
        out_shape=jax.ShapeDtypeStruct(dist.shape, f32),
        compiler_params=_cparams(("arbitrary", "arbitrary")),
        name="bias_table",
    )(rel_bias.astype(f32), dist)


def _blockmean_kernel(x_ref, o_ref):
    x = x_ref[0]
    nb = x.shape[0] // NSA_BLOCK
    o_ref[0] = jnp.sum(x.reshape(nb, NSA_BLOCK, x.shape[1]), axis=1) * (1.0 / NSA_BLOCK)


def _block_means(kv4, *, tb):
    bsz, t, _ = kv4.shape
    w = 2 * NSA_KV_GROUPS * NSA_DH
    return pl.pallas_call(
        _blockmean_kernel,
        grid=(bsz, t // tb),
        in_specs=[pl.BlockSpec((1, tb, w), lambda b, i: (b, i, 0))],
        out_specs=pl.BlockSpec((1, tb // NSA_BLOCK, w), lambda b, i: (b, i, 0)),
        out_shape=jax.ShapeDtypeStruct((bsz, t // NSA_BLOCK, w), f32),
        compiler_params=_cparams(("parallel", "parallel")),
        name="nsa_block_means",
    )(kv4)


def _topn_mask(imp, n_sel):
    lane = lax.broadcasted_iota(i32, imp.shape, 1)
    sel = jnp.zeros(imp.shape, jnp.bool_)
    for _ in range(n_sel):
        mx = jnp.max(imp, axis=-1, keepdims=True)
        first = jnp.min(jnp.where(imp == mx, lane, imp.shape[1]), axis=-1, keepdims=True)
        hit = lane == first
        sel = sel | hit
        imp = jnp.where(hit, -jnp.inf, imp)
    return sel


def _nsa_prompt_kernel(q_ref, misc_ref, kcvc_ref, kaug_ref, v2_ref, kw_ref, vw_ref,
                       tabs_ref, tabw_ref, tabc_ref, c31_ref, o_ref):
    g = pl.program_id(1)
    i = pl.program_id(2)
    qb, dh, hpg = NSA_QBLOCK, NSA_DH, NSA_HPG
    rows = hpg * qb
    is_g0 = g == 0

    def ghalf(a):
        return jnp.where(is_g0, a[:, :dh], a[:, dh:])

    qt = q_ref[0] * (dh ** -0.5)
    qs = jnp.concatenate([qt[:, h * dh:(h + 1) * dh] for h in range(hpg)], axis=0)
    zero = jnp.zeros_like(qs)
    q2 = jnp.concatenate([jnp.where(is_g0, qs, zero), jnp.where(is_g0, zero, qs)], axis=1)

    kcvc = kcvc_ref[0]
    nblk = kcvc.shape[0]
    kc = ghalf(kcvc[:, :2 * dh])
    vc = ghalf(kcvc[:, 2 * dh:])
    s_c = _mm3_nt(qs, kc)
    lane = lax.broadcasted_iota(i32, (rows, nblk), 1)
    qrow = lax.broadcasted_iota(i32, (rows, nblk), 0) & (qb - 1)
    shift = (2 * i - 2 * NEAR_PAIRS + 1 + 4 * nblk) % nblk
    bias_c = pltpu.roll(tabc_ref[0], shift, 1)
    bias_c = jnp.where(lane < 2 * i - 2 * NEAR_PAIRS + 1, c31_ref[0], bias_c)
    readable = lane * NSA_BLOCK + (NSA_BLOCK - 1) <= i * qb + qrow
    s_c = jnp.where(readable, s_c + bias_c, NEG)
    m_c = jnp.max(s_c, axis=-1, keepdims=True)
    p_c = jnp.where(readable, jnp.exp(s_c - m_c), 0.0)
    p_c = p_c / jnp.maximum(jnp.sum(p_c, axis=-1, keepdims=True), 1e-30)
    o_c = _mm(p_c, vc)

    imp = p_c[0:qb]
    for h in range(1, hpg):
        imp = imp + p_c[h * qb:(h + 1) * qb]
    blk = lax.broadcasted_iota(i32, (qb, nblk), 1)
    cur = (i * qb + lax.broadcasted_iota(i32, (qb, nblk), 0)) >> int(math.log2(NSA_BLOCK))
    valid = blk <= cur
    forced = valid & ((blk == 0) | (blk > cur - NSA_LOCAL))
    imp = jnp.where(forced, -jnp.inf, jnp.where(valid, imp, -1.0))
    sel = (forced | _topn_mask(imp, min(NSA_TOPN, nblk) - 1 - NSA_LOCAL)) & valid
    msel = jnp.where(sel, 0.0, NEG).astype(bf16)
    qaug = jnp.concatenate([q2.astype(bf16), jnp.concatenate([msel] * hpg, axis=0)], axis=1)

    t_d = i // 2
    tk = 2 * PAIR

    def tile(k, carry):
        m, l, acc = carry
        t = t_d - k
        ks = pl.multiple_of(t * tk, tk)
        s = lax.dot_general(qaug, kaug_ref[0, pl.ds(ks, tk), :], (((1,), (1,)), ((), ())),
                            preferred_element_type=f32)
        i0 = jnp.clip(i - 2 * t, 0, NEAR_PAIRS)
        i1 = jnp.clip(i - 2 * t - 1, 0, NEAR_PAIRS)
        s = s + jnp.concatenate([tabs_ref[0, i0], tabs_ref[0, i1]], axis=1)
        m_new = jnp.maximum(m, jnp.max(s, axis=-1, keepdims=True))
        alpha = jnp.exp(m - m_new)
        p = jnp.exp(s - m_new)
        l = alpha * l + jnp.sum(p, axis=-1, keepdims=True)
        acc = alpha * acc + jnp.dot(p.astype(bf16), v2_ref[0, pl.ds(ks, tk), :], preferred_element_type=f32)
        return m_new, l, acc

    m0 = jnp.full((rows, 1), NEG, f32)
    l0 = jnp.zeros((rows, 1), f32)
    a0 = jnp.zeros((rows, 2 * dh), f32)
    _, l_s, acc_s = lax.fori_loop(0, t_d + 1, tile, (m0, l0, a0))
    o_s = ghalf(acc_s) / jnp.maximum(l_s, 1e-30)

    span = NSA_WINDOW + qb
    ws = pl.multiple_of(i * qb, qb)
    s_w = lax.dot_general(q2.astype(bf16), kw_ref[0, pl.ds(ws, span), :], (((1,), (1,)), ((), ())),
                          preferred_element_type=f32) + tabw_ref[0]
    kpos = lax.broadcasted_iota(i32, (rows, span), 1) + (i * qb - NSA_WINDOW)
    s_w = jnp.where(kpos >= 0, s_w, NEG)
    m_w = jnp.max(s_w, axis=-1, keepdims=True)
    p_w = jnp.exp(s_w - m_w)
    l_w = jnp.sum(p_w, axis=-1, keepdims=True)
    o_w = ghalf(jnp.dot(p_w.astype(bf16), vw_ref[0, pl.ds(ws, span), :], preferred_element_type=f32))
    o_w = o_w / jnp.maximum(l_w, 1e-30)

    gates = jax.nn.sigmoid(misc_ref[0])
    outs = []
    for h in range(hpg):
        r = slice(h * qb, (h + 1) * qb)
        acc = None
        for jb, branch in enumerate((o_c, o_s, o_w)):
            l0_ = 8 + 3 * h + jb
            l1_ = 8 + 3 * (hpg + h) + jb
            gcol = jnp.where(is_g0, gates[:, l0_:l0_ + 1], gates[:, l1_:l1_ + 1])
            term = gcol * branch[r]
            acc = term if acc is None else acc + term
        outs.append(acc)
    o_ref[0] = jnp.concatenate(outs, axis=1)


LOG2E = 1.4426950408889634


def _nsa_prompt_kernel_h(q_ref, misc_ref, kcvc_ref, kaug_ref, vg_ref, kw_ref, vw_ref,
                         tabs_ref, tabw_ref, tabc_ref, c31_ref, o_ref, qaug_scr, s_scr):
    g = pl.program_id(1)
    i = pl.program_id(2)
    qb, dh, hpg = NSA_QBLOCK, NSA_DH, NSA_HPG
    is_g0 = g == 0
    nn = (((1,), (1,)), ((), ()))

    def ghalf(a):
        return jnp.where(is_g0, a[:, :dh], a[:, dh:])

    def hrows(ref, *lead):
        return [ref[lead + (slice(h * qb, (h + 1) * qb), slice(None))] for h in range(hpg)]

    qt = q_ref[0] * (dh ** -0.5)
    qh = [qt[:, h * dh:(h + 1) * dh] for h in range(hpg)]
    kcvc = kcvc_ref[0]
    nblk = kcvc.shape[0]
    kc = ghalf(kcvc[:, :2 * dh])
    vc = ghalf(kcvc[:, 2 * dh:])

    rows = hpg * qb
    qs = jnp.concatenate(qh, axis=0)
    zero = jnp.zeros_like(qs)
    q2 = jnp.concatenate([jnp.where(is_g0, qs, zero), jnp.where(is_g0, zero, qs)], axis=1)
    q2 = (q2 * LOG2E).astype(bf16)

    blk4 = lax.broadcasted_iota(i32, (rows, nblk), 1)
    qrow4 = lax.broadcasted_iota(i32, (rows, nblk), 0) & (qb - 1)
    first_tab = 2 * i - 2 * NEAR_PAIRS + 1
    shift = (first_tab + 4 * nblk) % nblk
    readable = blk4 * NSA_BLOCK + (NSA_BLOCK - 1) <= i * qb + qrow4
    bias = jnp.where(blk4 < first_tab, c31_ref[0], pltpu.roll(tabc_ref[0], shift, 1))
    s_c = jnp.where(readable, _mm3_nt(qs, kc) + bias, NEG)
    m_c = jnp.max(s_c, axis=-1, keepdims=True)
    p_c = jnp.where(readable, jnp.exp(s_c - m_c), 0.0)
    p_c = p_c / jnp.maximum(jnp.sum(p_c, axis=-1, keepdims=True), 1e-30)
    o_cmp = _mm(p_c, vc)
    imp = p_c[0:qb]
    for h in range(1, hpg):
        imp = imp + p_c[h * qb:(h + 1) * qb]

    span = NSA_WINDOW + qb
    ws = pl.multiple_of(i * qb, qb)
    s_w = lax.dot_general(q2, kw_ref[0, pl.ds(ws, span), :], nn, preferred_element_type=f32) + tabw_ref[0]
    in_seq = lax.broadcasted_iota(i32, (rows, span), 1) + (i * qb - NSA_WINDOW) >= 0
    s_w = jnp.where(in_seq, s_w, NEG)
    m_w = jnp.max(s_w, axis=-1, keepdims=True)
    p_w = jnp.exp2(s_w - m_w)
    l_w = jnp.sum(p_w, axis=-1, keepdims=True)
    o_win = ghalf(jnp.dot(p_w.astype(bf16), vw_ref[0, pl.ds(ws, span), :], preferred_element_type=f32))
    o_win = o_win / jnp.maximum(l_w, 1e-30)

    blk = lax.broadcasted_iota(i32, (qb, nblk), 1)
    qrow = lax.broadcasted_iota(i32, (qb, nblk), 0)
    cur = (i * qb + qrow) >> int(math.log2(NSA_BLOCK))
    valid = blk <= cur
    forced = valid & ((blk == 0) | (blk > cur - NSA_LOCAL))
    imp_t = jnp.where(forced, -jnp.inf, jnp.where(valid, imp, -1.0)).T
    brow = lax.broadcasted_iota(i32, (nblk, qb), 0)
    picked = jnp.zeros((nblk, qb), f32)
    for _ in range(min(NSA_TOPN, nblk) - 1 - NSA_LOCAL):
        mx = jnp.max(imp_t, axis=0, keepdims=True)
        first = jnp.min(jnp.where(imp_t == mx, brow, nblk), axis=0, keepdims=True)
        hit = brow == first
        picked = jnp.where(hit, 1.0, picked)
        imp_t = jnp.where(hit, -jnp.inf, imp_t)
    sel = (forced | (picked.T > 0.5)) & valid
    msel = jnp.where(sel, 0.0, NEG).astype(bf16)
    qaug_scr[...] = jnp.concatenate([q2, jnp.concatenate([msel] * hpg, axis=0)], axis=1)

    t_d = i // 2
    tk = 2 * PAIR

    def scores(t):
        ks = pl.multiple_of(t * tk, tk)
        s = lax.dot_general(qaug_scr[...], kaug_ref[0, pl.ds(ks, tk), :], nn, preferred_element_type=f32)
        i0 = jnp.clip(i - 2 * t, 0, NEAR_PAIRS)
        i1 = jnp.clip(i - 2 * t - 1, 0, NEAR_PAIRS)
        return s + jnp.concatenate([tabs_ref[0, i0], tabs_ref[0, i1]], axis=1)

    s_scr[...] = scores(t_d)

    def tile(k_it, carry):
        m, acc = carry
        t = t_d - k_it
        s = s_scr[...]
        s_next = scores(jnp.maximum(t - 1, 0))
        m_new = jnp.maximum(m, jnp.max(s, axis=-1, keepdims=True))
        alpha = jnp.exp2(m - m_new)
        p = jnp.exp2(s - m_new).astype(bf16)
        ks = pl.multiple_of(t * tk, tk)
        acc = alpha * acc + jnp.dot(p, vg_ref[0, 0, pl.ds(ks, tk), :], preferred_element_type=f32)
        s_scr[...] = s_next
        return m_new, acc

    _, acc_sel = lax.fori_loop(0, t_d + 1, tile, (jnp.full((rows, 1), NEG, f32), jnp.zeros((rows, 2 * dh), f32)))
    o_sel = acc_sel[:, :dh] / jnp.maximum(acc_sel[:, dh:dh + 1], 1e-30)

    gates = jax.nn.sigmoid(misc_ref[0])
    outs = []
    for h in range(hpg):
        r = slice(h * qb, (h + 1) * qb)
        acc = None
        for jb, branch in enumerate((o_cmp[r], o_sel[r], o_win[r])):
            la = 8 + 3 * h + jb
            lb = 8 + 3 * (hpg + h) + jb
            term = jnp.where(is_g0, gates[:, la:la + 1], gates[:, lb:lb + 1]) * branch
            acc = term if acc is None else acc + term
        outs.append(acc)
    o_ref[0] = jnp.concatenate(outs, axis=1)


def _nsa_tables(rel_bias):
    qb, hpg = NSA_QBLOCK, NSA_HPG
    rows = hpg * qb
    q = (np.arange(rows) % qb)[:, None]
    c = np.arange(PAIR)[None, :]
    d_sel = [PAIR * idx + q - c for idx in range(NEAR_PAIRS + 1)]
    assert PAIR * NEAR_PAIRS - (PAIR - 1) >= FAR_DIST
    mm = 2 * NEAR_PAIRS - 1 - c
    d_cmp = np.maximum(np.where(mm >= -1, NSA_BLOCK * mm + q - (NSA_BLOCK - 1), 0), 0)
    assert NSA_BLOCK * (2 * NEAR_PAIRS) - (NSA_BLOCK - 1) >= FAR_DIST
    cw = np.arange(NSA_WINDOW + qb)[None, :]
    d_win = q + NSA_WINDOW - cw
    d_win = np.where((d_win >= 0) & (d_win < NSA_WINDOW), d_win, -1)
    n_win = (NSA_WINDOW + qb) // PAIR
    tiles = d_sel + [d_cmp] + [d_win[:, k * PAIR:(k + 1) * PAIR] for k in range(n_win)]
    dist = np.broadcast_to(np.stack(tiles)[None], (NSA_KV_GROUPS, len(tiles), rows, PAIR))
    tab = _bias_table(rel_bias, jnp.asarray(dist, i32), slab=qb)
    ns = NEAR_PAIRS + 1
    tabs = tab[:, :ns]
    tabc = tab[:, ns]
    tabw = jnp.concatenate([tab[:, ns + 1 + k] for k in range(n_win)], axis=-1)
    c31 = jnp.repeat(rel_bias.astype(f32)[REL_BUCKETS - 1].reshape(NSA_KV_GROUPS, hpg), qb, axis=1)
    return tabs, tabw, tabc, c31.reshape(NSA_KV_GROUPS, rows, 1)


def _nsa_prompt(q_b, misc, kv4, kvwin, rel_bias):
    bsz, t, _ = q_b.shape
    dh, qb = NSA_DH, NSA_QBLOCK
    nblk = t // NSA_BLOCK
    assert nblk == PAIR and t % (2 * PAIR) == 0
    kcvc = _block_means(kv4, tb=512)
    onehot = (jnp.arange(t, dtype=i32)[:, None] // NSA_BLOCK == jnp.arange(nblk, dtype=i32)[None, :]).astype(bf16)
    kaug = jnp.concatenate([kv4[:, :, 4 * dh:6 * dh].astype(bf16),
                            jnp.broadcast_to(onehot[None], (bsz, t, nblk))], axis=-1)
    ones = jnp.ones((bsz, t, dh), bf16)
    vg = jnp.stack([jnp.concatenate([kv4[:, :, (6 + g) * dh:(7 + g) * dh].astype(bf16), ones], axis=-1)
                    for g in range(NSA_KV_GROUPS)], axis=1)
    pad = ((0, 0), (NSA_WINDOW, 0), (0, 0))
    kw = jnp.pad(kvwin[:, :, :2 * dh].astype(bf16), pad)
    vw = jnp.pad(kvwin[:, :, 2 * dh:].astype(bf16), pad)
    tabs, tabw, tabc, c31 = _nsa_tables(rel_bias)
    tabs, tabw = tabs * LOG2E, tabw * LOG2E
    rows = NSA_HPG * qb
    gw = NSA_HPG * dh
    span = NSA_WINDOW + qb
    return pl.pallas_call(
        _nsa_prompt_kernel_h,
        grid=(bsz, NSA_KV_GROUPS, t // qb),
        in_specs=[pl.BlockSpec((1, qb, gw), lambda b, g, i: (b, i, g)),
                  pl.BlockSpec((1, qb, MISC_W), lambda b, g, i: (b, i, 0)),
                  pl.BlockSpec((1, nblk, 4 * dh), lambda b, g, i: (b, 0, 0)),
                  pl.BlockSpec((1, t, 2 * dh + nblk), lambda b, g, i: (b, 0, 0)),
                  pl.BlockSpec((1, 1, t, 2 * dh), lambda b, g, i: (b, g, 0, 0)),
                  pl.BlockSpec((1, t + NSA_WINDOW, 2 * dh), lambda b, g, i: (b, 0, 0)),
                  pl.BlockSpec((1, t + NSA_WINDOW, 2 * dh), lambda b, g, i: (b, 0, 0)),
                  pl.BlockSpec((1, NEAR_PAIRS + 1, rows, PAIR), lambda b, g, i: (g, 0, 0, 0)),
                  pl.BlockSpec((1, rows, span), lambda b, g, i: (g, 0, 0)),
                  pl.BlockSpec((1, rows, PAIR), lambda b, g, i: (g, 0, 0)),
                  pl.BlockSpec((1, rows, 1), lambda b, g, i: (g, 0, 0))],
        out_specs=pl.BlockSpec((1, qb, gw), lambda b, g, i: (b, i, g)),
        out_shape=jax.ShapeDtypeStruct((bsz, t, NSA_HEADS * dh), f32),
        scratch_shapes=[pltpu.VMEM((rows, 2 * dh + nblk), bf16), pltpu.VMEM((rows, 2 * PAIR), f32)],
        compiler_params=_cparams(("parallel", "parallel", "arbitrary")),
        name="nsa_prompt",
    )(q_b, misc, kcvc, kaug, vg, kw, vw, tabs, tabw, tabc, c31)


MEANS_PAGES = 8


def _nsa_s_means_kernel(pt_ref, *refs):
    del pt_ref
    x_refs, o_ref = refs[:-1], refs[-1]
    rows = [[[] for _ in range(NSA_KV_GROUPS)] for _ in range(2)]
    for x_ref in x_refs:
        page = x_ref.shape[1]
        for j in range(2):
            for g in range(NSA_KV_GROUPS):
                x = x_ref[0, :, j, g, :]
                for n in range(page // NSA_BLOCK):
                    rows[j][g].append(jnp.sum(x[n * NSA_BLOCK:(n + 1) * NSA_BLOCK], axis=0, keepdims=True)
                                      * (1.0 / NSA_BLOCK))
    for j in range(2):
        for g in range(NSA_KV_GROUPS):
            o_ref[0, j * NSA_KV_GROUPS + g] = jnp.concatenate(rows[j][g], axis=0)


def _nsa_s_means(cache_kv, page_table):
    db, n_pages = page_table.shape
    page = cache_kv.shape[1]
    nblk = n_pages * page // NSA_BLOCK
    kp = MEANS_PAGES
    rows = kp * page // NSA_BLOCK
    assert n_pages % kp == 0 and rows % 8 == 0

    def page_spec(k):
        return pl.BlockSpec((1, page, 2, NSA_KV_GROUPS, NSA_DH),
                            lambda b, p, pt: (pt[b * n_pages + p * kp + k], 0, 0, 0, 0))

    return pl.pallas_call(
        _nsa_s_means_kernel,
        grid_spec=pltpu.PrefetchScalarGridSpec(
            num_scalar_prefetch=1,
            grid=(db, n_pages // kp),
            in_specs=[page_spec(k) for k in range(kp)],
            out_specs=pl.BlockSpec((1, 2 * NSA_KV_GROUPS, rows, NSA_DH), lambda b, p, pt: (b, 0, p, 0))),
        out_shape=jax.ShapeDtypeStruct((db, 2 * NSA_KV_GROUPS, nblk, NSA_DH), f32),
        compiler_params=_cparams(("parallel", "parallel")),
        name="nsa_sample_means",
    )(page_table.reshape(-1), *([cache_kv] * kp))


def _nsa_s_scores_kernel(q_ref, kcvc_ref, win_ref, kvn_ref, tc_ref, tw_ref, b0_ref, oc_ref, ow_ref, sel_ref):
    dh, hpg = NSA_DH, NSA_HPG
    q = q_ref[0] * (dh ** -0.5)
    nblk = kcvc_ref.shape[2]
    kvn = kvn_ref[0]
    row = lax.broadcasted_iota(i32, (NSA_HEADS, 1), 0)
    lane = lax.broadcasted_iota(i32, (1, nblk), 1)
    n_sel = NSA_TOPN - 1
    for g in range(NSA_KV_GROUPS):
        in_g = (row >= g * hpg) & (row < (g + 1) * hpg)
        kc = kcvc_ref[0, g]
        vc = kcvc_ref[0, NSA_KV_GROUPS + g]
        s_c = _mm3_nt(q, kc) + tc_ref[...]
        m_c = jnp.max(s_c, axis=-1, keepdims=True)
        p_c = jnp.exp(s_c - m_c)
        p_c = p_c / jnp.maximum(jnp.sum(p_c, axis=-1, keepdims=True), 1e-30)
        o_c = _mm(p_c, vc)
        imp = jnp.sum(jnp.where(in_g, p_c, 0.0), axis=0, keepdims=True)
        forced_blocks = [0] + [nblk - k for k in range(1, NSA_LOCAL)]
        imp = jnp.where((lane == 0) | (lane > nblk - NSA_LOCAL), -jnp.inf, imp)
        picks = jnp.zeros((1, nblk), i32)
        for it, fb in enumerate(forced_blocks):
            picks = jnp.where(lane == it, fb, picks)
        for it in range(len(forced_blocks), n_sel):
            mx = jnp.max(imp, axis=-1, keepdims=True)
            first = jnp.min(jnp.where(imp == mx, lane, nblk), axis=-1, keepdims=True)
            picks = jnp.where(lane == it, first, picks)
            imp = jnp.where(lane == first, -jnp.inf, imp)
        sel_ref[0, g:g + 1, :] = picks
        kw = win_ref[0, :, 0, g, :]
        vw = win_ref[0, :, 1, g, :]
        s_w = _mm3_nt(q, kw) + tw_ref[...]
        s_n = jnp.sum(q * kvn[:, g * dh:(g + 1) * dh], axis=-1, keepdims=True) + b0_ref[...]
        m_w = jnp.maximum(jnp.max(s_w, axis=-1, keepdims=True), s_n)
        p_w = jnp.exp(s_w - m_w)
        p_n = jnp.exp(s_n - m_w)
        l_w = jnp.sum(p_w, axis=-1, keepdims=True) + p_n
        o_w = (_mm(p_w, vw) + p_n * kvn[:, (NSA_KV_GROUPS + g) * dh:(NSA_KV_GROUPS + g + 1) * dh]) / jnp.maximum(l_w, 1e-30)
        if g == 0:
            oc_ref[0] = o_c
            ow_ref[0] = o_w
        else:
            oc_ref[0] = jnp.where(in_g, o_c, oc_ref[0])
            ow_ref[0] = jnp.where(in_g, o_w, ow_ref[0])


def _nsa_s_sel_kernel(sel_ref, pt_ref, q_ref, k_ref, v_ref, tb_ref, kvn_ref, b0_ref, oc_ref, ow_ref, gt_ref,
                      o_ref, m_scr, l_scr, acc_scr):
    del sel_ref, pt_ref
    g = pl.program_id(1)
    j = pl.program_id(2)
    dh, hpg = NSA_DH, NSA_HPG
    is_g0 = g == 0
    q = q_ref[0] * (dh ** -0.5)
    kvn = kvn_ref[0]

    @pl.when(j == 0)
    def _():
        k_n = jnp.where(is_g0, kvn[:, 4 * dh:5 * dh], kvn[:, 5 * dh:6 * dh])
        v_n = jnp.where(is_g0, kvn[:, 6 * dh:7 * dh], kvn[:, 7 * dh:8 * dh])
        m_scr[...] = jnp.sum(q * k_n, axis=-1, keepdims=True) + b0_ref[...]
        l_scr[...] = jnp.ones_like(l_scr)
        acc_scr[...] = jnp.broadcast_to(v_n, acc_scr.shape)

    k = jnp.where(is_g0, k_ref[0, :, 0, 0, :], k_ref[0, :, 0, 1, :])
    v = jnp.where(is_g0, v_ref[0, :, 0, 0, :], v_ref[0, :, 0, 1, :])
    s = _mm3_nt(q, k) + tb_ref[0]
    m_old = m_scr[...]
    m_new = jnp.maximum(m_old, jnp.max(s, axis=-1, keepdims=True))
    alpha = jnp.exp(m_old - m_new)
    p = jnp.exp(s - m_new)
    l_scr[...] = alpha * l_scr[...] + jnp.sum(p, axis=-1, keepdims=True)
    acc_scr[...] = alpha * acc_scr[...] + _mm(p, v)
    m_scr[...] = m_new

    @pl.when(j == pl.num_programs(2) - 1)
    def _():
        o_s = acc_scr[...] / jnp.maximum(l_scr[...], 1e-30)
        gt = gt_ref[0]
        o = gt[:, 0:1] * oc_ref[0] + gt[:, 1:2] * o_s + gt[:, 2:3] * ow_ref[0]
        row = lax.broadcasted_iota(i32, o.shape, 0)
        in_g = (row >= g * hpg) & (row < (g + 1) * hpg)

        @pl.when(is_g0)
        def _():
            o_ref[0] = o

        @pl.when(jnp.logical_not(is_g0))
        def _():
            o_ref[0] = jnp.where(in_g, o, o_ref[0])


def _nsa_sample(q_b, gates, kv4_new, kvwin_new, cache_kv, cache_win, page_table, rel_bias):
    db = q_b.shape[0]
    dh = NSA_DH
    n_pool, page = cache_kv.shape[:2]
    n_pages = page_table.shape[1]
    past = n_pages * page
    nblk = past // NSA_BLOCK
    wb = cache_win.shape[1]
    assert nblk == PAIR and page % NSA_BLOCK == 0 and wb == NSA_WINDOW
    kcvc = _nsa_s_means(cache_kv, page_table)

    n = np.arange(nblk)
    d_cmp = past - (n * NSA_BLOCK + NSA_BLOCK - 1)
    jw = np.arange(wb)
    d_win = np.where(jw >= 1, wb - jw, -1)
    pos = np.arange(nblk * NSA_BLOCK)
    d_sel = past - pos
    width = nblk + wb + nblk * NSA_BLOCK
    dist = np.broadcast_to(np.concatenate([d_cmp, d_win, d_sel])[None, :], (NSA_HEADS, width))
    tab = _bias_table(rel_bias, jnp.asarray(dist[None, None], i32), slab=1)[0, 0]
    t_cmp, t_win = tab[:, :nblk], tab[:, nblk:nblk + wb]
    t_sel = jnp.transpose(tab[:, nblk + wb:].reshape(NSA_HEADS, nblk, NSA_BLOCK), (1, 0, 2))
    b0 = rel_bias.astype(f32)[0].reshape(NSA_HEADS, 1)

    q3 = q_b.reshape(db, NSA_HEADS, dh)
    o_c, o_w, sel = pl.pallas_call(
        _nsa_s_scores_kernel,
        grid=(db,),
        in_specs=[pl.BlockSpec((1, NSA_HEADS, dh), lambda b: (b, 0, 0)),
                  pl.BlockSpec((1, 2 * NSA_KV_GROUPS, nblk, dh), lambda b: (b, 0, 0, 0)),
                  pl.BlockSpec((1, wb, 2, NSA_KV_GROUPS, dh), lambda b: (b, 0, 0, 0, 0)),
                  pl.BlockSpec((1, 1, 4 * dh), lambda b: (b, 0, 0)),
                  pl.BlockSpec((NSA_HEADS, nblk), lambda b: (0, 0)),
                  pl.BlockSpec((NSA_HEADS, wb), lambda b: (0, 0)),
                  pl.BlockSpec((NSA_HEADS, 1), lambda b: (0, 0))],
        out_specs=[pl.BlockSpec((1, NSA_HEADS, dh), lambda b: (b, 0, 0)),
                   pl.BlockSpec((1, NSA_HEADS, dh), lambda b: (b, 0, 0)),
                   pl.BlockSpec((1, NSA_KV_GROUPS, nblk), lambda b: (b, 0, 0))],
        out_shape=[jax.ShapeDtypeStruct((db, NSA_HEADS, dh), f32),
                   jax.ShapeDtypeStruct((db, NSA_HEADS, dh), f32),
                   jax.ShapeDtypeStruct((db, NSA_KV_GROUPS, nblk), i32)],
        compiler_params=_cparams(("parallel",)),
        name="nsa_sample_scores",
    )(q3, kcvc, cache_win.astype(f32), kvwin_new.reshape(db, 1, 4 * dh), t_cmp, t_win, b0)

    n_sel = NSA_TOPN - 1
    sel_flat = sel[:, :, :n_sel].reshape(-1)
    halves = page // NSA_BLOCK
    cache_h = cache_kv.reshape(n_pool * halves, NSA_BLOCK, 4, NSA_KV_GROUPS, dh)

    def blk_of(b, g, j, sel_r, pt_r):
        nb = sel_r[(b * NSA_KV_GROUPS + g) * n_sel + j]
        return nb, pt_r[b * n_pages + nb // halves] * halves + nb % halves

    o = pl.pallas_call(
        _nsa_s_sel_kernel,
        grid_spec=pltpu.PrefetchScalarGridSpec(
            num_scalar_prefetch=2,
            grid=(db, NSA_KV_GROUPS, n_sel),
            in_specs=[pl.BlockSpec((1, NSA_HEADS, dh), lambda b, g, j, s, p: (b, 0, 0)),
                      pl.BlockSpec((1, NSA_BLOCK, 1, NSA_KV_GROUPS, dh),
                                   lambda b, g, j, s, p: (blk_of(b, g, j, s, p)[1], 0, 2, 0, 0)),
                      pl.BlockSpec((1, NSA_BLOCK, 1, NSA_KV_GROUPS, dh),
                                   lambda b, g, j, s, p: (blk_of(b, g, j, s, p)[1], 0, 3, 0, 0)),
                      pl.BlockSpec((1, NSA_HEADS, NSA_BLOCK), lambda b, g, j, s, p: (blk_of(b, g, j, s, p)[0], 0, 0)),
                      pl.BlockSpec((1, 1, 8 * dh), lambda b, g, j, s, p: (b, 0, 0)),
                      pl.BlockSpec((NSA_HEADS, 1), lambda b, g, j, s, p: (0, 0)),
                      pl.BlockSpec((1, NSA_HEADS, dh), lambda b, g, j, s, p: (b, 0, 0)),
                      pl.BlockSpec((1, NSA_HEADS, dh), lambda b, g, j, s, p: (b, 0, 0)),
                      pl.BlockSpec((1, NSA_HEADS, 3), lambda b, g, j, s, p: (b, 0, 0))],
            out_specs=pl.BlockSpec((1, NSA_HEADS, dh), lambda b, g, j, s, p: (b, 0, 0)),
            scratch_shapes=[pltpu.VMEM((NSA_HEADS, 1), f32), pltpu.VMEM((NSA_HEADS, 1), f32),
                            pltpu.VMEM((NSA_HEADS, dh), f32)]),
        out_shape=jax.ShapeDtypeStruct((db, NSA_HEADS, dh), f32),
        compiler_params=_cparams(("parallel", "arbitrary", "arbitrary")),
        name="nsa_sample_selected",
    )(sel_flat, page_table.reshape(-1), q3, cache_h, cache_h, t_sel, kv4_new.reshape(db, 1, 8 * dh), b0,
      o_c, o_w, gates)
    return o.reshape(db, NSA_HEADS * dh)


def _nsa_st_means_kernel(pt_ref, *refs):
    del pt_ref
    x_refs, o_ref = refs[:-1], refs[-1]
    p = pl.program_id(1)

    @pl.when(p == 0)
    def _():
        o_ref[...] = jnp.zeros_like(o_ref)

    page = x_refs[0].shape[-1]
    per_page = page // NSA_BLOCK
    lane = lax.broadcasted_iota(i32, (NSA_DH, page), 1)
    out_lane = lax.broadcasted_iota(i32, (NSA_DH, o_ref.shape[-1]), 1)
    for j in range(2):
        for g in range(NSA_KV_GROUPS):
            acc = o_ref[0, j * NSA_KV_GROUPS + g]
            for k, x_ref in enumerate(x_refs):
                x = x_ref[0, j, g]
                for n in range(per_page):
                    in_blk = (lane >= n * NSA_BLOCK) & (lane < (n + 1) * NSA_BLOCK)
                    mean = jnp.sum(jnp.where(in_blk, x, 0.0), axis=1, keepdims=True) * (1.0 / NSA_BLOCK)
                    col = (p * len(x_refs) + k) * per_page + n
                    acc = jnp.where(out_lane == col, mean, acc)
            o_ref[0, j * NSA_KV_GROUPS + g] = acc


def _nsa_st_means(cache_t, page_table):
    db, n_pages = page_table.shape
    page = cache_t.shape[-1]
    nblk = n_pages * page // NSA_BLOCK
    kp = MEANS_PAGES
    assert n_pages % kp == 0

    def page_spec(k):
        return pl.BlockSpec((1, 2, NSA_KV_GROUPS, NSA_DH, page),
                            lambda b, p, pt: (pt[b * n_pages + p * kp + k], 0, 0, 0, 0))

    return pl.pallas_call(
        _nsa_st_means_kernel,
        grid_spec=pltpu.PrefetchScalarGridSpec(
            num_scalar_prefetch=1,
            grid=(db, n_pages // kp),
            in_specs=[page_spec(k) for k in range(kp)],
            out_specs=pl.BlockSpec((1, 2 * NSA_KV_GROUPS, NSA_DH, nblk), lambda b, p, pt: (b, 0, 0, 0))),
        out_shape=jax.ShapeDtypeStruct((db, 2 * NSA_KV_GROUPS, NSA_DH, nblk), f32),
        compiler_params=_cparams(("parallel", "arbitrary")),
        name="nsa_sample_means",
    )(page_table.reshape(-1), *([cache_t] * kp))


def _nsa_st_scores_kernel(q_ref, kcvc_ref, win_ref, kvn_ref, tc_ref, tw_ref, b0_ref, oc_ref, ow_ref, sel_ref):
    dh, hpg = NSA_DH, NSA_HPG
    q = q_ref[0] * (dh ** -0.5)
    nblk = kcvc_ref.shape[-1]
    kvn = kvn_ref[0]
    row = lax.broadcasted_iota(i32, (NSA_HEADS, 1), 0)
    lane = lax.broadcasted_iota(i32, (1, nblk), 1)
    n_sel = NSA_TOPN - 1
    for g in range(NSA_KV_GROUPS):
        in_g = (row >= g * hpg) & (row < (g + 1) * hpg)
        s_c = _mm3(q, kcvc_ref[0, g]) + tc_ref[...]
        m_c = jnp.max(s_c, axis=-1, keepdims=True)
        p_c = jnp.exp(s_c - m_c)
        p_c = p_c / jnp.maximum(jnp.sum(p_c, axis=-1, keepdims=True), 1e-30)
        o_c = _mm_nt(p_c, kcvc_ref[0, NSA_KV_GROUPS + g])
        imp = jnp.sum(jnp.where(in_g, p_c, 0.0), axis=0, keepdims=True)
        forced_blocks = [0] + [nblk - k for k in range(1, NSA_LOCAL)]
        imp = jnp.where((lane == 0) | (lane > nblk - NSA_LOCAL), -jnp.inf, imp)
        picks = jnp.zeros((1, nblk), i32)
        for it, fb in enumerate(forced_blocks):
            picks = jnp.where(lane == it, fb, picks)
        for it in range(len(forced_blocks), n_sel):
            mx = jnp.max(imp, axis=-1, keepdims=True)
            first = jnp.min(jnp.where(imp == mx, lane, nblk), axis=-1, keepdims=True)
            picks = jnp.where(lane == it, first, picks)
            imp = jnp.where(lane == first, -jnp.inf, imp)
        sel_ref[0, g:g + 1, :] = picks
        s_w = _mm3(q, win_ref[0, 0, g]) + tw_ref[...]
        s_n = jnp.sum(q * kvn[:, g * dh:(g + 1) * dh], axis=-1, keepdims=True) + b0_ref[...]
        m_w = jnp.maximum(jnp.max(s_w, axis=-1, keepdims=True), s_n)
        p_w = jnp.exp(s_w - m_w)
        p_n = jnp.exp(s_n - m_w)
        l_w = jnp.sum(p_w, axis=-1, keepdims=True) + p_n
        v_n = kvn[:, (NSA_KV_GROUPS + g) * dh:(NSA_KV_GROUPS + g + 1) * dh]
        o_w = (_mm_nt(p_w, win_ref[0, 1, g]) + p_n * v_n) / jnp.maximum(l_w, 1e-30)
        if g == 0:
            oc_ref[0] = o_c
            ow_ref[0] = o_w
        else:
            oc_ref[0] = jnp.where(in_g, o_c, oc_ref[0])
            ow_ref[0] = jnp.where(in_g, o_w, ow_ref[0])


def _nsa_st_sel_kernel(sel_ref, pt_ref, q_ref, rb_ref, kvn_ref, oc_ref, ow_ref, gt_ref, *refs, n_sel, past):
    del pt_ref
    k_refs, v_refs, o_ref = refs[:n_sel], refs[n_sel:2 * n_sel], refs[2 * n_sel]
    b = pl.program_id(0)
    g = pl.program_id(1)
    dh, hpg = NSA_DH, NSA_HPG
    is_g0 = g == 0
    page = k_refs[0].shape[-1]
    per_page = page // NSA_BLOCK
    q = q_ref[0] * (dh ** -0.5)
    kvn = kvn_ref[0]
    lane = lax.broadcasted_iota(i32, (NSA_HEADS, page), 1)
    s_parts, d_parts = [], []
    for j in range(n_sel):
        nb = sel_ref[(b * NSA_KV_GROUPS + g) * n_sel + j]
        in_blk = (lane >> int(math.log2(NSA_BLOCK))) == (nb % per_page)
        d_parts.append(jnp.where(in_blk, past - ((nb // per_page) * page + lane), -1))
        s_parts.append(_mm3(q, k_refs[j][0, 0, 0]))
    d = jnp.concatenate(d_parts, axis=1)
    bias = jnp.broadcast_to(rb_ref[0], d.shape)
    for k in range(1, REL_BUCKETS):
        bias = jnp.where(d >= _BUCKET_THR[k - 1], rb_ref[k], bias)
    s = jnp.concatenate(s_parts, axis=1) + jnp.where(d < 0, NEG, bias)
    k_n = jnp.where(is_g0, kvn[:, 4 * dh:5 * dh], kvn[:, 5 * dh:6 * dh])
    v_n = jnp.where(is_g0, kvn[:, 6 * dh:7 * dh], kvn[:, 7 * dh:8 * dh])
    s_n = jnp.sum(q * k_n, axis=-1, keepdims=True) + rb_ref[0]
    m = jnp.maximum(jnp.max(s, axis=-1, keepdims=True), s_n)
    p = jnp.exp(s - m)
    p_n = jnp.exp(s_n - m)
    l = jnp.sum(p, axis=-1, keepdims=True) + p_n
    acc = p_n * v_n
    for j in range(n_sel):
        acc = acc + _mm_nt(p[:, j * page:(j + 1) * page], v_refs[j][0, 0, 0])
    o_s = acc / jnp.maximum(l, 1e-30)
    gt = gt_ref[0]
    o = gt[:, 0:1] * oc_ref[0] + gt[:, 1:2] * o_s + gt[:, 2:3] * ow_ref[0]
    row = lax.broadcasted_iota(i32, o.shape, 0)
    in_g = (row >= g * hpg) & (row < (g + 1) * hpg)

    @pl.when(is_g0)
    def _():
        o_ref[0] = o

    @pl.when(jnp.logical_not(is_g0))
    def _():
        o_ref[0] = jnp.where(in_g, o, o_ref[0])


def _nsa_sample_t(q_b, gates, kv4_new, kvwin_new, cache_kv, cache_win, page_table, rel_bias):
    db = q_b.shape[0]
    dh = NSA_DH
    n_pool, page = cache_kv.shape[:2]
    n_pages = page_table.shape[1]
    past = n_pages * page
    nblk = past // NSA_BLOCK
    wb = cache_win.shape[1]
    assert nblk == PAIR and page % NSA_BLOCK == 0 and wb == NSA_WINDOW and nblk > NSA_LOCAL
    cache_t = jnp.transpose(cache_kv, (0, 2, 3, 4, 1)).astype(f32)
    win_t = jnp.transpose(cache_win, (0, 2, 3, 4, 1)).astype(f32)
    kcvc = _nsa_st_means(cache_t, page_table)

    n = np.arange(nblk)
    d_cmp = past - (n * NSA_BLOCK + NSA_BLOCK - 1)
    jw = np.arange(wb)
    d_win = np.where(jw >= 1, wb - jw, -1)
    dist = np.broadcast_to(np.concatenate([d_cmp, d_win])[None, :], (NSA_HEADS, nblk + wb))
    tab = _bias_table(rel_bias, jnp.asarray(dist[None, None], i32), slab=1)[0, 0]
    t_cmp, t_win = tab[:, :nblk], tab[:, nblk:]
    rb = rel_bias.astype(f32)
    b0 = rb[0].reshape(NSA_HEADS, 1)

    q3 = q_b.reshape(db, NSA_HEADS, dh)
    o_c, o_w, sel = pl.pallas_call(
        _nsa_st_scores_kernel,
        grid=(db,),
        in_specs=[pl.BlockSpec((1, NSA_HEADS, dh), lambda b: (b, 0, 0)),
                  pl.BlockSpec((1, 2 * NSA_KV_GROUPS, dh, nblk), lambda b: (b, 0, 0, 0)),
                  pl.BlockSpec((1, 2, NSA_KV_GROUPS, dh, wb), lambda b: (b, 0, 0, 0, 0)),
                  pl.BlockSpec((1, 1, 4 * dh), lambda b: (b, 0, 0)),
                  pl.BlockSpec((NSA_HEADS, nblk), lambda b: (0, 0)),
                  pl.BlockSpec((NSA_HEADS, wb), lambda b: (0, 0)),
                  pl.BlockSpec((NSA_HEADS, 1), lambda b: (0, 0))],
        out_specs=[pl.BlockSpec((1, NSA_HEADS, dh), lambda b: (b, 0, 0)),
                   pl.BlockSpec((1, NSA_HEADS, dh), lambda b: (b, 0, 0)),
                   pl.BlockSpec((1, NSA_KV_GROUPS, nblk), lambda b: (b, 0, 0))],
        out_shape=[jax.ShapeDtypeStruct((db, NSA_HEADS, dh), f32),
                   jax.ShapeDtypeStruct((db, NSA_HEADS, dh), f32),
                   jax.ShapeDtypeStruct((db, NSA_KV_GROUPS, nblk), i32)],
        compiler_params=_cparams(("parallel",)),
        name="nsa_sample_scores",
    )(q3, kcvc, win_t, kvwin_new.reshape(db, 1, 4 * dh), t_cmp, t_win, b0)

    n_sel = NSA_TOPN - 1
    sel_flat = sel[:, :, :n_sel].reshape(-1)
    per_page = page // NSA_BLOCK

    def page_spec(j, plane):
        def imap(b, g, s, p):
            nb = s[(b * NSA_KV_GROUPS + g) * n_sel + j]
            return (p[b * n_pages + nb // per_page], plane, g, 0, 0)
        return pl.BlockSpec((1, 1, 1, dh, page), imap)

    const = lambda shape: pl.BlockSpec(shape, lambda b, g, s, p: (0,) * len(shape))
    per_b = lambda shape: pl.BlockSpec((1,) + shape, lambda b, g, s, p: (b,) + (0,) * len(shape))
    o = pl.pallas_call(
        functools.partial(_nsa_st_sel_kernel, n_sel=n_sel, past=past),
        grid_spec=pltpu.PrefetchScalarGridSpec(
            num_scalar_prefetch=2,
            grid=(db, NSA_KV_GROUPS),
            in_specs=[per_b((NSA_HEADS, dh)), const((REL_BUCKETS, NSA_HEADS, 1)), per_b((1, 8 * dh)),
                      per_b((NSA_HEADS, dh)), per_b((NSA_HEADS, dh)), per_b((NSA_HEADS, 3))]
                     + [page_spec(j, 2) for j in range(n_sel)] + [page_spec(j, 3) for j in range(n_sel)],
            out_specs=per_b((NSA_HEADS, dh))),
        out_shape=jax.ShapeDtypeStruct((db, NSA_HEADS, dh), f32),
        compiler_params=_cparams(("parallel", "arbitrary")),
        name="nsa_sample_selected",
    )(sel_flat, page_table.reshape(-1), q3, rb.reshape(REL_BUCKETS, NSA_HEADS, 1), kv4_new.reshape(db, 1, 8 * dh),
      o_c, o_w, gates, *([cache_t] * (2 * n_sel)))
    return o.reshape(db, NSA_HEADS * dh)


def _gelu_tanh(x):
    return 0.5 * x * (1.0 + jnp.tanh(math.sqrt(2.0 / math.pi) * (x + 0.044715 * (x * x * x))))


def _lru_gates(xc, wa_ref, wx_ref, ba_ref, bx_ref, lam_ref, prec):
    r_parts, i_parts = [], []
    for n in range(RNN_BLOCKS):
        xb = xc[:, n * RNN_BW:(n + 1) * RNN_BW]
        r_parts.append(_dotp(xb, wa_ref[n], prec))
        i_parts.append(_dotp(xb, wx_ref[n], prec))
    r = jax.nn.sigmoid(jnp.concatenate(r_parts, axis=1) + ba_ref[...])
    i = jax.nn.sigmoid(jnp.concatenate(i_parts, axis=1) + bx_ref[...])
    log_a = -RG_C * r * _softplus(-lam_ref[...])
    a = jnp.exp(log_a)
    t = jnp.tanh(log_a)
    b = jnp.sqrt(jnp.maximum(-2.0 * t / (1.0 - t), 0.0)) * (i * xc)
    return a, b


def _lru_kernel(rec_ref, gate_ref, cw_ref, cb_ref, wa_ref, wx_ref, ba_ref, bx_ref, lam_ref, cinit_ref, h0_ref,
                y_ref, hfin_ref, xc_scr, a_scr, b_scr, hs_scr, h_scr, *, tb):
    j = pl.program_id(1)

    @pl.when(j == 0)
    def _():
        xc_scr[0:8, :] = cinit_ref[0]
        h_scr[...] = h0_ref[0]

    x = rec_ref[0]
    xc_scr[8:8 + tb, :] = x
    xc = xc_scr[5:5 + tb, :] * cw_ref[0:1, :]
    xc = xc + xc_scr[6:6 + tb, :] * cw_ref[1:2, :]
    xc = xc + xc_scr[7:7 + tb, :] * cw_ref[2:3, :]
    xc = xc + x * cw_ref[3:4, :]
    xc = xc + cb_ref[...]
    xc_scr[0:8, :] = xc_scr[tb:tb + 8, :]
    a, b = _lru_gates(xc, wa_ref, wx_ref, ba_ref, bx_ref, lam_ref, 1)
    a_scr[...] = a
    b_scr[...] = b

    def step(t, h):
        h = a_scr[pl.ds(t, 1), :] * h + b_scr[pl.ds(t, 1), :]
        hs_scr[pl.ds(t, 1), :] = h
        return h

    h = lax.fori_loop(0, tb, step, h_scr[...], unroll=8)
    h_scr[...] = h
    y_ref[0] = _gelu_tanh(gate_ref[0]) * hs_scr[...]

    @pl.when(j == pl.num_programs(1) - 1)
    def _():
        hfin_ref[0] = h


def _lru_prompt(rec, gate, conv_w, conv_b, wa, wx, ba, bx, lam, *, tb):
    bsz, t, w = rec.shape
    row = lambda a: a.reshape(1, w).astype(f32)
    cinit = jnp.zeros((bsz, 8, w), f32)
    h0 = jnp.zeros((bsz, 1, w), f32)
    full = lambda shape: pl.BlockSpec(shape, lambda b, j: (0,) * len(shape))
    return pl.pallas_call(
        functools.partial(_lru_kernel, tb=tb),
        grid=(bsz, t // tb),
        in_specs=[pl.BlockSpec((1, tb, w), lambda b, j: (b, j, 0)),
                  pl.BlockSpec((1, tb, w), lambda b, j: (b, j, 0)),
                  full((RNN_CONV, w)), full((1, w)),
                  full((RNN_BLOCKS, RNN_BW, RNN_BW)), full((RNN_BLOCKS, RNN_BW, RNN_BW)),
                  full((1, w)), full((1, w)), full((1, w)),
                  pl.BlockSpec((1, 8, w), lambda b, j: (b, 0, 0)),
                  pl.BlockSpec((1, 1, w), lambda b, j: (b, 0, 0))],
        out_specs=[pl.BlockSpec((1, tb, w), lambda b, j: (b, j, 0)),
                   pl.BlockSpec((1, 1, w), lambda b, j: (b, 0, 0))],
        out_shape=[jax.ShapeDtypeStruct((bsz, t, w), f32), jax.ShapeDtypeStruct((bsz, 1, w), f32)],
        scratch_shapes=[pltpu.VMEM((tb + 8, w), f32), pltpu.VMEM((tb, w), f32), pltpu.VMEM((tb, w), f32),
                        pltpu.VMEM((tb, w), f32), pltpu.VMEM((1, w), f32)],
        compiler_params=_cparams(("parallel", "arbitrary")),
        name="rglru",
    )(rec, gate, conv_w.astype(f32), row(conv_b), wa.astype(f32), wx.astype(f32), row(ba), row(bx), row(lam),
      cinit, h0)


def _lru_step_kernel(rec_ref, gate_ref, b0_ref, b1_ref, b2_ref, cw_ref, cb_ref, wa_ref, wx_ref, ba_ref, bx_ref,
                     lam_ref, h0_ref, y_ref, h_ref):
    xc = b0_ref[...] * cw_ref[0:1, :]
    xc = xc + b1_ref[...] * cw_ref[1:2, :]
    xc = xc + b2_ref[...] * cw_ref[2:3, :]
    xc = xc + rec_ref[...] * cw_ref[3:4, :]
    xc = xc + cb_ref[...]
    a, b = _lru_gates(xc, wa_ref, wx_ref, ba_ref, bx_ref, lam_ref, 3)
    h = a * h0_ref[...] + b
    h_ref[...] = h
    y_ref[...] = _gelu_tanh(gate_ref[...]) * h


def _lru_sample(rec, gate, conv_buf, h0, conv_w, conv_b, wa, wx, ba, bx, lam):
    db, w = rec.shape
    row = lambda a: a.reshape(1, w).astype(f32)
    buf = conv_buf.astype(f32)
    return pl.pallas_call(
        _lru_step_kernel,
        out_shape=[jax.ShapeDtypeStruct((db, w), f32), jax.ShapeDtypeStruct((db, w), f32)],
        compiler_params=pltpu.CompilerParams(vmem_limit_bytes=VMEM_LIMIT),
        name="rglru_step",
    )(rec, gate, buf[:, 0], buf[:, 1], buf[:, 2], conv_w.astype(f32), row(conv_b), wa.astype(f32), wx.astype(f32),
      row(ba), row(bx), row(lam), h0.astype(f32))


def _top2_kernel(l_ref, e_ref, g_ref):
    lg = l_ref[...]
    lane = lax.broadcasted_iota(i32, lg.shape, 1)
    lg = jnp.where(lane < N_EXPERTS, lg, -jnp.inf)
    m1 = jnp.max(lg, axis=-1, keepdims=True)
    i1 = jnp.min(jnp.where(lg == m1, lane, lg.shape[1]), axis=-1, keepdims=True)
    lg2 = jnp.where(lane == i1, -jnp.inf, lg)
    m2 = jnp.max(lg2, axis=-1, keepdims=True)
    i2 = jnp.min(jnp.where(lg2 == m2, lane, lg.shape[1]), axis=-1, keepdims=True)
    e2 = jnp.exp(m2 - m1)
    den = 1.0 + e2
    e_ref[...] = jnp.where(lane == 0, i1, jnp.where(lane == 1, i2, 0))
    g_ref[...] = jnp.where(lane == 0, 1.0 / den, jnp.where(lane == 1, e2 / den, 0.0))


def _top2(logits, *, tm):
    n, w = logits.shape
    return pl.pallas_call(
        _top2_kernel,
        grid=(n // tm,),
        in_specs=[pl.BlockSpec((tm, w), lambda i: (i, 0))],
        out_specs=[pl.BlockSpec((tm, w), lambda i: (i, 0)), pl.BlockSpec((tm, w), lambda i: (i, 0))],
        out_shape=[jax.ShapeDtypeStruct((n, w), i32), jax.ShapeDtypeStruct((n, w), f32)],
        compiler_params=_cparams(("parallel",)),
        name="moe_top2",
    )(logits)


def _moe_kernel(be_ref, nu_ref, xs_ref, w1_ref, w3_ref, w2_ref, o_ref, acc_scr):
    del be_ref
    i = pl.program_id(0)
    f = pl.program_id(1)
    last = pl.num_programs(1) - 1
    used = i < nu_ref[0]

    @pl.when(used)
    def _():
        @pl.when(f == 0)
        def _():
            acc_scr[...] = jnp.zeros_like(acc_scr)

        x = xs_ref[...]
        act = _silu(_mm(x, w1_ref[0])) * _mm(x, w3_ref[0])
        acc_scr[...] += _mm(act, w2_ref[0])

        @pl.when(f == last)
        def _():
            o_ref[...] = acc_scr[...]

    @pl.when(jnp.logical_not(used) & (f == last))
    def _():
        o_ref[...] = jnp.zeros_like(o_ref)


def _moe_experts(xs, blk_e, n_used, w1, w3, w2, *, tm, tf):
    rows, d = xs.shape
    fdim = w1.shape[2]
    nf = fdim // tf

    def wcol(i, f, be, nu):
        return (be[i], 0, jnp.where(i < nu[0], f, nf - 1))

    def wrow(i, f, be, nu):
        return (be[i], jnp.where(i < nu[0], f, nf - 1), 0)

    return pl.pallas_call(
        _moe_kernel,
        grid_spec=pltpu.PrefetchScalarGridSpec(
            num_scalar_prefetch=2,
            grid=(rows // tm, nf),
            in_specs=[pl.BlockSpec((tm, d), lambda i, f, be, nu: (i, 0)),
                      pl.BlockSpec((1, d, tf), wcol),
                      pl.BlockSpec((1, d, tf), wcol),
                      pl.BlockSpec((1, tf, d), wrow)],
            out_specs=pl.BlockSpec((tm, d), lambda i, f, be, nu: (i, 0)),
            scratch_shapes=[pltpu.VMEM((tm, d), f32)]),
        out_shape=jax.ShapeDtypeStruct((rows, d), f32),
        compiler_params=_cparams(("arbitrary", "arbitrary")),
        name="moe_experts",
    )(blk_e, n_used, xs, w1, w3, w2)


def _combine_kernel(x_ref, gate_ref, y_ref, g_ref, nw_ref, o_ref):
    gw = g_ref[0]
    d = x_ref.shape[-1]
    y = y_ref[0, :, :d] * gw[:, 0:1] + y_ref[0, :, d:] * gw[:, 1:2]
    x = x_ref[0] + gate_ref[0] * y
    o_ref[0] = x * lax.rsqrt(jnp.mean(x * x, axis=-1, keepdims=True) + EPS) * nw_ref[...]


def _moe_combine(x, gate, y01, gw, norm_w, *, tm):
    bsz, t, d = x.shape
    tok = lambda w: pl.BlockSpec((1, tm, w), lambda b, i: (b, i, 0))
    return pl.pallas_call(
        _combine_kernel,
        grid=(bsz, t // tm),
        in_specs=[tok(d), _mod_spec(gate, tm), tok(TOP_K * d), tok(gw.shape[-1]),
                  pl.BlockSpec((1, d), lambda b, i: (0, 0))],
        out_specs=tok(d),
        out_shape=jax.ShapeDtypeStruct((bsz, t, d), f32),
        compiler_params=_cparams(("parallel", "parallel")),
        name="moe_combine",
    )(x, gate, y01, gw, norm_w.reshape(1, d).astype(f32))


MOE_TM = 1024
MOE_TF = 512


def _moe_dispatch(e_idx, n_tok):
    tm = MOE_TM
    n_assign = n_tok * TOP_K
    e_flat = e_idx.reshape(n_assign)
    onehot = (e_flat[:, None] == jnp.arange(N_EXPERTS, dtype=i32)[None, :]).astype(i32)
    rank = jnp.take_along_axis(jnp.cumsum(onehot, axis=0), e_flat[:, None], axis=1)[:, 0] - 1
    counts = jnp.sum(onehot, axis=0)
    padded = (counts + tm - 1) // tm * tm
    pend = jnp.cumsum(padded)
    pstart = pend - padded
    dest = pstart[e_flat] + rank
    n_blocks = -(-n_assign // tm) + N_EXPERTS
    rows = n_blocks * tm
    tok_flat = jnp.repeat(jnp.arange(n_tok, dtype=i32), TOP_K)
    row_tok = jnp.zeros((rows,), i32).at[dest].set(tok_flat)
    n_used = (pend[-1] // tm).astype(i32)
    blk = jnp.minimum(jnp.arange(n_blocks, dtype=i32), n_used - 1) * tm
    blk_e = jnp.minimum(jnp.searchsorted(pend, blk, side='right'), N_EXPERTS - 1).astype(i32)
    return dest.reshape(n_tok, TOP_K), row_tok, blk_e, n_used.reshape(1)


def _w_in0_layout(w_in0):
    c_qkv = GDN_CONV_CH
    c_z = c_qkv + GDN_HEADS * GDN_DV
    c_ab = c_z + 2 * GDN_HEADS
    c_q = c_ab + NSA_HEADS * NSA_DH
    c_kv = c_q + 6 * NSA_KV_GROUPS * NSA_DH
    c_g = c_kv + 3 * NSA_HEADS
    assert c_g == w_in0.shape[1]
    n_misc = 2 * GDN_HEADS + 3 * NSA_HEADS
    w = jnp.concatenate([w_in0[:, :c_z], w_in0[:, c_ab:c_kv], w_in0[:, c_z:c_ab], w_in0[:, c_kv:c_g],
                         jnp.zeros((w_in0.shape[0], MISC_W - n_misc), w_in0.dtype)], axis=1)
    widths = (GDN_CONV_CH, GDN_HEADS * GDN_DV, NSA_HEADS * NSA_DH, 4 * NSA_KV_GROUPS * NSA_DH,
              2 * NSA_KV_GROUPS * NSA_DH, MISC_W)
    splits, s = [], 0
    for wd in widths:
        splits.append((s, s + wd))
        s += wd
    return w, tuple(splits)


def kernel(x_prompt, x_sample, c_prompt, c_sample, cache_nsa_kv, cache_nsa_win, state_gdn, state_gdn_conv, state_lru, state_lru_conv, page_table, rel_bias, w_ada, b_ada, norm_mix, norm_ffn, norm_final, w_in0, gdn_conv_w, gdn_a_log, gdn_dt_bias, gdn_norm_w, w_out0, ffn_w_gate, ffn_w_up, ffn_w_down, w_in1, lru_conv_w, lru_conv_b, lru_wa, lru_ba, lru_wx, lru_bx, lru_lambda, w_out1, moe_router, moe_w1, moe_w3, moe_w2):
    bsz, seq, d = x_prompt.shape
    db = x_sample.shape[0]
    assert x_sample.shape[1] == 1
    dh = NSA_DH

    n_c = bsz + db
    n_c_pad = -(-n_c // 8) * 8
    c_all = jnp.concatenate([c_prompt, c_sample, jnp.zeros((n_c_pad - n_c, d), f32)], axis=0)
    mods = _adaln(c_all, w_ada, b_ada).reshape(2, n_c_pad, N_MOD, d)
    mod_p = [[mods[l, :bsz, k].reshape(bsz, 1, d) for k in range(N_MOD)] for l in range(2)]
    mod_s = [[mods[l, bsz:n_c, k].reshape(1, db, d) for k in range(N_MOD)] for l in range(2)]

    w0, splits0 = _w_in0_layout(w_in0)
    splits1 = ((0, RNN_WIDTH), (RNN_WIDTH, 2 * RNN_WIDTH))
    router = jnp.concatenate([moe_router, jnp.zeros((d, MISC_W - N_EXPERTS), f32)], axis=1)
    bf = lambda w: w.astype(bf16)

    tm = 512
    xp = x_prompt
    qkv, z, q_b, kv4, kvwin, misc = _mod_matmul(xp, norm_mix[0], mod_p[0][0], mod_p[0][1], bf(w0), splits0,
                                                tm=tm, prec=1)
    o_a, p_gdn = _gdn(qkv, z, misc, gdn_conv_w, jnp.zeros((bsz, GDN_CONV - 1, GDN_CONV_CH), f32), gdn_a_log,
                      gdn_dt_bias, gdn_norm_w, jnp.zeros((bsz, GDN_HEADS, GDN_DK, GDN_DV), f32), tb=256, n_valid=seq)
    p_gdn_conv = qkv[:, seq - (GDN_CONV - 1):]
    o_b = _nsa_prompt(q_b, misc, kv4, kvwin, rel_bias)
    p_nsa_kv = kv4.reshape(bsz, seq, 4, NSA_KV_GROUPS, dh)
    keep = min(NSA_WINDOW, seq)
    p_nsa_win = kvwin[:, seq - keep:].reshape(bsz, keep, 2, NSA_KV_GROUPS, dh)
    xp = _proj_residual([o_a, o_b], bf(w_out0), xp, mod_p[0][2], tm=tm, prec=1)
    xp = _ffn(xp, norm_ffn[0], mod_p[0][3], mod_p[0][4], mod_p[0][5], bf(ffn_w_gate), bf(ffn_w_up), bf(ffn_w_down),
              tm=tm, tf=FFN_DIM // 2, prec=1)
    gate_br, rec_br = _mod_matmul(xp, norm_mix[1], mod_p[1][0], mod_p[1][1], bf(w_in1), splits1, tm=tm, prec=1)
    y_in, p_lru = _lru_prompt(rec_br, gate_br, lru_conv_w, lru_conv_b, lru_wa, lru_wx, lru_ba, lru_bx, lru_lambda,
                              tb=256)
    p_lru_conv = rec_br[:, seq - (RNN_CONV - 1):]
    xp = _proj_residual([y_in], bf(w_out1), xp, mod_p[1][2], tm=tm, prec=1)
    logit_p, h_p = _mod_matmul(xp, norm_ffn[1], mod_p[1][3], mod_p[1][4], router, ((0, MISC_W),), tm=tm, prec=3,
                               emit_h=bf16)

    xs = x_sample.reshape(1, db, d)
    qkv_s, z_s, q_s, kv4_s, kvwin_s, misc_s = _mod_matmul(xs, norm_mix[0], mod_s[0][0], mod_s[0][1], w0, splits0,
                                                          tm=db, prec=3)
    c = GDN_CHUNK
    tpad = lambda a: jnp.pad(a.reshape(db, 1, a.shape[-1]), ((0, 0), (0, c - 1), (0, 0)))
    o_a_s, s_gdn = _gdn(tpad(qkv_s), tpad(z_s), tpad(misc_s), gdn_conv_w, state_gdn_conv, gdn_a_log, gdn_dt_bias,
                        gdn_norm_w, state_gdn, tb=c, n_valid=1)
    o_a_s = o_a_s[:, 0].reshape(1, db, GDN_HEADS * GDN_DV)
    s_gdn_conv = jnp.concatenate([state_gdn_conv[:, 1:], qkv_s.reshape(db, 1, GDN_CONV_CH)], axis=1)
    gates_s = jax.nn.sigmoid(misc_s[0, :, 2 * GDN_HEADS:2 * GDN_HEADS + 3 * NSA_HEADS]).reshape(db, NSA_HEADS, 3)
    o_b_s = _nsa_sample_t(q_s[0], gates_s, kv4_s[0], kvwin_s[0], cache_nsa_kv, cache_nsa_win, page_table, rel_bias)
    s_nsa_kv = kv4_s.reshape(db, 1, 4, NSA_KV_GROUPS, dh)
    s_nsa_win = jnp.concatenate([cache_nsa_win[:, 1:],
                                 kvwin_s.reshape(db, 1, 2, NSA_KV_GROUPS, dh).astype(cache_nsa_win.dtype)], axis=1)
    xs = _proj_residual([o_a_s, o_b_s.reshape(1, db, NSA_HEADS * dh)], w_out0, xs, mod_s[0][2], tm=db, prec=3)
    xs = _ffn(xs, norm_ffn[0], mod_s[0][3], mod_s[0][4], mod_s[0][5], ffn_w_gate, ffn_w_up, ffn_w_down,
              tm=db, tf=256, prec=3)
    gate_s, rec_s = _mod_matmul(xs, norm_mix[1], mod_s[1][0], mod_s[1][1], w_in1, splits1, tm=db, prec=3)
    y_in_s, s_lru = _lru_sample(rec_s[0], gate_s[0], state_lru_conv, state_lru, lru_conv_w, lru_conv_b, lru_wa, lru_wx,
                                lru_ba, lru_bx, lru_lambda)
    s_lru_conv = jnp.concatenate([state_lru_conv[:, 1:], rec_s.reshape(db, 1, RNN_WIDTH)], axis=1)
    xs = _proj_residual([y_in_s.reshape(1, db, RNN_WIDTH)], w_out1, xs, mod_s[1][2], tm=db, prec=3)
    logit_s, h_s = _mod_matmul(xs, norm_ffn[1], mod_s[1][3], mod_s[1][4], router, ((0, MISC_W),), tm=db, prec=3,
                               emit_h=bf16)

    n_p = bsz * seq
    n_tok = n_p + db
    logits = jnp.concatenate([logit_p.reshape(n_p, MISC_W), logit_s.reshape(db, MISC_W)], axis=0)
    h_all = jnp.concatenate([h_p.reshape(n_p, d), h_s.reshape(db, d)], axis=0)
    e_idx, gw = _top2(logits, tm=max(t for t in range(8, 1025, 8) if n_tok % t == 0))
    dest, row_tok, blk_e, n_used = _moe_dispatch(e_idx[:, :TOP_K], n_tok)
    yb = _moe_experts(h_all[row_tok], blk_e, n_used, moe_w1, moe_w3, moe_w2, tm=MOE_TM, tf=MOE_TF)
    y01 = yb[dest.reshape(-1)].reshape(n_tok, TOP_K * d)
    y_prompt = _moe_combine(xp, mod_p[1][5], y01[:n_p].reshape(bsz, seq, TOP_K * d),
                            gw[:n_p].reshape(bsz, seq, MISC_W), norm_final, tm=tm)
    y_sample = _moe_combine(xs, mod_s[1][5], y01[n_p:].reshape(1, db, TOP_K * d),
                            gw[n_p:].reshape(1, db, MISC_W), norm_final, tm=db)

    return (y_prompt, y_sample.reshape(db, 1, d),
            p_nsa_kv, p_nsa_win, p_gdn, p_gdn_conv, p_lru.reshape(bsz, RNN_WIDTH), p_lru_conv,
            s_nsa_kv, s_nsa_win, s_gdn, s_gdn_conv, s_lru, s_lru_conv)
```

```python
import functools
import math

import numpy as np
import jax
import jax.numpy as jnp
from jax import lax
from jax.experimental import pallas as pl
from jax.experimental.pallas import tpu as pltpu

f32 = jnp.float32
bf16 = jnp.bfloat16
i32 = jnp.int32

D_MODEL = 1024
EPS = 1e-6
N_MOD = 6
GDN_HEADS = 4
GDN_DK = 128
GDN_DV = 128
GDN_CONV = 4
GDN_CHUNK = 64
GDN_CONV_CH = GDN_HEADS * (2 * GDN_DK + GDN_DV)
NSA_HEADS = 8
NSA_KV_GROUPS = 2
NSA_HPG = NSA_HEADS // NSA_KV_GROUPS
NSA_DH = 64
NSA_BLOCK = 64
NSA_TOPN = 16
NSA_LOCAL = 2
NSA_WINDOW = 512
NSA_QBLOCK = 128
NSA_FORCE = 1.0e4
REL_BUCKETS = 32
REL_MAX_DIST = 2048
RNN_WIDTH = D_MODEL
RNN_BLOCKS = 8
RNN_BW = RNN_WIDTH // RNN_BLOCKS
RNN_CONV = 4
RG_C = 8.0
FFN_DIM = 2816
N_EXPERTS = 8
TOP_K = 2
EXPERT_DIM = 3584

NEG = -1e30
PAIR = 2 * NSA_BLOCK
NEAR_PAIRS = 13
MISC_W = 128
VMEM_LIMIT = 56 * 1024 * 1024


def _cparams(sem, vmem=VMEM_LIMIT):
    return pltpu.CompilerParams(dimension_semantics=sem, vmem_limit_bytes=vmem)


def _mm(a, b):
    return jnp.dot(a.astype(bf16), b.astype(bf16), preferred_element_type=f32)


def _mm_nt(a, b):
    return lax.dot_general(a.astype(bf16), b.astype(bf16), (((1,), (1,)), ((), ())),
                           preferred_element_type=f32)


def _split2(a):
    hi = a.astype(bf16)
    lo = (a - hi.astype(f32)).astype(bf16)
    return hi, lo


def _mm3(a, b):
    ah, al = _split2(a)
    bh, bl = _split2(b)
    return (jnp.dot(ah, bh, preferred_element_type=f32) + jnp.dot(ah, bl, preferred_element_type=f32)
            + jnp.dot(al, bh, preferred_element_type=f32))


def _mm3_nt(a, b):
    ah, al = _split2(a)
    bh, bl = _split2(b)
    dn = (((1,), (1,)), ((), ()))
    return (lax.dot_general(ah, bh, dn, preferred_element_type=f32)
            + lax.dot_general(ah, bl, dn, preferred_element_type=f32)
            + lax.dot_general(al, bh, dn, preferred_element_type=f32))


def _mm_01(m01, a):
    hi = a.astype(bf16)
    r1 = a - hi.astype(f32)
    mid = r1.astype(bf16)
    lo = (r1 - mid.astype(f32)).astype(bf16)
    return (jnp.dot(m01, hi, preferred_element_type=f32) + jnp.dot(m01, mid, preferred_element_type=f32)
            + jnp.dot(m01, lo, preferred_element_type=f32))


def _dotp(a, b, prec):
    return _mm3(a, b) if prec == 3 else _mm(a, b)


def _silu(x):
    return x * jax.nn.sigmoid(x)


def _softplus(x):
    return jnp.maximum(x, 0.0) + jnp.log1p(jnp.exp(-jnp.abs(x)))


def _modulate(x, gain, shift, scale):
    r = lax.rsqrt(jnp.mean(x * x, axis=-1, keepdims=True) + EPS)
    return x * r * gain * (1.0 + scale) + shift


def _ada_kernel(c_ref, w_ref, b_ref, o_ref):
    o_ref[0] = _mm3(_silu(c_ref[...]), w_ref[0]) + b_ref[0]


def _adaln(c_all, w_ada, b_ada):
    rows = c_all.shape[0]
    depth, d, n = w_ada.shape
    tn = 1536
    return pl.pallas_call(
        _ada_kernel,
        grid=(depth, n // tn),
        in_specs=[pl.BlockSpec((rows, d), lambda l, j: (0, 0)),
                  pl.BlockSpec((1, d, tn), lambda l, j: (l, 0, j)),
                  pl.BlockSpec((1, 1, tn), lambda l, j: (l, 0, j))],
        out_specs=pl.BlockSpec((1, rows, tn), lambda l, j: (l, 0, j)),
        out_shape=jax.ShapeDtypeStruct((depth, rows, n), f32),
        compiler_params=_cparams(("arbitrary", "arbitrary")),
        name="adaln",
    )(c_all, w_ada, b_ada.reshape(depth, 1, n))


def _mod_spec(mod, tm):
    r = mod.shape[1]
    if r == 1:
        return pl.BlockSpec((1, 1, mod.shape[2]), lambda b, i: (b, 0, 0))
    return pl.BlockSpec((1, tm, mod.shape[2]), lambda b, i: (b, i, 0))


def _modmm_kernel(x_ref, gain_ref, shift_ref, scale_ref, w_ref, *o_refs, splits, prec, emit_h):
    h = _modulate(x_ref[0], gain_ref[...], shift_ref[0], scale_ref[0])
    if emit_h:
        o_refs[-1][0] = h.astype(o_refs[-1].dtype)
    hh = _split2(h) if prec == 3 else h.astype(bf16)
    for o_ref, (a, b) in zip(o_refs, splits):
        w = w_ref[:, a:b]
        if prec == 3:
            wh, wl = _split2(w)
            acc = (jnp.dot(hh[0], wh, preferred_element_type=f32) + jnp.dot(hh[0], wl, preferred_element_type=f32)
                   + jnp.dot(hh[1], wh, preferred_element_type=f32))
        else:
            acc = jnp.dot(hh, w, preferred_element_type=f32)
        o_ref[0] = acc


def _mod_matmul(x, gain, shift, scale, w, splits, *, tm, prec, emit_h=None):
    bsz, t, d = x.shape
    out_shape = [jax.ShapeDtypeStruct((bsz, t, b - a), f32) for a, b in splits]
    out_specs = [pl.BlockSpec((1, tm, b - a), lambda bi, i: (bi, i, 0)) for a, b in splits]
    if emit_h is not None:
        out_shape.append(jax.ShapeDtypeStruct((bsz, t, d), emit_h))
        out_specs.append(pl.BlockSpec((1, tm, d), lambda bi, i: (bi, i, 0)))
    return pl.pallas_call(
        functools.partial(_modmm_kernel, splits=tuple(splits), prec=prec, emit_h=emit_h is not None),
        grid=(bsz, t // tm),
        in_specs=[pl.BlockSpec((1, tm, d), lambda bi, i: (bi, i, 0)),
                  pl.BlockSpec((1, d), lambda bi, i: (0, 0)),
                  _mod_spec(shift, tm), _mod_spec(scale, tm),
                  pl.BlockSpec(w.shape, lambda bi, i: (0, 0))],
        out_specs=out_specs,
        out_shape=out_shape,
        compiler_params=_cparams(("parallel", "parallel")),
        name="mod_matmul",
    )(x, gain.reshape(1, d), shift, scale, w)


def _projres_kernel(*refs, n_lhs, ksplits, prec, final_norm):
    lhs = refs[:n_lhs]
    w_ref, x_ref, gate_ref = refs[n_lhs:n_lhs + 3]
    o_ref = refs[-1]
    acc = None
    for l_ref, (a, b) in zip(lhs, ksplits):
        part = _dotp(l_ref[0], w_ref[a:b, :], prec)
        acc = part if acc is None else acc + part
    y = x_ref[0] + gate_ref[0] * acc
    if final_norm:
        nw_ref = refs[n_lhs + 3]
        y = y * lax.rsqrt(jnp.mean(y * y, axis=-1, keepdims=True) + EPS) * nw_ref[...]
    o_ref[0] = y


def _proj_residual(lhs_list, w, x, gate, *, tm, prec, norm_w=None):
    bsz, t, d = x.shape
    ksplits, k0 = [], 0
    for l in lhs_list:
        ksplits.append((k0, k0 + l.shape[-1]))
        k0 += l.shape[-1]
    in_specs = [pl.BlockSpec((1, tm, l.shape[-1]), lambda bi, i: (bi, i, 0)) for l in lhs_list]
    in_specs += [pl.BlockSpec(w.shape, lambda bi, i: (0, 0)),
                 pl.BlockSpec((1, tm, d), lambda bi, i: (bi, i, 0)),
                 _mod_spec(gate, tm)]
    args = list(lhs_list) + [w, x, gate]
    if norm_w is not None:
        in_specs.append(pl.BlockSpec((1, d), lambda bi, i: (0, 0)))
        args.append(norm_w.reshape(1, d))
    return pl.pallas_call(
        functools.partial(_projres_kernel, n_lhs=len(lhs_list), ksplits=tuple(ksplits), prec=prec,
                          final_norm=norm_w is not None),
        grid=(bsz, t // tm),
        in_specs=in_specs,
        out_specs=pl.BlockSpec((1, tm, d), lambda bi, i: (bi, i, 0)),
        out_shape=jax.ShapeDtypeStruct((bsz, t, d), f32),
        compiler_params=_cparams(("parallel", "parallel")),
        name="proj_residual",
    )(*args)


def _ffn_kernel(x_ref, gain_ref, shift_ref, scale_ref, gate_ref, wg_ref, wu_ref, wd_ref, o_ref,
                h_scr, acc_scr, *, prec):
    f = pl.program_id(2)

    @pl.when(f == 0)
    def _():
        h_scr[...] = _modulate(x_ref[0], gain_ref[...], shift_ref[0], scale_ref[0]).astype(h_scr.dtype)
        acc_scr[...] = jnp.zeros_like(acc_scr)

    h = h_scr[...]
    act = _silu(_dotp(h, wg_ref[...], prec)) * _dotp(h, wu_ref[...], prec)
    acc_scr[...] += _dotp(act, wd_ref[...], prec)

    @pl.when(f == pl.num_programs(2) - 1)
    def _():
        o_ref[0] = x_ref[0] + gate_ref[0] * acc_scr[...]


def _ffn(x, gain, shift, scale, gate, wg, wu, wd, *, tm, tf, prec):
    bsz, t, d = x.shape
    fdim = wg.shape[1]
    mod_specs = []
    for mod in (shift, scale, gate):
        if mod.shape[1] == 1:
            mod_specs.append(pl.BlockSpec((1, 1, d), lambda b, i, f: (b, 0, 0)))
        else:
            mod_specs.append(pl.BlockSpec((1, tm, d), lambda b, i, f: (b, i, 0)))
    return pl.pallas_call(
        functools.partial(_ffn_kernel, prec=prec),
        grid=(bsz, t // tm, fdim // tf),
        in_specs=[pl.BlockSpec((1, tm, d), lambda b, i, f: (b, i, 0)),
                  pl.BlockSpec((1, d), lambda b, i, f: (0, 0)),
                  mod_specs[0], mod_specs[1], mod_specs[2],
                  pl.BlockSpec((d, tf), lambda b, i, f: (0, f)),
                  pl.BlockSpec((d, tf), lambda b, i, f: (0, f)),
                  pl.BlockSpec((tf, d), lambda b, i, f: (f, 0))],
        out_specs=pl.BlockSpec((1, tm, d), lambda b, i, f: (b, i, 0)),
        out_shape=jax.ShapeDtypeStruct((bsz, t, d), f32),
        scratch_shapes=[pltpu.VMEM((tm, d), f32 if prec == 3 else bf16), pltpu.VMEM((tm, d), f32)],
        compiler_params=_cparams(("parallel", "parallel", "arbitrary")),
        name="ffn",
    )(x, gain.reshape(1, d), shift, scale, gate, wg, wu, wd)


def _gdn_kernel(qkv_ref, z_ref, misc_ref, cw_ref, cinit_ref, hp_ref, nw_ref, s0_ref,
                o_ref, sfin_ref, xc_scr, s_scr, *, tb, n_valid):
    j = pl.program_id(1)
    c = GDN_CHUNK
    nh = GDN_HEADS

    @pl.when(j == 0)
    def _():
        xc_scr[0:8, :] = cinit_ref[0]
        s_scr[...] = s0_ref[0]

    x = qkv_ref[0]
    xc_scr[8:8 + tb, :] = x
    y = xc_scr[5:5 + tb, :] * cw_ref[0:1, :]
    y = y + xc_scr[6:6 + tb, :] * cw_ref[1:2, :]
    y = y + xc_scr[7:7 + tb, :] * cw_ref[2:3, :]
    y = y + x * cw_ref[3:4, :]
    xc_scr[0:8, :] = xc_scr[tb:tb + 8, :]
    y = _silu(y)

    misc = misc_ref[0]
    row = lax.broadcasted_iota(i32, (tb, MISC_W), 0) + j * tb
    live = row < n_valid
    log_a = jnp.where(live, hp_ref[0:1, :] * _softplus(misc + hp_ref[1:2, :]), 0.0)
    beta = jnp.where(live, jax.nn.sigmoid(misc), 0.0)

    r_i = lax.broadcasted_iota(i32, (tb, tb), 0)
    c_i = lax.broadcasted_iota(i32, (tb, tb), 1)
    sh = int(math.log2(c))
    ltri = (((r_i >> sh) == (c_i >> sh)) & (c_i <= r_i)).astype(bf16)
    g = _mm_01(ltri, log_a)
    g_t = g.T

    ri = lax.broadcasted_iota(i32, (c, c), 0)
    ci = lax.broadcasted_iota(i32, (c, c), 1)
    incl = ci <= ri
    strict = ci < ri
    eye = (ci == ri).astype(f32)
    quad = []
    for lvl in range(int(math.log2(c))):
        quad.append(((ri >> (lvl + 1)) == (ci >> (lvl + 1))) & (((ri >> lvl) & 1) == 1) & (((ci >> lvl) & 1) == 0))

    nchunk = tb // c
    units = [(n, h) for n in range(nchunk) for h in range(nh)]

    def stack(fn):
        return jnp.stack([fn(n * c, h) for n, h in units], axis=0)

    def bmm(eq, a_, b_):
        return jnp.einsum(eq, a_.astype(bf16), b_.astype(bf16), preferred_element_type=f32)

    q = stack(lambda r0, h: y[r0:r0 + c, h * GDN_DK:(h + 1) * GDN_DK])
    k = stack(lambda r0, h: y[r0:r0 + c, (nh + h) * GDN_DK:(nh + h + 1) * GDN_DK])
    v = stack(lambda r0, h: y[r0:r0 + c, 2 * nh * GDN_DK + h * GDN_DV:2 * nh * GDN_DK + (h + 1) * GDN_DV])
    q = q * lax.rsqrt(jnp.sum(q * q, axis=-1, keepdims=True) + EPS) * (GDN_DK ** -0.5)
    k = k * lax.rsqrt(jnp.sum(k * k, axis=-1, keepdims=True) + EPS)
    g_col = stack(lambda r0, h: g[r0:r0 + c, h:h + 1])
    g_row = stack(lambda r0, h: g_t[h:h + 1, r0:r0 + c])
    b_col = stack(lambda r0, h: beta[r0:r0 + c, nh + h:nh + h + 1])
    gam = jnp.where(incl, jnp.exp(jnp.where(incl, g_col - g_row, 0.0)), 0.0)
    kk = bmm('uid,ujd->uij', k, k)
    a = jnp.where(strict, b_col * gam * kk, 0.0)
    p = eye - jnp.where(quad[0], a, 0.0)
    for lvl in range(1, len(quad)):
        m = bmm('uij,ujk->uik', jnp.where(quad[lvl], a, 0.0), p)
        p = p - bmm('uij,ujk->uik', p, m)
    e_g = jnp.exp(g_col)
    sol = bmm('uij,ujd->uid', p, jnp.concatenate([b_col * v, (b_col * e_g) * k], axis=-1))
    vb, w = sol[:, :, :GDN_DV], sol[:, :, GDN_DV:]
    aqk = bmm('uid,ujd->uij', q, k) * gam
    qg = q * e_g
    g_last = g_col[:, c - 1:c, :]
    kd_t = jnp.swapaxes(k * jnp.exp(g_last - g_col), 1, 2)
    gc = jnp.exp(g_last)

    s = s_scr[...]
    for n in range(nchunk):
        sl = slice(n * nh, (n + 1) * nh)
        u = vb[sl] - bmm('hcd,hde->hce', w[sl], s)
        o = bmm('hcd,hde->hce', qg[sl], s) + bmm('hij,hje->hie', aqk[sl], u)
        s = gc[sl] * s + bmm('hdc,hce->hde', kd_t[sl], u)
        o = o * lax.rsqrt(jnp.mean(o * o, axis=-1, keepdims=True) + EPS) * nw_ref[...]
        for h in range(nh):
            zs = z_ref[0, n * c:(n + 1) * c, h * GDN_DV:(h + 1) * GDN_DV]
            o_ref[0, n * c:(n + 1) * c, h * GDN_DV:(h + 1) * GDN_DV] = o[h] * _silu(zs)
    s_scr[...] = s

    @pl.when(j == pl.num_programs(1) - 1)
    def _():
        sfin_ref[0] = s_scr[...]


def _gdn(qkv_raw, z, misc, conv_w, conv_buf, a_log, dt_bias, norm_w, s0, *, tb, n_valid):
    bsz, tp, ch = qkv_raw.shape
    cinit = jnp.concatenate([jnp.zeros((bsz, 5, ch), f32), conv_buf.astype(f32)], axis=1)
    hp = jnp.zeros((8, MISC_W), f32)
    hp = hp.at[0, :GDN_HEADS].set(-jnp.exp(a_log.astype(f32))).at[1, :GDN_HEADS].set(dt_bias.astype(f32))
    zw = GDN_HEADS * GDN_DV
    return pl.pallas_call(
        functools.partial(_gdn_kernel, tb=tb, n_valid=n_valid),
        grid=(bsz, tp // tb),
        in_specs=[pl.BlockSpec((1, tb, ch), lambda b, j: (b, j, 0)),
                  pl.BlockSpec((1, tb, zw), lambda b, j: (b, j, 0)),
                  pl.BlockSpec((1, tb, MISC_W), lambda b, j: (b, j, 0)),
                  pl.BlockSpec((GDN_CONV, ch), lambda b, j: (0, 0)),
                  pl.BlockSpec((1, 8, ch), lambda b, j: (b, 0, 0)),
                  pl.BlockSpec((8, MISC_W), lambda b, j: (0, 0)),
                  pl.BlockSpec((1, GDN_DV), lambda b, j: (0, 0)),
                  pl.BlockSpec((1, GDN_HEADS, GDN_DK, GDN_DV), lambda b, j: (b, 0, 0, 0))],
        out_specs=[pl.BlockSpec((1, tb, zw), lambda b, j: (b, j, 0)),
                   pl.BlockSpec((1, GDN_HEADS, GDN_DK, GDN_DV), lambda b, j: (b, 0, 0, 0))],
        out_shape=[jax.ShapeDtypeStruct((bsz, tp, zw), f32),
                   jax.ShapeDtypeStruct((bsz, GDN_HEADS, GDN_DK, GDN_DV), f32)],
        scratch_shapes=[pltpu.VMEM((tb + 8, ch), f32), pltpu.VMEM((GDN_HEADS, GDN_DK, GDN_DV), f32)],
        compiler_params=_cparams(("parallel", "arbitrary")),
        name="gdn",
    )(qkv_raw, z, misc, conv_w.astype(f32), cinit, hp, norm_w.reshape(1, GDN_DV).astype(f32), s0.astype(f32))


def _bucket_thresholds():
    exact = REL_BUCKETS // 2
    d = np.arange(0, 4 * REL_MAX_DIST, dtype=np.int64)

    def buckets(ft):
        nf = np.maximum(d, exact).astype(ft)
        large = exact + (np.log(nf / ft(exact)) / ft(math.log(REL_MAX_DIST / exact)) * ft(REL_BUCKETS - exact)).astype(np.int32)
        return np.where(d < exact, d, np.minimum(large, REL_BUCKETS - 1))

    b64, b32 = buckets(np.float64), buckets(np.float32)
    assert np.array_equal(b64, b32) and b64[-1] == REL_BUCKETS - 1 and np.all(np.diff(b64) >= 0)
    return [int(np.argmax(b64 >= k)) for k in range(1, REL_BUCKETS)]


_BUCKET_THR = _bucket_thresholds()
FAR_DIST = _BUCKET_THR[-1]


def _bias_kernel(rb_ref, d_ref, o_ref, *, slab, group_heads):
    rows = d_ref.shape[-2]
    for s in range(rows // slab):
        d = d_ref[0, 0, s * slab:(s + 1) * slab, :]
        h = pl.program_id(0) * group_heads + s
        val = jnp.full(d.shape, rb_ref[0, h], f32)
        for k in range(1, REL_BUCKETS):
            val = jnp.where(d >= _BUCKET_THR[k - 1], rb_ref[k, h], val)
        o_ref[0, 0, s * slab:(s + 1) * slab, :] = jnp.where(d < 0, NEG, val)


def _bias_table(rel_bias, dist, *, slab):
    g, nt, rows, cols = dist.shape
    return pl.pallas_call(
        functools.partial(_bias_kernel, slab=slab, group_heads=rows // slab),
        grid=(g, nt),
        in_specs=[pl.BlockSpec(memory_space=pltpu.SMEM),
                  pl.BlockSpec((1, 1, rows, cols), lambda a, b: (a, b, 0, 0))],
        out_specs=pl.BlockSpec((1, 1, rows, cols), lambda a, b: (a, b, 0, 0)),
        out_shape=jax.ShapeDtypeStruct(dist.shape, f32),
        compiler_params=_cparams(("arbitrary", "arbitrary")),
        name="bias_table",
    )(rel_bias.astype(f32), dist)


def _blockmean_kernel(x_ref, o_ref):
    x = x_ref[0]
    nb = x.shape[0] // NSA_BLOCK
    o_ref[0] = jnp.sum(x.reshape(nb, NSA_BLOCK, x.shape[1]), axis=1) * (1.0 / NSA_BLOCK)


def _block_means(kv4, *, tb):
    bsz, t, _ = kv4.shape
    w = 2 * NSA_KV_GROUPS * NSA_DH
    return pl.pallas_call(
        _blockmean_kernel,
        grid=(bsz, t // tb),
        in_specs=[pl.BlockSpec((1, tb, w), lambda b, i: (b, i, 0))],
        out_specs=pl.BlockSpec((1, tb // NSA_BLOCK, w), lambda b, i: (b, i, 0)),
        out_shape=jax.ShapeDtypeStruct((bsz, t // NSA_BLOCK, w), f32),
        compiler_params=_cparams(("parallel", "parallel")),
        name="nsa_block_means",
    )(kv4)


def _topn_mask(imp, n_sel):
    lane = lax.broadcasted_iota(i32, imp.shape, 1)
    sel = jnp.zeros(imp.shape, jnp.bool_)
    for _ in range(n_sel):
        mx = jnp.max(imp, axis=-1, keepdims=True)
        first = jnp.min(jnp.where(imp == mx, lane, imp.shape[1]), axis=-1, keepdims=True)
        hit = lane == first
        sel = sel | hit
        imp = jnp.where(hit, -jnp.inf, imp)
    return sel


def _nsa_prompt_kernel(q_ref, misc_ref, kcvc_ref, kaug_ref, v2_ref, kw_ref, vw_ref,
                       tabs_ref, tabw_ref, tabc_ref, c31_ref, o_ref):
    g = pl.program_id(1)
    i = pl.program_id(2)
    qb, dh, hpg = NSA_QBLOCK, NSA_DH, NSA_HPG
    rows = hpg * qb
    is_g0 = g == 0

    def ghalf(a):
        return jnp.where(is_g0, a[:, :dh], a[:, dh:])

    qt = q_ref[0] * (dh ** -0.5)
    qs = jnp.concatenate([qt[:, h * dh:(h + 1) * dh] for h in range(hpg)], axis=0)
    zero = jnp.zeros_like(qs)
    q2 = jnp.concatenate([jnp.where(is_g0, qs, zero), jnp.where(is_g0, zero, qs)], axis=1)

    kcvc = kcvc_ref[0]
    nblk = kcvc.shape[0]
    kc = ghalf(kcvc[:, :2 * dh])
    vc = ghalf(kcvc[:, 2 * dh:])
    s_c = _mm3_nt(qs, kc)
    lane = lax.broadcasted_iota(i32, (rows, nblk), 1)
    qrow = lax.broadcasted_iota(i32, (rows, nblk), 0) & (qb - 1)
    shift = (2 * i - 2 * NEAR_PAIRS + 1 + 4 * nblk) % nblk
    bias_c = pltpu.roll(tabc_ref[0], shift, 1)
    bias_c = jnp.where(lane < 2 * i - 2 * NEAR_PAIRS + 1, c31_ref[0], bias_c)
    readable = lane * NSA_BLOCK + (NSA_BLOCK - 1) <= i * qb + qrow
    s_c = jnp.where(readable, s_c + bias_c, NEG)
    m_c = jnp.max(s_c, axis=-1, keepdims=True)
    p_c = jnp.where(readable, jnp.exp(s_c - m_c), 0.0)
    p_c = p_c / jnp.maximum(jnp.sum(p_c, axis=-1, keepdims=True), 1e-30)
    o_c = _mm(p_c, vc)

    imp = p_c[0:qb]
    for h in range(1, hpg):
        imp = imp + p_c[h * qb:(h + 1) * qb]
    blk = lax.broadcasted_iota(i32, (qb, nblk), 1)
    cur = (i * qb + lax.broadcasted_iota(i32, (qb, nblk), 0)) >> int(math.log2(NSA_BLOCK))
    valid = blk <= cur
    forced = valid & ((blk == 0) | (blk > cur - NSA_LOCAL))
    imp = jnp.where(forced, -jnp.inf, jnp.where(valid, imp, -1.0))
    sel = (forced | _topn_mask(imp, min(NSA_TOPN, nblk) - 1 - NSA_LOCAL)) & valid
    msel = jnp.where(sel, 0.0, NEG).astype(bf16)
    qaug = jnp.concatenate([q2.astype(bf16), jnp.concatenate([msel] * hpg, axis=0)], axis=1)

    t_d = i // 2
    tk = 2 * PAIR

    def tile(k, carry):
        m, l, acc = carry
        t = t_d - k
        ks = pl.multiple_of(t * tk, tk)
        s = lax.dot_general(qaug, kaug_ref[0, pl.ds(ks, tk), :], (((1,), (1,)), ((), ())),
                            preferred_element_type=f32)
        i0 = jnp.clip(i - 2 * t, 0, NEAR_PAIRS)
        i1 = jnp.clip(i - 2 * t - 1, 0, NEAR_PAIRS)
        s = s + jnp.concatenate([tabs_ref[0, i0], tabs_ref[0, i1]], axis=1)
        m_new = jnp.maximum(m, jnp.max(s, axis=-1, keepdims=True))
        alpha = jnp.exp(m - m_new)
        p = jnp.exp(s - m_new)
        l = alpha * l + jnp.sum(p, axis=-1, keepdims=True)
        acc = alpha * acc + jnp.dot(p.astype(bf16), v2_ref[0, pl.ds(ks, tk), :], preferred_element_type=f32)
        return m_new, l, acc

    m0 = jnp.full((rows, 1), NEG, f32)
    l0 = jnp.zeros((rows, 1), f32)
    a0 = jnp.zeros((rows, 2 * dh), f32)
    _, l_s, acc_s = lax.fori_loop(0, t_d + 1, tile, (m0, l0, a0))
    o_s = ghalf(acc_s) / jnp.maximum(l_s, 1e-30)

    span = NSA_WINDOW + qb
    ws = pl.multiple_of(i * qb, qb)
    s_w = lax.dot_general(q2.astype(bf16), kw_ref[0, pl.ds(ws, span), :], (((1,), (1,)), ((), ())),
                          preferred_element_type=f32) + tabw_ref[0]
    kpos = lax.broadcasted_iota(i32, (rows, span), 1) + (i * qb - NSA_WINDOW)
    s_w = jnp.where(kpos >= 0, s_w, NEG)
    m_w = jnp.max(s_w, axis=-1, keepdims=True)
    p_w = jnp.exp(s_w - m_w)
    l_w = jnp.sum(p_w, axis=-1, keepdims=True)
    o_w = ghalf(jnp.dot(p_w.astype(bf16), vw_ref[0, pl.ds(ws, span), :], preferred_element_type=f32))
    o_w = o_w / jnp.maximum(l_w, 1e-30)

    gates = jax.nn.sigmoid(misc_ref[0])
    outs = []
    for h in range(hpg):
        r = slice(h * qb, (h + 1) * qb)
        acc = None
        for jb, branch in enumerate((o_c, o_s, o_w)):
            l0_ = 8 + 3 * h + jb
            l1_ = 8 + 3 * (hpg + h) + jb
            gcol = jnp.where(is_g0, gates[:, l0_:l0_ + 1], gates[:, l1_:l1_ + 1])
            term = gcol * branch[r]
            acc = term if acc is None else acc + term
        outs.append(acc)
    o_ref[0] = jnp.concatenate(outs, axis=1)


LOG2E = 1.4426950408889634


def _nsa_prompt_kernel_h(q_ref, misc_ref, kcvc_ref, kaug_ref, vg_ref, kw_ref, vw_ref,
                         tabs_ref, tabw_ref, tabc_ref, c31_ref, o_ref, qaug_scr, s_scr):
    g = pl.program_id(1)
    i = pl.program_id(2)
    qb, dh, hpg = NSA_QBLOCK, NSA_DH, NSA_HPG
    is_g0 = g == 0
    nn = (((1,), (1,)), ((), ()))

    def ghalf(a):
        return jnp.where(is_g0, a[:, :dh], a[:, dh:])

    def hrows(ref, *lead):
        return [ref[lead + (slice(h * qb, (h + 1) * qb), slice(None))] for h in range(hpg)]

    qt = q_ref[0] * (dh ** -0.5)
    qh = [qt[:, h * dh:(h + 1) * dh] for h in range(hpg)]
    kcvc = kcvc_ref[0]
    nblk = kcvc.shape[0]
    kc = ghalf(kcvc[:, :2 * dh])
    vc = ghalf(kcvc[:, 2 * dh:])

    rows = hpg * qb
    qs = jnp.concatenate(qh, axis=0)
    zero = jnp.zeros_like(qs)
    q2 = jnp.concatenate([jnp.where(is_g0, qs, zero), jnp.where(is_g0, zero, qs)], axis=1)
    q2 = (q2 * LOG2E).astype(bf16)

    blk4 = lax.broadcasted_iota(i32, (rows, nblk), 1)
    qrow4 = lax.broadcasted_iota(i32, (rows, nblk), 0) & (qb - 1)
    first_tab = 2 * i - 2 * NEAR_PAIRS + 1
    shift = (first_tab + 4 * nblk) % nblk
    readable = blk4 * NSA_BLOCK + (NSA_BLOCK - 1) <= i * qb + qrow4
    bias = jnp.where(blk4 < first_tab, c31_ref[0], pltpu.roll(tabc_ref[0], shift, 1))
    s_c = jnp.where(readable, _mm3_nt(qs, kc) + bias, NEG)
    m_c = jnp.max(s_c, axis=-1, keepdims=True)
    p_c = jnp.where(readable, jnp.exp(s_c - m_c), 0.0)
    p_c = p_c / jnp.maximum(jnp.sum(p_c, axis=-1, keepdims=True), 1e-30)
    o_cmp = _mm(p_c, vc)
    imp = p_c[0:qb]
    for h in range(1, hpg):
        imp = imp + p_c[h * qb:(h + 1) * qb]

    span = NSA_WINDOW + qb
    ws = pl.multiple_of(i * qb, qb)
    s_w = lax.dot_general(q2, kw_ref[0, pl.ds(ws, span), :], nn, preferred_element_type=f32) + tabw_ref[0]
    in_seq = lax.broadcasted_iota(i32, (rows, span), 1) + (i * qb - NSA_WINDOW) >= 0
    s_w = jnp.where(in_seq, s_w, NEG)
    m_w = jnp.max(s_w, axis=-1, keepdims=True)
    p_w = jnp.exp2(s_w - m_w)
    l_w = jnp.sum(p_w, axis=-1, keepdims=True)
    o_win = ghalf(jnp.dot(p_w.astype(bf16), vw_ref[0, pl.ds(ws, span), :], preferred_element_type=f32))
    o_win = o_win / jnp.maximum(l_w, 1e-30)

    blk = lax.broadcasted_iota(i32, (qb, nblk), 1)
    qrow = lax.broadcasted_iota(i32, (qb, nblk), 0)
    cur = (i * qb + qrow) >> int(math.log2(NSA_BLOCK))
    valid = blk <= cur
    forced = valid & ((blk == 0) | (blk > cur - NSA_LOCAL))
    imp_t = jnp.where(forced, -jnp.inf, jnp.where(valid, imp, -1.0)).T
    brow = lax.broadcasted_iota(i32, (nblk, qb), 0)
    picked = jnp.zeros((nblk, qb), f32)
    for _ in range(min(NSA_TOPN, nblk) - 1 - NSA_LOCAL):
        mx = jnp.max(imp_t, axis=0, keepdims=True)
        first = jnp.min(jnp.where(imp_t == mx, brow, nblk), axis=0, keepdims=True)
        hit = brow == first
        picked = jnp.where(hit, 1.0, picked)
        imp_t = jnp.where(hit, -jnp.inf, imp_t)
    sel = (forced | (picked.T > 0.5)) & valid
    msel = jnp.where(sel, 0.0, NEG).astype(bf16)
    qaug_scr[...] = jnp.concatenate([q2, jnp.concatenate([msel] * hpg, axis=0)], axis=1)

    t_d = i // 2
    tk = 2 * PAIR

    def scores(t):
        ks = pl.multiple_of(t * tk, tk)
        s = lax.dot_general(qaug_scr[...], kaug_ref[0, pl.ds(ks, tk), :], nn, preferred_element_type=f32)
        i0 = jnp.clip(i - 2 * t, 0, NEAR_PAIRS)
        i1 = jnp.clip(i - 2 * t - 1, 0, NEAR_PAIRS)
        return s + jnp.concatenate([tabs_ref[0, i0], tabs_ref[0, i1]], axis=1)

    s_scr[...] = scores(t_d)

    def tile(k_it, carry):
        m, acc = carry
        t = t_d - k_it
        s = s_scr[...]
        s_next = scores(jnp.maximum(t - 1, 0))
        m_new = jnp.maximum(m, jnp.max(s, axis=-1, keepdims=True))
        alpha = jnp.exp2(m - m_new)
        p = jnp.exp2(s - m_new).astype(bf16)
        ks = pl.multiple_of(t * tk, tk)
        acc = alpha * acc + jnp.dot(p, vg_ref[0, 0, pl.ds(ks, tk), :], preferred_element_type=f32)
        s_scr[...] = s_next
        return m_new, acc

    _, acc_sel = lax.fori_loop(0, t_d + 1, tile, (jnp.full((rows, 1), NEG, f32), jnp.zeros((rows, 2 * dh), f32)))
    o_sel = acc_sel[:, :dh] / jnp.maximum(acc_sel[:, dh:dh + 1], 1e-30)

    gates = jax.nn.sigmoid(misc_ref[0])
    outs = []
    for h in range(hpg):
        r = slice(h * qb, (h + 1) * qb)
        acc = None
        for jb, branch in enumerate((o_cmp[r], o_sel[r], o_win[r])):
            la = 8 + 3 * h + jb
            lb = 8 + 3 * (hpg + h) + jb
            term = jnp.where(is_g0, gates[:, la:la + 1], gates[:, lb:lb + 1]) * branch
            acc = term if acc is None else acc + term
        outs.append(acc)
    o_ref[0] = jnp.concatenate(outs, axis=1)


def _nsa_tables(rel_bias):
    qb, hpg = NSA_QBLOCK, NSA_HPG
    rows = hpg * qb
    q = (np.arange(rows) % qb)[:, None]
    c = np.arange(PAIR)[None, :]
    d_sel = [PAIR * idx + q - c for idx in range(NEAR_PAIRS + 1)]
    assert PAIR * NEAR_PAIRS - (PAIR - 1) >= FAR_DIST
    mm = 2 * NEAR_PAIRS - 1 - c
    d_cmp = np.maximum(np.where(mm >= -1, NSA_BLOCK * mm + q - (NSA_BLOCK - 1), 0), 0)
    assert NSA_BLOCK * (2 * NEAR_PAIRS) - (NSA_BLOCK - 1) >= FAR_DIST
    cw = np.arange(NSA_WINDOW + qb)[None, :]
    d_win = q + NSA_WINDOW - cw
    d_win = np.where((d_win >= 0) & (d_win < NSA_WINDOW), d_win, -1)
    n_win = (NSA_WINDOW + qb) // PAIR
    tiles = d_sel + [d_cmp] + [d_win[:, k * PAIR:(k + 1) * PAIR] for k in range(n_win)]
    dist = np.broadcast_to(np.stack(tiles)[None], (NSA_KV_GROUPS, len(tiles), rows, PAIR))
    tab = _bias_table(rel_bias, jnp.asarray(dist, i32), slab=qb)
    ns = NEAR_PAIRS + 1
    tabs = tab[:, :ns]
    tabc = tab[:, ns]
    tabw = jnp.concatenate([tab[:, ns + 1 + k] for k in range(n_win)], axis=-1)
    c31 = jnp.repeat(rel_bias.astype(f32)[REL_BUCKETS - 1].reshape(NSA_KV_GROUPS, hpg), qb, axis=1)
    return tabs, tabw, tabc, c31.reshape(NSA_KV_GROUPS, rows, 1)


def _nsa_prompt(q_b, misc, kv4, kvwin, rel_bias):
    bsz, t, _ = q_b.shape
    dh, qb = NSA_DH, NSA_QBLOCK
    nblk = t // NSA_BLOCK
    assert nblk == PAIR and t % (2 * PAIR) == 0
    kcvc = _block_means(kv4, tb=512)
    onehot = (jnp.arange(t, dtype=i32)[:, None] // NSA_BLOCK == jnp.arange(nblk, dtype=i32)[None, :]).astype(bf16)
    kaug = jnp.concatenate([kv4[:, :, 4 * dh:6 * dh].astype(bf16),
                            jnp.broadcast_to(onehot[None], (bsz, t, nblk))], axis=-1)
    ones = jnp.ones((bsz, t, dh), bf16)
    vg = jnp.stack([jnp.concatenate([kv4[:, :, (6 + g) * dh:(7 + g) * dh].astype(bf16), ones], axis=-1)
                    for g in range(NSA_KV_GROUPS)], axis=1)
    pad = ((0, 0), (NSA_WINDOW, 0), (0, 0))
    kw = jnp.pad(kvwin[:, :, :2 * dh].astype(bf16), pad)
    vw = jnp.pad(kvwin[:, :, 2 * dh:].astype(bf16), pad)
    tabs, tabw, tabc, c31 = _nsa_tables(rel_bias)
    tabs, tabw = tabs * LOG2E, tabw * LOG2E
    rows = NSA_HPG * qb
    gw = NSA_HPG * dh
    span = NSA_WINDOW + qb
    return pl.pallas_call(
        _nsa_prompt_kernel_h,
        grid=(bsz, NSA_KV_GROUPS, t // qb),
        in_specs=[pl.BlockSpec((1, qb, gw), lambda b, g, i: (b, i, g)),
                  pl.BlockSpec((1, qb, MISC_W), lambda b, g, i: (b, i, 0)),
                  pl.BlockSpec((1, nblk, 4 * dh), lambda b, g, i: (b, 0, 0)),
                  pl.BlockSpec((1, t, 2 * dh + nblk), lambda b, g, i: (b, 0, 0)),
                  pl.BlockSpec((1, 1, t, 2 * dh), lambda b, g, i: (b, g, 0, 0)),
                  pl.BlockSpec((1, t + NSA_WINDOW, 2 * dh), lambda b, g, i: (b, 0, 0)),
                  pl.BlockSpec((1, t + NSA_WINDOW, 2 * dh), lambda b, g, i: (b, 0, 0)),
                  pl.BlockSpec((1, NEAR_PAIRS + 1, rows, PAIR), lambda b, g, i: (g, 0, 0, 0)),
                  pl.BlockSpec((1, rows, span), lambda b, g, i: (g, 0, 0)),
                  pl.BlockSpec((1, rows, PAIR), lambda b, g, i: (g, 0, 0)),
                  pl.BlockSpec((1, rows, 1), lambda b, g, i: (g, 0, 0))],
        out_specs=pl.BlockSpec((1, qb, gw), lambda b, g, i: (b, i, g)),
        out_shape=jax.ShapeDtypeStruct((bsz, t, NSA_HEADS * dh), f32),
        scratch_shapes=[pltpu.VMEM((rows, 2 * dh + nblk), bf16), pltpu.VMEM((rows, 2 * PAIR), f32)],
        compiler_params=_cparams(("parallel", "parallel", "arbitrary")),
        name="nsa_prompt",
    )(q_b, misc, kcvc, kaug, vg, kw, vw, tabs, tabw, tabc, c31)


MEANS_PAGES = 8


def _nsa_s_means_kernel(pt_ref, *refs):
    del pt_ref
    x_refs, o_ref = refs[:-1], refs[-1]
    rows = [[[] for _ in range(NSA_KV_GROUPS)] for _ in range(2)]
    for x_ref in x_refs:
        page = x_ref.shape[1]
        for j in range(2):
            for g in range(NSA_KV_GROUPS):
                x = x_ref[0, :, j, g, :]
                for n in range(page // NSA_BLOCK):
                    rows[j][g].append(jnp.sum(x[n * NSA_BLOCK:(n + 1) * NSA_BLOCK], axis=0, keepdims=True)
                                      * (1.0 / NSA_BLOCK))
    for j in range(2):
        for g in range(NSA_KV_GROUPS):
            o_ref[0, j * NSA_KV_GROUPS + g] = jnp.concatenate(rows[j][g], axis=0)


def _nsa_s_means(cache_kv, page_table):
    db, n_pages = page_table.shape
    page = cache_kv.shape[1]
    nblk = n_pages * page // NSA_BLOCK
    kp = MEANS_PAGES
    rows = kp * page // NSA_BLOCK
    assert n_pages % kp == 0 and rows % 8 == 0

    def page_spec(k):
        return pl.BlockSpec((1, page, 2, NSA_KV_GROUPS, NSA_DH),
                            lambda b, p, pt: (pt[b * n_pages + p * kp + k], 0, 0, 0, 0))

    return pl.pallas_call(
        _nsa_s_means_kernel,
        grid_spec=pltpu.PrefetchScalarGridSpec(
            num_scalar_prefetch=1,
            grid=(db, n_pages // kp),
            in_specs=[page_spec(k) for k in range(kp)],
            out_specs=pl.BlockSpec((1, 2 * NSA_KV_GROUPS, rows, NSA_DH), lambda b, p, pt: (b, 0, p, 0))),
        out_shape=jax.ShapeDtypeStruct((db, 2 * NSA_KV_GROUPS, nblk, NSA_DH), f32),
        compiler_params=_cparams(("parallel", "parallel")),
        name="nsa_sample_means",
    )(page_table.reshape(-1), *([cache_kv] * kp))


def _nsa_s_scores_kernel(q_ref, kcvc_ref, win_ref, kvn_ref, tc_ref, tw_ref, b0_ref, oc_ref, ow_ref, sel_ref):
    dh, hpg = NSA_DH, NSA_HPG
    q = q_ref[0] * (dh ** -0.5)
    nblk = kcvc_ref.shape[2]
    kvn = kvn_ref[0]
    row = lax.broadcasted_iota(i32, (NSA_HEADS, 1), 0)
    lane = lax.broadcasted_iota(i32, (1, nblk), 1)
    n_sel = NSA_TOPN - 1
    for g in range(NSA_KV_GROUPS):
        in_g = (row >= g * hpg) & (row < (g + 1) * hpg)
        kc = kcvc_ref[0, g]
        vc = kcvc_ref[0, NSA_KV_GROUPS + g]
        s_c = _mm3_nt(q, kc) + tc_ref[...]
        m_c = jnp.max(s_c, axis=-1, keepdims=True)
        p_c = jnp.exp(s_c - m_c)
        p_c = p_c / jnp.maximum(jnp.sum(p_c, axis=-1, keepdims=True), 1e-30)
        o_c = _mm(p_c, vc)
        imp = jnp.sum(jnp.where(in_g, p_c, 0.0), axis=0, keepdims=True)
        forced_blocks = [0] + [nblk - k for k in range(1, NSA_LOCAL)]
        imp = jnp.where((lane == 0) | (lane > nblk - NSA_LOCAL), -jnp.inf, imp)
        picks = jnp.zeros((1, nblk), i32)
        for it, fb in enumerate(forced_blocks):
            picks = jnp.where(lane == it, fb, picks)
        for it in range(len(forced_blocks), n_sel):
            mx = jnp.max(imp, axis=-1, keepdims=True)
            first = jnp.min(jnp.where(imp == mx, lane, nblk), axis=-1, keepdims=True)
            picks = jnp.where(lane == it, first, picks)
            imp = jnp.where(lane == first, -jnp.inf, imp)
        sel_ref[0, g:g + 1, :] = picks
        kw = win_ref[0, :, 0, g, :]
        vw = win_ref[0, :, 1, g, :]
        s_w = _mm3_nt(q, kw) + tw_ref[...]
        s_n = jnp.sum(q * kvn[:, g * dh:(g + 1) * dh], axis=-1, keepdims=True) + b0_ref[...]
        m_w = jnp.maximum(jnp.max(s_w, axis=-1, keepdims=True), s_n)
        p_w = jnp.exp(s_w - m_w)
        p_n = jnp.exp(s_n - m_w)
        l_w = jnp.sum(p_w, axis=-1, keepdims=True) + p_n
        o_w = (_mm(p_w, vw) + p_n * kvn[:, (NSA_KV_GROUPS + g) * dh:(NSA_KV_GROUPS + g + 1) * dh]) / jnp.maximum(l_w, 1e-30)
        if g == 0:
            oc_ref[0] = o_c
            ow_ref[0] = o_w
        else:
            oc_ref[0] = jnp.where(in_g, o_c, oc_ref[0])
            ow_ref[0] = jnp.where(in_g, o_w, ow_ref[0])


def _nsa_s_sel_kernel(sel_ref, pt_ref, q_ref, k_ref, v_ref, tb_ref, kvn_ref, b0_ref, oc_ref, ow_ref, gt_ref,
                      o_ref, m_scr, l_scr, acc_scr):
    del sel_ref, pt_ref
    g = pl.program_id(1)
    j = pl.program_id(2)
    dh, hpg = NSA_DH, NSA_HPG
    is_g0 = g == 0
    q = q_ref[0] * (dh ** -0.5)
    kvn = kvn_ref[0]

    @pl.when(j == 0)
    def _():
        k_n = jnp.where(is_g0, kvn[:, 4 * dh:5 * dh], kvn[:, 5 * dh:6 * dh])
        v_n = jnp.where(is_g0, kvn[:, 6 * dh:7 * dh], kvn[:, 7 * dh:8 * dh])
        m_scr[...] = jnp.sum(q * k_n, axis=-1, keepdims=True) + b0_ref[...]
        l_scr[...] = jnp.ones_like(l_scr)
        acc_scr[...] = jnp.broadcast_to(v_n, acc_scr.shape)

    k = jnp.where(is_g0, k_ref[0, :, 0, 0, :], k_ref[0, :, 0, 1, :])
    v = jnp.where(is_g0, v_ref[0, :, 0, 0, :], v_ref[0, :, 0, 1, :])
    s = _mm3_nt(q, k) + tb_ref[0]
    m_old = m_scr[...]
    m_new = jnp.maximum(m_old, jnp.max(s, axis=-1, keepdims=True))
    alpha = jnp.exp(m_old - m_new)
    p = jnp.exp(s - m_new)
    l_scr[...] = alpha * l_scr[...] + jnp.sum(p, axis=-1, keepdims=True)
    acc_scr[...] = alpha * acc_scr[...] + _mm(p, v)
    m_scr[...] = m_new

    @pl.when(j == pl.num_programs(2) - 1)
    def _():
        o_s = acc_scr[...] / jnp.maximum(l_scr[...], 1e-30)
        gt = gt_ref[0]
        o = gt[:, 0:1] * oc_ref[0] + gt[:, 1:2] * o_s + gt[:, 2:3] * ow_ref[0]
        row = lax.broadcasted_iota(i32, o.shape, 0)
        in_g = (row >= g * hpg) & (row < (g + 1) * hpg)

        @pl.when(is_g0)
        def _():
            o_ref[0] = o

        @pl.when(jnp.logical_not(is_g0))
        def _():
            o_ref[0] = jnp.where(in_g, o, o_ref[0])


def _nsa_sample(q_b, gates, kv4_new, kvwin_new, cache_kv, cache_win, page_table, rel_bias):
    db = q_b.shape[0]
    dh = NSA_DH
    n_pool, page = cache_kv.shape[:2]
    n_pages = page_table.shape[1]
    past = n_pages * page
    nblk = past // NSA_BLOCK
    wb = cache_win.shape[1]
    assert nblk == PAIR and page % NSA_BLOCK == 0 and wb == NSA_WINDOW
    kcvc = _nsa_s_means(cache_kv, page_table)

    n = np.arange(nblk)
    d_cmp = past - (n * NSA_BLOCK + NSA_BLOCK - 1)
    jw = np.arange(wb)
    d_win = np.where(jw >= 1, wb - jw, -1)
    pos = np.arange(nblk * NSA_BLOCK)
    d_sel = past - pos
    width = nblk + wb + nblk * NSA_BLOCK
    dist = np.broadcast_to(np.concatenate([d_cmp, d_win, d_sel])[None, :], (NSA_HEADS, width))
    tab = _bias_table(rel_bias, jnp.asarray(dist[None, None], i32), slab=1)[0, 0]
    t_cmp, t_win = tab[:, :nblk], tab[:, nblk:nblk + wb]
    t_sel = jnp.transpose(tab[:, nblk + wb:].reshape(NSA_HEADS, nblk, NSA_BLOCK), (1, 0, 2))
    b0 = rel_bias.astype(f32)[0].reshape(NSA_HEADS, 1)

    q3 = q_b.reshape(db, NSA_HEADS, dh)
    o_c, o_w, sel = pl.pallas_call(
        _nsa_s_scores_kernel,
        grid=(db,),
        in_specs=[pl.BlockSpec((1, NSA_HEADS, dh), lambda b: (b, 0, 0)),
                  pl.BlockSpec((1, 2 * NSA_KV_GROUPS, nblk, dh), lambda b: (b, 0, 0, 0)),
                  pl.BlockSpec((1, wb, 2, NSA_KV_GROUPS, dh), lambda b: (b, 0, 0, 0, 0)),
                  pl.BlockSpec((1, 1, 4 * dh), lambda b: (b, 0, 0)),
                  pl.BlockSpec((NSA_HEADS, nblk), lambda b: (0, 0)),
                  pl.BlockSpec((NSA_HEADS, wb), lambda b: (0, 0)),
                  pl.BlockSpec((NSA_HEADS, 1), lambda b: (0, 0))],
        out_specs=[pl.BlockSpec((1, NSA_HEADS, dh), lambda b: (b, 0, 0)),
                   pl.BlockSpec((1, NSA_HEADS, dh), lambda b: (b, 0, 0)),
                   pl.BlockSpec((1, NSA_KV_GROUPS, nblk), lambda b: (b, 0, 0))],
        out_shape=[jax.ShapeDtypeStruct((db, NSA_HEADS, dh), f32),
                   jax.ShapeDtypeStruct((db, NSA_HEADS, dh), f32),
                   jax.ShapeDtypeStruct((db, NSA_KV_GROUPS, nblk), i32)],
        compiler_params=_cparams(("parallel",)),
        name="nsa_sample_scores",
    )(q3, kcvc, cache_win.astype(f32), kvwin_new.reshape(db, 1, 4 * dh), t_cmp, t_win, b0)

    n_sel = NSA_TOPN - 1
    sel_flat = sel[:, :, :n_sel].reshape(-1)
    halves = page // NSA_BLOCK
    cache_h = cache_kv.reshape(n_pool * halves, NSA_BLOCK, 4, NSA_KV_GROUPS, dh)

    def blk_of(b, g, j, sel_r, pt_r):
        nb = sel_r[(b * NSA_KV_GROUPS + g) * n_sel + j]
        return nb, pt_r[b * n_pages + nb // halves] * halves + nb % halves

    o = pl.pallas_call(
        _nsa_s_sel_kernel,
        grid_spec=pltpu.PrefetchScalarGridSpec(
            num_scalar_prefetch=2,
            grid=(db, NSA_KV_GROUPS, n_sel),
            in_specs=[pl.BlockSpec((1, NSA_HEADS, dh), lambda b, g, j, s, p: (b, 0, 0)),
                      pl.BlockSpec((1, NSA_BLOCK, 1, NSA_KV_GROUPS, dh),
                                   lambda b, g, j, s, p: (blk_of(b, g, j, s, p)[1], 0, 2, 0, 0)),
                      pl.BlockSpec((1, NSA_BLOCK, 1, NSA_KV_GROUPS, dh),
                                   lambda b, g, j, s, p: (blk_of(b, g, j, s, p)[1], 0, 3, 0, 0)),
                      pl.BlockSpec((1, NSA_HEADS, NSA_BLOCK), lambda b, g, j, s, p: (blk_of(b, g, j, s, p)[0], 0, 0)),
                      pl.BlockSpec((1, 1, 8 * dh), lambda b, g, j, s, p: (b, 0, 0)),
                      pl.BlockSpec((NSA_HEADS, 1), lambda b, g, j, s, p: (0, 0)),
                      pl.BlockSpec((1, NSA_HEADS, dh), lambda b, g, j, s, p: (b, 0, 0)),
                      pl.BlockSpec((1, NSA_HEADS, dh), lambda b, g, j, s, p: (b, 0, 0)),
                      pl.BlockSpec((1, NSA_HEADS, 3), lambda b, g, j, s, p: (b, 0, 0))],
            out_specs=pl.BlockSpec((1, NSA_HEADS, dh), lambda b, g, j, s, p: (b, 0, 0)),
            scratch_shapes=[pltpu.VMEM((NSA_HEADS, 1), f32), pltpu.VMEM((NSA_HEADS, 1), f32),
                            pltpu.VMEM((NSA_HEADS, dh), f32)]),
        out_shape=jax.ShapeDtypeStruct((db, NSA_HEADS, dh), f32),
        compiler_params=_cparams(("parallel", "arbitrary", "arbitrary")),
        name="nsa_sample_selected",
    )(sel_flat, page_table.reshape(-1), q3, cache_h, cache_h, t_sel, kv4_new.reshape(db, 1, 8 * dh), b0,
      o_c, o_w, gates)
    return o.reshape(db, NSA_HEADS * dh)


def _nsa_st_means_kernel(pt_ref, *refs):
    del pt_ref
    x_refs, o_ref = refs[:-1], refs[-1]
    p = pl.program_id(1)

    @pl.when(p == 0)
    def _():
        o_ref[...] = jnp.zeros_like(o_ref)

    page = x_refs[0].shape[-1]
    per_page = page // NSA_BLOCK
    lane = lax.broadcasted_iota(i32, (NSA_DH, page), 1)
    out_lane = lax.broadcasted_iota(i32, (NSA_DH, o_ref.shape[-1]), 1)
    for j in range(2):
        for g in range(NSA_KV_GROUPS):
            acc = o_ref[0, j * NSA_KV_GROUPS + g]
            for k, x_ref in enumerate(x_refs):
                x = x_ref[0, j, g]
                for n in range(per_page):
                    in_blk = (lane >= n * NSA_BLOCK) & (lane < (n + 1) * NSA_BLOCK)
                    mean = jnp.sum(jnp.where(in_blk, x, 0.0), axis=1, keepdims=True) * (1.0 / NSA_BLOCK)
                    col = (p * len(x_refs) + k) * per_page + n
                    acc = jnp.where(out_lane == col, mean, acc)
            o_ref[0, j * NSA_KV_GROUPS + g] = acc


def _nsa_st_means(cache_t, page_table):
    db, n_pages = page_table.shape
    page = cache_t.shape[-1]
    nblk = n_pages * page // NSA_BLOCK
    kp = MEANS_PAGES
    assert n_pages % kp == 0

    def page_spec(k):
        return pl.BlockSpec((1, 2, NSA_KV_GROUPS, NSA_DH, page),
                            lambda b, p, pt: (pt[b * n_pages + p * kp + k], 0, 0, 0, 0))

    return pl.pallas_call(
        _nsa_st_means_kernel,
        grid_spec=pltpu.PrefetchScalarGridSpec(
            num_scalar_prefetch=1,
            grid=(db, n_pages // kp),
            in_specs=[page_spec(k) for k in range(kp)],
            out_specs=pl.BlockSpec((1, 2 * NSA_KV_GROUPS, NSA_DH, nblk), lambda b, p, pt: (b, 0, 0, 0))),
        out_shape=jax.ShapeDtypeStruct((db, 2 * NSA_KV_GROUPS, NSA_DH, nblk), f32),
        compiler_params=_cparams(("parallel", "arbitrary")),
        name="nsa_sample_means",
    )(page_table.reshape(-1), *([cache_t] * kp))


def _nsa_st_scores_kernel(q_ref, kcvc_ref, win_ref, kvn_ref, tc_ref, tw_ref, b0_ref, oc_ref, ow_ref, sel_ref):
    dh, hpg = NSA_DH, NSA_HPG
    q = q_ref[0] * (dh ** -0.5)
    nblk = kcvc_ref.shape[-1]
    kvn = kvn_ref[0]
    row = lax.broadcasted_iota(i32, (NSA_HEADS, 1), 0)
    lane = lax.broadcasted_iota(i32, (1, nblk), 1)
    n_sel = NSA_TOPN - 1
    for g in range(NSA_KV_GROUPS):
        in_g = (row >= g * hpg) & (row < (g + 1) * hpg)
        s_c = _mm3(q, kcvc_ref[0, g]) + tc_ref[...]
        m_c = jnp.max(s_c, axis=-1, keepdims=True)
        p_c = jnp.exp(s_c - m_c)
        p_c = p_c / jnp.maximum(jnp.sum(p_c, axis=-1, keepdims=True), 1e-30)
        o_c = _mm_nt(p_c, kcvc_ref[0, NSA_KV_GROUPS + g])
        imp = jnp.sum(jnp.where(in_g, p_c, 0.0), axis=0, keepdims=True)
        forced_blocks = [0] + [nblk - k for k in range(1, NSA_LOCAL)]
        imp = jnp.where((lane == 0) | (lane > nblk - NSA_LOCAL), -jnp.inf, imp)
        picks = jnp.zeros((1, nblk), i32)
        for it, fb in enumerate(forced_blocks):
            picks = jnp.where(lane == it, fb, picks)
        for it in range(len(forced_blocks), n_sel):
            mx = jnp.max(imp, axis=-1, keepdims=True)
            first = jnp.min(jnp.where(imp == mx, lane, nblk), axis=-1, keepdims=True)
            picks = jnp.where(lane == it, first, picks)
            imp = jnp.where(lane == first, -jnp.inf, imp)
        sel_ref[0, g:g + 1, :] = picks
        s_w = _mm3(q, win_ref[0, 0, g]) + tw_ref[...]
        s_n = jnp.sum(q * kvn[:, g * dh:(g + 1) * dh], axis=-1, keepdims=True) + b0_ref[...]
        m_w = jnp.maximum(jnp.max(s_w, axis=-1, keepdims=True), s_n)
        p_w = jnp.exp(s_w - m_w)
        p_n = jnp.exp(s_n - m_w)
        l_w = jnp.sum(p_w, axis=-1, keepdims=True) + p_n
        v_n = kvn[:, (NSA_KV_GROUPS + g) * dh:(NSA_KV_GROUPS + g + 1) * dh]
        o_w = (_mm_nt(p_w, win_ref[0, 1, g]) + p_n * v_n) / jnp.maximum(l_w, 1e-30)
        if g == 0:
            oc_ref[0] = o_c
            ow_ref[0] = o_w
        else:
            oc_ref[0] = jnp.where(in_g, o_c, oc_ref[0])
            ow_ref[0] = jnp.where(in_g, o_w, ow_ref[0])


def _nsa_st_sel_kernel(sel_ref, pt_ref, q_ref, rb_ref, kvn_ref, oc_ref, ow_ref, gt_ref, *refs, n_sel, past):
    del pt_ref
    k_refs, v_refs, o_ref = refs[:n_sel], refs[n_sel:2 * n_sel], refs[2 * n_sel]
    b = pl.program_id(0)
    g = pl.program_id(1)
    dh, hpg = NSA_DH, NSA_HPG
    is_g0 = g == 0
    page = k_refs[0].shape[-1]
    per_page = page // NSA_BLOCK
    q = q_ref[0] * (dh ** -0.5)
    kvn = kvn_ref[0]
    lane = lax.broadcasted_iota(i32, (NSA_HEADS, page), 1)
    s_parts, d_parts = [], []
    for j in range(n_sel):
        nb = sel_ref[(b * NSA_KV_GROUPS + g) * n_sel + j]
        in_blk = (lane >> int(math.log2(NSA_BLOCK))) == (nb % per_page)
        d_parts.append(jnp.where(in_blk, past - ((nb // per_page) * page + lane), -1))
        s_parts.append(_mm3(q, k_refs[j][0, 0, 0]))
    d = jnp.concatenate(d_parts, axis=1)
    bias = jnp.broadcast_to(rb_ref[0], d.shape)
    for k in range(1, REL_BUCKETS):
        bias = jnp.where(d >= _BUCKET_THR[k - 1], rb_ref[k], bias)
    s = jnp.concatenate(s_parts, axis=1) + jnp.where(d < 0, NEG, bias)
    k_n = jnp.where(is_g0, kvn[:, 4 * dh:5 * dh], kvn[:, 5 * dh:6 * dh])
    v_n = jnp.where(is_g0, kvn[:, 6 * dh:7 * dh], kvn[:, 7 * dh:8 * dh])
    s_n = jnp.sum(q * k_n, axis=-1, keepdims=True) + rb_ref[0]
    m = jnp.maximum(jnp.max(s, axis=-1, keepdims=True), s_n)
    p = jnp.exp(s - m)
    p_n = jnp.exp(s_n - m)
    l = jnp.sum(p, axis=-1, keepdims=True) + p_n
    acc = p_n * v_n
    for j in range(n_sel):
        acc = acc + _mm_nt(p[:, j * page:(j + 1) * page], v_refs[j][0, 0, 0])
    o_s = acc / jnp.maximum(l, 1e-30)
    gt = gt_ref[0]
    o = gt[:, 0:1] * oc_ref[0] + gt[:, 1:2] * o_s + gt[:, 2:3] * ow_ref[0]
    row = lax.broadcasted_iota(i32, o.shape, 0)
    in_g = (row >= g * hpg) & (row < (g + 1) * hpg)

    @pl.when(is_g0)
    def _():
        o_ref[0] = o

    @pl.when(jnp.logical_not(is_g0))
    def _():
        o_ref[0] = jnp.where(in_g, o, o_ref[0])


def _nsa_sample_t(q_b, gates, kv4_new, kvwin_new, cache_kv, cache_win, page_table, rel_bias):
    db = q_b.shape[0]
    dh = NSA_DH
    n_pool, page = cache_kv.shape[:2]
    n_pages = page_table.shape[1]
    past = n_pages * page
    nblk = past // NSA_BLOCK
    wb = cache_win.shape[1]
    assert nblk == PAIR and page % NSA_BLOCK == 0 and wb == NSA_WINDOW and nblk > NSA_LOCAL
    cache_t = jnp.transpose(cache_kv, (0, 2, 3, 4, 1)).astype(f32)
    win_t = jnp.transpose(cache_win, (0, 2, 3, 4, 1)).astype(f32)
    kcvc = _nsa_st_means(cache_t, page_table)

    n = np.arange(nblk)
    d_cmp = past - (n * NSA_BLOCK + NSA_BLOCK - 1)
    jw = np.arange(wb)
    d_win = np.where(jw >= 1, wb - jw, -1)
    dist = np.broadcast_to(np.concatenate([d_cmp, d_win])[None, :], (NSA_HEADS, nblk + wb))
    tab = _bias_table(rel_bias, jnp.asarray(dist[None, None], i32), slab=1)[0, 0]
    t_cmp, t_win = tab[:, :nblk], tab[:, nblk:]
    rb = rel_bias.astype(f32)
    b0 = rb[0].reshape(NSA_HEADS, 1)

    q3 = q_b.reshape(db, NSA_HEADS, dh)
    o_c, o_w, sel = pl.pallas_call(
        _nsa_st_scores_kernel,
        grid=(db,),
        in_specs=[pl.BlockSpec((1, NSA_HEADS, dh), lambda b: (b, 0, 0)),
                  pl.BlockSpec((1, 2 * NSA_KV_GROUPS, dh, nblk), lambda b: (b, 0, 0, 0)),
                  pl.BlockSpec((1, 2, NSA_KV_GROUPS, dh, wb), lambda b: (b, 0, 0, 0, 0)),
                  pl.BlockSpec((1, 1, 4 * dh), lambda b: (b, 0, 0)),
                  pl.BlockSpec((NSA_HEADS, nblk), lambda b: (0, 0)),
                  pl.BlockSpec((NSA_HEADS, wb), lambda b: (0, 0)),
                  pl.BlockSpec((NSA_HEADS, 1), lambda b: (0, 0))],
        out_specs=[pl.BlockSpec((1, NSA_HEADS, dh), lambda b: (b, 0, 0)),
                   pl.BlockSpec((1, NSA_HEADS, dh), lambda b: (b, 0, 0)),
                   pl.BlockSpec((1, NSA_KV_GROUPS, nblk), lambda b: (b, 0, 0))],
        out_shape=[jax.ShapeDtypeStruct((db, NSA_HEADS, dh), f32),
                   jax.ShapeDtypeStruct((db, NSA_HEADS, dh), f32),
                   jax.ShapeDtypeStruct((db, NSA_KV_GROUPS, nblk), i32)],
        compiler_params=_cparams(("parallel",)),
        name="nsa_sample_scores",
    )(q3, kcvc, win_t, kvwin_new.reshape(db, 1, 4 * dh), t_cmp, t_win, b0)

    n_sel = NSA_TOPN - 1
    sel_flat = sel[:, :, :n_sel].reshape(-1)
    per_page = page // NSA_BLOCK

    def page_spec(j, plane):
        def imap(b, g, s, p):
            nb = s[(b * NSA_KV_GROUPS + g) * n_sel + j]
            return (p[b * n_pages + nb // per_page], plane, g, 0, 0)
        return pl.BlockSpec((1, 1, 1, dh, page), imap)

    const = lambda shape: pl.BlockSpec(shape, lambda b, g, s, p: (0,) * len(shape))
    per_b = lambda shape: pl.BlockSpec((1,) + shape, lambda b, g, s, p: (b,) + (0,) * len(shape))
    o = pl.pallas_call(
        functools.partial(_nsa_st_sel_kernel, n_sel=n_sel, past=past),
        grid_spec=pltpu.PrefetchScalarGridSpec(
            num_scalar_prefetch=2,
            grid=(db, NSA_KV_GROUPS),
            in_specs=[per_b((NSA_HEADS, dh)), const((REL_BUCKETS, NSA_HEADS, 1)), per_b((1, 8 * dh)),
                      per_b((NSA_HEADS, dh)), per_b((NSA_HEADS, dh)), per_b((NSA_HEADS, 3))]
                     + [page_spec(j, 2) for j in range(n_sel)] + [page_spec(j, 3) for j in range(n_sel)],
            out_specs=per_b((NSA_HEADS, dh))),
        out_shape=jax.ShapeDtypeStruct((db, NSA_HEADS, dh), f32),
        compiler_params=_cparams(("parallel", "arbitrary")),
        name="nsa_sample_selected",
    )(sel_flat, page_table.reshape(-1), q3, rb.reshape(REL_BUCKETS, NSA_HEADS, 1), kv4_new.reshape(db, 1, 8 * dh),
      o_c, o_w, gates, *([cache_t] * (2 * n_sel)))
    return o.reshape(db, NSA_HEADS * dh)


def _gelu_tanh(x):
    return 0.5 * x * (1.0 + jnp.tanh(math.sqrt(2.0 / math.pi) * (x + 0.044715 * (x * x * x))))


def _lru_gates(xc, wa_ref, wx_ref, ba_ref, bx_ref, lam_ref, prec):
    r_parts, i_parts = [], []
    for n in range(RNN_BLOCKS):
        xb = xc[:, n * RNN_BW:(n + 1) * RNN_BW]
        r_parts.append(_dotp(xb, wa_ref[n], prec))
        i_parts.append(_dotp(xb, wx_ref[n], prec))
    r = jax.nn.sigmoid(jnp.concatenate(r_parts, axis=1) + ba_ref[...])
    i = jax.nn.sigmoid(jnp.concatenate(i_parts, axis=1) + bx_ref[...])
    log_a = -RG_C * r * _softplus(-lam_ref[...])
    a = jnp.exp(log_a)
    t = jnp.tanh(log_a)
    b = jnp.sqrt(jnp.maximum(-2.0 * t / (1.0 - t), 0.0)) * (i * xc)
    return a, b


def _lru_kernel(rec_ref, gate_ref, cw_ref, cb_ref, wa_ref, wx_ref, ba_ref, bx_ref, lam_ref, cinit_ref, h0_ref,
                y_ref, hfin_ref, xc_scr, a_scr, b_scr, hs_scr, h_scr, *, tb):
    j = pl.program_id(1)

    @pl.when(j == 0)
    def _():
        xc_scr[0:8, :] = cinit_ref[0]
        h_scr[...] = h0_ref[0]

    x = rec_ref[0]
    xc_scr[8:8 + tb, :] = x
    xc = xc_scr[5:5 + tb, :] * cw_ref[0:1, :]
    xc = xc + xc_scr[6:6 + tb, :] * cw_ref[1:2, :]
    xc = xc + xc_scr[7:7 + tb, :] * cw_ref[2:3, :]
    xc = xc + x * cw_ref[3:4, :]
    xc = xc + cb_ref[...]
    xc_scr[0:8, :] = xc_scr[tb:tb + 8, :]
    a, b = _lru_gates(xc, wa_ref, wx_ref, ba_ref, bx_ref, lam_ref, 1)
    a_scr[...] = a
    b_scr[...] = b

    def step(t, h):
        h = a_scr[pl.ds(t, 1), :] * h + b_scr[pl.ds(t, 1), :]
        hs_scr[pl.ds(t, 1), :] = h
        return h

    h = lax.fori_loop(0, tb, step, h_scr[...], unroll=8)
    h_scr[...] = h
    y_ref[0] = _gelu_tanh(gate_ref[0]) * hs_scr[...]

    @pl.when(j == pl.num_programs(1) - 1)
    def _():
        hfin_ref[0] = h


def _lru_prompt(rec, gate, conv_w, conv_b, wa, wx, ba, bx, lam, *, tb):
    bsz, t, w = rec.shape
    row = lambda a: a.reshape(1, w).astype(f32)
    cinit = jnp.zeros((bsz, 8, w), f32)
    h0 = jnp.zeros((bsz, 1, w), f32)
    full = lambda shape: pl.BlockSpec(shape, lambda b, j: (0,) * len(shape))
    return pl.pallas_call(
        functools.partial(_lru_kernel, tb=tb),
        grid=(bsz, t // tb),
        in_specs=[pl.BlockSpec((1, tb, w), lambda b, j: (b, j, 0)),
                  pl.BlockSpec((1, tb, w), lambda b, j: (b, j, 0)),
                  full((RNN_CONV, w)), full((1, w)),
                  full((RNN_BLOCKS, RNN_BW, RNN_BW)), full((RNN_BLOCKS, RNN_BW, RNN_BW)),
                  full((1, w)), full((1, w)), full((1, w)),
                  pl.BlockSpec((1, 8, w), lambda b, j: (b, 0, 0)),
                  pl.BlockSpec((1, 1, w), lambda b, j: (b, 0, 0))],
        out_specs=[pl.BlockSpec((1, tb, w), lambda b, j: (b, j, 0)),
                   pl.BlockSpec((1, 1, w), lambda b, j: (b, 0, 0))],
        out_shape=[jax.ShapeDtypeStruct((bsz, t, w), f32), jax.ShapeDtypeStruct((bsz, 1, w), f32)],
        scratch_shapes=[pltpu.VMEM((tb + 8, w), f32), pltpu.VMEM((tb, w), f32), pltpu.VMEM((tb, w), f32),
                        pltpu.VMEM((tb, w), f32), pltpu.VMEM((1, w), f32)],
        compiler_params=_cparams(("parallel", "arbitrary")),
        name="rglru",
    )(rec, gate, conv_w.astype(f32), row(conv_b), wa.astype(f32), wx.astype(f32), row(ba), row(bx), row(lam),
      cinit, h0)


def _lru_step_kernel(rec_ref, gate_ref, b0_ref, b1_ref, b2_ref, cw_ref, cb_ref, wa_ref, wx_ref, ba_ref, bx_ref,
                     lam_ref, h0_ref, y_ref, h_ref):
    xc = b0_ref[...] * cw_ref[0:1, :]
    xc = xc + b1_ref[...] * cw_ref[1:2, :]
    xc = xc + b2_ref[...] * cw_ref[2:3, :]
    xc = xc + rec_ref[...] * cw_ref[3:4, :]
    xc = xc + cb_ref[...]
    a, b = _lru_gates(xc, wa_ref, wx_ref, ba_ref, bx_ref, lam_ref, 3)
    h = a * h0_ref[...] + b
    h_ref[...] = h
    y_ref[...] = _gelu_tanh(gate_ref[...]) * h


def _lru_sample(rec, gate, conv_buf, h0, conv_w, conv_b, wa, wx, ba, bx, lam):
    db, w = rec.shape
    row = lambda a: a.reshape(1, w).astype(f32)
    buf = conv_buf.astype(f32)
    return pl.pallas_call(
        _lru_step_kernel,
        out_shape=[jax.ShapeDtypeStruct((db, w), f32), jax.ShapeDtypeStruct((db, w), f32)],
        compiler_params=pltpu.CompilerParams(vmem_limit_bytes=VMEM_LIMIT),
        name="rglru_step",
    )(rec, gate, buf[:, 0], buf[:, 1], buf[:, 2], conv_w.astype(f32), row(conv_b), wa.astype(f32), wx.astype(f32),
      row(ba), row(bx), row(lam), h0.astype(f32))


def _top2_kernel(l_ref, e_ref, g_ref):
    lg = l_ref[...]
    lane = lax.broadcasted_iota(i32, lg.shape, 1)
    lg = jnp.where(lane < N_EXPERTS, lg, -jnp.inf)
    m1 = jnp.max(lg, axis=-1, keepdims=True)
    i1 = jnp.min(jnp.where(lg == m1, lane, lg.shape[1]), axis=-1, keepdims=True)
    lg2 = jnp.where(lane == i1, -jnp.inf, lg)
    m2 = jnp.max(lg2, axis=-1, keepdims=True)
    i2 = jnp.min(jnp.where(lg2 == m2, lane, lg.shape[1]), axis=-1, keepdims=True)
    e2 = jnp.exp(m2 - m1)
    den = 1.0 + e2
    e_ref[...] = jnp.where(lane == 0, i1, jnp.where(lane == 1, i2, 0))
    g_ref[...] = jnp.where(lane == 0, 1.0 / den, jnp.where(lane == 1, e2 / den, 0.0))


def _top2(logits, *, tm):
    n, w = logits.shape
    return pl.pallas_call(
        _top2_kernel,
        grid=(n // tm,),
        in_specs=[pl.BlockSpec((tm, w), lambda i: (i, 0))],
        out_specs=[pl.BlockSpec((tm, w), lambda i: (i, 0)), pl.BlockSpec((tm, w), lambda i: (i, 0))],
        out_shape=[jax.ShapeDtypeStruct((n, w), i32), jax.ShapeDtypeStruct((n, w), f32)],
        compiler_params=_cparams(("parallel",)),
        name="moe_top2",
    )(logits)


def _moe_kernel(be_ref, nu_ref, xs_ref, w1_ref, w3_ref, w2_ref, o_ref, acc_scr):
    del be_ref
    i = pl.program_id(0)
    f = pl.program_id(1)
    last = pl.num_programs(1) - 1
    used = i < nu_ref[0]

    @pl.when(used)
    def _():
        @pl.when(f == 0)
        def _():
            acc_scr[...] = jnp.zeros_like(acc_scr)

        x = xs_ref[...]
        act = _silu(_mm(x, w1_ref[0])) * _mm(x, w3_ref[0])
        acc_scr[...] += _mm(act, w2_ref[0])

        @pl.when(f == last)
        def _():
            o_ref[...] = acc_scr[...]

    @pl.when(jnp.logical_not(used) & (f == last))
    def _():
        o_ref[...] = jnp.zeros_like(o_ref)


def _moe_experts(xs, blk_e, n_used, w1, w3, w2, *, tm, tf):
    rows, d = xs.shape
    fdim = w1.shape[2]
    nf = fdim // tf

    def wcol(i, f, be, nu):
        return (be[i], 0, jnp.where(i < nu[0], f, nf - 1))

    def wrow(i, f, be, nu):
        return (be[i], jnp.where(i < nu[0], f, nf - 1), 0)

    return pl.pallas_call(
        _moe_kernel,
        grid_spec=pltpu.PrefetchScalarGridSpec(
            num_scalar_prefetch=2,
            grid=(rows // tm, nf),
            in_specs=[pl.BlockSpec((tm, d), lambda i, f, be, nu: (i, 0)),
                      pl.BlockSpec((1, d, tf), wcol),
                      pl.BlockSpec((1, d, tf), wcol),
                      pl.BlockSpec((1, tf, d), wrow)],
            out_specs=pl.BlockSpec((tm, d), lambda i, f, be, nu: (i, 0)),
            scratch_shapes=[pltpu.VMEM((tm, d), f32)]),
        out_shape=jax.ShapeDtypeStruct((rows, d), f32),
        compiler_params=_cparams(("arbitrary", "arbitrary")),
        name="moe_experts",
    )(blk_e, n_used, xs, w1, w3, w2)


def _combine_kernel(x_ref, gate_ref, y0_ref, y1_ref, g_ref, nw_ref, o_ref):
    gw = g_ref[...]
    y = y0_ref[...] * gw[:, 0:1] + y1_ref[...] * gw[:, 1:2]
    x = x_ref[0] + gate_ref[0] * y
    o_ref[0] = x * lax.rsqrt(jnp.mean(x * x, axis=-1, keepdims=True) + EPS) * nw_ref[...]


def _moe_combine(x, gate, y0, y1, gw, norm_w, *, tm, row0):
    bsz, t, d = x.shape
    assert row0 % tm == 0
    tok = pl.BlockSpec((1, tm, d), lambda b, i: (b, i, 0))
    flat = lambda w: pl.BlockSpec((tm, w), lambda b, i: (row0 // tm + b * (t // tm) + i, 0))
    return pl.pallas_call(
        _combine_kernel,
        grid=(bsz, t // tm),
        in_specs=[tok, _mod_spec(gate, tm), flat(d), flat(d), flat(gw.shape[-1]),
                  pl.BlockSpec((1, d), lambda b, i: (0, 0))],
        out_specs=tok,
        out_shape=jax.ShapeDtypeStruct((bsz, t, d), f32),
        compiler_params=_cparams(("parallel", "parallel")),
        name="moe_combine",
    )(x, gate, y0, y1, gw, norm_w.reshape(1, d).astype(f32))


MOE_TM = 1024
MOE_TF = 512


def _moe_dispatch(e_idx, n_tok):
    tm = MOE_TM
    n_assign = n_tok * TOP_K
    e_flat = e_idx.reshape(n_assign)
    onehot = (e_flat[:, None] == jnp.arange(N_EXPERTS, dtype=i32)[None, :]).astype(i32)
    rank = jnp.take_along_axis(jnp.cumsum(onehot, axis=0), e_flat[:, None], axis=1)[:, 0] - 1
    counts = jnp.sum(onehot, axis=0)
    padded = (counts + tm - 1) // tm * tm
    pend = jnp.cumsum(padded)
    pstart = pend - padded
    dest = pstart[e_flat] + rank
    n_blocks = -(-n_assign // tm) + N_EXPERTS
    rows = n_blocks * tm
    tok_flat = jnp.repeat(jnp.arange(n_tok, dtype=i32), TOP_K)
    row_tok = jnp.zeros((rows,), i32).at[dest].set(tok_flat)
    n_used = (pend[-1] // tm).astype(i32)
    blk = jnp.minimum(jnp.arange(n_blocks, dtype=i32), n_used - 1) * tm
    blk_e = jnp.minimum(jnp.searchsorted(pend, blk, side='right'), N_EXPERTS - 1).astype(i32)
    return dest.reshape(n_tok, TOP_K), row_tok, blk_e, n_used.reshape(1)


def _w_in0_layout(w_in0):
    c_qkv = GDN_CONV_CH
    c_z = c_qkv + GDN_HEADS * GDN_DV
    c_ab = c_z + 2 * GDN_HEADS
    c_q = c_ab + NSA_HEADS * NSA_DH
    c_kv = c_q + 6 * NSA_KV_GROUPS * NSA_DH
    c_g = c_kv + 3 * NSA_HEADS
    assert c_g == w_in0.shape[1]
    n_misc = 2 * GDN_HEADS + 3 * NSA_HEADS
    w = jnp.concatenate([w_in0[:, :c_z], w_in0[:, c_ab:c_kv], w_in0[:, c_z:c_ab], w_in0[:, c_kv:c_g],
                         jnp.zeros((w_in0.shape[0], MISC_W - n_misc), w_in0.dtype)], axis=1)
    widths = (GDN_CONV_CH, GDN_HEADS * GDN_DV, NSA_HEADS * NSA_DH, 4 * NSA_KV_GROUPS * NSA_DH,
              2 * NSA_KV_GROUPS * NSA_DH, MISC_W)
    splits, s = [], 0
    for wd in widths:
        splits.append((s, s + wd))
        s += wd
    return w, tuple(splits)


def kernel(x_prompt, x_sample, c_prompt, c_sample, cache_nsa_kv, cache_nsa_win, state_gdn, state_gdn_conv, state_lru, state_lru_conv, page_table, rel_bias, w_ada, b_ada, norm_mix, norm_ffn, norm_final, w_in0, gdn_conv_w, gdn_a_log, gdn_dt_bias, gdn_norm_w, w_out0, ffn_w_gate, ffn_w_up, ffn_w_down, w_in1, lru_conv_w, lru_conv_b, lru_wa, lru_ba, lru_wx, lru_bx, lru_lambda, w_out1, moe_router, moe_w1, moe_w3, moe_w2):
    bsz, seq, d = x_prompt.shape
    db = x_sample.shape[0]
    assert x_sample.shape[1] == 1
    dh = NSA_DH

    n_c = bsz + db
    n_c_pad = -(-n_c // 8) * 8
    c_all = jnp.concatenate([c_prompt, c_sample, jnp.zeros((n_c_pad - n_c, d), f32)], axis=0)
    mods = _adaln(c_all, w_ada, b_ada).reshape(2, n_c_pad, N_MOD, d)
    mod_p = [[mods[l, :bsz, k].reshape(bsz, 1, d) for k in range(N_MOD)] for l in range(2)]
    mod_s = [[mods[l, bsz:n_c, k].reshape(1, db, d) for k in range(N_MOD)] for l in range(2)]

    w0, splits0 = _w_in0_layout(w_in0)
    splits1 = ((0, RNN_WIDTH), (RNN_WIDTH, 2 * RNN_WIDTH))
    router = jnp.concatenate([moe_router, jnp.zeros((d, MISC_W - N_EXPERTS), f32)], axis=1)
    bf = lambda w: w.astype(bf16)

    tm = 512
    xp = x_prompt
    qkv, z, q_b, kv4, kvwin, misc = _mod_matmul(xp, norm_mix[0], mod_p[0][0], mod_p[0][1], bf(w0), splits0,
                                                tm=tm, prec=1)
    o_a, p_gdn = _gdn(qkv, z, misc, gdn_conv_w, jnp.zeros((bsz, GDN_CONV - 1, GDN_CONV_CH), f32), gdn_a_log,
                      gdn_dt_bias, gdn_norm_w, jnp.zeros((bsz, GDN_HEADS, GDN_DK, GDN_DV), f32), tb=256, n_valid=seq)
    p_gdn_conv = qkv[:, seq - (GDN_CONV - 1):]
    o_b = _nsa_prompt(q_b, misc, kv4, kvwin, rel_bias)
    p_nsa_kv = kv4.reshape(bsz, seq, 4, NSA_KV_GROUPS, dh)
    keep = min(NSA_WINDOW, seq)
    p_nsa_win = kvwin[:, seq - keep:].reshape(bsz, keep, 2, NSA_KV_GROUPS, dh)
    xp = _proj_residual([o_a, o_b], bf(w_out0), xp, mod_p[0][2], tm=tm, prec=1)
    xp = _ffn(xp, norm_ffn[0], mod_p[0][3], mod_p[0][4], mod_p[0][5], bf(ffn_w_gate), bf(ffn_w_up), bf(ffn_w_down),
              tm=tm, tf=FFN_DIM // 2, prec=1)
    gate_br, rec_br = _mod_matmul(xp, norm_mix[1], mod_p[1][0], mod_p[1][1], bf(w_in1), splits1, tm=tm, prec=1)
    y_in, p_lru = _lru_prompt(rec_br, gate_br, lru_conv_w, lru_conv_b, lru_wa, lru_wx, lru_ba, lru_bx, lru_lambda,
                              tb=256)
    p_lru_conv = rec_br[:, seq - (RNN_CONV - 1):]
    xp = _proj_residual([y_in], bf(w_out1), xp, mod_p[1][2], tm=tm, prec=1)
    logit_p, h_p = _mod_matmul(xp, norm_ffn[1], mod_p[1][3], mod_p[1][4], router, ((0, MISC_W),), tm=tm, prec=3,
                               emit_h=f32)

    xs = x_sample.reshape(1, db, d)
    qkv_s, z_s, q_s, kv4_s, kvwin_s, misc_s = _mod_matmul(xs, norm_mix[0], mod_s[0][0], mod_s[0][1], w0, splits0,
                                                          tm=db, prec=3)
    c = GDN_CHUNK
    tpad = lambda a: jnp.pad(a.reshape(db, 1, a.shape[-1]), ((0, 0), (0, c - 1), (0, 0)))
    o_a_s, s_gdn = _gdn(tpad(qkv_s), tpad(z_s), tpad(misc_s), gdn_conv_w, state_gdn_conv, gdn_a_log, gdn_dt_bias,
                        gdn_norm_w, state_gdn, tb=c, n_valid=1)
    o_a_s = o_a_s[:, 0].reshape(1, db, GDN_HEADS * GDN_DV)
    s_gdn_conv = jnp.concatenate([state_gdn_conv[:, 1:], qkv_s.reshape(db, 1, GDN_CONV_CH)], axis=1)
    gates_s = jax.nn.sigmoid(misc_s[0, :, 2 * GDN_HEADS:2 * GDN_HEADS + 3 * NSA_HEADS]).reshape(db, NSA_HEADS, 3)
    o_b_s = _nsa_sample_t(q_s[0], gates_s, kv4_s[0], kvwin_s[0], cache_nsa_kv, cache_nsa_win, page_table, rel_bias)
    s_nsa_kv = kv4_s.reshape(db, 1, 4, NSA_KV_GROUPS, dh)
    s_nsa_win = jnp.concatenate([cache_nsa_win[:, 1:],
                                 kvwin_s.reshape(db, 1, 2, NSA_KV_GROUPS, dh).astype(cache_nsa_win.dtype)], axis=1)
    xs = _proj_residual([o_a_s, o_b_s.reshape(1, db, NSA_HEADS * dh)], w_out0, xs, mod_s[0][2], tm=db, prec=3)
    xs = _ffn(xs, norm_ffn[0], mod_s[0][3], mod_s[0][4], mod_s[0][5], ffn_w_gate, ffn_w_up, ffn_w_down,
              tm=db, tf=256, prec=3)
    gate_s, rec_s = _mod_matmul(xs, norm_mix[1], mod_s[1][0], mod_s[1][1], w_in1, splits1, tm=db, prec=3)
    y_in_s, s_lru = _lru_sample(rec_s[0], gate_s[0], state_lru_conv, state_lru, lru_conv_w, lru_conv_b, lru_wa, lru_wx,
                                lru_ba, lru_bx, lru_lambda)
    s_lru_conv = jnp.concatenate([state_lru_conv[:, 1:], rec_s.reshape(db, 1, RNN_WIDTH)], axis=1)
    xs = _proj_residual([y_in_s.reshape(1, db, RNN_WIDTH)], w_out1, xs, mod_s[1][2], tm=db, prec=3)
    logit_s, h_s = _mod_matmul(xs, norm_ffn[1], mod_s[1][3], mod_s[1][4], router, ((0, MISC_W),), tm=db, prec=3,
                               emit_h=f32)

    n_p = bsz * seq
    n_tok = n_p + db
    logits = jnp.concatenate([logit_p.reshape(n_p, MISC_W), logit_s.reshape(db, MISC_W)], axis=0)
    h_all = jnp.concatenate([h_p.reshape(n_p, d), h_s.reshape(db, d)], axis=0)
    e_idx, gw = _top2(logits, tm=max(t for t in range(8, 1025, 8) if n_tok % t == 0))
    dest, row_tok, blk_e, n_used = _moe_dispatch(e_idx[:, :TOP_K], n_tok)
    yb = _moe_experts(h_all[row_tok], blk_e, n_used, moe_w1, moe_w3, moe_w2, tm=MOE_TM, tf=MOE_TF)
    y0, y1 = yb[dest[:, 0]], yb[dest[:, 1]]
    y_prompt = _moe_combine(xp, mod_p[1][5], y0, y1, gw, norm_final, tm=tm, row0=0)
    y_sample = _moe_combine(xs, mod_s[1][5], y0, y1, gw, norm_final, tm=db, row0=n_p)

    return (y_prompt, y_sample.reshape(db, 1, d),
            p_nsa_kv, p_nsa_win, p_gdn, p_gdn_conv, p_lru.reshape(bsz, RNN_WIDTH), p_lru_conv,
            s_nsa_kv, s_nsa_win, s_gdn, s_gdn_conv, s_lru, s_lru_conv)
```

```python
import functools
import math

import numpy as np
import jax
import jax.numpy as jnp
from jax import lax
from jax.experimental import pallas as pl
from jax.experimental.pallas import tpu as pltpu

f32 = jnp.float32
bf16 = jnp.bfloat16
i32 = jnp.int32

D_MODEL = 1024
EPS = 1e-6
N_MOD = 6
GDN_HEADS = 4
GDN_DK = 128
GDN_DV = 128
GDN_CONV = 4
GDN_CHUNK = 64
GDN_CONV_CH = GDN_HEADS * (2 * GDN_DK + GDN_DV)
NSA_HEADS = 8
NSA_KV_GROUPS = 2
NSA_HPG = NSA_HEADS // NSA_KV_GROUPS
NSA_DH = 64
NSA_BLOCK = 64
NSA_TOPN = 16
NSA_LOCAL = 2
NSA_WINDOW = 512
NSA_QBLOCK = 128
NSA_FORCE = 1.0e4
REL_BUCKETS = 32
REL_MAX_DIST = 2048
RNN_WIDTH = D_MODEL
RNN_BLOCKS = 8
RNN_BW = RNN_WIDTH // RNN_BLOCKS
RNN_CONV = 4
RG_C = 8.0
FFN_DIM = 2816
N_EXPERTS = 8
TOP_K = 2
EXPERT_DIM = 3584

NEG = -1e30
PAIR = 2 * NSA_BLOCK
NEAR_PAIRS = 13
MISC_W = 128
VMEM_LIMIT = 56 * 1024 * 1024


def _cparams(sem, vmem=VMEM_LIMIT):
    return pltpu.CompilerParams(dimension_semantics=sem, vmem_limit_bytes=vmem)


def _mm(a, b):
    return jnp.dot(a.astype(bf16), b.astype(bf16), preferred_element_type=f32)


def _mm_nt(a, b):
    return lax.dot_general(a.astype(bf16), b.astype(bf16), (((1,), (1,)), ((), ())),
                           preferred_element_type=f32)


def _split2(a):
    hi = a.astype(bf16)
    lo = (a - hi.astype(f32)).astype(bf16)
    return hi, lo


def _mm3(a, b):
    ah, al = _split2(a)
    bh, bl = _split2(b)
    return (jnp.dot(ah, bh, preferred_element_type=f32) + jnp.dot(ah, bl, preferred_element_type=f32)
            + jnp.dot(al, bh, preferred_element_type=f32))


def _mm3_nt(a, b):
    ah, al = _split2(a)
    bh, bl = _split2(b)
    dn = (((1,), (1,)), ((), ()))
    return (lax.dot_general(ah, bh, dn, preferred_element_type=f32)
            + lax.dot_general(ah, bl, dn, preferred_element_type=f32)
            + lax.dot_general(al, bh, dn, preferred_element_type=f32))


def _mm_01(m01, a):
    hi = a.astype(bf16)
    r1 = a - hi.astype(f32)
    mid = r1.astype(bf16)
    lo = (r1 - mid.astype(f32)).astype(bf16)
    return (jnp.dot(m01, hi, preferred_element_type=f32) + jnp.dot(m01, mid, preferred_element_type=f32)
            + jnp.dot(m01, lo, preferred_element_type=f32))


def _dotp(a, b, prec):
    return _mm3(a, b) if prec == 3 else _mm(a, b)


def _silu(x):
    return x * jax.nn.sigmoid(x)


def _softplus(x):
    return jnp.maximum(x, 0.0) + jnp.log1p(jnp.exp(-jnp.abs(x)))


def _modulate(x, gain, shift, scale):
    r = lax.rsqrt(jnp.mean(x * x, axis=-1, keepdims=True) + EPS)
    return x * r * gain * (1.0 + scale) + shift


def _ada_kernel(c_ref, w_ref, b_ref, o_ref):
    o_ref[0] = _mm3(_silu(c_ref[...]), w_ref[0]) + b_ref[0]


def _adaln(c_all, w_ada, b_ada):
    rows = c_all.shape[0]
    depth, d, n = w_ada.shape
    tn = 1536
    return pl.pallas_call(
        _ada_kernel,
        grid=(depth, n // tn),
        in_specs=[pl.BlockSpec((rows, d), lambda l, j: (0, 0)),
                  pl.BlockSpec((1, d, tn), lambda l, j: (l, 0, j)),
                  pl.BlockSpec((1, 1, tn), lambda l, j: (l, 0, j))],
        out_specs=pl.BlockSpec((1, rows, tn), lambda l, j: (l, 0, j)),
        out_shape=jax.ShapeDtypeStruct((depth, rows, n), f32),
        compiler_params=_cparams(("arbitrary", "arbitrary")),
        name="adaln",
    )(c_all, w_ada, b_ada.reshape(depth, 1, n))


def _mod_spec(mod, tm):
    r = mod.shape[1]
    if r == 1:
        return pl.BlockSpec((1, 1, mod.shape[2]), lambda b, i: (b, 0, 0))
    return pl.BlockSpec((1, tm, mod.shape[2]), lambda b, i: (b, i, 0))


def _modmm_kernel(x_ref, gain_ref, shift_ref, scale_ref, w_ref, *o_refs, splits, prec, emit_h):
    h = _modulate(x_ref[0], gain_ref[...], shift_ref[0], scale_ref[0])
    if emit_h:
        o_refs[-1][0] = h.astype(o_refs[-1].dtype)
    hh = _split2(h) if prec == 3 else h.astype(bf16)
    for o_ref, (a, b) in zip(o_refs, splits):
        w = w_ref[:, a:b]
        if prec == 3:
            wh, wl = _split2(w)
            acc = (jnp.dot(hh[0], wh, preferred_element_type=f32) + jnp.dot(hh[0], wl, preferred_element_type=f32)
                   + jnp.dot(hh[1], wh, preferred_element_type=f32))
        else:
            acc = jnp.dot(hh, w, preferred_element_type=f32)
        o_ref[0] = acc


def _mod_matmul(x, gain, shift, scale, w, splits, *, tm, prec, emit_h=None):
    bsz, t, d = x.shape
    out_shape = [jax.ShapeDtypeStruct((bsz, t, b - a), f32) for a, b in splits]
    out_specs = [pl.BlockSpec((1, tm, b - a), lambda bi, i: (bi, i, 0)) for a, b in splits]
    if emit_h is not None:
        out_shape.append(jax.ShapeDtypeStruct((bsz, t, d), emit_h))
        out_specs.append(pl.BlockSpec((1, tm, d), lambda bi, i: (bi, i, 0)))
    return pl.pallas_call(
        functools.partial(_modmm_kernel, splits=tuple(splits), prec=prec, emit_h=emit_h is not None),
        grid=(bsz, t // tm),
        in_specs=[pl.BlockSpec((1, tm, d), lambda bi, i: (bi, i, 0)),
                  pl.BlockSpec((1, d), lambda bi, i: (0, 0)),
                  _mod_spec(shift, tm), _mod_spec(scale, tm),
                  pl.BlockSpec(w.shape, lambda bi, i: (0, 0))],
        out_specs=out_specs,
        out_shape=out_shape,
        compiler_params=_cparams(("parallel", "parallel")),
        name="mod_matmul",
    )(x, gain.reshape(1, d), shift, scale, w)


def _projres_kernel(*refs, n_lhs, ksplits, prec, final_norm):
    lhs = refs[:n_lhs]
    w_ref, x_ref, gate_ref = refs[n_lhs:n_lhs + 3]
    o_ref = refs[-1]
    acc = None
    for l_ref, (a, b) in zip(lhs, ksplits):
        part = _dotp(l_ref[0], w_ref[a:b, :], prec)
        acc = part if acc is None else acc + part
    y = x_ref[0] + gate_ref[0] * acc
    if final_norm:
        nw_ref = refs[n_lhs + 3]
        y = y * lax.rsqrt(jnp.mean(y * y, axis=-1, keepdims=True) + EPS) * nw_ref[...]
    o_ref[0] = y


def _proj_residual(lhs_list, w, x, gate, *, tm, prec, norm_w=None):
    bsz, t, d = x.shape
    ksplits, k0 = [], 0
    for l in lhs_list:
        ksplits.append((k0, k0 + l.shape[-1]))
        k0 += l.shape[-1]
    in_specs = [pl.BlockSpec((1, tm, l.shape[-1]), lambda bi, i: (bi, i, 0)) for l in lhs_list]
    in_specs += [pl.BlockSpec(w.shape, lambda bi, i: (0, 0)),
                 pl.BlockSpec((1, tm, d), lambda bi, i: (bi, i, 0)),
                 _mod_spec(gate, tm)]
    args = list(lhs_list) + [w, x, gate]
    if norm_w is not None:
        in_specs.append(pl.BlockSpec((1, d), lambda bi, i: (0, 0)))
        args.append(norm_w.reshape(1, d))
    return pl.pallas_call(
        functools.partial(_projres_kernel, n_lhs=len(lhs_list), ksplits=tuple(ksplits), prec=prec,
                          final_norm=norm_w is not None),
        grid=(bsz, t // tm),
        in_specs=in_specs,
        out_specs=pl.BlockSpec((1, tm, d), lambda bi, i: (bi, i, 0)),
        out_shape=jax.ShapeDtypeStruct((bsz, t, d), f32),
        compiler_params=_cparams(("parallel", "parallel")),
        name="proj_residual",
    )(*args)


def _ffn_kernel(x_ref, gain_ref, shift_ref, scale_ref, gate_ref, wg_ref, wu_ref, wd_ref, o_ref,
                h_scr, acc_scr, *, prec):
    f = pl.program_id(2)

    @pl.when(f == 0)
    def _():
        h_scr[...] = _modulate(x_ref[0], gain_ref[...], shift_ref[0], scale_ref[0]).astype(h_scr.dtype)
        acc_scr[...] = jnp.zeros_like(acc_scr)

    h = h_scr[...]
    act = _silu(_dotp(h, wg_ref[...], prec)) * _dotp(h, wu_ref[...], prec)
    acc_scr[...] += _dotp(act, wd_ref[...], prec)

    @pl.when(f == pl.num_programs(2) - 1)
    def _():
        o_ref[0] = x_ref[0] + gate_ref[0] * acc_scr[...]


def _ffn(x, gain, shift, scale, gate, wg, wu, wd, *, tm, tf, prec):
    bsz, t, d = x.shape
    fdim = wg.shape[1]
    mod_specs = []
    for mod in (shift, scale, gate):
        if mod.shape[1] == 1:
            mod_specs.append(pl.BlockSpec((1, 1, d), lambda b, i, f: (b, 0, 0)))
        else:
            mod_specs.append(pl.BlockSpec((1, tm, d), lambda b, i, f: (b, i, 0)))
    return pl.pallas_call(
        functools.partial(_ffn_kernel, prec=prec),
        grid=(bsz, t // tm, fdim // tf),
        in_specs=[pl.BlockSpec((1, tm, d), lambda b, i, f: (b, i, 0)),
                  pl.BlockSpec((1, d), lambda b, i, f: (0, 0)),
                  mod_specs[0], mod_specs[1], mod_specs[2],
                  pl.BlockSpec((d, tf), lambda b, i, f: (0, f)),
                  pl.BlockSpec((d, tf), lambda b, i, f: (0, f)),
                  pl.BlockSpec((tf, d), lambda b, i, f: (f, 0))],
        out_specs=pl.BlockSpec((1, tm, d), lambda b, i, f: (b, i, 0)),
        out_shape=jax.ShapeDtypeStruct((bsz, t, d), f32),
        scratch_shapes=[pltpu.VMEM((tm, d), f32 if prec == 3 else bf16), pltpu.VMEM((tm, d), f32)],
        compiler_params=_cparams(("parallel", "parallel", "arbitrary")),
        name="ffn",
    )(x, gain.reshape(1, d), shift, scale, gate, wg, wu, wd)


def _gdn_kernel(qkv_ref, z_ref, misc_ref, cw_ref, cinit_ref, hp_ref, nw_ref, s0_ref,
                o_ref, sfin_ref, xc_scr, s_scr, *, tb, n_valid):
    j = pl.program_id(1)
    c = GDN_CHUNK
    nh = GDN_HEADS

    @pl.when(j == 0)
    def _():
        xc_scr[0:8, :] = cinit_ref[0]
        s_scr[...] = s0_ref[0]

    x = qkv_ref[0]
    xc_scr[8:8 + tb, :] = x
    y = xc_scr[5:5 + tb, :] * cw_ref[0:1, :]
    y = y + xc_scr[6:6 + tb, :] * cw_ref[1:2, :]
    y = y + xc_scr[7:7 + tb, :] * cw_ref[2:3, :]
    y = y + x * cw_ref[3:4, :]
    xc_scr[0:8, :] = xc_scr[tb:tb + 8, :]
    y = _silu(y)

    misc = misc_ref[0]
    row = lax.broadcasted_iota(i32, (tb, MISC_W), 0) + j * tb
    live = row < n_valid
    log_a = jnp.where(live, hp_ref[0:1, :] * _softplus(misc + hp_ref[1:2, :]), 0.0)
    beta = jnp.where(live, jax.nn.sigmoid(misc), 0.0)

    r_i = lax.broadcasted_iota(i32, (tb, tb), 0)
    c_i = lax.broadcasted_iota(i32, (tb, tb), 1)
    sh = int(math.log2(c))
    ltri = (((r_i >> sh) == (c_i >> sh)) & (c_i <= r_i)).astype(bf16)
    g = _mm_01(ltri, log_a)
    g_t = g.T

    ri = lax.broadcasted_iota(i32, (c, c), 0)
    ci = lax.broadcasted_iota(i32, (c, c), 1)
    incl = ci <= ri
    strict = ci < ri
    eye = (ci == ri).astype(f32)
    quad = []
    for lvl in range(int(math.log2(c))):
        quad.append(((ri >> (lvl + 1)) == (ci >> (lvl + 1))) & (((ri >> lvl) & 1) == 1) & (((ci >> lvl) & 1) == 0))

    nchunk = tb // c
    units = [(n, h) for n in range(nchunk) for h in range(nh)]

    def stack(fn):
        return jnp.stack([fn(n * c, h) for n, h in units], axis=0)

    def bmm(eq, a_, b_):
        return jnp.einsum(eq, a_.astype(bf16), b_.astype(bf16), preferred_element_type=f32)

    q = stack(lambda r0, h: y[r0:r0 + c, h * GDN_DK:(h + 1) * GDN_DK])
    k = stack(lambda r0, h: y[r0:r0 + c, (nh + h) * GDN_DK:(nh + h + 1) * GDN_DK])
    v = stack(lambda r0, h: y[r0:r0 + c, 2 * nh * GDN_DK + h * GDN_DV:2 * nh * GDN_DK + (h + 1) * GDN_DV])
    q = q * lax.rsqrt(jnp.sum(q * q, axis=-1, keepdims=True) + EPS) * (GDN_DK ** -0.5)
    k = k * lax.rsqrt(jnp.sum(k * k, axis=-1, keepdims=True) + EPS)
    g_col = stack(lambda r0, h: g[r0:r0 + c, h:h + 1])
    g_row = stack(lambda r0, h: g_t[h:h + 1, r0:r0 + c])
    b_col = stack(lambda r0, h: beta[r0:r0 + c, nh + h:nh + h + 1])
    gam = jnp.where(incl, jnp.exp(jnp.where(incl, g_col - g_row, 0.0)), 0.0)
    kk = bmm('uid,ujd->uij', k, k)
    a = jnp.where(strict, b_col * gam * kk, 0.0)
    p = eye - jnp.where(quad[0], a, 0.0)
    for lvl in range(1, len(quad)):
        m = bmm('uij,ujk->uik', jnp.where(quad[lvl], a, 0.0), p)
        p = p - bmm('uij,ujk->uik', p, m)
    e_g = jnp.exp(g_col)
    sol = bmm('uij,ujd->uid', p, jnp.concatenate([b_col * v, (b_col * e_g) * k], axis=-1))
    vb, w = sol[:, :, :GDN_DV], sol[:, :, GDN_DV:]
    aqk = bmm('uid,ujd->uij', q, k) * gam
    qg = q * e_g
    g_last = g_col[:, c - 1:c, :]
    kd_t = jnp.swapaxes(k * jnp.exp(g_last - g_col), 1, 2)
    gc = jnp.exp(g_last)

    s = s_scr[...]
    for n in range(nchunk):
        sl = slice(n * nh, (n + 1) * nh)
        u = vb[sl] - bmm('hcd,hde->hce', w[sl], s)
        o = bmm('hcd,hde->hce', qg[sl], s) + bmm('hij,hje->hie', aqk[sl], u)
        s = gc[sl] * s + bmm('hdc,hce->hde', kd_t[sl], u)
        o = o * lax.rsqrt(jnp.mean(o * o, axis=-1, keepdims=True) + EPS) * nw_ref[...]
        for h in range(nh):
            zs = z_ref[0, n * c:(n + 1) * c, h * GDN_DV:(h + 1) * GDN_DV]
            o_ref[0, n * c:(n + 1) * c, h * GDN_DV:(h + 1) * GDN_DV] = o[h] * _silu(zs)
    s_scr[...] = s

    @pl.when(j == pl.num_programs(1) - 1)
    def _():
        sfin_ref[0] = s_scr[...]


def _gdn(qkv_raw, z, misc, conv_w, conv_buf, a_log, dt_bias, norm_w, s0, *, tb, n_valid):
    bsz, tp, ch = qkv_raw.shape
    cinit = jnp.concatenate([jnp.zeros((bsz, 5, ch), f32), conv_buf.astype(f32)], axis=1)
    hp = jnp.zeros((8, MISC_W), f32)
    hp = hp.at[0, :GDN_HEADS].set(-jnp.exp(a_log.astype(f32))).at[1, :GDN_HEADS].set(dt_bias.astype(f32))
    zw = GDN_HEADS * GDN_DV
    return pl.pallas_call(
        functools.partial(_gdn_kernel, tb=tb, n_valid=n_valid),
        grid=(bsz, tp // tb),
        in_specs=[pl.BlockSpec((1, tb, ch), lambda b, j: (b, j, 0)),
                  pl.BlockSpec((1, tb, zw), lambda b, j: (b, j, 0)),
                  pl.BlockSpec((1, tb, MISC_W), lambda b, j: (b, j, 0)),
                  pl.BlockSpec((GDN_CONV, ch), lambda b, j: (0, 0)),
                  pl.BlockSpec((1, 8, ch), lambda b, j: (b, 0, 0)),
                  pl.BlockSpec((8, MISC_W), lambda b, j: (0, 0)),
                  pl.BlockSpec((1, GDN_DV), lambda b, j: (0, 0)),
                  pl.BlockSpec((1, GDN_HEADS, GDN_DK, GDN_DV), lambda b, j: (b, 0, 0, 0))],
        out_specs=[pl.BlockSpec((1, tb, zw), lambda b, j: (b, j, 0)),
                   pl.BlockSpec((1, GDN_HEADS, GDN_DK, GDN_DV), lambda b, j: (b, 0, 0, 0))],
        out_shape=[jax.ShapeDtypeStruct((bsz, tp, zw), f32),
                   jax.ShapeDtypeStruct((bsz, GDN_HEADS, GDN_DK, GDN_DV), f32)],
        scratch_shapes=[pltpu.VMEM((tb + 8, ch), f32), pltpu.VMEM((GDN_HEADS, GDN_DK, GDN_DV), f32)],
        compiler_params=_cparams(("parallel", "arbitrary")),
        name="gdn",
    )(qkv_raw, z, misc, conv_w.astype(f32), cinit, hp, norm_w.reshape(1, GDN_DV).astype(f32), s0.astype(f32))


def _bucket_thresholds():
    exact = REL_BUCKETS // 2
    d = np.arange(0, 4 * REL_MAX_DIST, dtype=np.int64)

    def buckets(ft):
        nf = np.maximum(d, exact).astype(ft)
        large = exact + (np.log(nf / ft(exact)) / ft(math.log(REL_MAX_DIST / exact)) * ft(REL_BUCKETS - exact)).astype(np.int32)
        return np.where(d < exact, d, np.minimum(large, REL_BUCKETS - 1))

    b64, b32 = buckets(np.float64), buckets(np.float32)
    assert np.array_equal(b64, b32) and b64[-1] == REL_BUCKETS - 1 and np.all(np.diff(b64) >= 0)
    return [int(np.argmax(b64 >= k)) for k in range(1, REL_BUCKETS)]


_BUCKET_THR = _bucket_thresholds()
FAR_DIST = _BUCKET_THR[-1]


def _bias_kernel(rb_ref, d_ref, o_ref, *, slab, group_heads):
    rows = d_ref.shape[-2]
    for s in range(rows // slab):
        d = d_ref[0, 0, s * slab:(s + 1) * slab, :]
        h = pl.program_id(0) * group_heads + s
        val = jnp.full(d.shape, rb_ref[0, h], f32)
        for k in range(1, REL_BUCKETS):
            val = jnp.where(d >= _BUCKET_THR[k - 1], rb_ref[k, h], val)
        o_ref[0, 0, s * slab:(s + 1) * slab, :] = jnp.where(d < 0, NEG, val)


def _bias_table(rel_bias, dist, *, slab):
    g, nt, rows, cols = dist.shape
    return pl.pallas_call(
        functools.partial(_bias_kernel, slab=slab, group_heads=rows // slab),
        grid=(g, nt),
        in_specs=[pl.BlockSpec(memory_space=pltpu.SMEM),
                  pl.BlockSpec((1, 1, rows, cols), lambda a, b: (a, b, 0, 0))],
        out_specs=pl.BlockSpec((1, 1, rows, cols), lambda a, b: (a, b, 0, 0)),
        out_shape=jax.ShapeDtypeStruct(dist.shape, f32),
        compiler_params=_cparams(("arbitrary", "arbitrary")),
        name="bias_table",
    )(rel_bias.astype(f32), dist)


def _blockmean_kernel(x_ref, o_ref):
    x = x_ref[0]
    nb = x.shape[0] // NSA_BLOCK
    o_ref[0] = jnp.sum(x.reshape(nb, NSA_BLOCK, x.shape[1]), axis=1) * (1.0 / NSA_BLOCK)


def _block_means(kv4, *, tb):
    bsz, t, _ = kv4.shape
    w = 2 * NSA_KV_GROUPS * NSA_DH
    return pl.pallas_call(
        _blockmean_kernel,
        grid=(bsz, t // tb),
        in_specs=[pl.BlockSpec((1, tb, w), lambda b, i: (b, i, 0))],
        out_specs=pl.BlockSpec((1, tb // NSA_BLOCK, w), lambda b, i: (b, i, 0)),
        out_shape=jax.ShapeDtypeStruct((bsz, t // NSA_BLOCK, w), f32),
        compiler_params=_cparams(("parallel", "parallel")),
        name="nsa_block_means",
    )(kv4)


def _topn_mask(imp, n_sel):
    lane = lax.broadcasted_iota(i32, imp.shape, 1)
    sel = jnp.zeros(imp.shape, jnp.bool_)
    for _ in range(n_sel):
        mx = jnp.max(imp, axis=-1, keepdims=True)
        first = jnp.min(jnp.where(imp == mx, lane, imp.shape[1]), axis=-1, keepdims=True)
        hit = lane == first
        sel = sel | hit
        imp = jnp.where(hit, -jnp.inf, imp)
    return sel


def _nsa_prompt_kernel(q_ref, misc_ref, kcvc_ref, kaug_ref, v2_ref, kw_ref, vw_ref,
                       tabs_ref, tabw_ref, tabc_ref, c31_ref, o_ref):
    g = pl.program_id(1)
    i = pl.program_id(2)
    qb, dh, hpg = NSA_QBLOCK, NSA_DH, NSA_HPG
    rows = hpg * qb
    is_g0 = g == 0

    def ghalf(a):
        return jnp.where(is_g0, a[:, :dh], a[:, dh:])

    qt = q_ref[0] * (dh ** -0.5)
    qs = jnp.concatenate([qt[:, h * dh:(h + 1) * dh] for h in range(hpg)], axis=0)
    zero = jnp.zeros_like(qs)
    q2 = jnp.concatenate([jnp.where(is_g0, qs, zero), jnp.where(is_g0, zero, qs)], axis=1)

    kcvc = kcvc_ref[0]
    nblk = kcvc.shape[0]
    kc = ghalf(kcvc[:, :2 * dh])
    vc = ghalf(kcvc[:, 2 * dh:])
    s_c = _mm3_nt(qs, kc)
    lane = lax.broadcasted_iota(i32, (rows, nblk), 1)
    qrow = lax.broadcasted_iota(i32, (rows, nblk), 0) & (qb - 1)
    shift = (2 * i - 2 * NEAR_PAIRS + 1 + 4 * nblk) % nblk
    bias_c = pltpu.roll(tabc_ref[0], shift, 1)
    bias_c = jnp.where(lane < 2 * i - 2 * NEAR_PAIRS + 1, c31_ref[0], bias_c)
    readable = lane * NSA_BLOCK + (NSA_BLOCK - 1) <= i * qb + qrow
    s_c = jnp.where(readable, s_c + bias_c, NEG)
    m_c = jnp.max(s_c, axis=-1, keepdims=True)
    p_c = jnp.where(readable, jnp.exp(s_c - m_c), 0.0)
    p_c = p_c / jnp.maximum(jnp.sum(p_c, axis=-1, keepdims=True), 1e-30)
    o_c = _mm(p_c, vc)

    imp = p_c[0:qb]
    for h in range(1, hpg):
        imp = imp + p_c[h * qb:(h + 1) * qb]
    blk = lax.broadcasted_iota(i32, (qb, nblk), 1)
    cur = (i * qb + lax.broadcasted_iota(i32, (qb, nblk), 0)) >> int(math.log2(NSA_BLOCK))
    valid = blk <= cur
    forced = valid & ((blk == 0) | (blk > cur - NSA_LOCAL))
    imp = jnp.where(forced, -jnp.inf, jnp.where(valid, imp, -1.0))
    sel = (forced | _topn_mask(imp, min(NSA_TOPN, nblk) - 1 - NSA_LOCAL)) & valid
    msel = jnp.where(sel, 0.0, NEG).astype(bf16)
    qaug = jnp.concatenate([q2.astype(bf16), jnp.concatenate([msel] * hpg, axis=0)], axis=1)

    t_d = i // 2
    tk = 2 * PAIR

    def tile(k, carry):
        m, l, acc = carry
        t = t_d - k
        ks = pl.multiple_of(t * tk, tk)
        s = lax.dot_general(qaug, kaug_ref[0, pl.ds(ks, tk), :], (((1,), (1,)), ((), ())),
                            preferred_element_type=f32)
        i0 = jnp.clip(i - 2 * t, 0, NEAR_PAIRS)
        i1 = jnp.clip(i - 2 * t - 1, 0, NEAR_PAIRS)
        s = s + jnp.concatenate([tabs_ref[0, i0], tabs_ref[0, i1]], axis=1)
        m_new = jnp.maximum(m, jnp.max(s, axis=-1, keepdims=True))
        alpha = jnp.exp(m - m_new)
        p = jnp.exp(s - m_new)
        l = alpha * l + jnp.sum(p, axis=-1, keepdims=True)
        acc = alpha * acc + jnp.dot(p.astype(bf16), v2_ref[0, pl.ds(ks, tk), :], preferred_element_type=f32)
        return m_new, l, acc

    m0 = jnp.full((rows, 1), NEG, f32)
    l0 = jnp.zeros((rows, 1), f32)
    a0 = jnp.zeros((rows, 2 * dh), f32)
    _, l_s, acc_s = lax.fori_loop(0, t_d + 1, tile, (m0, l0, a0))
    o_s = ghalf(acc_s) / jnp.maximum(l_s, 1e-30)

    span = NSA_WINDOW + qb
    ws = pl.multiple_of(i * qb, qb)
    s_w = lax.dot_general(q2.astype(bf16), kw_ref[0, pl.ds(ws, span), :], (((1,), (1,)), ((), ())),
                          preferred_element_type=f32) + tabw_ref[0]
    kpos = lax.broadcasted_iota(i32, (rows, span), 1) + (i * qb - NSA_WINDOW)
    s_w = jnp.where(kpos >= 0, s_w, NEG)
    m_w = jnp.max(s_w, axis=-1, keepdims=True)
    p_w = jnp.exp(s_w - m_w)
    l_w = jnp.sum(p_w, axis=-1, keepdims=True)
    o_w = ghalf(jnp.dot(p_w.astype(bf16), vw_ref[0, pl.ds(ws, span), :], preferred_element_type=f32))
    o_w = o_w / jnp.maximum(l_w, 1e-30)

    gates = jax.nn.sigmoid(misc_ref[0])
    outs = []
    for h in range(hpg):
        r = slice(h * qb, (h + 1) * qb)
        acc = None
        for jb, branch in enumerate((o_c, o_s, o_w)):
            l0_ = 8 + 3 * h + jb
            l1_ = 8 + 3 * (hpg + h) + jb
            gcol = jnp.where(is_g0, gates[:, l0_:l0_ + 1], gates[:, l1_:l1_ + 1])
            term = gcol * branch[r]
            acc = term if acc is None else acc + term
        outs.append(acc)
    o_ref[0] = jnp.concatenate(outs, axis=1)


LOG2E = 1.4426950408889634
NSA_KEY_TILE = 4 * PAIR


def _nsa_prompt_kernel_h(q_ref, misc_ref, kcvc_ref, kaug_ref, vg_ref, kw_ref, vw_ref,
                         tabs_ref, tabw_ref, tabc_ref, c31_ref, o_ref, qaug_scr, s_scr):
    g = pl.program_id(1)
    i = pl.program_id(2)
    qb, dh, hpg = NSA_QBLOCK, NSA_DH, NSA_HPG
    is_g0 = g == 0
    nn = (((1,), (1,)), ((), ()))

    def ghalf(a):
        return jnp.where(is_g0, a[:, :dh], a[:, dh:])

    def hrows(ref, *lead):
        return [ref[lead + (slice(h * qb, (h + 1) * qb), slice(None))] for h in range(hpg)]

    qt = q_ref[0] * (dh ** -0.5)
    qh = [qt[:, h * dh:(h + 1) * dh] for h in range(hpg)]
    kcvc = kcvc_ref[0]
    nblk = kcvc.shape[0]
    kc = ghalf(kcvc[:, :2 * dh])
    vc = ghalf(kcvc[:, 2 * dh:])

    rows = hpg * qb
    qs = jnp.concatenate(qh, axis=0)
    zero = jnp.zeros_like(qs)
    q2 = jnp.concatenate([jnp.where(is_g0, qs, zero), jnp.where(is_g0, zero, qs)], axis=1)
    q2 = (q2 * LOG2E).astype(bf16)

    blk4 = lax.broadcasted_iota(i32, (rows, nblk), 1)
    qrow4 = lax.broadcasted_iota(i32, (rows, nblk), 0) & (qb - 1)
    first_tab = 2 * i - 2 * NEAR_PAIRS + 1
    shift = (first_tab + 4 * nblk) % nblk
    readable = blk4 * NSA_BLOCK + (NSA_BLOCK - 1) <= i * qb + qrow4
    bias = jnp.where(blk4 < first_tab, c31_ref[0], pltpu.roll(tabc_ref[0], shift, 1))
    s_c = jnp.where(readable, _mm3_nt(qs, kc) + bias, NEG)
    m_c = jnp.max(s_c, axis=-1, keepdims=True)
    p_c = jnp.where(readable, jnp.exp(s_c - m_c), 0.0)
    p_c = p_c / jnp.maximum(jnp.sum(p_c, axis=-1, keepdims=True), 1e-30)
    o_cmp = _mm(p_c, vc)
    imp = p_c[0:qb]
    for h in range(1, hpg):
        imp = imp + p_c[h * qb:(h + 1) * qb]

    span = NSA_WINDOW + qb
    ws = pl.multiple_of(i * qb, qb)
    s_w = lax.dot_general(q2, kw_ref[0, pl.ds(ws, span), :], nn, preferred_element_type=f32) + tabw_ref[0]
    in_seq = lax.broadcasted_iota(i32, (rows, span), 1) + (i * qb - NSA_WINDOW) >= 0
    s_w = jnp.where(in_seq, s_w, NEG)
    m_w = jnp.max(s_w, axis=-1, keepdims=True)
    p_w = jnp.exp2(s_w - m_w)
    l_w = jnp.sum(p_w, axis=-1, keepdims=True)
    o_win = ghalf(jnp.dot(p_w.astype(bf16), vw_ref[0, pl.ds(ws, span), :], preferred_element_type=f32))
    o_win = o_win / jnp.maximum(l_w, 1e-30)

    blk = lax.broadcasted_iota(i32, (qb, nblk), 1)
    qrow = lax.broadcasted_iota(i32, (qb, nblk), 0)
    cur = (i * qb + qrow) >> int(math.log2(NSA_BLOCK))
    valid = blk <= cur
    forced = valid & ((blk == 0) | (blk > cur - NSA_LOCAL))
    imp_t = jnp.where(forced, -jnp.inf, jnp.where(valid, imp, -1.0)).T
    brow = lax.broadcasted_iota(i32, (nblk, qb), 0)
    picked = jnp.zeros((nblk, qb), f32)
    for _ in range(min(NSA_TOPN, nblk) - 1 - NSA_LOCAL):
        mx = jnp.max(imp_t, axis=0, keepdims=True)
        first = jnp.min(jnp.where(imp_t == mx, brow, nblk), axis=0, keepdims=True)
        hit = brow == first
        picked = jnp.where(hit, 1.0, picked)
        imp_t = jnp.where(hit, -jnp.inf, imp_t)
    sel = (forced | (picked.T > 0.5)) & valid
    msel = jnp.where(sel, 0.0, NEG).astype(bf16)
    qaug_scr[...] = jnp.concatenate([q2, jnp.concatenate([msel] * hpg, axis=0)], axis=1)

    tk = s_scr.shape[1]
    ppt = tk // PAIR
    t_d = i // ppt

    def scores(t):
        ks = pl.multiple_of(t * tk, tk)
        s = lax.dot_general(qaug_scr[...], kaug_ref[0, pl.ds(ks, tk), :], nn, preferred_element_type=f32)
        tabs = [tabs_ref[0, jnp.clip(i - ppt * t - j, 0, NEAR_PAIRS)] for j in range(ppt)]
        return s + jnp.concatenate(tabs, axis=1)

    s_scr[...] = scores(t_d)

    def tile(k_it, carry):
        m, acc = carry
        t = t_d - k_it
        s = s_scr[...]
        s_next = scores(jnp.maximum(t - 1, 0))
        m_new = jnp.maximum(m, jnp.max(s, axis=-1, keepdims=True))
        alpha = jnp.exp2(m - m_new)
        p = jnp.exp2(s - m_new).astype(bf16)
        ks = pl.multiple_of(t * tk, tk)
        acc = alpha * acc + jnp.dot(p, vg_ref[0, 0, pl.ds(ks, tk), :], preferred_element_type=f32)
        s_scr[...] = s_next
        return m_new, acc

    _, acc_sel = lax.fori_loop(0, t_d + 1, tile, (jnp.full((rows, 1), NEG, f32), jnp.zeros((rows, 2 * dh), f32)))
    o_sel = acc_sel[:, :dh] / jnp.maximum(acc_sel[:, dh:dh + 1], 1e-30)

    gates = jax.nn.sigmoid(misc_ref[0])
    outs = []
    for h in range(hpg):
        r = slice(h * qb, (h + 1) * qb)
        acc = None
        for jb, branch in enumerate((o_cmp[r], o_sel[r], o_win[r])):
            la = 8 + 3 * h + jb
            lb = 8 + 3 * (hpg + h) + jb
            term = jnp.where(is_g0, gates[:, la:la + 1], gates[:, lb:lb + 1]) * branch
            acc = term if acc is None else acc + term
        outs.append(acc)
    o_ref[0] = jnp.concatenate(outs, axis=1)


def _nsa_tables(rel_bias):
    qb, hpg = NSA_QBLOCK, NSA_HPG
    rows = hpg * qb
    q = (np.arange(rows) % qb)[:, None]
    c = np.arange(PAIR)[None, :]
    d_sel = [PAIR * idx + q - c for idx in range(NEAR_PAIRS + 1)]
    assert PAIR * NEAR_PAIRS - (PAIR - 1) >= FAR_DIST
    mm = 2 * NEAR_PAIRS - 1 - c
    d_cmp = np.maximum(np.where(mm >= -1, NSA_BLOCK * mm + q - (NSA_BLOCK - 1), 0), 0)
    assert NSA_BLOCK * (2 * NEAR_PAIRS) - (NSA_BLOCK - 1) >= FAR_DIST
    cw = np.arange(NSA_WINDOW + qb)[None, :]
    d_win = q + NSA_WINDOW - cw
    d_win = np.where((d_win >= 0) & (d_win < NSA_WINDOW), d_win, -1)
    n_win = (NSA_WINDOW + qb) // PAIR
    tiles = d_sel + [d_cmp] + [d_win[:, k * PAIR:(k + 1) * PAIR] for k in range(n_win)]
    dist = np.broadcast_to(np.stack(tiles)[None], (NSA_KV_GROUPS, len(tiles), rows, PAIR))
    tab = _bias_table(rel_bias, jnp.asarray(dist, i32), slab=qb)
    ns = NEAR_PAIRS + 1
    tabs = tab[:, :ns]
    tabc = tab[:, ns]
    tabw = jnp.concatenate([tab[:, ns + 1 + k] for k in range(n_win)], axis=-1)
    c31 = jnp.repeat(rel_bias.astype(f32)[REL_BUCKETS - 1].reshape(NSA_KV_GROUPS, hpg), qb, axis=1)
    return tabs, tabw, tabc, c31.reshape(NSA_KV_GROUPS, rows, 1)


def _nsa_prompt(q_b, misc, kv4, kvwin, rel_bias):
    bsz, t, _ = q_b.shape
    dh, qb = NSA_DH, NSA_QBLOCK
    nblk = t // NSA_BLOCK
    assert nblk == PAIR and t % NSA_KEY_TILE == 0
    kcvc = _block_means(kv4, tb=512)
    onehot = (jnp.arange(t, dtype=i32)[:, None] // NSA_BLOCK == jnp.arange(nblk, dtype=i32)[None, :]).astype(bf16)
    kaug = jnp.concatenate([kv4[:, :, 4 * dh:6 * dh].astype(bf16),
                            jnp.broadcast_to(onehot[None], (bsz, t, nblk))], axis=-1)
    ones = jnp.ones((bsz, t, dh), bf16)
    vg = jnp.stack([jnp.concatenate([kv4[:, :, (6 + g) * dh:(7 + g) * dh].astype(bf16), ones], axis=-1)
                    for g in range(NSA_KV_GROUPS)], axis=1)
    pad = ((0, 0), (NSA_WINDOW, 0), (0, 0))
    kw = jnp.pad(kvwin[:, :, :2 * dh].astype(bf16), pad)
    vw = jnp.pad(kvwin[:, :, 2 * dh:].astype(bf16), pad)
    tabs, tabw, tabc, c31 = _nsa_tables(rel_bias)
    tabs, tabw = tabs * LOG2E, tabw * LOG2E
    rows = NSA_HPG * qb
    gw = NSA_HPG * dh
    span = NSA_WINDOW + qb
    return pl.pallas_call(
        _nsa_prompt_kernel_h,
        grid=(bsz, NSA_KV_GROUPS, t // qb),
        in_specs=[pl.BlockSpec((1, qb, gw), lambda b, g, i: (b, i, g)),
                  pl.BlockSpec((1, qb, MISC_W), lambda b, g, i: (b, i, 0)),
                  pl.BlockSpec((1, nblk, 4 * dh), lambda b, g, i: (b, 0, 0)),
                  pl.BlockSpec((1, t, 2 * dh + nblk), lambda b, g, i: (b, 0, 0)),
                  pl.BlockSpec((1, 1, t, 2 * dh), lambda b, g, i: (b, g, 0, 0)),
                  pl.BlockSpec((1, t + NSA_WINDOW, 2 * dh), lambda b, g, i: (b, 0, 0)),
                  pl.BlockSpec((1, t + NSA_WINDOW, 2 * dh), lambda b, g, i: (b, 0, 0)),
                  pl.BlockSpec((1, NEAR_PAIRS + 1, rows, PAIR), lambda b, g, i: (g, 0, 0, 0)),
                  pl.BlockSpec((1, rows, span), lambda b, g, i: (g, 0, 0)),
                  pl.BlockSpec((1, rows, PAIR), lambda b, g, i: (g, 0, 0)),
                  pl.BlockSpec((1, rows, 1), lambda b, g, i: (g, 0, 0))],
        out_specs=pl.BlockSpec((1, qb, gw), lambda b, g, i: (b, i, g)),
        out_shape=jax.ShapeDtypeStruct((bsz, t, NSA_HEADS * dh), f32),
        scratch_shapes=[pltpu.VMEM((rows, 2 * dh + nblk), bf16), pltpu.VMEM((rows, NSA_KEY_TILE), f32)],
        compiler_params=_cparams(("parallel", "parallel", "arbitrary")),
        name="nsa_prompt",
    )(q_b, misc, kcvc, kaug, vg, kw, vw, tabs, tabw, tabc, c31)


MEANS_PAGES = 8


def _nsa_s_means_kernel(pt_ref, *refs):
    del pt_ref
    x_refs, o_ref = refs[:-1], refs[-1]
    rows = [[[] for _ in range(NSA_KV_GROUPS)] for _ in range(2)]
    for x_ref in x_refs:
        page = x_ref.shape[1]
        for j in range(2):
            for g in range(NSA_KV_GROUPS):
                x = x_ref[0, :, j, g, :]
                for n in range(page // NSA_BLOCK):
                    rows[j][g].append(jnp.sum(x[n * NSA_BLOCK:(n + 1) * NSA_BLOCK], axis=0, keepdims=True)
                                      * (1.0 / NSA_BLOCK))
    for j in range(2):
        for g in range(NSA_KV_GROUPS):
            o_ref[0, j * NSA_KV_GROUPS + g] = jnp.concatenate(rows[j][g], axis=0)


def _nsa_s_means(cache_kv, page_table):
    db, n_pages = page_table.shape
    page = cache_kv.shape[1]
    nblk = n_pages * page // NSA_BLOCK
    kp = MEANS_PAGES
    rows = kp * page // NSA_BLOCK
    assert n_pages % kp == 0 and rows % 8 == 0

    def page_spec(k):
        return pl.BlockSpec((1, page, 2, NSA_KV_GROUPS, NSA_DH),
                            lambda b, p, pt: (pt[b * n_pages + p * kp + k], 0, 0, 0, 0))

    return pl.pallas_call(
        _nsa_s_means_kernel,
        grid_spec=pltpu.PrefetchScalarGridSpec(
            num_scalar_prefetch=1,
            grid=(db, n_pages // kp),
            in_specs=[page_spec(k) for k in range(kp)],
            out_specs=pl.BlockSpec((1, 2 * NSA_KV_GROUPS, rows, NSA_DH), lambda b, p, pt: (b, 0, p, 0))),
        out_shape=jax.ShapeDtypeStruct((db, 2 * NSA_KV_GROUPS, nblk, NSA_DH), f32),
        compiler_params=_cparams(("parallel", "parallel")),
        name="nsa_sample_means",
    )(page_table.reshape(-1), *([cache_kv] * kp))


def _nsa_s_scores_kernel(q_ref, kcvc_ref, win_ref, kvn_ref, tc_ref, tw_ref, b0_ref, oc_ref, ow_ref, sel_ref):
    dh, hpg = NSA_DH, NSA_HPG
    q = q_ref[0] * (dh ** -0.5)
    nblk = kcvc_ref.shape[2]
    kvn = kvn_ref[0]
    row = lax.broadcasted_iota(i32, (NSA_HEADS, 1), 0)
    lane = lax.broadcasted_iota(i32, (1, nblk), 1)
    n_sel = NSA_TOPN - 1
    for g in range(NSA_KV_GROUPS):
        in_g = (row >= g * hpg) & (row < (g + 1) * hpg)
        kc = kcvc_ref[0, g]
        vc = kcvc_ref[0, NSA_KV_GROUPS + g]
        s_c = _mm3_nt(q, kc) + tc_ref[...]
        m_c = jnp.max(s_c, axis=-1, keepdims=True)
        p_c = jnp.exp(s_c - m_c)
        p_c = p_c / jnp.maximum(jnp.sum(p_c, axis=-1, keepdims=True), 1e-30)
        o_c = _mm(p_c, vc)
        imp = jnp.sum(jnp.where(in_g, p_c, 0.0), axis=0, keepdims=True)
        forced_blocks = [0] + [nblk - k for k in range(1, NSA_LOCAL)]
        imp = jnp.where((lane == 0) | (lane > nblk - NSA_LOCAL), -jnp.inf, imp)
        picks = jnp.zeros((1, nblk), i32)
        for it, fb in enumerate(forced_blocks):
            picks = jnp.where(lane == it, fb, picks)
        for it in range(len(forced_blocks), n_sel):
            mx = jnp.max(imp, axis=-1, keepdims=True)
            first = jnp.min(jnp.where(imp == mx, lane, nblk), axis=-1, keepdims=True)
            picks = jnp.where(lane == it, first, picks)
            imp = jnp.where(lane == first, -jnp.inf, imp)
        sel_ref[0, g:g + 1, :] = picks
        kw = win_ref[0, :, 0, g, :]
        vw = win_ref[0, :, 1, g, :]
        s_w = _mm3_nt(q, kw) + tw_ref[...]
        s_n = jnp.sum(q * kvn[:, g * dh:(g + 1) * dh], axis=-1, keepdims=True) + b0_ref[...]
        m_w = jnp.maximum(jnp.max(s_w, axis=-1, keepdims=True), s_n)
        p_w = jnp.exp(s_w - m_w)
        p_n = jnp.exp(s_n - m_w)
        l_w = jnp.sum(p_w, axis=-1, keepdims=True) + p_n
        o_w = (_mm(p_w, vw) + p_n * kvn[:, (NSA_KV_GROUPS + g) * dh:(NSA_KV_GROUPS + g + 1) * dh]) / jnp.maximum(l_w, 1e-30)
        if g == 0:
            oc_ref[0] = o_c
            ow_ref[0] = o_w
        else:
            oc_ref[0] = jnp.where(in_g, o_c, oc_ref[0])
            ow_ref[0] = jnp.where(in_g, o_w, ow_ref[0])


def _nsa_s_sel_kernel(sel_ref, pt_ref, q_ref, k_ref, v_ref, tb_ref, kvn_ref, b0_ref, oc_ref, ow_ref, gt_ref,
                      o_ref, m_scr, l_scr, acc_scr):
    del sel_ref, pt_ref
    g = pl.program_id(1)
    j = pl.program_id(2)
    dh, hpg = NSA_DH, NSA_HPG
    is_g0 = g == 0
    q = q_ref[0] * (dh ** -0.5)
    kvn = kvn_ref[0]

    @pl.when(j == 0)
    def _():
        k_n = jnp.where(is_g0, kvn[:, 4 * dh:5 * dh], kvn[:, 5 * dh:6 * dh])
        v_n = jnp.where(is_g0, kvn[:, 6 * dh:7 * dh], kvn[:, 7 * dh:8 * dh])
        m_scr[...] = jnp.sum(q * k_n, axis=-1, keepdims=True) + b0_ref[...]
        l_scr[...] = jnp.ones_like(l_scr)
        acc_scr[...] = jnp.broadcast_to(v_n, acc_scr.shape)

    k = jnp.where(is_g0, k_ref[0, :, 0, 0, :], k_ref[0, :, 0, 1, :])
    v = jnp.where(is_g0, v_ref[0, :, 0, 0, :], v_ref[0, :, 0, 1, :])
    s = _mm3_nt(q, k) + tb_ref[0]
    m_old = m_scr[...]
    m_new = jnp.maximum(m_old, jnp.max(s, axis=-1, keepdims=True))
    alpha = jnp.exp(m_old - m_new)
    p = jnp.exp(s - m_new)
    l_scr[...] = alpha * l_scr[...] + jnp.sum(p, axis=-1, keepdims=True)
    acc_scr[...] = alpha * acc_scr[...] + _mm(p, v)
    m_scr[...] = m_new

    @pl.when(j == pl.num_programs(2) - 1)
    def _():
        o_s = acc_scr[...] / jnp.maximum(l_scr[...], 1e-30)
        gt = gt_ref[0]
        o = gt[:, 0:1] * oc_ref[0] + gt[:, 1:2] * o_s + gt[:, 2:3] * ow_ref[0]
        row = lax.broadcasted_iota(i32, o.shape, 0)
        in_g = (row >= g * hpg) & (row < (g + 1) * hpg)

        @pl.when(is_g0)
        def _():
            o_ref[0] = o

        @pl.when(jnp.logical_not(is_g0))
        def _():
            o_ref[0] = jnp.where(in_g, o, o_ref[0])


def _nsa_sample(q_b, gates, kv4_new, kvwin_new, cache_kv, cache_win, page_table, rel_bias):
    db = q_b.shape[0]
    dh = NSA_DH
    n_pool, page = cache_kv.shape[:2]
    n_pages = page_table.shape[1]
    past = n_pages * page
    nblk = past // NSA_BLOCK
    wb = cache_win.shape[1]
    assert nblk == PAIR and page % NSA_BLOCK == 0 and wb == NSA_WINDOW
    kcvc = _nsa_s_means(cache_kv, page_table)

    n = np.arange(nblk)
    d_cmp = past - (n * NSA_BLOCK + NSA_BLOCK - 1)
    jw = np.arange(wb)
    d_win = np.where(jw >= 1, wb - jw, -1)
    pos = np.arange(nblk * NSA_BLOCK)
    d_sel = past - pos
    width = nblk + wb + nblk * NSA_BLOCK
    dist = np.broadcast_to(np.concatenate([d_cmp, d_win, d_sel])[None, :], (NSA_HEADS, width))
    tab = _bias_table(rel_bias, jnp.asarray(dist[None, None], i32), slab=1)[0, 0]
    t_cmp, t_win = tab[:, :nblk], tab[:, nblk:nblk + wb]
    t_sel = jnp.transpose(tab[:, nblk + wb:].reshape(NSA_HEADS, nblk, NSA_BLOCK), (1, 0, 2))
    b0 = rel_bias.astype(f32)[0].reshape(NSA_HEADS, 1)

    q3 = q_b.reshape(db, NSA_HEADS, dh)
    o_c, o_w, sel = pl.pallas_call(
        _nsa_s_scores_kernel,
        grid=(db,),
        in_specs=[pl.BlockSpec((1, NSA_HEADS, dh), lambda b: (b, 0, 0)),
                  pl.BlockSpec((1, 2 * NSA_KV_GROUPS, nblk, dh), lambda b: (b, 0, 0, 0)),
                  pl.BlockSpec((1, wb, 2, NSA_KV_GROUPS, dh), lambda b: (b, 0, 0, 0, 0)),
                  pl.BlockSpec((1, 1, 4 * dh), lambda b: (b, 0, 0)),
                  pl.BlockSpec((NSA_HEADS, nblk), lambda b: (0, 0)),
                  pl.BlockSpec((NSA_HEADS, wb), lambda b: (0, 0)),
                  pl.BlockSpec((NSA_HEADS, 1), lambda b: (0, 0))],
        out_specs=[pl.BlockSpec((1, NSA_HEADS, dh), lambda b: (b, 0, 0)),
                   pl.BlockSpec((1, NSA_HEADS, dh), lambda b: (b, 0, 0)),
                   pl.BlockSpec((1, NSA_KV_GROUPS, nblk), lambda b: (b, 0, 0))],
        out_shape=[jax.ShapeDtypeStruct((db, NSA_HEADS, dh), f32),
                   jax.ShapeDtypeStruct((db, NSA_HEADS, dh), f32),
                   jax.ShapeDtypeStruct((db, NSA_KV_GROUPS, nblk), i32)],
        compiler_params=_cparams(("parallel",)),
        name="nsa_sample_scores",
    )(q3, kcvc, cache_win.astype(f32), kvwin_new.reshape(db, 1, 4 * dh), t_cmp, t_win, b0)

    n_sel = NSA_TOPN - 1
    sel_flat = sel[:, :, :n_sel].reshape(-1)
    halves = page // NSA_BLOCK
    cache_h = cache_kv.reshape(n_pool * halves, NSA_BLOCK, 4, NSA_KV_GROUPS, dh)

    def blk_of(b, g, j, sel_r, pt_r):
        nb = sel_r[(b * NSA_KV_GROUPS + g) * n_sel + j]
        return nb, pt_r[b * n_pages + nb // halves] * halves + nb % halves

    o = pl.pallas_call(
        _nsa_s_sel_kernel,
        grid_spec=pltpu.PrefetchScalarGridSpec(
            num_scalar_prefetch=2,
            grid=(db, NSA_KV_GROUPS, n_sel),
            in_specs=[pl.BlockSpec((1, NSA_HEADS, dh), lambda b, g, j, s, p: (b, 0, 0)),
                      pl.BlockSpec((1, NSA_BLOCK, 1, NSA_KV_GROUPS, dh),
                                   lambda b, g, j, s, p: (blk_of(b, g, j, s, p)[1], 0, 2, 0, 0)),
                      pl.BlockSpec((1, NSA_BLOCK, 1, NSA_KV_GROUPS, dh),
                                   lambda b, g, j, s, p: (blk_of(b, g, j, s, p)[1], 0, 3, 0, 0)),
                      pl.BlockSpec((1, NSA_HEADS, NSA_BLOCK), lambda b, g, j, s, p: (blk_of(b, g, j, s, p)[0], 0, 0)),
                      pl.BlockSpec((1, 1, 8 * dh), lambda b, g, j, s, p: (b, 0, 0)),
                      pl.BlockSpec((NSA_HEADS, 1), lambda b, g, j, s, p: (0, 0)),
                      pl.BlockSpec((1, NSA_HEADS, dh), lambda b, g, j, s, p: (b, 0, 0)),
                      pl.BlockSpec((1, NSA_HEADS, dh), lambda b, g, j, s, p: (b, 0, 0)),
                      pl.BlockSpec((1, NSA_HEADS, 3), lambda b, g, j, s, p: (b, 0, 0))],
            out_specs=pl.BlockSpec((1, NSA_HEADS, dh), lambda b, g, j, s, p: (b, 0, 0)),
            scratch_shapes=[pltpu.VMEM((NSA_HEADS, 1), f32), pltpu.VMEM((NSA_HEADS, 1), f32),
                            pltpu.VMEM((NSA_HEADS, dh), f32)]),
        out_shape=jax.ShapeDtypeStruct((db, NSA_HEADS, dh), f32),
        compiler_params=_cparams(("parallel", "arbitrary", "arbitrary")),
        name="nsa_sample_selected",
    )(sel_flat, page_table.reshape(-1), q3, cache_h, cache_h, t_sel, kv4_new.reshape(db, 1, 8 * dh), b0,
      o_c, o_w, gates)
    return o.reshape(db, NSA_HEADS * dh)


def _nsa_st_means_kernel(pt_ref, *refs):
    del pt_ref
    x_refs, o_ref = refs[:-1], refs[-1]
    p = pl.program_id(1)

    @pl.when(p == 0)
    def _():
        o_ref[...] = jnp.zeros_like(o_ref)

    page = x_refs[0].shape[-1]
    per_page = page // NSA_BLOCK
    lane = lax.broadcasted_iota(i32, (NSA_DH, page), 1)
    out_lane = lax.broadcasted_iota(i32, (NSA_DH, o_ref.shape[-1]), 1)
    for j in range(2):
        for g in range(NSA_KV_GROUPS):
            acc = o_ref[0, j * NSA_KV_GROUPS + g]
            for k, x_ref in enumerate(x_refs):
                x = x_ref[0, j, g]
                for n in range(per_page):
                    in_blk = (lane >= n * NSA_BLOCK) & (lane < (n + 1) * NSA_BLOCK)
                    mean = jnp.sum(jnp.where(in_blk, x, 0.0), axis=1, keepdims=True) * (1.0 / NSA_BLOCK)
                    col = (p * len(x_refs) + k) * per_page + n
                    acc = jnp.where(out_lane == col, mean, acc)
            o_ref[0, j * NSA_KV_GROUPS + g] = acc


def _nsa_st_means(cache_t, page_table):
    db, n_pages = page_table.shape
    page = cache_t.shape[-1]
    nblk = n_pages * page // NSA_BLOCK
    kp = MEANS_PAGES
    assert n_pages % kp == 0

    def page_spec(k):
        return pl.BlockSpec((1, 2, NSA_KV_GROUPS, NSA_DH, page),
                            lambda b, p, pt: (pt[b * n_pages + p * kp + k], 0, 0, 0, 0))

    return pl.pallas_call(
        _nsa_st_means_kernel,
        grid_spec=pltpu.PrefetchScalarGridSpec(
            num_scalar_prefetch=1,
            grid=(db, n_pages // kp),
            in_specs=[page_spec(k) for k in range(kp)],
            out_specs=pl.BlockSpec((1, 2 * NSA_KV_GROUPS, NSA_DH, nblk), lambda b, p, pt: (b, 0, 0, 0))),
        out_shape=jax.ShapeDtypeStruct((db, 2 * NSA_KV_GROUPS, NSA_DH, nblk), f32),
        compiler_params=_cparams(("parallel", "arbitrary")),
        name="nsa_sample_means",
    )(page_table.reshape(-1), *([cache_t] * kp))


def _nsa_st_scores_kernel(q_ref, kcvc_ref, win_ref, kvn_ref, tc_ref, tw_ref, b0_ref, oc_ref, ow_ref, sel_ref):
    dh, hpg = NSA_DH, NSA_HPG
    q = q_ref[0] * (dh ** -0.5)
    nblk = kcvc_ref.shape[-1]
    kvn = kvn_ref[0]
    row = lax.broadcasted_iota(i32, (NSA_HEADS, 1), 0)
    lane = lax.broadcasted_iota(i32, (1, nblk), 1)
    n_sel = NSA_TOPN - 1
    for g in range(NSA_KV_GROUPS):
        in_g = (row >= g * hpg) & (row < (g + 1) * hpg)
        s_c = _mm3(q, kcvc_ref[0, g]) + tc_ref[...]
        m_c = jnp.max(s_c, axis=-1, keepdims=True)
        p_c = jnp.exp(s_c - m_c)
        p_c = p_c / jnp.maximum(jnp.sum(p_c, axis=-1, keepdims=True), 1e-30)
        o_c = _mm_nt(p_c, kcvc_ref[0, NSA_KV_GROUPS + g])
        imp = jnp.sum(jnp.where(in_g, p_c, 0.0), axis=0, keepdims=True)
        forced_blocks = [0] + [nblk - k for k in range(1, NSA_LOCAL)]
        imp = jnp.where((lane == 0) | (lane > nblk - NSA_LOCAL), -jnp.inf, imp)
        picks = jnp.zeros((1, nblk), i32)
        for it, fb in enumerate(forced_blocks):
            picks = jnp.where(lane == it, fb, picks)
        for it in range(len(forced_blocks), n_sel):
            mx = jnp.max(imp, axis=-1, keepdims=True)
            first = jnp.min(jnp.where(imp == mx, lane, nblk), axis=-1, keepdims=True)
            picks = jnp.where(lane == it, first, picks)
            imp = jnp.where(lane == first, -jnp.inf, imp)
        sel_ref[0, g:g + 1, :] = picks
        s_w = _mm3(q, win_ref[0, 0, g]) + tw_ref[...]
        s_n = jnp.sum(q * kvn[:, g * dh:(g + 1) * dh], axis=-1, keepdims=True) + b0_ref[...]
        m_w = jnp.maximum(jnp.max(s_w, axis=-1, keepdims=True), s_n)
        p_w = jnp.exp(s_w - m_w)
        p_n = jnp.exp(s_n - m_w)
        l_w = jnp.sum(p_w, axis=-1, keepdims=True) + p_n
        v_n = kvn[:, (NSA_KV_GROUPS + g) * dh:(NSA_KV_GROUPS + g + 1) * dh]
        o_w = (_mm_nt(p_w, win_ref[0, 1, g]) + p_n * v_n) / jnp.maximum(l_w, 1e-30)
        if g == 0:
            oc_ref[0] = o_c
            ow_ref[0] = o_w
        else:
            oc_ref[0] = jnp.where(in_g, o_c, oc_ref[0])
            ow_ref[0] = jnp.where(in_g, o_w, ow_ref[0])


def _nsa_st_sel_kernel(sel_ref, pt_ref, q_ref, rb_ref, kvn_ref, oc_ref, ow_ref, gt_ref, *refs, n_sel, past):
    del pt_ref
    k_refs, v_refs, o_ref = refs[:n_sel], refs[n_sel:2 * n_sel], refs[2 * n_sel]
    b = pl.program_id(0)
    g = pl.program_id(1)
    dh, hpg = NSA_DH, NSA_HPG
    is_g0 = g == 0
    page = k_refs[0].shape[-1]
    per_page = page // NSA_BLOCK
    q = q_ref[0] * (dh ** -0.5)
    kvn = kvn_ref[0]
    lane = lax.broadcasted_iota(i32, (NSA_HEADS, page), 1)
    s_parts, d_parts = [], []
    for j in range(n_sel):
        nb = sel_ref[(b * NSA_KV_GROUPS + g) * n_sel + j]
        in_blk = (lane >> int(math.log2(NSA_BLOCK))) == (nb % per_page)
        d_parts.append(jnp.where(in_blk, past - ((nb // per_page) * page + lane), -1))
        s_parts.append(_mm3(q, k_refs[j][0, 0, 0]))
    d = jnp.concatenate(d_parts, axis=1)
    bias = jnp.broadcast_to(rb_ref[0], d.shape)
    for k in range(1, REL_BUCKETS):
        bias = jnp.where(d >= _BUCKET_THR[k - 1], rb_ref[k], bias)
    s = jnp.concatenate(s_parts, axis=1) + jnp.where(d < 0, NEG, bias)
    k_n = jnp.where(is_g0, kvn[:, 4 * dh:5 * dh], kvn[:, 5 * dh:6 * dh])
    v_n = jnp.where(is_g0, kvn[:, 6 * dh:7 * dh], kvn[:, 7 * dh:8 * dh])
    s_n = jnp.sum(q * k_n, axis=-1, keepdims=True) + rb_ref[0]
    m = jnp.maximum(jnp.max(s, axis=-1, keepdims=True), s_n)
    p = jnp.exp(s - m)
    p_n = jnp.exp(s_n - m)
    l = jnp.sum(p, axis=-1, keepdims=True) + p_n
    acc = p_n * v_n
    for j in range(n_sel):
        acc = acc + _mm_nt(p[:, j * page:(j + 1) * page], v_refs[j][0, 0, 0])
    o_s = acc / jnp.maximum(l, 1e-30)
    gt = gt_ref[0]
    o = gt[:, 0:1] * oc_ref[0] + gt[:, 1:2] * o_s + gt[:, 2:3] * ow_ref[0]
    row = lax.broadcasted_iota(i32, o.shape, 0)
    in_g = (row >= g * hpg) & (row < (g + 1) * hpg)

    @pl.when(is_g0)
    def _():
        o_ref[0] = o

    @pl.when(jnp.logical_not(is_g0))
    def _():
        o_ref[0] = jnp.where(in_g, o, o_ref[0])


def _nsa_sample_t(q_b, gates, kv4_new, kvwin_new, cache_kv, cache_win, page_table, rel_bias):
    db = q_b.shape[0]
    dh = NSA_DH
    n_pool, page = cache_kv.shape[:2]
    n_pages = page_table.shape[1]
    past = n_pages * page
    nblk = past // NSA_BLOCK
    wb = cache_win.shape[1]
    assert nblk == PAIR and page % NSA_BLOCK == 0 and wb == NSA_WINDOW and nblk > NSA_LOCAL
    cache_t = jnp.transpose(cache_kv, (0, 2, 3, 4, 1)).astype(f32)
    win_t = jnp.transpose(cache_win, (0, 2, 3, 4, 1)).astype(f32)
    kcvc = _nsa_st_means(cache_t, page_table)

    n = np.arange(nblk)
    d_cmp = past - (n * NSA_BLOCK + NSA_BLOCK - 1)
    jw = np.arange(wb)
    d_win = np.where(jw >= 1, wb - jw, -1)
    dist = np.broadcast_to(np.concatenate([d_cmp, d_win])[None, :], (NSA_HEADS, nblk + wb))
    tab = _bias_table(rel_bias, jnp.asarray(dist[None, None], i32), slab=1)[0, 0]
    t_cmp, t_win = tab[:, :nblk], tab[:, nblk:]
    rb = rel_bias.astype(f32)
    b0 = rb[0].reshape(NSA_HEADS, 1)

    q3 = q_b.reshape(db, NSA_HEADS, dh)
    o_c, o_w, sel = pl.pallas_call(
        _nsa_st_scores_kernel,
        grid=(db,),
        in_specs=[pl.BlockSpec((1, NSA_HEADS, dh), lambda b: (b, 0, 0)),
                  pl.BlockSpec((1, 2 * NSA_KV_GROUPS, dh, nblk), lambda b: (b, 0, 0, 0)),
                  pl.BlockSpec((1, 2, NSA_KV_GROUPS, dh, wb), lambda b: (b, 0, 0, 0, 0)),
                  pl.BlockSpec((1, 1, 4 * dh), lambda b: (b, 0, 0)),
                  pl.BlockSpec((NSA_HEADS, nblk), lambda b: (0, 0)),
                  pl.BlockSpec((NSA_HEADS, wb), lambda b: (0, 0)),
                  pl.BlockSpec((NSA_HEADS, 1), lambda b: (0, 0))],
        out_specs=[pl.BlockSpec((1, NSA_HEADS, dh), lambda b: (b, 0, 0)),
                   pl.BlockSpec((1, NSA_HEADS, dh), lambda b: (b, 0, 0)),
                   pl.BlockSpec((1, NSA_KV_GROUPS, nblk), lambda b: (b, 0, 0))],
        out_shape=[jax.ShapeDtypeStruct((db, NSA_HEADS, dh), f32),
                   jax.ShapeDtypeStruct((db, NSA_HEADS, dh), f32),
                   jax.ShapeDtypeStruct((db, NSA_KV_GROUPS, nblk), i32)],
        compiler_params=_cparams(("parallel",)),
        name="nsa_sample_scores",
    )(q3, kcvc, win_t, kvwin_new.reshape(db, 1, 4 * dh), t_cmp, t_win, b0)

    n_sel = NSA_TOPN - 1
    sel_flat = sel[:, :, :n_sel].reshape(-1)
    per_page = page // NSA_BLOCK

    def page_spec(j, plane):
        def imap(b, g, s, p):
            nb = s[(b * NSA_KV_GROUPS + g) * n_sel + j]
            return (p[b * n_pages + nb // per_page], plane, g, 0, 0)
        return pl.BlockSpec((1, 1, 1, dh, page), imap)

    const = lambda shape: pl.BlockSpec(shape, lambda b, g, s, p: (0,) * len(shape))
    per_b = lambda shape: pl.BlockSpec((1,) + shape, lambda b, g, s, p: (b,) + (0,) * len(shape))
    o = pl.pallas_call(
        functools.partial(_nsa_st_sel_kernel, n_sel=n_sel, past=past),
        grid_spec=pltpu.PrefetchScalarGridSpec(
            num_scalar_prefetch=2,
            grid=(db, NSA_KV_GROUPS),
            in_specs=[per_b((NSA_HEADS, dh)), const((REL_BUCKETS, NSA_HEADS, 1)), per_b((1, 8 * dh)),
                      per_b((NSA_HEADS, dh)), per_b((NSA_HEADS, dh)), per_b((NSA_HEADS, 3))]
                     + [page_spec(j, 2) for j in range(n_sel)] + [page_spec(j, 3) for j in range(n_sel)],
            out_specs=per_b((NSA_HEADS, dh))),
        out_shape=jax.ShapeDtypeStruct((db, NSA_HEADS, dh), f32),
        compiler_params=_cparams(("parallel", "arbitrary")),
        name="nsa_sample_selected",
    )(sel_flat, page_table.reshape(-1), q3, rb.reshape(REL_BUCKETS, NSA_HEADS, 1), kv4_new.reshape(db, 1, 8 * dh),
      o_c, o_w, gates, *([cache_t] * (2 * n_sel)))
    return o.reshape(db, NSA_HEADS * dh)


def _gelu_tanh(x):
    return 0.5 * x * (1.0 + jnp.tanh(math.sqrt(2.0 / math.pi) * (x + 0.044715 * (x * x * x))))


def _lru_gates(xc, wa_ref, wx_ref, ba_ref, bx_ref, lam_ref, prec):
    r_parts, i_parts = [], []
    for n in range(RNN_BLOCKS):
        xb = xc[:, n * RNN_BW:(n + 1) * RNN_BW]
        r_parts.append(_dotp(xb, wa_ref[n], prec))
        i_parts.append(_dotp(xb, wx_ref[n], prec))
    r = jax.nn.sigmoid(jnp.concatenate(r_parts, axis=1) + ba_ref[...])
    i = jax.nn.sigmoid(jnp.concatenate(i_parts, axis=1) + bx_ref[...])
    log_a = -RG_C * r * _softplus(-lam_ref[...])
    a = jnp.exp(log_a)
    t = jnp.tanh(log_a)
    b = jnp.sqrt(jnp.maximum(-2.0 * t / (1.0 - t), 0.0)) * (i * xc)
    return a, b


def _lru_kernel(rec_ref, gate_ref, cw_ref, cb_ref, wa_ref, wx_ref, ba_ref, bx_ref, lam_ref, cinit_ref, h0_ref,
                y_ref, hfin_ref, xc_scr, a_scr, b_scr, hs_scr, h_scr, *, tb):
    j = pl.program_id(1)

    @pl.when(j == 0)
    def _():
        xc_scr[0:8, :] = cinit_ref[0]
        h_scr[...] = h0_ref[0]

    x = rec_ref[0]
    xc_scr[8:8 + tb, :] = x
    xc = xc_scr[5:5 + tb, :] * cw_ref[0:1, :]
    xc = xc + xc_scr[6:6 + tb, :] * cw_ref[1:2, :]
    xc = xc + xc_scr[7:7 + tb, :] * cw_ref[2:3, :]
    xc = xc + x * cw_ref[3:4, :]
    xc = xc + cb_ref[...]
    xc_scr[0:8, :] = xc_scr[tb:tb + 8, :]
    a, b = _lru_gates(xc, wa_ref, wx_ref, ba_ref, bx_ref, lam_ref, 1)
    a_scr[...] = a
    b_scr[...] = b

    def step(t, h):
        h = a_scr[pl.ds(t, 1), :] * h + b_scr[pl.ds(t, 1), :]
        hs_scr[pl.ds(t, 1), :] = h
        return h

    h = lax.fori_loop(0, tb, step, h_scr[...], unroll=8)
    h_scr[...] = h
    y_ref[0] = _gelu_tanh(gate_ref[0]) * hs_scr[...]

    @pl.when(j == pl.num_programs(1) - 1)
    def _():
        hfin_ref[0] = h


def _lru_prompt(rec, gate, conv_w, conv_b, wa, wx, ba, bx, lam, *, tb):
    bsz, t, w = rec.shape
    row = lambda a: a.reshape(1, w).astype(f32)
    cinit = jnp.zeros((bsz, 8, w), f32)
    h0 = jnp.zeros((bsz, 1, w), f32)
    full = lambda shape: pl.BlockSpec(shape, lambda b, j: (0,) * len(shape))
    return pl.pallas_call(
        functools.partial(_lru_kernel, tb=tb),
        grid=(bsz, t // tb),
        in_specs=[pl.BlockSpec((1, tb, w), lambda b, j: (b, j, 0)),
                  pl.BlockSpec((1, tb, w), lambda b, j: (b, j, 0)),
                  full((RNN_CONV, w)), full((1, w)),
                  full((RNN_BLOCKS, RNN_BW, RNN_BW)), full((RNN_BLOCKS, RNN_BW, RNN_BW)),
                  full((1, w)), full((1, w)), full((1, w)),
                  pl.BlockSpec((1, 8, w), lambda b, j: (b, 0, 0)),
                  pl.BlockSpec((1, 1, w), lambda b, j: (b, 0, 0))],
        out_specs=[pl.BlockSpec((1, tb, w), lambda b, j: (b, j, 0)),
                   pl.BlockSpec((1, 1, w), lambda b, j: (b, 0, 0))],
        out_shape=[jax.ShapeDtypeStruct((bsz, t, w), f32), jax.ShapeDtypeStruct((bsz, 1, w), f32)],
        scratch_shapes=[pltpu.VMEM((tb + 8, w), f32), pltpu.VMEM((tb, w), f32), pltpu.VMEM((tb, w), f32),
                        pltpu.VMEM((tb, w), f32), pltpu.VMEM((1, w), f32)],
        compiler_params=_cparams(("parallel", "arbitrary")),
        name="rglru",
    )(rec, gate, conv_w.astype(f32), row(conv_b), wa.astype(f32), wx.astype(f32), row(ba), row(bx), row(lam),
      cinit, h0)


def _lru_step_kernel(rec_ref, gate_ref, b0_ref, b1_ref, b2_ref, cw_ref, cb_ref, wa_ref, wx_ref, ba_ref, bx_ref,
                     lam_ref, h0_ref, y_ref, h_ref):
    xc = b0_ref[...] * cw_ref[0:1, :]
    xc = xc + b1_ref[...] * cw_ref[1:2, :]
    xc = xc + b2_ref[...] * cw_ref[2:3, :]
    xc = xc + rec_ref[...] * cw_ref[3:4, :]
    xc = xc + cb_ref[...]
    a, b = _lru_gates(xc, wa_ref, wx_ref, ba_ref, bx_ref, lam_ref, 3)
    h = a * h0_ref[...] + b
    h_ref[...] = h
    y_ref[...] = _gelu_tanh(gate_ref[...]) * h


def _lru_sample(rec, gate, conv_buf, h0, conv_w, conv_b, wa, wx, ba, bx, lam):
    db, w = rec.shape
    row = lambda a: a.reshape(1, w).astype(f32)
    buf = conv_buf.astype(f32)
    return pl.pallas_call(
        _lru_step_kernel,
        out_shape=[jax.ShapeDtypeStruct((db, w), f32), jax.ShapeDtypeStruct((db, w), f32)],
        compiler_params=pltpu.CompilerParams(vmem_limit_bytes=VMEM_LIMIT),
        name="rglru_step",
    )(rec, gate, buf[:, 0], buf[:, 1], buf[:, 2], conv_w.astype(f32), row(conv_b), wa.astype(f32), wx.astype(f32),
      row(ba), row(bx), row(lam), h0.astype(f32))


def _top2_kernel(l_ref, e_ref, g_ref):
    lg = l_ref[...]
    lane = lax.broadcasted_iota(i32, lg.shape, 1)
    lg = jnp.where(lane < N_EXPERTS, lg, -jnp.inf)
    m1 = jnp.max(lg, axis=-1, keepdims=True)
    i1 = jnp.min(jnp.where(lg == m1, lane, lg.shape[1]), axis=-1, keepdims=True)
    lg2 = jnp.where(lane == i1, -jnp.inf, lg)
    m2 = jnp.max(lg2, axis=-1, keepdims=True)
    i2 = jnp.min(jnp.where(lg2 == m2, lane, lg.shape[1]), axis=-1, keepdims=True)
    e2 = jnp.exp(m2 - m1)
    den = 1.0 + e2
    e_ref[...] = jnp.where(lane == 0, i1, jnp.where(lane == 1, i2, 0))
    g_ref[...] = jnp.where(lane == 0, 1.0 / den, jnp.where(lane == 1, e2 / den, 0.0))


def _top2(logits, *, tm):
    n, w = logits.shape
    return pl.pallas_call(
        _top2_kernel,
        grid=(n // tm,),
        in_specs=[pl.BlockSpec((tm, w), lambda i: (i, 0))],
        out_specs=[pl.BlockSpec((tm, w), lambda i: (i, 0)), pl.BlockSpec((tm, w), lambda i: (i, 0))],
        out_shape=[jax.ShapeDtypeStruct((n, w), i32), jax.ShapeDtypeStruct((n, w), f32)],
        compiler_params=_cparams(("parallel",)),
        name="moe_top2",
    )(logits)


def _moe_kernel(be_ref, nu_ref, xs_ref, w1_ref, w3_ref, w2_ref, o_ref, acc_scr):
    del be_ref
    i = pl.program_id(0)
    f = pl.program_id(1)
    last = pl.num_programs(1) - 1
    used = i < nu_ref[0]

    @pl.when(used)
    def _():
        @pl.when(f == 0)
        def _():
            acc_scr[...] = jnp.zeros_like(acc_scr)

        x = xs_ref[...]
        act = _silu(_mm(x, w1_ref[0])) * _mm(x, w3_ref[0])
        acc_scr[...] += _mm(act, w2_ref[0])

        @pl.when(f == last)
        def _():
            o_ref[...] = acc_scr[...]

    @pl.when(jnp.logical_not(used) & (f == last))
    def _():
        o_ref[...] = jnp.zeros_like(o_ref)


def _moe_experts(xs, blk_e, n_used, w1, w3, w2, *, tm, tf):
    rows, d = xs.shape
    fdim = w1.shape[2]
    nf = fdim // tf

    def wcol(i, f, be, nu):
        return (be[i], 0, jnp.where(i < nu[0], f, nf - 1))

    def wrow(i, f, be, nu):
        return (be[i], jnp.where(i < nu[0], f, nf - 1), 0)

    return pl.pallas_call(
        _moe_kernel,
        grid_spec=pltpu.PrefetchScalarGridSpec(
            num_scalar_prefetch=2,
            grid=(rows // tm, nf),
            in_specs=[pl.BlockSpec((tm, d), lambda i, f, be, nu: (i, 0)),
                      pl.BlockSpec((1, d, tf), wcol),
                      pl.BlockSpec((1, d, tf), wcol),
                      pl.BlockSpec((1, tf, d), wrow)],
            out_specs=pl.BlockSpec((tm, d), lambda i, f, be, nu: (i, 0)),
            scratch_shapes=[pltpu.VMEM((tm, d), f32)]),
        out_shape=jax.ShapeDtypeStruct((rows, d), f32),
        compiler_params=_cparams(("arbitrary", "arbitrary")),
        name="moe_experts",
    )(blk_e, n_used, xs, w1, w3, w2)


def _combine_kernel(x_ref, gate_ref, y0_ref, y1_ref, g_ref, nw_ref, o_ref):
    gw = g_ref[...]
    y = y0_ref[...] * gw[:, 0:1] + y1_ref[...] * gw[:, 1:2]
    x = x_ref[0] + gate_ref[0] * y
    o_ref[0] = x * lax.rsqrt(jnp.mean(x * x, axis=-1, keepdims=True) + EPS) * nw_ref[...]


def _moe_combine(x, gate, y0, y1, gw, norm_w, *, tm, row0):
    bsz, t, d = x.shape
    assert row0 % tm == 0
    tok = pl.BlockSpec((1, tm, d), lambda b, i: (b, i, 0))
    flat = lambda w: pl.BlockSpec((tm, w), lambda b, i: (row0 // tm + b * (t // tm) + i, 0))
    return pl.pallas_call(
        _combine_kernel,
        grid=(bsz, t // tm),
        in_specs=[tok, _mod_spec(gate, tm), flat(d), flat(d), flat(gw.shape[-1]),
                  pl.BlockSpec((1, d), lambda b, i: (0, 0))],
        out_specs=tok,
        out_shape=jax.ShapeDtypeStruct((bsz, t, d), f32),
        compiler_params=_cparams(("parallel", "parallel")),
        name="moe_combine",
    )(x, gate, y0, y1, gw, norm_w.reshape(1, d).astype(f32))


MOE_TM = 1024
MOE_TF = 512


def _moe_dispatch(e_idx, n_tok):
    tm = MOE_TM
    n_assign = n_tok * TOP_K
    e_flat = e_idx.reshape(n_assign)
    onehot = (e_flat[:, None] == jnp.arange(N_EXPERTS, dtype=i32)[None, :]).astype(i32)
    rank = jnp.take_along_axis(jnp.cumsum(onehot, axis=0), e_flat[:, None], axis=1)[:, 0] - 1
    counts = jnp.sum(onehot, axis=0)
    padded = (counts + tm - 1) // tm * tm
    pend = jnp.cumsum(padded)
    pstart = pend - padded
    dest = pstart[e_flat] + rank
    n_blocks = -(-n_assign // tm) + N_EXPERTS
    rows = n_blocks * tm
    tok_flat = jnp.repeat(jnp.arange(n_tok, dtype=i32), TOP_K)
    row_tok = (jnp.arange(rows, dtype=i32) % n_tok).at[dest].set(tok_flat)
    n_used = (pend[-1] // tm).astype(i32)
    blk = jnp.minimum(jnp.arange(n_blocks, dtype=i32), n_used - 1) * tm
    blk_e = jnp.minimum(jnp.searchsorted(pend, blk, side='right'), N_EXPERTS - 1).astype(i32)
    return dest.reshape(n_tok, TOP_K), row_tok, blk_e, n_used.reshape(1)


def _w_in0_layout(w_in0):
    c_qkv = GDN_CONV_CH
    c_z = c_qkv + GDN_HEADS * GDN_DV
    c_ab = c_z + 2 * GDN_HEADS
    c_q = c_ab + NSA_HEADS * NSA_DH
    c_kv = c_q + 6 * NSA_KV_GROUPS * NSA_DH
    c_g = c_kv + 3 * NSA_HEADS
    assert c_g == w_in0.shape[1]
    n_misc = 2 * GDN_HEADS + 3 * NSA_HEADS
    w = jnp.concatenate([w_in0[:, :c_z], w_in0[:, c_ab:c_kv], w_in0[:, c_z:c_ab], w_in0[:, c_kv:c_g],
                         jnp.zeros((w_in0.shape[0], MISC_W - n_misc), w_in0.dtype)], axis=1)
    widths = (GDN_CONV_CH, GDN_HEADS * GDN_DV, NSA_HEADS * NSA_DH, 4 * NSA_KV_GROUPS * NSA_DH,
              2 * NSA_KV_GROUPS * NSA_DH, MISC_W)
    splits, s = [], 0
    for wd in widths:
        splits.append((s, s + wd))
        s += wd
    return w, tuple(splits)


def kernel(x_prompt, x_sample, c_prompt, c_sample, cache_nsa_kv, cache_nsa_win, state_gdn, state_gdn_conv, state_lru, state_lru_conv, page_table, rel_bias, w_ada, b_ada, norm_mix, norm_ffn, norm_final, w_in0, gdn_conv_w, gdn_a_log, gdn_dt_bias, gdn_norm_w, w_out0, ffn_w_gate, ffn_w_up, ffn_w_down, w_in1, lru_conv_w, lru_conv_b, lru_wa, lru_ba, lru_wx, lru_bx, lru_lambda, w_out1, moe_router, moe_w1, moe_w3, moe_w2):
    bsz, seq, d = x_prompt.shape
    db = x_sample.shape[0]
    assert x_sample.shape[1] == 1
    dh = NSA_DH

    n_c = bsz + db
    n_c_pad = -(-n_c // 8) * 8
    c_all = jnp.concatenate([c_prompt, c_sample, jnp.zeros((n_c_pad - n_c, d), f32)], axis=0)
    mods = _adaln(c_all, w_ada, b_ada).reshape(2, n_c_pad, N_MOD, d)
    mod_p = [[mods[l, :bsz, k].reshape(bsz, 1, d) for k in range(N_MOD)] for l in range(2)]
    mod_s = [[mods[l, bsz:n_c, k].reshape(1, db, d) for k in range(N_MOD)] for l in range(2)]

    w0, splits0 = _w_in0_layout(w_in0)
    splits1 = ((0, RNN_WIDTH), (RNN_WIDTH, 2 * RNN_WIDTH))
    router = jnp.concatenate([moe_router, jnp.zeros((d, MISC_W - N_EXPERTS), f32)], axis=1)
    bf = lambda w: w.astype(bf16)

    tm = 512
    xp = x_prompt
    qkv, z, q_b, kv4, kvwin, misc = _mod_matmul(xp, norm_mix[0], mod_p[0][0], mod_p[0][1], bf(w0), splits0,
                                                tm=tm, prec=1)
    o_a, p_gdn = _gdn(qkv, z, misc, gdn_conv_w, jnp.zeros((bsz, GDN_CONV - 1, GDN_CONV_CH), f32), gdn_a_log,
                      gdn_dt_bias, gdn_norm_w, jnp.zeros((bsz, GDN_HEADS, GDN_DK, GDN_DV), f32), tb=256, n_valid=seq)
    p_gdn_conv = qkv[:, seq - (GDN_CONV - 1):]
    o_b = _nsa_prompt(q_b, misc, kv4, kvwin, rel_bias)
    p_nsa_kv = kv4.reshape(bsz, seq, 4, NSA_KV_GROUPS, dh)
    keep = min(NSA_WINDOW, seq)
    p_nsa_win = kvwin[:, seq - keep:].reshape(bsz, keep, 2, NSA_KV_GROUPS, dh)
    xp = _proj_residual([o_a, o_b], bf(w_out0), xp, mod_p[0][2], tm=tm, prec=1)
    xp = _ffn(xp, norm_ffn[0], mod_p[0][3], mod_p[0][4], mod_p[0][5], bf(ffn_w_gate), bf(ffn_w_up), bf(ffn_w_down),
              tm=tm, tf=FFN_DIM // 2, prec=1)
    gate_br, rec_br = _mod_matmul(xp, norm_mix[1], mod_p[1][0], mod_p[1][1], bf(w_in1), splits1, tm=tm, prec=1)
    y_in, p_lru = _lru_prompt(rec_br, gate_br, lru_conv_w, lru_conv_b, lru_wa, lru_wx, lru_ba, lru_bx, lru_lambda,
                              tb=256)
    p_lru_conv = rec_br[:, seq - (RNN_CONV - 1):]
    xp = _proj_residual([y_in], bf(w_out1), xp, mod_p[1][2], tm=tm, prec=1)
    logit_p, h_p = _mod_matmul(xp, norm_ffn[1], mod_p[1][3], mod_p[1][4], router, ((0, MISC_W),), tm=tm, prec=3,
                               emit_h=f32)

    xs = x_sample.reshape(1, db, d)
    qkv_s, z_s, q_s, kv4_s, kvwin_s, misc_s = _mod_matmul(xs, norm_mix[0], mod_s[0][0], mod_s[0][1], w0, splits0,
                                                          tm=db, prec=3)
    c = GDN_CHUNK
    tpad = lambda a: jnp.pad(a.reshape(db, 1, a.shape[-1]), ((0, 0), (0, c - 1), (0, 0)))
    o_a_s, s_gdn = _gdn(tpad(qkv_s), tpad(z_s), tpad(misc_s), gdn_conv_w, state_gdn_conv, gdn_a_log, gdn_dt_bias,
                        gdn_norm_w, state_gdn, tb=c, n_valid=1)
    o_a_s = o_a_s[:, 0].reshape(1, db, GDN_HEADS * GDN_DV)
    s_gdn_conv = jnp.concatenate([state_gdn_conv[:, 1:], qkv_s.reshape(db, 1, GDN_CONV_CH)], axis=1)
    gates_s = jax.nn.sigmoid(misc_s[0, :, 2 * GDN_HEADS:2 * GDN_HEADS + 3 * NSA_HEADS]).reshape(db, NSA_HEADS, 3)
    o_b_s = _nsa_sample_t(q_s[0], gates_s, kv4_s[0], kvwin_s[0], cache_nsa_kv, cache_nsa_win, page_table, rel_bias)
    s_nsa_kv = kv4_s.reshape(db, 1, 4, NSA_KV_GROUPS, dh)
    s_nsa_win = jnp.concatenate([cache_nsa_win[:, 1:],
                                 kvwin_s.reshape(db, 1, 2, NSA_KV_GROUPS, dh).astype(cache_nsa_win.dtype)], axis=1)
    xs = _proj_residual([o_a_s, o_b_s.reshape(1, db, NSA_HEADS * dh)], w_out0, xs, mod_s[0][2], tm=db, prec=3)
    xs = _ffn(xs, norm_ffn[0], mod_s[0][3], mod_s[0][4], mod_s[0][5], ffn_w_gate, ffn_w_up, ffn_w_down,
              tm=db, tf=256, prec=3)
    gate_s, rec_s = _mod_matmul(xs, norm_mix[1], mod_s[1][0], mod_s[1][1], w_in1, splits1, tm=db, prec=3)
    y_in_s, s_lru = _lru_sample(rec_s[0], gate_s[0], state_lru_conv, state_lru, lru_conv_w, lru_conv_b, lru_wa, lru_wx,
                                lru_ba, lru_bx, lru_lambda)
    s_lru_conv = jnp.concatenate([state_lru_conv[:, 1:], rec_s.reshape(db, 1, RNN_WIDTH)], axis=1)
    xs = _proj_residual([y_in_s.reshape(1, db, RNN_WIDTH)], w_out1, xs, mod_s[1][2], tm=db, prec=3)
    logit_s, h_s = _mod_matmul(xs, norm_ffn[1], mod_s[1][3], mod_s[1][4], router, ((0, MISC_W),), tm=db, prec=3,
                               emit_h=f32)

    n_p = bsz * seq
    n_tok = n_p + db
    logits = jnp.concatenate([logit_p.reshape(n_p, MISC_W), logit_s.reshape(db, MISC_W)], axis=0)
    h_all = jnp.concatenate([h_p.reshape(n_p, d), h_s.reshape(db, d)], axis=0)
    e_idx, gw = _top2(logits, tm=max(t for t in range(8, 1025, 8) if n_tok % t == 0))
    dest, row_tok, blk_e, n_used = _moe_dispatch(e_idx[:, :TOP_K], n_tok)
    yb = _moe_experts(h_all[row_tok], blk_e, n_used, moe_w1, moe_w3, moe_w2, tm=MOE_TM, tf=MOE_TF)
    y0, y1 = yb[dest[:, 0]], yb[dest[:, 1]]
    y_prompt = _moe_combine(xp, mod_p[1][5], y0, y1, gw, norm_final, tm=tm, row0=0)
    y_sample = _moe_combine(xs, mod_s[1][5], y0, y1, gw, norm_final, tm=db, row0=n_p)

    return (y_prompt, y_sample.reshape(db, 1, d),
            p_nsa_kv, p_nsa_win, p_gdn, p_gdn_conv, p_lru.reshape(bsz, RNN_WIDTH), p_lru_conv,
            s_nsa_kv, s_nsa_win, s_gdn, s_gdn_conv, s_lru, s_lru_conv)
```

```python
import functools
import math

import numpy as np
import jax
import jax.numpy as jnp
from jax import lax
from jax.experimental import pallas as pl
from jax.experimental.pallas import tpu as pltpu

f32 = jnp.float32
bf16 = jnp.bfloat16
i32 = jnp.int32

D_MODEL = 1024
EPS = 1e-6
N_MOD = 6
GDN_HEADS = 4
GDN_DK = 128
GDN_DV = 128
GDN_CONV = 4
GDN_CHUNK = 64
GDN_CONV_CH = GDN_HEADS * (2 * GDN_DK + GDN_DV)
NSA_HEADS = 8
NSA_KV_GROUPS = 2
NSA_HPG = NSA_HEADS // NSA_KV_GROUPS
NSA_DH = 64
NSA_BLOCK = 64
NSA_TOPN = 16
NSA_LOCAL = 2
NSA_WINDOW = 512
NSA_QBLOCK = 128
NSA_FORCE = 1.0e4
REL_BUCKETS = 32
REL_MAX_DIST = 2048
RNN_WIDTH = D_MODEL
RNN_BLOCKS = 8
RNN_BW = RNN_WIDTH // RNN_BLOCKS
RNN_CONV = 4
RG_C = 8.0
FFN_DIM = 2816
N_EXPERTS = 8
TOP_K = 2
EXPERT_DIM = 3584

NEG = -1e30
PAIR = 2 * NSA_BLOCK
NEAR_PAIRS = 13
MISC_W = 128
VMEM_LIMIT = 56 * 1024 * 1024


def _cparams(sem, vmem=VMEM_LIMIT):
    return pltpu.CompilerParams(dimension_semantics=sem, vmem_limit_bytes=vmem)


def _mm(a, b):
    return jnp.dot(a.astype(bf16), b.astype(bf16), preferred_element_type=f32)


def _mm_nt(a, b):
    return lax.dot_general(a.astype(bf16), b.astype(bf16), (((1,), (1,)), ((), ())),
                           preferred_element_type=f32)


def _split2(a):
    hi = a.astype(bf16)
    lo = (a - hi.astype(f32)).astype(bf16)
    return hi, lo


def _mm3(a, b):
    ah, al = _split2(a)
    bh, bl = _split2(b)
    return (jnp.dot(ah, bh, preferred_element_type=f32) + jnp.dot(ah, bl, preferred_element_type=f32)
            + jnp.dot(al, bh, preferred_element_type=f32))


def _mm3_nt(a, b):
    ah, al = _split2(a)
    bh, bl = _split2(b)
    dn = (((1,), (1,)), ((), ()))
    return (lax.dot_general(ah, bh, dn, preferred_element_type=f32)
            + lax.dot_general(ah, bl, dn, preferred_element_type=f32)
            + lax.dot_general(al, bh, dn, preferred_element_type=f32))


def _mm_01(m01, a):
    hi = a.astype(bf16)
    r1 = a - hi.astype(f32)
    mid = r1.astype(bf16)
    lo = (r1 - mid.astype(f32)).astype(bf16)
    return (jnp.dot(m01, hi, preferred_element_type=f32) + jnp.dot(m01, mid, preferred_element_type=f32)
            + jnp.dot(m01, lo, preferred_element_type=f32))


def _dotp(a, b, prec):
    return _mm3(a, b) if prec == 3 else _mm(a, b)


def _silu(x):
    return x * jax.nn.sigmoid(x)


def _softplus(x):
    return jnp.maximum(x, 0.0) + jnp.log1p(jnp.exp(-jnp.abs(x)))


def _modulate(x, gain, shift, scale):
    r = lax.rsqrt(jnp.mean(x * x, axis=-1, keepdims=True) + EPS)
    return x * r * gain * (1.0 + scale) + shift


def _ada_kernel(c_ref, w_ref, b_ref, o_ref):
    o_ref[0] = _mm3(_silu(c_ref[...]), w_ref[0]) + b_ref[0]


def _adaln(c_all, w_ada, b_ada):
    rows = c_all.shape[0]
    depth, d, n = w_ada.shape
    tn = 1536
    return pl.pallas_call(
        _ada_kernel,
        grid=(depth, n // tn),
        in_specs=[pl.BlockSpec((rows, d), lambda l, j: (0, 0)),
                  pl.BlockSpec((1, d, tn), lambda l, j: (l, 0, j)),
                  pl.BlockSpec((1, 1, tn), lambda l, j: (l, 0, j))],
        out_specs=pl.BlockSpec((1, rows, tn), lambda l, j: (l, 0, j)),
        out_shape=jax.ShapeDtypeStruct((depth, rows, n), f32),
        compiler_params=_cparams(("arbitrary", "arbitrary")),
        name="adaln",
    )(c_all, w_ada, b_ada.reshape(depth, 1, n))


def _mod_spec(mod, tm):
    r = mod.shape[1]
    if r == 1:
        return pl.BlockSpec((1, 1, mod.shape[2]), lambda b, i: (b, 0, 0))
    return pl.BlockSpec((1, tm, mod.shape[2]), lambda b, i: (b, i, 0))


def _modmm_kernel(x_ref, gain_ref, shift_ref, scale_ref, w_ref, *o_refs, splits, prec, emit_h):
    h = _modulate(x_ref[0], gain_ref[...], shift_ref[0], scale_ref[0])
    if emit_h:
        o_refs[-1][0] = h.astype(o_refs[-1].dtype)
    hh = _split2(h) if prec == 3 else h.astype(bf16)
    for o_ref, (a, b) in zip(o_refs, splits):
        w = w_ref[:, a:b]
        if prec == 3:
            wh, wl = _split2(w)
            acc = (jnp.dot(hh[0], wh, preferred_element_type=f32) + jnp.dot(hh[0], wl, preferred_element_type=f32)
                   + jnp.dot(hh[1], wh, preferred_element_type=f32))
        else:
            acc = jnp.dot(hh, w, preferred_element_type=f32)
        o_ref[0] = acc


def _mod_matmul(x, gain, shift, scale, w, splits, *, tm, prec, emit_h=None):
    bsz, t, d = x.shape
    out_shape = [jax.ShapeDtypeStruct((bsz, t, b - a), f32) for a, b in splits]
    out_specs = [pl.BlockSpec((1, tm, b - a), lambda bi, i: (bi, i, 0)) for a, b in splits]
    if emit_h is not None:
        out_shape.append(jax.ShapeDtypeStruct((bsz, t, d), emit_h))
        out_specs.append(pl.BlockSpec((1, tm, d), lambda bi, i: (bi, i, 0)))
    return pl.pallas_call(
        functools.partial(_modmm_kernel, splits=tuple(splits), prec=prec, emit_h=emit_h is not None),
        grid=(bsz, t // tm),
        in_specs=[pl.BlockSpec((1, tm, d), lambda bi, i: (bi, i, 0)),
                  pl.BlockSpec((1, d), lambda bi, i: (0, 0)),
                  _mod_spec(shift, tm), _mod_spec(scale, tm),
                  pl.BlockSpec(w.shape, lambda bi, i: (0, 0))],
        out_specs=out_specs,
        out_shape=out_shape,
        compiler_params=_cparams(("parallel", "parallel")),
        name="mod_matmul",
    )(x, gain.reshape(1, d), shift, scale, w)


def _projres_kernel(*refs, n_lhs, ksplits, prec, final_norm):
    lhs = refs[:n_lhs]
    w_ref, x_ref, gate_ref = refs[n_lhs:n_lhs + 3]
    o_ref = refs[-1]
    acc = None
    for l_ref, (a, b) in zip(lhs, ksplits):
        part = _dotp(l_ref[0], w_ref[a:b, :], prec)
        acc = part if acc is None else acc + part
    y = x_ref[0] + gate_ref[0] * acc
    if final_norm:
        nw_ref = refs[n_lhs + 3]
        y = y * lax.rsqrt(jnp.mean(y * y, axis=-1, keepdims=True) + EPS) * nw_ref[...]
    o_ref[0] = y


def _proj_residual(lhs_list, w, x, gate, *, tm, prec, norm_w=None):
    bsz, t, d = x.shape
    ksplits, k0 = [], 0
    for l in lhs_list:
        ksplits.append((k0, k0 + l.shape[-1]))
        k0 += l.shape[-1]
    in_specs = [pl.BlockSpec((1, tm, l.shape[-1]), lambda bi, i: (bi, i, 0)) for l in lhs_list]
    in_specs += [pl.BlockSpec(w.shape, lambda bi, i: (0, 0)),
                 pl.BlockSpec((1, tm, d), lambda bi, i: (bi, i, 0)),
                 _mod_spec(gate, tm)]
    args = list(lhs_list) + [w, x, gate]
    if norm_w is not None:
        in_specs.append(pl.BlockSpec((1, d), lambda bi, i: (0, 0)))
        args.append(norm_w.reshape(1, d))
    return pl.pallas_call(
        functools.partial(_projres_kernel, n_lhs=len(lhs_list), ksplits=tuple(ksplits), prec=prec,
                          final_norm=norm_w is not None),
        grid=(bsz, t // tm),
        in_specs=in_specs,
        out_specs=pl.BlockSpec((1, tm, d), lambda bi, i: (bi, i, 0)),
        out_shape=jax.ShapeDtypeStruct((bsz, t, d), f32),
        compiler_params=_cparams(("parallel", "parallel")),
        name="proj_residual",
    )(*args)


def _ffn_kernel(x_ref, gain_ref, shift_ref, scale_ref, gate_ref, wg_ref, wu_ref, wd_ref, o_ref,
                h_scr, acc_scr, *, prec):
    f = pl.program_id(2)

    @pl.when(f == 0)
    def _():
        h_scr[...] = _modulate(x_ref[0], gain_ref[...], shift_ref[0], scale_ref[0]).astype(h_scr.dtype)
        acc_scr[...] = jnp.zeros_like(acc_scr)

    h = h_scr[...]
    act = _silu(_dotp(h, wg_ref[...], prec)) * _dotp(h, wu_ref[...], prec)
    acc_scr[...] += _dotp(act, wd_ref[...], prec)

    @pl.when(f == pl.num_programs(2) - 1)
    def _():
        o_ref[0] = x_ref[0] + gate_ref[0] * acc_scr[...]


def _ffn(x, gain, shift, scale, gate, wg, wu, wd, *, tm, tf, prec):
    bsz, t, d = x.shape
    fdim = wg.shape[1]
    mod_specs = []
    for mod in (shift, scale, gate):
        if mod.shape[1] == 1:
            mod_specs.append(pl.BlockSpec((1, 1, d), lambda b, i, f: (b, 0, 0)))
        else:
            mod_specs.append(pl.BlockSpec((1, tm, d), lambda b, i, f: (b, i, 0)))
    return pl.pallas_call(
        functools.partial(_ffn_kernel, prec=prec),
        grid=(bsz, t // tm, fdim // tf),
        in_specs=[pl.BlockSpec((1, tm, d), lambda b, i, f: (b, i, 0)),
                  pl.BlockSpec((1, d), lambda b, i, f: (0, 0)),
                  mod_specs[0], mod_specs[1], mod_specs[2],
                  pl.BlockSpec((d, tf), lambda b, i, f: (0, f)),
                  pl.BlockSpec((d, tf), lambda b, i, f: (0, f)),
                  pl.BlockSpec((tf, d), lambda b, i, f: (f, 0))],
        out_specs=pl.BlockSpec((1, tm, d), lambda b, i, f: (b, i, 0)),
        out_shape=jax.ShapeDtypeStruct((bsz, t, d), f32),
        scratch_shapes=[pltpu.VMEM((tm, d), f32 if prec == 3 else bf16), pltpu.VMEM((tm, d), f32)],
        compiler_params=_cparams(("parallel", "parallel", "arbitrary")),
        name="ffn",
    )(x, gain.reshape(1, d), shift, scale, gate, wg, wu, wd)


def _gdn_kernel(qkv_ref, z_ref, misc_ref, cw_ref, cinit_ref, hp_ref, nw_ref, s0_ref,
                o_ref, sfin_ref, xc_scr, s_scr, *, tb, n_valid):
    j = pl.program_id(1)
    c = GDN_CHUNK
    nh = GDN_HEADS

    @pl.when(j == 0)
    def _():
        xc_scr[0:8, :] = cinit_ref[0]
        s_scr[...] = s0_ref[0]

    x = qkv_ref[0]
    xc_scr[8:8 + tb, :] = x
    y = xc_scr[5:5 + tb, :] * cw_ref[0:1, :]
    y = y + xc_scr[6:6 + tb, :] * cw_ref[1:2, :]
    y = y + xc_scr[7:7 + tb, :] * cw_ref[2:3, :]
    y = y + x * cw_ref[3:4, :]
    xc_scr[0:8, :] = xc_scr[tb:tb + 8, :]
    y = _silu(y)

    misc = misc_ref[0]
    row = lax.broadcasted_iota(i32, (tb, MISC_W), 0) + j * tb
    live = row < n_valid
    log_a = jnp.where(live, hp_ref[0:1, :] * _softplus(misc + hp_ref[1:2, :]), 0.0)
    beta = jnp.where(live, jax.nn.sigmoid(misc), 0.0)

    r_i = lax.broadcasted_iota(i32, (tb, tb), 0)
    c_i = lax.broadcasted_iota(i32, (tb, tb), 1)
    sh = int(math.log2(c))
    ltri = (((r_i >> sh) == (c_i >> sh)) & (c_i <= r_i)).astype(bf16)
    g = _mm_01(ltri, log_a)
    g_t = g.T

    ri = lax.broadcasted_iota(i32, (c, c), 0)
    ci = lax.broadcasted_iota(i32, (c, c), 1)
    incl = ci <= ri
    strict = ci < ri
    eye = (ci == ri).astype(f32)
    quad = []
    for lvl in range(int(math.log2(c))):
        quad.append(((ri >> (lvl + 1)) == (ci >> (lvl + 1))) & (((ri >> lvl) & 1) == 1) & (((ci >> lvl) & 1) == 0))

    nchunk = tb // c
    units = [(n, h) for n in range(nchunk) for h in range(nh)]

    def stack(fn):
        return jnp.stack([fn(n * c, h) for n, h in units], axis=0)

    def bmm(eq, a_, b_):
        return jnp.einsum(eq, a_.astype(bf16), b_.astype(bf16), preferred_element_type=f32)

    q = stack(lambda r0, h: y[r0:r0 + c, h * GDN_DK:(h + 1) * GDN_DK])
    k = stack(lambda r0, h: y[r0:r0 + c, (nh + h) * GDN_DK:(nh + h + 1) * GDN_DK])
    v = stack(lambda r0, h: y[r0:r0 + c, 2 * nh * GDN_DK + h * GDN_DV:2 * nh * GDN_DK + (h + 1) * GDN_DV])
    q = q * lax.rsqrt(jnp.sum(q * q, axis=-1, keepdims=True) + EPS) * (GDN_DK ** -0.5)
    k = k * lax.rsqrt(jnp.sum(k * k, axis=-1, keepdims=True) + EPS)
    g_col = stack(lambda r0, h: g[r0:r0 + c, h:h + 1])
    g_row = stack(lambda r0, h: g_t[h:h + 1, r0:r0 + c])
    b_col = stack(lambda r0, h: beta[r0:r0 + c, nh + h:nh + h + 1])
    gam = jnp.where(incl, jnp.exp(jnp.where(incl, g_col - g_row, 0.0)), 0.0)
    kk = bmm('uid,ujd->uij', k, k)
    a = jnp.where(strict, b_col * gam * kk, 0.0)
    p = eye - jnp.where(quad[0], a, 0.0)
    for lvl in range(1, len(quad)):
        m = bmm('uij,ujk->uik', jnp.where(quad[lvl], a, 0.0), p)
        p = p - bmm('uij,ujk->uik', p, m)
    e_g = jnp.exp(g_col)
    sol = bmm('uij,ujd->uid', p, jnp.concatenate([b_col * v, (b_col * e_g) * k], axis=-1))
    vb, w = sol[:, :, :GDN_DV], sol[:, :, GDN_DV:]
    aqk = bmm('uid,ujd->uij', q, k) * gam
    qg = q * e_g
    g_last = g_col[:, c - 1:c, :]
    kd_t = jnp.swapaxes(k * jnp.exp(g_last - g_col), 1, 2)
    gc = jnp.exp(g_last)

    s = s_scr[...]
    for n in range(nchunk):
        sl = slice(n * nh, (n + 1) * nh)
        u = vb[sl] - bmm('hcd,hde->hce', w[sl], s)
        o = bmm('hcd,hde->hce', qg[sl], s) + bmm('hij,hje->hie', aqk[sl], u)
        s = gc[sl] * s + bmm('hdc,hce->hde', kd_t[sl], u)
        o = o * lax.rsqrt(jnp.mean(o * o, axis=-1, keepdims=True) + EPS) * nw_ref[...]
        for h in range(nh):
            zs = z_ref[0, n * c:(n + 1) * c, h * GDN_DV:(h + 1) * GDN_DV]
            o_ref[0, n * c:(n + 1) * c, h * GDN_DV:(h + 1) * GDN_DV] = o[h] * _silu(zs)
    s_scr[...] = s

    @pl.when(j == pl.num_programs(1) - 1)
    def _():
        sfin_ref[0] = s_scr[...]


def _gdn(qkv_raw, z, misc, conv_w, conv_buf, a_log, dt_bias, norm_w, s0, *, tb, n_valid):
    bsz, tp, ch = qkv_raw.shape
    cinit = jnp.concatenate([jnp.zeros((bsz, 5, ch), f32), conv_buf.astype(f32)], axis=1)
    hp = jnp.zeros((8, MISC_W), f32)
    hp = hp.at[0, :GDN_HEADS].set(-jnp.exp(a_log.astype(f32))).at[1, :GDN_HEADS].set(dt_bias.astype(f32))
    zw = GDN_HEADS * GDN_DV
    return pl.pallas_call(
        functools.partial(_gdn_kernel, tb=tb, n_valid=n_valid),
        grid=(bsz, tp // tb),
        in_specs=[pl.BlockSpec((1, tb, ch), lambda b, j: (b, j, 0)),
                  pl.BlockSpec((1, tb, zw), lambda b, j: (b, j, 0)),
                  pl.BlockSpec((1, tb, MISC_W), lambda b, j: (b, j, 0)),
                  pl.BlockSpec((GDN_CONV, ch), lambda b, j: (0, 0)),
                  pl.BlockSpec((1, 8, ch), lambda b, j: (b, 0, 0)),
                  pl.BlockSpec((8, MISC_W), lambda b, j: (0, 0)),
                  pl.BlockSpec((1, GDN_DV), lambda b, j: (0, 0)),
                  pl.BlockSpec((1, GDN_HEADS, GDN_DK, GDN_DV), lambda b, j: (b, 0, 0, 0))],
        out_specs=[pl.BlockSpec((1, tb, zw), lambda b, j: (b, j, 0)),
                   pl.BlockSpec((1, GDN_HEADS, GDN_DK, GDN_DV), lambda b, j: (b, 0, 0, 0))],
        out_shape=[jax.ShapeDtypeStruct((bsz, tp, zw), f32),
                   jax.ShapeDtypeStruct((bsz, GDN_HEADS, GDN_DK, GDN_DV), f32)],
        scratch_shapes=[pltpu.VMEM((tb + 8, ch), f32), pltpu.VMEM((GDN_HEADS, GDN_DK, GDN_DV), f32)],
        compiler_params=_cparams(("parallel", "arbitrary")),
        name="gdn",
    )(qkv_raw, z, misc, conv_w.astype(f32), cinit, hp, norm_w.reshape(1, GDN_DV).astype(f32), s0.astype(f32))


def _bucket_thresholds():
    exact = REL_BUCKETS // 2
    d = np.arange(0, 4 * REL_MAX_DIST, dtype=np.int64)

    def buckets(ft):
        nf = np.maximum(d, exact).astype(ft)
        large = exact + (np.log(nf / ft(exact)) / ft(math.log(REL_MAX_DIST / exact)) * ft(REL_BUCKETS - exact)).astype(np.int32)
        return np.where(d < exact, d, np.minimum(large, REL_BUCKETS - 1))

    b64, b32 = buckets(np.float64), buckets(np.float32)
    assert np.array_equal(b64, b32) and b64[-1] == REL_BUCKETS - 1 and np.all(np.diff(b64) >= 0)
    return [int(np.argmax(b64 >= k)) for k in range(1, REL_BUCKETS)]


_BUCKET_THR = _bucket_thresholds()
FAR_DIST = _BUCKET_THR[-1]


def _bias_kernel(rb_ref, d_ref, o_ref, *, slab, group_heads):
    rows = d_ref.shape[-2]
    for s in range(rows // slab):
        d = d_ref[0, 0, s * slab:(s + 1) * slab, :]
        h = pl.program_id(0) * group_heads + s
        val = jnp.full(d.shape, rb_ref[0, h], f32)
        for k in range(1, REL_BUCKETS):
            val = jnp.where(d >= _BUCKET_THR[k - 1], rb_ref[k, h], val)
        o_ref[0, 0, s * slab:(s + 1) * slab, :] = jnp.where(d < 0, NEG, val)


def _bias_table(rel_bias, dist, *, slab):
    g, nt, rows, cols = dist.shape
    return pl.pallas_call(
        functools.partial(_bias_kernel, slab=slab, group_heads=rows // slab),
        grid=(g, nt),
        in_specs=[pl.BlockSpec(memory_space=pltpu.SMEM),
                  pl.BlockSpec((1, 1, rows, cols), lambda a, b: (a, b, 0, 0))],
        out_specs=pl.BlockSpec((1, 1, rows, cols), lambda a, b: (a, b, 0, 0)),
        out_shape=jax.ShapeDtypeStruct(dist.shape, f32),
        compiler_params=_cparams(("arbitrary", "arbitrary")),
        name="bias_table",
    )(rel_bias.astype(f32), dist)


def _blockmean_kernel(x_ref, o_ref):
    x = x_ref[0]
    nb = x.shape[0] // NSA_BLOCK
    o_ref[0] = jnp.sum(x.reshape(nb, NSA_BLOCK, x.shape[1]), axis=1) * (1.0 / NSA_BLOCK)


def _block_means(kv4, *, tb):
    bsz, t, _ = kv4.shape
    w = 2 * NSA_KV_GROUPS * NSA_DH
    return pl.pallas_call(
        _blockmean_kernel,
        grid=(bsz, t // tb),
        in_specs=[pl.BlockSpec((1, tb, w), lambda b, i: (b, i, 0))],
        out_specs=pl.BlockSpec((1, tb // NSA_BLOCK, w), lambda b, i: (b, i, 0)),
        out_shape=jax.ShapeDtypeStruct((bsz, t // NSA_BLOCK, w), f32),
        compiler_params=_cparams(("parallel", "parallel")),
        name="nsa_block_means",
    )(kv4)


def _topn_mask(imp, n_sel):
    lane = lax.broadcasted_iota(i32, imp.shape, 1)
    sel = jnp.zeros(imp.shape, jnp.bool_)
    for _ in range(n_sel):
        mx = jnp.max(imp, axis=-1, keepdims=True)
        first = jnp.min(jnp.where(imp == mx, lane, imp.shape[1]), axis=-1, keepdims=True)
        hit = lane == first
        sel = sel | hit
        imp = jnp.where(hit, -jnp.inf, imp)
    return sel


def _nsa_prompt_kernel(q_ref, misc_ref, kcvc_ref, kaug_ref, v2_ref, kw_ref, vw_ref,
                       tabs_ref, tabw_ref, tabc_ref, c31_ref, o_ref):
    g = pl.program_id(1)
    i = pl.program_id(2)
    qb, dh, hpg = NSA_QBLOCK, NSA_DH, NSA_HPG
    rows = hpg * qb
    is_g0 = g == 0

    def ghalf(a):
        return jnp.where(is_g0, a[:, :dh], a[:, dh:])

    qt = q_ref[0] * (dh ** -0.5)
    qs = jnp.concatenate([qt[:, h * dh:(h + 1) * dh] for h in range(hpg)], axis=0)
    zero = jnp.zeros_like(qs)
    q2 = jnp.concatenate([jnp.where(is_g0, qs, zero), jnp.where(is_g0, zero, qs)], axis=1)

    kcvc = kcvc_ref[0]
    nblk = kcvc.shape[0]
    kc = ghalf(kcvc[:, :2 * dh])
    vc = ghalf(kcvc[:, 2 * dh:])
    s_c = _mm3_nt(qs, kc)
    lane = lax.broadcasted_iota(i32, (rows, nblk), 1)
    qrow = lax.broadcasted_iota(i32, (rows, nblk), 0) & (qb - 1)
    shift = (2 * i - 2 * NEAR_PAIRS + 1 + 4 * nblk) % nblk
    bias_c = pltpu.roll(tabc_ref[0], shift, 1)
    bias_c = jnp.where(lane < 2 * i - 2 * NEAR_PAIRS + 1, c31_ref[0], bias_c)
    readable = lane * NSA_BLOCK + (NSA_BLOCK - 1) <= i * qb + qrow
    s_c = jnp.where(readable, s_c + bias_c, NEG)
    m_c = jnp.max(s_c, axis=-1, keepdims=True)
    p_c = jnp.where(readable, jnp.exp(s_c - m_c), 0.0)
    p_c = p_c / jnp.maximum(jnp.sum(p_c, axis=-1, keepdims=True), 1e-30)
    o_c = _mm(p_c, vc)

    imp = p_c[0:qb]
    for h in range(1, hpg):
        imp = imp + p_c[h * qb:(h + 1) * qb]
    blk = lax.broadcasted_iota(i32, (qb, nblk), 1)
    cur = (i * qb + lax.broadcasted_iota(i32, (qb, nblk), 0)) >> int(math.log2(NSA_BLOCK))
    valid = blk <= cur
    forced = valid & ((blk == 0) | (blk > cur - NSA_LOCAL))
    imp = jnp.where(forced, -jnp.inf, jnp.where(valid, imp, -1.0))
    sel = (forced | _topn_mask(imp, min(NSA_TOPN, nblk) - 1 - NSA_LOCAL)) & valid
    msel = jnp.where(sel, 0.0, NEG).astype(bf16)
    qaug = jnp.concatenate([q2.astype(bf16), jnp.concatenate([msel] * hpg, axis=0)], axis=1)

    t_d = i // 2
    tk = 2 * PAIR

    def tile(k, carry):
        m, l, acc = carry
        t = t_d - k
        ks = pl.multiple_of(t * tk, tk)
        s = lax.dot_general(qaug, kaug_ref[0, pl.ds(ks, tk), :], (((1,), (1,)), ((), ())),
                            preferred_element_type=f32)
        i0 = jnp.clip(i - 2 * t, 0, NEAR_PAIRS)
        i1 = jnp.clip(i - 2 * t - 1, 0, NEAR_PAIRS)
        s = s + jnp.concatenate([tabs_ref[0, i0], tabs_ref[0, i1]], axis=1)
        m_new = jnp.maximum(m, jnp.max(s, axis=-1, keepdims=True))
        alpha = jnp.exp(m - m_new)
        p = jnp.exp(s - m_new)
        l = alpha * l + jnp.sum(p, axis=-1, keepdims=True)
        acc = alpha * acc + jnp.dot(p.astype(bf16), v2_ref[0, pl.ds(ks, tk), :], preferred_element_type=f32)
        return m_new, l, acc

    m0 = jnp.full((rows, 1), NEG, f32)
    l0 = jnp.zeros((rows, 1), f32)
    a0 = jnp.zeros((rows, 2 * dh), f32)
    _, l_s, acc_s = lax.fori_loop(0, t_d + 1, tile, (m0, l0, a0))
    o_s = ghalf(acc_s) / jnp.maximum(l_s, 1e-30)

    span = NSA_WINDOW + qb
    ws = pl.multiple_of(i * qb, qb)
    s_w = lax.dot_general(q2.astype(bf16), kw_ref[0, pl.ds(ws, span), :], (((1,), (1,)), ((), ())),
                          preferred_element_type=f32) + tabw_ref[0]
    kpos = lax.broadcasted_iota(i32, (rows, span), 1) + (i * qb - NSA_WINDOW)
    s_w = jnp.where(kpos >= 0, s_w, NEG)
    m_w = jnp.max(s_w, axis=-1, keepdims=True)
    p_w = jnp.exp(s_w - m_w)
    l_w = jnp.sum(p_w, axis=-1, keepdims=True)
    o_w = ghalf(jnp.dot(p_w.astype(bf16), vw_ref[0, pl.ds(ws, span), :], preferred_element_type=f32))
    o_w = o_w / jnp.maximum(l_w, 1e-30)

    gates = jax.nn.sigmoid(misc_ref[0])
    outs = []
    for h in range(hpg):
        r = slice(h * qb, (h + 1) * qb)
        acc = None
        for jb, branch in enumerate((o_c, o_s, o_w)):
            l0_ = 8 + 3 * h + jb
            l1_ = 8 + 3 * (hpg + h) + jb
            gcol = jnp.where(is_g0, gates[:, l0_:l0_ + 1], gates[:, l1_:l1_ + 1])
            term = gcol * branch[r]
            acc = term if acc is None else acc + term
        outs.append(acc)
    o_ref[0] = jnp.concatenate(outs, axis=1)


LOG2E = 1.4426950408889634
NSA_KEY_TILE = 4 * PAIR


def _nsa_prompt_kernel_h(q_ref, misc_ref, kcvc_ref, kaug_ref, vg_ref, kw_ref, vw_ref,
                         tabs_ref, tabw_ref, tabc_ref, c31_ref, o_ref, qaug_scr, s_scr):
    g = pl.program_id(1)
    i = pl.program_id(2)
    qb, dh, hpg = NSA_QBLOCK, NSA_DH, NSA_HPG
    is_g0 = g == 0
    nn = (((1,), (1,)), ((), ()))

    def ghalf(a):
        return jnp.where(is_g0, a[:, :dh], a[:, dh:])

    def hrows(ref, *lead):
        return [ref[lead + (slice(h * qb, (h + 1) * qb), slice(None))] for h in range(hpg)]

    qt = q_ref[0] * (dh ** -0.5)
    qh = [qt[:, h * dh:(h + 1) * dh] for h in range(hpg)]
    kcvc = kcvc_ref[0]
    nblk = kcvc.shape[0]
    kc = ghalf(kcvc[:, :2 * dh])
    vc = ghalf(kcvc[:, 2 * dh:])

    rows = hpg * qb
    qs = jnp.concatenate(qh, axis=0)
    zero = jnp.zeros_like(qs)
    q2 = jnp.concatenate([jnp.where(is_g0, qs, zero), jnp.where(is_g0, zero, qs)], axis=1)
    q2 = (q2 * LOG2E).astype(bf16)

    blk4 = lax.broadcasted_iota(i32, (rows, nblk), 1)
    qrow4 = lax.broadcasted_iota(i32, (rows, nblk), 0) & (qb - 1)
    first_tab = 2 * i - 2 * NEAR_PAIRS + 1
    shift = (first_tab + 4 * nblk) % nblk
    readable = blk4 * NSA_BLOCK + (NSA_BLOCK - 1) <= i * qb + qrow4
    bias = jnp.where(blk4 < first_tab, c31_ref[0], pltpu.roll(tabc_ref[0], shift, 1))
    s_c = jnp.where(readable, _mm3_nt(qs, kc) + bias, NEG)
    m_c = jnp.max(s_c, axis=-1, keepdims=True)
    p_c = jnp.where(readable, jnp.exp(s_c - m_c), 0.0)
    p_c = p_c / jnp.maximum(jnp.sum(p_c, axis=-1, keepdims=True), 1e-30)
    o_cmp = _mm(p_c, vc)
    imp = p_c[0:qb]
    for h in range(1, hpg):
        imp = imp + p_c[h * qb:(h + 1) * qb]

    span = NSA_WINDOW + qb
    ws = pl.multiple_of(i * qb, qb)
    s_w = lax.dot_general(q2, kw_ref[0, pl.ds(ws, span), :], nn, preferred_element_type=f32) + tabw_ref[0]
    in_seq = lax.broadcasted_iota(i32, (rows, span), 1) + (i * qb - NSA_WINDOW) >= 0
    s_w = jnp.where(in_seq, s_w, NEG)
    m_w = jnp.max(s_w, axis=-1, keepdims=True)
    p_w = jnp.exp2(s_w - m_w)
    l_w = jnp.sum(p_w, axis=-1, keepdims=True)
    o_win = ghalf(jnp.dot(p_w.astype(bf16), vw_ref[0, pl.ds(ws, span), :], preferred_element_type=f32))
    o_win = o_win / jnp.maximum(l_w, 1e-30)

    blk = lax.broadcasted_iota(i32, (qb, nblk), 1)
    qrow = lax.broadcasted_iota(i32, (qb, nblk), 0)
    cur = (i * qb + qrow) >> int(math.log2(NSA_BLOCK))
    valid = blk <= cur
    forced = valid & ((blk == 0) | (blk > cur - NSA_LOCAL))
    imp_t = jnp.where(forced, -jnp.inf, jnp.where(valid, imp, -1.0)).T
    brow = lax.broadcasted_iota(i32, (nblk, qb), 0)
    picked = jnp.zeros((nblk, qb), f32)
    for _ in range(min(NSA_TOPN, nblk) - 1 - NSA_LOCAL):
        mx = jnp.max(imp_t, axis=0, keepdims=True)
        first = jnp.min(jnp.where(imp_t == mx, brow, nblk), axis=0, keepdims=True)
        hit = brow == first
        picked = jnp.where(hit, 1.0, picked)
        imp_t = jnp.where(hit, -jnp.inf, imp_t)
    sel = (forced | (picked.T > 0.5)) & valid
    msel = jnp.where(sel, 0.0, NEG).astype(bf16)
    qaug_scr[...] = jnp.concatenate([q2, jnp.concatenate([msel] * hpg, axis=0)], axis=1)

    tk = s_scr.shape[1]
    ppt = tk // PAIR
    t_d = i // ppt

    def scores(t):
        ks = pl.multiple_of(t * tk, tk)
        s = lax.dot_general(qaug_scr[...], kaug_ref[0, pl.ds(ks, tk), :], nn, preferred_element_type=f32)
        tabs = [tabs_ref[0, jnp.clip(i - ppt * t - j, 0, NEAR_PAIRS)] for j in range(ppt)]
        return s + jnp.concatenate(tabs, axis=1)

    s_scr[...] = scores(t_d)

    def tile(k_it, carry):
        m, acc = carry
        t = t_d - k_it
        s = s_scr[...]
        s_next = scores(jnp.maximum(t - 1, 0))
        m_new = jnp.maximum(m, jnp.max(s, axis=-1, keepdims=True))
        alpha = jnp.exp2(m - m_new)
        p = jnp.exp2(s - m_new).astype(bf16)
        ks = pl.multiple_of(t * tk, tk)
        acc = alpha * acc + jnp.dot(p, vg_ref[0, 0, pl.ds(ks, tk), :], preferred_element_type=f32)
        s_scr[...] = s_next
        return m_new, acc

    _, acc_sel = lax.fori_loop(0, t_d + 1, tile, (jnp.full((rows, 1), NEG, f32), jnp.zeros((rows, 2 * dh), f32)))
    o_sel = acc_sel[:, :dh] / jnp.maximum(acc_sel[:, dh:dh + 1], 1e-30)

    gates = jax.nn.sigmoid(misc_ref[0])
    outs = []
    for h in range(hpg):
        r = slice(h * qb, (h + 1) * qb)
        acc = None
        for jb, branch in enumerate((o_cmp[r], o_sel[r], o_win[r])):
            la = 8 + 3 * h + jb
            lb = 8 + 3 * (hpg + h) + jb
            term = jnp.where(is_g0, gates[:, la:la + 1], gates[:, lb:lb + 1]) * branch
            acc = term if acc is None else acc + term
        outs.append(acc)
    o_ref[0] = jnp.concatenate(outs, axis=1)


def _nsa_prompt_kernel_m(q_ref, misc_ref, kcvc_ref, kaug_ref, v3_ref, kw_ref, vw3_ref,
                         tabs_ref, tabw_ref, tabc_ref, c31_ref, o_ref, qaug_scr, s_scr):
    i = pl.program_id(1)
    qb, dh, hpg, ng, nh = NSA_QBLOCK, NSA_DH, NSA_HPG, NSA_KV_GROUPS, NSA_HEADS
    assert ng == 2
    rows, grows = nh * qb, hpg * qb
    nn = (((1,), (1,)), ((), ()))
    in_g0 = lax.broadcasted_iota(i32, (rows, 1), 0) < grows

    def own(a):
        return jnp.where(in_g0, a[:, :dh], a[:, dh:2 * dh])

    qt = q_ref[0] * (dh ** -0.5)
    qs = jnp.concatenate([qt[:, h * dh:(h + 1) * dh] for h in range(nh)], axis=0)
    zero = jnp.zeros_like(qs)
    q2f = jnp.concatenate([jnp.where(in_g0, qs, zero), jnp.where(in_g0, zero, qs)], axis=1)
    q2 = (q2f * LOG2E).astype(bf16)
    kcvc = kcvc_ref[0]
    nblk = kcvc.shape[0]

    blk4 = lax.broadcasted_iota(i32, (rows, nblk), 1)
    qrow4 = lax.broadcasted_iota(i32, (rows, nblk), 0) & (qb - 1)
    first_tab = 2 * i - 2 * NEAR_PAIRS + 1
    shift = (first_tab + 4 * nblk) % nblk
    readable = blk4 * NSA_BLOCK + (NSA_BLOCK - 1) <= i * qb + qrow4
    bias = jnp.where(blk4 < first_tab, c31_ref[...], pltpu.roll(tabc_ref[...], shift, 1))
    s_c = jnp.where(readable, _mm3_nt(q2f, kcvc[:, :2 * dh]) + bias, NEG)
    m_c = jnp.max(s_c, axis=-1, keepdims=True)
    p_c = jnp.where(readable, jnp.exp(s_c - m_c), 0.0)
    p_c = p_c / jnp.maximum(jnp.sum(p_c, axis=-1, keepdims=True), 1e-30)
    o_cmp = own(_mm(p_c, kcvc[:, 2 * dh:]))
    imp = []
    for g in range(ng):
        acc = p_c[g * grows:g * grows + qb]
        for h in range(1, hpg):
            acc = acc + p_c[g * grows + h * qb:g * grows + (h + 1) * qb]
        imp.append(acc)
    imp = jnp.concatenate(imp, axis=0)

    span = NSA_WINDOW + qb
    ws = pl.multiple_of(i * qb, qb)
    s_w = lax.dot_general(q2, kw_ref[0, pl.ds(ws, span), :], nn, preferred_element_type=f32) + tabw_ref[...]
    in_seq = lax.broadcasted_iota(i32, (rows, span), 1) + (i * qb - NSA_WINDOW) >= 0
    s_w = jnp.where(in_seq, s_w, NEG)
    p_w = jnp.exp2(s_w - jnp.max(s_w, axis=-1, keepdims=True)).astype(bf16)
    acc_w = jnp.dot(p_w, vw3_ref[0, pl.ds(ws, span), :], preferred_element_type=f32)
    o_win = own(acc_w) / jnp.maximum(acc_w[:, 2 * dh:2 * dh + 1], 1e-30)

    blk = lax.broadcasted_iota(i32, (ng * qb, nblk), 1)
    qrow = lax.broadcasted_iota(i32, (ng * qb, nblk), 0) & (qb - 1)
    cur = (i * qb + qrow) >> int(math.log2(NSA_BLOCK))
    valid = blk <= cur
    forced = valid & ((blk == 0) | (blk > cur - NSA_LOCAL))
    imp_t = jnp.where(forced, -jnp.inf, jnp.where(valid, imp, -1.0)).T
    brow = lax.broadcasted_iota(i32, imp_t.shape, 0)
    picked = jnp.zeros(imp_t.shape, f32)
    for _ in range(min(NSA_TOPN, nblk) - 1 - NSA_LOCAL):
        mx = jnp.max(imp_t, axis=0, keepdims=True)
        first = jnp.min(jnp.where(imp_t == mx, brow, nblk), axis=0, keepdims=True)
        hit = brow == first
        picked = jnp.where(hit, 1.0, picked)
        imp_t = jnp.where(hit, -jnp.inf, imp_t)
    sel = (forced | (picked.T > 0.5)) & valid
    msel = jnp.where(sel, 0.0, NEG).astype(bf16)
    msel = jnp.concatenate([msel[g * qb:(g + 1) * qb] for g in range(ng) for _ in range(hpg)], axis=0)
    qaug_scr[...] = jnp.concatenate([q2, msel], axis=1)

    tk = s_scr.shape[1]
    ppt = tk // PAIR
    t_d = i // ppt

    def scores(t):
        ks = pl.multiple_of(t * tk, tk)
        s = lax.dot_general(qaug_scr[...], kaug_ref[0, pl.ds(ks, tk), :], nn, preferred_element_type=f32)
        tabs = [tabs_ref[jnp.clip(i - ppt * t - j, 0, NEAR_PAIRS)] for j in range(ppt)]
        return s + jnp.concatenate(tabs, axis=1)

    s_scr[...] = scores(t_d)

    def tile(k_it, carry):
        m, acc = carry
        t = t_d - k_it
        s = s_scr[...]
        s_next = scores(jnp.maximum(t - 1, 0))
        m_new = jnp.maximum(m, jnp.max(s, axis=-1, keepdims=True))
        alpha = jnp.exp2(m - m_new)
        p = jnp.exp2(s - m_new).astype(bf16)
        ks = pl.multiple_of(t * tk, tk)
        acc = alpha * acc + jnp.dot(p, v3_ref[0, pl.ds(ks, tk), :], preferred_element_type=f32)
        s_scr[...] = s_next
        return m_new, acc

    _, acc_s = lax.fori_loop(0, t_d + 1, tile, (jnp.full((rows, 1), NEG, f32), jnp.zeros((rows, 4 * dh), f32)))
    o_sel = own(acc_s) / jnp.maximum(acc_s[:, 2 * dh:2 * dh + 1], 1e-30)

    gates = jax.nn.sigmoid(misc_ref[0])
    outs = []
    for h in range(nh):
        r = slice(h * qb, (h + 1) * qb)
        acc = None
        for jb, branch in enumerate((o_cmp[r], o_sel[r], o_win[r])):
            lane = 2 * GDN_HEADS + 3 * h + jb
            term = gates[:, lane:lane + 1] * branch
            acc = term if acc is None else acc + term
        outs.append(acc)
    o_ref[0] = jnp.concatenate(outs, axis=1)


def _nsa_prompt_m(q_b, misc, kv4, kvwin, rel_bias):
    bsz, t, _ = q_b.shape
    dh, qb = NSA_DH, NSA_QBLOCK
    nblk = t // NSA_BLOCK
    assert nblk == PAIR and t % NSA_KEY_TILE == 0
    kcvc = _block_means(kv4, tb=512)
    onehot = (jnp.arange(t, dtype=i32)[:, None] // NSA_BLOCK == jnp.arange(nblk, dtype=i32)[None, :]).astype(bf16)
    kaug = jnp.concatenate([kv4[:, :, 4 * dh:6 * dh].astype(bf16),
                            jnp.broadcast_to(onehot[None], (bsz, t, nblk))], axis=-1)
    ones = jnp.ones((bsz, t, 2 * dh), bf16)
    v3 = jnp.concatenate([kv4[:, :, 6 * dh:8 * dh].astype(bf16), ones], axis=-1)
    pad = ((0, 0), (NSA_WINDOW, 0), (0, 0))
    kw = jnp.pad(kvwin[:, :, :2 * dh].astype(bf16), pad)
    vw3 = jnp.pad(jnp.concatenate([kvwin[:, :, 2 * dh:].astype(bf16), ones], axis=-1), pad)
    tabs, tabw, tabc, c31 = _nsa_tables(rel_bias)
    rows = NSA_HEADS * qb
    span = NSA_WINDOW + qb
    ns = NEAR_PAIRS + 1
    tabs = (jnp.transpose(tabs, (1, 0, 2, 3)) * LOG2E).reshape(ns, rows, PAIR)
    tabw = (tabw * LOG2E).reshape(rows, span)
    tabc = tabc.reshape(rows, PAIR)
    c31 = c31.reshape(rows, 1)
    once = pl.Buffered(1)
    per_b = lambda shape: pl.BlockSpec((1,) + shape, lambda b, i: (b,) + (0,) * len(shape), pipeline_mode=once)
    const = lambda shape: pl.BlockSpec(shape, lambda b, i: (0,) * len(shape), pipeline_mode=once)
    return pl.pallas_call(
        _nsa_prompt_kernel_m,
        grid=(bsz, t // qb),
        in_specs=[pl.BlockSpec((1, qb, NSA_HEADS * dh), lambda b, i: (b, i, 0)),
                  pl.BlockSpec((1, qb, MISC_W), lambda b, i: (b, i, 0)),
                  per_b((nblk, 4 * dh)), per_b((t, 2 * dh + nblk)), per_b((t, 4 * dh)),
                  per_b((t + NSA_WINDOW, 2 * dh)), per_b((t + NSA_WINDOW, 4 * dh)),
                  const((ns, rows, PAIR)), const((rows, span)), const((rows, PAIR)), const((rows, 1))],
        out_specs=pl.BlockSpec((1, qb, NSA_HEADS * dh), lambda b, i: (b, i, 0)),
        out_shape=jax.ShapeDtypeStruct((bsz, t, NSA_HEADS * dh), f32),
        scratch_shapes=[pltpu.VMEM((rows, 2 * dh + nblk), bf16), pltpu.VMEM((rows, NSA_KEY_TILE), f32)],
        compiler_params=_cparams(("parallel", "arbitrary")),
        name="nsa_prompt",
    )(q_b, misc, kcvc, kaug, v3, kw, vw3, tabs, tabw, tabc, c31)


def _nsa_tables(rel_bias):
    qb, hpg = NSA_QBLOCK, NSA_HPG
    rows = hpg * qb
    q = (np.arange(rows) % qb)[:, None]
    c = np.arange(PAIR)[None, :]
    d_sel = [PAIR * idx + q - c for idx in range(NEAR_PAIRS + 1)]
    assert PAIR * NEAR_PAIRS - (PAIR - 1) >= FAR_DIST
    mm = 2 * NEAR_PAIRS - 1 - c
    d_cmp = np.maximum(np.where(mm >= -1, NSA_BLOCK * mm + q - (NSA_BLOCK - 1), 0), 0)
    assert NSA_BLOCK * (2 * NEAR_PAIRS) - (NSA_BLOCK - 1) >= FAR_DIST
    cw = np.arange(NSA_WINDOW + qb)[None, :]
    d_win = q + NSA_WINDOW - cw
    d_win = np.where((d_win >= 0) & (d_win < NSA_WINDOW), d_win, -1)
    n_win = (NSA_WINDOW + qb) // PAIR
    tiles = d_sel + [d_cmp] + [d_win[:, k * PAIR:(k + 1) * PAIR] for k in range(n_win)]
    dist = np.broadcast_to(np.stack(tiles)[None], (NSA_KV_GROUPS, len(tiles), rows, PAIR))
    tab = _bias_table(rel_bias, jnp.asarray(dist, i32), slab=qb)
    ns = NEAR_PAIRS + 1
    tabs = tab[:, :ns]
    tabc = tab[:, ns]
    tabw = jnp.concatenate([tab[:, ns + 1 + k] for k in range(n_win)], axis=-1)
    c31 = jnp.repeat(rel_bias.astype(f32)[REL_BUCKETS - 1].reshape(NSA_KV_GROUPS, hpg), qb, axis=1)
    return tabs, tabw, tabc, c31.reshape(NSA_KV_GROUPS, rows, 1)


def _nsa_prompt(q_b, misc, kv4, kvwin, rel_bias):
    bsz, t, _ = q_b.shape
    dh, qb = NSA_DH, NSA_QBLOCK
    nblk = t // NSA_BLOCK
    assert nblk == PAIR and t % NSA_KEY_TILE == 0
    kcvc = _block_means(kv4, tb=512)
    onehot = (jnp.arange(t, dtype=i32)[:, None] // NSA_BLOCK == jnp.arange(nblk, dtype=i32)[None, :]).astype(bf16)
    kaug = jnp.concatenate([kv4[:, :, 4 * dh:6 * dh].astype(bf16),
                            jnp.broadcast_to(onehot[None], (bsz, t, nblk))], axis=-1)
    ones = jnp.ones((bsz, t, dh), bf16)
    vg = jnp.stack([jnp.concatenate([kv4[:, :, (6 + g) * dh:(7 + g) * dh].astype(bf16), ones], axis=-1)
                    for g in range(NSA_KV_GROUPS)], axis=1)
    pad = ((0, 0), (NSA_WINDOW, 0), (0, 0))
    kw = jnp.pad(kvwin[:, :, :2 * dh].astype(bf16), pad)
    vw = jnp.pad(kvwin[:, :, 2 * dh:].astype(bf16), pad)
    tabs, tabw, tabc, c31 = _nsa_tables(rel_bias)
    tabs, tabw = tabs * LOG2E, tabw * LOG2E
    rows = NSA_HPG * qb
    gw = NSA_HPG * dh
    span = NSA_WINDOW + qb
    return pl.pallas_call(
        _nsa_prompt_kernel_h,
        grid=(bsz, NSA_KV_GROUPS, t // qb),
        in_specs=[pl.BlockSpec((1, qb, gw), lambda b, g, i: (b, i, g)),
                  pl.BlockSpec((1, qb, MISC_W), lambda b, g, i: (b, i, 0)),
                  pl.BlockSpec((1, nblk, 4 * dh), lambda b, g, i: (b, 0, 0)),
                  pl.BlockSpec((1, t, 2 * dh + nblk), lambda b, g, i: (b, 0, 0)),
                  pl.BlockSpec((1, 1, t, 2 * dh), lambda b, g, i: (b, g, 0, 0)),
                  pl.BlockSpec((1, t + NSA_WINDOW, 2 * dh), lambda b, g, i: (b, 0, 0)),
                  pl.BlockSpec((1, t + NSA_WINDOW, 2 * dh), lambda b, g, i: (b, 0, 0)),
                  pl.BlockSpec((1, NEAR_PAIRS + 1, rows, PAIR), lambda b, g, i: (g, 0, 0, 0)),
                  pl.BlockSpec((1, rows, span), lambda b, g, i: (g, 0, 0)),
                  pl.BlockSpec((1, rows, PAIR), lambda b, g, i: (g, 0, 0)),
                  pl.BlockSpec((1, rows, 1), lambda b, g, i: (g, 0, 0))],
        out_specs=pl.BlockSpec((1, qb, gw), lambda b, g, i: (b, i, g)),
        out_shape=jax.ShapeDtypeStruct((bsz, t, NSA_HEADS * dh), f32),
        scratch_shapes=[pltpu.VMEM((rows, 2 * dh + nblk), bf16), pltpu.VMEM((rows, NSA_KEY_TILE), f32)],
        compiler_params=_cparams(("parallel", "parallel", "arbitrary")),
        name="nsa_prompt",
    )(q_b, misc, kcvc, kaug, vg, kw, vw, tabs, tabw, tabc, c31)


MEANS_PAGES = 8


def _nsa_s_means_kernel(pt_ref, *refs):
    del pt_ref
    x_refs, o_ref = refs[:-1], refs[-1]
    rows = [[[] for _ in range(NSA_KV_GROUPS)] for _ in range(2)]
    for x_ref in x_refs:
        page = x_ref.shape[1]
        for j in range(2):
            for g in range(NSA_KV_GROUPS):
                x = x_ref[0, :, j, g, :]
                for n in range(page // NSA_BLOCK):
                    rows[j][g].append(jnp.sum(x[n * NSA_BLOCK:(n + 1) * NSA_BLOCK], axis=0, keepdims=True)
                                      * (1.0 / NSA_BLOCK))
    for j in range(2):
        for g in range(NSA_KV_GROUPS):
            o_ref[0, j * NSA_KV_GROUPS + g] = jnp.concatenate(rows[j][g], axis=0)


def _nsa_s_means(cache_kv, page_table):
    db, n_pages = page_table.shape
    page = cache_kv.shape[1]
    nblk = n_pages * page // NSA_BLOCK
    kp = MEANS_PAGES
    rows = kp * page // NSA_BLOCK
    assert n_pages % kp == 0 and rows % 8 == 0

    def page_spec(k):
        return pl.BlockSpec((1, page, 2, NSA_KV_GROUPS, NSA_DH),
                            lambda b, p, pt: (pt[b * n_pages + p * kp + k], 0, 0, 0, 0))

    return pl.pallas_call(
        _nsa_s_means_kernel,
        grid_spec=pltpu.PrefetchScalarGridSpec(
            num_scalar_prefetch=1,
            grid=(db, n_pages // kp),
            in_specs=[page_spec(k) for k in range(kp)],
            out_specs=pl.BlockSpec((1, 2 * NSA_KV_GROUPS, rows, NSA_DH), lambda b, p, pt: (b, 0, p, 0))),
        out_shape=jax.ShapeDtypeStruct((db, 2 * NSA_KV_GROUPS, nblk, NSA_DH), f32),
        compiler_params=_cparams(("parallel", "parallel")),
        name="nsa_sample_means",
    )(page_table.reshape(-1), *([cache_kv] * kp))


def _nsa_s_scores_kernel(q_ref, kcvc_ref, win_ref, kvn_ref, tc_ref, tw_ref, b0_ref, oc_ref, ow_ref, sel_ref):
    dh, hpg = NSA_DH, NSA_HPG
    q = q_ref[0] * (dh ** -0.5)
    nblk = kcvc_ref.shape[2]
    kvn = kvn_ref[0]
    row = lax.broadcasted_iota(i32, (NSA_HEADS, 1), 0)
    lane = lax.broadcasted_iota(i32, (1, nblk), 1)
    n_sel = NSA_TOPN - 1
    for g in range(NSA_KV_GROUPS):
        in_g = (row >= g * hpg) & (row < (g + 1) * hpg)
        kc = kcvc_ref[0, g]
        vc = kcvc_ref[0, NSA_KV_GROUPS + g]
        s_c = _mm3_nt(q, kc) + tc_ref[...]
        m_c = jnp.max(s_c, axis=-1, keepdims=True)
        p_c = jnp.exp(s_c - m_c)
        p_c = p_c / jnp.maximum(jnp.sum(p_c, axis=-1, keepdims=True), 1e-30)
        o_c = _mm(p_c, vc)
        imp = jnp.sum(jnp.where(in_g, p_c, 0.0), axis=0, keepdims=True)
        forced_blocks = [0] + [nblk - k for k in range(1, NSA_LOCAL)]
        imp = jnp.where((lane == 0) | (lane > nblk - NSA_LOCAL), -jnp.inf, imp)
        picks = jnp.zeros((1, nblk), i32)
        for it, fb in enumerate(forced_blocks):
            picks = jnp.where(lane == it, fb, picks)
        for it in range(len(forced_blocks), n_sel):
            mx = jnp.max(imp, axis=-1, keepdims=True)
            first = jnp.min(jnp.where(imp == mx, lane, nblk), axis=-1, keepdims=True)
            picks = jnp.where(lane == it, first, picks)
            imp = jnp.where(lane == first, -jnp.inf, imp)
        sel_ref[0, g:g + 1, :] = picks
        kw = win_ref[0, :, 0, g, :]
        vw = win_ref[0, :, 1, g, :]
        s_w = _mm3_nt(q, kw) + tw_ref[...]
        s_n = jnp.sum(q * kvn[:, g * dh:(g + 1) * dh], axis=-1, keepdims=True) + b0_ref[...]
        m_w = jnp.maximum(jnp.max(s_w, axis=-1, keepdims=True), s_n)
        p_w = jnp.exp(s_w - m_w)
        p_n = jnp.exp(s_n - m_w)
        l_w = jnp.sum(p_w, axis=-1, keepdims=True) + p_n
        o_w = (_mm(p_w, vw) + p_n * kvn[:, (NSA_KV_GROUPS + g) * dh:(NSA_KV_GROUPS + g + 1) * dh]) / jnp.maximum(l_w, 1e-30)
        if g == 0:
            oc_ref[0] = o_c
            ow_ref[0] = o_w
        else:
            oc_ref[0] = jnp.where(in_g, o_c, oc_ref[0])
            ow_ref[0] = jnp.where(in_g, o_w, ow_ref[0])


def _nsa_s_sel_kernel(sel_ref, pt_ref, q_ref, k_ref, v_ref, tb_ref, kvn_ref, b0_ref, oc_ref, ow_ref, gt_ref,
                      o_ref, m_scr, l_scr, acc_scr):
    del sel_ref, pt_ref
    g = pl.program_id(1)
    j = pl.program_id(2)
    dh, hpg = NSA_DH, NSA_HPG
    is_g0 = g == 0
    q = q_ref[0] * (dh ** -0.5)
    kvn = kvn_ref[0]

    @pl.when(j == 0)
    def _():
        k_n = jnp.where(is_g0, kvn[:, 4 * dh:5 * dh], kvn[:, 5 * dh:6 * dh])
        v_n = jnp.where(is_g0, kvn[:, 6 * dh:7 * dh], kvn[:, 7 * dh:8 * dh])
        m_scr[...] = jnp.sum(q * k_n, axis=-1, keepdims=True) + b0_ref[...]
        l_scr[...] = jnp.ones_like(l_scr)
        acc_scr[...] = jnp.broadcast_to(v_n, acc_scr.shape)

    k = jnp.where(is_g0, k_ref[0, :, 0, 0, :], k_ref[0, :, 0, 1, :])
    v = jnp.where(is_g0, v_ref[0, :, 0, 0, :], v_ref[0, :, 0, 1, :])
    s = _mm3_nt(q, k) + tb_ref[0]
    m_old = m_scr[...]
    m_new = jnp.maximum(m_old, jnp.max(s, axis=-1, keepdims=True))
    alpha = jnp.exp(m_old - m_new)
    p = jnp.exp(s - m_new)
    l_scr[...] = alpha * l_scr[...] + jnp.sum(p, axis=-1, keepdims=True)
    acc_scr[...] = alpha * acc_scr[...] + _mm(p, v)
    m_scr[...] = m_new

    @pl.when(j == pl.num_programs(2) - 1)
    def _():
        o_s = acc_scr[...] / jnp.maximum(l_scr[...], 1e-30)
        gt = gt_ref[0]
        o = gt[:, 0:1] * oc_ref[0] + gt[:, 1:2] * o_s + gt[:, 2:3] * ow_ref[0]
        row = lax.broadcasted_iota(i32, o.shape, 0)
        in_g = (row >= g * hpg) & (row < (g + 1) * hpg)

        @pl.when(is_g0)
        def _():
            o_ref[0] = o

        @pl.when(jnp.logical_not(is_g0))
        def _():
            o_ref[0] = jnp.where(in_g, o, o_ref[0])


def _nsa_sample(q_b, gates, kv4_new, kvwin_new, cache_kv, cache_win, page_table, rel_bias):
    db = q_b.shape[0]
    dh = NSA_DH
    n_pool, page = cache_kv.shape[:2]
    n_pages = page_table.shape[1]
    past = n_pages * page
    nblk = past // NSA_BLOCK
    wb = cache_win.shape[1]
    assert nblk == PAIR and page % NSA_BLOCK == 0 and wb == NSA_WINDOW
    kcvc = _nsa_s_means(cache_kv, page_table)

    n = np.arange(nblk)
    d_cmp = past - (n * NSA_BLOCK + NSA_BLOCK - 1)
    jw = np.arange(wb)
    d_win = np.where(jw >= 1, wb - jw, -1)
    pos = np.arange(nblk * NSA_BLOCK)
    d_sel = past - pos
    width = nblk + wb + nblk * NSA_BLOCK
    dist = np.broadcast_to(np.concatenate([d_cmp, d_win, d_sel])[None, :], (NSA_HEADS, width))
    tab = _bias_table(rel_bias, jnp.asarray(dist[None, None], i32), slab=1)[0, 0]
    t_cmp, t_win = tab[:, :nblk], tab[:, nblk:nblk + wb]
    t_sel = jnp.transpose(tab[:, nblk + wb:].reshape(NSA_HEADS, nblk, NSA_BLOCK), (1, 0, 2))
    b0 = rel_bias.astype(f32)[0].reshape(NSA_HEADS, 1)

    q3 = q_b.reshape(db, NSA_HEADS, dh)
    o_c, o_w, sel = pl.pallas_call(
        _nsa_s_scores_kernel,
        grid=(db,),
        in_specs=[pl.BlockSpec((1, NSA_HEADS, dh), lambda b: (b, 0, 0)),
                  pl.BlockSpec((1, 2 * NSA_KV_GROUPS, nblk, dh), lambda b: (b, 0, 0, 0)),
                  pl.BlockSpec((1, wb, 2, NSA_KV_GROUPS, dh), lambda b: (b, 0, 0, 0, 0)),
                  pl.BlockSpec((1, 1, 4 * dh), lambda b: (b, 0, 0)),
                  pl.BlockSpec((NSA_HEADS, nblk), lambda b: (0, 0)),
                  pl.BlockSpec((NSA_HEADS, wb), lambda b: (0, 0)),
                  pl.BlockSpec((NSA_HEADS, 1), lambda b: (0, 0))],
        out_specs=[pl.BlockSpec((1, NSA_HEADS, dh), lambda b: (b, 0, 0)),
                   pl.BlockSpec((1, NSA_HEADS, dh), lambda b: (b, 0, 0)),
                   pl.BlockSpec((1, NSA_KV_GROUPS, nblk), lambda b: (b, 0, 0))],
        out_shape=[jax.ShapeDtypeStruct((db, NSA_HEADS, dh), f32),
                   jax.ShapeDtypeStruct((db, NSA_HEADS, dh), f32),
                   jax.ShapeDtypeStruct((db, NSA_KV_GROUPS, nblk), i32)],
        compiler_params=_cparams(("parallel",)),
        name="nsa_sample_scores",
    )(q3, kcvc, cache_win.astype(f32), kvwin_new.reshape(db, 1, 4 * dh), t_cmp, t_win, b0)

    n_sel = NSA_TOPN - 1
    sel_flat = sel[:, :, :n_sel].reshape(-1)
    halves = page // NSA_BLOCK
    cache_h = cache_kv.reshape(n_pool * halves, NSA_BLOCK, 4, NSA_KV_GROUPS, dh)

    def blk_of(b, g, j, sel_r, pt_r):
        nb = sel_r[(b * NSA_KV_GROUPS + g) * n_sel + j]
        return nb, pt_r[b * n_pages + nb // halves] * halves + nb % halves

    o = pl.pallas_call(
        _nsa_s_sel_kernel,
        grid_spec=pltpu.PrefetchScalarGridSpec(
            num_scalar_prefetch=2,
            grid=(db, NSA_KV_GROUPS, n_sel),
            in_specs=[pl.BlockSpec((1, NSA_HEADS, dh), lambda b, g, j, s, p: (b, 0, 0)),
                      pl.BlockSpec((1, NSA_BLOCK, 1, NSA_KV_GROUPS, dh),
                                   lambda b, g, j, s, p: (blk_of(b, g, j, s, p)[1], 0, 2, 0, 0)),
                      pl.BlockSpec((1, NSA_BLOCK, 1, NSA_KV_GROUPS, dh),
                                   lambda b, g, j, s, p: (blk_of(b, g, j, s, p)[1], 0, 3, 0, 0)),
                      pl.BlockSpec((1, NSA_HEADS, NSA_BLOCK), lambda b, g, j, s, p: (blk_of(b, g, j, s, p)[0], 0, 0)),
                      pl.BlockSpec((1, 1, 8 * dh), lambda b, g, j, s, p: (b, 0, 0)),
                      pl.BlockSpec((NSA_HEADS, 1), lambda b, g, j, s, p: (0, 0)),
                      pl.BlockSpec((1, NSA_HEADS, dh), lambda b, g, j, s, p: (b, 0, 0)),
                      pl.BlockSpec((1, NSA_HEADS, dh), lambda b, g, j, s, p: (b, 0, 0)),
                      pl.BlockSpec((1, NSA_HEADS, 3), lambda b, g, j, s, p: (b, 0, 0))],
            out_specs=pl.BlockSpec((1, NSA_HEADS, dh), lambda b, g, j, s, p: (b, 0, 0)),
            scratch_shapes=[pltpu.VMEM((NSA_HEADS, 1), f32), pltpu.VMEM((NSA_HEADS, 1), f32),
                            pltpu.VMEM((NSA_HEADS, dh), f32)]),
        out_shape=jax.ShapeDtypeStruct((db, NSA_HEADS, dh), f32),
        compiler_params=_cparams(("parallel", "arbitrary", "arbitrary")),
        name="nsa_sample_selected",
    )(sel_flat, page_table.reshape(-1), q3, cache_h, cache_h, t_sel, kv4_new.reshape(db, 1, 8 * dh), b0,
      o_c, o_w, gates)
    return o.reshape(db, NSA_HEADS * dh)


def _nsa_st_means_kernel(pt_ref, *refs):
    del pt_ref
    x_refs, o_ref = refs[:-1], refs[-1]
    p = pl.program_id(1)

    @pl.when(p == 0)
    def _():
        o_ref[...] = jnp.zeros_like(o_ref)

    page = x_refs[0].shape[-1]
    per_page = page // NSA_BLOCK
    lane = lax.broadcasted_iota(i32, (NSA_DH, page), 1)
    out_lane = lax.broadcasted_iota(i32, (NSA_DH, o_ref.shape[-1]), 1)
    for j in range(2):
        for g in range(NSA_KV_GROUPS):
            acc = o_ref[0, j * NSA_KV_GROUPS + g]
            for k, x_ref in enumerate(x_refs):
                x = x_ref[0, j, g]
                for n in range(per_page):
                    in_blk = (lane >= n * NSA_BLOCK) & (lane < (n + 1) * NSA_BLOCK)
                    mean = jnp.sum(jnp.where(in_blk, x, 0.0), axis=1, keepdims=True) * (1.0 / NSA_BLOCK)
                    col = (p * len(x_refs) + k) * per_page + n
                    acc = jnp.where(out_lane == col, mean, acc)
            o_ref[0, j * NSA_KV_GROUPS + g] = acc


def _nsa_st_means(cache_t, page_table):
    db, n_pages = page_table.shape
    page = cache_t.shape[-1]
    nblk = n_pages * page // NSA_BLOCK
    kp = MEANS_PAGES
    assert n_pages % kp == 0

    def page_spec(k):
        return pl.BlockSpec((1, 2, NSA_KV_GROUPS, NSA_DH, page),
                            lambda b, p, pt: (pt[b * n_pages + p * kp + k], 0, 0, 0, 0))

    return pl.pallas_call(
        _nsa_st_means_kernel,
        grid_spec=pltpu.PrefetchScalarGridSpec(
            num_scalar_prefetch=1,
            grid=(db, n_pages // kp),
            in_specs=[page_spec(k) for k in range(kp)],
            out_specs=pl.BlockSpec((1, 2 * NSA_KV_GROUPS, NSA_DH, nblk), lambda b, p, pt: (b, 0, 0, 0))),
        out_shape=jax.ShapeDtypeStruct((db, 2 * NSA_KV_GROUPS, NSA_DH, nblk), f32),
        compiler_params=_cparams(("parallel", "arbitrary")),
        name="nsa_sample_means",
    )(page_table.reshape(-1), *([cache_t] * kp))


def _nsa_st_scores_kernel(q_ref, kcvc_ref, win_ref, kvn_ref, tc_ref, tw_ref, b0_ref, oc_ref, ow_ref, sel_ref):
    dh, hpg, ng = NSA_DH, NSA_HPG, NSA_KV_GROUPS
    sb = q_ref.shape[0]
    nblk = kcvc_ref.shape[-1]
    row = lax.broadcasted_iota(i32, (NSA_HEADS, 1), 0)
    in_g = [(row >= g * hpg) & (row < (g + 1) * hpg) for g in range(ng)]
    pairs = [(b, g) for b in range(sb) for g in range(ng)]
    q = [q_ref[b] * (dh ** -0.5) for b in range(sb)]

    s_c = [_mm3(q[b], kcvc_ref[b, g]) + tc_ref[...] for b, g in pairs]
    p_c = []
    for s in s_c:
        p = jnp.exp(s - jnp.max(s, axis=-1, keepdims=True))
        p_c.append(p / jnp.maximum(jnp.sum(p, axis=-1, keepdims=True), 1e-30))
    o_c = [_mm_nt(p_c[k], kcvc_ref[b, ng + g]) for k, (b, g) in enumerate(pairs)]
    imp = jnp.concatenate([jnp.sum(jnp.where(in_g[g], p_c[k], 0.0), axis=0, keepdims=True)
                           for k, (b, g) in enumerate(pairs)], axis=0)
    for b in range(sb):
        oc_ref[b] = jnp.where(in_g[0], o_c[b * ng], o_c[b * ng + 1])

    s_w = [_mm3(q[b], win_ref[b, 0, g]) + tw_ref[...] for b, g in pairs]
    o_w = []
    for k, (b, g) in enumerate(pairs):
        kvn = kvn_ref[b]
        s_n = jnp.sum(q[b] * kvn[:, g * dh:(g + 1) * dh], axis=-1, keepdims=True) + b0_ref[...]
        m_w = jnp.maximum(jnp.max(s_w[k], axis=-1, keepdims=True), s_n)
        p_w = jnp.exp(s_w[k] - m_w)
        p_n = jnp.exp(s_n - m_w)
        l_w = jnp.sum(p_w, axis=-1, keepdims=True) + p_n
        v_n = kvn[:, (ng + g) * dh:(ng + g + 1) * dh]
        o_w.append((_mm_nt(p_w, win_ref[b, 1, g]) + p_n * v_n) / jnp.maximum(l_w, 1e-30))
    for b in range(sb):
        ow_ref[b] = jnp.where(in_g[0], o_w[b * ng], o_w[b * ng + 1])

    lane = lax.broadcasted_iota(i32, imp.shape, 1)
    n_sel = NSA_TOPN - 1
    forced_blocks = [0] + [nblk - k for k in range(1, NSA_LOCAL)]
    imp = jnp.where((lane == 0) | (lane > nblk - NSA_LOCAL), -jnp.inf, imp)
    picks = jnp.zeros(imp.shape, i32)
    for it, fb in enumerate(forced_blocks):
        picks = jnp.where(lane == it, fb, picks)
    for it in range(len(forced_blocks), n_sel):
        mx = jnp.max(imp, axis=-1, keepdims=True)
        first = jnp.min(jnp.where(imp == mx, lane, nblk), axis=-1, keepdims=True)
        picks = jnp.where(lane == it, first, picks)
        imp = jnp.where(lane == first, -jnp.inf, imp)
    for b in range(sb):
        sel_ref[b] = picks[b * ng:(b + 1) * ng]


def _nsa_st_sel_kernel(sel_ref, pt_ref, q_ref, rb_ref, kvn_ref, oc_ref, ow_ref, gt_ref, *refs, n_sel, past):
    del pt_ref
    k_refs, v_refs, o_ref = refs[:n_sel], refs[n_sel:2 * n_sel], refs[2 * n_sel]
    b = pl.program_id(0)
    g = pl.program_id(1)
    dh, hpg = NSA_DH, NSA_HPG
    is_g0 = g == 0
    page = k_refs[0].shape[-1]
    per_page = page // NSA_BLOCK
    q = q_ref[0] * (dh ** -0.5)
    kvn = kvn_ref[0]
    lane = lax.broadcasted_iota(i32, (NSA_HEADS, page), 1)
    s_parts, d_parts = [], []
    for j in range(n_sel):
        nb = sel_ref[(b * NSA_KV_GROUPS + g) * n_sel + j]
        in_blk = (lane >> int(math.log2(NSA_BLOCK))) == (nb % per_page)
        d_parts.append(jnp.where(in_blk, past - ((nb // per_page) * page + lane), -1))
        s_parts.append(_mm3(q, k_refs[j][0, 0, 0]))
    d = jnp.concatenate(d_parts, axis=1)
    bias = jnp.broadcast_to(rb_ref[0], d.shape)
    for k in range(1, REL_BUCKETS):
        bias = jnp.where(d >= _BUCKET_THR[k - 1], rb_ref[k], bias)
    s = jnp.concatenate(s_parts, axis=1) + jnp.where(d < 0, NEG, bias)
    k_n = jnp.where(is_g0, kvn[:, 4 * dh:5 * dh], kvn[:, 5 * dh:6 * dh])
    v_n = jnp.where(is_g0, kvn[:, 6 * dh:7 * dh], kvn[:, 7 * dh:8 * dh])
    s_n = jnp.sum(q * k_n, axis=-1, keepdims=True) + rb_ref[0]
    m = jnp.maximum(jnp.max(s, axis=-1, keepdims=True), s_n)
    p = jnp.exp(s - m)
    p_n = jnp.exp(s_n - m)
    l = jnp.sum(p, axis=-1, keepdims=True) + p_n
    acc = p_n * v_n
    for j in range(n_sel):
        acc = acc + _mm_nt(p[:, j * page:(j + 1) * page], v_refs[j][0, 0, 0])
    o_s = acc / jnp.maximum(l, 1e-30)
    gt = gt_ref[0]
    o = gt[:, 0:1] * oc_ref[0] + gt[:, 1:2] * o_s + gt[:, 2:3] * ow_ref[0]
    row = lax.broadcasted_iota(i32, o.shape, 0)
    in_g = (row >= g * hpg) & (row < (g + 1) * hpg)

    @pl.when(is_g0)
    def _():
        o_ref[0] = o

    @pl.when(jnp.logical_not(is_g0))
    def _():
        o_ref[0] = jnp.where(in_g, o, o_ref[0])


def _nsa_sample_t(q_b, gates, kv4_new, kvwin_new, cache_kv, cache_win, page_table, rel_bias):
    db = q_b.shape[0]
    dh = NSA_DH
    n_pool, page = cache_kv.shape[:2]
    n_pages = page_table.shape[1]
    past = n_pages * page
    nblk = past // NSA_BLOCK
    wb = cache_win.shape[1]
    assert nblk == PAIR and page % NSA_BLOCK == 0 and wb == NSA_WINDOW and nblk > NSA_LOCAL
    cache_t = jnp.transpose(cache_kv, (0, 2, 3, 4, 1)).astype(f32)
    win_t = jnp.transpose(cache_win, (0, 2, 3, 4, 1)).astype(f32)
    kcvc = _nsa_st_means(cache_t, page_table)

    n = np.arange(nblk)
    d_cmp = past - (n * NSA_BLOCK + NSA_BLOCK - 1)
    jw = np.arange(wb)
    d_win = np.where(jw >= 1, wb - jw, -1)
    dist = np.broadcast_to(np.concatenate([d_cmp, d_win])[None, :], (NSA_HEADS, nblk + wb))
    tab = _bias_table(rel_bias, jnp.asarray(dist[None, None], i32), slab=1)[0, 0]
    t_cmp, t_win = tab[:, :nblk], tab[:, nblk:]
    rb = rel_bias.astype(f32)
    b0 = rb[0].reshape(NSA_HEADS, 1)

    q3 = q_b.reshape(db, NSA_HEADS, dh)
    sb = max(s for s in (8, 4, 2, 1) if db % s == 0)
    o_c, o_w, sel = pl.pallas_call(
        _nsa_st_scores_kernel,
        grid=(db // sb,),
        in_specs=[pl.BlockSpec((sb, NSA_HEADS, dh), lambda b: (b, 0, 0)),
                  pl.BlockSpec((sb, 2 * NSA_KV_GROUPS, dh, nblk), lambda b: (b, 0, 0, 0)),
                  pl.BlockSpec((sb, 2, NSA_KV_GROUPS, dh, wb), lambda b: (b, 0, 0, 0, 0)),
                  pl.BlockSpec((sb, 1, 4 * dh), lambda b: (b, 0, 0)),
                  pl.BlockSpec((NSA_HEADS, nblk), lambda b: (0, 0)),
                  pl.BlockSpec((NSA_HEADS, wb), lambda b: (0, 0)),
                  pl.BlockSpec((NSA_HEADS, 1), lambda b: (0, 0))],
        out_specs=[pl.BlockSpec((sb, NSA_HEADS, dh), lambda b: (b, 0, 0)),
                   pl.BlockSpec((sb, NSA_HEADS, dh), lambda b: (b, 0, 0)),
                   pl.BlockSpec((sb, NSA_KV_GROUPS, nblk), lambda b: (b, 0, 0))],
        out_shape=[jax.ShapeDtypeStruct((db, NSA_HEADS, dh), f32),
                   jax.ShapeDtypeStruct((db, NSA_HEADS, dh), f32),
                   jax.ShapeDtypeStruct((db, NSA_KV_GROUPS, nblk), i32)],
        compiler_params=_cparams(("parallel",)),
        name="nsa_sample_scores",
    )(q3, kcvc, win_t, kvwin_new.reshape(db, 1, 4 * dh), t_cmp, t_win, b0)

    n_sel = NSA_TOPN - 1
    sel_flat = sel[:, :, :n_sel].reshape(-1)
    per_page = page // NSA_BLOCK

    def page_spec(j, plane):
        def imap(b, g, s, p):
            nb = s[(b * NSA_KV_GROUPS + g) * n_sel + j]
            return (p[b * n_pages + nb // per_page], plane, g, 0, 0)
        return pl.BlockSpec((1, 1, 1, dh, page), imap)

    const = lambda shape: pl.BlockSpec(shape, lambda b, g, s, p: (0,) * len(shape))
    per_b = lambda shape: pl.BlockSpec((1,) + shape, lambda b, g, s, p: (b,) + (0,) * len(shape))
    o = pl.pallas_call(
        functools.partial(_nsa_st_sel_kernel, n_sel=n_sel, past=past),
        grid_spec=pltpu.PrefetchScalarGridSpec(
            num_scalar_prefetch=2,
            grid=(db, NSA_KV_GROUPS),
            in_specs=[per_b((NSA_HEADS, dh)), const((REL_BUCKETS, NSA_HEADS, 1)), per_b((1, 8 * dh)),
                      per_b((NSA_HEADS, dh)), per_b((NSA_HEADS, dh)), per_b((NSA_HEADS, 3))]
                     + [page_spec(j, 2) for j in range(n_sel)] + [page_spec(j, 3) for j in range(n_sel)],
            out_specs=per_b((NSA_HEADS, dh))),
        out_shape=jax.ShapeDtypeStruct((db, NSA_HEADS, dh), f32),
        compiler_params=_cparams(("parallel", "arbitrary")),
        name="nsa_sample_selected",
    )(sel_flat, page_table.reshape(-1), q3, rb.reshape(REL_BUCKETS, NSA_HEADS, 1), kv4_new.reshape(db, 1, 8 * dh),
      o_c, o_w, gates, *([cache_t] * (2 * n_sel)))
    return o.reshape(db, NSA_HEADS * dh)


def _gelu_tanh(x):
    return 0.5 * x * (1.0 + jnp.tanh(math.sqrt(2.0 / math.pi) * (x + 0.044715 * (x * x * x))))


def _lru_gates(xc, wa_ref, wx_ref, ba_ref, bx_ref, lam_ref, prec):
    r_parts, i_parts = [], []
    for n in range(RNN_BLOCKS):
        xb = xc[:, n * RNN_BW:(n + 1) * RNN_BW]
        r_parts.append(_dotp(xb, wa_ref[n], prec))
        i_parts.append(_dotp(xb, wx_ref[n], prec))
    r = jax.nn.sigmoid(jnp.concatenate(r_parts, axis=1) + ba_ref[...])
    i = jax.nn.sigmoid(jnp.concatenate(i_parts, axis=1) + bx_ref[...])
    log_a = -RG_C * r * _softplus(-lam_ref[...])
    a = jnp.exp(log_a)
    t = jnp.tanh(log_a)
    b = jnp.sqrt(jnp.maximum(-2.0 * t / (1.0 - t), 0.0)) * (i * xc)
    return a, b


def _lru_kernel(rec_ref, gate_ref, cw_ref, cb_ref, wa_ref, wx_ref, ba_ref, bx_ref, lam_ref, cinit_ref, h0_ref,
                y_ref, hfin_ref, xc_scr, a_scr, b_scr, hs_scr, h_scr, *, tb):
    j = pl.program_id(1)

    @pl.when(j == 0)
    def _():
        xc_scr[0:8, :] = cinit_ref[0]
        h_scr[...] = h0_ref[0]

    x = rec_ref[0]
    xc_scr[8:8 + tb, :] = x
    xc = xc_scr[5:5 + tb, :] * cw_ref[0:1, :]
    xc = xc + xc_scr[6:6 + tb, :] * cw_ref[1:2, :]
    xc = xc + xc_scr[7:7 + tb, :] * cw_ref[2:3, :]
    xc = xc + x * cw_ref[3:4, :]
    xc = xc + cb_ref[...]
    xc_scr[0:8, :] = xc_scr[tb:tb + 8, :]
    a, b = _lru_gates(xc, wa_ref, wx_ref, ba_ref, bx_ref, lam_ref, 1)
    a_scr[...] = a
    b_scr[...] = b

    def step(t, h):
        h = a_scr[pl.ds(t, 1), :] * h + b_scr[pl.ds(t, 1), :]
        hs_scr[pl.ds(t, 1), :] = h
        return h

    h = lax.fori_loop(0, tb, step, h_scr[...], unroll=8)
    h_scr[...] = h
    y_ref[0] = _gelu_tanh(gate_ref[0]) * hs_scr[...]

    @pl.when(j == pl.num_programs(1) - 1)
    def _():
        hfin_ref[0] = h


def _lru_prompt(rec, gate, conv_w, conv_b, wa, wx, ba, bx, lam, *, tb):
    bsz, t, w = rec.shape
    row = lambda a: a.reshape(1, w).astype(f32)
    cinit = jnp.zeros((bsz, 8, w), f32)
    h0 = jnp.zeros((bsz, 1, w), f32)
    full = lambda shape: pl.BlockSpec(shape, lambda b, j: (0,) * len(shape))
    return pl.pallas_call(
        functools.partial(_lru_kernel, tb=tb),
        grid=(bsz, t // tb),
        in_specs=[pl.BlockSpec((1, tb, w), lambda b, j: (b, j, 0)),
                  pl.BlockSpec((1, tb, w), lambda b, j: (b, j, 0)),
                  full((RNN_CONV, w)), full((1, w)),
                  full((RNN_BLOCKS, RNN_BW, RNN_BW)), full((RNN_BLOCKS, RNN_BW, RNN_BW)),
                  full((1, w)), full((1, w)), full((1, w)),
                  pl.BlockSpec((1, 8, w), lambda b, j: (b, 0, 0)),
                  pl.BlockSpec((1, 1, w), lambda b, j: (b, 0, 0))],
        out_specs=[pl.BlockSpec((1, tb, w), lambda b, j: (b, j, 0)),
                   pl.BlockSpec((1, 1, w), lambda b, j: (b, 0, 0))],
        out_shape=[jax.ShapeDtypeStruct((bsz, t, w), f32), jax.ShapeDtypeStruct((bsz, 1, w), f32)],
        scratch_shapes=[pltpu.VMEM((tb + 8, w), f32), pltpu.VMEM((tb, w), f32), pltpu.VMEM((tb, w), f32),
                        pltpu.VMEM((tb, w), f32), pltpu.VMEM((1, w), f32)],
        compiler_params=_cparams(("parallel", "arbitrary")),
        name="rglru",
    )(rec, gate, conv_w.astype(f32), row(conv_b), wa.astype(f32), wx.astype(f32), row(ba), row(bx), row(lam),
      cinit, h0)


def _lru_step_kernel(rec_ref, gate_ref, b0_ref, b1_ref, b2_ref, cw_ref, cb_ref, wa_ref, wx_ref, ba_ref, bx_ref,
                     lam_ref, h0_ref, y_ref, h_ref):
    xc = b0_ref[...] * cw_ref[0:1, :]
    xc = xc + b1_ref[...] * cw_ref[1:2, :]
    xc = xc + b2_ref[...] * cw_ref[2:3, :]
    xc = xc + rec_ref[...] * cw_ref[3:4, :]
    xc = xc + cb_ref[...]
    a, b = _lru_gates(xc, wa_ref, wx_ref, ba_ref, bx_ref, lam_ref, 3)
    h = a * h0_ref[...] + b
    h_ref[...] = h
    y_ref[...] = _gelu_tanh(gate_ref[...]) * h


def _lru_sample(rec, gate, conv_buf, h0, conv_w, conv_b, wa, wx, ba, bx, lam):
    db, w = rec.shape
    row = lambda a: a.reshape(1, w).astype(f32)
    buf = conv_buf.astype(f32)
    return pl.pallas_call(
        _lru_step_kernel,
        out_shape=[jax.ShapeDtypeStruct((db, w), f32), jax.ShapeDtypeStruct((db, w), f32)],
        compiler_params=pltpu.CompilerParams(vmem_limit_bytes=VMEM_LIMIT),
        name="rglru_step",
    )(rec, gate, buf[:, 0], buf[:, 1], buf[:, 2], conv_w.astype(f32), row(conv_b), wa.astype(f32), wx.astype(f32),
      row(ba), row(bx), row(lam), h0.astype(f32))


def _top2_kernel(l_ref, e_ref, g_ref, cnt_ref, cnt_scr):
    @pl.when(pl.program_id(0) == 0)
    def _():
        cnt_scr[...] = jnp.zeros_like(cnt_scr)

    lg = l_ref[...]
    tm, w = lg.shape
    lane = lax.broadcasted_iota(i32, lg.shape, 1)
    lg = jnp.where(lane < N_EXPERTS, lg, -jnp.inf)
    m1 = jnp.max(lg, axis=-1, keepdims=True)
    i1 = jnp.min(jnp.where(lg == m1, lane, w), axis=-1, keepdims=True)
    lg2 = jnp.where(lane == i1, -jnp.inf, lg)
    m2 = jnp.max(lg2, axis=-1, keepdims=True)
    i2 = jnp.min(jnp.where(lg2 == m2, lane, w), axis=-1, keepdims=True)
    e2 = jnp.exp(m2 - m1)
    den = 1.0 + e2
    g_ref[...] = jnp.where(lane == 0, 1.0 / den, jnp.where(lane == 1, e2 / den, 0.0))
    hit1, hit2 = lane == i1, lane == i2
    routed = (hit1 | hit2).astype(bf16)
    r_i = lax.broadcasted_iota(i32, (tm, tm), 0)
    c_i = lax.broadcasted_iota(i32, (tm, tm), 1)
    before = jnp.dot((c_i < r_i).astype(bf16), routed, preferred_element_type=f32) + cnt_scr[...]
    rank1 = jnp.sum(jnp.where(hit1, before, 0.0), axis=-1, keepdims=True).astype(i32)
    rank2 = jnp.sum(jnp.where(hit2, before, 0.0), axis=-1, keepdims=True).astype(i32)
    e_ref[...] = jnp.where(lane == 0, i1, jnp.where(lane == 1, i2, jnp.where(lane == 2, rank1,
                                                                               jnp.where(lane == 3, rank2, 0))))
    cnt_scr[...] += jnp.sum(routed.astype(f32), axis=0, keepdims=True)
    cnt_ref[...] = cnt_scr[...].astype(i32)


def _top2(logits, *, tm):
    n, w = logits.shape
    return pl.pallas_call(
        _top2_kernel,
        grid=(n // tm,),
        in_specs=[pl.BlockSpec((tm, w), lambda i: (i, 0))],
        out_specs=[pl.BlockSpec((tm, w), lambda i: (i, 0)), pl.BlockSpec((tm, w), lambda i: (i, 0)),
                   pl.BlockSpec((1, w), lambda i: (0, 0))],
        out_shape=[jax.ShapeDtypeStruct((n, w), i32), jax.ShapeDtypeStruct((n, w), f32),
                   jax.ShapeDtypeStruct((1, w), i32)],
        scratch_shapes=[pltpu.VMEM((1, w), f32)],
        compiler_params=_cparams(("arbitrary",)),
        name="moe_top2",
    )(logits)


def _moe_kernel(be_ref, nu_ref, xs_ref, w1_ref, w3_ref, w2_ref, o_ref, acc_scr):
    del be_ref
    i = pl.program_id(0)
    f = pl.program_id(1)
    last = pl.num_programs(1) - 1
    used = i < nu_ref[0]

    @pl.when(used)
    def _():
        @pl.when(f == 0)
        def _():
            acc_scr[...] = jnp.zeros_like(acc_scr)

        x = xs_ref[...]
        act = _silu(_mm(x, w1_ref[0])) * _mm(x, w3_ref[0])
        acc_scr[...] += _mm(act, w2_ref[0])

        @pl.when(f == last)
        def _():
            o_ref[...] = acc_scr[...]

    @pl.when(jnp.logical_not(used) & (f == last))
    def _():
        o_ref[...] = jnp.zeros_like(o_ref)


def _moe_experts(xs, blk_e, n_used, w1, w3, w2, *, tm, tf):
    rows, d = xs.shape
    fdim = w1.shape[2]
    nf = fdim // tf

    def wcol(i, f, be, nu):
        return (be[i], 0, jnp.where(i < nu[0], f, nf - 1))

    def wrow(i, f, be, nu):
        return (be[i], jnp.where(i < nu[0], f, nf - 1), 0)

    return pl.pallas_call(
        _moe_kernel,
        grid_spec=pltpu.PrefetchScalarGridSpec(
            num_scalar_prefetch=2,
            grid=(rows // tm, nf),
            in_specs=[pl.BlockSpec((tm, d), lambda i, f, be, nu: (i, 0)),
                      pl.BlockSpec((1, d, tf), wcol),
                      pl.BlockSpec((1, d, tf), wcol),
                      pl.BlockSpec((1, tf, d), wrow)],
            out_specs=pl.BlockSpec((tm, d), lambda i, f, be, nu: (i, 0)),
            scratch_shapes=[pltpu.VMEM((tm, d), f32)]),
        out_shape=jax.ShapeDtypeStruct((rows, d), f32),
        compiler_params=_cparams(("arbitrary", "arbitrary")),
        name="moe_experts",
    )(blk_e, n_used, xs, w1, w3, w2)


def _combine_kernel(x_ref, gate_ref, y0_ref, y1_ref, g_ref, nw_ref, o_ref):
    gw = g_ref[...]
    y = y0_ref[...] * gw[:, 0:1] + y1_ref[...] * gw[:, 1:2]
    x = x_ref[0] + gate_ref[0] * y
    o_ref[0] = x * lax.rsqrt(jnp.mean(x * x, axis=-1, keepdims=True) + EPS) * nw_ref[...]


def _moe_combine(x, gate, y0, y1, gw, norm_w, *, tm, row0):
    bsz, t, d = x.shape
    assert row0 % tm == 0
    tok = pl.BlockSpec((1, tm, d), lambda b, i: (b, i, 0))
    flat = lambda w: pl.BlockSpec((tm, w), lambda b, i: (row0 // tm + b * (t // tm) + i, 0))
    return pl.pallas_call(
        _combine_kernel,
        grid=(bsz, t // tm),
        in_specs=[tok, _mod_spec(gate, tm), flat(d), flat(d), flat(gw.shape[-1]),
                  pl.BlockSpec((1, d), lambda b, i: (0, 0))],
        out_specs=tok,
        out_shape=jax.ShapeDtypeStruct((bsz, t, d), f32),
        compiler_params=_cparams(("parallel", "parallel")),
        name="moe_combine",
    )(x, gate, y0, y1, gw, norm_w.reshape(1, d).astype(f32))


MOE_TM = 1024
MOE_TF = 512


def _moe_dispatch(e_idx, rank, counts, n_tok):
    tm = MOE_TM
    n_assign = n_tok * TOP_K
    padded = (counts + tm - 1) // tm * tm
    pend = jnp.cumsum(padded)
    pstart = pend - padded
    experts = jnp.arange(N_EXPERTS, dtype=i32)
    dest = jnp.sum(jnp.where(e_idx[:, :, None] == experts, pstart, 0), axis=-1) + rank
    n_blocks = -(-n_assign // tm) + N_EXPERTS
    rows = n_blocks * tm
    tok = jnp.broadcast_to(jnp.arange(n_tok, dtype=i32)[:, None], (n_tok, TOP_K))
    row_tok = (jnp.arange(rows, dtype=i32) % n_tok).at[dest.reshape(-1)].set(
        tok.reshape(-1), unique_indices=True, mode='promise_in_bounds')
    n_used = (pend[-1] // tm).astype(i32)
    blk = jnp.minimum(jnp.arange(n_blocks, dtype=i32), n_used - 1) * tm
    blk_e = jnp.minimum(jnp.sum((blk[:, None] >= pend[None, :]).astype(i32), axis=1), N_EXPERTS - 1)
    return dest, row_tok, blk_e, n_used.reshape(1)


def _w_in0_layout(w_in0):
    c_qkv = GDN_CONV_CH
    c_z = c_qkv + GDN_HEADS * GDN_DV
    c_ab = c_z + 2 * GDN_HEADS
    c_q = c_ab + NSA_HEADS * NSA_DH
    c_kv = c_q + 6 * NSA_KV_GROUPS * NSA_DH
    c_g = c_kv + 3 * NSA_HEADS
    assert c_g == w_in0.shape[1]
    n_misc = 2 * GDN_HEADS + 3 * NSA_HEADS
    w = jnp.concatenate([w_in0[:, :c_z], w_in0[:, c_ab:c_kv], w_in0[:, c_z:c_ab], w_in0[:, c_kv:c_g],
                         jnp.zeros((w_in0.shape[0], MISC_W - n_misc), w_in0.dtype)], axis=1)
    widths = (GDN_CONV_CH, GDN_HEADS * GDN_DV, NSA_HEADS * NSA_DH, 4 * NSA_KV_GROUPS * NSA_DH,
              2 * NSA_KV_GROUPS * NSA_DH, MISC_W)
    splits, s = [], 0
    for wd in widths:
        splits.append((s, s + wd))
        s += wd
    return w, tuple(splits)


def kernel(x_prompt, x_sample, c_prompt, c_sample, cache_nsa_kv, cache_nsa_win, state_gdn, state_gdn_conv, state_lru, state_lru_conv, page_table, rel_bias, w_ada, b_ada, norm_mix, norm_ffn, norm_final, w_in0, gdn_conv_w, gdn_a_log, gdn_dt_bias, gdn_norm_w, w_out0, ffn_w_gate, ffn_w_up, ffn_w_down, w_in1, lru_conv_w, lru_conv_b, lru_wa, lru_ba, lru_wx, lru_bx, lru_lambda, w_out1, moe_router, moe_w1, moe_w3, moe_w2):
    bsz, seq, d = x_prompt.shape
    db = x_sample.shape[0]
    assert x_sample.shape[1] == 1
    dh = NSA_DH

    n_c = bsz + db
    n_c_pad = -(-n_c // 8) * 8
    c_all = jnp.concatenate([c_prompt, c_sample, jnp.zeros((n_c_pad - n_c, d), f32)], axis=0)
    mods = _adaln(c_all, w_ada, b_ada).reshape(2, n_c_pad, N_MOD, d)
    mod_p = [[mods[l, :bsz, k].reshape(bsz, 1, d) for k in range(N_MOD)] for l in range(2)]
    mod_s = [[mods[l, bsz:n_c, k].reshape(1, db, d) for k in range(N_MOD)] for l in range(2)]

    w0, splits0 = _w_in0_layout(w_in0)
    splits1 = ((0, RNN_WIDTH), (RNN_WIDTH, 2 * RNN_WIDTH))
    router = jnp.concatenate([moe_router, jnp.zeros((d, MISC_W - N_EXPERTS), f32)], axis=1)
    bf = lambda w: w.astype(bf16)

    tm = 512
    xp = x_prompt
    qkv, z, q_b, kv4, kvwin, misc = _mod_matmul(xp, norm_mix[0], mod_p[0][0], mod_p[0][1], bf(w0), splits0,
                                                tm=tm, prec=1)
    o_a, p_gdn = _gdn(qkv, z, misc, gdn_conv_w, jnp.zeros((bsz, GDN_CONV - 1, GDN_CONV_CH), f32), gdn_a_log,
                      gdn_dt_bias, gdn_norm_w, jnp.zeros((bsz, GDN_HEADS, GDN_DK, GDN_DV), f32), tb=256, n_valid=seq)
    p_gdn_conv = qkv[:, seq - (GDN_CONV - 1):]
    o_b = _nsa_prompt(q_b, misc, kv4, kvwin, rel_bias)
    p_nsa_kv = kv4.reshape(bsz, seq, 4, NSA_KV_GROUPS, dh)
    keep = min(NSA_WINDOW, seq)
    p_nsa_win = kvwin[:, seq - keep:].reshape(bsz, keep, 2, NSA_KV_GROUPS, dh)
    xp = _proj_residual([o_a, o_b], bf(w_out0), xp, mod_p[0][2], tm=tm, prec=1)
    xp = _ffn(xp, norm_ffn[0], mod_p[0][3], mod_p[0][4], mod_p[0][5], bf(ffn_w_gate), bf(ffn_w_up), bf(ffn_w_down),
              tm=tm, tf=FFN_DIM // 2, prec=1)
    gate_br, rec_br = _mod_matmul(xp, norm_mix[1], mod_p[1][0], mod_p[1][1], bf(w_in1), splits1, tm=tm, prec=1)
    y_in, p_lru = _lru_prompt(rec_br, gate_br, lru_conv_w, lru_conv_b, lru_wa, lru_wx, lru_ba, lru_bx, lru_lambda,
                              tb=256)
    p_lru_conv = rec_br[:, seq - (RNN_CONV - 1):]
    xp = _proj_residual([y_in], bf(w_out1), xp, mod_p[1][2], tm=tm, prec=1)
    logit_p, h_p = _mod_matmul(xp, norm_ffn[1], mod_p[1][3], mod_p[1][4], router, ((0, MISC_W),), tm=tm, prec=3,
                               emit_h=f32)

    xs = x_sample.reshape(1, db, d)
    qkv_s, z_s, q_s, kv4_s, kvwin_s, misc_s = _mod_matmul(xs, norm_mix[0], mod_s[0][0], mod_s[0][1], w0, splits0,
                                                          tm=db, prec=3)
    c = GDN_CHUNK
    tpad = lambda a: jnp.pad(a.reshape(db, 1, a.shape[-1]), ((0, 0), (0, c - 1), (0, 0)))
    o_a_s, s_gdn = _gdn(tpad(qkv_s), tpad(z_s), tpad(misc_s), gdn_conv_w, state_gdn_conv, gdn_a_log, gdn_dt_bias,
                        gdn_norm_w, state_gdn, tb=c, n_valid=1)
    o_a_s = o_a_s[:, 0].reshape(1, db, GDN_HEADS * GDN_DV)
    s_gdn_conv = jnp.concatenate([state_gdn_conv[:, 1:], qkv_s.reshape(db, 1, GDN_CONV_CH)], axis=1)
    gates_s = jax.nn.sigmoid(misc_s[0, :, 2 * GDN_HEADS:2 * GDN_HEADS + 3 * NSA_HEADS]).reshape(db, NSA_HEADS, 3)
    o_b_s = _nsa_sample_t(q_s[0], gates_s, kv4_s[0], kvwin_s[0], cache_nsa_kv, cache_nsa_win, page_table, rel_bias)
    s_nsa_kv = kv4_s.reshape(db, 1, 4, NSA_KV_GROUPS, dh)
    s_nsa_win = jnp.concatenate([cache_nsa_win[:, 1:],
                                 kvwin_s.reshape(db, 1, 2, NSA_KV_GROUPS, dh).astype(cache_nsa_win.dtype)], axis=1)
    xs = _proj_residual([o_a_s, o_b_s.reshape(1, db, NSA_HEADS * dh)], w_out0, xs, mod_s[0][2], tm=db, prec=3)
    xs = _ffn(xs, norm_ffn[0], mod_s[0][3], mod_s[0][4], mod_s[0][5], ffn_w_gate, ffn_w_up, ffn_w_down,
              tm=db, tf=256, prec=3)
    gate_s, rec_s = _mod_matmul(xs, norm_mix[1], mod_s[1][0], mod_s[1][1], w_in1, splits1, tm=db, prec=3)
    y_in_s, s_lru = _lru_sample(rec_s[0], gate_s[0], state_lru_conv, state_lru, lru_conv_w, lru_conv_b, lru_wa, lru_wx,
                                lru_ba, lru_bx, lru_lambda)
    s_lru_conv = jnp.concatenate([state_lru_conv[:, 1:], rec_s.reshape(db, 1, RNN_WIDTH)], axis=1)
    xs = _proj_residual([y_in_s.reshape(1, db, RNN_WIDTH)], w_out1, xs, mod_s[1][2], tm=db, prec=3)
    logit_s, h_s = _mod_matmul(xs, norm_ffn[1], mod_s[1][3], mod_s[1][4], router, ((0, MISC_W),), tm=db, prec=3,
                               emit_h=f32)

    n_p = bsz * seq
    n_tok = n_p + db
    logits = jnp.concatenate([logit_p.reshape(n_p, MISC_W), logit_s.reshape(db, MISC_W)], axis=0)
    h_all = jnp.concatenate([h_p.reshape(n_p, d), h_s.reshape(db, d)], axis=0)
    route, gw, cnt = _top2(logits, tm=max(t for t in range(8, 1025, 8) if n_tok % t == 0))
    dest, row_tok, blk_e, n_used = _moe_dispatch(route[:, :TOP_K], route[:, TOP_K:2 * TOP_K], cnt[0, :N_EXPERTS],
                                                 n_tok)
    yb = _moe_experts(h_all[row_tok], blk_e, n_used, moe_w1, moe_w3, moe_w2, tm=MOE_TM, tf=MOE_TF)
    y0, y1 = yb[dest[:, 0]], yb[dest[:, 1]]
    y_prompt = _moe_combine(xp, mod_p[1][5], y0, y1, gw, norm_final, tm=tm, row0=0)
    y_sample = _moe_combine(xs, mod_s[1][5], y0, y1, gw, norm_final, tm=db, row0=n_p)

    return (y_prompt, y_sample.reshape(db, 1, d),
            p_nsa_kv, p_nsa_win, p_gdn, p_gdn_conv, p_lru.reshape(bsz, RNN_WIDTH), p_lru_conv,
            s_nsa_kv, s_nsa_win, s_gdn, s_gdn_conv, s_lru, s_lru_conv)
```

```python
import functools
import math

import numpy as np
import jax
import jax.numpy as jnp
from jax import lax
from jax.experimental import pallas as pl
from jax.experimental.pallas import tpu as pltpu

f32 = jnp.float32
bf16 = jnp.bfloat16
i32 = jnp.int32

D_MODEL = 1024
EPS = 1e-6
N_MOD = 6
GDN_HEADS = 4
GDN_DK = 128
GDN_DV = 128
GDN_CONV = 4
GDN_CHUNK = 64
GDN_CONV_CH = GDN_HEADS * (2 * GDN_DK + GDN_DV)
NSA_HEADS = 8
NSA_KV_GROUPS = 2
NSA_HPG = NSA_HEADS // NSA_KV_GROUPS
NSA_DH = 64
NSA_BLOCK = 64
NSA_TOPN = 16
NSA_LOCAL = 2
NSA_WINDOW = 512
NSA_QBLOCK = 128
NSA_FORCE = 1.0e4
REL_BUCKETS = 32
REL_MAX_DIST = 2048
RNN_WIDTH = D_MODEL
RNN_BLOCKS = 8
RNN_BW = RNN_WIDTH // RNN_BLOCKS
RNN_CONV = 4
RG_C = 8.0
FFN_DIM = 2816
N_EXPERTS = 8
TOP_K = 2
EXPERT_DIM = 3584

NEG = -1e30
PAIR = 2 * NSA_BLOCK
NEAR_PAIRS = 13
MISC_W = 128
VMEM_LIMIT = 56 * 1024 * 1024


def _cparams(sem, vmem=VMEM_LIMIT):
    return pltpu.CompilerParams(dimension_semantics=sem, vmem_limit_bytes=vmem)


def _mm(a, b):
    return jnp.dot(a.astype(bf16), b.astype(bf16), preferred_element_type=f32)


def _mm_nt(a, b):
    return lax.dot_general(a.astype(bf16), b.astype(bf16), (((1,), (1,)), ((), ())),
                           preferred_element_type=f32)


def _split2(a):
    hi = a.astype(bf16)
    lo = (a - hi.astype(f32)).astype(bf16)
    return hi, lo


def _mm3(a, b):
    ah, al = _split2(a)
    bh, bl = _split2(b)
    return (jnp.dot(ah, bh, preferred_element_type=f32) + jnp.dot(ah, bl, preferred_element_type=f32)
            + jnp.dot(al, bh, preferred_element_type=f32))


def _mm3_nt(a, b):
    ah, al = _split2(a)
    bh, bl = _split2(b)
    dn = (((1,), (1,)), ((), ()))
    return (lax.dot_general(ah, bh, dn, preferred_element_type=f32)
            + lax.dot_general(ah, bl, dn, preferred_element_type=f32)
            + lax.dot_general(al, bh, dn, preferred_element_type=f32))


def _mm_01(m01, a):
    hi = a.astype(bf16)
    r1 = a - hi.astype(f32)
    mid = r1.astype(bf16)
    lo = (r1 - mid.astype(f32)).astype(bf16)
    return (jnp.dot(m01, hi, preferred_element_type=f32) + jnp.dot(m01, mid, preferred_element_type=f32)
            + jnp.dot(m01, lo, preferred_element_type=f32))


def _dotp(a, b, prec):
    return _mm3(a, b) if prec == 3 else _mm(a, b)


def _silu(x):
    return x * jax.nn.sigmoid(x)


def _softplus(x):
    return jnp.maximum(x, 0.0) + jnp.log1p(jnp.exp(-jnp.abs(x)))


def _modulate(x, gain, shift, scale):
    r = lax.rsqrt(jnp.mean(x * x, axis=-1, keepdims=True) + EPS)
    return x * r * gain * (1.0 + scale) + shift


def _ada_kernel(c_ref, w_ref, b_ref, o_ref):
    o_ref[0] = _mm3(_silu(c_ref[...]), w_ref[0]) + b_ref[0]


def _adaln(c_all, w_ada, b_ada):
    rows = c_all.shape[0]
    depth, d, n = w_ada.shape
    tn = 1536
    return pl.pallas_call(
        _ada_kernel,
        grid=(depth, n // tn),
        in_specs=[pl.BlockSpec((rows, d), lambda l, j: (0, 0)),
                  pl.BlockSpec((1, d, tn), lambda l, j: (l, 0, j)),
                  pl.BlockSpec((1, 1, tn), lambda l, j: (l, 0, j))],
        out_specs=pl.BlockSpec((1, rows, tn), lambda l, j: (l, 0, j)),
        out_shape=jax.ShapeDtypeStruct((depth, rows, n), f32),
        compiler_params=_cparams(("arbitrary", "arbitrary")),
        name="adaln",
    )(c_all, w_ada, b_ada.reshape(depth, 1, n))


def _mod_spec(mod, tm):
    r = mod.shape[1]
    if r == 1:
        return pl.BlockSpec((1, 1, mod.shape[2]), lambda b, i: (b, 0, 0))
    return pl.BlockSpec((1, tm, mod.shape[2]), lambda b, i: (b, i, 0))


def _modmm_kernel(x_ref, gain_ref, shift_ref, scale_ref, w_ref, *o_refs, splits, prec, emit_h):
    h = _modulate(x_ref[0], gain_ref[...], shift_ref[0], scale_ref[0])
    if emit_h:
        o_refs[-1][0] = h.astype(o_refs[-1].dtype)
    hh = _split2(h) if prec == 3 else h.astype(bf16)
    for o_ref, (a, b) in zip(o_refs, splits):
        w = w_ref[:, a:b]
        if prec == 3:
            wh, wl = _split2(w)
            acc = (jnp.dot(hh[0], wh, preferred_element_type=f32) + jnp.dot(hh[0], wl, preferred_element_type=f32)
                   + jnp.dot(hh[1], wh, preferred_element_type=f32))
        else:
            acc = jnp.dot(hh, w, preferred_element_type=f32)
        o_ref[0] = acc


def _mod_matmul(x, gain, shift, scale, w, splits, *, tm, prec, emit_h=None):
    bsz, t, d = x.shape
    out_shape = [jax.ShapeDtypeStruct((bsz, t, b - a), f32) for a, b in splits]
    out_specs = [pl.BlockSpec((1, tm, b - a), lambda bi, i: (bi, i, 0)) for a, b in splits]
    if emit_h is not None:
        out_shape.append(jax.ShapeDtypeStruct((bsz, t, d), emit_h))
        out_specs.append(pl.BlockSpec((1, tm, d), lambda bi, i: (bi, i, 0)))
    return pl.pallas_call(
        functools.partial(_modmm_kernel, splits=tuple(splits), prec=prec, emit_h=emit_h is not None),
        grid=(bsz, t // tm),
        in_specs=[pl.BlockSpec((1, tm, d), lambda bi, i: (bi, i, 0)),
                  pl.BlockSpec((1, d), lambda bi, i: (0, 0)),
                  _mod_spec(shift, tm), _mod_spec(scale, tm),
                  pl.BlockSpec(w.shape, lambda bi, i: (0, 0))],
        out_specs=out_specs,
        out_shape=out_shape,
        compiler_params=_cparams(("parallel", "parallel")),
        name="mod_matmul",
    )(x, gain.reshape(1, d), shift, scale, w)


def _projres_kernel(*refs, n_lhs, ksplits, prec, final_norm):
    lhs = refs[:n_lhs]
    w_ref, x_ref, gate_ref = refs[n_lhs:n_lhs + 3]
    o_ref = refs[-1]
    acc = None
    for l_ref, (a, b) in zip(lhs, ksplits):
        part = _dotp(l_ref[0], w_ref[a:b, :], prec)
        acc = part if acc is None else acc + part
    y = x_ref[0] + gate_ref[0] * acc
    if final_norm:
        nw_ref = refs[n_lhs + 3]
        y = y * lax.rsqrt(jnp.mean(y * y, axis=-1, keepdims=True) + EPS) * nw_ref[...]
    o_ref[0] = y


def _proj_residual(lhs_list, w, x, gate, *, tm, prec, norm_w=None):
    bsz, t, d = x.shape
    ksplits, k0 = [], 0
    for l in lhs_list:
        ksplits.append((k0, k0 + l.shape[-1]))
        k0 += l.shape[-1]
    in_specs = [pl.BlockSpec((1, tm, l.shape[-1]), lambda bi, i: (bi, i, 0)) for l in lhs_list]
    in_specs += [pl.BlockSpec(w.shape, lambda bi, i: (0, 0)),
                 pl.BlockSpec((1, tm, d), lambda bi, i: (bi, i, 0)),
                 _mod_spec(gate, tm)]
    args = list(lhs_list) + [w, x, gate]
    if norm_w is not None:
        in_specs.append(pl.BlockSpec((1, d), lambda bi, i: (0, 0)))
        args.append(norm_w.reshape(1, d))
    return pl.pallas_call(
        functools.partial(_projres_kernel, n_lhs=len(lhs_list), ksplits=tuple(ksplits), prec=prec,
                          final_norm=norm_w is not None),
        grid=(bsz, t // tm),
        in_specs=in_specs,
        out_specs=pl.BlockSpec((1, tm, d), lambda bi, i: (bi, i, 0)),
        out_shape=jax.ShapeDtypeStruct((bsz, t, d), f32),
        compiler_params=_cparams(("parallel", "parallel")),
        name="proj_residual",
    )(*args)


def _ffn_kernel(x_ref, gain_ref, shift_ref, scale_ref, gate_ref, wg_ref, wu_ref, wd_ref, o_ref,
                h_scr, acc_scr, *, prec):
    f = pl.program_id(2)

    @pl.when(f == 0)
    def _():
        h_scr[...] = _modulate(x_ref[0], gain_ref[...], shift_ref[0], scale_ref[0]).astype(h_scr.dtype)
        acc_scr[...] = jnp.zeros_like(acc_scr)

    h = h_scr[...]
    act = _silu(_dotp(h, wg_ref[...], prec)) * _dotp(h, wu_ref[...], prec)
    acc_scr[...] += _dotp(act, wd_ref[...], prec)

    @pl.when(f == pl.num_programs(2) - 1)
    def _():
        o_ref[0] = x_ref[0] + gate_ref[0] * acc_scr[...]


def _ffn(x, gain, shift, scale, gate, wg, wu, wd, *, tm, tf, prec):
    bsz, t, d = x.shape
    fdim = wg.shape[1]
    mod_specs = []
    for mod in (shift, scale, gate):
        if mod.shape[1] == 1:
            mod_specs.append(pl.BlockSpec((1, 1, d), lambda b, i, f: (b, 0, 0)))
        else:
            mod_specs.append(pl.BlockSpec((1, tm, d), lambda b, i, f: (b, i, 0)))
    return pl.pallas_call(
        functools.partial(_ffn_kernel, prec=prec),
        grid=(bsz, t // tm, fdim // tf),
        in_specs=[pl.BlockSpec((1, tm, d), lambda b, i, f: (b, i, 0)),
                  pl.BlockSpec((1, d), lambda b, i, f: (0, 0)),
                  mod_specs[0], mod_specs[1], mod_specs[2],
                  pl.BlockSpec((d, tf), lambda b, i, f: (0, f)),
                  pl.BlockSpec((d, tf), lambda b, i, f: (0, f)),
                  pl.BlockSpec((tf, d), lambda b, i, f: (f, 0))],
        out_specs=pl.BlockSpec((1, tm, d), lambda b, i, f: (b, i, 0)),
        out_shape=jax.ShapeDtypeStruct((bsz, t, d), f32),
        scratch_shapes=[pltpu.VMEM((tm, d), f32 if prec == 3 else bf16), pltpu.VMEM((tm, d), f32)],
        compiler_params=_cparams(("parallel", "parallel", "arbitrary")),
        name="ffn",
    )(x, gain.reshape(1, d), shift, scale, gate, wg, wu, wd)


def _gdn_kernel(qkv_ref, z_ref, misc_ref, cw_ref, cinit_ref, hp_ref, nw_ref, s0_ref,
                o_ref, sfin_ref, xc_scr, s_scr, *, tb, n_valid):
    j = pl.program_id(1)
    c = GDN_CHUNK
    nh = GDN_HEADS

    @pl.when(j == 0)
    def _():
        xc_scr[0:8, :] = cinit_ref[0]
        s_scr[...] = s0_ref[0]

    x = qkv_ref[0]
    xc_scr[8:8 + tb, :] = x
    y = xc_scr[5:5 + tb, :] * cw_ref[0:1, :]
    y = y + xc_scr[6:6 + tb, :] * cw_ref[1:2, :]
    y = y + xc_scr[7:7 + tb, :] * cw_ref[2:3, :]
    y = y + x * cw_ref[3:4, :]
    xc_scr[0:8, :] = xc_scr[tb:tb + 8, :]
    y = _silu(y)

    misc = misc_ref[0]
    row = lax.broadcasted_iota(i32, (tb, MISC_W), 0) + j * tb
    live = row < n_valid
    log_a = jnp.where(live, hp_ref[0:1, :] * _softplus(misc + hp_ref[1:2, :]), 0.0)
    beta = jnp.where(live, jax.nn.sigmoid(misc), 0.0)

    r_i = lax.broadcasted_iota(i32, (tb, tb), 0)
    c_i = lax.broadcasted_iota(i32, (tb, tb), 1)
    sh = int(math.log2(c))
    ltri = (((r_i >> sh) == (c_i >> sh)) & (c_i <= r_i)).astype(bf16)
    g = _mm_01(ltri, log_a)
    g_t = g.T

    ri = lax.broadcasted_iota(i32, (c, c), 0)
    ci = lax.broadcasted_iota(i32, (c, c), 1)
    incl = ci <= ri
    strict = ci < ri
    eye = (ci == ri).astype(f32)
    quad = []
    for lvl in range(int(math.log2(c))):
        quad.append(((ri >> (lvl + 1)) == (ci >> (lvl + 1))) & (((ri >> lvl) & 1) == 1) & (((ci >> lvl) & 1) == 0))

    nchunk = tb // c
    units = [(n, h) for n in range(nchunk) for h in range(nh)]

    def stack(fn):
        return jnp.stack([fn(n * c, h) for n, h in units], axis=0)

    def bmm(eq, a_, b_):
        return jnp.einsum(eq, a_.astype(bf16), b_.astype(bf16), preferred_element_type=f32)

    q = stack(lambda r0, h: y[r0:r0 + c, h * GDN_DK:(h + 1) * GDN_DK])
    k = stack(lambda r0, h: y[r0:r0 + c, (nh + h) * GDN_DK:(nh + h + 1) * GDN_DK])
    v = stack(lambda r0, h: y[r0:r0 + c, 2 * nh * GDN_DK + h * GDN_DV:2 * nh * GDN_DK + (h + 1) * GDN_DV])
    q = q * lax.rsqrt(jnp.sum(q * q, axis=-1, keepdims=True) + EPS) * (GDN_DK ** -0.5)
    k = k * lax.rsqrt(jnp.sum(k * k, axis=-1, keepdims=True) + EPS)
    g_col = stack(lambda r0, h: g[r0:r0 + c, h:h + 1])
    g_row = stack(lambda r0, h: g_t[h:h + 1, r0:r0 + c])
    b_col = stack(lambda r0, h: beta[r0:r0 + c, nh + h:nh + h + 1])
    gam = jnp.where(incl, jnp.exp(jnp.where(incl, g_col - g_row, 0.0)), 0.0)
    kk = bmm('uid,ujd->uij', k, k)
    a = jnp.where(strict, b_col * gam * kk, 0.0)
    p = eye - jnp.where(quad[0], a, 0.0)
    for lvl in range(1, len(quad)):
        m = bmm('uij,ujk->uik', jnp.where(quad[lvl], a, 0.0), p)
        p = p - bmm('uij,ujk->uik', p, m)
    e_g = jnp.exp(g_col)
    sol = bmm('uij,ujd->uid', p, jnp.concatenate([b_col * v, (b_col * e_g) * k], axis=-1))
    vb, w = sol[:, :, :GDN_DV], sol[:, :, GDN_DV:]
    aqk = bmm('uid,ujd->uij', q, k) * gam
    qg = q * e_g
    g_last = g_col[:, c - 1:c, :]
    kd_t = jnp.swapaxes(k * jnp.exp(g_last - g_col), 1, 2)
    gc = jnp.exp(g_last)

    s = s_scr[...]
    for n in range(nchunk):
        sl = slice(n * nh, (n + 1) * nh)
        u = vb[sl] - bmm('hcd,hde->hce', w[sl], s)
        o = bmm('hcd,hde->hce', qg[sl], s) + bmm('hij,hje->hie', aqk[sl], u)
        s = gc[sl] * s + bmm('hdc,hce->hde', kd_t[sl], u)
        o = o * lax.rsqrt(jnp.mean(o * o, axis=-1, keepdims=True) + EPS) * nw_ref[...]
        for h in range(nh):
            zs = z_ref[0, n * c:(n + 1) * c, h * GDN_DV:(h + 1) * GDN_DV]
            o_ref[0, n * c:(n + 1) * c, h * GDN_DV:(h + 1) * GDN_DV] = o[h] * _silu(zs)
    s_scr[...] = s

    @pl.when(j == pl.num_programs(1) - 1)
    def _():
        sfin_ref[0] = s_scr[...]


def _gdn(qkv_raw, z, misc, conv_w, conv_buf, a_log, dt_bias, norm_w, s0, *, tb, n_valid):
    bsz, tp, ch = qkv_raw.shape
    cinit = jnp.concatenate([jnp.zeros((bsz, 5, ch), f32), conv_buf.astype(f32)], axis=1)
    hp = jnp.zeros((8, MISC_W), f32)
    hp = hp.at[0, :GDN_HEADS].set(-jnp.exp(a_log.astype(f32))).at[1, :GDN_HEADS].set(dt_bias.astype(f32))
    zw = GDN_HEADS * GDN_DV
    return pl.pallas_call(
        functools.partial(_gdn_kernel, tb=tb, n_valid=n_valid),
        grid=(bsz, tp // tb),
        in_specs=[pl.BlockSpec((1, tb, ch), lambda b, j: (b, j, 0)),
                  pl.BlockSpec((1, tb, zw), lambda b, j: (b, j, 0)),
                  pl.BlockSpec((1, tb, MISC_W), lambda b, j: (b, j, 0)),
                  pl.BlockSpec((GDN_CONV, ch), lambda b, j: (0, 0)),
                  pl.BlockSpec((1, 8, ch), lambda b, j: (b, 0, 0)),
                  pl.BlockSpec((8, MISC_W), lambda b, j: (0, 0)),
                  pl.BlockSpec((1, GDN_DV), lambda b, j: (0, 0)),
                  pl.BlockSpec((1, GDN_HEADS, GDN_DK, GDN_DV), lambda b, j: (b, 0, 0, 0))],
        out_specs=[pl.BlockSpec((1, tb, zw), lambda b, j: (b, j, 0)),
                   pl.BlockSpec((1, GDN_HEADS, GDN_DK, GDN_DV), lambda b, j: (b, 0, 0, 0))],
        out_shape=[jax.ShapeDtypeStruct((bsz, tp, zw), f32),
                   jax.ShapeDtypeStruct((bsz, GDN_HEADS, GDN_DK, GDN_DV), f32)],
        scratch_shapes=[pltpu.VMEM((tb + 8, ch), f32), pltpu.VMEM((GDN_HEADS, GDN_DK, GDN_DV), f32)],
        compiler_params=_cparams(("parallel", "arbitrary")),
        name="gdn",
    )(qkv_raw, z, misc, conv_w.astype(f32), cinit, hp, norm_w.reshape(1, GDN_DV).astype(f32), s0.astype(f32))


def _bucket_thresholds():
    exact = REL_BUCKETS // 2
    d = np.arange(0, 4 * REL_MAX_DIST, dtype=np.int64)

    def buckets(ft):
        nf = np.maximum(d, exact).astype(ft)
        large = exact + (np.log(nf / ft(exact)) / ft(math.log(REL_MAX_DIST / exact)) * ft(REL_BUCKETS - exact)).astype(np.int32)
        return np.where(d < exact, d, np.minimum(large, REL_BUCKETS - 1))

    b64, b32 = buckets(np.float64), buckets(np.float32)
    assert np.array_equal(b64, b32) and b64[-1] == REL_BUCKETS - 1 and np.all(np.diff(b64) >= 0)
    return [int(np.argmax(b64 >= k)) for k in range(1, REL_BUCKETS)]


_BUCKET_THR = _bucket_thresholds()
FAR_DIST = _BUCKET_THR[-1]


def _bias_kernel(rb_ref, d_ref, o_ref, *, slab, group_heads):
    rows = d_ref.shape[-2]
    for s in range(rows // slab):
        d = d_ref[0, 0, s * slab:(s + 1) * slab, :]
        h = pl.program_id(0) * group_heads + s
        val = jnp.full(d.shape, rb_ref[0, h], f32)
        for k in range(1, REL_BUCKETS):
            val = jnp.where(d >= _BUCKET_THR[k - 1], rb_ref[k, h], val)
        o_ref[0, 0, s * slab:(s + 1) * slab, :] = jnp.where(d < 0, NEG, val)


def _bias_table(rel_bias, dist, *, slab):
    g, nt, rows, cols = dist.shape
    return pl.pallas_call(
        functools.partial(_bias_kernel, slab=slab, group_heads=rows // slab),
        grid=(g, nt),
        in_specs=[pl.BlockSpec(memory_space=pltpu.SMEM),
                  pl.BlockSpec((1, 1, rows, cols), lambda a, b: (a, b, 0, 0))],
        out_specs=pl.BlockSpec((1, 1, rows, cols), lambda a, b: (a, b, 0, 0)),
        out_shape=jax.ShapeDtypeStruct(dist.shape, f32),
        compiler_params=_cparams(("arbitrary", "arbitrary")),
        name="bias_table",
    )(rel_bias.astype(f32), dist)


def _blockmean_kernel(x_ref, o_ref):
    x = x_ref[0]
    nb = x.shape[0] // NSA_BLOCK
    o_ref[0] = jnp.sum(x.reshape(nb, NSA_BLOCK, x.shape[1]), axis=1) * (1.0 / NSA_BLOCK)


def _block_means(kv4, *, tb):
    bsz, t, _ = kv4.shape
    w = 2 * NSA_KV_GROUPS * NSA_DH
    return pl.pallas_call(
        _blockmean_kernel,
        grid=(bsz, t // tb),
        in_specs=[pl.BlockSpec((1, tb, w), lambda b, i: (b, i, 0))],
        out_specs=pl.BlockSpec((1, tb // NSA_BLOCK, w), lambda b, i: (b, i, 0)),
        out_shape=jax.ShapeDtypeStruct((bsz, t // NSA_BLOCK, w), f32),
        compiler_params=_cparams(("parallel", "parallel")),
        name="nsa_block_means",
    )(kv4)


def _topn_mask(imp, n_sel):
    lane = lax.broadcasted_iota(i32, imp.shape, 1)
    sel = jnp.zeros(imp.shape, jnp.bool_)
    for _ in range(n_sel):
        mx = jnp.max(imp, axis=-1, keepdims=True)
        first = jnp.min(jnp.where(imp == mx, lane, imp.shape[1]), axis=-1, keepdims=True)
        hit = lane == first
        sel = sel | hit
        imp = jnp.where(hit, -jnp.inf, imp)
    return sel


def _nsa_prompt_kernel(q_ref, misc_ref, kcvc_ref, kaug_ref, v2_ref, kw_ref, vw_ref,
                       tabs_ref, tabw_ref, tabc_ref, c31_ref, o_ref):
    g = pl.program_id(1)
    i = pl.program_id(2)
    qb, dh, hpg = NSA_QBLOCK, NSA_DH, NSA_HPG
    rows = hpg * qb
    is_g0 = g == 0

    def ghalf(a):
        return jnp.where(is_g0, a[:, :dh], a[:, dh:])

    qt = q_ref[0] * (dh ** -0.5)
    qs = jnp.concatenate([qt[:, h * dh:(h + 1) * dh] for h in range(hpg)], axis=0)
    zero = jnp.zeros_like(qs)
    q2 = jnp.concatenate([jnp.where(is_g0, qs, zero), jnp.where(is_g0, zero, qs)], axis=1)

    kcvc = kcvc_ref[0]
    nblk = kcvc.shape[0]
    kc = ghalf(kcvc[:, :2 * dh])
    vc = ghalf(kcvc[:, 2 * dh:])
    s_c = _mm3_nt(qs, kc)
    lane = lax.broadcasted_iota(i32, (rows, nblk), 1)
    qrow = lax.broadcasted_iota(i32, (rows, nblk), 0) & (qb - 1)
    shift = (2 * i - 2 * NEAR_PAIRS + 1 + 4 * nblk) % nblk
    bias_c = pltpu.roll(tabc_ref[0], shift, 1)
    bias_c = jnp.where(lane < 2 * i - 2 * NEAR_PAIRS + 1, c31_ref[0], bias_c)
    readable = lane * NSA_BLOCK + (NSA_BLOCK - 1) <= i * qb + qrow
    s_c = jnp.where(readable, s_c + bias_c, NEG)
    m_c = jnp.max(s_c, axis=-1, keepdims=True)
    p_c = jnp.where(readable, jnp.exp(s_c - m_c), 0.0)
    p_c = p_c / jnp.maximum(jnp.sum(p_c, axis=-1, keepdims=True), 1e-30)
    o_c = _mm(p_c, vc)

    imp = p_c[0:qb]
    for h in range(1, hpg):
        imp = imp + p_c[h * qb:(h + 1) * qb]
    blk = lax.broadcasted_iota(i32, (qb, nblk), 1)
    cur = (i * qb + lax.broadcasted_iota(i32, (qb, nblk), 0)) >> int(math.log2(NSA_BLOCK))
    valid = blk <= cur
    forced = valid & ((blk == 0) | (blk > cur - NSA_LOCAL))
    imp = jnp.where(forced, -jnp.inf, jnp.where(valid, imp, -1.0))
    sel = (forced | _topn_mask(imp, min(NSA_TOPN, nblk) - 1 - NSA_LOCAL)) & valid
    msel = jnp.where(sel, 0.0, NEG).astype(bf16)
    qaug = jnp.concatenate([q2.astype(bf16), jnp.concatenate([msel] * hpg, axis=0)], axis=1)

    t_d = i // 2
    tk = 2 * PAIR

    def tile(k, carry):
        m, l, acc = carry
        t = t_d - k
        ks = pl.multiple_of(t * tk, tk)
        s = lax.dot_general(qaug, kaug_ref[0, pl.ds(ks, tk), :], (((1,), (1,)), ((), ())),
                            preferred_element_type=f32)
        i0 = jnp.clip(i - 2 * t, 0, NEAR_PAIRS)
        i1 = jnp.clip(i - 2 * t - 1, 0, NEAR_PAIRS)
        s = s + jnp.concatenate([tabs_ref[0, i0], tabs_ref[0, i1]], axis=1)
        m_new = jnp.maximum(m, jnp.max(s, axis=-1, keepdims=True))
        alpha = jnp.exp(m - m_new)
        p = jnp.exp(s - m_new)
        l = alpha * l + jnp.sum(p, axis=-1, keepdims=True)
        acc = alpha * acc + jnp.dot(p.astype(bf16), v2_ref[0, pl.ds(ks, tk), :], preferred_element_type=f32)
        return m_new, l, acc

    m0 = jnp.full((rows, 1), NEG, f32)
    l0 = jnp.zeros((rows, 1), f32)
    a0 = jnp.zeros((rows, 2 * dh), f32)
    _, l_s, acc_s = lax.fori_loop(0, t_d + 1, tile, (m0, l0, a0))
    o_s = ghalf(acc_s) / jnp.maximum(l_s, 1e-30)

    span = NSA_WINDOW + qb
    ws = pl.multiple_of(i * qb, qb)
    s_w = lax.dot_general(q2.astype(bf16), kw_ref[0, pl.ds(ws, span), :], (((1,), (1,)), ((), ())),
                          preferred_element_type=f32) + tabw_ref[0]
    kpos = lax.broadcasted_iota(i32, (rows, span), 1) + (i * qb - NSA_WINDOW)
    s_w = jnp.where(kpos >= 0, s_w, NEG)
    m_w = jnp.max(s_w, axis=-1, keepdims=True)
    p_w = jnp.exp(s_w - m_w)
    l_w = jnp.sum(p_w, axis=-1, keepdims=True)
    o_w = ghalf(jnp.dot(p_w.astype(bf16), vw_ref[0, pl.ds(ws, span), :], preferred_element_type=f32))
    o_w = o_w / jnp.maximum(l_w, 1e-30)

    gates = jax.nn.sigmoid(misc_ref[0])
    outs = []
    for h in range(hpg):
        r = slice(h * qb, (h + 1) * qb)
        acc = None
        for jb, branch in enumerate((o_c, o_s, o_w)):
            l0_ = 8 + 3 * h + jb
            l1_ = 8 + 3 * (hpg + h) + jb
            gcol = jnp.where(is_g0, gates[:, l0_:l0_ + 1], gates[:, l1_:l1_ + 1])
            term = gcol * branch[r]
            acc = term if acc is None else acc + term
        outs.append(acc)
    o_ref[0] = jnp.concatenate(outs, axis=1)


LOG2E = 1.4426950408889634
NSA_KEY_TILE = 4 * PAIR


def _nsa_prompt_kernel_h(q_ref, misc_ref, kcvc_ref, kaug_ref, vg_ref, kw_ref, vw_ref,
                         tabs_ref, tabw_ref, tabc_ref, c31_ref, o_ref, qaug_scr, s_scr):
    g = pl.program_id(1)
    i = pl.program_id(2)
    qb, dh, hpg = NSA_QBLOCK, NSA_DH, NSA_HPG
    is_g0 = g == 0
    nn = (((1,), (1,)), ((), ()))

    def ghalf(a):
        return jnp.where(is_g0, a[:, :dh], a[:, dh:])

    def hrows(ref, *lead):
        return [ref[lead + (slice(h * qb, (h + 1) * qb), slice(None))] for h in range(hpg)]

    qt = q_ref[0] * (dh ** -0.5)
    qh = [qt[:, h * dh:(h + 1) * dh] for h in range(hpg)]
    kcvc = kcvc_ref[0]
    nblk = kcvc.shape[0]
    kc = ghalf(kcvc[:, :2 * dh])
    vc = ghalf(kcvc[:, 2 * dh:])

    rows = hpg * qb
    qs = jnp.concatenate(qh, axis=0)
    zero = jnp.zeros_like(qs)
    q2 = jnp.concatenate([jnp.where(is_g0, qs, zero), jnp.where(is_g0, zero, qs)], axis=1)
    q2 = (q2 * LOG2E).astype(bf16)

    blk4 = lax.broadcasted_iota(i32, (rows, nblk), 1)
    qrow4 = lax.broadcasted_iota(i32, (rows, nblk), 0) & (qb - 1)
    first_tab = 2 * i - 2 * NEAR_PAIRS + 1
    shift = (first_tab + 4 * nblk) % nblk
    readable = blk4 * NSA_BLOCK + (NSA_BLOCK - 1) <= i * qb + qrow4
    bias = jnp.where(blk4 < first_tab, c31_ref[0], pltpu.roll(tabc_ref[0], shift, 1))
    s_c = jnp.where(readable, _mm3_nt(qs, kc) + bias, NEG)
    m_c = jnp.max(s_c, axis=-1, keepdims=True)
    p_c = jnp.where(readable, jnp.exp(s_c - m_c), 0.0)
    p_c = p_c / jnp.maximum(jnp.sum(p_c, axis=-1, keepdims=True), 1e-30)
    o_cmp = _mm(p_c, vc)
    imp = p_c[0:qb]
    for h in range(1, hpg):
        imp = imp + p_c[h * qb:(h + 1) * qb]

    span = NSA_WINDOW + qb
    ws = pl.multiple_of(i * qb, qb)
    s_w = lax.dot_general(q2, kw_ref[0, pl.ds(ws, span), :], nn, preferred_element_type=f32) + tabw_ref[0]
    in_seq = lax.broadcasted_iota(i32, (rows, span), 1) + (i * qb - NSA_WINDOW) >= 0
    s_w = jnp.where(in_seq, s_w, NEG)
    m_w = jnp.max(s_w, axis=-1, keepdims=True)
    p_w = jnp.exp2(s_w - m_w)
    l_w = jnp.sum(p_w, axis=-1, keepdims=True)
    o_win = ghalf(jnp.dot(p_w.astype(bf16), vw_ref[0, pl.ds(ws, span), :], preferred_element_type=f32))
    o_win = o_win / jnp.maximum(l_w, 1e-30)

    blk = lax.broadcasted_iota(i32, (qb, nblk), 1)
    qrow = lax.broadcasted_iota(i32, (qb, nblk), 0)
    cur = (i * qb + qrow) >> int(math.log2(NSA_BLOCK))
    valid = blk <= cur
    forced = valid & ((blk == 0) | (blk > cur - NSA_LOCAL))
    imp_t = jnp.where(forced, -jnp.inf, jnp.where(valid, imp, -1.0)).T
    brow = lax.broadcasted_iota(i32, (nblk, qb), 0)
    picked = jnp.zeros((nblk, qb), f32)
    for _ in range(min(NSA_TOPN, nblk) - 1 - NSA_LOCAL):
        mx = jnp.max(imp_t, axis=0, keepdims=True)
        first = jnp.min(jnp.where(imp_t == mx, brow, nblk), axis=0, keepdims=True)
        hit = brow == first
        picked = jnp.where(hit, 1.0, picked)
        imp_t = jnp.where(hit, -jnp.inf, imp_t)
    sel = (forced | (picked.T > 0.5)) & valid
    msel = jnp.where(sel, 0.0, NEG).astype(bf16)
    qaug_scr[...] = jnp.concatenate([q2, jnp.concatenate([msel] * hpg, axis=0)], axis=1)

    tk = s_scr.shape[1]
    ppt = tk // PAIR
    t_d = i // ppt

    def scores(t):
        ks = pl.multiple_of(t * tk, tk)
        s = lax.dot_general(qaug_scr[...], kaug_ref[0, pl.ds(ks, tk), :], nn, preferred_element_type=f32)
        tabs = [tabs_ref[0, jnp.clip(i - ppt * t - j, 0, NEAR_PAIRS)] for j in range(ppt)]
        return s + jnp.concatenate(tabs, axis=1)

    s_scr[...] = scores(t_d)

    def tile(k_it, carry):
        m, acc = carry
        t = t_d - k_it
        s = s_scr[...]
        s_next = scores(jnp.maximum(t - 1, 0))
        m_new = jnp.maximum(m, jnp.max(s, axis=-1, keepdims=True))
        alpha = jnp.exp2(m - m_new)
        p = jnp.exp2(s - m_new).astype(bf16)
        ks = pl.multiple_of(t * tk, tk)
        acc = alpha * acc + jnp.dot(p, vg_ref[0, 0, pl.ds(ks, tk), :], preferred_element_type=f32)
        s_scr[...] = s_next
        return m_new, acc

    _, acc_sel = lax.fori_loop(0, t_d + 1, tile, (jnp.full((rows, 1), NEG, f32), jnp.zeros((rows, 2 * dh), f32)))
    o_sel = acc_sel[:, :dh] / jnp.maximum(acc_sel[:, dh:dh + 1], 1e-30)

    gates = jax.nn.sigmoid(misc_ref[0])
    outs = []
    for h in range(hpg):
        r = slice(h * qb, (h + 1) * qb)
        acc = None
        for jb, branch in enumerate((o_cmp[r], o_sel[r], o_win[r])):
            la = 8 + 3 * h + jb
            lb = 8 + 3 * (hpg + h) + jb
            term = jnp.where(is_g0, gates[:, la:la + 1], gates[:, lb:lb + 1]) * branch
            acc = term if acc is None else acc + term
        outs.append(acc)
    o_ref[0] = jnp.concatenate(outs, axis=1)


def _nsa_prompt_kernel_m(q_ref, misc_ref, kcvc_ref, kaug_ref, v3_ref, kw_ref, vw3_ref,
                         tabs_ref, tabw_ref, tabc_ref, c31_ref, o_ref, qaug_scr, s_scr):
    i = pl.program_id(1)
    qb, dh, hpg, ng, nh = NSA_QBLOCK, NSA_DH, NSA_HPG, NSA_KV_GROUPS, NSA_HEADS
    assert ng == 2
    rows, grows = nh * qb, hpg * qb
    nn = (((1,), (1,)), ((), ()))
    in_g0 = lax.broadcasted_iota(i32, (rows, 1), 0) < grows

    def own(a):
        return jnp.where(in_g0, a[:, :dh], a[:, dh:2 * dh])

    qt = q_ref[0] * (dh ** -0.5)
    qs = jnp.concatenate([qt[:, h * dh:(h + 1) * dh] for h in range(nh)], axis=0)
    zero = jnp.zeros_like(qs)
    q2f = jnp.concatenate([jnp.where(in_g0, qs, zero), jnp.where(in_g0, zero, qs)], axis=1)
    q2 = (q2f * LOG2E).astype(bf16)
    kcvc = kcvc_ref[0]
    nblk = kcvc.shape[0]

    blk4 = lax.broadcasted_iota(i32, (rows, nblk), 1)
    qrow4 = lax.broadcasted_iota(i32, (rows, nblk), 0) & (qb - 1)
    first_tab = 2 * i - 2 * NEAR_PAIRS + 1
    shift = (first_tab + 4 * nblk) % nblk
    readable = blk4 * NSA_BLOCK + (NSA_BLOCK - 1) <= i * qb + qrow4
    bias = jnp.where(blk4 < first_tab, c31_ref[...], pltpu.roll(tabc_ref[...], shift, 1))
    s_c = jnp.where(readable, _mm3_nt(q2f, kcvc[:, :2 * dh]) + bias, NEG)
    m_c = jnp.max(s_c, axis=-1, keepdims=True)
    p_c = jnp.where(readable, jnp.exp(s_c - m_c), 0.0)
    p_c = p_c / jnp.maximum(jnp.sum(p_c, axis=-1, keepdims=True), 1e-30)
    o_cmp = own(_mm(p_c, kcvc[:, 2 * dh:]))
    imp = []
    for g in range(ng):
        acc = p_c[g * grows:g * grows + qb]
        for h in range(1, hpg):
            acc = acc + p_c[g * grows + h * qb:g * grows + (h + 1) * qb]
        imp.append(acc)
    imp = jnp.concatenate(imp, axis=0)

    span = NSA_WINDOW + qb
    ws = pl.multiple_of(i * qb, qb)
    s_w = lax.dot_general(q2, kw_ref[0, pl.ds(ws, span), :], nn, preferred_element_type=f32) + tabw_ref[...]
    in_seq = lax.broadcasted_iota(i32, (rows, span), 1) + (i * qb - NSA_WINDOW) >= 0
    s_w = jnp.where(in_seq, s_w, NEG)
    p_w = jnp.exp2(s_w - jnp.max(s_w, axis=-1, keepdims=True)).astype(bf16)
    acc_w = jnp.dot(p_w, vw3_ref[0, pl.ds(ws, span), :], preferred_element_type=f32)
    o_win = own(acc_w) / jnp.maximum(acc_w[:, 2 * dh:2 * dh + 1], 1e-30)

    blk = lax.broadcasted_iota(i32, (ng * qb, nblk), 1)
    qrow = lax.broadcasted_iota(i32, (ng * qb, nblk), 0) & (qb - 1)
    cur = (i * qb + qrow) >> int(math.log2(NSA_BLOCK))
    valid = blk <= cur
    forced = valid & ((blk == 0) | (blk > cur - NSA_LOCAL))
    imp_t = jnp.where(forced, -jnp.inf, jnp.where(valid, imp, -1.0)).T
    brow = lax.broadcasted_iota(i32, imp_t.shape, 0)
    picked = jnp.zeros(imp_t.shape, f32)
    for _ in range(min(NSA_TOPN, nblk) - 1 - NSA_LOCAL):
        mx = jnp.max(imp_t, axis=0, keepdims=True)
        first = jnp.min(jnp.where(imp_t == mx, brow, nblk), axis=0, keepdims=True)
        hit = brow == first
        picked = jnp.where(hit, 1.0, picked)
        imp_t = jnp.where(hit, -jnp.inf, imp_t)
    sel = (forced | (picked.T > 0.5)) & valid
    msel = jnp.where(sel, 0.0, NEG).astype(bf16)
    msel = jnp.concatenate([msel[g * qb:(g + 1) * qb] for g in range(ng) for _ in range(hpg)], axis=0)
    qaug_scr[...] = jnp.concatenate([q2, msel], axis=1)

    tk = s_scr.shape[1]
    ppt = tk // PAIR
    t_d = i // ppt

    def scores(t):
        ks = pl.multiple_of(t * tk, tk)
        s = lax.dot_general(qaug_scr[...], kaug_ref[0, pl.ds(ks, tk), :], nn, preferred_element_type=f32)
        tabs = [tabs_ref[jnp.clip(i - ppt * t - j, 0, NEAR_PAIRS)] for j in range(ppt)]
        return s + jnp.concatenate(tabs, axis=1)

    s_scr[...] = scores(t_d)

    def tile(k_it, carry):
        m, acc = carry
        t = t_d - k_it
        s = s_scr[...]
        s_next = scores(jnp.maximum(t - 1, 0))
        m_new = jnp.maximum(m, jnp.max(s, axis=-1, keepdims=True))
        alpha = jnp.exp2(m - m_new)
        p = jnp.exp2(s - m_new).astype(bf16)
        ks = pl.multiple_of(t * tk, tk)
        acc = alpha * acc + jnp.dot(p, v3_ref[0, pl.ds(ks, tk), :], preferred_element_type=f32)
        s_scr[...] = s_next
        return m_new, acc

    _, acc_s = lax.fori_loop(0, t_d + 1, tile, (jnp.full((rows, 1), NEG, f32), jnp.zeros((rows, 4 * dh), f32)))
    o_sel = own(acc_s) / jnp.maximum(acc_s[:, 2 * dh:2 * dh + 1], 1e-30)

    gates = jax.nn.sigmoid(misc_ref[0])
    outs = []
    for h in range(nh):
        r = slice(h * qb, (h + 1) * qb)
        acc = None
        for jb, branch in enumerate((o_cmp[r], o_sel[r], o_win[r])):
            lane = 2 * GDN_HEADS + 3 * h + jb
            term = gates[:, lane:lane + 1] * branch
            acc = term if acc is None else acc + term
        outs.append(acc)
    o_ref[0] = jnp.concatenate(outs, axis=1)


def _nsa_prompt_m(q_b, misc, kv4, kvwin, rel_bias):
    bsz, t, _ = q_b.shape
    dh, qb = NSA_DH, NSA_QBLOCK
    nblk = t // NSA_BLOCK
    assert nblk == PAIR and t % NSA_KEY_TILE == 0
    kcvc = _block_means(kv4, tb=512)
    onehot = (jnp.arange(t, dtype=i32)[:, None] // NSA_BLOCK == jnp.arange(nblk, dtype=i32)[None, :]).astype(bf16)
    kaug = jnp.concatenate([kv4[:, :, 4 * dh:6 * dh].astype(bf16),
                            jnp.broadcast_to(onehot[None], (bsz, t, nblk))], axis=-1)
    ones = jnp.ones((bsz, t, 2 * dh), bf16)
    v3 = jnp.concatenate([kv4[:, :, 6 * dh:8 * dh].astype(bf16), ones], axis=-1)
    pad = ((0, 0), (NSA_WINDOW, 0), (0, 0))
    kw = jnp.pad(kvwin[:, :, :2 * dh].astype(bf16), pad)
    vw3 = jnp.pad(jnp.concatenate([kvwin[:, :, 2 * dh:].astype(bf16), ones], axis=-1), pad)
    tabs, tabw, tabc, c31 = _nsa_tables(rel_bias)
    rows = NSA_HEADS * qb
    span = NSA_WINDOW + qb
    ns = NEAR_PAIRS + 1
    tabs = (jnp.transpose(tabs, (1, 0, 2, 3)) * LOG2E).reshape(ns, rows, PAIR)
    tabw = (tabw * LOG2E).reshape(rows, span)
    tabc = tabc.reshape(rows, PAIR)
    c31 = c31.reshape(rows, 1)
    once = pl.Buffered(1)
    per_b = lambda shape: pl.BlockSpec((1,) + shape, lambda b, i: (b,) + (0,) * len(shape), pipeline_mode=once)
    const = lambda shape: pl.BlockSpec(shape, lambda b, i: (0,) * len(shape), pipeline_mode=once)
    return pl.pallas_call(
        _nsa_prompt_kernel_m,
        grid=(bsz, t // qb),
        in_specs=[pl.BlockSpec((1, qb, NSA_HEADS * dh), lambda b, i: (b, i, 0)),
                  pl.BlockSpec((1, qb, MISC_W), lambda b, i: (b, i, 0)),
                  per_b((nblk, 4 * dh)), per_b((t, 2 * dh + nblk)), per_b((t, 4 * dh)),
                  per_b((t + NSA_WINDOW, 2 * dh)), per_b((t + NSA_WINDOW, 4 * dh)),
                  const((ns, rows, PAIR)), const((rows, span)), const((rows, PAIR)), const((rows, 1))],
        out_specs=pl.BlockSpec((1, qb, NSA_HEADS * dh), lambda b, i: (b, i, 0)),
        out_shape=jax.ShapeDtypeStruct((bsz, t, NSA_HEADS * dh), f32),
        scratch_shapes=[pltpu.VMEM((rows, 2 * dh + nblk), bf16), pltpu.VMEM((rows, NSA_KEY_TILE), f32)],
        compiler_params=_cparams(("parallel", "arbitrary")),
        name="nsa_prompt",
    )(q_b, misc, kcvc, kaug, v3, kw, vw3, tabs, tabw, tabc, c31)


def _nsa_tables(rel_bias):
    qb, hpg = NSA_QBLOCK, NSA_HPG
    rows = hpg * qb
    q = (np.arange(rows) % qb)[:, None]
    c = np.arange(PAIR)[None, :]
    d_sel = [PAIR * idx + q - c for idx in range(NEAR_PAIRS + 1)]
    assert PAIR * NEAR_PAIRS - (PAIR - 1) >= FAR_DIST
    mm = 2 * NEAR_PAIRS - 1 - c
    d_cmp = np.maximum(np.where(mm >= -1, NSA_BLOCK * mm + q - (NSA_BLOCK - 1), 0), 0)
    assert NSA_BLOCK * (2 * NEAR_PAIRS) - (NSA_BLOCK - 1) >= FAR_DIST
    cw = np.arange(NSA_WINDOW + qb)[None, :]
    d_win = q + NSA_WINDOW - cw
    d_win = np.where((d_win >= 0) & (d_win < NSA_WINDOW), d_win, -1)
    n_win = (NSA_WINDOW + qb) // PAIR
    tiles = d_sel + [d_cmp] + [d_win[:, k * PAIR:(k + 1) * PAIR] for k in range(n_win)]
    dist = np.broadcast_to(np.stack(tiles)[None], (NSA_KV_GROUPS, len(tiles), rows, PAIR))
    tab = _bias_table(rel_bias, jnp.asarray(dist, i32), slab=qb)
    ns = NEAR_PAIRS + 1
    tabs = tab[:, :ns]
    tabc = tab[:, ns]
    tabw = jnp.concatenate([tab[:, ns + 1 + k] for k in range(n_win)], axis=-1)
    c31 = jnp.repeat(rel_bias.astype(f32)[REL_BUCKETS - 1].reshape(NSA_KV_GROUPS, hpg), qb, axis=1)
    return tabs, tabw, tabc, c31.reshape(NSA_KV_GROUPS, rows, 1)


def _nsa_prompt(q_b, misc, kv4, kvwin, rel_bias):
    bsz, t, _ = q_b.shape
    dh, qb = NSA_DH, NSA_QBLOCK
    nblk = t // NSA_BLOCK
    assert nblk == PAIR and t % NSA_KEY_TILE == 0
    kcvc = _block_means(kv4, tb=512)
    onehot = (jnp.arange(t, dtype=i32)[:, None] // NSA_BLOCK == jnp.arange(nblk, dtype=i32)[None, :]).astype(bf16)
    kaug = jnp.concatenate([kv4[:, :, 4 * dh:6 * dh].astype(bf16),
                            jnp.broadcast_to(onehot[None], (bsz, t, nblk))], axis=-1)
    ones = jnp.ones((bsz, t, dh), bf16)
    vg = jnp.stack([jnp.concatenate([kv4[:, :, (6 + g) * dh:(7 + g) * dh].astype(bf16), ones], axis=-1)
                    for g in range(NSA_KV_GROUPS)], axis=1)
    pad = ((0, 0), (NSA_WINDOW, 0), (0, 0))
    kw = jnp.pad(kvwin[:, :, :2 * dh].astype(bf16), pad)
    vw = jnp.pad(kvwin[:, :, 2 * dh:].astype(bf16), pad)
    tabs, tabw, tabc, c31 = _nsa_tables(rel_bias)
    tabs, tabw = tabs * LOG2E, tabw * LOG2E
    rows = NSA_HPG * qb
    gw = NSA_HPG * dh
    span = NSA_WINDOW + qb
    return pl.pallas_call(
        _nsa_prompt_kernel_h,
        grid=(bsz, NSA_KV_GROUPS, t // qb),
        in_specs=[pl.BlockSpec((1, qb, gw), lambda b, g, i: (b, i, g)),
                  pl.BlockSpec((1, qb, MISC_W), lambda b, g, i: (b, i, 0)),
                  pl.BlockSpec((1, nblk, 4 * dh), lambda b, g, i: (b, 0, 0)),
                  pl.BlockSpec((1, t, 2 * dh + nblk), lambda b, g, i: (b, 0, 0)),
                  pl.BlockSpec((1, 1, t, 2 * dh), lambda b, g, i: (b, g, 0, 0)),
                  pl.BlockSpec((1, t + NSA_WINDOW, 2 * dh), lambda b, g, i: (b, 0, 0)),
                  pl.BlockSpec((1, t + NSA_WINDOW, 2 * dh), lambda b, g, i: (b, 0, 0)),
                  pl.BlockSpec((1, NEAR_PAIRS + 1, rows, PAIR), lambda b, g, i: (g, 0, 0, 0)),
                  pl.BlockSpec((1, rows, span), lambda b, g, i: (g, 0, 0)),
                  pl.BlockSpec((1, rows, PAIR), lambda b, g, i: (g, 0, 0)),
                  pl.BlockSpec((1, rows, 1), lambda b, g, i: (g, 0, 0))],
        out_specs=pl.BlockSpec((1, qb, gw), lambda b, g, i: (b, i, g)),
        out_shape=jax.ShapeDtypeStruct((bsz, t, NSA_HEADS * dh), f32),
        scratch_shapes=[pltpu.VMEM((rows, 2 * dh + nblk), bf16), pltpu.VMEM((rows, NSA_KEY_TILE), f32)],
        compiler_params=_cparams(("parallel", "parallel", "arbitrary")),
        name="nsa_prompt",
    )(q_b, misc, kcvc, kaug, vg, kw, vw, tabs, tabw, tabc, c31)


MEANS_PAGES = 8


def _nsa_s_means_kernel(pt_ref, *refs):
    del pt_ref
    x_refs, o_ref = refs[:-1], refs[-1]
    rows = [[[] for _ in range(NSA_KV_GROUPS)] for _ in range(2)]
    for x_ref in x_refs:
        page = x_ref.shape[1]
        for j in range(2):
            for g in range(NSA_KV_GROUPS):
                x = x_ref[0, :, j, g, :]
                for n in range(page // NSA_BLOCK):
                    rows[j][g].append(jnp.sum(x[n * NSA_BLOCK:(n + 1) * NSA_BLOCK], axis=0, keepdims=True)
                                      * (1.0 / NSA_BLOCK))
    for j in range(2):
        for g in range(NSA_KV_GROUPS):
            o_ref[0, j * NSA_KV_GROUPS + g] = jnp.concatenate(rows[j][g], axis=0)


def _nsa_s_means(cache_kv, page_table):
    db, n_pages = page_table.shape
    page = cache_kv.shape[1]
    nblk = n_pages * page // NSA_BLOCK
    kp = MEANS_PAGES
    rows = kp * page // NSA_BLOCK
    assert n_pages % kp == 0 and rows % 8 == 0

    def page_spec(k):
        return pl.BlockSpec((1, page, 2, NSA_KV_GROUPS, NSA_DH),
                            lambda b, p, pt: (pt[b * n_pages + p * kp + k], 0, 0, 0, 0))

    return pl.pallas_call(
        _nsa_s_means_kernel,
        grid_spec=pltpu.PrefetchScalarGridSpec(
            num_scalar_prefetch=1,
            grid=(db, n_pages // kp),
            in_specs=[page_spec(k) for k in range(kp)],
            out_specs=pl.BlockSpec((1, 2 * NSA_KV_GROUPS, rows, NSA_DH), lambda b, p, pt: (b, 0, p, 0))),
        out_shape=jax.ShapeDtypeStruct((db, 2 * NSA_KV_GROUPS, nblk, NSA_DH), f32),
        compiler_params=_cparams(("parallel", "parallel")),
        name="nsa_sample_means",
    )(page_table.reshape(-1), *([cache_kv] * kp))


def _nsa_s_scores_kernel(q_ref, kcvc_ref, win_ref, kvn_ref, tc_ref, tw_ref, b0_ref, oc_ref, ow_ref, sel_ref):
    dh, hpg = NSA_DH, NSA_HPG
    q = q_ref[0] * (dh ** -0.5)
    nblk = kcvc_ref.shape[2]
    kvn = kvn_ref[0]
    row = lax.broadcasted_iota(i32, (NSA_HEADS, 1), 0)
    lane = lax.broadcasted_iota(i32, (1, nblk), 1)
    n_sel = NSA_TOPN - 1
    for g in range(NSA_KV_GROUPS):
        in_g = (row >= g * hpg) & (row < (g + 1) * hpg)
        kc = kcvc_ref[0, g]
        vc = kcvc_ref[0, NSA_KV_GROUPS + g]
        s_c = _mm3_nt(q, kc) + tc_ref[...]
        m_c = jnp.max(s_c, axis=-1, keepdims=True)
        p_c = jnp.exp(s_c - m_c)
        p_c = p_c / jnp.maximum(jnp.sum(p_c, axis=-1, keepdims=True), 1e-30)
        o_c = _mm(p_c, vc)
        imp = jnp.sum(jnp.where(in_g, p_c, 0.0), axis=0, keepdims=True)
        forced_blocks = [0] + [nblk - k for k in range(1, NSA_LOCAL)]
        imp = jnp.where((lane == 0) | (lane > nblk - NSA_LOCAL), -jnp.inf, imp)
        picks = jnp.zeros((1, nblk), i32)
        for it, fb in enumerate(forced_blocks):
            picks = jnp.where(lane == it, fb, picks)
        for it in range(len(forced_blocks), n_sel):
            mx = jnp.max(imp, axis=-1, keepdims=True)
            first = jnp.min(jnp.where(imp == mx, lane, nblk), axis=-1, keepdims=True)
            picks = jnp.where(lane == it, first, picks)
            imp = jnp.where(lane == first, -jnp.inf, imp)
        sel_ref[0, g:g + 1, :] = picks
        kw = win_ref[0, :, 0, g, :]
        vw = win_ref[0, :, 1, g, :]
        s_w = _mm3_nt(q, kw) + tw_ref[...]
        s_n = jnp.sum(q * kvn[:, g * dh:(g + 1) * dh], axis=-1, keepdims=True) + b0_ref[...]
        m_w = jnp.maximum(jnp.max(s_w, axis=-1, keepdims=True), s_n)
        p_w = jnp.exp(s_w - m_w)
        p_n = jnp.exp(s_n - m_w)
        l_w = jnp.sum(p_w, axis=-1, keepdims=True) + p_n
        o_w = (_mm(p_w, vw) + p_n * kvn[:, (NSA_KV_GROUPS + g) * dh:(NSA_KV_GROUPS + g + 1) * dh]) / jnp.maximum(l_w, 1e-30)
        if g == 0:
            oc_ref[0] = o_c
            ow_ref[0] = o_w
        else:
            oc_ref[0] = jnp.where(in_g, o_c, oc_ref[0])
            ow_ref[0] = jnp.where(in_g, o_w, ow_ref[0])


def _nsa_s_sel_kernel(sel_ref, pt_ref, q_ref, k_ref, v_ref, tb_ref, kvn_ref, b0_ref, oc_ref, ow_ref, gt_ref,
                      o_ref, m_scr, l_scr, acc_scr):
    del sel_ref, pt_ref
    g = pl.program_id(1)
    j = pl.program_id(2)
    dh, hpg = NSA_DH, NSA_HPG
    is_g0 = g == 0
    q = q_ref[0] * (dh ** -0.5)
    kvn = kvn_ref[0]

    @pl.when(j == 0)
    def _():
        k_n = jnp.where(is_g0, kvn[:, 4 * dh:5 * dh], kvn[:, 5 * dh:6 * dh])
        v_n = jnp.where(is_g0, kvn[:, 6 * dh:7 * dh], kvn[:, 7 * dh:8 * dh])
        m_scr[...] = jnp.sum(q * k_n, axis=-1, keepdims=True) + b0_ref[...]
        l_scr[...] = jnp.ones_like(l_scr)
        acc_scr[...] = jnp.broadcast_to(v_n, acc_scr.shape)

    k = jnp.where(is_g0, k_ref[0, :, 0, 0, :], k_ref[0, :, 0, 1, :])
    v = jnp.where(is_g0, v_ref[0, :, 0, 0, :], v_ref[0, :, 0, 1, :])
    s = _mm3_nt(q, k) + tb_ref[0]
    m_old = m_scr[...]
    m_new = jnp.maximum(m_old, jnp.max(s, axis=-1, keepdims=True))
    alpha = jnp.exp(m_old - m_new)
    p = jnp.exp(s - m_new)
    l_scr[...] = alpha * l_scr[...] + jnp.sum(p, axis=-1, keepdims=True)
    acc_scr[...] = alpha * acc_scr[...] + _mm(p, v)
    m_scr[...] = m_new

    @pl.when(j == pl.num_programs(2) - 1)
    def _():
        o_s = acc_scr[...] / jnp.maximum(l_scr[...], 1e-30)
        gt = gt_ref[0]
        o = gt[:, 0:1] * oc_ref[0] + gt[:, 1:2] * o_s + gt[:, 2:3] * ow_ref[0]
        row = lax.broadcasted_iota(i32, o.shape, 0)
        in_g = (row >= g * hpg) & (row < (g + 1) * hpg)

        @pl.when(is_g0)
        def _():
            o_ref[0] = o

        @pl.when(jnp.logical_not(is_g0))
        def _():
            o_ref[0] = jnp.where(in_g, o, o_ref[0])


def _nsa_sample(q_b, gates, kv4_new, kvwin_new, cache_kv, cache_win, page_table, rel_bias):
    db = q_b.shape[0]
    dh = NSA_DH
    n_pool, page = cache_kv.shape[:2]
    n_pages = page_table.shape[1]
    past = n_pages * page
    nblk = past // NSA_BLOCK
    wb = cache_win.shape[1]
    assert nblk == PAIR and page % NSA_BLOCK == 0 and wb == NSA_WINDOW
    kcvc = _nsa_s_means(cache_kv, page_table)

    n = np.arange(nblk)
    d_cmp = past - (n * NSA_BLOCK + NSA_BLOCK - 1)
    jw = np.arange(wb)
    d_win = np.where(jw >= 1, wb - jw, -1)
    pos = np.arange(nblk * NSA_BLOCK)
    d_sel = past - pos
    width = nblk + wb + nblk * NSA_BLOCK
    dist = np.broadcast_to(np.concatenate([d_cmp, d_win, d_sel])[None, :], (NSA_HEADS, width))
    tab = _bias_table(rel_bias, jnp.asarray(dist[None, None], i32), slab=1)[0, 0]
    t_cmp, t_win = tab[:, :nblk], tab[:, nblk:nblk + wb]
    t_sel = jnp.transpose(tab[:, nblk + wb:].reshape(NSA_HEADS, nblk, NSA_BLOCK), (1, 0, 2))
    b0 = rel_bias.astype(f32)[0].reshape(NSA_HEADS, 1)

    q3 = q_b.reshape(db, NSA_HEADS, dh)
    o_c, o_w, sel = pl.pallas_call(
        _nsa_s_scores_kernel,
        grid=(db,),
        in_specs=[pl.BlockSpec((1, NSA_HEADS, dh), lambda b: (b, 0, 0)),
                  pl.BlockSpec((1, 2 * NSA_KV_GROUPS, nblk, dh), lambda b: (b, 0, 0, 0)),
                  pl.BlockSpec((1, wb, 2, NSA_KV_GROUPS, dh), lambda b: (b, 0, 0, 0, 0)),
                  pl.BlockSpec((1, 1, 4 * dh), lambda b: (b, 0, 0)),
                  pl.BlockSpec((NSA_HEADS, nblk), lambda b: (0, 0)),
                  pl.BlockSpec((NSA_HEADS, wb), lambda b: (0, 0)),
                  pl.BlockSpec((NSA_HEADS, 1), lambda b: (0, 0))],
        out_specs=[pl.BlockSpec((1, NSA_HEADS, dh), lambda b: (b, 0, 0)),
                   pl.BlockSpec((1, NSA_HEADS, dh), lambda b: (b, 0, 0)),
                   pl.BlockSpec((1, NSA_KV_GROUPS, nblk), lambda b: (b, 0, 0))],
        out_shape=[jax.ShapeDtypeStruct((db, NSA_HEADS, dh), f32),
                   jax.ShapeDtypeStruct((db, NSA_HEADS, dh), f32),
                   jax.ShapeDtypeStruct((db, NSA_KV_GROUPS, nblk), i32)],
        compiler_params=_cparams(("parallel",)),
        name="nsa_sample_scores",
    )(q3, kcvc, cache_win.astype(f32), kvwin_new.reshape(db, 1, 4 * dh), t_cmp, t_win, b0)

    n_sel = NSA_TOPN - 1
    sel_flat = sel[:, :, :n_sel].reshape(-1)
    halves = page // NSA_BLOCK
    cache_h = cache_kv.reshape(n_pool * halves, NSA_BLOCK, 4, NSA_KV_GROUPS, dh)

    def blk_of(b, g, j, sel_r, pt_r):
        nb = sel_r[(b * NSA_KV_GROUPS + g) * n_sel + j]
        return nb, pt_r[b * n_pages + nb // halves] * halves + nb % halves

    o = pl.pallas_call(
        _nsa_s_sel_kernel,
        grid_spec=pltpu.PrefetchScalarGridSpec(
            num_scalar_prefetch=2,
            grid=(db, NSA_KV_GROUPS, n_sel),
            in_specs=[pl.BlockSpec((1, NSA_HEADS, dh), lambda b, g, j, s, p: (b, 0, 0)),
                      pl.BlockSpec((1, NSA_BLOCK, 1, NSA_KV_GROUPS, dh),
                                   lambda b, g, j, s, p: (blk_of(b, g, j, s, p)[1], 0, 2, 0, 0)),
                      pl.BlockSpec((1, NSA_BLOCK, 1, NSA_KV_GROUPS, dh),
                                   lambda b, g, j, s, p: (blk_of(b, g, j, s, p)[1], 0, 3, 0, 0)),
                      pl.BlockSpec((1, NSA_HEADS, NSA_BLOCK), lambda b, g, j, s, p: (blk_of(b, g, j, s, p)[0], 0, 0)),
                      pl.BlockSpec((1, 1, 8 * dh), lambda b, g, j, s, p: (b, 0, 0)),
                      pl.BlockSpec((NSA_HEADS, 1), lambda b, g, j, s, p: (0, 0)),
                      pl.BlockSpec((1, NSA_HEADS, dh), lambda b, g, j, s, p: (b, 0, 0)),
                      pl.BlockSpec((1, NSA_HEADS, dh), lambda b, g, j, s, p: (b, 0, 0)),
                      pl.BlockSpec((1, NSA_HEADS, 3), lambda b, g, j, s, p: (b, 0, 0))],
            out_specs=pl.BlockSpec((1, NSA_HEADS, dh), lambda b, g, j, s, p: (b, 0, 0)),
            scratch_shapes=[pltpu.VMEM((NSA_HEADS, 1), f32), pltpu.VMEM((NSA_HEADS, 1), f32),
                            pltpu.VMEM((NSA_HEADS, dh), f32)]),
        out_shape=jax.ShapeDtypeStruct((db, NSA_HEADS, dh), f32),
        compiler_params=_cparams(("parallel", "arbitrary", "arbitrary")),
        name="nsa_sample_selected",
    )(sel_flat, page_table.reshape(-1), q3, cache_h, cache_h, t_sel, kv4_new.reshape(db, 1, 8 * dh), b0,
      o_c, o_w, gates)
    return o.reshape(db, NSA_HEADS * dh)


def _nsa_st_means_kernel(pt_ref, *refs):
    del pt_ref
    x_refs, o_ref = refs[:-1], refs[-1]
    p = pl.program_id(1)

    @pl.when(p == 0)
    def _():
        o_ref[...] = jnp.zeros_like(o_ref)

    page = x_refs[0].shape[-1]
    per_page = page // NSA_BLOCK
    nblk = o_ref.shape[-1]
    assert page == nblk
    planes = 2 * NSA_KV_GROUPS
    xs = jnp.concatenate([x_ref[0, j, g] for x_ref in x_refs for j in range(2) for g in range(NSA_KV_GROUPS)],
                         axis=0)
    tok = lax.broadcasted_iota(i32, (page, nblk), 0)
    col = lax.broadcasted_iota(i32, (page, nblk), 1)
    pool = (col == (tok >> int(math.log2(NSA_BLOCK)))).astype(bf16)
    hi, lo = _split2(xs)
    sums = jnp.dot(hi, pool, preferred_element_type=f32) + jnp.dot(lo, pool, preferred_element_type=f32)
    rows = planes * NSA_DH
    acc = None
    for k in range(len(x_refs)):
        shift = (p * len(x_refs) + k) * per_page
        part = pltpu.roll(sums[k * rows:(k + 1) * rows], shift, 1)
        acc = part if acc is None else acc + part
    acc = acc * (1.0 / NSA_BLOCK)
    for pl_i in range(planes):
        o_ref[0, pl_i] = o_ref[0, pl_i] + acc[pl_i * NSA_DH:(pl_i + 1) * NSA_DH]


def _nsa_st_means(cache_t, page_table):
    db, n_pages = page_table.shape
    page = cache_t.shape[-1]
    nblk = n_pages * page // NSA_BLOCK
    kp = MEANS_PAGES
    assert n_pages % kp == 0

    def page_spec(k):
        return pl.BlockSpec((1, 2, NSA_KV_GROUPS, NSA_DH, page),
                            lambda b, p, pt: (pt[b * n_pages + p * kp + k], 0, 0, 0, 0))

    return pl.pallas_call(
        _nsa_st_means_kernel,
        grid_spec=pltpu.PrefetchScalarGridSpec(
            num_scalar_prefetch=1,
            grid=(db, n_pages // kp),
            in_specs=[page_spec(k) for k in range(kp)],
            out_specs=pl.BlockSpec((1, 2 * NSA_KV_GROUPS, NSA_DH, nblk), lambda b, p, pt: (b, 0, 0, 0))),
        out_shape=jax.ShapeDtypeStruct((db, 2 * NSA_KV_GROUPS, NSA_DH, nblk), f32),
        compiler_params=_cparams(("parallel", "arbitrary")),
        name="nsa_sample_means",
    )(page_table.reshape(-1), *([cache_t] * kp))


def _nsa_st_scores_kernel(q_ref, kcvc_ref, win_ref, kvn_ref, tc_ref, tw_ref, b0_ref, oc_ref, ow_ref, sel_ref):
    dh, hpg, ng = NSA_DH, NSA_HPG, NSA_KV_GROUPS
    sb = q_ref.shape[0]
    nblk = kcvc_ref.shape[-1]
    row = lax.broadcasted_iota(i32, (NSA_HEADS, 1), 0)
    in_g = [(row >= g * hpg) & (row < (g + 1) * hpg) for g in range(ng)]
    pairs = [(b, g) for b in range(sb) for g in range(ng)]
    q = [q_ref[b] * (dh ** -0.5) for b in range(sb)]

    s_c = [_mm3(q[b], kcvc_ref[b, g]) + tc_ref[...] for b, g in pairs]
    p_c = []
    for s in s_c:
        p = jnp.exp(s - jnp.max(s, axis=-1, keepdims=True))
        p_c.append(p / jnp.maximum(jnp.sum(p, axis=-1, keepdims=True), 1e-30))
    o_c = [_mm_nt(p_c[k], kcvc_ref[b, ng + g]) for k, (b, g) in enumerate(pairs)]
    imp = jnp.concatenate([jnp.sum(jnp.where(in_g[g], p_c[k], 0.0), axis=0, keepdims=True)
                           for k, (b, g) in enumerate(pairs)], axis=0)
    for b in range(sb):
        oc_ref[b] = jnp.where(in_g[0], o_c[b * ng], o_c[b * ng + 1])

    s_w = [_mm3(q[b], win_ref[b, 0, g]) + tw_ref[...] for b, g in pairs]
    o_w = []
    for k, (b, g) in enumerate(pairs):
        kvn = kvn_ref[b]
        s_n = jnp.sum(q[b] * kvn[:, g * dh:(g + 1) * dh], axis=-1, keepdims=True) + b0_ref[...]
        m_w = jnp.maximum(jnp.max(s_w[k], axis=-1, keepdims=True), s_n)
        p_w = jnp.exp(s_w[k] - m_w)
        p_n = jnp.exp(s_n - m_w)
        l_w = jnp.sum(p_w, axis=-1, keepdims=True) + p_n
        v_n = kvn[:, (ng + g) * dh:(ng + g + 1) * dh]
        o_w.append((_mm_nt(p_w, win_ref[b, 1, g]) + p_n * v_n) / jnp.maximum(l_w, 1e-30))
    for b in range(sb):
        ow_ref[b] = jnp.where(in_g[0], o_w[b * ng], o_w[b * ng + 1])

    lane = lax.broadcasted_iota(i32, imp.shape, 1)
    n_sel = NSA_TOPN - 1
    forced_blocks = [0] + [nblk - k for k in range(1, NSA_LOCAL)]
    imp = jnp.where((lane == 0) | (lane > nblk - NSA_LOCAL), -jnp.inf, imp)
    picks = jnp.zeros(imp.shape, i32)
    for it, fb in enumerate(forced_blocks):
        picks = jnp.where(lane == it, fb, picks)
    for it in range(len(forced_blocks), n_sel):
        mx = jnp.max(imp, axis=-1, keepdims=True)
        first = jnp.min(jnp.where(imp == mx, lane, nblk), axis=-1, keepdims=True)
        picks = jnp.where(lane == it, first, picks)
        imp = jnp.where(lane == first, -jnp.inf, imp)
    for b in range(sb):
        sel_ref[b] = picks[b * ng:(b + 1) * ng]


def _nsa_st_sel_kernel(sel_ref, pt_ref, q_ref, rb_ref, kvn_ref, oc_ref, ow_ref, gt_ref, *refs, n_sel, past):
    del pt_ref
    k_refs, v_refs, o_ref = refs[:n_sel], refs[n_sel:2 * n_sel], refs[2 * n_sel]
    b = pl.program_id(0)
    g = pl.program_id(1)
    dh, hpg = NSA_DH, NSA_HPG
    is_g0 = g == 0
    page = k_refs[0].shape[-1]
    per_page = page // NSA_BLOCK
    q = q_ref[0] * (dh ** -0.5)
    kvn = kvn_ref[0]
    lane = lax.broadcasted_iota(i32, (NSA_HEADS, page), 1)
    s_parts, d_parts = [], []
    for j in range(n_sel):
        nb = sel_ref[(b * NSA_KV_GROUPS + g) * n_sel + j]
        in_blk = (lane >> int(math.log2(NSA_BLOCK))) == (nb % per_page)
        d_parts.append(jnp.where(in_blk, past - ((nb // per_page) * page + lane), -1))
        s_parts.append(_mm3(q, k_refs[j][0, 0, 0]))
    d = jnp.concatenate(d_parts, axis=1)
    bias = jnp.broadcast_to(rb_ref[0], d.shape)
    for k in range(1, REL_BUCKETS):
        bias = jnp.where(d >= _BUCKET_THR[k - 1], rb_ref[k], bias)
    s = jnp.concatenate(s_parts, axis=1) + jnp.where(d < 0, NEG, bias)
    k_n = jnp.where(is_g0, kvn[:, 4 * dh:5 * dh], kvn[:, 5 * dh:6 * dh])
    v_n = jnp.where(is_g0, kvn[:, 6 * dh:7 * dh], kvn[:, 7 * dh:8 * dh])
    s_n = jnp.sum(q * k_n, axis=-1, keepdims=True) + rb_ref[0]
    m = jnp.maximum(jnp.max(s, axis=-1, keepdims=True), s_n)
    p = jnp.exp(s - m)
    p_n = jnp.exp(s_n - m)
    l = jnp.sum(p, axis=-1, keepdims=True) + p_n
    acc = p_n * v_n
    for j in range(n_sel):
        acc = acc + _mm_nt(p[:, j * page:(j + 1) * page], v_refs[j][0, 0, 0])
    o_s = acc / jnp.maximum(l, 1e-30)
    gt = gt_ref[0]
    o = gt[:, 0:1] * oc_ref[0] + gt[:, 1:2] * o_s + gt[:, 2:3] * ow_ref[0]
    row = lax.broadcasted_iota(i32, o.shape, 0)
    in_g = (row >= g * hpg) & (row < (g + 1) * hpg)

    @pl.when(is_g0)
    def _():
        o_ref[0] = o

    @pl.when(jnp.logical_not(is_g0))
    def _():
        o_ref[0] = jnp.where(in_g, o, o_ref[0])


def _nsa_sample_t(q_b, gates, kv4_new, kvwin_new, cache_kv, cache_win, page_table, rel_bias):
    db = q_b.shape[0]
    dh = NSA_DH
    n_pool, page = cache_kv.shape[:2]
    n_pages = page_table.shape[1]
    past = n_pages * page
    nblk = past // NSA_BLOCK
    wb = cache_win.shape[1]
    assert nblk == PAIR and page % NSA_BLOCK == 0 and wb == NSA_WINDOW and nblk > NSA_LOCAL
    cache_t = jnp.transpose(cache_kv, (0, 2, 3, 4, 1)).astype(f32)
    win_t = jnp.transpose(cache_win, (0, 2, 3, 4, 1)).astype(f32)
    kcvc = _nsa_st_means(cache_t, page_table)

    n = np.arange(nblk)
    d_cmp = past - (n * NSA_BLOCK + NSA_BLOCK - 1)
    jw = np.arange(wb)
    d_win = np.where(jw >= 1, wb - jw, -1)
    dist = np.broadcast_to(np.concatenate([d_cmp, d_win])[None, :], (NSA_HEADS, nblk + wb))
    tab = _bias_table(rel_bias, jnp.asarray(dist[None, None], i32), slab=1)[0, 0]
    t_cmp, t_win = tab[:, :nblk], tab[:, nblk:]
    rb = rel_bias.astype(f32)
    b0 = rb[0].reshape(NSA_HEADS, 1)

    q3 = q_b.reshape(db, NSA_HEADS, dh)
    sb = max(s for s in (8, 4, 2, 1) if db % s == 0)
    o_c, o_w, sel = pl.pallas_call(
        _nsa_st_scores_kernel,
        grid=(db // sb,),
        in_specs=[pl.BlockSpec((sb, NSA_HEADS, dh), lambda b: (b, 0, 0)),
                  pl.BlockSpec((sb, 2 * NSA_KV_GROUPS, dh, nblk), lambda b: (b, 0, 0, 0)),
                  pl.BlockSpec((sb, 2, NSA_KV_GROUPS, dh, wb), lambda b: (b, 0, 0, 0, 0)),
                  pl.BlockSpec((sb, 1, 4 * dh), lambda b: (b, 0, 0)),
                  pl.BlockSpec((NSA_HEADS, nblk), lambda b: (0, 0)),
                  pl.BlockSpec((NSA_HEADS, wb), lambda b: (0, 0)),
                  pl.BlockSpec((NSA_HEADS, 1), lambda b: (0, 0))],
        out_specs=[pl.BlockSpec((sb, NSA_HEADS, dh), lambda b: (b, 0, 0)),
                   pl.BlockSpec((sb, NSA_HEADS, dh), lambda b: (b, 0, 0)),
                   pl.BlockSpec((sb, NSA_KV_GROUPS, nblk), lambda b: (b, 0, 0))],
        out_shape=[jax.ShapeDtypeStruct((db, NSA_HEADS, dh), f32),
                   jax.ShapeDtypeStruct((db, NSA_HEADS, dh), f32),
                   jax.ShapeDtypeStruct((db, NSA_KV_GROUPS, nblk), i32)],
        compiler_params=_cparams(("parallel",)),
        name="nsa_sample_scores",
    )(q3, kcvc, win_t, kvwin_new.reshape(db, 1, 4 * dh), t_cmp, t_win, b0)

    n_sel = NSA_TOPN - 1
    sel_flat = sel[:, :, :n_sel].reshape(-1)
    per_page = page // NSA_BLOCK

    def page_spec(j, plane):
        def imap(b, g, s, p):
            nb = s[(b * NSA_KV_GROUPS + g) * n_sel + j]
            return (p[b * n_pages + nb // per_page], plane, g, 0, 0)
        return pl.BlockSpec((1, 1, 1, dh, page), imap)

    const = lambda shape: pl.BlockSpec(shape, lambda b, g, s, p: (0,) * len(shape))
    per_b = lambda shape: pl.BlockSpec((1,) + shape, lambda b, g, s, p: (b,) + (0,) * len(shape))
    o = pl.pallas_call(
        functools.partial(_nsa_st_sel_kernel, n_sel=n_sel, past=past),
        grid_spec=pltpu.PrefetchScalarGridSpec(
            num_scalar_prefetch=2,
            grid=(db, NSA_KV_GROUPS),
            in_specs=[per_b((NSA_HEADS, dh)), const((REL_BUCKETS, NSA_HEADS, 1)), per_b((1, 8 * dh)),
                      per_b((NSA_HEADS, dh)), per_b((NSA_HEADS, dh)), per_b((NSA_HEADS, 3))]
                     + [page_spec(j, 2) for j in range(n_sel)] + [page_spec(j, 3) for j in range(n_sel)],
            out_specs=per_b((NSA_HEADS, dh))),
        out_shape=jax.ShapeDtypeStruct((db, NSA_HEADS, dh), f32),
        compiler_params=_cparams(("parallel", "arbitrary")),
        name="nsa_sample_selected",
    )(sel_flat, page_table.reshape(-1), q3, rb.reshape(REL_BUCKETS, NSA_HEADS, 1), kv4_new.reshape(db, 1, 8 * dh),
      o_c, o_w, gates, *([cache_t] * (2 * n_sel)))
    return o.reshape(db, NSA_HEADS * dh)


def _gelu_tanh(x):
    return 0.5 * x * (1.0 + jnp.tanh(math.sqrt(2.0 / math.pi) * (x + 0.044715 * (x * x * x))))


def _lru_gates(xc, wa_ref, wx_ref, ba_ref, bx_ref, lam_ref, prec):
    r_parts, i_parts = [], []
    for n in range(RNN_BLOCKS):
        xb = xc[:, n * RNN_BW:(n + 1) * RNN_BW]
        r_parts.append(_dotp(xb, wa_ref[n], prec))
        i_parts.append(_dotp(xb, wx_ref[n], prec))
    r = jax.nn.sigmoid(jnp.concatenate(r_parts, axis=1) + ba_ref[...])
    i = jax.nn.sigmoid(jnp.concatenate(i_parts, axis=1) + bx_ref[...])
    log_a = -RG_C * r * _softplus(-lam_ref[...])
    a = jnp.exp(log_a)
    t = jnp.tanh(log_a)
    b = jnp.sqrt(jnp.maximum(-2.0 * t / (1.0 - t), 0.0)) * (i * xc)
    return a, b


def _lru_kernel(rec_ref, gate_ref, cw_ref, cb_ref, wa_ref, wx_ref, ba_ref, bx_ref, lam_ref, cinit_ref, h0_ref,
                y_ref, hfin_ref, xc_scr, a_scr, b_scr, hs_scr, h_scr, *, tb):
    j = pl.program_id(1)

    @pl.when(j == 0)
    def _():
        xc_scr[0:8, :] = cinit_ref[0]
        h_scr[...] = h0_ref[0]

    x = rec_ref[0]
    xc_scr[8:8 + tb, :] = x
    xc = xc_scr[5:5 + tb, :] * cw_ref[0:1, :]
    xc = xc + xc_scr[6:6 + tb, :] * cw_ref[1:2, :]
    xc = xc + xc_scr[7:7 + tb, :] * cw_ref[2:3, :]
    xc = xc + x * cw_ref[3:4, :]
    xc = xc + cb_ref[...]
    xc_scr[0:8, :] = xc_scr[tb:tb + 8, :]
    a, b = _lru_gates(xc, wa_ref, wx_ref, ba_ref, bx_ref, lam_ref, 1)
    a_scr[...] = a
    b_scr[...] = b

    def step(t, h):
        h = a_scr[pl.ds(t, 1), :] * h + b_scr[pl.ds(t, 1), :]
        hs_scr[pl.ds(t, 1), :] = h
        return h

    h = lax.fori_loop(0, tb, step, h_scr[...], unroll=8)
    h_scr[...] = h
    y_ref[0] = _gelu_tanh(gate_ref[0]) * hs_scr[...]

    @pl.when(j == pl.num_programs(1) - 1)
    def _():
        hfin_ref[0] = h


def _lru_prompt(rec, gate, conv_w, conv_b, wa, wx, ba, bx, lam, *, tb):
    bsz, t, w = rec.shape
    row = lambda a: a.reshape(1, w).astype(f32)
    cinit = jnp.zeros((bsz, 8, w), f32)
    h0 = jnp.zeros((bsz, 1, w), f32)
    full = lambda shape: pl.BlockSpec(shape, lambda b, j: (0,) * len(shape))
    return pl.pallas_call(
        functools.partial(_lru_kernel, tb=tb),
        grid=(bsz, t // tb),
        in_specs=[pl.BlockSpec((1, tb, w), lambda b, j: (b, j, 0)),
                  pl.BlockSpec((1, tb, w), lambda b, j: (b, j, 0)),
                  full((RNN_CONV, w)), full((1, w)),
                  full((RNN_BLOCKS, RNN_BW, RNN_BW)), full((RNN_BLOCKS, RNN_BW, RNN_BW)),
                  full((1, w)), full((1, w)), full((1, w)),
                  pl.BlockSpec((1, 8, w), lambda b, j: (b, 0, 0)),
                  pl.BlockSpec((1, 1, w), lambda b, j: (b, 0, 0))],
        out_specs=[pl.BlockSpec((1, tb, w), lambda b, j: (b, j, 0)),
                   pl.BlockSpec((1, 1, w), lambda b, j: (b, 0, 0))],
        out_shape=[jax.ShapeDtypeStruct((bsz, t, w), f32), jax.ShapeDtypeStruct((bsz, 1, w), f32)],
        scratch_shapes=[pltpu.VMEM((tb + 8, w), f32), pltpu.VMEM((tb, w), f32), pltpu.VMEM((tb, w), f32),
                        pltpu.VMEM((tb, w), f32), pltpu.VMEM((1, w), f32)],
        compiler_params=_cparams(("parallel", "arbitrary")),
        name="rglru",
    )(rec, gate, conv_w.astype(f32), row(conv_b), wa.astype(f32), wx.astype(f32), row(ba), row(bx), row(lam),
      cinit, h0)


def _lru_step_kernel(rec_ref, gate_ref, b0_ref, b1_ref, b2_ref, cw_ref, cb_ref, wa_ref, wx_ref, ba_ref, bx_ref,
                     lam_ref, h0_ref, y_ref, h_ref):
    xc = b0_ref[...] * cw_ref[0:1, :]
    xc = xc + b1_ref[...] * cw_ref[1:2, :]
    xc = xc + b2_ref[...] * cw_ref[2:3, :]
    xc = xc + rec_ref[...] * cw_ref[3:4, :]
    xc = xc + cb_ref[...]
    a, b = _lru_gates(xc, wa_ref, wx_ref, ba_ref, bx_ref, lam_ref, 3)
    h = a * h0_ref[...] + b
    h_ref[...] = h
    y_ref[...] = _gelu_tanh(gate_ref[...]) * h


def _lru_sample(rec, gate, conv_buf, h0, conv_w, conv_b, wa, wx, ba, bx, lam):
    db, w = rec.shape
    row = lambda a: a.reshape(1, w).astype(f32)
    buf = conv_buf.astype(f32)
    return pl.pallas_call(
        _lru_step_kernel,
        out_shape=[jax.ShapeDtypeStruct((db, w), f32), jax.ShapeDtypeStruct((db, w), f32)],
        compiler_params=pltpu.CompilerParams(vmem_limit_bytes=VMEM_LIMIT),
        name="rglru_step",
    )(rec, gate, buf[:, 0], buf[:, 1], buf[:, 2], conv_w.astype(f32), row(conv_b), wa.astype(f32), wx.astype(f32),
      row(ba), row(bx), row(lam), h0.astype(f32))


def _top2_kernel(l_ref, e_ref, g_ref, cnt_ref, cnt_scr):
    @pl.when(pl.program_id(0) == 0)
    def _():
        cnt_scr[...] = jnp.zeros_like(cnt_scr)

    lg = l_ref[...]
    tm, w = lg.shape
    lane = lax.broadcasted_iota(i32, lg.shape, 1)
    lg = jnp.where(lane < N_EXPERTS, lg, -jnp.inf)
    m1 = jnp.max(lg, axis=-1, keepdims=True)
    i1 = jnp.min(jnp.where(lg == m1, lane, w), axis=-1, keepdims=True)
    lg2 = jnp.where(lane == i1, -jnp.inf, lg)
    m2 = jnp.max(lg2, axis=-1, keepdims=True)
    i2 = jnp.min(jnp.where(lg2 == m2, lane, w), axis=-1, keepdims=True)
    e2 = jnp.exp(m2 - m1)
    den = 1.0 + e2
    g_ref[...] = jnp.where(lane == 0, 1.0 / den, jnp.where(lane == 1, e2 / den, 0.0))
    hit1, hit2 = lane == i1, lane == i2
    routed = (hit1 | hit2).astype(bf16)
    r_i = lax.broadcasted_iota(i32, (tm, tm), 0)
    c_i = lax.broadcasted_iota(i32, (tm, tm), 1)
    before = jnp.dot((c_i < r_i).astype(bf16), routed, preferred_element_type=f32) + cnt_scr[...]
    rank1 = jnp.sum(jnp.where(hit1, before, 0.0), axis=-1, keepdims=True).astype(i32)
    rank2 = jnp.sum(jnp.where(hit2, before, 0.0), axis=-1, keepdims=True).astype(i32)
    e_ref[...] = jnp.where(lane == 0, i1, jnp.where(lane == 1, i2, jnp.where(lane == 2, rank1,
                                                                               jnp.where(lane == 3, rank2, 0))))
    cnt_scr[...] += jnp.sum(routed.astype(f32), axis=0, keepdims=True)
    cnt_ref[...] = cnt_scr[...].astype(i32)


def _top2(logits, *, tm):
    n, w = logits.shape
    return pl.pallas_call(
        _top2_kernel,
        grid=(n // tm,),
        in_specs=[pl.BlockSpec((tm, w), lambda i: (i, 0))],
        out_specs=[pl.BlockSpec((tm, w), lambda i: (i, 0)), pl.BlockSpec((tm, w), lambda i: (i, 0)),
                   pl.BlockSpec((1, w), lambda i: (0, 0))],
        out_shape=[jax.ShapeDtypeStruct((n, w), i32), jax.ShapeDtypeStruct((n, w), f32),
                   jax.ShapeDtypeStruct((1, w), i32)],
        scratch_shapes=[pltpu.VMEM((1, w), f32)],
        compiler_params=_cparams(("arbitrary",)),
        name="moe_top2",
    )(logits)


def _moe_kernel(be_ref, nu_ref, nv_ref, xs_ref, w1_ref, w3_ref, w2_ref, o_ref, acc_scr):
    del be_ref
    i = pl.program_id(0)
    f = pl.program_id(1)
    last = pl.num_programs(1) - 1
    used = i < nu_ref[0]
    half = xs_ref.shape[0] // 2
    top_only = nv_ref[i] <= half

    @pl.when(used & (f == 0))
    def _():
        acc_scr[...] = jnp.zeros_like(acc_scr)

    def update(rows):
        x = xs_ref[rows, :]
        act = _silu(_mm(x, w1_ref[0])) * _mm(x, w3_ref[0])
        acc_scr[rows, :] += _mm(act, w2_ref[0])

    @pl.when(used & jnp.logical_not(top_only))
    def _():
        update(slice(None))

    @pl.when(used & top_only)
    def _():
        update(slice(0, half))

    @pl.when(used & (f == last))
    def _():
        o_ref[...] = acc_scr[...]

    @pl.when(jnp.logical_not(used) & (f == last))
    def _():
        o_ref[...] = jnp.zeros_like(o_ref)


def _moe_experts(xs, blk_e, n_used, n_valid, w1, w3, w2, *, tm, tf):
    rows, d = xs.shape
    fdim = w1.shape[2]
    nf = fdim // tf

    def wcol(i, f, be, nu, nv):
        return (be[i], 0, jnp.where(i < nu[0], f, nf - 1))

    def wrow(i, f, be, nu, nv):
        return (be[i], jnp.where(i < nu[0], f, nf - 1), 0)

    return pl.pallas_call(
        _moe_kernel,
        grid_spec=pltpu.PrefetchScalarGridSpec(
            num_scalar_prefetch=3,
            grid=(rows // tm, nf),
            in_specs=[pl.BlockSpec((tm, d), lambda i, f, be, nu, nv: (i, 0)),
                      pl.BlockSpec((1, d, tf), wcol),
                      pl.BlockSpec((1, d, tf), wcol),
                      pl.BlockSpec((1, tf, d), wrow)],
            out_specs=pl.BlockSpec((tm, d), lambda i, f, be, nu, nv: (i, 0)),
            scratch_shapes=[pltpu.VMEM((tm, d), f32)]),
        out_shape=jax.ShapeDtypeStruct((rows, d), f32),
        compiler_params=_cparams(("arbitrary", "arbitrary")),
        name="moe_experts",
    )(blk_e, n_used, n_valid, xs, w1, w3, w2)


def _combine_kernel(x_ref, gate_ref, y0_ref, y1_ref, g_ref, nw_ref, o_ref):
    gw = g_ref[...]
    y = y0_ref[...] * gw[:, 0:1] + y1_ref[...] * gw[:, 1:2]
    x = x_ref[0] + gate_ref[0] * y
    o_ref[0] = x * lax.rsqrt(jnp.mean(x * x, axis=-1, keepdims=True) + EPS) * nw_ref[...]


def _moe_combine(x, gate, y0, y1, gw, norm_w, *, tm, row0):
    bsz, t, d = x.shape
    assert row0 % tm == 0
    tok = pl.BlockSpec((1, tm, d), lambda b, i: (b, i, 0))
    flat = lambda w: pl.BlockSpec((tm, w), lambda b, i: (row0 // tm + b * (t // tm) + i, 0))
    return pl.pallas_call(
        _combine_kernel,
        grid=(bsz, t // tm),
        in_specs=[tok, _mod_spec(gate, tm), flat(d), flat(d), flat(gw.shape[-1]),
                  pl.BlockSpec((1, d), lambda b, i: (0, 0))],
        out_specs=tok,
        out_shape=jax.ShapeDtypeStruct((bsz, t, d), f32),
        compiler_params=_cparams(("parallel", "parallel")),
        name="moe_combine",
    )(x, gate, y0, y1, gw, norm_w.reshape(1, d).astype(f32))


MOE_TM = 1024
MOE_TF = 512


def _moe_dispatch(e_idx, rank, counts, n_tok):
    tm = MOE_TM
    n_assign = n_tok * TOP_K
    padded = (counts + tm - 1) // tm * tm
    pend = jnp.cumsum(padded)
    pstart = pend - padded
    experts = jnp.arange(N_EXPERTS, dtype=i32)
    dest = jnp.sum(jnp.where(e_idx[:, :, None] == experts, pstart, 0), axis=-1) + rank
    n_blocks = -(-n_assign // tm) + N_EXPERTS
    rows = n_blocks * tm
    tok = jnp.broadcast_to(jnp.arange(n_tok, dtype=i32)[:, None], (n_tok, TOP_K))
    row_tok = (jnp.arange(rows, dtype=i32) % n_tok).at[dest.reshape(-1)].set(
        tok.reshape(-1), unique_indices=True, mode='promise_in_bounds')
    n_used = (pend[-1] // tm).astype(i32)
    blk = jnp.minimum(jnp.arange(n_blocks, dtype=i32), n_used - 1) * tm
    blk_e = jnp.minimum(jnp.sum((blk[:, None] >= pend[None, :]).astype(i32), axis=1), N_EXPERTS - 1)
    is_e = blk_e[:, None] == experts[None, :]
    first_row = jnp.sum(jnp.where(is_e, pstart, 0), axis=1)
    n_valid = jnp.clip(jnp.sum(jnp.where(is_e, counts, 0), axis=1) - (blk - first_row), 0, tm)
    return dest, row_tok, blk_e, n_used.reshape(1), n_valid.astype(i32)


def _w_in0_layout(w_in0):
    c_qkv = GDN_CONV_CH
    c_z = c_qkv + GDN_HEADS * GDN_DV
    c_ab = c_z + 2 * GDN_HEADS
    c_q = c_ab + NSA_HEADS * NSA_DH
    c_kv = c_q + 6 * NSA_KV_GROUPS * NSA_DH
    c_g = c_kv + 3 * NSA_HEADS
    assert c_g == w_in0.shape[1]
    n_misc = 2 * GDN_HEADS + 3 * NSA_HEADS
    w = jnp.concatenate([w_in0[:, :c_z], w_in0[:, c_ab:c_kv], w_in0[:, c_z:c_ab], w_in0[:, c_kv:c_g],
                         jnp.zeros((w_in0.shape[0], MISC_W - n_misc), w_in0.dtype)], axis=1)
    widths = (GDN_CONV_CH, GDN_HEADS * GDN_DV, NSA_HEADS * NSA_DH, 4 * NSA_KV_GROUPS * NSA_DH,
              2 * NSA_KV_GROUPS * NSA_DH, MISC_W)
    splits, s = [], 0
    for wd in widths:
        splits.append((s, s + wd))
        s += wd
    return w, tuple(splits)


def kernel(x_prompt, x_sample, c_prompt, c_sample, cache_nsa_kv, cache_nsa_win, state_gdn, state_gdn_conv, state_lru, state_lru_conv, page_table, rel_bias, w_ada, b_ada, norm_mix, norm_ffn, norm_final, w_in0, gdn_conv_w, gdn_a_log, gdn_dt_bias, gdn_norm_w, w_out0, ffn_w_gate, ffn_w_up, ffn_w_down, w_in1, lru_conv_w, lru_conv_b, lru_wa, lru_ba, lru_wx, lru_bx, lru_lambda, w_out1, moe_router, moe_w1, moe_w3, moe_w2):
    bsz, seq, d = x_prompt.shape
    db = x_sample.shape[0]
    assert x_sample.shape[1] == 1
    dh = NSA_DH

    n_c = bsz + db
    n_c_pad = -(-n_c // 8) * 8
    c_all = jnp.concatenate([c_prompt, c_sample, jnp.zeros((n_c_pad - n_c, d), f32)], axis=0)
    mods = _adaln(c_all, w_ada, b_ada).reshape(2, n_c_pad, N_MOD, d)
    mod_p = [[mods[l, :bsz, k].reshape(bsz, 1, d) for k in range(N_MOD)] for l in range(2)]
    mod_s = [[mods[l, bsz:n_c, k].reshape(1, db, d) for k in range(N_MOD)] for l in range(2)]

    w0, splits0 = _w_in0_layout(w_in0)
    splits1 = ((0, RNN_WIDTH), (RNN_WIDTH, 2 * RNN_WIDTH))
    router = jnp.concatenate([moe_router, jnp.zeros((d, MISC_W - N_EXPERTS), f32)], axis=1)
    bf = lambda w: w.astype(bf16)

    tm = 512
    xp = x_prompt
    qkv, z, q_b, kv4, kvwin, misc = _mod_matmul(xp, norm_mix[0], mod_p[0][0], mod_p[0][1], bf(w0), splits0,
                                                tm=tm, prec=1)
    o_a, p_gdn = _gdn(qkv, z, misc, gdn_conv_w, jnp.zeros((bsz, GDN_CONV - 1, GDN_CONV_CH), f32), gdn_a_log,
                      gdn_dt_bias, gdn_norm_w, jnp.zeros((bsz, GDN_HEADS, GDN_DK, GDN_DV), f32), tb=256, n_valid=seq)
    p_gdn_conv = qkv[:, seq - (GDN_CONV - 1):]
    o_b = _nsa_prompt(q_b, misc, kv4, kvwin, rel_bias)
    p_nsa_kv = kv4.reshape(bsz, seq, 4, NSA_KV_GROUPS, dh)
    keep = min(NSA_WINDOW, seq)
    p_nsa_win = kvwin[:, seq - keep:].reshape(bsz, keep, 2, NSA_KV_GROUPS, dh)
    xp = _proj_residual([o_a, o_b], bf(w_out0), xp, mod_p[0][2], tm=tm, prec=1)
    xp = _ffn(xp, norm_ffn[0], mod_p[0][3], mod_p[0][4], mod_p[0][5], bf(ffn_w_gate), bf(ffn_w_up), bf(ffn_w_down),
              tm=tm, tf=FFN_DIM // 2, prec=1)
    gate_br, rec_br = _mod_matmul(xp, norm_mix[1], mod_p[1][0], mod_p[1][1], bf(w_in1), splits1, tm=tm, prec=1)
    y_in, p_lru = _lru_prompt(rec_br, gate_br, lru_conv_w, lru_conv_b, lru_wa, lru_wx, lru_ba, lru_bx, lru_lambda,
                              tb=256)
    p_lru_conv = rec_br[:, seq - (RNN_CONV - 1):]
    xp = _proj_residual([y_in], bf(w_out1), xp, mod_p[1][2], tm=tm, prec=1)
    logit_p, h_p = _mod_matmul(xp, norm_ffn[1], mod_p[1][3], mod_p[1][4], router, ((0, MISC_W),), tm=tm, prec=3,
                               emit_h=f32)

    xs = x_sample.reshape(1, db, d)
    qkv_s, z_s, q_s, kv4_s, kvwin_s, misc_s = _mod_matmul(xs, norm_mix[0], mod_s[0][0], mod_s[0][1], w0, splits0,
                                                          tm=db, prec=3)
    c = GDN_CHUNK
    tpad = lambda a: jnp.pad(a.reshape(db, 1, a.shape[-1]), ((0, 0), (0, c - 1), (0, 0)))
    o_a_s, s_gdn = _gdn(tpad(qkv_s), tpad(z_s), tpad(misc_s), gdn_conv_w, state_gdn_conv, gdn_a_log, gdn_dt_bias,
                        gdn_norm_w, state_gdn, tb=c, n_valid=1)
    o_a_s = o_a_s[:, 0].reshape(1, db, GDN_HEADS * GDN_DV)
    s_gdn_conv = jnp.concatenate([state_gdn_conv[:, 1:], qkv_s.reshape(db, 1, GDN_CONV_CH)], axis=1)
    gates_s = jax.nn.sigmoid(misc_s[0, :, 2 * GDN_HEADS:2 * GDN_HEADS + 3 * NSA_HEADS]).reshape(db, NSA_HEADS, 3)
    o_b_s = _nsa_sample_t(q_s[0], gates_s, kv4_s[0], kvwin_s[0], cache_nsa_kv, cache_nsa_win, page_table, rel_bias)
    s_nsa_kv = kv4_s.reshape(db, 1, 4, NSA_KV_GROUPS, dh)
    s_nsa_win = jnp.concatenate([cache_nsa_win[:, 1:],
                                 kvwin_s.reshape(db, 1, 2, NSA_KV_GROUPS, dh).astype(cache_nsa_win.dtype)], axis=1)
    xs = _proj_residual([o_a_s, o_b_s.reshape(1, db, NSA_HEADS * dh)], w_out0, xs, mod_s[0][2], tm=db, prec=3)
    xs = _ffn(xs, norm_ffn[0], mod_s[0][3], mod_s[0][4], mod_s[0][5], ffn_w_gate, ffn_w_up, ffn_w_down,
              tm=db, tf=256, prec=3)
    gate_s, rec_s = _mod_matmul(xs, norm_mix[1], mod_s[1][0], mod_s[1][1], w_in1, splits1, tm=db, prec=3)
    y_in_s, s_lru = _lru_sample(rec_s[0], gate_s[0], state_lru_conv, state_lru, lru_conv_w, lru_conv_b, lru_wa, lru_wx,
                                lru_ba, lru_bx, lru_lambda)
    s_lru_conv = jnp.concatenate([state_lru_conv[:, 1:], rec_s.reshape(db, 1, RNN_WIDTH)], axis=1)
    xs = _proj_residual([y_in_s.reshape(1, db, RNN_WIDTH)], w_out1, xs, mod_s[1][2], tm=db, prec=3)
    logit_s, h_s = _mod_matmul(xs, norm_ffn[1], mod_s[1][3], mod_s[1][4], router, ((0, MISC_W),), tm=db, prec=3,
                               emit_h=f32)

    n_p = bsz * seq
    n_tok = n_p + db
    logits = jnp.concatenate([logit_p.reshape(n_p, MISC_W), logit_s.reshape(db, MISC_W)], axis=0)
    h_all = jnp.concatenate([h_p.reshape(n_p, d), h_s.reshape(db, d)], axis=0)
    route, gw, cnt = _top2(logits, tm=max(t for t in range(8, 1025, 8) if n_tok % t == 0))
    dest, row_tok, blk_e, n_used, n_valid = _moe_dispatch(route[:, :TOP_K], route[:, TOP_K:2 * TOP_K],
                                                          cnt[0, :N_EXPERTS], n_tok)
    yb = _moe_experts(h_all[row_tok], blk_e, n_used, n_valid, moe_w1, moe_w3, moe_w2, tm=MOE_TM, tf=MOE_TF)
    y0, y1 = yb[dest[:, 0]], yb[dest[:, 1]]
    y_prompt = _moe_combine(xp, mod_p[1][5], y0, y1, gw, norm_final, tm=tm, row0=0)
    y_sample = _moe_combine(xs, mod_s[1][5], y0, y1, gw, norm_final, tm=db, row0=n_p)

    return (y_prompt, y_sample.reshape(db, 1, d),
            p_nsa_kv, p_nsa_win, p_gdn, p_gdn_conv, p_lru.reshape(bsz, RNN_WIDTH), p_lru_conv,
            s_nsa_kv, s_nsa_win, s_gdn, s_gdn_conv, s_lru, s_lru_conv)
```

```python
import functools
import math

import numpy as np
import jax
import jax.numpy as jnp
from jax import lax
from jax.experimental import pallas as pl
from jax.experimental.pallas import tpu as pltpu

f32 = jnp.float32
bf16 = jnp.bfloat16
i32 = jnp.int32

D_MODEL = 1024
EPS = 1e-6
N_MOD = 6
GDN_HEADS = 4
GDN_DK = 128
GDN_DV = 128
GDN_CONV = 4
GDN_CHUNK = 64
GDN_CONV_CH = GDN_HEADS * (2 * GDN_DK + GDN_DV)
NSA_HEADS = 8
NSA_KV_GROUPS = 2
NSA_HPG = NSA_HEADS // NSA_KV_GROUPS
NSA_DH = 64
NSA_BLOCK = 64
NSA_TOPN = 16
NSA_LOCAL = 2
NSA_WINDOW = 512
NSA_QBLOCK = 128
NSA_FORCE = 1.0e4
REL_BUCKETS = 32
REL_MAX_DIST = 2048
RNN_WIDTH = D_MODEL
RNN_BLOCKS = 8
RNN_BW = RNN_WIDTH // RNN_BLOCKS
RNN_CONV = 4
RG_C = 8.0
FFN_DIM = 2816
N_EXPERTS = 8
TOP_K = 2
EXPERT_DIM = 3584

NEG = -1e30
PAIR = 2 * NSA_BLOCK
NEAR_PAIRS = 13
MISC_W = 128
VMEM_LIMIT = 56 * 1024 * 1024


def _cparams(sem, vmem=VMEM_LIMIT):
    return pltpu.CompilerParams(dimension_semantics=sem, vmem_limit_bytes=vmem)


def _mm(a, b):
    return jnp.dot(a.astype(bf16), b.astype(bf16), preferred_element_type=f32)


def _mm_nt(a, b):
    return lax.dot_general(a.astype(bf16), b.astype(bf16), (((1,), (1,)), ((), ())),
                           preferred_element_type=f32)


def _split2(a):
    hi = a.astype(bf16)
    lo = (a - hi.astype(f32)).astype(bf16)
    return hi, lo


def _mm3(a, b):
    ah, al = _split2(a)
    bh, bl = _split2(b)
    return (jnp.dot(ah, bh, preferred_element_type=f32) + jnp.dot(ah, bl, preferred_element_type=f32)
            + jnp.dot(al, bh, preferred_element_type=f32))


def _mm3_nt(a, b):
    ah, al = _split2(a)
    bh, bl = _split2(b)
    dn = (((1,), (1,)), ((), ()))
    return (lax.dot_general(ah, bh, dn, preferred_element_type=f32)
            + lax.dot_general(ah, bl, dn, preferred_element_type=f32)
            + lax.dot_general(al, bh, dn, preferred_element_type=f32))


def _mm_01(m01, a):
    hi = a.astype(bf16)
    r1 = a - hi.astype(f32)
    mid = r1.astype(bf16)
    lo = (r1 - mid.astype(f32)).astype(bf16)
    return (jnp.dot(m01, hi, preferred_element_type=f32) + jnp.dot(m01, mid, preferred_element_type=f32)
            + jnp.dot(m01, lo, preferred_element_type=f32))


def _dotp(a, b, prec):
    return _mm3(a, b) if prec == 3 else _mm(a, b)


def _silu(x):
    return x * jax.nn.sigmoid(x)


def _softplus(x):
    return jnp.maximum(x, 0.0) + jnp.log1p(jnp.exp(-jnp.abs(x)))


def _modulate(x, gain, shift, scale):
    r = lax.rsqrt(jnp.mean(x * x, axis=-1, keepdims=True) + EPS)
    return x * r * gain * (1.0 + scale) + shift


def _ada_kernel(c_ref, w_ref, b_ref, o_ref):
    o_ref[0] = _mm3(_silu(c_ref[...]), w_ref[0]) + b_ref[0]


def _adaln(c_all, w_ada, b_ada):
    rows = c_all.shape[0]
    depth, d, n = w_ada.shape
    tn = 1536
    return pl.pallas_call(
        _ada_kernel,
        grid=(depth, n // tn),
        in_specs=[pl.BlockSpec((rows, d), lambda l, j: (0, 0)),
                  pl.BlockSpec((1, d, tn), lambda l, j: (l, 0, j)),
                  pl.BlockSpec((1, 1, tn), lambda l, j: (l, 0, j))],
        out_specs=pl.BlockSpec((1, rows, tn), lambda l, j: (l, 0, j)),
        out_shape=jax.ShapeDtypeStruct((depth, rows, n), f32),
        compiler_params=_cparams(("arbitrary", "arbitrary")),
        name="adaln",
    )(c_all, w_ada, b_ada.reshape(depth, 1, n))


def _mod_spec(mod, tm):
    r = mod.shape[1]
    if r == 1:
        return pl.BlockSpec((1, 1, mod.shape[2]), lambda b, i: (b, 0, 0))
    return pl.BlockSpec((1, tm, mod.shape[2]), lambda b, i: (b, i, 0))


def _modmm_kernel(x_ref, gain_ref, shift_ref, scale_ref, w_ref, *o_refs, splits, prec, emit_h):
    h = _modulate(x_ref[0], gain_ref[...], shift_ref[0], scale_ref[0])
    if emit_h:
        o_refs[-1][0] = h.astype(o_refs[-1].dtype)
    hh = _split2(h) if prec == 3 else h.astype(bf16)
    for o_ref, (a, b) in zip(o_refs, splits):
        w = w_ref[:, a:b]
        if prec == 3:
            wh, wl = _split2(w)
            acc = (jnp.dot(hh[0], wh, preferred_element_type=f32) + jnp.dot(hh[0], wl, preferred_element_type=f32)
                   + jnp.dot(hh[1], wh, preferred_element_type=f32))
        else:
            acc = jnp.dot(hh, w, preferred_element_type=f32)
        o_ref[0] = acc


def _mod_matmul(x, gain, shift, scale, w, splits, *, tm, prec, emit_h=None):
    bsz, t, d = x.shape
    out_shape = [jax.ShapeDtypeStruct((bsz, t, b - a), f32) for a, b in splits]
    out_specs = [pl.BlockSpec((1, tm, b - a), lambda bi, i: (bi, i, 0)) for a, b in splits]
    if emit_h is not None:
        out_shape.append(jax.ShapeDtypeStruct((bsz, t, d), emit_h))
        out_specs.append(pl.BlockSpec((1, tm, d), lambda bi, i: (bi, i, 0)))
    return pl.pallas_call(
        functools.partial(_modmm_kernel, splits=tuple(splits), prec=prec, emit_h=emit_h is not None),
        grid=(bsz, t // tm),
        in_specs=[pl.BlockSpec((1, tm, d), lambda bi, i: (bi, i, 0)),
                  pl.BlockSpec((1, d), lambda bi, i: (0, 0)),
                  _mod_spec(shift, tm), _mod_spec(scale, tm),
                  pl.BlockSpec(w.shape, lambda bi, i: (0, 0))],
        out_specs=out_specs,
        out_shape=out_shape,
        compiler_params=_cparams(("parallel", "parallel")),
        name="mod_matmul",
    )(x, gain.reshape(1, d), shift, scale, w)


def _projres_kernel(*refs, n_lhs, ksplits, prec, final_norm):
    lhs = refs[:n_lhs]
    w_ref, x_ref, gate_ref = refs[n_lhs:n_lhs + 3]
    o_ref = refs[-1]
    acc = None
    for l_ref, (a, b) in zip(lhs, ksplits):
        part = _dotp(l_ref[0], w_ref[a:b, :], prec)
        acc = part if acc is None else acc + part
    y = x_ref[0] + gate_ref[0] * acc
    if final_norm:
        nw_ref = refs[n_lhs + 3]
        y = y * lax.rsqrt(jnp.mean(y * y, axis=-1, keepdims=True) + EPS) * nw_ref[...]
    o_ref[0] = y


def _proj_residual(lhs_list, w, x, gate, *, tm, prec, norm_w=None):
    bsz, t, d = x.shape
    ksplits, k0 = [], 0
    for l in lhs_list:
        ksplits.append((k0, k0 + l.shape[-1]))
        k0 += l.shape[-1]
    in_specs = [pl.BlockSpec((1, tm, l.shape[-1]), lambda bi, i: (bi, i, 0)) for l in lhs_list]
    in_specs += [pl.BlockSpec(w.shape, lambda bi, i: (0, 0)),
                 pl.BlockSpec((1, tm, d), lambda bi, i: (bi, i, 0)),
                 _mod_spec(gate, tm)]
    args = list(lhs_list) + [w, x, gate]
    if norm_w is not None:
        in_specs.append(pl.BlockSpec((1, d), lambda bi, i: (0, 0)))
        args.append(norm_w.reshape(1, d))
    return pl.pallas_call(
        functools.partial(_projres_kernel, n_lhs=len(lhs_list), ksplits=tuple(ksplits), prec=prec,
                          final_norm=norm_w is not None),
        grid=(bsz, t // tm),
        in_specs=in_specs,
        out_specs=pl.BlockSpec((1, tm, d), lambda bi, i: (bi, i, 0)),
        out_shape=jax.ShapeDtypeStruct((bsz, t, d), f32),
        compiler_params=_cparams(("parallel", "parallel")),
        name="proj_residual",
    )(*args)


def _ffn_kernel(x_ref, gain_ref, shift_ref, scale_ref, gate_ref, wg_ref, wu_ref, wd_ref, o_ref,
                h_scr, acc_scr, *, prec):
    f = pl.program_id(2)

    @pl.when(f == 0)
    def _():
        h_scr[...] = _modulate(x_ref[0], gain_ref[...], shift_ref[0], scale_ref[0]).astype(h_scr.dtype)
        acc_scr[...] = jnp.zeros_like(acc_scr)

    h = h_scr[...]
    act = _silu(_dotp(h, wg_ref[...], prec)) * _dotp(h, wu_ref[...], prec)
    acc_scr[...] += _dotp(act, wd_ref[...], prec)

    @pl.when(f == pl.num_programs(2) - 1)
    def _():
        o_ref[0] = x_ref[0] + gate_ref[0] * acc_scr[...]


def _ffn(x, gain, shift, scale, gate, wg, wu, wd, *, tm, tf, prec):
    bsz, t, d = x.shape
    fdim = wg.shape[1]
    mod_specs = []
    for mod in (shift, scale, gate):
        if mod.shape[1] == 1:
            mod_specs.append(pl.BlockSpec((1, 1, d), lambda b, i, f: (b, 0, 0)))
        else:
            mod_specs.append(pl.BlockSpec((1, tm, d), lambda b, i, f: (b, i, 0)))
    return pl.pallas_call(
        functools.partial(_ffn_kernel, prec=prec),
        grid=(bsz, t // tm, fdim // tf),
        in_specs=[pl.BlockSpec((1, tm, d), lambda b, i, f: (b, i, 0)),
                  pl.BlockSpec((1, d), lambda b, i, f: (0, 0)),
                  mod_specs[0], mod_specs[1], mod_specs[2],
                  pl.BlockSpec((d, tf), lambda b, i, f: (0, f)),
                  pl.BlockSpec((d, tf), lambda b, i, f: (0, f)),
                  pl.BlockSpec((tf, d), lambda b, i, f: (f, 0))],
        out_specs=pl.BlockSpec((1, tm, d), lambda b, i, f: (b, i, 0)),
        out_shape=jax.ShapeDtypeStruct((bsz, t, d), f32),
        scratch_shapes=[pltpu.VMEM((tm, d), f32 if prec == 3 else bf16), pltpu.VMEM((tm, d), f32)],
        compiler_params=_cparams(("parallel", "parallel", "arbitrary")),
        name="ffn",
    )(x, gain.reshape(1, d), shift, scale, gate, wg, wu, wd)


def _gdn_kernel(qkv_ref, z_ref, misc_ref, cw_ref, cinit_ref, hp_ref, nw_ref, s0_ref,
                o_ref, sfin_ref, xc_scr, s_scr, *, tb, n_valid):
    j = pl.program_id(1)
    c = GDN_CHUNK
    nh = GDN_HEADS

    @pl.when(j == 0)
    def _():
        xc_scr[0:8, :] = cinit_ref[0]
        s_scr[...] = s0_ref[0]

    x = qkv_ref[0]
    xc_scr[8:8 + tb, :] = x
    y = xc_scr[5:5 + tb, :] * cw_ref[0:1, :]
    y = y + xc_scr[6:6 + tb, :] * cw_ref[1:2, :]
    y = y + xc_scr[7:7 + tb, :] * cw_ref[2:3, :]
    y = y + x * cw_ref[3:4, :]
    xc_scr[0:8, :] = xc_scr[tb:tb + 8, :]
    y = _silu(y)

    misc = misc_ref[0]
    row = lax.broadcasted_iota(i32, (tb, MISC_W), 0) + j * tb
    live = row < n_valid
    log_a = jnp.where(live, hp_ref[0:1, :] * _softplus(misc + hp_ref[1:2, :]), 0.0)
    beta = jnp.where(live, jax.nn.sigmoid(misc), 0.0)

    r_i = lax.broadcasted_iota(i32, (tb, tb), 0)
    c_i = lax.broadcasted_iota(i32, (tb, tb), 1)
    sh = int(math.log2(c))
    ltri = (((r_i >> sh) == (c_i >> sh)) & (c_i <= r_i)).astype(bf16)
    g = _mm_01(ltri, log_a)
    g_t = g.T

    ri = lax.broadcasted_iota(i32, (c, c), 0)
    ci = lax.broadcasted_iota(i32, (c, c), 1)
    incl = ci <= ri
    strict = ci < ri
    eye = (ci == ri).astype(f32)
    quad = []
    for lvl in range(int(math.log2(c))):
        quad.append(((ri >> (lvl + 1)) == (ci >> (lvl + 1))) & (((ri >> lvl) & 1) == 1) & (((ci >> lvl) & 1) == 0))

    nchunk = tb // c
    units = [(n, h) for n in range(nchunk) for h in range(nh)]

    def stack(fn):
        return jnp.stack([fn(n * c, h) for n, h in units], axis=0)

    def bmm(eq, a_, b_):
        return jnp.einsum(eq, a_.astype(bf16), b_.astype(bf16), preferred_element_type=f32)

    q = stack(lambda r0, h: y[r0:r0 + c, h * GDN_DK:(h + 1) * GDN_DK])
    k = stack(lambda r0, h: y[r0:r0 + c, (nh + h) * GDN_DK:(nh + h + 1) * GDN_DK])
    v = stack(lambda r0, h: y[r0:r0 + c, 2 * nh * GDN_DK + h * GDN_DV:2 * nh * GDN_DK + (h + 1) * GDN_DV])
    q = q * lax.rsqrt(jnp.sum(q * q, axis=-1, keepdims=True) + EPS) * (GDN_DK ** -0.5)
    k = k * lax.rsqrt(jnp.sum(k * k, axis=-1, keepdims=True) + EPS)
    g_col = stack(lambda r0, h: g[r0:r0 + c, h:h + 1])
    g_row = stack(lambda r0, h: g_t[h:h + 1, r0:r0 + c])
    b_col = stack(lambda r0, h: beta[r0:r0 + c, nh + h:nh + h + 1])
    gam = jnp.where(incl, jnp.exp(jnp.where(incl, g_col - g_row, 0.0)), 0.0)
    kk = bmm('uid,ujd->uij', k, k)
    a = jnp.where(strict, b_col * gam * kk, 0.0)
    p = eye - jnp.where(quad[0], a, 0.0)
    for lvl in range(1, len(quad)):
        m = bmm('uij,ujk->uik', jnp.where(quad[lvl], a, 0.0), p)
        p = p - bmm('uij,ujk->uik', p, m)
    e_g = jnp.exp(g_col)
    sol = bmm('uij,ujd->uid', p, jnp.concatenate([b_col * v, (b_col * e_g) * k], axis=-1))
    vb, w = sol[:, :, :GDN_DV], sol[:, :, GDN_DV:]
    aqk = bmm('uid,ujd->uij', q, k) * gam
    qg = q * e_g
    g_last = g_col[:, c - 1:c, :]
    kd_t = jnp.swapaxes(k * jnp.exp(g_last - g_col), 1, 2)
    gc = jnp.exp(g_last)

    s = s_scr[...]
    for n in range(nchunk):
        sl = slice(n * nh, (n + 1) * nh)
        u = vb[sl] - bmm('hcd,hde->hce', w[sl], s)
        o = bmm('hcd,hde->hce', qg[sl], s) + bmm('hij,hje->hie', aqk[sl], u)
        s = gc[sl] * s + bmm('hdc,hce->hde', kd_t[sl], u)
        o = o * lax.rsqrt(jnp.mean(o * o, axis=-1, keepdims=True) + EPS) * nw_ref[...]
        for h in range(nh):
            zs = z_ref[0, n * c:(n + 1) * c, h * GDN_DV:(h + 1) * GDN_DV]
            o_ref[0, n * c:(n + 1) * c, h * GDN_DV:(h + 1) * GDN_DV] = o[h] * _silu(zs)
    s_scr[...] = s

    @pl.when(j == pl.num_programs(1) - 1)
    def _():
        sfin_ref[0] = s_scr[...]


def _gdn(qkv_raw, z, misc, conv_w, conv_buf, a_log, dt_bias, norm_w, s0, *, tb, n_valid):
    bsz, tp, ch = qkv_raw.shape
    cinit = jnp.concatenate([jnp.zeros((bsz, 5, ch), f32), conv_buf.astype(f32)], axis=1)
    hp = jnp.zeros((8, MISC_W), f32)
    hp = hp.at[0, :GDN_HEADS].set(-jnp.exp(a_log.astype(f32))).at[1, :GDN_HEADS].set(dt_bias.astype(f32))
    zw = GDN_HEADS * GDN_DV
    return pl.pallas_call(
        functools.partial(_gdn_kernel, tb=tb, n_valid=n_valid),
        grid=(bsz, tp // tb),
        in_specs=[pl.BlockSpec((1, tb, ch), lambda b, j: (b, j, 0)),
                  pl.BlockSpec((1, tb, zw), lambda b, j: (b, j, 0)),
                  pl.BlockSpec((1, tb, MISC_W), lambda b, j: (b, j, 0)),
                  pl.BlockSpec((GDN_CONV, ch), lambda b, j: (0, 0)),
                  pl.BlockSpec((1, 8, ch), lambda b, j: (b, 0, 0)),
                  pl.BlockSpec((8, MISC_W), lambda b, j: (0, 0)),
                  pl.BlockSpec((1, GDN_DV), lambda b, j: (0, 0)),
                  pl.BlockSpec((1, GDN_HEADS, GDN_DK, GDN_DV), lambda b, j: (b, 0, 0, 0))],
        out_specs=[pl.BlockSpec((1, tb, zw), lambda b, j: (b, j, 0)),
                   pl.BlockSpec((1, GDN_HEADS, GDN_DK, GDN_DV), lambda b, j: (b, 0, 0, 0))],
        out_shape=[jax.ShapeDtypeStruct((bsz, tp, zw), f32),
                   jax.ShapeDtypeStruct((bsz, GDN_HEADS, GDN_DK, GDN_DV), f32)],
        scratch_shapes=[pltpu.VMEM((tb + 8, ch), f32), pltpu.VMEM((GDN_HEADS, GDN_DK, GDN_DV), f32)],
        compiler_params=_cparams(("parallel", "arbitrary")),
        name="gdn",
    )(qkv_raw, z, misc, conv_w.astype(f32), cinit, hp, norm_w.reshape(1, GDN_DV).astype(f32), s0.astype(f32))


def _bucket_thresholds():
    exact = REL_BUCKETS // 2
    d = np.arange(0, 4 * REL_MAX_DIST, dtype=np.int64)

    def buckets(ft):
        nf = np.maximum(d, exact).astype(ft)
        large = exact + (np.log(nf / ft(exact)) / ft(math.log(REL_MAX_DIST / exact)) * ft(REL_BUCKETS - exact)).astype(np.int32)
        return np.where(d < exact, d, np.minimum(large, REL_BUCKETS - 1))

    b64, b32 = buckets(np.float64), buckets(np.float32)
    assert np.array_equal(b64, b32) and b64[-1] == REL_BUCKETS - 1 and np.all(np.diff(b64) >= 0)
    return [int(np.argmax(b64 >= k)) for k in range(1, REL_BUCKETS)]


_BUCKET_THR = _bucket_thresholds()
FAR_DIST = _BUCKET_THR[-1]


def _bias_kernel(rb_ref, d_ref, o_ref, *, slab, group_heads):
    rows = d_ref.shape[-2]
    for s in range(rows // slab):
        d = d_ref[0, 0, s * slab:(s + 1) * slab, :]
        h = pl.program_id(0) * group_heads + s
        val = jnp.full(d.shape, rb_ref[0, h], f32)
        for k in range(1, REL_BUCKETS):
            val = jnp.where(d >= _BUCKET_THR[k - 1], rb_ref[k, h], val)
        o_ref[0, 0, s * slab:(s + 1) * slab, :] = jnp.where(d < 0, NEG, val)


def _bias_table(rel_bias, dist, *, slab):
    g, nt, rows, cols = dist.shape
    return pl.pallas_call(
        functools.partial(_bias_kernel, slab=slab, group_heads=rows // slab),
        grid=(g, nt),
        in_specs=[pl.BlockSpec(memory_space=pltpu.SMEM),
                  pl.BlockSpec((1, 1, rows, cols), lambda a, b: (a, b, 0, 0))],
        out_specs=pl.BlockSpec((1, 1, rows, cols), lambda a, b: (a, b, 0, 0)),
        out_shape=jax.ShapeDtypeStruct(dist.shape, f32),
        compiler_params=_cparams(("arbitrary", "arbitrary")),
        name="bias_table",
    )(rel_bias.astype(f32), dist)


def _blockmean_kernel(x_ref, o_ref):
    x = x_ref[0]
    nb = x.shape[0] // NSA_BLOCK
    o_ref[0] = jnp.sum(x.reshape(nb, NSA_BLOCK, x.shape[1]), axis=1) * (1.0 / NSA_BLOCK)


def _block_means(kv4, *, tb):
    bsz, t, _ = kv4.shape
    w = 2 * NSA_KV_GROUPS * NSA_DH
    return pl.pallas_call(
        _blockmean_kernel,
        grid=(bsz, t // tb),
        in_specs=[pl.BlockSpec((1, tb, w), lambda b, i: (b, i, 0))],
        out_specs=pl.BlockSpec((1, tb // NSA_BLOCK, w), lambda b, i: (b, i, 0)),
        out_shape=jax.ShapeDtypeStruct((bsz, t // NSA_BLOCK, w), f32),
        compiler_params=_cparams(("parallel", "parallel")),
        name="nsa_block_means",
    )(kv4)


def _topn_mask(imp, n_sel):
    lane = lax.broadcasted_iota(i32, imp.shape, 1)
    sel = jnp.zeros(imp.shape, jnp.bool_)
    for _ in range(n_sel):
        mx = jnp.max(imp, axis=-1, keepdims=True)
        first = jnp.min(jnp.where(imp == mx, lane, imp.shape[1]), axis=-1, keepdims=True)
        hit = lane == first
        sel = sel | hit
        imp = jnp.where(hit, -jnp.inf, imp)
    return sel


def _nsa_prompt_kernel(q_ref, misc_ref, kcvc_ref, kaug_ref, v2_ref, kw_ref, vw_ref,
                       tabs_ref, tabw_ref, tabc_ref, c31_ref, o_ref):
    g = pl.program_id(1)
    i = pl.program_id(2)
    qb, dh, hpg = NSA_QBLOCK, NSA_DH, NSA_HPG
    rows = hpg * qb
    is_g0 = g == 0

    def ghalf(a):
        return jnp.where(is_g0, a[:, :dh], a[:, dh:])

    qt = q_ref[0] * (dh ** -0.5)
    qs = jnp.concatenate([qt[:, h * dh:(h + 1) * dh] for h in range(hpg)], axis=0)
    zero = jnp.zeros_like(qs)
    q2 = jnp.concatenate([jnp.where(is_g0, qs, zero), jnp.where(is_g0, zero, qs)], axis=1)

    kcvc = kcvc_ref[0]
    nblk = kcvc.shape[0]
    kc = ghalf(kcvc[:, :2 * dh])
    vc = ghalf(kcvc[:, 2 * dh:])
    s_c = _mm3_nt(qs, kc)
    lane = lax.broadcasted_iota(i32, (rows, nblk), 1)
    qrow = lax.broadcasted_iota(i32, (rows, nblk), 0) & (qb - 1)
    shift = (2 * i - 2 * NEAR_PAIRS + 1 + 4 * nblk) % nblk
    bias_c = pltpu.roll(tabc_ref[0], shift, 1)
    bias_c = jnp.where(lane < 2 * i - 2 * NEAR_PAIRS + 1, c31_ref[0], bias_c)
    readable = lane * NSA_BLOCK + (NSA_BLOCK - 1) <= i * qb + qrow
    s_c = jnp.where(readable, s_c + bias_c, NEG)
    m_c = jnp.max(s_c, axis=-1, keepdims=True)
    p_c = jnp.where(readable, jnp.exp(s_c - m_c), 0.0)
    p_c = p_c / jnp.maximum(jnp.sum(p_c, axis=-1, keepdims=True), 1e-30)
    o_c = _mm(p_c, vc)

    imp = p_c[0:qb]
    for h in range(1, hpg):
        imp = imp + p_c[h * qb:(h + 1) * qb]
    blk = lax.broadcasted_iota(i32, (qb, nblk), 1)
    cur = (i * qb + lax.broadcasted_iota(i32, (qb, nblk), 0)) >> int(math.log2(NSA_BLOCK))
    valid = blk <= cur
    forced = valid & ((blk == 0) | (blk > cur - NSA_LOCAL))
    imp = jnp.where(forced, -jnp.inf, jnp.where(valid, imp, -1.0))
    sel = (forced | _topn_mask(imp, min(NSA_TOPN, nblk) - 1 - NSA_LOCAL)) & valid
    msel = jnp.where(sel, 0.0, NEG).astype(bf16)
    qaug = jnp.concatenate([q2.astype(bf16), jnp.concatenate([msel] * hpg, axis=0)], axis=1)

    t_d = i // 2
    tk = 2 * PAIR

    def tile(k, carry):
        m, l, acc = carry
        t = t_d - k
        ks = pl.multiple_of(t * tk, tk)
        s = lax.dot_general(qaug, kaug_ref[0, pl.ds(ks, tk), :], (((1,), (1,)), ((), ())),
                            preferred_element_type=f32)
        i0 = jnp.clip(i - 2 * t, 0, NEAR_PAIRS)
        i1 = jnp.clip(i - 2 * t - 1, 0, NEAR_PAIRS)
        s = s + jnp.concatenate([tabs_ref[0, i0], tabs_ref[0, i1]], axis=1)
        m_new = jnp.maximum(m, jnp.max(s, axis=-1, keepdims=True))
        alpha = jnp.exp(m - m_new)
        p = jnp.exp(s - m_new)
        l = alpha * l + jnp.sum(p, axis=-1, keepdims=True)
        acc = alpha * acc + jnp.dot(p.astype(bf16), v2_ref[0, pl.ds(ks, tk), :], preferred_element_type=f32)
        return m_new, l, acc

    m0 = jnp.full((rows, 1), NEG, f32)
    l0 = jnp.zeros((rows, 1), f32)
    a0 = jnp.zeros((rows, 2 * dh), f32)
    _, l_s, acc_s = lax.fori_loop(0, t_d + 1, tile, (m0, l0, a0))
    o_s = ghalf(acc_s) / jnp.maximum(l_s, 1e-30)

    span = NSA_WINDOW + qb
    ws = pl.multiple_of(i * qb, qb)
    s_w = lax.dot_general(q2.astype(bf16), kw_ref[0, pl.ds(ws, span), :], (((1,), (1,)), ((), ())),
                          preferred_element_type=f32) + tabw_ref[0]
    kpos = lax.broadcasted_iota(i32, (rows, span), 1) + (i * qb - NSA_WINDOW)
    s_w = jnp.where(kpos >= 0, s_w, NEG)
    m_w = jnp.max(s_w, axis=-1, keepdims=True)
    p_w = jnp.exp(s_w - m_w)
    l_w = jnp.sum(p_w, axis=-1, keepdims=True)
    o_w = ghalf(jnp.dot(p_w.astype(bf16), vw_ref[0, pl.ds(ws, span), :], preferred_element_type=f32))
    o_w = o_w / jnp.maximum(l_w, 1e-30)

    gates = jax.nn.sigmoid(misc_ref[0])
    outs = []
    for h in range(hpg):
        r = slice(h * qb, (h + 1) * qb)
        acc = None
        for jb, branch in enumerate((o_c, o_s, o_w)):
            l0_ = 8 + 3 * h + jb
            l1_ = 8 + 3 * (hpg + h) + jb
            gcol = jnp.where(is_g0, gates[:, l0_:l0_ + 1], gates[:, l1_:l1_ + 1])
            term = gcol * branch[r]
            acc = term if acc is None else acc + term
        outs.append(acc)
    o_ref[0] = jnp.concatenate(outs, axis=1)


LOG2E = 1.4426950408889634
NSA_KEY_TILE = 4 * PAIR


def _nsa_prompt_kernel_h(q_ref, misc_ref, kcvc_ref, kaug_ref, vgt_ref, kw_ref, vw_ref,
                         tabs_ref, tabw_ref, tabc_ref, c31_ref, o_ref, qaug_scr, s_scr):
    g = pl.program_id(1)
    i = pl.program_id(2)
    qb, dh, hpg = NSA_QBLOCK, NSA_DH, NSA_HPG
    is_g0 = g == 0
    nn = (((1,), (1,)), ((), ()))

    def ghalf(a):
        return jnp.where(is_g0, a[:, :dh], a[:, dh:])

    def hrows(ref, *lead):
        return [ref[lead + (slice(h * qb, (h + 1) * qb), slice(None))] for h in range(hpg)]

    qt = q_ref[0] * (dh ** -0.5)
    qh = [qt[:, h * dh:(h + 1) * dh] for h in range(hpg)]
    kcvc = kcvc_ref[0]
    nblk = kcvc.shape[0]
    kc = ghalf(kcvc[:, :2 * dh])
    vc = ghalf(kcvc[:, 2 * dh:])

    rows = hpg * qb
    qs = jnp.concatenate(qh, axis=0)
    zero = jnp.zeros_like(qs)
    q2f = jnp.concatenate([jnp.where(is_g0, qs, zero), jnp.where(is_g0, zero, qs)], axis=1) * LOG2E
    q2 = q2f.astype(bf16)

    blk4 = lax.broadcasted_iota(i32, (rows, nblk), 1)
    qrow4 = lax.broadcasted_iota(i32, (rows, nblk), 0) & (qb - 1)
    first_tab = 2 * i - 2 * NEAR_PAIRS + 1
    shift = (first_tab + 4 * nblk) % nblk
    readable = blk4 * NSA_BLOCK + (NSA_BLOCK - 1) <= i * qb + qrow4
    bias = jnp.where(blk4 < first_tab, c31_ref[0], pltpu.roll(tabc_ref[0], shift, 1))
    s_c = jnp.where(readable, _mm3_nt(qs, kc) + bias, NEG)
    m_c = jnp.max(s_c, axis=-1, keepdims=True)
    p_c = jnp.where(readable, jnp.exp(s_c - m_c), 0.0)
    p_c = p_c / jnp.maximum(jnp.sum(p_c, axis=-1, keepdims=True), 1e-30)
    o_cmp = _mm(p_c, vc)
    imp = p_c[0:qb]
    for h in range(1, hpg):
        imp = imp + p_c[h * qb:(h + 1) * qb]

    span = NSA_WINDOW + qb
    ws = pl.multiple_of(i * qb, qb)
    s_w = lax.dot_general(q2, kw_ref[0, pl.ds(ws, span), :], nn, preferred_element_type=f32) + tabw_ref[0]
    in_seq = lax.broadcasted_iota(i32, (rows, span), 1) + (i * qb - NSA_WINDOW) >= 0
    s_w = jnp.where(in_seq, s_w, NEG)
    m_w = jnp.max(s_w, axis=-1, keepdims=True)
    p_w = jnp.exp2(s_w - m_w)
    l_w = jnp.sum(p_w, axis=-1, keepdims=True)
    o_win = ghalf(jnp.dot(p_w.astype(bf16), vw_ref[0, pl.ds(ws, span), :], preferred_element_type=f32))
    o_win = o_win / jnp.maximum(l_w, 1e-30)

    blk = lax.broadcasted_iota(i32, (nblk, qb), 0)
    cur = (i * qb + lax.broadcasted_iota(i32, (nblk, qb), 1)) >> int(math.log2(NSA_BLOCK))
    valid = blk <= cur
    forced = valid & ((blk == 0) | (blk > cur - NSA_LOCAL))
    imp_t = jnp.where(forced, -jnp.inf, jnp.where(valid, imp.T, -1.0))
    picked = forced
    for _ in range(min(NSA_TOPN, nblk) - 1 - NSA_LOCAL):
        mx = jnp.max(imp_t, axis=0, keepdims=True)
        first = jnp.min(jnp.where(imp_t == mx, blk, nblk), axis=0, keepdims=True)
        hit = blk == first
        picked = picked | hit
        imp_t = jnp.where(hit, -jnp.inf, imp_t)
    msel_t = jnp.where(picked & valid, 0.0, NEG).astype(bf16)

    qaug_scr[...] = jnp.concatenate([q2f.T.astype(bf16), jnp.concatenate([msel_t] * hpg, axis=1)], axis=0)
    tk = s_scr.shape[0]
    ppt = tk // PAIR
    t_d = i // ppt

    def scores(t):
        ks = pl.multiple_of(t * tk, tk)
        s = jnp.dot(kaug_ref[0, pl.ds(ks, tk), :], qaug_scr[...], preferred_element_type=f32)
        tabs = [tabs_ref[0, jnp.clip(i - ppt * t - j, 0, NEAR_PAIRS)] for j in range(ppt)]
        return s + jnp.concatenate(tabs, axis=0)

    s_scr[...] = scores(t_d)

    def tile(k_it, carry):
        m, acc = carry
        t = t_d - k_it
        s = s_scr[...]
        s_next = scores(jnp.maximum(t - 1, 0))
        m_new = jnp.maximum(m, jnp.max(s, axis=0, keepdims=True))
        alpha = jnp.exp2(m - m_new)
        p = jnp.exp2(s - m_new).astype(bf16)
        ks = pl.multiple_of(t * tk, tk)
        acc = alpha * acc + jnp.dot(vgt_ref[0, 0, :, pl.ds(ks, tk)], p, preferred_element_type=f32)
        s_scr[...] = s_next
        return m_new, acc

    _, acc_t = lax.fori_loop(0, t_d + 1, tile, (jnp.full((1, rows), NEG, f32), jnp.zeros((2 * dh, rows), f32)))
    o_sel = (acc_t[:dh] / jnp.maximum(acc_t[dh:dh + 1], 1e-30)).T

    gates = jax.nn.sigmoid(misc_ref[0])
    outs = []
    for h in range(hpg):
        r = slice(h * qb, (h + 1) * qb)
        acc = None
        for jb, branch in enumerate((o_cmp[r], o_sel[r], o_win[r])):
            la = 8 + 3 * h + jb
            lb = 8 + 3 * (hpg + h) + jb
            term = jnp.where(is_g0, gates[:, la:la + 1], gates[:, lb:lb + 1]) * branch
            acc = term if acc is None else acc + term
        outs.append(acc)
    o_ref[0] = jnp.concatenate(outs, axis=1)


def _nsa_prompt_kernel_m(q_ref, misc_ref, kcvc_ref, kaug_ref, v3_ref, kw_ref, vw3_ref,
                         tabs_ref, tabw_ref, tabc_ref, c31_ref, o_ref, qaug_scr, s_scr):
    i = pl.program_id(1)
    qb, dh, hpg, ng, nh = NSA_QBLOCK, NSA_DH, NSA_HPG, NSA_KV_GROUPS, NSA_HEADS
    assert ng == 2
    rows, grows = nh * qb, hpg * qb
    nn = (((1,), (1,)), ((), ()))
    in_g0 = lax.broadcasted_iota(i32, (rows, 1), 0) < grows

    def own(a):
        return jnp.where(in_g0, a[:, :dh], a[:, dh:2 * dh])

    qt = q_ref[0] * (dh ** -0.5)
    qs = jnp.concatenate([qt[:, h * dh:(h + 1) * dh] for h in range(nh)], axis=0)
    zero = jnp.zeros_like(qs)
    q2f = jnp.concatenate([jnp.where(in_g0, qs, zero), jnp.where(in_g0, zero, qs)], axis=1)
    q2 = (q2f * LOG2E).astype(bf16)
    kcvc = kcvc_ref[0]
    nblk = kcvc.shape[0]

    blk4 = lax.broadcasted_iota(i32, (rows, nblk), 1)
    qrow4 = lax.broadcasted_iota(i32, (rows, nblk), 0) & (qb - 1)
    first_tab = 2 * i - 2 * NEAR_PAIRS + 1
    shift = (first_tab + 4 * nblk) % nblk
    readable = blk4 * NSA_BLOCK + (NSA_BLOCK - 1) <= i * qb + qrow4
    bias = jnp.where(blk4 < first_tab, c31_ref[...], pltpu.roll(tabc_ref[...], shift, 1))
    s_c = jnp.where(readable, _mm3_nt(q2f, kcvc[:, :2 * dh]) + bias, NEG)
    m_c = jnp.max(s_c, axis=-1, keepdims=True)
    p_c = jnp.where(readable, jnp.exp(s_c - m_c), 0.0)
    p_c = p_c / jnp.maximum(jnp.sum(p_c, axis=-1, keepdims=True), 1e-30)
    o_cmp = own(_mm(p_c, kcvc[:, 2 * dh:]))
    imp = []
    for g in range(ng):
        acc = p_c[g * grows:g * grows + qb]
        for h in range(1, hpg):
            acc = acc + p_c[g * grows + h * qb:g * grows + (h + 1) * qb]
        imp.append(acc)
    imp = jnp.concatenate(imp, axis=0)

    span = NSA_WINDOW + qb
    ws = pl.multiple_of(i * qb, qb)
    s_w = lax.dot_general(q2, kw_ref[0, pl.ds(ws, span), :], nn, preferred_element_type=f32) + tabw_ref[...]
    in_seq = lax.broadcasted_iota(i32, (rows, span), 1) + (i * qb - NSA_WINDOW) >= 0
    s_w = jnp.where(in_seq, s_w, NEG)
    p_w = jnp.exp2(s_w - jnp.max(s_w, axis=-1, keepdims=True)).astype(bf16)
    acc_w = jnp.dot(p_w, vw3_ref[0, pl.ds(ws, span), :], preferred_element_type=f32)
    o_win = own(acc_w) / jnp.maximum(acc_w[:, 2 * dh:2 * dh + 1], 1e-30)

    blk = lax.broadcasted_iota(i32, (ng * qb, nblk), 1)
    qrow = lax.broadcasted_iota(i32, (ng * qb, nblk), 0) & (qb - 1)
    cur = (i * qb + qrow) >> int(math.log2(NSA_BLOCK))
    valid = blk <= cur
    forced = valid & ((blk == 0) | (blk > cur - NSA_LOCAL))
    imp_t = jnp.where(forced, -jnp.inf, jnp.where(valid, imp, -1.0)).T
    brow = lax.broadcasted_iota(i32, imp_t.shape, 0)
    picked = jnp.zeros(imp_t.shape, f32)
    for _ in range(min(NSA_TOPN, nblk) - 1 - NSA_LOCAL):
        mx = jnp.max(imp_t, axis=0, keepdims=True)
        first = jnp.min(jnp.where(imp_t == mx, brow, nblk), axis=0, keepdims=True)
        hit = brow == first
        picked = jnp.where(hit, 1.0, picked)
        imp_t = jnp.where(hit, -jnp.inf, imp_t)
    sel = (forced | (picked.T > 0.5)) & valid
    msel = jnp.where(sel, 0.0, NEG).astype(bf16)
    msel = jnp.concatenate([msel[g * qb:(g + 1) * qb] for g in range(ng) for _ in range(hpg)], axis=0)
    qaug_scr[...] = jnp.concatenate([q2, msel], axis=1)

    tk = s_scr.shape[1]
    ppt = tk // PAIR
    t_d = i // ppt

    def scores(t):
        ks = pl.multiple_of(t * tk, tk)
        s = lax.dot_general(qaug_scr[...], kaug_ref[0, pl.ds(ks, tk), :], nn, preferred_element_type=f32)
        tabs = [tabs_ref[jnp.clip(i - ppt * t - j, 0, NEAR_PAIRS)] for j in range(ppt)]
        return s + jnp.concatenate(tabs, axis=1)

    s_scr[...] = scores(t_d)

    def tile(k_it, carry):
        m, acc = carry
        t = t_d - k_it
        s = s_scr[...]
        s_next = scores(jnp.maximum(t - 1, 0))
        m_new = jnp.maximum(m, jnp.max(s, axis=-1, keepdims=True))
        alpha = jnp.exp2(m - m_new)
        p = jnp.exp2(s - m_new).astype(bf16)
        ks = pl.multiple_of(t * tk, tk)
        acc = alpha * acc + jnp.dot(p, v3_ref[0, pl.ds(ks, tk), :], preferred_element_type=f32)
        s_scr[...] = s_next
        return m_new, acc

    _, acc_s = lax.fori_loop(0, t_d + 1, tile, (jnp.full((rows, 1), NEG, f32), jnp.zeros((rows, 4 * dh), f32)))
    o_sel = own(acc_s) / jnp.maximum(acc_s[:, 2 * dh:2 * dh + 1], 1e-30)

    gates = jax.nn.sigmoid(misc_ref[0])
    outs = []
    for h in range(nh):
        r = slice(h * qb, (h + 1) * qb)
        acc = None
        for jb, branch in enumerate((o_cmp[r], o_sel[r], o_win[r])):
            lane = 2 * GDN_HEADS + 3 * h + jb
            term = gates[:, lane:lane + 1] * branch
            acc = term if acc is None else acc + term
        outs.append(acc)
    o_ref[0] = jnp.concatenate(outs, axis=1)


def _nsa_prompt_m(q_b, misc, kv4, kvwin, rel_bias):
    bsz, t, _ = q_b.shape
    dh, qb = NSA_DH, NSA_QBLOCK
    nblk = t // NSA_BLOCK
    assert nblk == PAIR and t % NSA_KEY_TILE == 0
    kcvc = _block_means(kv4, tb=512)
    onehot = (jnp.arange(t, dtype=i32)[:, None] // NSA_BLOCK == jnp.arange(nblk, dtype=i32)[None, :]).astype(bf16)
    kaug = jnp.concatenate([kv4[:, :, 4 * dh:6 * dh].astype(bf16),
                            jnp.broadcast_to(onehot[None], (bsz, t, nblk))], axis=-1)
    ones = jnp.ones((bsz, t, 2 * dh), bf16)
    v3 = jnp.concatenate([kv4[:, :, 6 * dh:8 * dh].astype(bf16), ones], axis=-1)
    pad = ((0, 0), (NSA_WINDOW, 0), (0, 0))
    kw = jnp.pad(kvwin[:, :, :2 * dh].astype(bf16), pad)
    vw3 = jnp.pad(jnp.concatenate([kvwin[:, :, 2 * dh:].astype(bf16), ones], axis=-1), pad)
    tabs, tabw, tabc, c31 = _nsa_tables(rel_bias)
    rows = NSA_HEADS * qb
    span = NSA_WINDOW + qb
    ns = NEAR_PAIRS + 1
    tabs = (jnp.transpose(tabs, (1, 0, 2, 3)) * LOG2E).reshape(ns, rows, PAIR)
    tabw = (tabw * LOG2E).reshape(rows, span)
    tabc = tabc.reshape(rows, PAIR)
    c31 = c31.reshape(rows, 1)
    once = pl.Buffered(1)
    per_b = lambda shape: pl.BlockSpec((1,) + shape, lambda b, i: (b,) + (0,) * len(shape), pipeline_mode=once)
    const = lambda shape: pl.BlockSpec(shape, lambda b, i: (0,) * len(shape), pipeline_mode=once)
    return pl.pallas_call(
        _nsa_prompt_kernel_m,
        grid=(bsz, t // qb),
        in_specs=[pl.BlockSpec((1, qb, NSA_HEADS * dh), lambda b, i: (b, i, 0)),
                  pl.BlockSpec((1, qb, MISC_W), lambda b, i: (b, i, 0)),
                  per_b((nblk, 4 * dh)), per_b((t, 2 * dh + nblk)), per_b((t, 4 * dh)),
                  per_b((t + NSA_WINDOW, 2 * dh)), per_b((t + NSA_WINDOW, 4 * dh)),
                  const((ns, rows, PAIR)), const((rows, span)), const((rows, PAIR)), const((rows, 1))],
        out_specs=pl.BlockSpec((1, qb, NSA_HEADS * dh), lambda b, i: (b, i, 0)),
        out_shape=jax.ShapeDtypeStruct((bsz, t, NSA_HEADS * dh), f32),
        scratch_shapes=[pltpu.VMEM((rows, 2 * dh + nblk), bf16), pltpu.VMEM((rows, NSA_KEY_TILE), f32)],
        compiler_params=_cparams(("parallel", "arbitrary")),
        name="nsa_prompt",
    )(q_b, misc, kcvc, kaug, v3, kw, vw3, tabs, tabw, tabc, c31)


def _nsa_tables(rel_bias):
    qb, hpg = NSA_QBLOCK, NSA_HPG
    rows = hpg * qb
    q = (np.arange(rows) % qb)[:, None]
    c = np.arange(PAIR)[None, :]
    d_sel = [PAIR * idx + q - c for idx in range(NEAR_PAIRS + 1)]
    assert PAIR * NEAR_PAIRS - (PAIR - 1) >= FAR_DIST
    mm = 2 * NEAR_PAIRS - 1 - c
    d_cmp = np.maximum(np.where(mm >= -1, NSA_BLOCK * mm + q - (NSA_BLOCK - 1), 0), 0)
    assert NSA_BLOCK * (2 * NEAR_PAIRS) - (NSA_BLOCK - 1) >= FAR_DIST
    cw = np.arange(NSA_WINDOW + qb)[None, :]
    d_win = q + NSA_WINDOW - cw
    d_win = np.where((d_win >= 0) & (d_win < NSA_WINDOW), d_win, -1)
    n_win = (NSA_WINDOW + qb) // PAIR
    tiles = d_sel + [d_cmp] + [d_win[:, k * PAIR:(k + 1) * PAIR] for k in range(n_win)]
    dist = np.broadcast_to(np.stack(tiles)[None], (NSA_KV_GROUPS, len(tiles), rows, PAIR))
    tab = _bias_table(rel_bias, jnp.asarray(dist, i32), slab=qb)
    ns = NEAR_PAIRS + 1
    tabs = tab[:, :ns]
    tabc = tab[:, ns]
    tabw = jnp.concatenate([tab[:, ns + 1 + k] for k in range(n_win)], axis=-1)
    c31 = jnp.repeat(rel_bias.astype(f32)[REL_BUCKETS - 1].reshape(NSA_KV_GROUPS, hpg), qb, axis=1)
    return tabs, tabw, tabc, c31.reshape(NSA_KV_GROUPS, rows, 1)


def _nsa_prompt(q_b, misc, kv4, kvwin, rel_bias):
    bsz, t, _ = q_b.shape
    dh, qb = NSA_DH, NSA_QBLOCK
    nblk = t // NSA_BLOCK
    assert nblk == PAIR and t % NSA_KEY_TILE == 0
    kcvc = _block_means(kv4, tb=512)
    onehot = (jnp.arange(t, dtype=i32)[:, None] // NSA_BLOCK == jnp.arange(nblk, dtype=i32)[None, :]).astype(bf16)
    kaug = jnp.concatenate([kv4[:, :, 4 * dh:6 * dh].astype(bf16),
                            jnp.broadcast_to(onehot[None], (bsz, t, nblk))], axis=-1)
    ones = jnp.ones((bsz, t, dh), bf16)
    vgt = jnp.stack([jnp.swapaxes(jnp.concatenate([kv4[:, :, (6 + g) * dh:(7 + g) * dh].astype(bf16), ones],
                                                   axis=-1), 1, 2)
                     for g in range(NSA_KV_GROUPS)], axis=1)
    pad = ((0, 0), (NSA_WINDOW, 0), (0, 0))
    kw = jnp.pad(kvwin[:, :, :2 * dh].astype(bf16), pad)
    vw = jnp.pad(kvwin[:, :, 2 * dh:].astype(bf16), pad)
    tabs, tabw, tabc, c31 = _nsa_tables(rel_bias)
    tabs, tabw = jnp.swapaxes(tabs, 2, 3) * LOG2E, tabw * LOG2E
    rows = NSA_HPG * qb
    gw = NSA_HPG * dh
    span = NSA_WINDOW + qb
    return pl.pallas_call(
        _nsa_prompt_kernel_h,
        grid=(bsz, NSA_KV_GROUPS, t // qb),
        in_specs=[pl.BlockSpec((1, qb, gw), lambda b, g, i: (b, i, g)),
                  pl.BlockSpec((1, qb, MISC_W), lambda b, g, i: (b, i, 0)),
                  pl.BlockSpec((1, nblk, 4 * dh), lambda b, g, i: (b, 0, 0)),
                  pl.BlockSpec((1, t, 2 * dh + nblk), lambda b, g, i: (b, 0, 0)),
                  pl.BlockSpec((1, 1, 2 * dh, t), lambda b, g, i: (b, g, 0, 0)),
                  pl.BlockSpec((1, t + NSA_WINDOW, 2 * dh), lambda b, g, i: (b, 0, 0)),
                  pl.BlockSpec((1, t + NSA_WINDOW, 2 * dh), lambda b, g, i: (b, 0, 0)),
                  pl.BlockSpec((1, NEAR_PAIRS + 1, PAIR, rows), lambda b, g, i: (g, 0, 0, 0)),
                  pl.BlockSpec((1, rows, span), lambda b, g, i: (g, 0, 0)),
                  pl.BlockSpec((1, rows, PAIR), lambda b, g, i: (g, 0, 0)),
                  pl.BlockSpec((1, rows, 1), lambda b, g, i: (g, 0, 0))],
        out_specs=pl.BlockSpec((1, qb, gw), lambda b, g, i: (b, i, g)),
        out_shape=jax.ShapeDtypeStruct((bsz, t, NSA_HEADS * dh), f32),
        scratch_shapes=[pltpu.VMEM((2 * dh + nblk, rows), bf16), pltpu.VMEM((NSA_KEY_TILE, rows), f32)],
        compiler_params=_cparams(("parallel", "parallel", "arbitrary")),
        name="nsa_prompt",
    )(q_b, misc, kcvc, kaug, vgt, kw, vw, tabs, tabw, tabc, c31)


MEANS_PAGES = 8


def _nsa_s_means_kernel(pt_ref, *refs):
    del pt_ref
    x_refs, o_ref = refs[:-1], refs[-1]
    rows = [[[] for _ in range(NSA_KV_GROUPS)] for _ in range(2)]
    for x_ref in x_refs:
        page = x_ref.shape[1]
        for j in range(2):
            for g in range(NSA_KV_GROUPS):
                x = x_ref[0, :, j, g, :]
                for n in range(page // NSA_BLOCK):
                    rows[j][g].append(jnp.sum(x[n * NSA_BLOCK:(n + 1) * NSA_BLOCK], axis=0, keepdims=True)
                                      * (1.0 / NSA_BLOCK))
    for j in range(2):
        for g in range(NSA_KV_GROUPS):
            o_ref[0, j * NSA_KV_GROUPS + g] = jnp.concatenate(rows[j][g], axis=0)


def _nsa_s_means(cache_kv, page_table):
    db, n_pages = page_table.shape
    page = cache_kv.shape[1]
    nblk = n_pages * page // NSA_BLOCK
    kp = MEANS_PAGES
    rows = kp * page // NSA_BLOCK
    assert n_pages % kp == 0 and rows % 8 == 0

    def page_spec(k):
        return pl.BlockSpec((1, page, 2, NSA_KV_GROUPS, NSA_DH),
                            lambda b, p, pt: (pt[b * n_pages + p * kp + k], 0, 0, 0, 0))

    return pl.pallas_call(
        _nsa_s_means_kernel,
        grid_spec=pltpu.PrefetchScalarGridSpec(
            num_scalar_prefetch=1,
            grid=(db, n_pages // kp),
            in_specs=[page_spec(k) for k in range(kp)],
            out_specs=pl.BlockSpec((1, 2 * NSA_KV_GROUPS, rows, NSA_DH), lambda b, p, pt: (b, 0, p, 0))),
        out_shape=jax.ShapeDtypeStruct((db, 2 * NSA_KV_GROUPS, nblk, NSA_DH), f32),
        compiler_params=_cparams(("parallel", "parallel")),
        name="nsa_sample_means",
    )(page_table.reshape(-1), *([cache_kv] * kp))


def _nsa_s_scores_kernel(q_ref, kcvc_ref, win_ref, kvn_ref, tc_ref, tw_ref, b0_ref, oc_ref, ow_ref, sel_ref):
    dh, hpg = NSA_DH, NSA_HPG
    q = q_ref[0] * (dh ** -0.5)
    nblk = kcvc_ref.shape[2]
    kvn = kvn_ref[0]
    row = lax.broadcasted_iota(i32, (NSA_HEADS, 1), 0)
    lane = lax.broadcasted_iota(i32, (1, nblk), 1)
    n_sel = NSA_TOPN - 1
    for g in range(NSA_KV_GROUPS):
        in_g = (row >= g * hpg) & (row < (g + 1) * hpg)
        kc = kcvc_ref[0, g]
        vc = kcvc_ref[0, NSA_KV_GROUPS + g]
        s_c = _mm3_nt(q, kc) + tc_ref[...]
        m_c = jnp.max(s_c, axis=-1, keepdims=True)
        p_c = jnp.exp(s_c - m_c)
        p_c = p_c / jnp.maximum(jnp.sum(p_c, axis=-1, keepdims=True), 1e-30)
        o_c = _mm(p_c, vc)
        imp = jnp.sum(jnp.where(in_g, p_c, 0.0), axis=0, keepdims=True)
        forced_blocks = [0] + [nblk - k for k in range(1, NSA_LOCAL)]
        imp = jnp.where((lane == 0) | (lane > nblk - NSA_LOCAL), -jnp.inf, imp)
        picks = jnp.zeros((1, nblk), i32)
        for it, fb in enumerate(forced_blocks):
            picks = jnp.where(lane == it, fb, picks)
        for it in range(len(forced_blocks), n_sel):
            mx = jnp.max(imp, axis=-1, keepdims=True)
            first = jnp.min(jnp.where(imp == mx, lane, nblk), axis=-1, keepdims=True)
            picks = jnp.where(lane == it, first, picks)
            imp = jnp.where(lane == first, -jnp.inf, imp)
        sel_ref[0, g:g + 1, :] = picks
        kw = win_ref[0, :, 0, g, :]
        vw = win_ref[0, :, 1, g, :]
        s_w = _mm3_nt(q, kw) + tw_ref[...]
        s_n = jnp.sum(q * kvn[:, g * dh:(g + 1) * dh], axis=-1, keepdims=True) + b0_ref[...]
        m_w = jnp.maximum(jnp.max(s_w, axis=-1, keepdims=True), s_n)
        p_w = jnp.exp(s_w - m_w)
        p_n = jnp.exp(s_n - m_w)
        l_w = jnp.sum(p_w, axis=-1, keepdims=True) + p_n
        o_w = (_mm(p_w, vw) + p_n * kvn[:, (NSA_KV_GROUPS + g) * dh:(NSA_KV_GROUPS + g + 1) * dh]) / jnp.maximum(l_w, 1e-30)
        if g == 0:
            oc_ref[0] = o_c
            ow_ref[0] = o_w
        else:
            oc_ref[0] = jnp.where(in_g, o_c, oc_ref[0])
            ow_ref[0] = jnp.where(in_g, o_w, ow_ref[0])


def _nsa_s_sel_kernel(sel_ref, pt_ref, q_ref, k_ref, v_ref, tb_ref, kvn_ref, b0_ref, oc_ref, ow_ref, gt_ref,
                      o_ref, m_scr, l_scr, acc_scr):
    del sel_ref, pt_ref
    g = pl.program_id(1)
    j = pl.program_id(2)
    dh, hpg = NSA_DH, NSA_HPG
    is_g0 = g == 0
    q = q_ref[0] * (dh ** -0.5)
    kvn = kvn_ref[0]

    @pl.when(j == 0)
    def _():
        k_n = jnp.where(is_g0, kvn[:, 4 * dh:5 * dh], kvn[:, 5 * dh:6 * dh])
        v_n = jnp.where(is_g0, kvn[:, 6 * dh:7 * dh], kvn[:, 7 * dh:8 * dh])
        m_scr[...] = jnp.sum(q * k_n, axis=-1, keepdims=True) + b0_ref[...]
        l_scr[...] = jnp.ones_like(l_scr)
        acc_scr[...] = jnp.broadcast_to(v_n, acc_scr.shape)

    k = jnp.where(is_g0, k_ref[0, :, 0, 0, :], k_ref[0, :, 0, 1, :])
    v = jnp.where(is_g0, v_ref[0, :, 0, 0, :], v_ref[0, :, 0, 1, :])
    s = _mm3_nt(q, k) + tb_ref[0]
    m_old = m_scr[...]
    m_new = jnp.maximum(m_old, jnp.max(s, axis=-1, keepdims=True))
    alpha = jnp.exp(m_old - m_new)
    p = jnp.exp(s - m_new)
    l_scr[...] = alpha * l_scr[...] + jnp.sum(p, axis=-1, keepdims=True)
    acc_scr[...] = alpha * acc_scr[...] + _mm(p, v)
    m_scr[...] = m_new

    @pl.when(j == pl.num_programs(2) - 1)
    def _():
        o_s = acc_scr[...] / jnp.maximum(l_scr[...], 1e-30)
        gt = gt_ref[0]
        o = gt[:, 0:1] * oc_ref[0] + gt[:, 1:2] * o_s + gt[:, 2:3] * ow_ref[0]
        row = lax.broadcasted_iota(i32, o.shape, 0)
        in_g = (row >= g * hpg) & (row < (g + 1) * hpg)

        @pl.when(is_g0)
        def _():
            o_ref[0] = o

        @pl.when(jnp.logical_not(is_g0))
        def _():
            o_ref[0] = jnp.where(in_g, o, o_ref[0])


def _nsa_sample(q_b, gates, kv4_new, kvwin_new, cache_kv, cache_win, page_table, rel_bias):
    db = q_b.shape[0]
    dh = NSA_DH
    n_pool, page = cache_kv.shape[:2]
    n_pages = page_table.shape[1]
    past = n_pages * page
    nblk = past // NSA_BLOCK
    wb = cache_win.shape[1]
    assert nblk == PAIR and page % NSA_BLOCK == 0 and wb == NSA_WINDOW
    kcvc = _nsa_s_means(cache_kv, page_table)

    n = np.arange(nblk)
    d_cmp = past - (n * NSA_BLOCK + NSA_BLOCK - 1)
    jw = np.arange(wb)
    d_win = np.where(jw >= 1, wb - jw, -1)
    pos = np.arange(nblk * NSA_BLOCK)
    d_sel = past - pos
    width = nblk + wb + nblk * NSA_BLOCK
    dist = np.broadcast_to(np.concatenate([d_cmp, d_win, d_sel])[None, :], (NSA_HEADS, width))
    tab = _bias_table(rel_bias, jnp.asarray(dist[None, None], i32), slab=1)[0, 0]
    t_cmp, t_win = tab[:, :nblk], tab[:, nblk:nblk + wb]
    t_sel = jnp.transpose(tab[:, nblk + wb:].reshape(NSA_HEADS, nblk, NSA_BLOCK), (1, 0, 2))
    b0 = rel_bias.astype(f32)[0].reshape(NSA_HEADS, 1)

    q3 = q_b.reshape(db, NSA_HEADS, dh)
    o_c, o_w, sel = pl.pallas_call(
        _nsa_s_scores_kernel,
        grid=(db,),
        in_specs=[pl.BlockSpec((1, NSA_HEADS, dh), lambda b: (b, 0, 0)),
                  pl.BlockSpec((1, 2 * NSA_KV_GROUPS, nblk, dh), lambda b: (b, 0, 0, 0)),
                  pl.BlockSpec((1, wb, 2, NSA_KV_GROUPS, dh), lambda b: (b, 0, 0, 0, 0)),
                  pl.BlockSpec((1, 1, 4 * dh), lambda b: (b, 0, 0)),
                  pl.BlockSpec((NSA_HEADS, nblk), lambda b: (0, 0)),
                  pl.BlockSpec((NSA_HEADS, wb), lambda b: (0, 0)),
                  pl.BlockSpec((NSA_HEADS, 1), lambda b: (0, 0))],
        out_specs=[pl.BlockSpec((1, NSA_HEADS, dh), lambda b: (b, 0, 0)),
                   pl.BlockSpec((1, NSA_HEADS, dh), lambda b: (b, 0, 0)),
                   pl.BlockSpec((1, NSA_KV_GROUPS, nblk), lambda b: (b, 0, 0))],
        out_shape=[jax.ShapeDtypeStruct((db, NSA_HEADS, dh), f32),
                   jax.ShapeDtypeStruct((db, NSA_HEADS, dh), f32),
                   jax.ShapeDtypeStruct((db, NSA_KV_GROUPS, nblk), i32)],
        compiler_params=_cparams(("parallel",)),
        name="nsa_sample_scores",
    )(q3, kcvc, cache_win.astype(f32), kvwin_new.reshape(db, 1, 4 * dh), t_cmp, t_win, b0)

    n_sel = NSA_TOPN - 1
    sel_flat = sel[:, :, :n_sel].reshape(-1)
    halves = page // NSA_BLOCK
    cache_h = cache_kv.reshape(n_pool * halves, NSA_BLOCK, 4, NSA_KV_GROUPS, dh)

    def blk_of(b, g, j, sel_r, pt_r):
        nb = sel_r[(b * NSA_KV_GROUPS + g) * n_sel + j]
        return nb, pt_r[b * n_pages + nb // halves] * halves + nb % halves

    o = pl.pallas_call(
        _nsa_s_sel_kernel,
        grid_spec=pltpu.PrefetchScalarGridSpec(
            num_scalar_prefetch=2,
            grid=(db, NSA_KV_GROUPS, n_sel),
            in_specs=[pl.BlockSpec((1, NSA_HEADS, dh), lambda b, g, j, s, p: (b, 0, 0)),
                      pl.BlockSpec((1, NSA_BLOCK, 1, NSA_KV_GROUPS, dh),
                                   lambda b, g, j, s, p: (blk_of(b, g, j, s, p)[1], 0, 2, 0, 0)),
                      pl.BlockSpec((1, NSA_BLOCK, 1, NSA_KV_GROUPS, dh),
                                   lambda b, g, j, s, p: (blk_of(b, g, j, s, p)[1], 0, 3, 0, 0)),
                      pl.BlockSpec((1, NSA_HEADS, NSA_BLOCK), lambda b, g, j, s, p: (blk_of(b, g, j, s, p)[0], 0, 0)),
                      pl.BlockSpec((1, 1, 8 * dh), lambda b, g, j, s, p: (b, 0, 0)),
                      pl.BlockSpec((NSA_HEADS, 1), lambda b, g, j, s, p: (0, 0)),
                      pl.BlockSpec((1, NSA_HEADS, dh), lambda b, g, j, s, p: (b, 0, 0)),
                      pl.BlockSpec((1, NSA_HEADS, dh), lambda b, g, j, s, p: (b, 0, 0)),
                      pl.BlockSpec((1, NSA_HEADS, 3), lambda b, g, j, s, p: (b, 0, 0))],
            out_specs=pl.BlockSpec((1, NSA_HEADS, dh), lambda b, g, j, s, p: (b, 0, 0)),
            scratch_shapes=[pltpu.VMEM((NSA_HEADS, 1), f32), pltpu.VMEM((NSA_HEADS, 1), f32),
                            pltpu.VMEM((NSA_HEADS, dh), f32)]),
        out_shape=jax.ShapeDtypeStruct((db, NSA_HEADS, dh), f32),
        compiler_params=_cparams(("parallel", "arbitrary", "arbitrary")),
        name="nsa_sample_selected",
    )(sel_flat, page_table.reshape(-1), q3, cache_h, cache_h, t_sel, kv4_new.reshape(db, 1, 8 * dh), b0,
      o_c, o_w, gates)
    return o.reshape(db, NSA_HEADS * dh)


def _nsa_st_means_kernel(pt_ref, *refs):
    del pt_ref
    x_refs, o_ref = refs[:-1], refs[-1]
    p = pl.program_id(1)

    @pl.when(p == 0)
    def _():
        o_ref[...] = jnp.zeros_like(o_ref)

    page = x_refs[0].shape[-1]
    per_page = page // NSA_BLOCK
    nblk = o_ref.shape[-1]
    assert page == nblk
    planes = 2 * NSA_KV_GROUPS
    xs = jnp.concatenate([x_ref[0, j, g] for x_ref in x_refs for j in range(2) for g in range(NSA_KV_GROUPS)],
                         axis=0)
    tok = lax.broadcasted_iota(i32, (page, nblk), 0)
    col = lax.broadcasted_iota(i32, (page, nblk), 1)
    pool = (col == (tok >> int(math.log2(NSA_BLOCK)))).astype(bf16)
    hi, lo = _split2(xs)
    sums = jnp.dot(hi, pool, preferred_element_type=f32) + jnp.dot(lo, pool, preferred_element_type=f32)
    rows = planes * NSA_DH
    acc = None
    for k in range(len(x_refs)):
        shift = (p * len(x_refs) + k) * per_page
        part = pltpu.roll(sums[k * rows:(k + 1) * rows], shift, 1)
        acc = part if acc is None else acc + part
    acc = acc * (1.0 / NSA_BLOCK)
    for pl_i in range(planes):
        o_ref[0, pl_i] = o_ref[0, pl_i] + acc[pl_i * NSA_DH:(pl_i + 1) * NSA_DH]


def _nsa_st_means(cache_t, page_table):
    db, n_pages = page_table.shape
    page = cache_t.shape[-1]
    nblk = n_pages * page // NSA_BLOCK
    kp = MEANS_PAGES
    assert n_pages % kp == 0

    def page_spec(k):
        return pl.BlockSpec((1, 2, NSA_KV_GROUPS, NSA_DH, page),
                            lambda b, p, pt: (pt[b * n_pages + p * kp + k], 0, 0, 0, 0))

    return pl.pallas_call(
        _nsa_st_means_kernel,
        grid_spec=pltpu.PrefetchScalarGridSpec(
            num_scalar_prefetch=1,
            grid=(db, n_pages // kp),
            in_specs=[page_spec(k) for k in range(kp)],
            out_specs=pl.BlockSpec((1, 2 * NSA_KV_GROUPS, NSA_DH, nblk), lambda b, p, pt: (b, 0, 0, 0))),
        out_shape=jax.ShapeDtypeStruct((db, 2 * NSA_KV_GROUPS, NSA_DH, nblk), f32),
        compiler_params=_cparams(("parallel", "arbitrary")),
        name="nsa_sample_means",
    )(page_table.reshape(-1), *([cache_t] * kp))


def _nsa_st_scores_kernel(q_ref, kcvc_ref, win_ref, kvn_ref, tc_ref, tw_ref, b0_ref, oc_ref, ow_ref, sel_ref):
    dh, hpg, ng = NSA_DH, NSA_HPG, NSA_KV_GROUPS
    sb = q_ref.shape[0]
    nblk = kcvc_ref.shape[-1]
    row = lax.broadcasted_iota(i32, (NSA_HEADS, 1), 0)
    in_g = [(row >= g * hpg) & (row < (g + 1) * hpg) for g in range(ng)]
    pairs = [(b, g) for b in range(sb) for g in range(ng)]
    q = [q_ref[b] * (dh ** -0.5) for b in range(sb)]

    s_c = [_mm3(q[b], kcvc_ref[b, g]) + tc_ref[...] for b, g in pairs]
    p_c = []
    for s in s_c:
        p = jnp.exp(s - jnp.max(s, axis=-1, keepdims=True))
        p_c.append(p / jnp.maximum(jnp.sum(p, axis=-1, keepdims=True), 1e-30))
    o_c = [_mm_nt(p_c[k], kcvc_ref[b, ng + g]) for k, (b, g) in enumerate(pairs)]
    imp = jnp.concatenate([jnp.sum(jnp.where(in_g[g], p_c[k], 0.0), axis=0, keepdims=True)
                           for k, (b, g) in enumerate(pairs)], axis=0)
    for b in range(sb):
        oc_ref[b] = jnp.where(in_g[0], o_c[b * ng], o_c[b * ng + 1])

    s_w = [_mm3(q[b], win_ref[b, 0, g]) + tw_ref[...] for b, g in pairs]
    o_w = []
    for k, (b, g) in enumerate(pairs):
        kvn = kvn_ref[b]
        s_n = jnp.sum(q[b] * kvn[:, g * dh:(g + 1) * dh], axis=-1, keepdims=True) + b0_ref[...]
        m_w = jnp.maximum(jnp.max(s_w[k], axis=-1, keepdims=True), s_n)
        p_w = jnp.exp(s_w[k] - m_w)
        p_n = jnp.exp(s_n - m_w)
        l_w = jnp.sum(p_w, axis=-1, keepdims=True) + p_n
        v_n = kvn[:, (ng + g) * dh:(ng + g + 1) * dh]
        o_w.append((_mm_nt(p_w, win_ref[b, 1, g]) + p_n * v_n) / jnp.maximum(l_w, 1e-30))
    for b in range(sb):
        ow_ref[b] = jnp.where(in_g[0], o_w[b * ng], o_w[b * ng + 1])

    lane = lax.broadcasted_iota(i32, imp.shape, 1)
    n_sel = NSA_TOPN - 1
    forced_blocks = [0] + [nblk - k for k in range(1, NSA_LOCAL)]
    imp = jnp.where((lane == 0) | (lane > nblk - NSA_LOCAL), -jnp.inf, imp)
    picks = jnp.zeros(imp.shape, i32)
    for it, fb in enumerate(forced_blocks):
        picks = jnp.where(lane == it, fb, picks)
    for it in range(len(forced_blocks), n_sel):
        mx = jnp.max(imp, axis=-1, keepdims=True)
        first = jnp.min(jnp.where(imp == mx, lane, nblk), axis=-1, keepdims=True)
        picks = jnp.where(lane == it, first, picks)
        imp = jnp.where(lane == first, -jnp.inf, imp)
    for b in range(sb):
        sel_ref[b] = picks[b * ng:(b + 1) * ng]


def _nsa_st_sel_kernel(sel_ref, pt_ref, q_ref, rb_ref, kvn_ref, oc_ref, ow_ref, gt_ref, *refs, n_sel, past):
    del pt_ref
    k_refs, v_refs, o_ref = refs[:n_sel], refs[n_sel:2 * n_sel], refs[2 * n_sel]
    b = pl.program_id(0)
    g = pl.program_id(1)
    dh, hpg = NSA_DH, NSA_HPG
    is_g0 = g == 0
    page = k_refs[0].shape[-1]
    per_page = page // NSA_BLOCK
    q = q_ref[0] * (dh ** -0.5)
    kvn = kvn_ref[0]
    lane = lax.broadcasted_iota(i32, (NSA_HEADS, page), 1)
    s_parts, d_parts = [], []
    for j in range(n_sel):
        nb = sel_ref[(b * NSA_KV_GROUPS + g) * n_sel + j]
        in_blk = (lane >> int(math.log2(NSA_BLOCK))) == (nb % per_page)
        d_parts.append(jnp.where(in_blk, past - ((nb // per_page) * page + lane), -1))
        s_parts.append(_mm3(q, k_refs[j][0, 0, 0]))
    d = jnp.concatenate(d_parts, axis=1)
    bias = jnp.broadcast_to(rb_ref[0], d.shape)
    for k in range(1, REL_BUCKETS):
        bias = jnp.where(d >= _BUCKET_THR[k - 1], rb_ref[k], bias)
    s = jnp.concatenate(s_parts, axis=1) + jnp.where(d < 0, NEG, bias)
    k_n = jnp.where(is_g0, kvn[:, 4 * dh:5 * dh], kvn[:, 5 * dh:6 * dh])
    v_n = jnp.where(is_g0, kvn[:, 6 * dh:7 * dh], kvn[:, 7 * dh:8 * dh])
    s_n = jnp.sum(q * k_n, axis=-1, keepdims=True) + rb_ref[0]
    m = jnp.maximum(jnp.max(s, axis=-1, keepdims=True), s_n)
    p = jnp.exp(s - m)
    p_n = jnp.exp(s_n - m)
    l = jnp.sum(p, axis=-1, keepdims=True) + p_n
    acc = p_n * v_n
    for j in range(n_sel):
        acc = acc + _mm_nt(p[:, j * page:(j + 1) * page], v_refs[j][0, 0, 0])
    o_s = acc / jnp.maximum(l, 1e-30)
    gt = gt_ref[0]
    o = gt[:, 0:1] * oc_ref[0] + gt[:, 1:2] * o_s + gt[:, 2:3] * ow_ref[0]
    row = lax.broadcasted_iota(i32, o.shape, 0)
    in_g = (row >= g * hpg) & (row < (g + 1) * hpg)

    @pl.when(is_g0)
    def _():
        o_ref[0] = o

    @pl.when(jnp.logical_not(is_g0))
    def _():
        o_ref[0] = jnp.where(in_g, o, o_ref[0])


def _nsa_sample_t(q_b, gates, kv4_new, kvwin_new, cache_kv, cache_win, page_table, rel_bias):
    db = q_b.shape[0]
    dh = NSA_DH
    n_pool, page = cache_kv.shape[:2]
    n_pages = page_table.shape[1]
    past = n_pages * page
    nblk = past // NSA_BLOCK
    wb = cache_win.shape[1]
    assert nblk == PAIR and page % NSA_BLOCK == 0 and wb == NSA_WINDOW and nblk > NSA_LOCAL
    cache_t = jnp.transpose(cache_kv, (0, 2, 3, 4, 1)).astype(f32)
    win_t = jnp.transpose(cache_win, (0, 2, 3, 4, 1)).astype(f32)
    kcvc = _nsa_st_means(cache_t, page_table)

    n = np.arange(nblk)
    d_cmp = past - (n * NSA_BLOCK + NSA_BLOCK - 1)
    jw = np.arange(wb)
    d_win = np.where(jw >= 1, wb - jw, -1)
    dist = np.broadcast_to(np.concatenate([d_cmp, d_win])[None, :], (NSA_HEADS, nblk + wb))
    tab = _bias_table(rel_bias, jnp.asarray(dist[None, None], i32), slab=1)[0, 0]
    t_cmp, t_win = tab[:, :nblk], tab[:, nblk:]
    rb = rel_bias.astype(f32)
    b0 = rb[0].reshape(NSA_HEADS, 1)

    q3 = q_b.reshape(db, NSA_HEADS, dh)
    sb = max(s for s in (8, 4, 2, 1) if db % s == 0)
    o_c, o_w, sel = pl.pallas_call(
        _nsa_st_scores_kernel,
        grid=(db // sb,),
        in_specs=[pl.BlockSpec((sb, NSA_HEADS, dh), lambda b: (b, 0, 0)),
                  pl.BlockSpec((sb, 2 * NSA_KV_GROUPS, dh, nblk), lambda b: (b, 0, 0, 0)),
                  pl.BlockSpec((sb, 2, NSA_KV_GROUPS, dh, wb), lambda b: (b, 0, 0, 0, 0)),
                  pl.BlockSpec((sb, 1, 4 * dh), lambda b: (b, 0, 0)),
                  pl.BlockSpec((NSA_HEADS, nblk), lambda b: (0, 0)),
                  pl.BlockSpec((NSA_HEADS, wb), lambda b: (0, 0)),
                  pl.BlockSpec((NSA_HEADS, 1), lambda b: (0, 0))],
        out_specs=[pl.BlockSpec((sb, NSA_HEADS, dh), lambda b: (b, 0, 0)),
                   pl.BlockSpec((sb, NSA_HEADS, dh), lambda b: (b, 0, 0)),
                   pl.BlockSpec((sb, NSA_KV_GROUPS, nblk), lambda b: (b, 0, 0))],
        out_shape=[jax.ShapeDtypeStruct((db, NSA_HEADS, dh), f32),
                   jax.ShapeDtypeStruct((db, NSA_HEADS, dh), f32),
                   jax.ShapeDtypeStruct((db, NSA_KV_GROUPS, nblk), i32)],
        compiler_params=_cparams(("parallel",)),
        name="nsa_sample_scores",
    )(q3, kcvc, win_t, kvwin_new.reshape(db, 1, 4 * dh), t_cmp, t_win, b0)

    n_sel = NSA_TOPN - 1
    sel_flat = sel[:, :, :n_sel].reshape(-1)
    per_page = page // NSA_BLOCK

    def page_spec(j, plane):
        def imap(b, g, s, p):
            nb = s[(b * NSA_KV_GROUPS + g) * n_sel + j]
            return (p[b * n_pages + nb // per_page], plane, g, 0, 0)
        return pl.BlockSpec((1, 1, 1, dh, page), imap)

    const = lambda shape: pl.BlockSpec(shape, lambda b, g, s, p: (0,) * len(shape))
    per_b = lambda shape: pl.BlockSpec((1,) + shape, lambda b, g, s, p: (b,) + (0,) * len(shape))
    o = pl.pallas_call(
        functools.partial(_nsa_st_sel_kernel, n_sel=n_sel, past=past),
        grid_spec=pltpu.PrefetchScalarGridSpec(
            num_scalar_prefetch=2,
            grid=(db, NSA_KV_GROUPS),
            in_specs=[per_b((NSA_HEADS, dh)), const((REL_BUCKETS, NSA_HEADS, 1)), per_b((1, 8 * dh)),
                      per_b((NSA_HEADS, dh)), per_b((NSA_HEADS, dh)), per_b((NSA_HEADS, 3))]
                     + [page_spec(j, 2) for j in range(n_sel)] + [page_spec(j, 3) for j in range(n_sel)],
            out_specs=per_b((NSA_HEADS, dh))),
        out_shape=jax.ShapeDtypeStruct((db, NSA_HEADS, dh), f32),
        compiler_params=_cparams(("parallel", "arbitrary")),
        name="nsa_sample_selected",
    )(sel_flat, page_table.reshape(-1), q3, rb.reshape(REL_BUCKETS, NSA_HEADS, 1), kv4_new.reshape(db, 1, 8 * dh),
      o_c, o_w, gates, *([cache_t] * (2 * n_sel)))
    return o.reshape(db, NSA_HEADS * dh)


def _gelu_tanh(x):
    return 0.5 * x * (1.0 + jnp.tanh(math.sqrt(2.0 / math.pi) * (x + 0.044715 * (x * x * x))))


def _lru_gates(xc, wa_ref, wx_ref, ba_ref, bx_ref, lam_ref, prec):
    r_parts, i_parts = [], []
    for n in range(RNN_BLOCKS):
        xb = xc[:, n * RNN_BW:(n + 1) * RNN_BW]
        r_parts.append(_dotp(xb, wa_ref[n], prec))
        i_parts.append(_dotp(xb, wx_ref[n], prec))
    r = jax.nn.sigmoid(jnp.concatenate(r_parts, axis=1) + ba_ref[...])
    i = jax.nn.sigmoid(jnp.concatenate(i_parts, axis=1) + bx_ref[...])
    log_a = -RG_C * r * _softplus(-lam_ref[...])
    a = jnp.exp(log_a)
    t = jnp.tanh(log_a)
    b = jnp.sqrt(jnp.maximum(-2.0 * t / (1.0 - t), 0.0)) * (i * xc)
    return a, b


def _lru_kernel(rec_ref, gate_ref, cw_ref, cb_ref, wa_ref, wx_ref, ba_ref, bx_ref, lam_ref, cinit_ref, h0_ref,
                y_ref, hfin_ref, xc_scr, a_scr, b_scr, hs_scr, h_scr, *, tb):
    j = pl.program_id(1)

    @pl.when(j == 0)
    def _():
        xc_scr[0:8, :] = cinit_ref[0]
        h_scr[...] = h0_ref[0]

    x = rec_ref[0]
    xc_scr[8:8 + tb, :] = x
    xc = xc_scr[5:5 + tb, :] * cw_ref[0:1, :]
    xc = xc + xc_scr[6:6 + tb, :] * cw_ref[1:2, :]
    xc = xc + xc_scr[7:7 + tb, :] * cw_ref[2:3, :]
    xc = xc + x * cw_ref[3:4, :]
    xc = xc + cb_ref[...]
    xc_scr[0:8, :] = xc_scr[tb:tb + 8, :]
    a, b = _lru_gates(xc, wa_ref, wx_ref, ba_ref, bx_ref, lam_ref, 1)
    a_scr[...] = a
    b_scr[...] = b

    def step(t, h):
        h = a_scr[pl.ds(t, 1), :] * h + b_scr[pl.ds(t, 1), :]
        hs_scr[pl.ds(t, 1), :] = h
        return h

    h = lax.fori_loop(0, tb, step, h_scr[...], unroll=8)
    h_scr[...] = h
    y_ref[0] = _gelu_tanh(gate_ref[0]) * hs_scr[...]

    @pl.when(j == pl.num_programs(1) - 1)
    def _():
        hfin_ref[0] = h


def _lru_prompt(rec, gate, conv_w, conv_b, wa, wx, ba, bx, lam, *, tb):
    bsz, t, w = rec.shape
    row = lambda a: a.reshape(1, w).astype(f32)
    cinit = jnp.zeros((bsz, 8, w), f32)
    h0 = jnp.zeros((bsz, 1, w), f32)
    full = lambda shape: pl.BlockSpec(shape, lambda b, j: (0,) * len(shape))
    return pl.pallas_call(
        functools.partial(_lru_kernel, tb=tb),
        grid=(bsz, t // tb),
        in_specs=[pl.BlockSpec((1, tb, w), lambda b, j: (b, j, 0)),
                  pl.BlockSpec((1, tb, w), lambda b, j: (b, j, 0)),
                  full((RNN_CONV, w)), full((1, w)),
                  full((RNN_BLOCKS, RNN_BW, RNN_BW)), full((RNN_BLOCKS, RNN_BW, RNN_BW)),
                  full((1, w)), full((1, w)), full((1, w)),
                  pl.BlockSpec((1, 8, w), lambda b, j: (b, 0, 0)),
                  pl.BlockSpec((1, 1, w), lambda b, j: (b, 0, 0))],
        out_specs=[pl.BlockSpec((1, tb, w), lambda b, j: (b, j, 0)),
                   pl.BlockSpec((1, 1, w), lambda b, j: (b, 0, 0))],
        out_shape=[jax.ShapeDtypeStruct((bsz, t, w), f32), jax.ShapeDtypeStruct((bsz, 1, w), f32)],
        scratch_shapes=[pltpu.VMEM((tb + 8, w), f32), pltpu.VMEM((tb, w), f32), pltpu.VMEM((tb, w), f32),
                        pltpu.VMEM((tb, w), f32), pltpu.VMEM((1, w), f32)],
        compiler_params=_cparams(("parallel", "arbitrary")),
        name="rglru",
    )(rec, gate, conv_w.astype(f32), row(conv_b), wa.astype(f32), wx.astype(f32), row(ba), row(bx), row(lam),
      cinit, h0)


def _lru_step_kernel(rec_ref, gate_ref, b0_ref, b1_ref, b2_ref, cw_ref, cb_ref, wa_ref, wx_ref, ba_ref, bx_ref,
                     lam_ref, h0_ref, y_ref, h_ref):
    xc = b0_ref[...] * cw_ref[0:1, :]
    xc = xc + b1_ref[...] * cw_ref[1:2, :]
    xc = xc + b2_ref[...] * cw_ref[2:3, :]
    xc = xc + rec_ref[...] * cw_ref[3:4, :]
    xc = xc + cb_ref[...]
    a, b = _lru_gates(xc, wa_ref, wx_ref, ba_ref, bx_ref, lam_ref, 3)
    h = a * h0_ref[...] + b
    h_ref[...] = h
    y_ref[...] = _gelu_tanh(gate_ref[...]) * h


def _lru_sample(rec, gate, conv_buf, h0, conv_w, conv_b, wa, wx, ba, bx, lam):
    db, w = rec.shape
    row = lambda a: a.reshape(1, w).astype(f32)
    buf = conv_buf.astype(f32)
    return pl.pallas_call(
        _lru_step_kernel,
        out_shape=[jax.ShapeDtypeStruct((db, w), f32), jax.ShapeDtypeStruct((db, w), f32)],
        compiler_params=pltpu.CompilerParams(vmem_limit_bytes=VMEM_LIMIT),
        name="rglru_step",
    )(rec, gate, buf[:, 0], buf[:, 1], buf[:, 2], conv_w.astype(f32), row(conv_b), wa.astype(f32), wx.astype(f32),
      row(ba), row(bx), row(lam), h0.astype(f32))


def _top2_kernel(l_ref, e_ref, g_ref, cnt_ref, cnt_scr):
    @pl.when(pl.program_id(0) == 0)
    def _():
        cnt_scr[...] = jnp.zeros_like(cnt_scr)

    lg = l_ref[...]
    tm, w = lg.shape
    lane = lax.broadcasted_iota(i32, lg.shape, 1)
    lg = jnp.where(lane < N_EXPERTS, lg, -jnp.inf)
    m1 = jnp.max(lg, axis=-1, keepdims=True)
    i1 = jnp.min(jnp.where(lg == m1, lane, w), axis=-1, keepdims=True)
    lg2 = jnp.where(lane == i1, -jnp.inf, lg)
    m2 = jnp.max(lg2, axis=-1, keepdims=True)
    i2 = jnp.min(jnp.where(lg2 == m2, lane, w), axis=-1, keepdims=True)
    e2 = jnp.exp(m2 - m1)
    den = 1.0 + e2
    g_ref[...] = jnp.where(lane == 0, 1.0 / den, jnp.where(lane == 1, e2 / den, 0.0))
    hit1, hit2 = lane == i1, lane == i2
    routed = (hit1 | hit2).astype(bf16)
    r_i = lax.broadcasted_iota(i32, (tm, tm), 0)
    c_i = lax.broadcasted_iota(i32, (tm, tm), 1)
    before = jnp.dot((c_i < r_i).astype(bf16), routed, preferred_element_type=f32) + cnt_scr[...]
    rank1 = jnp.sum(jnp.where(hit1, before, 0.0), axis=-1, keepdims=True).astype(i32)
    rank2 = jnp.sum(jnp.where(hit2, before, 0.0), axis=-1, keepdims=True).astype(i32)
    e_ref[...] = jnp.where(lane == 0, i1, jnp.where(lane == 1, i2, jnp.where(lane == 2, rank1,
                                                                               jnp.where(lane == 3, rank2, 0))))
    cnt_scr[...] += jnp.sum(routed.astype(f32), axis=0, keepdims=True)
    cnt_ref[...] = cnt_scr[...].astype(i32)


def _top2(logits, *, tm):
    n, w = logits.shape
    return pl.pallas_call(
        _top2_kernel,
        grid=(n // tm,),
        in_specs=[pl.BlockSpec((tm, w), lambda i: (i, 0))],
        out_specs=[pl.BlockSpec((tm, w), lambda i: (i, 0)), pl.BlockSpec((tm, w), lambda i: (i, 0)),
                   pl.BlockSpec((1, w), lambda i: (0, 0))],
        out_shape=[jax.ShapeDtypeStruct((n, w), i32), jax.ShapeDtypeStruct((n, w), f32),
                   jax.ShapeDtypeStruct((1, w), i32)],
        scratch_shapes=[pltpu.VMEM((1, w), f32)],
        compiler_params=_cparams(("arbitrary",)),
        name="moe_top2",
    )(logits)


def _moe_kernel(be_ref, nu_ref, nv_ref, xs_ref, w1_ref, w3_ref, w2_ref, o_ref, acc_scr):
    del be_ref
    i = pl.program_id(0)
    f = pl.program_id(1)
    last = pl.num_programs(1) - 1
    used = i < nu_ref[0]
    half = xs_ref.shape[0] // 2
    top_only = nv_ref[i] <= half

    @pl.when(used & (f == 0))
    def _():
        acc_scr[...] = jnp.zeros_like(acc_scr)

    def update(rows):
        x = xs_ref[rows, :]
        act = _silu(_mm(x, w1_ref[0])) * _mm(x, w3_ref[0])
        acc_scr[rows, :] += _mm(act, w2_ref[0])

    @pl.when(used & jnp.logical_not(top_only))
    def _():
        update(slice(None))

    @pl.when(used & top_only)
    def _():
        update(slice(0, half))

    @pl.when(used & (f == last))
    def _():
        o_ref[...] = acc_scr[...]

    @pl.when(jnp.logical_not(used) & (f == last))
    def _():
        o_ref[...] = jnp.zeros_like(o_ref)


def _moe_experts(xs, blk_e, n_used, n_valid, w1, w3, w2, *, tm, tf):
    rows, d = xs.shape
    fdim = w1.shape[2]
    nf = fdim // tf

    def wcol(i, f, be, nu, nv):
        return (be[i], 0, jnp.where(i < nu[0], f, nf - 1))

    def wrow(i, f, be, nu, nv):
        return (be[i], jnp.where(i < nu[0], f, nf - 1), 0)

    return pl.pallas_call(
        _moe_kernel,
        grid_spec=pltpu.PrefetchScalarGridSpec(
            num_scalar_prefetch=3,
            grid=(rows // tm, nf),
            in_specs=[pl.BlockSpec((tm, d), lambda i, f, be, nu, nv: (i, 0)),
                      pl.BlockSpec((1, d, tf), wcol),
                      pl.BlockSpec((1, d, tf), wcol),
                      pl.BlockSpec((1, tf, d), wrow)],
            out_specs=pl.BlockSpec((tm, d), lambda i, f, be, nu, nv: (i, 0)),
            scratch_shapes=[pltpu.VMEM((tm, d), f32)]),
        out_shape=jax.ShapeDtypeStruct((rows, d), f32),
        compiler_params=_cparams(("arbitrary", "arbitrary")),
        name="moe_experts",
    )(blk_e, n_used, n_valid, xs, w1, w3, w2)


def _combine_kernel(x_ref, gate_ref, y0_ref, y1_ref, g_ref, nw_ref, o_ref):
    gw = g_ref[...]
    y = y0_ref[...] * gw[:, 0:1] + y1_ref[...] * gw[:, 1:2]
    x = x_ref[0] + gate_ref[0] * y
    o_ref[0] = x * lax.rsqrt(jnp.mean(x * x, axis=-1, keepdims=True) + EPS) * nw_ref[...]


def _moe_combine(x, gate, y0, y1, gw, norm_w, *, tm, row0):
    bsz, t, d = x.shape
    assert row0 % tm == 0
    tok = pl.BlockSpec((1, tm, d), lambda b, i: (b, i, 0))
    flat = lambda w: pl.BlockSpec((tm, w), lambda b, i: (row0 // tm + b * (t // tm) + i, 0))
    return pl.pallas_call(
        _combine_kernel,
        grid=(bsz, t // tm),
        in_specs=[tok, _mod_spec(gate, tm), flat(d), flat(d), flat(gw.shape[-1]),
                  pl.BlockSpec((1, d), lambda b, i: (0, 0))],
        out_specs=tok,
        out_shape=jax.ShapeDtypeStruct((bsz, t, d), f32),
        compiler_params=_cparams(("parallel", "parallel")),
        name="moe_combine",
    )(x, gate, y0, y1, gw, norm_w.reshape(1, d).astype(f32))


MOE_TM = 1024
MOE_TF = 512


def _moe_dispatch(e_idx, rank, counts, n_tok):
    tm = MOE_TM
    n_assign = n_tok * TOP_K
    padded = (counts + tm - 1) // tm * tm
    pend = jnp.cumsum(padded)
    pstart = pend - padded
    experts = jnp.arange(N_EXPERTS, dtype=i32)
    dest = jnp.sum(jnp.where(e_idx[:, :, None] == experts, pstart, 0), axis=-1) + rank
    n_blocks = -(-n_assign // tm) + N_EXPERTS
    rows = n_blocks * tm
    tok = jnp.broadcast_to(jnp.arange(n_tok, dtype=i32)[:, None], (n_tok, TOP_K))
    row_tok = (jnp.arange(rows, dtype=i32) % n_tok).at[dest.reshape(-1)].set(
        tok.reshape(-1), unique_indices=True, mode='promise_in_bounds')
    n_used = (pend[-1] // tm).astype(i32)
    blk = jnp.minimum(jnp.arange(n_blocks, dtype=i32), n_used - 1) * tm
    blk_e = jnp.minimum(jnp.sum((blk[:, None] >= pend[None, :]).astype(i32), axis=1), N_EXPERTS - 1)
    is_e = blk_e[:, None] == experts[None, :]
    first_row = jnp.sum(jnp.where(is_e, pstart, 0), axis=1)
    n_valid = jnp.clip(jnp.sum(jnp.where(is_e, counts, 0), axis=1) - (blk - first_row), 0, tm)
    return dest, row_tok, blk_e, n_used.reshape(1), n_valid.astype(i32)


def _w_in0_layout(w_in0):
    c_qkv = GDN_CONV_CH
    c_z = c_qkv + GDN_HEADS * GDN_DV
    c_ab = c_z + 2 * GDN_HEADS
    c_q = c_ab + NSA_HEADS * NSA_DH
    c_kv = c_q + 6 * NSA_KV_GROUPS * NSA_DH
    c_g = c_kv + 3 * NSA_HEADS
    assert c_g == w_in0.shape[1]
    n_misc = 2 * GDN_HEADS + 3 * NSA_HEADS
    w = jnp.concatenate([w_in0[:, :c_z], w_in0[:, c_ab:c_kv], w_in0[:, c_z:c_ab], w_in0[:, c_kv:c_g],
                         jnp.zeros((w_in0.shape[0], MISC_W - n_misc), w_in0.dtype)], axis=1)
    widths = (GDN_CONV_CH, GDN_HEADS * GDN_DV, NSA_HEADS * NSA_DH, 4 * NSA_KV_GROUPS * NSA_DH,
              2 * NSA_KV_GROUPS * NSA_DH, MISC_W)
    splits, s = [], 0
    for wd in widths:
        splits.append((s, s + wd))
        s += wd
    return w, tuple(splits)


def kernel(x_prompt, x_sample, c_prompt, c_sample, cache_nsa_kv, cache_nsa_win, state_gdn, state_gdn_conv, state_lru, state_lru_conv, page_table, rel_bias, w_ada, b_ada, norm_mix, norm_ffn, norm_final, w_in0, gdn_conv_w, gdn_a_log, gdn_dt_bias, gdn_norm_w, w_out0, ffn_w_gate, ffn_w_up, ffn_w_down, w_in1, lru_conv_w, lru_conv_b, lru_wa, lru_ba, lru_wx, lru_bx, lru_lambda, w_out1, moe_router, moe_w1, moe_w3, moe_w2):
    bsz, seq, d = x_prompt.shape
    db = x_sample.shape[0]
    assert x_sample.shape[1] == 1
    dh = NSA_DH

    n_c = bsz + db
    n_c_pad = -(-n_c // 8) * 8
    c_all = jnp.concatenate([c_prompt, c_sample, jnp.zeros((n_c_pad - n_c, d), f32)], axis=0)
    mods = _adaln(c_all, w_ada, b_ada).reshape(2, n_c_pad, N_MOD, d)
    mod_p = [[mods[l, :bsz, k].reshape(bsz, 1, d) for k in range(N_MOD)] for l in range(2)]
    mod_s = [[mods[l, bsz:n_c, k].reshape(1, db, d) for k in range(N_MOD)] for l in range(2)]

    w0, splits0 = _w_in0_layout(w_in0)
    splits1 = ((0, RNN_WIDTH), (RNN_WIDTH, 2 * RNN_WIDTH))
    router = jnp.concatenate([moe_router, jnp.zeros((d, MISC_W - N_EXPERTS), f32)], axis=1)
    bf = lambda w: w.astype(bf16)

    tm = 512
    xp = x_prompt
    qkv, z, q_b, kv4, kvwin, misc = _mod_matmul(xp, norm_mix[0], mod_p[0][0], mod_p[0][1], bf(w0), splits0,
                                                tm=tm, prec=1)
    o_a, p_gdn = _gdn(qkv, z, misc, gdn_conv_w, jnp.zeros((bsz, GDN_CONV - 1, GDN_CONV_CH), f32), gdn_a_log,
                      gdn_dt_bias, gdn_norm_w, jnp.zeros((bsz, GDN_HEADS, GDN_DK, GDN_DV), f32), tb=256, n_valid=seq)
    p_gdn_conv = qkv[:, seq - (GDN_CONV - 1):]
    o_b = _nsa_prompt(q_b, misc, kv4, kvwin, rel_bias)
    p_nsa_kv = kv4.reshape(bsz, seq, 4, NSA_KV_GROUPS, dh)
    keep = min(NSA_WINDOW, seq)
    p_nsa_win = kvwin[:, seq - keep:].reshape(bsz, keep, 2, NSA_KV_GROUPS, dh)
    xp = _proj_residual([o_a, o_b], bf(w_out0), xp, mod_p[0][2], tm=tm, prec=1)
    xp = _ffn(xp, norm_ffn[0], mod_p[0][3], mod_p[0][4], mod_p[0][5], bf(ffn_w_gate), bf(ffn_w_up), bf(ffn_w_down),
              tm=tm, tf=FFN_DIM // 2, prec=1)
    gate_br, rec_br = _mod_matmul(xp, norm_mix[1], mod_p[1][0], mod_p[1][1], bf(w_in1), splits1, tm=tm, prec=1)
    y_in, p_lru = _lru_prompt(rec_br, gate_br, lru_conv_w, lru_conv_b, lru_wa, lru_wx, lru_ba, lru_bx, lru_lambda,
                              tb=256)
    p_lru_conv = rec_br[:, seq - (RNN_CONV - 1):]
    xp = _proj_residual([y_in], bf(w_out1), xp, mod_p[1][2], tm=tm, prec=1)
    logit_p, h_p = _mod_matmul(xp, norm_ffn[1], mod_p[1][3], mod_p[1][4], router, ((0, MISC_W),), tm=tm, prec=3,
                               emit_h=f32)

    xs = x_sample.reshape(1, db, d)
    qkv_s, z_s, q_s, kv4_s, kvwin_s, misc_s = _mod_matmul(xs, norm_mix[0], mod_s[0][0], mod_s[0][1], w0, splits0,
                                                          tm=db, prec=3)
    c = GDN_CHUNK
    tpad = lambda a: jnp.pad(a.reshape(db, 1, a.shape[-1]), ((0, 0), (0, c - 1), (0, 0)))
    o_a_s, s_gdn = _gdn(tpad(qkv_s), tpad(z_s), tpad(misc_s), gdn_conv_w, state_gdn_conv, gdn_a_log, gdn_dt_bias,
                        gdn_norm_w, state_gdn, tb=c, n_valid=1)
    o_a_s = o_a_s[:, 0].reshape(1, db, GDN_HEADS * GDN_DV)
    s_gdn_conv = jnp.concatenate([state_gdn_conv[:, 1:], qkv_s.reshape(db, 1, GDN_CONV_CH)], axis=1)
    gates_s = jax.nn.sigmoid(misc_s[0, :, 2 * GDN_HEADS:2 * GDN_HEADS + 3 * NSA_HEADS]).reshape(db, NSA_HEADS, 3)
    o_b_s = _nsa_sample_t(q_s[0], gates_s, kv4_s[0], kvwin_s[0], cache_nsa_kv, cache_nsa_win, page_table, rel_bias)
    s_nsa_kv = kv4_s.reshape(db, 1, 4, NSA_KV_GROUPS, dh)
    s_nsa_win = jnp.concatenate([cache_nsa_win[:, 1:],
                                 kvwin_s.reshape(db, 1, 2, NSA_KV_GROUPS, dh).astype(cache_nsa_win.dtype)], axis=1)
    xs = _proj_residual([o_a_s, o_b_s.reshape(1, db, NSA_HEADS * dh)], w_out0, xs, mod_s[0][2], tm=db, prec=3)
    xs = _ffn(xs, norm_ffn[0], mod_s[0][3], mod_s[0][4], mod_s[0][5], ffn_w_gate, ffn_w_up, ffn_w_down,
              tm=db, tf=256, prec=3)
    gate_s, rec_s = _mod_matmul(xs, norm_mix[1], mod_s[1][0], mod_s[1][1], w_in1, splits1, tm=db, prec=3)
    y_in_s, s_lru = _lru_sample(rec_s[0], gate_s[0], state_lru_conv, state_lru, lru_conv_w, lru_conv_b, lru_wa, lru_wx,
                                lru_ba, lru_bx, lru_lambda)
    s_lru_conv = jnp.concatenate([state_lru_conv[:, 1:], rec_s.reshape(db, 1, RNN_WIDTH)], axis=1)
    xs = _proj_residual([y_in_s.reshape(1, db, RNN_WIDTH)], w_out1, xs, mod_s[1][2], tm=db, prec=3)
    logit_s, h_s = _mod_matmul(xs, norm_ffn[1], mod_s[1][3], mod_s[1][4], router, ((0, MISC_W),), tm=db, prec=3,
                               emit_h=f32)

    n_p = bsz * seq
    n_tok = n_p + db
    logits = jnp.concatenate([logit_p.reshape(n_p, MISC_W), logit_s.reshape(db, MISC_W)], axis=0)
    h_all = jnp.concatenate([h_p.reshape(n_p, d), h_s.reshape(db, d)], axis=0)
    route, gw, cnt = _top2(logits, tm=max(t for t in range(8, 1025, 8) if n_tok % t == 0))
    dest, row_tok, blk_e, n_used, n_valid = _moe_dispatch(route[:, :TOP_K], route[:, TOP_K:2 * TOP_K],
                                                          cnt[0, :N_EXPERTS], n_tok)
    yb = _moe_experts(h_all[row_tok], blk_e, n_used, n_valid, moe_w1, moe_w3, moe_w2, tm=MOE_TM, tf=MOE_TF)
    y0, y1 = yb[dest[:, 0]], yb[dest[:, 1]]
    y_prompt = _moe_combine(xp, mod_p[1][5], y0, y1, gw, norm_final, tm=tm, row0=0)
    y_sample = _moe_combine(xs, mod_s[1][5], y0, y1, gw, norm_final, tm=db, row0=n_p)

    return (y_prompt, y_sample.reshape(db, 1, d),
            p_nsa_kv, p_nsa_win, p_gdn, p_gdn_conv, p_lru.reshape(bsz, RNN_WIDTH), p_lru_conv,
            s_nsa_kv, s_nsa_win, s_gdn, s_gdn_conv, s_lru, s_lru_conv)
```

```python
import functools
import math

import numpy as np
import jax
import jax.numpy as jnp
from jax import lax
from jax.experimental import pallas as pl
from jax.experimental.pallas import tpu as pltpu

f32 = jnp.float32
bf16 = jnp.bfloat16
i32 = jnp.int32

D_MODEL = 1024
EPS = 1e-6
N_MOD = 6
GDN_HEADS = 4
GDN_DK = 128
GDN_DV = 128
GDN_CONV = 4
GDN_CHUNK = 64
GDN_CONV_CH = GDN_HEADS * (2 * GDN_DK + GDN_DV)
NSA_HEADS = 8
NSA_KV_GROUPS = 2
NSA_HPG = NSA_HEADS // NSA_KV_GROUPS
NSA_DH = 64
NSA_BLOCK = 64
NSA_TOPN = 16
NSA_LOCAL = 2
NSA_WINDOW = 512
NSA_QBLOCK = 128
REL_BUCKETS = 32
REL_MAX_DIST = 2048
RNN_WIDTH = D_MODEL
RNN_BLOCKS = 8
RNN_BW = RNN_WIDTH // RNN_BLOCKS
RNN_CONV = 4
RG_C = 8.0
FFN_DIM = 2816
N_EXPERTS = 8
TOP_K = 2
EXPERT_DIM = 3584

NEG = -1e30
PAIR = 2 * NSA_BLOCK
NEAR_PAIRS = 13
MISC_W = 128
VMEM_LIMIT = 56 * 1024 * 1024


def _cparams(sem, vmem=VMEM_LIMIT):
    return pltpu.CompilerParams(dimension_semantics=sem, vmem_limit_bytes=vmem)


def _mm(a, b):
    return jnp.dot(a.astype(bf16), b.astype(bf16), preferred_element_type=f32)


def _mm_nt(a, b):
    return lax.dot_general(a.astype(bf16), b.astype(bf16), (((1,), (1,)), ((), ())),
                           preferred_element_type=f32)


def _split2(a):
    hi = a.astype(bf16)
    lo = (a - hi.astype(f32)).astype(bf16)
    return hi, lo


def _mm3(a, b):
    ah, al = _split2(a)
    bh, bl = _split2(b)
    return (jnp.dot(ah, bh, preferred_element_type=f32) + jnp.dot(ah, bl, preferred_element_type=f32)
            + jnp.dot(al, bh, preferred_element_type=f32))


def _mm3_nt(a, b):
    ah, al = _split2(a)
    bh, bl = _split2(b)
    dn = (((1,), (1,)), ((), ()))
    return (lax.dot_general(ah, bh, dn, preferred_element_type=f32)
            + lax.dot_general(ah, bl, dn, preferred_element_type=f32)
            + lax.dot_general(al, bh, dn, preferred_element_type=f32))


def _mm_01(m01, a):
    hi = a.astype(bf16)
    r1 = a - hi.astype(f32)
    mid = r1.astype(bf16)
    lo = (r1 - mid.astype(f32)).astype(bf16)
    return (jnp.dot(m01, hi, preferred_element_type=f32) + jnp.dot(m01, mid, preferred_element_type=f32)
            + jnp.dot(m01, lo, preferred_element_type=f32))


def _dotp(a, b, prec):
    return _mm3(a, b) if prec == 3 else _mm(a, b)


def _silu(x):
    return x * jax.nn.sigmoid(x)


def _softplus(x):
    return jnp.maximum(x, 0.0) + jnp.log1p(jnp.exp(-jnp.abs(x)))


def _modulate(x, gain, shift, scale):
    r = lax.rsqrt(jnp.mean(x * x, axis=-1, keepdims=True) + EPS)
    return x * r * gain * (1.0 + scale) + shift


def _ada_kernel(c_ref, w_ref, b_ref, o_ref):
    o_ref[0] = _mm3(_silu(c_ref[...]), w_ref[0]) + b_ref[0]


def _adaln(c_all, w_ada, b_ada):
    rows = c_all.shape[0]
    depth, d, n = w_ada.shape
    tn = 1536
    return pl.pallas_call(
        _ada_kernel,
        grid=(depth, n // tn),
        in_specs=[pl.BlockSpec((rows, d), lambda l, j: (0, 0)),
                  pl.BlockSpec((1, d, tn), lambda l, j: (l, 0, j)),
                  pl.BlockSpec((1, 1, tn), lambda l, j: (l, 0, j))],
        out_specs=pl.BlockSpec((1, rows, tn), lambda l, j: (l, 0, j)),
        out_shape=jax.ShapeDtypeStruct((depth, rows, n), f32),
        compiler_params=_cparams(("arbitrary", "arbitrary")),
        name="adaln",
    )(c_all, w_ada, b_ada.reshape(depth, 1, n))


def _mod_spec(mod, tm):
    r = mod.shape[1]
    if r == 1:
        return pl.BlockSpec((1, 1, mod.shape[2]), lambda b, i: (b, 0, 0))
    return pl.BlockSpec((1, tm, mod.shape[2]), lambda b, i: (b, i, 0))


def _modmm_kernel(x_ref, gain_ref, shift_ref, scale_ref, w_ref, *o_refs, splits, prec, emit_h):
    h = _modulate(x_ref[0], gain_ref[...], shift_ref[0], scale_ref[0])
    if emit_h:
        o_refs[-1][0] = h.astype(o_refs[-1].dtype)
    hh = _split2(h) if prec == 3 else h.astype(bf16)
    for o_ref, (a, b) in zip(o_refs, splits):
        w = w_ref[:, a:b]
        if prec == 3:
            wh, wl = _split2(w)
            acc = (jnp.dot(hh[0], wh, preferred_element_type=f32) + jnp.dot(hh[0], wl, preferred_element_type=f32)
                   + jnp.dot(hh[1], wh, preferred_element_type=f32))
        else:
            acc = jnp.dot(hh, w, preferred_element_type=f32)
        o_ref[0] = acc


def _mod_matmul(x, gain, shift, scale, w, splits, *, tm, prec, emit_h=None):
    bsz, t, d = x.shape
    out_shape = [jax.ShapeDtypeStruct((bsz, t, b - a), f32) for a, b in splits]
    out_specs = [pl.BlockSpec((1, tm, b - a), lambda bi, i: (bi, i, 0)) for a, b in splits]
    if emit_h is not None:
        out_shape.append(jax.ShapeDtypeStruct((bsz, t, d), emit_h))
        out_specs.append(pl.BlockSpec((1, tm, d), lambda bi, i: (bi, i, 0)))
    return pl.pallas_call(
        functools.partial(_modmm_kernel, splits=tuple(splits), prec=prec, emit_h=emit_h is not None),
        grid=(bsz, t // tm),
        in_specs=[pl.BlockSpec((1, tm, d), lambda bi, i: (bi, i, 0)),
                  pl.BlockSpec((1, d), lambda bi, i: (0, 0)),
                  _mod_spec(shift, tm), _mod_spec(scale, tm),
                  pl.BlockSpec(w.shape, lambda bi, i: (0, 0))],
        out_specs=out_specs,
        out_shape=out_shape,
        compiler_params=_cparams(("parallel", "parallel")),
        name="mod_matmul",
    )(x, gain.reshape(1, d), shift, scale, w)


def _projres_kernel(*refs, n_lhs, ksplits, prec, final_norm):
    lhs = refs[:n_lhs]
    w_ref, x_ref, gate_ref = refs[n_lhs:n_lhs + 3]
    o_ref = refs[-1]
    acc = None
    for l_ref, (a, b) in zip(lhs, ksplits):
        part = _dotp(l_ref[0], w_ref[a:b, :], prec)
        acc = part if acc is None else acc + part
    y = x_ref[0] + gate_ref[0] * acc
    if final_norm:
        nw_ref = refs[n_lhs + 3]
        y = y * lax.rsqrt(jnp.mean(y * y, axis=-1, keepdims=True) + EPS) * nw_ref[...]
    o_ref[0] = y


def _proj_residual(lhs_list, w, x, gate, *, tm, prec, norm_w=None):
    bsz, t, d = x.shape
    ksplits, k0 = [], 0
    for l in lhs_list:
        ksplits.append((k0, k0 + l.shape[-1]))
        k0 += l.shape[-1]
    in_specs = [pl.BlockSpec((1, tm, l.shape[-1]), lambda bi, i: (bi, i, 0)) for l in lhs_list]
    in_specs += [pl.BlockSpec(w.shape, lambda bi, i: (0, 0)),
                 pl.BlockSpec((1, tm, d), lambda bi, i: (bi, i, 0)),
                 _mod_spec(gate, tm)]
    args = list(lhs_list) + [w, x, gate]
    if norm_w is not None:
        in_specs.append(pl.BlockSpec((1, d), lambda bi, i: (0, 0)))
        args.append(norm_w.reshape(1, d))
    return pl.pallas_call(
        functools.partial(_projres_kernel, n_lhs=len(lhs_list), ksplits=tuple(ksplits), prec=prec,
                          final_norm=norm_w is not None),
        grid=(bsz, t // tm),
        in_specs=in_specs,
        out_specs=pl.BlockSpec((1, tm, d), lambda bi, i: (bi, i, 0)),
        out_shape=jax.ShapeDtypeStruct((bsz, t, d), f32),
        compiler_params=_cparams(("parallel", "parallel")),
        name="proj_residual",
    )(*args)


def _ffn_kernel(x_ref, gain_ref, shift_ref, scale_ref, gate_ref, wg_ref, wu_ref, wd_ref, o_ref,
                h_scr, acc_scr, *, prec):
    f = pl.program_id(2)

    @pl.when(f == 0)
    def _():
        h_scr[...] = _modulate(x_ref[0], gain_ref[...], shift_ref[0], scale_ref[0]).astype(h_scr.dtype)
        acc_scr[...] = jnp.zeros_like(acc_scr)

    h = h_scr[...]
    act = _silu(_dotp(h, wg_ref[...], prec)) * _dotp(h, wu_ref[...], prec)
    acc_scr[...] += _dotp(act, wd_ref[...], prec)

    @pl.when(f == pl.num_programs(2) - 1)
    def _():
        o_ref[0] = x_ref[0] + gate_ref[0] * acc_scr[...]


def _ffn(x, gain, shift, scale, gate, wg, wu, wd, *, tm, tf, prec):
    bsz, t, d = x.shape
    fdim = wg.shape[1]
    mod_specs = []
    for mod in (shift, scale, gate):
        if mod.shape[1] == 1:
            mod_specs.append(pl.BlockSpec((1, 1, d), lambda b, i, f: (b, 0, 0)))
        else:
            mod_specs.append(pl.BlockSpec((1, tm, d), lambda b, i, f: (b, i, 0)))
    return pl.pallas_call(
        functools.partial(_ffn_kernel, prec=prec),
        grid=(bsz, t // tm, fdim // tf),
        in_specs=[pl.BlockSpec((1, tm, d), lambda b, i, f: (b, i, 0)),
                  pl.BlockSpec((1, d), lambda b, i, f: (0, 0)),
                  mod_specs[0], mod_specs[1], mod_specs[2],
                  pl.BlockSpec((d, tf), lambda b, i, f: (0, f)),
                  pl.BlockSpec((d, tf), lambda b, i, f: (0, f)),
                  pl.BlockSpec((tf, d), lambda b, i, f: (f, 0))],
        out_specs=pl.BlockSpec((1, tm, d), lambda b, i, f: (b, i, 0)),
        out_shape=jax.ShapeDtypeStruct((bsz, t, d), f32),
        scratch_shapes=[pltpu.VMEM((tm, d), f32 if prec == 3 else bf16), pltpu.VMEM((tm, d), f32)],
        compiler_params=_cparams(("parallel", "parallel", "arbitrary")),
        name="ffn",
    )(x, gain.reshape(1, d), shift, scale, gate, wg, wu, wd)


def _gdn_kernel(qkv_ref, z_ref, misc_ref, cw_ref, cinit_ref, hp_ref, nw_ref, s0_ref,
                o_ref, sfin_ref, xc_scr, s_scr, *, tb, n_valid):
    j = pl.program_id(1)
    c = GDN_CHUNK
    nh = GDN_HEADS

    @pl.when(j == 0)
    def _():
        xc_scr[0:8, :] = cinit_ref[0]
        s_scr[...] = s0_ref[0]

    x = qkv_ref[0]
    xc_scr[8:8 + tb, :] = x
    y = xc_scr[5:5 + tb, :] * cw_ref[0:1, :]
    y = y + xc_scr[6:6 + tb, :] * cw_ref[1:2, :]
    y = y + xc_scr[7:7 + tb, :] * cw_ref[2:3, :]
    y = y + x * cw_ref[3:4, :]
    xc_scr[0:8, :] = xc_scr[tb:tb + 8, :]
    y = _silu(y)

    misc = misc_ref[0]
    row = lax.broadcasted_iota(i32, (tb, MISC_W), 0) + j * tb
    live = row < n_valid
    log_a = jnp.where(live, hp_ref[0:1, :] * _softplus(misc + hp_ref[1:2, :]), 0.0)
    beta = jnp.where(live, jax.nn.sigmoid(misc), 0.0)

    r_i = lax.broadcasted_iota(i32, (tb, tb), 0)
    c_i = lax.broadcasted_iota(i32, (tb, tb), 1)
    sh = int(math.log2(c))
    ltri = (((r_i >> sh) == (c_i >> sh)) & (c_i <= r_i)).astype(bf16)
    g = _mm_01(ltri, log_a)
    g_t = g.T

    ri = lax.broadcasted_iota(i32, (c, c), 0)
    ci = lax.broadcasted_iota(i32, (c, c), 1)
    incl = ci <= ri
    strict = ci < ri
    eye = (ci == ri).astype(f32)
    quad = []
    for lvl in range(int(math.log2(c))):
        quad.append(((ri >> (lvl + 1)) == (ci >> (lvl + 1))) & (((ri >> lvl) & 1) == 1) & (((ci >> lvl) & 1) == 0))

    nchunk = tb // c
    units = [(n, h) for n in range(nchunk) for h in range(nh)]

    def stack(fn):
        return jnp.stack([fn(n * c, h) for n, h in units], axis=0)

    def bmm(eq, a_, b_):
        return jnp.einsum(eq, a_.astype(bf16), b_.astype(bf16), preferred_element_type=f32)

    q = stack(lambda r0, h: y[r0:r0 + c, h * GDN_DK:(h + 1) * GDN_DK])
    k = stack(lambda r0, h: y[r0:r0 + c, (nh + h) * GDN_DK:(nh + h + 1) * GDN_DK])
    v = stack(lambda r0, h: y[r0:r0 + c, 2 * nh * GDN_DK + h * GDN_DV:2 * nh * GDN_DK + (h + 1) * GDN_DV])
    q = q * lax.rsqrt(jnp.sum(q * q, axis=-1, keepdims=True) + EPS) * (GDN_DK ** -0.5)
    k = k * lax.rsqrt(jnp.sum(k * k, axis=-1, keepdims=True) + EPS)
    g_col = stack(lambda r0, h: g[r0:r0 + c, h:h + 1])
    g_row = stack(lambda r0, h: g_t[h:h + 1, r0:r0 + c])
    b_col = stack(lambda r0, h: beta[r0:r0 + c, nh + h:nh + h + 1])
    gam = jnp.where(incl, jnp.exp(jnp.where(incl, g_col - g_row, 0.0)), 0.0)
    kk = bmm('uid,ujd->uij', k, k)
    a = jnp.where(strict, b_col * gam * kk, 0.0)
    p = eye - jnp.where(quad[0], a, 0.0)
    for lvl in range(1, len(quad)):
        m = bmm('uij,ujk->uik', jnp.where(quad[lvl], a, 0.0), p)
        p = p - bmm('uij,ujk->uik', p, m)
    e_g = jnp.exp(g_col)
    sol = bmm('uij,ujd->uid', p, jnp.concatenate([b_col * v, (b_col * e_g) * k], axis=-1))
    vb, w = sol[:, :, :GDN_DV], sol[:, :, GDN_DV:]
    aqk = bmm('uid,ujd->uij', q, k) * gam
    qg = q * e_g
    g_last = g_col[:, c - 1:c, :]
    kd_t = jnp.swapaxes(k * jnp.exp(g_last - g_col), 1, 2)
    gc = jnp.exp(g_last)

    s = s_scr[...]
    for n in range(nchunk):
        sl = slice(n * nh, (n + 1) * nh)
        u = vb[sl] - bmm('hcd,hde->hce', w[sl], s)
        o = bmm('hcd,hde->hce', qg[sl], s) + bmm('hij,hje->hie', aqk[sl], u)
        s = gc[sl] * s + bmm('hdc,hce->hde', kd_t[sl], u)
        o = o * lax.rsqrt(jnp.mean(o * o, axis=-1, keepdims=True) + EPS) * nw_ref[...]
        for h in range(nh):
            zs = z_ref[0, n * c:(n + 1) * c, h * GDN_DV:(h + 1) * GDN_DV]
            o_ref[0, n * c:(n + 1) * c, h * GDN_DV:(h + 1) * GDN_DV] = o[h] * _silu(zs)
    s_scr[...] = s

    @pl.when(j == pl.num_programs(1) - 1)
    def _():
        sfin_ref[0] = s_scr[...]


def _gdn(qkv_raw, z, misc, conv_w, conv_buf, a_log, dt_bias, norm_w, s0, *, tb, n_valid):
    bsz, tp, ch = qkv_raw.shape
    cinit = jnp.concatenate([jnp.zeros((bsz, 5, ch), f32), conv_buf.astype(f32)], axis=1)
    hp = jnp.zeros((8, MISC_W), f32)
    hp = hp.at[0, :GDN_HEADS].set(-jnp.exp(a_log.astype(f32))).at[1, :GDN_HEADS].set(dt_bias.astype(f32))
    zw = GDN_HEADS * GDN_DV
    return pl.pallas_call(
        functools.partial(_gdn_kernel, tb=tb, n_valid=n_valid),
        grid=(bsz, tp // tb),
        in_specs=[pl.BlockSpec((1, tb, ch), lambda b, j: (b, j, 0)),
                  pl.BlockSpec((1, tb, zw), lambda b, j: (b, j, 0)),
                  pl.BlockSpec((1, tb, MISC_W), lambda b, j: (b, j, 0)),
                  pl.BlockSpec((GDN_CONV, ch), lambda b, j: (0, 0)),
                  pl.BlockSpec((1, 8, ch), lambda b, j: (b, 0, 0)),
                  pl.BlockSpec((8, MISC_W), lambda b, j: (0, 0)),
                  pl.BlockSpec((1, GDN_DV), lambda b, j: (0, 0)),
                  pl.BlockSpec((1, GDN_HEADS, GDN_DK, GDN_DV), lambda b, j: (b, 0, 0, 0))],
        out_specs=[pl.BlockSpec((1, tb, zw), lambda b, j: (b, j, 0)),
                   pl.BlockSpec((1, GDN_HEADS, GDN_DK, GDN_DV), lambda b, j: (b, 0, 0, 0))],
        out_shape=[jax.ShapeDtypeStruct((bsz, tp, zw), f32),
                   jax.ShapeDtypeStruct((bsz, GDN_HEADS, GDN_DK, GDN_DV), f32)],
        scratch_shapes=[pltpu.VMEM((tb + 8, ch), f32), pltpu.VMEM((GDN_HEADS, GDN_DK, GDN_DV), f32)],
        compiler_params=_cparams(("parallel", "arbitrary")),
        name="gdn",
    )(qkv_raw, z, misc, conv_w.astype(f32), cinit, hp, norm_w.reshape(1, GDN_DV).astype(f32), s0.astype(f32))


def _bucket_thresholds():
    exact = REL_BUCKETS // 2
    d = np.arange(0, 4 * REL_MAX_DIST, dtype=np.int64)

    def buckets(ft):
        nf = np.maximum(d, exact).astype(ft)
        large = exact + (np.log(nf / ft(exact)) / ft(math.log(REL_MAX_DIST / exact)) * ft(REL_BUCKETS - exact)).astype(np.int32)
        return np.where(d < exact, d, np.minimum(large, REL_BUCKETS - 1))

    b64, b32 = buckets(np.float64), buckets(np.float32)
    assert np.array_equal(b64, b32) and b64[-1] == REL_BUCKETS - 1 and np.all(np.diff(b64) >= 0)
    return [int(np.argmax(b64 >= k)) for k in range(1, REL_BUCKETS)]


_BUCKET_THR = _bucket_thresholds()
FAR_DIST = _BUCKET_THR[-1]


def _bias_kernel(rb_ref, d_ref, o_ref, *, slab, group_heads):
    rows = d_ref.shape[-2]
    for s in range(rows // slab):
        d = d_ref[0, 0, s * slab:(s + 1) * slab, :]
        h = pl.program_id(0) * group_heads + s
        val = jnp.full(d.shape, rb_ref[0, h], f32)
        for k in range(1, REL_BUCKETS):
            val = jnp.where(d >= _BUCKET_THR[k - 1], rb_ref[k, h], val)
        o_ref[0, 0, s * slab:(s + 1) * slab, :] = jnp.where(d < 0, NEG, val)


def _bias_table(rel_bias, dist, *, slab):
    g, nt, rows, cols = dist.shape
    return pl.pallas_call(
        functools.partial(_bias_kernel, slab=slab, group_heads=rows // slab),
        grid=(g, nt),
        in_specs=[pl.BlockSpec(memory_space=pltpu.SMEM),
                  pl.BlockSpec((1, 1, rows, cols), lambda a, b: (a, b, 0, 0))],
        out_specs=pl.BlockSpec((1, 1, rows, cols), lambda a, b: (a, b, 0, 0)),
        out_shape=jax.ShapeDtypeStruct(dist.shape, f32),
        compiler_params=_cparams(("arbitrary", "arbitrary")),
        name="bias_table",
    )(rel_bias.astype(f32), dist)


def _blockmean_kernel(x_ref, o_ref):
    x = x_ref[0]
    nb = x.shape[0] // NSA_BLOCK
    o_ref[0] = jnp.sum(x.reshape(nb, NSA_BLOCK, x.shape[1]), axis=1) * (1.0 / NSA_BLOCK)


def _block_means(kv4, *, tb):
    bsz, t, _ = kv4.shape
    w = 2 * NSA_KV_GROUPS * NSA_DH
    return pl.pallas_call(
        _blockmean_kernel,
        grid=(bsz, t // tb),
        in_specs=[pl.BlockSpec((1, tb, w), lambda b, i: (b, i, 0))],
        out_specs=pl.BlockSpec((1, tb // NSA_BLOCK, w), lambda b, i: (b, i, 0)),
        out_shape=jax.ShapeDtypeStruct((bsz, t // NSA_BLOCK, w), f32),
        compiler_params=_cparams(("parallel", "parallel")),
        name="nsa_block_means",
    )(kv4)


LOG2E = 1.4426950408889634
NSA_KEY_TILE = 4 * PAIR


def _nsa_prompt_kernel(q_ref, misc_ref, kcvc_ref, kaug_ref, vgt_ref, kw_ref, vw_ref,
                       tabs_ref, tabw_ref, tabc_ref, c31_ref, o_ref, qaug_scr, s_scr):
    g = pl.program_id(1)
    i = pl.program_id(2)
    qb, dh, hpg = NSA_QBLOCK, NSA_DH, NSA_HPG
    is_g0 = g == 0
    nn = (((1,), (1,)), ((), ()))

    def ghalf(a):
        return jnp.where(is_g0, a[:, :dh], a[:, dh:])

    qt = q_ref[0] * (dh ** -0.5)
    qh = [qt[:, h * dh:(h + 1) * dh] for h in range(hpg)]
    kcvc = kcvc_ref[0]
    nblk = kcvc.shape[0]
    kc = ghalf(kcvc[:, :2 * dh])
    vc = ghalf(kcvc[:, 2 * dh:])

    rows = hpg * qb
    qs = jnp.concatenate(qh, axis=0)
    zero = jnp.zeros_like(qs)
    q2f = jnp.concatenate([jnp.where(is_g0, qs, zero), jnp.where(is_g0, zero, qs)], axis=1) * LOG2E
    q2 = q2f.astype(bf16)

    blk4 = lax.broadcasted_iota(i32, (rows, nblk), 1)
    qrow4 = lax.broadcasted_iota(i32, (rows, nblk), 0) & (qb - 1)
    first_tab = 2 * i - 2 * NEAR_PAIRS + 1
    shift = (first_tab + 4 * nblk) % nblk
    readable = blk4 * NSA_BLOCK + (NSA_BLOCK - 1) <= i * qb + qrow4
    bias = jnp.where(blk4 < first_tab, c31_ref[0], pltpu.roll(tabc_ref[0], shift, 1))
    s_c = jnp.where(readable, _mm3_nt(qs, kc) + bias, NEG)
    m_c = jnp.max(s_c, axis=-1, keepdims=True)
    p_c = jnp.where(readable, jnp.exp(s_c - m_c), 0.0)
    p_c = p_c / jnp.maximum(jnp.sum(p_c, axis=-1, keepdims=True), 1e-30)
    o_cmp = _mm(p_c, vc)
    imp = p_c[0:qb]
    for h in range(1, hpg):
        imp = imp + p_c[h * qb:(h + 1) * qb]

    span = NSA_WINDOW + qb
    ws = pl.multiple_of(i * qb, qb)
    s_w = lax.dot_general(q2, kw_ref[0, pl.ds(ws, span), :], nn, preferred_element_type=f32) + tabw_ref[0]
    in_seq = lax.broadcasted_iota(i32, (rows, span), 1) + (i * qb - NSA_WINDOW) >= 0
    s_w = jnp.where(in_seq, s_w, NEG)
    m_w = jnp.max(s_w, axis=-1, keepdims=True)
    p_w = jnp.exp2(s_w - m_w)
    l_w = jnp.sum(p_w, axis=-1, keepdims=True)
    o_win = ghalf(jnp.dot(p_w.astype(bf16), vw_ref[0, pl.ds(ws, span), :], preferred_element_type=f32))
    o_win = o_win / jnp.maximum(l_w, 1e-30)

    blk = lax.broadcasted_iota(i32, (nblk, qb), 0)
    cur = (i * qb + lax.broadcasted_iota(i32, (nblk, qb), 1)) >> int(math.log2(NSA_BLOCK))
    valid = blk <= cur
    forced = valid & ((blk == 0) | (blk > cur - NSA_LOCAL))
    imp_t = jnp.where(forced, -jnp.inf, jnp.where(valid, imp.T, -1.0))
    picked = forced
    for _ in range(min(NSA_TOPN, nblk) - 1 - NSA_LOCAL):
        mx = jnp.max(imp_t, axis=0, keepdims=True)
        first = jnp.min(jnp.where(imp_t == mx, blk, nblk), axis=0, keepdims=True)
        hit = blk == first
        picked = picked | hit
        imp_t = jnp.where(hit, -jnp.inf, imp_t)
    msel_t = jnp.where(picked & valid, 0.0, NEG).astype(bf16)

    qaug_scr[...] = jnp.concatenate([q2f.T.astype(bf16), jnp.concatenate([msel_t] * hpg, axis=1)], axis=0)
    tk = s_scr.shape[0]
    ppt = tk // PAIR
    t_d = i // ppt

    def scores(t):
        ks = pl.multiple_of(t * tk, tk)
        s = jnp.dot(kaug_ref[0, pl.ds(ks, tk), :], qaug_scr[...], preferred_element_type=f32)
        tabs = [tabs_ref[0, jnp.clip(i - ppt * t - j, 0, NEAR_PAIRS)] for j in range(ppt)]
        return s + jnp.concatenate(tabs, axis=0)

    s_scr[...] = scores(t_d)

    def tile(k_it, carry):
        m, acc = carry
        t = t_d - k_it
        s = s_scr[...]
        s_next = scores(jnp.maximum(t - 1, 0))
        m_new = jnp.maximum(m, jnp.max(s, axis=0, keepdims=True))
        alpha = jnp.exp2(m - m_new)
        p = jnp.exp2(s - m_new).astype(bf16)
        ks = pl.multiple_of(t * tk, tk)
        acc = alpha * acc + jnp.dot(vgt_ref[0, 0, :, pl.ds(ks, tk)], p, preferred_element_type=f32)
        s_scr[...] = s_next
        return m_new, acc

    _, acc_t = lax.fori_loop(0, t_d + 1, tile, (jnp.full((1, rows), NEG, f32), jnp.zeros((2 * dh, rows), f32)))
    o_sel = (acc_t[:dh] / jnp.maximum(acc_t[dh:dh + 1], 1e-30)).T

    gates = jax.nn.sigmoid(misc_ref[0])
    outs = []
    for h in range(hpg):
        r = slice(h * qb, (h + 1) * qb)
        acc = None
        for jb, branch in enumerate((o_cmp[r], o_sel[r], o_win[r])):
            la = 8 + 3 * h + jb
            lb = 8 + 3 * (hpg + h) + jb
            term = jnp.where(is_g0, gates[:, la:la + 1], gates[:, lb:lb + 1]) * branch
            acc = term if acc is None else acc + term
        outs.append(acc)
    o_ref[0] = jnp.concatenate(outs, axis=1)


def _nsa_tables(rel_bias):
    qb, hpg = NSA_QBLOCK, NSA_HPG
    rows = hpg * qb
    q = (np.arange(rows) % qb)[:, None]
    c = np.arange(PAIR)[None, :]
    d_sel = [PAIR * idx + q - c for idx in range(NEAR_PAIRS + 1)]
    assert PAIR * NEAR_PAIRS - (PAIR - 1) >= FAR_DIST
    mm = 2 * NEAR_PAIRS - 1 - c
    d_cmp = np.maximum(np.where(mm >= -1, NSA_BLOCK * mm + q - (NSA_BLOCK - 1), 0), 0)
    assert NSA_BLOCK * (2 * NEAR_PAIRS) - (NSA_BLOCK - 1) >= FAR_DIST
    cw = np.arange(NSA_WINDOW + qb)[None, :]
    d_win = q + NSA_WINDOW - cw
    d_win = np.where((d_win >= 0) & (d_win < NSA_WINDOW), d_win, -1)
    n_win = (NSA_WINDOW + qb) // PAIR
    tiles = d_sel + [d_cmp] + [d_win[:, k * PAIR:(k + 1) * PAIR] for k in range(n_win)]
    dist = np.broadcast_to(np.stack(tiles)[None], (NSA_KV_GROUPS, len(tiles), rows, PAIR))
    tab = _bias_table(rel_bias, jnp.asarray(dist, i32), slab=qb)
    ns = NEAR_PAIRS + 1
    tabs = tab[:, :ns]
    tabc = tab[:, ns]
    tabw = jnp.concatenate([tab[:, ns + 1 + k] for k in range(n_win)], axis=-1)
    c31 = jnp.repeat(rel_bias.astype(f32)[REL_BUCKETS - 1].reshape(NSA_KV_GROUPS, hpg), qb, axis=1)
    return tabs, tabw, tabc, c31.reshape(NSA_KV_GROUPS, rows, 1)


def _nsa_prompt(q_b, misc, kv4, kvwin, rel_bias):
    bsz, t, _ = q_b.shape
    dh, qb = NSA_DH, NSA_QBLOCK
    nblk = t // NSA_BLOCK
    assert nblk == PAIR and t % NSA_KEY_TILE == 0
    kcvc = _block_means(kv4, tb=512)
    onehot = (jnp.arange(t, dtype=i32)[:, None] // NSA_BLOCK == jnp.arange(nblk, dtype=i32)[None, :]).astype(bf16)
    kaug = jnp.concatenate([kv4[:, :, 4 * dh:6 * dh].astype(bf16),
                            jnp.broadcast_to(onehot[None], (bsz, t, nblk))], axis=-1)
    ones = jnp.ones((bsz, t, dh), bf16)
    vgt = jnp.stack([jnp.swapaxes(jnp.concatenate([kv4[:, :, (6 + g) * dh:(7 + g) * dh].astype(bf16), ones],
                                                   axis=-1), 1, 2)
                     for g in range(NSA_KV_GROUPS)], axis=1)
    pad = ((0, 0), (NSA_WINDOW, 0), (0, 0))
    kw = jnp.pad(kvwin[:, :, :2 * dh].astype(bf16), pad)
    vw = jnp.pad(kvwin[:, :, 2 * dh:].astype(bf16), pad)
    tabs, tabw, tabc, c31 = _nsa_tables(rel_bias)
    tabs, tabw = jnp.swapaxes(tabs, 2, 3) * LOG2E, tabw * LOG2E
    rows = NSA_HPG * qb
    gw = NSA_HPG * dh
    span = NSA_WINDOW + qb
    return pl.pallas_call(
        _nsa_prompt_kernel,
        grid=(bsz, NSA_KV_GROUPS, t // qb),
        in_specs=[pl.BlockSpec((1, qb, gw), lambda b, g, i: (b, i, g)),
                  pl.BlockSpec((1, qb, MISC_W), lambda b, g, i: (b, i, 0)),
                  pl.BlockSpec((1, nblk, 4 * dh), lambda b, g, i: (b, 0, 0)),
                  pl.BlockSpec((1, t, 2 * dh + nblk), lambda b, g, i: (b, 0, 0)),
                  pl.BlockSpec((1, 1, 2 * dh, t), lambda b, g, i: (b, g, 0, 0)),
                  pl.BlockSpec((1, t + NSA_WINDOW, 2 * dh), lambda b, g, i: (b, 0, 0)),
                  pl.BlockSpec((1, t + NSA_WINDOW, 2 * dh), lambda b, g, i: (b, 0, 0)),
                  pl.BlockSpec((1, NEAR_PAIRS + 1, PAIR, rows), lambda b, g, i: (g, 0, 0, 0)),
                  pl.BlockSpec((1, rows, span), lambda b, g, i: (g, 0, 0)),
                  pl.BlockSpec((1, rows, PAIR), lambda b, g, i: (g, 0, 0)),
                  pl.BlockSpec((1, rows, 1), lambda b, g, i: (g, 0, 0))],
        out_specs=pl.BlockSpec((1, qb, gw), lambda b, g, i: (b, i, g)),
        out_shape=jax.ShapeDtypeStruct((bsz, t, NSA_HEADS * dh), f32),
        scratch_shapes=[pltpu.VMEM((2 * dh + nblk, rows), bf16), pltpu.VMEM((NSA_KEY_TILE, rows), f32)],
        compiler_params=_cparams(("parallel", "parallel", "arbitrary")),
        name="nsa_prompt",
    )(q_b, misc, kcvc, kaug, vgt, kw, vw, tabs, tabw, tabc, c31)


MEANS_PAGES = 16


def _nsa_s_means_kernel(pt_ref, *refs):
    del pt_ref
    x_refs, o_ref = refs[:-1], refs[-1]
    p = pl.program_id(1)

    @pl.when(p == 0)
    def _():
        o_ref[...] = jnp.zeros_like(o_ref)

    page = x_refs[0].shape[-1]
    per_page = page // NSA_BLOCK
    nblk = o_ref.shape[-1]
    assert page == nblk
    planes = 2 * NSA_KV_GROUPS
    xs = jnp.concatenate([x_ref[0, j, g] for x_ref in x_refs for j in range(2) for g in range(NSA_KV_GROUPS)],
                         axis=0)
    tok = lax.broadcasted_iota(i32, (page, nblk), 0)
    col = lax.broadcasted_iota(i32, (page, nblk), 1)
    pool = (col == (tok >> int(math.log2(NSA_BLOCK)))).astype(bf16)
    hi, lo = _split2(xs)
    sums = jnp.dot(hi, pool, preferred_element_type=f32) + jnp.dot(lo, pool, preferred_element_type=f32)
    rows = planes * NSA_DH
    acc = None
    for k in range(len(x_refs)):
        shift = (p * len(x_refs) + k) * per_page
        part = pltpu.roll(sums[k * rows:(k + 1) * rows], shift, 1)
        acc = part if acc is None else acc + part
    acc = acc * (1.0 / NSA_BLOCK)
    for pl_i in range(planes):
        o_ref[0, pl_i] = o_ref[0, pl_i] + acc[pl_i * NSA_DH:(pl_i + 1) * NSA_DH]


def _nsa_s_means(cache_t, page_table):
    db, n_pages = page_table.shape
    page = cache_t.shape[-1]
    nblk = n_pages * page // NSA_BLOCK
    kp = MEANS_PAGES
    assert n_pages % kp == 0

    def page_spec(k):
        return pl.BlockSpec((1, 2, NSA_KV_GROUPS, NSA_DH, page),
                            lambda b, p, pt: (pt[b * n_pages + p * kp + k], 0, 0, 0, 0))

    return pl.pallas_call(
        _nsa_s_means_kernel,
        grid_spec=pltpu.PrefetchScalarGridSpec(
            num_scalar_prefetch=1,
            grid=(db, n_pages // kp),
            in_specs=[page_spec(k) for k in range(kp)],
            out_specs=pl.BlockSpec((1, 2 * NSA_KV_GROUPS, NSA_DH, nblk), lambda b, p, pt: (b, 0, 0, 0))),
        out_shape=jax.ShapeDtypeStruct((db, 2 * NSA_KV_GROUPS, NSA_DH, nblk), f32),
        compiler_params=_cparams(("parallel", "arbitrary")),
        name="nsa_sample_means",
    )(page_table.reshape(-1), *([cache_t] * kp))


def _nsa_s_scores_kernel(q_ref, kcvc_ref, win_ref, kvn_ref, tc_ref, tw_ref, b0_ref, oc_ref, ow_ref, sel_ref):
    dh, hpg, ng = NSA_DH, NSA_HPG, NSA_KV_GROUPS
    sb = q_ref.shape[0]
    nblk = kcvc_ref.shape[-1]
    row = lax.broadcasted_iota(i32, (NSA_HEADS, 1), 0)
    in_g = [(row >= g * hpg) & (row < (g + 1) * hpg) for g in range(ng)]
    pairs = [(b, g) for b in range(sb) for g in range(ng)]
    q = [q_ref[b] * (dh ** -0.5) for b in range(sb)]

    s_c = [_mm3(q[b], kcvc_ref[b, g]) + tc_ref[...] for b, g in pairs]
    p_c = []
    for s in s_c:
        p = jnp.exp(s - jnp.max(s, axis=-1, keepdims=True))
        p_c.append(p / jnp.maximum(jnp.sum(p, axis=-1, keepdims=True), 1e-30))
    o_c = [_mm_nt(p_c[k], kcvc_ref[b, ng + g]) for k, (b, g) in enumerate(pairs)]
    imp = jnp.concatenate([jnp.sum(jnp.where(in_g[g], p_c[k], 0.0), axis=0, keepdims=True)
                           for k, (b, g) in enumerate(pairs)], axis=0)
    for b in range(sb):
        oc_ref[b] = jnp.where(in_g[0], o_c[b * ng], o_c[b * ng + 1])

    s_w = [_mm3(q[b], win_ref[b, 0, g]) + tw_ref[...] for b, g in pairs]
    o_w = []
    for k, (b, g) in enumerate(pairs):
        kvn = kvn_ref[b]
        s_n = jnp.sum(q[b] * kvn[:, g * dh:(g + 1) * dh], axis=-1, keepdims=True) + b0_ref[...]
        m_w = jnp.maximum(jnp.max(s_w[k], axis=-1, keepdims=True), s_n)
        p_w = jnp.exp(s_w[k] - m_w)
        p_n = jnp.exp(s_n - m_w)
        l_w = jnp.sum(p_w, axis=-1, keepdims=True) + p_n
        v_n = kvn[:, (ng + g) * dh:(ng + g + 1) * dh]
        o_w.append((_mm_nt(p_w, win_ref[b, 1, g]) + p_n * v_n) / jnp.maximum(l_w, 1e-30))
    for b in range(sb):
        ow_ref[b] = jnp.where(in_g[0], o_w[b * ng], o_w[b * ng + 1])

    lane = lax.broadcasted_iota(i32, imp.shape, 1)
    n_sel = NSA_TOPN - 1
    forced_blocks = [0] + [nblk - k for k in range(1, NSA_LOCAL)]
    imp = jnp.where((lane == 0) | (lane > nblk - NSA_LOCAL), -jnp.inf, imp)
    picks = jnp.zeros(imp.shape, i32)
    for it, fb in enumerate(forced_blocks):
        picks = jnp.where(lane == it, fb, picks)
    for it in range(len(forced_blocks), n_sel):
        mx = jnp.max(imp, axis=-1, keepdims=True)
        first = jnp.min(jnp.where(imp == mx, lane, nblk), axis=-1, keepdims=True)
        picks = jnp.where(lane == it, first, picks)
        imp = jnp.where(lane == first, -jnp.inf, imp)
    for b in range(sb):
        sel_ref[b] = picks[b * ng:(b + 1) * ng]


def _nsa_s_sel_kernel(sel_ref, pt_ref, q_ref, rb_ref, kvn_ref, oc_ref, ow_ref, gt_ref, *refs, n_sel, past):
    del pt_ref
    k_refs, v_refs, o_ref = refs[:n_sel], refs[n_sel:2 * n_sel], refs[2 * n_sel]
    b = pl.program_id(0)
    g = pl.program_id(1)
    dh, hpg = NSA_DH, NSA_HPG
    is_g0 = g == 0
    page = k_refs[0].shape[-1]
    per_page = page // NSA_BLOCK
    q = q_ref[0] * (dh ** -0.5)
    kvn = kvn_ref[0]
    lane = lax.broadcasted_iota(i32, (NSA_HEADS, page), 1)
    s_parts, d_parts = [], []
    for j in range(n_sel):
        nb = sel_ref[(b * NSA_KV_GROUPS + g) * n_sel + j]
        in_blk = (lane >> int(math.log2(NSA_BLOCK))) == (nb % per_page)
        d_parts.append(jnp.where(in_blk, past - ((nb // per_page) * page + lane), -1))
        s_parts.append(_mm3(q, k_refs[j][0, 0, 0]))
    d = jnp.concatenate(d_parts, axis=1)
    bias = jnp.broadcast_to(rb_ref[0], d.shape)
    for k in range(1, REL_BUCKETS):
        bias = jnp.where(d >= _BUCKET_THR[k - 1], rb_ref[k], bias)
    s = jnp.concatenate(s_parts, axis=1) + jnp.where(d < 0, NEG, bias)
    k_n = jnp.where(is_g0, kvn[:, 4 * dh:5 * dh], kvn[:, 5 * dh:6 * dh])
    v_n = jnp.where(is_g0, kvn[:, 6 * dh:7 * dh], kvn[:, 7 * dh:8 * dh])
    s_n = jnp.sum(q * k_n, axis=-1, keepdims=True) + rb_ref[0]
    m = jnp.maximum(jnp.max(s, axis=-1, keepdims=True), s_n)
    p = jnp.exp(s - m)
    p_n = jnp.exp(s_n - m)
    l = jnp.sum(p, axis=-1, keepdims=True) + p_n
    acc = p_n * v_n
    for j in range(n_sel):
        acc = acc + _mm_nt(p[:, j * page:(j + 1) * page], v_refs[j][0, 0, 0])
    o_s = acc / jnp.maximum(l, 1e-30)
    gt = gt_ref[0]
    o = gt[:, 0:1] * oc_ref[0] + gt[:, 1:2] * o_s + gt[:, 2:3] * ow_ref[0]
    row = lax.broadcasted_iota(i32, o.shape, 0)
    in_g = (row >= g * hpg) & (row < (g + 1) * hpg)

    @pl.when(is_g0)
    def _():
        o_ref[0] = o

    @pl.when(jnp.logical_not(is_g0))
    def _():
        o_ref[0] = jnp.where(in_g, o, o_ref[0])


def _nsa_sample(q_b, gates, kv4_new, kvwin_new, cache_kv, cache_win, page_table, rel_bias):
    db = q_b.shape[0]
    dh = NSA_DH
    n_pool, page = cache_kv.shape[:2]
    n_pages = page_table.shape[1]
    past = n_pages * page
    nblk = past // NSA_BLOCK
    wb = cache_win.shape[1]
    assert nblk == PAIR and page % NSA_BLOCK == 0 and wb == NSA_WINDOW and nblk > NSA_LOCAL
    cache_t = jnp.transpose(cache_kv, (0, 2, 3, 4, 1)).astype(f32)
    win_t = jnp.transpose(cache_win, (0, 2, 3, 4, 1)).astype(f32)
    kcvc = _nsa_s_means(cache_t, page_table)

    n = np.arange(nblk)
    d_cmp = past - (n * NSA_BLOCK + NSA_BLOCK - 1)
    jw = np.arange(wb)
    d_win = np.where(jw >= 1, wb - jw, -1)
    dist = np.broadcast_to(np.concatenate([d_cmp, d_win])[None, :], (NSA_HEADS, nblk + wb))
    tab = _bias_table(rel_bias, jnp.asarray(dist[None, None], i32), slab=1)[0, 0]
    t_cmp, t_win = tab[:, :nblk], tab[:, nblk:]
    rb = rel_bias.astype(f32)
    b0 = rb[0].reshape(NSA_HEADS, 1)

    q3 = q_b.reshape(db, NSA_HEADS, dh)
    sb = max(s for s in (8, 4, 2, 1) if db % s == 0)
    o_c, o_w, sel = pl.pallas_call(
        _nsa_s_scores_kernel,
        grid=(db // sb,),
        in_specs=[pl.BlockSpec((sb, NSA_HEADS, dh), lambda b: (b, 0, 0)),
                  pl.BlockSpec((sb, 2 * NSA_KV_GROUPS, dh, nblk), lambda b: (b, 0, 0, 0)),
                  pl.BlockSpec((sb, 2, NSA_KV_GROUPS, dh, wb), lambda b: (b, 0, 0, 0, 0)),
                  pl.BlockSpec((sb, 1, 4 * dh), lambda b: (b, 0, 0)),
                  pl.BlockSpec((NSA_HEADS, nblk), lambda b: (0, 0)),
                  pl.BlockSpec((NSA_HEADS, wb), lambda b: (0, 0)),
                  pl.BlockSpec((NSA_HEADS, 1), lambda b: (0, 0))],
        out_specs=[pl.BlockSpec((sb, NSA_HEADS, dh), lambda b: (b, 0, 0)),
                   pl.BlockSpec((sb, NSA_HEADS, dh), lambda b: (b, 0, 0)),
                   pl.BlockSpec((sb, NSA_KV_GROUPS, nblk), lambda b: (b, 0, 0))],
        out_shape=[jax.ShapeDtypeStruct((db, NSA_HEADS, dh), f32),
                   jax.ShapeDtypeStruct((db, NSA_HEADS, dh), f32),
                   jax.ShapeDtypeStruct((db, NSA_KV_GROUPS, nblk), i32)],
        compiler_params=_cparams(("parallel",)),
        name="nsa_sample_scores",
    )(q3, kcvc, win_t, kvwin_new.reshape(db, 1, 4 * dh), t_cmp, t_win, b0)

    n_sel = NSA_TOPN - 1
    sel_flat = sel[:, :, :n_sel].reshape(-1)
    per_page = page // NSA_BLOCK

    def page_spec(j, plane):
        def imap(b, g, s, p):
            nb = s[(b * NSA_KV_GROUPS + g) * n_sel + j]
            return (p[b * n_pages + nb // per_page], plane, g, 0, 0)
        return pl.BlockSpec((1, 1, 1, dh, page), imap)

    const = lambda shape: pl.BlockSpec(shape, lambda b, g, s, p: (0,) * len(shape))
    per_b = lambda shape: pl.BlockSpec((1,) + shape, lambda b, g, s, p: (b,) + (0,) * len(shape))
    o = pl.pallas_call(
        functools.partial(_nsa_s_sel_kernel, n_sel=n_sel, past=past),
        grid_spec=pltpu.PrefetchScalarGridSpec(
            num_scalar_prefetch=2,
            grid=(db, NSA_KV_GROUPS),
            in_specs=[per_b((NSA_HEADS, dh)), const((REL_BUCKETS, NSA_HEADS, 1)), per_b((1, 8 * dh)),
                      per_b((NSA_HEADS, dh)), per_b((NSA_HEADS, dh)), per_b((NSA_HEADS, 3))]
                     + [page_spec(j, 2) for j in range(n_sel)] + [page_spec(j, 3) for j in range(n_sel)],
            out_specs=per_b((NSA_HEADS, dh))),
        out_shape=jax.ShapeDtypeStruct((db, NSA_HEADS, dh), f32),
        compiler_params=_cparams(("parallel", "arbitrary")),
        name="nsa_sample_selected",
    )(sel_flat, page_table.reshape(-1), q3, rb.reshape(REL_BUCKETS, NSA_HEADS, 1), kv4_new.reshape(db, 1, 8 * dh),
      o_c, o_w, gates, *([cache_t] * (2 * n_sel)))
    return o.reshape(db, NSA_HEADS * dh)


def _gelu_tanh(x):
    return 0.5 * x * (1.0 + jnp.tanh(math.sqrt(2.0 / math.pi) * (x + 0.044715 * (x * x * x))))


def _lru_gates(xc, wa_ref, wx_ref, ba_ref, bx_ref, lam_ref, prec):
    r_parts, i_parts = [], []
    for n in range(RNN_BLOCKS):
        xb = xc[:, n * RNN_BW:(n + 1) * RNN_BW]
        r_parts.append(_dotp(xb, wa_ref[n], prec))
        i_parts.append(_dotp(xb, wx_ref[n], prec))
    r = jax.nn.sigmoid(jnp.concatenate(r_parts, axis=1) + ba_ref[...])
    i = jax.nn.sigmoid(jnp.concatenate(i_parts, axis=1) + bx_ref[...])
    log_a = -RG_C * r * _softplus(-lam_ref[...])
    a = jnp.exp(log_a)
    t = jnp.tanh(log_a)
    b = jnp.sqrt(jnp.maximum(-2.0 * t / (1.0 - t), 0.0)) * (i * xc)
    return a, b


def _lru_kernel(rec_ref, gate_ref, cw_ref, cb_ref, wa_ref, wx_ref, ba_ref, bx_ref, lam_ref, cinit_ref, h0_ref,
                y_ref, hfin_ref, xc_scr, a_scr, b_scr, hs_scr, h_scr, *, tb):
    j = pl.program_id(1)

    @pl.when(j == 0)
    def _():
        xc_scr[0:8, :] = cinit_ref[0]
        h_scr[...] = h0_ref[0]

    x = rec_ref[0]
    xc_scr[8:8 + tb, :] = x
    xc = xc_scr[5:5 + tb, :] * cw_ref[0:1, :]
    xc = xc + xc_scr[6:6 + tb, :] * cw_ref[1:2, :]
    xc = xc + xc_scr[7:7 + tb, :] * cw_ref[2:3, :]
    xc = xc + x * cw_ref[3:4, :]
    xc = xc + cb_ref[...]
    xc_scr[0:8, :] = xc_scr[tb:tb + 8, :]
    a, b = _lru_gates(xc, wa_ref, wx_ref, ba_ref, bx_ref, lam_ref, 1)
    a_scr[...] = a
    b_scr[...] = b

    def step(t, h):
        h = a_scr[pl.ds(t, 1), :] * h + b_scr[pl.ds(t, 1), :]
        hs_scr[pl.ds(t, 1), :] = h
        return h

    h = lax.fori_loop(0, tb, step, h_scr[...], unroll=8)
    h_scr[...] = h
    y_ref[0] = _gelu_tanh(gate_ref[0]) * hs_scr[...]

    @pl.when(j == pl.num_programs(1) - 1)
    def _():
        hfin_ref[0] = h


def _lru_prompt(rec, gate, conv_w, conv_b, wa, wx, ba, bx, lam, *, tb):
    bsz, t, w = rec.shape
    row = lambda a: a.reshape(1, w).astype(f32)
    cinit = jnp.zeros((bsz, 8, w), f32)
    h0 = jnp.zeros((bsz, 1, w), f32)
    full = lambda shape: pl.BlockSpec(shape, lambda b, j: (0,) * len(shape))
    return pl.pallas_call(
        functools.partial(_lru_kernel, tb=tb),
        grid=(bsz, t // tb),
        in_specs=[pl.BlockSpec((1, tb, w), lambda b, j: (b, j, 0)),
                  pl.BlockSpec((1, tb, w), lambda b, j: (b, j, 0)),
                  full((RNN_CONV, w)), full((1, w)),
                  full((RNN_BLOCKS, RNN_BW, RNN_BW)), full((RNN_BLOCKS, RNN_BW, RNN_BW)),
                  full((1, w)), full((1, w)), full((1, w)),
                  pl.BlockSpec((1, 8, w), lambda b, j: (b, 0, 0)),
                  pl.BlockSpec((1, 1, w), lambda b, j: (b, 0, 0))],
        out_specs=[pl.BlockSpec((1, tb, w), lambda b, j: (b, j, 0)),
                   pl.BlockSpec((1, 1, w), lambda b, j: (b, 0, 0))],
        out_shape=[jax.ShapeDtypeStruct((bsz, t, w), f32), jax.ShapeDtypeStruct((bsz, 1, w), f32)],
        scratch_shapes=[pltpu.VMEM((tb + 8, w), f32), pltpu.VMEM((tb, w), f32), pltpu.VMEM((tb, w), f32),
                        pltpu.VMEM((tb, w), f32), pltpu.VMEM((1, w), f32)],
        compiler_params=_cparams(("parallel", "arbitrary")),
        name="rglru",
    )(rec, gate, conv_w.astype(f32), row(conv_b), wa.astype(f32), wx.astype(f32), row(ba), row(bx), row(lam),
      cinit, h0)


def _lru_step_kernel(rec_ref, gate_ref, b0_ref, b1_ref, b2_ref, cw_ref, cb_ref, wa_ref, wx_ref, ba_ref, bx_ref,
                     lam_ref, h0_ref, y_ref, h_ref):
    xc = b0_ref[...] * cw_ref[0:1, :]
    xc = xc + b1_ref[...] * cw_ref[1:2, :]
    xc = xc + b2_ref[...] * cw_ref[2:3, :]
    xc = xc + rec_ref[...] * cw_ref[3:4, :]
    xc = xc + cb_ref[...]
    a, b = _lru_gates(xc, wa_ref, wx_ref, ba_ref, bx_ref, lam_ref, 3)
    h = a * h0_ref[...] + b
    h_ref[...] = h
    y_ref[...] = _gelu_tanh(gate_ref[...]) * h


def _lru_sample(rec, gate, conv_buf, h0, conv_w, conv_b, wa, wx, ba, bx, lam):
    db, w = rec.shape
    row = lambda a: a.reshape(1, w).astype(f32)
    buf = conv_buf.astype(f32)
    return pl.pallas_call(
        _lru_step_kernel,
        out_shape=[jax.ShapeDtypeStruct((db, w), f32), jax.ShapeDtypeStruct((db, w), f32)],
        compiler_params=pltpu.CompilerParams(vmem_limit_bytes=VMEM_LIMIT),
        name="rglru_step",
    )(rec, gate, buf[:, 0], buf[:, 1], buf[:, 2], conv_w.astype(f32), row(conv_b), wa.astype(f32), wx.astype(f32),
      row(ba), row(bx), row(lam), h0.astype(f32))


def _top2_kernel(l_ref, e_ref, g_ref, cnt_ref, cnt_scr):
    @pl.when(pl.program_id(0) == 0)
    def _():
        cnt_scr[...] = jnp.zeros_like(cnt_scr)

    lg = l_ref[...]
    tm, w = lg.shape
    lane = lax.broadcasted_iota(i32, lg.shape, 1)
    lg = jnp.where(lane < N_EXPERTS, lg, -jnp.inf)
    m1 = jnp.max(lg, axis=-1, keepdims=True)
    i1 = jnp.min(jnp.where(lg == m1, lane, w), axis=-1, keepdims=True)
    lg2 = jnp.where(lane == i1, -jnp.inf, lg)
    m2 = jnp.max(lg2, axis=-1, keepdims=True)
    i2 = jnp.min(jnp.where(lg2 == m2, lane, w), axis=-1, keepdims=True)
    e2 = jnp.exp(m2 - m1)
    den = 1.0 + e2
    g_ref[...] = jnp.where(lane == 0, 1.0 / den, jnp.where(lane == 1, e2 / den, 0.0))
    hit1, hit2 = lane == i1, lane == i2
    routed = (hit1 | hit2).astype(bf16)
    r_i = lax.broadcasted_iota(i32, (tm, tm), 0)
    c_i = lax.broadcasted_iota(i32, (tm, tm), 1)
    before = jnp.dot((c_i < r_i).astype(bf16), routed, preferred_element_type=f32) + cnt_scr[...]
    rank1 = jnp.sum(jnp.where(hit1, before, 0.0), axis=-1, keepdims=True).astype(i32)
    rank2 = jnp.sum(jnp.where(hit2, before, 0.0), axis=-1, keepdims=True).astype(i32)
    e_ref[...] = jnp.where(lane == 0, i1, jnp.where(lane == 1, i2, jnp.where(lane == 2, rank1,
                                                                               jnp.where(lane == 3, rank2, 0))))
    cnt_scr[...] += jnp.sum(routed.astype(f32), axis=0, keepdims=True)
    cnt_ref[...] = cnt_scr[...].astype(i32)


def _top2(logits, *, tm):
    n, w = logits.shape
    return pl.pallas_call(
        _top2_kernel,
        grid=(n // tm,),
        in_specs=[pl.BlockSpec((tm, w), lambda i: (i, 0))],
        out_specs=[pl.BlockSpec((tm, w), lambda i: (i, 0)), pl.BlockSpec((tm, w), lambda i: (i, 0)),
                   pl.BlockSpec((1, w), lambda i: (0, 0))],
        out_shape=[jax.ShapeDtypeStruct((n, w), i32), jax.ShapeDtypeStruct((n, w), f32),
                   jax.ShapeDtypeStruct((1, w), i32)],
        scratch_shapes=[pltpu.VMEM((1, w), f32)],
        compiler_params=_cparams(("arbitrary",)),
        name="moe_top2",
    )(logits)


def _moe_kernel(be_ref, nu_ref, nv_ref, xs_ref, w1_ref, w3_ref, w2_ref, o_ref, acc_scr):
    del be_ref
    i = pl.program_id(0)
    f = pl.program_id(1)
    last = pl.num_programs(1) - 1
    used = i < nu_ref[0]
    half = xs_ref.shape[0] // 2
    top_only = nv_ref[i] <= half

    @pl.when(used & (f == 0))
    def _():
        acc_scr[...] = jnp.zeros_like(acc_scr)

    def update(rows):
        x = xs_ref[rows, :]
        act = _silu(_mm(x, w1_ref[0])) * _mm(x, w3_ref[0])
        acc_scr[rows, :] += _mm(act, w2_ref[0])

    @pl.when(used & jnp.logical_not(top_only))
    def _():
        update(slice(None))

    @pl.when(used & top_only)
    def _():
        update(slice(0, half))

    @pl.when(used & (f == last))
    def _():
        o_ref[...] = acc_scr[...]

    @pl.when(jnp.logical_not(used) & (f == last))
    def _():
        o_ref[...] = jnp.zeros_like(o_ref)


def _moe_experts(xs, blk_e, n_used, n_valid, w1, w3, w2, *, tm, tf):
    rows, d = xs.shape
    fdim = w1.shape[2]
    nf = fdim // tf

    def wcol(i, f, be, nu, nv):
        return (be[i], 0, jnp.where(i < nu[0], f, nf - 1))

    def wrow(i, f, be, nu, nv):
        return (be[i], jnp.where(i < nu[0], f, nf - 1), 0)

    return pl.pallas_call(
        _moe_kernel,
        grid_spec=pltpu.PrefetchScalarGridSpec(
            num_scalar_prefetch=3,
            grid=(rows // tm, nf),
            in_specs=[pl.BlockSpec((tm, d), lambda i, f, be, nu, nv: (i, 0)),
                      pl.BlockSpec((1, d, tf), wcol),
                      pl.BlockSpec((1, d, tf), wcol),
                      pl.BlockSpec((1, tf, d), wrow)],
            out_specs=pl.BlockSpec((tm, d), lambda i, f, be, nu, nv: (i, 0)),
            scratch_shapes=[pltpu.VMEM((tm, d), f32)]),
        out_shape=jax.ShapeDtypeStruct((rows, d), f32),
        compiler_params=_cparams(("arbitrary", "arbitrary")),
        name="moe_experts",
    )(blk_e, n_used, n_valid, xs, w1, w3, w2)


def _combine_kernel(x_ref, gate_ref, y0_ref, y1_ref, g_ref, nw_ref, o_ref):
    gw = g_ref[...]
    y = y0_ref[...] * gw[:, 0:1] + y1_ref[...] * gw[:, 1:2]
    x = x_ref[0] + gate_ref[0] * y
    o_ref[0] = x * lax.rsqrt(jnp.mean(x * x, axis=-1, keepdims=True) + EPS) * nw_ref[...]


def _moe_combine(x, gate, y0, y1, gw, norm_w, *, tm, row0):
    bsz, t, d = x.shape
    assert row0 % tm == 0
    tok = pl.BlockSpec((1, tm, d), lambda b, i: (b, i, 0))
    flat = lambda w: pl.BlockSpec((tm, w), lambda b, i: (row0 // tm + b * (t // tm) + i, 0))
    return pl.pallas_call(
        _combine_kernel,
        grid=(bsz, t // tm),
        in_specs=[tok, _mod_spec(gate, tm), flat(d), flat(d), flat(gw.shape[-1]),
                  pl.BlockSpec((1, d), lambda b, i: (0, 0))],
        out_specs=tok,
        out_shape=jax.ShapeDtypeStruct((bsz, t, d), f32),
        compiler_params=_cparams(("parallel", "parallel")),
        name="moe_combine",
    )(x, gate, y0, y1, gw, norm_w.reshape(1, d).astype(f32))


MOE_TM = 1024
MOE_TF = 512


def _moe_dispatch(e_idx, rank, counts, n_tok):
    tm = MOE_TM
    n_assign = n_tok * TOP_K
    padded = (counts + tm - 1) // tm * tm
    pend = jnp.cumsum(padded)
    pstart = pend - padded
    experts = jnp.arange(N_EXPERTS, dtype=i32)
    dest = jnp.sum(jnp.where(e_idx[:, :, None] == experts, pstart, 0), axis=-1) + rank
    n_blocks = -(-n_assign // tm) + N_EXPERTS
    rows = n_blocks * tm
    tok = jnp.broadcast_to(jnp.arange(n_tok, dtype=i32)[:, None], (n_tok, TOP_K))
    row_tok = (jnp.arange(rows, dtype=i32) % n_tok).at[dest.reshape(-1)].set(
        tok.reshape(-1), unique_indices=True, mode='promise_in_bounds')
    n_used = (pend[-1] // tm).astype(i32)
    blk = jnp.minimum(jnp.arange(n_blocks, dtype=i32), n_used - 1) * tm
    blk_e = jnp.minimum(jnp.sum((blk[:, None] >= pend[None, :]).astype(i32), axis=1), N_EXPERTS - 1)
    is_e = blk_e[:, None] == experts[None, :]
    first_row = jnp.sum(jnp.where(is_e, pstart, 0), axis=1)
    n_valid = jnp.clip(jnp.sum(jnp.where(is_e, counts, 0), axis=1) - (blk - first_row), 0, tm)
    return dest, row_tok, blk_e, n_used.reshape(1), n_valid.astype(i32)


def _w_in0_layout(w_in0):
    c_qkv = GDN_CONV_CH
    c_z = c_qkv + GDN_HEADS * GDN_DV
    c_ab = c_z + 2 * GDN_HEADS
    c_q = c_ab + NSA_HEADS * NSA_DH
    c_kv = c_q + 6 * NSA_KV_GROUPS * NSA_DH
    c_g = c_kv + 3 * NSA_HEADS
    assert c_g == w_in0.shape[1]
    n_misc = 2 * GDN_HEADS + 3 * NSA_HEADS
    w = jnp.concatenate([w_in0[:, :c_z], w_in0[:, c_ab:c_kv], w_in0[:, c_z:c_ab], w_in0[:, c_kv:c_g],
                         jnp.zeros((w_in0.shape[0], MISC_W - n_misc), w_in0.dtype)], axis=1)
    widths = (GDN_CONV_CH, GDN_HEADS * GDN_DV, NSA_HEADS * NSA_DH, 4 * NSA_KV_GROUPS * NSA_DH,
              2 * NSA_KV_GROUPS * NSA_DH, MISC_W)
    splits, s = [], 0
    for wd in widths:
        splits.append((s, s + wd))
        s += wd
    return w, tuple(splits)


def kernel(x_prompt, x_sample, c_prompt, c_sample, cache_nsa_kv, cache_nsa_win, state_gdn, state_gdn_conv, state_lru, state_lru_conv, page_table, rel_bias, w_ada, b_ada, norm_mix, norm_ffn, norm_final, w_in0, gdn_conv_w, gdn_a_log, gdn_dt_bias, gdn_norm_w, w_out0, ffn_w_gate, ffn_w_up, ffn_w_down, w_in1, lru_conv_w, lru_conv_b, lru_wa, lru_ba, lru_wx, lru_bx, lru_lambda, w_out1, moe_router, moe_w1, moe_w3, moe_w2):
    bsz, seq, d = x_prompt.shape
    db = x_sample.shape[0]
    assert x_sample.shape[1] == 1
    dh = NSA_DH

    n_c = bsz + db
    n_c_pad = -(-n_c // 8) * 8
    c_all = jnp.concatenate([c_prompt, c_sample, jnp.zeros((n_c_pad - n_c, d), f32)], axis=0)
    mods = _adaln(c_all, w_ada, b_ada).reshape(2, n_c_pad, N_MOD, d)
    mod_p = [[mods[l, :bsz, k].reshape(bsz, 1, d) for k in range(N_MOD)] for l in range(2)]
    mod_s = [[mods[l, bsz:n_c, k].reshape(1, db, d) for k in range(N_MOD)] for l in range(2)]

    w0, splits0 = _w_in0_layout(w_in0)
    splits1 = ((0, RNN_WIDTH), (RNN_WIDTH, 2 * RNN_WIDTH))
    router = jnp.concatenate([moe_router, jnp.zeros((d, MISC_W - N_EXPERTS), f32)], axis=1)
    bf = lambda w: w.astype(bf16)

    tm = 512
    xp = x_prompt
    qkv, z, q_b, kv4, kvwin, misc = _mod_matmul(xp, norm_mix[0], mod_p[0][0], mod_p[0][1], bf(w0), splits0,
                                                tm=tm, prec=1)
    o_a, p_gdn = _gdn(qkv, z, misc, gdn_conv_w, jnp.zeros((bsz, GDN_CONV - 1, GDN_CONV_CH), f32), gdn_a_log,
                      gdn_dt_bias, gdn_norm_w, jnp.zeros((bsz, GDN_HEADS, GDN_DK, GDN_DV), f32), tb=256, n_valid=seq)
    p_gdn_conv = qkv[:, seq - (GDN_CONV - 1):]
    o_b = _nsa_prompt(q_b, misc, kv4, kvwin, rel_bias)
    p_nsa_kv = kv4.reshape(bsz, seq, 4, NSA_KV_GROUPS, dh)
    keep = min(NSA_WINDOW, seq)
    p_nsa_win = kvwin[:, seq - keep:].reshape(bsz, keep, 2, NSA_KV_GROUPS, dh)
    xp = _proj_residual([o_a, o_b], bf(w_out0), xp, mod_p[0][2], tm=tm, prec=1)
    xp = _ffn(xp, norm_ffn[0], mod_p[0][3], mod_p[0][4], mod_p[0][5], bf(ffn_w_gate), bf(ffn_w_up), bf(ffn_w_down),
              tm=tm, tf=FFN_DIM // 2, prec=1)
    gate_br, rec_br = _mod_matmul(xp, norm_mix[1], mod_p[1][0], mod_p[1][1], bf(w_in1), splits1, tm=tm, prec=1)
    y_in, p_lru = _lru_prompt(rec_br, gate_br, lru_conv_w, lru_conv_b, lru_wa, lru_wx, lru_ba, lru_bx, lru_lambda,
                              tb=256)
    p_lru_conv = rec_br[:, seq - (RNN_CONV - 1):]
    xp = _proj_residual([y_in], bf(w_out1), xp, mod_p[1][2], tm=tm, prec=1)
    logit_p, h_p = _mod_matmul(xp, norm_ffn[1], mod_p[1][3], mod_p[1][4], router, ((0, MISC_W),), tm=tm, prec=3,
                               emit_h=f32)

    xs = x_sample.reshape(1, db, d)
    qkv_s, z_s, q_s, kv4_s, kvwin_s, misc_s = _mod_matmul(xs, norm_mix[0], mod_s[0][0], mod_s[0][1], w0, splits0,
                                                          tm=db, prec=3)
    c = GDN_CHUNK
    tpad = lambda a: jnp.pad(a.reshape(db, 1, a.shape[-1]), ((0, 0), (0, c - 1), (0, 0)))
    o_a_s, s_gdn = _gdn(tpad(qkv_s), tpad(z_s), tpad(misc_s), gdn_conv_w, state_gdn_conv, gdn_a_log, gdn_dt_bias,
                        gdn_norm_w, state_gdn, tb=c, n_valid=1)
    o_a_s = o_a_s[:, 0].reshape(1, db, GDN_HEADS * GDN_DV)
    s_gdn_conv = jnp.concatenate([state_gdn_conv[:, 1:], qkv_s.reshape(db, 1, GDN_CONV_CH)], axis=1)
    gates_s = jax.nn.sigmoid(misc_s[0, :, 2 * GDN_HEADS:2 * GDN_HEADS + 3 * NSA_HEADS]).reshape(db, NSA_HEADS, 3)
    o_b_s = _nsa_sample(q_s[0], gates_s, kv4_s[0], kvwin_s[0], cache_nsa_kv, cache_nsa_win, page_table, rel_bias)
    s_nsa_kv = kv4_s.reshape(db, 1, 4, NSA_KV_GROUPS, dh)
    s_nsa_win = jnp.concatenate([cache_nsa_win[:, 1:],
                                 kvwin_s.reshape(db, 1, 2, NSA_KV_GROUPS, dh).astype(cache_nsa_win.dtype)], axis=1)
    xs = _proj_residual([o_a_s, o_b_s.reshape(1, db, NSA_HEADS * dh)], w_out0, xs, mod_s[0][2], tm=db, prec=3)
    xs = _ffn(xs, norm_ffn[0], mod_s[0][3], mod_s[0][4], mod_s[0][5], ffn_w_gate, ffn_w_up, ffn_w_down,
              tm=db, tf=256, prec=3)
    gate_s, rec_s = _mod_matmul(xs, norm_mix[1], mod_s[1][0], mod_s[1][1], w_in1, splits1, tm=db, prec=3)
    y_in_s, s_lru = _lru_sample(rec_s[0], gate_s[0], state_lru_conv, state_lru, lru_conv_w, lru_conv_b, lru_wa, lru_wx,
                                lru_ba, lru_bx, lru_lambda)
    s_lru_conv = jnp.concatenate([state_lru_conv[:, 1:], rec_s.reshape(db, 1, RNN_WIDTH)], axis=1)
    xs = _proj_residual([y_in_s.reshape(1, db, RNN_WIDTH)], w_out1, xs, mod_s[1][2], tm=db, prec=3)
    logit_s, h_s = _mod_matmul(xs, norm_ffn[1], mod_s[1][3], mod_s[1][4], router, ((0, MISC_W),), tm=db, prec=3,
                               emit_h=f32)

    n_p = bsz * seq
    n_tok = n_p + db
    logits = jnp.concatenate([logit_p.reshape(n_p, MISC_W), logit_s.reshape(db, MISC_W)], axis=0)
    h_all = jnp.concatenate([h_p.reshape(n_p, d), h_s.reshape(db, d)], axis=0)
    route, gw, cnt = _top2(logits, tm=max(t for t in range(8, 1025, 8) if n_tok % t == 0))
    dest, row_tok, blk_e, n_used, n_valid = _moe_dispatch(route[:, :TOP_K], route[:, TOP_K:2 * TOP_K],
                                                          cnt[0, :N_EXPERTS], n_tok)
    yb = _moe_experts(h_all[row_tok], blk_e, n_used, n_valid, moe_w1, moe_w3, moe_w2, tm=MOE_TM, tf=MOE_TF)
    y0, y1 = yb[dest[:, 0]], yb[dest[:, 1]]
    y_prompt = _moe_combine(xp, mod_p[1][5], y0, y1, gw, norm_final, tm=tm, row0=0)
    y_sample = _moe_combine(xs, mod_s[1][5], y0, y1, gw, norm_final, tm=db, row0=n_p)

    return (y_prompt, y_sample.reshape(db, 1, d),
            p_nsa_kv, p_nsa_win, p_gdn, p_gdn_conv, p_lru.reshape(bsz, RNN_WIDTH), p_lru_conv,
            s_nsa_kv, s_nsa_win, s_gdn, s_gdn_conv, s_lru, s_lru_conv)
```

```python
import functools
import math

import numpy as np
import jax
import jax.numpy as jnp
from jax import lax
from jax.experimental import pallas as pl
from jax.experimental.pallas import tpu as pltpu

f32 = jnp.float32
bf16 = jnp.bfloat16
i32 = jnp.int32

D_MODEL = 1024
EPS = 1e-6
N_MOD = 6
GDN_HEADS = 4
GDN_DK = 128
GDN_DV = 128
GDN_CONV = 4
GDN_CHUNK = 64
GDN_CONV_CH = GDN_HEADS * (2 * GDN_DK + GDN_DV)
NSA_HEADS = 8
NSA_KV_GROUPS = 2
NSA_HPG = NSA_HEADS // NSA_KV_GROUPS
NSA_DH = 64
NSA_BLOCK = 64
NSA_TOPN = 16
NSA_LOCAL = 2
NSA_WINDOW = 512
NSA_QBLOCK = 128
REL_BUCKETS = 32
REL_MAX_DIST = 2048
RNN_WIDTH = D_MODEL
RNN_BLOCKS = 8
RNN_BW = RNN_WIDTH // RNN_BLOCKS
RNN_CONV = 4
RG_C = 8.0
FFN_DIM = 2816
N_EXPERTS = 8
TOP_K = 2
EXPERT_DIM = 3584

NEG = -1e30
PAIR = 2 * NSA_BLOCK
NEAR_PAIRS = 13
MISC_W = 128
VMEM_LIMIT = 56 * 1024 * 1024


def _cparams(sem, vmem=VMEM_LIMIT):
    return pltpu.CompilerParams(dimension_semantics=sem, vmem_limit_bytes=vmem)


def _mm(a, b):
    return jnp.dot(a.astype(bf16), b.astype(bf16), preferred_element_type=f32)


def _mm_nt(a, b):
    return lax.dot_general(a.astype(bf16), b.astype(bf16), (((1,), (1,)), ((), ())),
                           preferred_element_type=f32)


def _split2(a):
    hi = a.astype(bf16)
    lo = (a - hi.astype(f32)).astype(bf16)
    return hi, lo


def _mm3(a, b):
    ah, al = _split2(a)
    bh, bl = _split2(b)
    return (jnp.dot(ah, bh, preferred_element_type=f32) + jnp.dot(ah, bl, preferred_element_type=f32)
            + jnp.dot(al, bh, preferred_element_type=f32))


def _mm3_nt(a, b):
    ah, al = _split2(a)
    bh, bl = _split2(b)
    dn = (((1,), (1,)), ((), ()))
    return (lax.dot_general(ah, bh, dn, preferred_element_type=f32)
            + lax.dot_general(ah, bl, dn, preferred_element_type=f32)
            + lax.dot_general(al, bh, dn, preferred_element_type=f32))


def _mm_01(m01, a):
    hi = a.astype(bf16)
    r1 = a - hi.astype(f32)
    mid = r1.astype(bf16)
    lo = (r1 - mid.astype(f32)).astype(bf16)
    return (jnp.dot(m01, hi, preferred_element_type=f32) + jnp.dot(m01, mid, preferred_element_type=f32)
            + jnp.dot(m01, lo, preferred_element_type=f32))


def _dotp(a, b, prec):
    return _mm3(a, b) if prec == 3 else _mm(a, b)


def _silu(x):
    return x * jax.nn.sigmoid(x)


def _softplus(x):
    return jnp.maximum(x, 0.0) + jnp.log1p(jnp.exp(-jnp.abs(x)))


def _modulate(x, gain, shift, scale):
    r = lax.rsqrt(jnp.mean(x * x, axis=-1, keepdims=True) + EPS)
    return x * r * gain * (1.0 + scale) + shift


def _ada_kernel(c_ref, w_ref, b_ref, o_ref):
    o_ref[0] = _mm3(_silu(c_ref[...]), w_ref[0]) + b_ref[0]


def _adaln(c_all, w_ada, b_ada):
    rows = c_all.shape[0]
    depth, d, n = w_ada.shape
    tn = 1536
    return pl.pallas_call(
        _ada_kernel,
        grid=(depth, n // tn),
        in_specs=[pl.BlockSpec((rows, d), lambda l, j: (0, 0)),
                  pl.BlockSpec((1, d, tn), lambda l, j: (l, 0, j)),
                  pl.BlockSpec((1, 1, tn), lambda l, j: (l, 0, j))],
        out_specs=pl.BlockSpec((1, rows, tn), lambda l, j: (l, 0, j)),
        out_shape=jax.ShapeDtypeStruct((depth, rows, n), f32),
        compiler_params=_cparams(("arbitrary", "arbitrary")),
        name="adaln",
    )(c_all, w_ada, b_ada.reshape(depth, 1, n))


def _mod_spec(mod, tm):
    r = mod.shape[1]
    if r == 1:
        return pl.BlockSpec((1, 1, mod.shape[2]), lambda b, i: (b, 0, 0))
    return pl.BlockSpec((1, tm, mod.shape[2]), lambda b, i: (b, i, 0))


def _modmm_kernel(x_ref, gain_ref, shift_ref, scale_ref, w_ref, *o_refs, splits, prec, emit_h):
    h = _modulate(x_ref[0], gain_ref[...], shift_ref[0], scale_ref[0])
    if emit_h:
        o_refs[-1][0] = h.astype(o_refs[-1].dtype)
    hh = _split2(h) if prec == 3 else h.astype(bf16)
    for o_ref, (a, b) in zip(o_refs, splits):
        w = w_ref[:, a:b]
        if prec == 3:
            wh, wl = _split2(w)
            acc = (jnp.dot(hh[0], wh, preferred_element_type=f32) + jnp.dot(hh[0], wl, preferred_element_type=f32)
                   + jnp.dot(hh[1], wh, preferred_element_type=f32))
        else:
            acc = jnp.dot(hh, w, preferred_element_type=f32)
        o_ref[0] = acc


def _mod_matmul(x, gain, shift, scale, w, splits, *, tm, prec, emit_h=None):
    bsz, t, d = x.shape
    out_shape = [jax.ShapeDtypeStruct((bsz, t, b - a), f32) for a, b in splits]
    out_specs = [pl.BlockSpec((1, tm, b - a), lambda bi, i: (bi, i, 0)) for a, b in splits]
    if emit_h is not None:
        out_shape.append(jax.ShapeDtypeStruct((bsz, t, d), emit_h))
        out_specs.append(pl.BlockSpec((1, tm, d), lambda bi, i: (bi, i, 0)))
    return pl.pallas_call(
        functools.partial(_modmm_kernel, splits=tuple(splits), prec=prec, emit_h=emit_h is not None),
        grid=(bsz, t // tm),
        in_specs=[pl.BlockSpec((1, tm, d), lambda bi, i: (bi, i, 0)),
                  pl.BlockSpec((1, d), lambda bi, i: (0, 0)),
                  _mod_spec(shift, tm), _mod_spec(scale, tm),
                  pl.BlockSpec(w.shape, lambda bi, i: (0, 0))],
        out_specs=out_specs,
        out_shape=out_shape,
        compiler_params=_cparams(("parallel", "parallel")),
        name="mod_matmul",
    )(x, gain.reshape(1, d), shift, scale, w)


def _projres_kernel(*refs, n_lhs, ksplits, prec, final_norm):
    lhs = refs[:n_lhs]
    w_ref, x_ref, gate_ref = refs[n_lhs:n_lhs + 3]
    o_ref = refs[-1]
    acc = None
    for l_ref, (a, b) in zip(lhs, ksplits):
        part = _dotp(l_ref[0], w_ref[a:b, :], prec)
        acc = part if acc is None else acc + part
    y = x_ref[0] + gate_ref[0] * acc
    if final_norm:
        nw_ref = refs[n_lhs + 3]
        y = y * lax.rsqrt(jnp.mean(y * y, axis=-1, keepdims=True) + EPS) * nw_ref[...]
    o_ref[0] = y


def _proj_residual(lhs_list, w, x, gate, *, tm, prec, norm_w=None):
    bsz, t, d = x.shape
    ksplits, k0 = [], 0
    for l in lhs_list:
        ksplits.append((k0, k0 + l.shape[-1]))
        k0 += l.shape[-1]
    in_specs = [pl.BlockSpec((1, tm, l.shape[-1]), lambda bi, i: (bi, i, 0)) for l in lhs_list]
    in_specs += [pl.BlockSpec(w.shape, lambda bi, i: (0, 0)),
                 pl.BlockSpec((1, tm, d), lambda bi, i: (bi, i, 0)),
                 _mod_spec(gate, tm)]
    args = list(lhs_list) + [w, x, gate]
    if norm_w is not None:
        in_specs.append(pl.BlockSpec((1, d), lambda bi, i: (0, 0)))
        args.append(norm_w.reshape(1, d))
    return pl.pallas_call(
        functools.partial(_projres_kernel, n_lhs=len(lhs_list), ksplits=tuple(ksplits), prec=prec,
                          final_norm=norm_w is not None),
        grid=(bsz, t // tm),
        in_specs=in_specs,
        out_specs=pl.BlockSpec((1, tm, d), lambda bi, i: (bi, i, 0)),
        out_shape=jax.ShapeDtypeStruct((bsz, t, d), f32),
        compiler_params=_cparams(("parallel", "parallel")),
        name="proj_residual",
    )(*args)


def _ffn_kernel(x_ref, gain_ref, shift_ref, scale_ref, gate_ref, wg_ref, wu_ref, wd_ref, o_ref,
                h_scr, acc_scr, *, prec):
    f = pl.program_id(2)

    @pl.when(f == 0)
    def _():
        h_scr[...] = _modulate(x_ref[0], gain_ref[...], shift_ref[0], scale_ref[0]).astype(h_scr.dtype)
        acc_scr[...] = jnp.zeros_like(acc_scr)

    h = h_scr[...]
    act = _silu(_dotp(h, wg_ref[...], prec)) * _dotp(h, wu_ref[...], prec)
    acc_scr[...] += _dotp(act, wd_ref[...], prec)

    @pl.when(f == pl.num_programs(2) - 1)
    def _():
        o_ref[0] = x_ref[0] + gate_ref[0] * acc_scr[...]


def _ffn(x, gain, shift, scale, gate, wg, wu, wd, *, tm, tf, prec):
    bsz, t, d = x.shape
    fdim = wg.shape[1]
    mod_specs = []
    for mod in (shift, scale, gate):
        if mod.shape[1] == 1:
            mod_specs.append(pl.BlockSpec((1, 1, d), lambda b, i, f: (b, 0, 0)))
        else:
            mod_specs.append(pl.BlockSpec((1, tm, d), lambda b, i, f: (b, i, 0)))
    return pl.pallas_call(
        functools.partial(_ffn_kernel, prec=prec),
        grid=(bsz, t // tm, fdim // tf),
        in_specs=[pl.BlockSpec((1, tm, d), lambda b, i, f: (b, i, 0)),
                  pl.BlockSpec((1, d), lambda b, i, f: (0, 0)),
                  mod_specs[0], mod_specs[1], mod_specs[2],
                  pl.BlockSpec((d, tf), lambda b, i, f: (0, f)),
                  pl.BlockSpec((d, tf), lambda b, i, f: (0, f)),
                  pl.BlockSpec((tf, d), lambda b, i, f: (f, 0))],
        out_specs=pl.BlockSpec((1, tm, d), lambda b, i, f: (b, i, 0)),
        out_shape=jax.ShapeDtypeStruct((bsz, t, d), f32),
        scratch_shapes=[pltpu.VMEM((tm, d), f32 if prec == 3 else bf16), pltpu.VMEM((tm, d), f32)],
        compiler_params=_cparams(("parallel", "parallel", "arbitrary")),
        name="ffn",
    )(x, gain.reshape(1, d), shift, scale, gate, wg, wu, wd)


def _gdn_kernel(qkv_ref, z_ref, misc_ref, cw_ref, cinit_ref, hp_ref, nw_ref, s0_ref,
                o_ref, sfin_ref, xc_scr, s_scr, *, tb, n_valid):
    j = pl.program_id(1)
    c = GDN_CHUNK
    nh = GDN_HEADS

    @pl.when(j == 0)
    def _():
        xc_scr[0:8, :] = cinit_ref[0]
        s_scr[...] = s0_ref[0]

    x = qkv_ref[0]
    xc_scr[8:8 + tb, :] = x
    y = xc_scr[5:5 + tb, :] * cw_ref[0:1, :]
    y = y + xc_scr[6:6 + tb, :] * cw_ref[1:2, :]
    y = y + xc_scr[7:7 + tb, :] * cw_ref[2:3, :]
    y = y + x * cw_ref[3:4, :]
    xc_scr[0:8, :] = xc_scr[tb:tb + 8, :]
    y = _silu(y)

    misc = misc_ref[0]
    row = lax.broadcasted_iota(i32, (tb, MISC_W), 0) + j * tb
    live = row < n_valid
    log_a = jnp.where(live, hp_ref[0:1, :] * _softplus(misc + hp_ref[1:2, :]), 0.0)
    beta = jnp.where(live, jax.nn.sigmoid(misc), 0.0)

    r_i = lax.broadcasted_iota(i32, (tb, tb), 0)
    c_i = lax.broadcasted_iota(i32, (tb, tb), 1)
    sh = int(math.log2(c))
    ltri = (((r_i >> sh) == (c_i >> sh)) & (c_i <= r_i)).astype(bf16)
    g = _mm_01(ltri, log_a)
    g_t = g.T

    ri = lax.broadcasted_iota(i32, (c, c), 0)
    ci = lax.broadcasted_iota(i32, (c, c), 1)
    incl = ci <= ri
    strict = ci < ri
    eye = (ci == ri).astype(f32)
    quad = []
    for lvl in range(int(math.log2(c))):
        quad.append(((ri >> (lvl + 1)) == (ci >> (lvl + 1))) & (((ri >> lvl) & 1) == 1) & (((ci >> lvl) & 1) == 0))

    nchunk = tb // c
    units = [(n, h) for n in range(nchunk) for h in range(nh)]

    def stack(fn):
        return jnp.stack([fn(n * c, h) for n, h in units], axis=0)

    def bmm(eq, a_, b_):
        return jnp.einsum(eq, a_.astype(bf16), b_.astype(bf16), preferred_element_type=f32)

    q = stack(lambda r0, h: y[r0:r0 + c, h * GDN_DK:(h + 1) * GDN_DK])
    k = stack(lambda r0, h: y[r0:r0 + c, (nh + h) * GDN_DK:(nh + h + 1) * GDN_DK])
    v = stack(lambda r0, h: y[r0:r0 + c, 2 * nh * GDN_DK + h * GDN_DV:2 * nh * GDN_DK + (h + 1) * GDN_DV])
    q = q * lax.rsqrt(jnp.sum(q * q, axis=-1, keepdims=True) + EPS) * (GDN_DK ** -0.5)
    k = k * lax.rsqrt(jnp.sum(k * k, axis=-1, keepdims=True) + EPS)
    g_col = stack(lambda r0, h: g[r0:r0 + c, h:h + 1])
    g_row = stack(lambda r0, h: g_t[h:h + 1, r0:r0 + c])
    b_col = stack(lambda r0, h: beta[r0:r0 + c, nh + h:nh + h + 1])
    gam = jnp.where(incl, jnp.exp(jnp.where(incl, g_col - g_row, 0.0)), 0.0)
    kk = bmm('uid,ujd->uij', k, k)
    a = jnp.where(strict, b_col * gam * kk, 0.0)
    p = eye - jnp.where(quad[0], a, 0.0)
    for lvl in range(1, len(quad)):
        m = bmm('uij,ujk->uik', jnp.where(quad[lvl], a, 0.0), p)
        p = p - bmm('uij,ujk->uik', p, m)
    e_g = jnp.exp(g_col)
    sol = bmm('uij,ujd->uid', p, jnp.concatenate([b_col * v, (b_col * e_g) * k], axis=-1))
    vb, w = sol[:, :, :GDN_DV], sol[:, :, GDN_DV:]
    aqk = bmm('uid,ujd->uij', q, k) * gam
    qg = q * e_g
    g_last = g_col[:, c - 1:c, :]
    kd_t = jnp.swapaxes(k * jnp.exp(g_last - g_col), 1, 2)
    gc = jnp.exp(g_last)

    s = s_scr[...]
    for n in range(nchunk):
        sl = slice(n * nh, (n + 1) * nh)
        u = vb[sl] - bmm('hcd,hde->hce', w[sl], s)
        o = bmm('hcd,hde->hce', qg[sl], s) + bmm('hij,hje->hie', aqk[sl], u)
        s = gc[sl] * s + bmm('hdc,hce->hde', kd_t[sl], u)
        o = o * lax.rsqrt(jnp.mean(o * o, axis=-1, keepdims=True) + EPS) * nw_ref[...]
        for h in range(nh):
            zs = z_ref[0, n * c:(n + 1) * c, h * GDN_DV:(h + 1) * GDN_DV]
            o_ref[0, n * c:(n + 1) * c, h * GDN_DV:(h + 1) * GDN_DV] = (o[h] * _silu(zs)).astype(o_ref.dtype)
    s_scr[...] = s

    @pl.when(j == pl.num_programs(1) - 1)
    def _():
        sfin_ref[0] = s_scr[...]


def _gdn(qkv_raw, z, misc, conv_w, conv_buf, a_log, dt_bias, norm_w, s0, *, tb, n_valid, out_dtype):
    bsz, tp, ch = qkv_raw.shape
    cinit = jnp.concatenate([jnp.zeros((bsz, 5, ch), f32), conv_buf.astype(f32)], axis=1)
    hp = jnp.zeros((8, MISC_W), f32)
    hp = hp.at[0, :GDN_HEADS].set(-jnp.exp(a_log.astype(f32))).at[1, :GDN_HEADS].set(dt_bias.astype(f32))
    zw = GDN_HEADS * GDN_DV
    return pl.pallas_call(
        functools.partial(_gdn_kernel, tb=tb, n_valid=n_valid),
        grid=(bsz, tp // tb),
        in_specs=[pl.BlockSpec((1, tb, ch), lambda b, j: (b, j, 0)),
                  pl.BlockSpec((1, tb, zw), lambda b, j: (b, j, 0)),
                  pl.BlockSpec((1, tb, MISC_W), lambda b, j: (b, j, 0)),
                  pl.BlockSpec((GDN_CONV, ch), lambda b, j: (0, 0)),
                  pl.BlockSpec((1, 8, ch), lambda b, j: (b, 0, 0)),
                  pl.BlockSpec((8, MISC_W), lambda b, j: (0, 0)),
                  pl.BlockSpec((1, GDN_DV), lambda b, j: (0, 0)),
                  pl.BlockSpec((1, GDN_HEADS, GDN_DK, GDN_DV), lambda b, j: (b, 0, 0, 0))],
        out_specs=[pl.BlockSpec((1, tb, zw), lambda b, j: (b, j, 0)),
                   pl.BlockSpec((1, GDN_HEADS, GDN_DK, GDN_DV), lambda b, j: (b, 0, 0, 0))],
        out_shape=[jax.ShapeDtypeStruct((bsz, tp, zw), out_dtype),
                   jax.ShapeDtypeStruct((bsz, GDN_HEADS, GDN_DK, GDN_DV), f32)],
        scratch_shapes=[pltpu.VMEM((tb + 8, ch), f32), pltpu.VMEM((GDN_HEADS, GDN_DK, GDN_DV), f32)],
        compiler_params=_cparams(("parallel", "arbitrary")),
        name="gdn",
    )(qkv_raw, z, misc, conv_w.astype(f32), cinit, hp, norm_w.reshape(1, GDN_DV).astype(f32), s0.astype(f32))


def _bucket_thresholds():
    exact = REL_BUCKETS // 2
    d = np.arange(0, 4 * REL_MAX_DIST, dtype=np.int64)

    def buckets(ft):
        nf = np.maximum(d, exact).astype(ft)
        large = exact + (np.log(nf / ft(exact)) / ft(math.log(REL_MAX_DIST / exact)) * ft(REL_BUCKETS - exact)).astype(np.int32)
        return np.where(d < exact, d, np.minimum(large, REL_BUCKETS - 1))

    b64, b32 = buckets(np.float64), buckets(np.float32)
    assert np.array_equal(b64, b32) and b64[-1] == REL_BUCKETS - 1 and np.all(np.diff(b64) >= 0)
    return [int(np.argmax(b64 >= k)) for k in range(1, REL_BUCKETS)]


_BUCKET_THR = _bucket_thresholds()
FAR_DIST = _BUCKET_THR[-1]


def _bias_kernel(rb_ref, d_ref, o_ref, *, slab, group_heads):
    rows = d_ref.shape[-2]
    for s in range(rows // slab):
        d = d_ref[0, 0, s * slab:(s + 1) * slab, :]
        h = pl.program_id(0) * group_heads + s
        val = jnp.full(d.shape, rb_ref[0, h], f32)
        for k in range(1, REL_BUCKETS):
            val = jnp.where(d >= _BUCKET_THR[k - 1], rb_ref[k, h], val)
        o_ref[0, 0, s * slab:(s + 1) * slab, :] = jnp.where(d < 0, NEG, val)


def _bias_table(rel_bias, dist, *, slab):
    g, nt, rows, cols = dist.shape
    return pl.pallas_call(
        functools.partial(_bias_kernel, slab=slab, group_heads=rows // slab),
        grid=(g, nt),
        in_specs=[pl.BlockSpec(memory_space=pltpu.SMEM),
                  pl.BlockSpec((1, 1, rows, cols), lambda a, b: (a, b, 0, 0))],
        out_specs=pl.BlockSpec((1, 1, rows, cols), lambda a, b: (a, b, 0, 0)),
        out_shape=jax.ShapeDtypeStruct(dist.shape, f32),
        compiler_params=_cparams(("arbitrary", "arbitrary")),
        name="bias_table",
    )(rel_bias.astype(f32), dist)


def _blockmean_kernel(x_ref, o_ref):
    x = x_ref[0]
    nb = x.shape[0] // NSA_BLOCK
    o_ref[0] = jnp.sum(x.reshape(nb, NSA_BLOCK, x.shape[1]), axis=1) * (1.0 / NSA_BLOCK)


def _block_means(kv4, *, tb):
    bsz, t, _ = kv4.shape
    w = 2 * NSA_KV_GROUPS * NSA_DH
    return pl.pallas_call(
        _blockmean_kernel,
        grid=(bsz, t // tb),
        in_specs=[pl.BlockSpec((1, tb, w), lambda b, i: (b, i, 0))],
        out_specs=pl.BlockSpec((1, tb // NSA_BLOCK, w), lambda b, i: (b, i, 0)),
        out_shape=jax.ShapeDtypeStruct((bsz, t // NSA_BLOCK, w), f32),
        compiler_params=_cparams(("parallel", "parallel")),
        name="nsa_block_means",
    )(kv4)


LOG2E = 1.4426950408889634
NSA_KEY_TILE = 4 * PAIR


def _nsa_prompt_kernel(q_ref, misc_ref, kcvc_ref, kaug_ref, vgt_ref, kw_ref, vw_ref,
                       tabs_ref, tabw_ref, tabc_ref, c31_ref, o_ref, qaug_scr, s_scr):
    g = pl.program_id(1)
    i = pl.program_id(2)
    qb, dh, hpg = NSA_QBLOCK, NSA_DH, NSA_HPG
    is_g0 = g == 0
    nn = (((1,), (1,)), ((), ()))

    def ghalf(a):
        return jnp.where(is_g0, a[:, :dh], a[:, dh:])

    qt = q_ref[0] * (dh ** -0.5)
    qh = [qt[:, h * dh:(h + 1) * dh] for h in range(hpg)]
    kcvc = kcvc_ref[0]
    nblk = kcvc.shape[0]
    kc = ghalf(kcvc[:, :2 * dh])
    vc = ghalf(kcvc[:, 2 * dh:])

    rows = hpg * qb
    qs = jnp.concatenate(qh, axis=0)
    zero = jnp.zeros_like(qs)
    q2f = jnp.concatenate([jnp.where(is_g0, qs, zero), jnp.where(is_g0, zero, qs)], axis=1) * LOG2E
    q2 = q2f.astype(bf16)

    blk4 = lax.broadcasted_iota(i32, (rows, nblk), 1)
    qrow4 = lax.broadcasted_iota(i32, (rows, nblk), 0) & (qb - 1)
    first_tab = 2 * i - 2 * NEAR_PAIRS + 1
    shift = (first_tab + 4 * nblk) % nblk
    readable = blk4 * NSA_BLOCK + (NSA_BLOCK - 1) <= i * qb + qrow4
    bias = jnp.where(blk4 < first_tab, c31_ref[0], pltpu.roll(tabc_ref[0], shift, 1))
    s_c = jnp.where(readable, _mm3_nt(qs, kc) + bias, NEG)
    m_c = jnp.max(s_c, axis=-1, keepdims=True)
    p_c = jnp.where(readable, jnp.exp(s_c - m_c), 0.0)
    p_c = p_c / jnp.maximum(jnp.sum(p_c, axis=-1, keepdims=True), 1e-30)
    o_cmp = _mm(p_c, vc)
    imp = p_c[0:qb]
    for h in range(1, hpg):
        imp = imp + p_c[h * qb:(h + 1) * qb]

    span = NSA_WINDOW + qb
    ws = pl.multiple_of(i * qb, qb)
    s_w = lax.dot_general(q2, kw_ref[0, pl.ds(ws, span), :], nn, preferred_element_type=f32) + tabw_ref[0]
    in_seq = lax.broadcasted_iota(i32, (rows, span), 1) + (i * qb - NSA_WINDOW) >= 0
    s_w = jnp.where(in_seq, s_w, NEG)
    m_w = jnp.max(s_w, axis=-1, keepdims=True)
    p_w = jnp.exp2(s_w - m_w)
    l_w = jnp.sum(p_w, axis=-1, keepdims=True)
    o_win = ghalf(jnp.dot(p_w.astype(bf16), vw_ref[0, pl.ds(ws, span), :], preferred_element_type=f32))
    o_win = o_win / jnp.maximum(l_w, 1e-30)

    blk = lax.broadcasted_iota(i32, (nblk, qb), 0)
    cur = (i * qb + lax.broadcasted_iota(i32, (nblk, qb), 1)) >> int(math.log2(NSA_BLOCK))
    valid = blk <= cur
    forced = valid & ((blk == 0) | (blk > cur - NSA_LOCAL))
    imp_t = jnp.where(forced, -jnp.inf, jnp.where(valid, imp.T, -1.0))
    picked = forced
    for _ in range(min(NSA_TOPN, nblk) - 1 - NSA_LOCAL):
        mx = jnp.max(imp_t, axis=0, keepdims=True)
        first = jnp.min(jnp.where(imp_t == mx, blk, nblk), axis=0, keepdims=True)
        hit = blk == first
        picked = picked | hit
        imp_t = jnp.where(hit, -jnp.inf, imp_t)
    msel_t = jnp.where(picked & valid, 0.0, NEG).astype(bf16)

    qaug_scr[...] = jnp.concatenate([q2f.T.astype(bf16), jnp.concatenate([msel_t] * hpg, axis=1)], axis=0)
    tk = s_scr.shape[0]
    ppt = tk // PAIR
    t_d = i // ppt

    def scores(t):
        ks = pl.multiple_of(t * tk, tk)
        s = jnp.dot(kaug_ref[0, pl.ds(ks, tk), :], qaug_scr[...], preferred_element_type=f32)
        tabs = [tabs_ref[0, jnp.clip(i - ppt * t - j, 0, NEAR_PAIRS)] for j in range(ppt)]
        return s + jnp.concatenate(tabs, axis=0)

    s_scr[...] = scores(t_d)

    def tile(k_it, carry):
        m, acc = carry
        t = t_d - k_it
        s = s_scr[...]
        s_next = scores(jnp.maximum(t - 1, 0))
        m_new = jnp.maximum(m, jnp.max(s, axis=0, keepdims=True))
        alpha = jnp.exp2(m - m_new)
        p = jnp.exp2(s - m_new).astype(bf16)
        ks = pl.multiple_of(t * tk, tk)
        acc = alpha * acc + jnp.dot(vgt_ref[0, 0, :, pl.ds(ks, tk)], p, preferred_element_type=f32)
        s_scr[...] = s_next
        return m_new, acc

    _, acc_t = lax.fori_loop(0, t_d + 1, tile, (jnp.full((1, rows), NEG, f32), jnp.zeros((2 * dh, rows), f32)))
    o_sel = (acc_t[:dh] / jnp.maximum(acc_t[dh:dh + 1], 1e-30)).T

    gates = jax.nn.sigmoid(misc_ref[0])
    outs = []
    for h in range(hpg):
        r = slice(h * qb, (h + 1) * qb)
        acc = None
        for jb, branch in enumerate((o_cmp[r], o_sel[r], o_win[r])):
            la = 8 + 3 * h + jb
            lb = 8 + 3 * (hpg + h) + jb
            term = jnp.where(is_g0, gates[:, la:la + 1], gates[:, lb:lb + 1]) * branch
            acc = term if acc is None else acc + term
        outs.append(acc)
    o_ref[0] = jnp.concatenate(outs, axis=1).astype(o_ref.dtype)


def _nsa_tables(rel_bias):
    qb, hpg = NSA_QBLOCK, NSA_HPG
    rows = hpg * qb
    q = (np.arange(rows) % qb)[:, None]
    c = np.arange(PAIR)[None, :]
    d_sel = [PAIR * idx + q - c for idx in range(NEAR_PAIRS + 1)]
    assert PAIR * NEAR_PAIRS - (PAIR - 1) >= FAR_DIST
    mm = 2 * NEAR_PAIRS - 1 - c
    d_cmp = np.maximum(np.where(mm >= -1, NSA_BLOCK * mm + q - (NSA_BLOCK - 1), 0), 0)
    assert NSA_BLOCK * (2 * NEAR_PAIRS) - (NSA_BLOCK - 1) >= FAR_DIST
    cw = np.arange(NSA_WINDOW + qb)[None, :]
    d_win = q + NSA_WINDOW - cw
    d_win = np.where((d_win >= 0) & (d_win < NSA_WINDOW), d_win, -1)
    n_win = (NSA_WINDOW + qb) // PAIR
    tiles = d_sel + [d_cmp] + [d_win[:, k * PAIR:(k + 1) * PAIR] for k in range(n_win)]
    dist = np.broadcast_to(np.stack(tiles)[None], (NSA_KV_GROUPS, len(tiles), rows, PAIR))
    tab = _bias_table(rel_bias, jnp.asarray(dist, i32), slab=qb)
    ns = NEAR_PAIRS + 1
    tabs = tab[:, :ns]
    tabc = tab[:, ns]
    tabw = jnp.concatenate([tab[:, ns + 1 + k] for k in range(n_win)], axis=-1)
    c31 = jnp.repeat(rel_bias.astype(f32)[REL_BUCKETS - 1].reshape(NSA_KV_GROUPS, hpg), qb, axis=1)
    return tabs, tabw, tabc, c31.reshape(NSA_KV_GROUPS, rows, 1)


def _nsa_prompt(q_b, misc, kv4, kvwin, rel_bias):
    bsz, t, _ = q_b.shape
    dh, qb = NSA_DH, NSA_QBLOCK
    nblk = t // NSA_BLOCK
    assert nblk == PAIR and t % NSA_KEY_TILE == 0
    kcvc = _block_means(kv4, tb=512)
    onehot = (jnp.arange(t, dtype=i32)[:, None] // NSA_BLOCK == jnp.arange(nblk, dtype=i32)[None, :]).astype(bf16)
    kaug = jnp.concatenate([kv4[:, :, 4 * dh:6 * dh].astype(bf16),
                            jnp.broadcast_to(onehot[None], (bsz, t, nblk))], axis=-1)
    ones = jnp.ones((bsz, t, dh), bf16)
    vgt = jnp.stack([jnp.swapaxes(jnp.concatenate([kv4[:, :, (6 + g) * dh:(7 + g) * dh].astype(bf16), ones],
                                                   axis=-1), 1, 2)
                     for g in range(NSA_KV_GROUPS)], axis=1)
    pad = ((0, 0), (NSA_WINDOW, 0), (0, 0))
    kw = jnp.pad(kvwin[:, :, :2 * dh].astype(bf16), pad)
    vw = jnp.pad(kvwin[:, :, 2 * dh:].astype(bf16), pad)
    tabs, tabw, tabc, c31 = _nsa_tables(rel_bias)
    tabs, tabw = jnp.swapaxes(tabs, 2, 3) * LOG2E, tabw * LOG2E
    rows = NSA_HPG * qb
    gw = NSA_HPG * dh
    span = NSA_WINDOW + qb
    return pl.pallas_call(
        _nsa_prompt_kernel,
        grid=(bsz, NSA_KV_GROUPS, t // qb),
        in_specs=[pl.BlockSpec((1, qb, gw), lambda b, g, i: (b, i, g)),
                  pl.BlockSpec((1, qb, MISC_W), lambda b, g, i: (b, i, 0)),
                  pl.BlockSpec((1, nblk, 4 * dh), lambda b, g, i: (b, 0, 0)),
                  pl.BlockSpec((1, t, 2 * dh + nblk), lambda b, g, i: (b, 0, 0)),
                  pl.BlockSpec((1, 1, 2 * dh, t), lambda b, g, i: (b, g, 0, 0)),
                  pl.BlockSpec((1, t + NSA_WINDOW, 2 * dh), lambda b, g, i: (b, 0, 0)),
                  pl.BlockSpec((1, t + NSA_WINDOW, 2 * dh), lambda b, g, i: (b, 0, 0)),
                  pl.BlockSpec((1, NEAR_PAIRS + 1, PAIR, rows), lambda b, g, i: (g, 0, 0, 0)),
                  pl.BlockSpec((1, rows, span), lambda b, g, i: (g, 0, 0)),
                  pl.BlockSpec((1, rows, PAIR), lambda b, g, i: (g, 0, 0)),
                  pl.BlockSpec((1, rows, 1), lambda b, g, i: (g, 0, 0))],
        out_specs=pl.BlockSpec((1, qb, gw), lambda b, g, i: (b, i, g)),
        out_shape=jax.ShapeDtypeStruct((bsz, t, NSA_HEADS * dh), bf16),
        scratch_shapes=[pltpu.VMEM((2 * dh + nblk, rows), bf16), pltpu.VMEM((NSA_KEY_TILE, rows), f32)],
        compiler_params=_cparams(("parallel", "parallel", "arbitrary")),
        name="nsa_prompt",
    )(q_b, misc, kcvc, kaug, vgt, kw, vw, tabs, tabw, tabc, c31)


MEANS_PAGES = 8


def _nsa_s_means_kernel(pt_ref, *refs):
    del pt_ref
    x_refs, o_ref = refs[:-1], refs[-1]
    p = pl.program_id(1)

    @pl.when(p == 0)
    def _():
        o_ref[...] = jnp.zeros_like(o_ref)

    page = x_refs[0].shape[-1]
    per_page = page // NSA_BLOCK
    nblk = o_ref.shape[-1]
    assert page == nblk
    planes = 2 * NSA_KV_GROUPS
    xs = jnp.concatenate([x_ref[0, j, g] for x_ref in x_refs for j in range(2) for g in range(NSA_KV_GROUPS)],
                         axis=0)
    tok = lax.broadcasted_iota(i32, (page, nblk), 0)
    col = lax.broadcasted_iota(i32, (page, nblk), 1)
    pool = (col == (tok >> int(math.log2(NSA_BLOCK)))).astype(bf16)
    hi, lo = _split2(xs)
    sums = jnp.dot(hi, pool, preferred_element_type=f32) + jnp.dot(lo, pool, preferred_element_type=f32)
    rows = planes * NSA_DH
    acc = None
    for k in range(len(x_refs)):
        shift = (p * len(x_refs) + k) * per_page
        part = pltpu.roll(sums[k * rows:(k + 1) * rows], shift, 1)
        acc = part if acc is None else acc + part
    acc = acc * (1.0 / NSA_BLOCK)
    for pl_i in range(planes):
        o_ref[0, pl_i] = o_ref[0, pl_i] + acc[pl_i * NSA_DH:(pl_i + 1) * NSA_DH]


def _nsa_s_means(cache_t, page_table):
    db, n_pages = page_table.shape
    page = cache_t.shape[-1]
    nblk = n_pages * page // NSA_BLOCK
    kp = MEANS_PAGES
    assert n_pages % kp == 0

    def page_spec(k):
        return pl.BlockSpec((1, 2, NSA_KV_GROUPS, NSA_DH, page),
                            lambda b, p, pt: (pt[b * n_pages + p * kp + k], 0, 0, 0, 0))

    return pl.pallas_call(
        _nsa_s_means_kernel,
        grid_spec=pltpu.PrefetchScalarGridSpec(
            num_scalar_prefetch=1,
            grid=(db, n_pages // kp),
            in_specs=[page_spec(k) for k in range(kp)],
            out_specs=pl.BlockSpec((1, 2 * NSA_KV_GROUPS, NSA_DH, nblk), lambda b, p, pt: (b, 0, 0, 0))),
        out_shape=jax.ShapeDtypeStruct((db, 2 * NSA_KV_GROUPS, NSA_DH, nblk), f32),
        compiler_params=_cparams(("parallel", "arbitrary")),
        name="nsa_sample_means",
    )(page_table.reshape(-1), *([cache_t] * kp))


def _nsa_s_scores_kernel(q_ref, kcvc_ref, win_ref, kvn_ref, tc_ref, tw_ref, b0_ref, oc_ref, ow_ref, sel_ref):
    dh, hpg, ng = NSA_DH, NSA_HPG, NSA_KV_GROUPS
    sb = q_ref.shape[0]
    nblk = kcvc_ref.shape[-1]
    row = lax.broadcasted_iota(i32, (NSA_HEADS, 1), 0)
    in_g = [(row >= g * hpg) & (row < (g + 1) * hpg) for g in range(ng)]
    pairs = [(b, g) for b in range(sb) for g in range(ng)]
    q = [q_ref[b] * (dh ** -0.5) for b in range(sb)]

    s_c = [_mm3(q[b], kcvc_ref[b, g]) + tc_ref[...] for b, g in pairs]
    p_c = []
    for s in s_c:
        p = jnp.exp(s - jnp.max(s, axis=-1, keepdims=True))
        p_c.append(p / jnp.maximum(jnp.sum(p, axis=-1, keepdims=True), 1e-30))
    o_c = [_mm_nt(p_c[k], kcvc_ref[b, ng + g]) for k, (b, g) in enumerate(pairs)]
    imp = jnp.concatenate([jnp.sum(jnp.where(in_g[g], p_c[k], 0.0), axis=0, keepdims=True)
                           for k, (b, g) in enumerate(pairs)], axis=0)
    for b in range(sb):
        oc_ref[b] = jnp.where(in_g[0], o_c[b * ng], o_c[b * ng + 1])

    s_w = [_mm3(q[b], win_ref[b, 0, g]) + tw_ref[...] for b, g in pairs]
    o_w = []
    for k, (b, g) in enumerate(pairs):
        kvn = kvn_ref[b]
        s_n = jnp.sum(q[b] * kvn[:, g * dh:(g + 1) * dh], axis=-1, keepdims=True) + b0_ref[...]
        m_w = jnp.maximum(jnp.max(s_w[k], axis=-1, keepdims=True), s_n)
        p_w = jnp.exp(s_w[k] - m_w)
        p_n = jnp.exp(s_n - m_w)
        l_w = jnp.sum(p_w, axis=-1, keepdims=True) + p_n
        v_n = kvn[:, (ng + g) * dh:(ng + g + 1) * dh]
        o_w.append((_mm_nt(p_w, win_ref[b, 1, g]) + p_n * v_n) / jnp.maximum(l_w, 1e-30))
    for b in range(sb):
        ow_ref[b] = jnp.where(in_g[0], o_w[b * ng], o_w[b * ng + 1])

    lane = lax.broadcasted_iota(i32, imp.shape, 1)
    n_sel = NSA_TOPN - 1
    forced_blocks = [0] + [nblk - k for k in range(1, NSA_LOCAL)]
    imp = jnp.where((lane == 0) | (lane > nblk - NSA_LOCAL), -jnp.inf, imp)
    picks = jnp.zeros(imp.shape, i32)
    for it, fb in enumerate(forced_blocks):
        picks = jnp.where(lane == it, fb, picks)
    for it in range(len(forced_blocks), n_sel):
        mx = jnp.max(imp, axis=-1, keepdims=True)
        first = jnp.min(jnp.where(imp == mx, lane, nblk), axis=-1, keepdims=True)
        picks = jnp.where(lane == it, first, picks)
        imp = jnp.where(lane == first, -jnp.inf, imp)
    for b in range(sb):
        sel_ref[b] = picks[b * ng:(b + 1) * ng]


def _nsa_s_sel_kernel(sel_ref, pt_ref, q_ref, rb_ref, kvn_ref, oc_ref, ow_ref, gt_ref, *refs, n_sel, past):
    del pt_ref
    k_refs, v_refs, o_ref = refs[:n_sel], refs[n_sel:2 * n_sel], refs[2 * n_sel]
    b = pl.program_id(0)
    g = pl.program_id(1)
    dh, hpg = NSA_DH, NSA_HPG
    is_g0 = g == 0
    page = k_refs[0].shape[-1]
    per_page = page // NSA_BLOCK
    q = q_ref[0] * (dh ** -0.5)
    kvn = kvn_ref[0]
    lane = lax.broadcasted_iota(i32, (NSA_HEADS, page), 1)
    s_parts, d_parts = [], []
    for j in range(n_sel):
        nb = sel_ref[(b * NSA_KV_GROUPS + g) * n_sel + j]
        in_blk = (lane >> int(math.log2(NSA_BLOCK))) == (nb % per_page)
        d_parts.append(jnp.where(in_blk, past - ((nb // per_page) * page + lane), -1))
        s_parts.append(_mm3(q, k_refs[j][0, 0, 0]))
    d = jnp.concatenate(d_parts, axis=1)
    bias = jnp.broadcast_to(rb_ref[0], d.shape)
    for k in range(1, REL_BUCKETS):
        bias = jnp.where(d >= _BUCKET_THR[k - 1], rb_ref[k], bias)
    s = jnp.concatenate(s_parts, axis=1) + jnp.where(d < 0, NEG, bias)
    k_n = jnp.where(is_g0, kvn[:, 4 * dh:5 * dh], kvn[:, 5 * dh:6 * dh])
    v_n = jnp.where(is_g0, kvn[:, 6 * dh:7 * dh], kvn[:, 7 * dh:8 * dh])
    s_n = jnp.sum(q * k_n, axis=-1, keepdims=True) + rb_ref[0]
    m = jnp.maximum(jnp.max(s, axis=-1, keepdims=True), s_n)
    p = jnp.exp(s - m)
    p_n = jnp.exp(s_n - m)
    l = jnp.sum(p, axis=-1, keepdims=True) + p_n
    acc = p_n * v_n
    for j in range(n_sel):
        acc = acc + _mm_nt(p[:, j * page:(j + 1) * page], v_refs[j][0, 0, 0])
    o_s = acc / jnp.maximum(l, 1e-30)
    gt = gt_ref[0]
    o = gt[:, 0:1] * oc_ref[0] + gt[:, 1:2] * o_s + gt[:, 2:3] * ow_ref[0]
    row = lax.broadcasted_iota(i32, o.shape, 0)
    in_g = (row >= g * hpg) & (row < (g + 1) * hpg)

    @pl.when(is_g0)
    def _():
        o_ref[0] = o

    @pl.when(jnp.logical_not(is_g0))
    def _():
        o_ref[0] = jnp.where(in_g, o, o_ref[0])


def _nsa_sample(q_b, gates, kv4_new, kvwin_new, cache_kv, cache_win, page_table, rel_bias):
    db = q_b.shape[0]
    dh = NSA_DH
    n_pool, page = cache_kv.shape[:2]
    n_pages = page_table.shape[1]
    past = n_pages * page
    nblk = past // NSA_BLOCK
    wb = cache_win.shape[1]
    assert nblk == PAIR and page % NSA_BLOCK == 0 and wb == NSA_WINDOW and nblk > NSA_LOCAL
    cache_t = jnp.transpose(cache_kv, (0, 2, 3, 4, 1)).astype(f32)
    win_t = jnp.transpose(cache_win, (0, 2, 3, 4, 1)).astype(f32)
    kcvc = _nsa_s_means(cache_t, page_table)

    n = np.arange(nblk)
    d_cmp = past - (n * NSA_BLOCK + NSA_BLOCK - 1)
    jw = np.arange(wb)
    d_win = np.where(jw >= 1, wb - jw, -1)
    dist = np.broadcast_to(np.concatenate([d_cmp, d_win])[None, :], (NSA_HEADS, nblk + wb))
    tab = _bias_table(rel_bias, jnp.asarray(dist[None, None], i32), slab=1)[0, 0]
    t_cmp, t_win = tab[:, :nblk], tab[:, nblk:]
    rb = rel_bias.astype(f32)
    b0 = rb[0].reshape(NSA_HEADS, 1)

    q3 = q_b.reshape(db, NSA_HEADS, dh)
    sb = max(s for s in (8, 4, 2, 1) if db % s == 0)
    o_c, o_w, sel = pl.pallas_call(
        _nsa_s_scores_kernel,
        grid=(db // sb,),
        in_specs=[pl.BlockSpec((sb, NSA_HEADS, dh), lambda b: (b, 0, 0)),
                  pl.BlockSpec((sb, 2 * NSA_KV_GROUPS, dh, nblk), lambda b: (b, 0, 0, 0)),
                  pl.BlockSpec((sb, 2, NSA_KV_GROUPS, dh, wb), lambda b: (b, 0, 0, 0, 0)),
                  pl.BlockSpec((sb, 1, 4 * dh), lambda b: (b, 0, 0)),
                  pl.BlockSpec((NSA_HEADS, nblk), lambda b: (0, 0)),
                  pl.BlockSpec((NSA_HEADS, wb), lambda b: (0, 0)),
                  pl.BlockSpec((NSA_HEADS, 1), lambda b: (0, 0))],
        out_specs=[pl.BlockSpec((sb, NSA_HEADS, dh), lambda b: (b, 0, 0)),
                   pl.BlockSpec((sb, NSA_HEADS, dh), lambda b: (b, 0, 0)),
                   pl.BlockSpec((sb, NSA_KV_GROUPS, nblk), lambda b: (b, 0, 0))],
        out_shape=[jax.ShapeDtypeStruct((db, NSA_HEADS, dh), f32),
                   jax.ShapeDtypeStruct((db, NSA_HEADS, dh), f32),
                   jax.ShapeDtypeStruct((db, NSA_KV_GROUPS, nblk), i32)],
        compiler_params=_cparams(("parallel",)),
        name="nsa_sample_scores",
    )(q3, kcvc, win_t, kvwin_new.reshape(db, 1, 4 * dh), t_cmp, t_win, b0)

    n_sel = NSA_TOPN - 1
    sel_flat = sel[:, :, :n_sel].reshape(-1)
    per_page = page // NSA_BLOCK

    def page_spec(j, plane):
        def imap(b, g, s, p):
            nb = s[(b * NSA_KV_GROUPS + g) * n_sel + j]
            return (p[b * n_pages + nb // per_page], plane, g, 0, 0)
        return pl.BlockSpec((1, 1, 1, dh, page), imap)

    const = lambda shape: pl.BlockSpec(shape, lambda b, g, s, p: (0,) * len(shape))
    per_b = lambda shape: pl.BlockSpec((1,) + shape, lambda b, g, s, p: (b,) + (0,) * len(shape))
    o = pl.pallas_call(
        functools.partial(_nsa_s_sel_kernel, n_sel=n_sel, past=past),
        grid_spec=pltpu.PrefetchScalarGridSpec(
            num_scalar_prefetch=2,
            grid=(db, NSA_KV_GROUPS),
            in_specs=[per_b((NSA_HEADS, dh)), const((REL_BUCKETS, NSA_HEADS, 1)), per_b((1, 8 * dh)),
                      per_b((NSA_HEADS, dh)), per_b((NSA_HEADS, dh)), per_b((NSA_HEADS, 3))]
                     + [page_spec(j, 2) for j in range(n_sel)] + [page_spec(j, 3) for j in range(n_sel)],
            out_specs=per_b((NSA_HEADS, dh))),
        out_shape=jax.ShapeDtypeStruct((db, NSA_HEADS, dh), f32),
        compiler_params=_cparams(("parallel", "arbitrary")),
        name="nsa_sample_selected",
    )(sel_flat, page_table.reshape(-1), q3, rb.reshape(REL_BUCKETS, NSA_HEADS, 1), kv4_new.reshape(db, 1, 8 * dh),
      o_c, o_w, gates, *([cache_t] * (2 * n_sel)))
    return o.reshape(db, NSA_HEADS * dh)


def _gelu_tanh(x):
    return 0.5 * x * (1.0 + jnp.tanh(math.sqrt(2.0 / math.pi) * (x + 0.044715 * (x * x * x))))


def _lru_gates(xc, wa_ref, wx_ref, ba_ref, bx_ref, lam_ref, prec):
    r_parts, i_parts = [], []
    for n in range(RNN_BLOCKS):
        xb = xc[:, n * RNN_BW:(n + 1) * RNN_BW]
        r_parts.append(_dotp(xb, wa_ref[n], prec))
        i_parts.append(_dotp(xb, wx_ref[n], prec))
    r = jax.nn.sigmoid(jnp.concatenate(r_parts, axis=1) + ba_ref[...])
    i = jax.nn.sigmoid(jnp.concatenate(i_parts, axis=1) + bx_ref[...])
    log_a = -RG_C * r * _softplus(-lam_ref[...])
    a = jnp.exp(log_a)
    t = jnp.tanh(log_a)
    b = jnp.sqrt(jnp.maximum(-2.0 * t / (1.0 - t), 0.0)) * (i * xc)
    return a, b


def _lru_kernel(rec_ref, gate_ref, cw_ref, cb_ref, wa_ref, wx_ref, ba_ref, bx_ref, lam_ref, cinit_ref, h0_ref,
                y_ref, hfin_ref, xc_scr, a_scr, b_scr, hs_scr, h_scr, *, tb):
    j = pl.program_id(1)

    @pl.when(j == 0)
    def _():
        xc_scr[0:8, :] = cinit_ref[0]
        h_scr[...] = h0_ref[0]

    x = rec_ref[0]
    xc_scr[8:8 + tb, :] = x
    xc = xc_scr[5:5 + tb, :] * cw_ref[0:1, :]
    xc = xc + xc_scr[6:6 + tb, :] * cw_ref[1:2, :]
    xc = xc + xc_scr[7:7 + tb, :] * cw_ref[2:3, :]
    xc = xc + x * cw_ref[3:4, :]
    xc = xc + cb_ref[...]
    xc_scr[0:8, :] = xc_scr[tb:tb + 8, :]
    a, b = _lru_gates(xc, wa_ref, wx_ref, ba_ref, bx_ref, lam_ref, 1)
    a_scr[...] = a
    b_scr[...] = b

    def step(t, h):
        h = a_scr[pl.ds(t, 1), :] * h + b_scr[pl.ds(t, 1), :]
        hs_scr[pl.ds(t, 1), :] = h
        return h

    h = lax.fori_loop(0, tb, step, h_scr[...], unroll=8)
    h_scr[...] = h
    y_ref[0] = (_gelu_tanh(gate_ref[0]) * hs_scr[...]).astype(y_ref.dtype)

    @pl.when(j == pl.num_programs(1) - 1)
    def _():
        hfin_ref[0] = h


def _lru_prompt(rec, gate, conv_w, conv_b, wa, wx, ba, bx, lam, *, tb):
    bsz, t, w = rec.shape
    row = lambda a: a.reshape(1, w).astype(f32)
    cinit = jnp.zeros((bsz, 8, w), f32)
    h0 = jnp.zeros((bsz, 1, w), f32)
    full = lambda shape: pl.BlockSpec(shape, lambda b, j: (0,) * len(shape))
    return pl.pallas_call(
        functools.partial(_lru_kernel, tb=tb),
        grid=(bsz, t // tb),
        in_specs=[pl.BlockSpec((1, tb, w), lambda b, j: (b, j, 0)),
                  pl.BlockSpec((1, tb, w), lambda b, j: (b, j, 0)),
                  full((RNN_CONV, w)), full((1, w)),
                  full((RNN_BLOCKS, RNN_BW, RNN_BW)), full((RNN_BLOCKS, RNN_BW, RNN_BW)),
                  full((1, w)), full((1, w)), full((1, w)),
                  pl.BlockSpec((1, 8, w), lambda b, j: (b, 0, 0)),
                  pl.BlockSpec((1, 1, w), lambda b, j: (b, 0, 0))],
        out_specs=[pl.BlockSpec((1, tb, w), lambda b, j: (b, j, 0)),
                   pl.BlockSpec((1, 1, w), lambda b, j: (b, 0, 0))],
        out_shape=[jax.ShapeDtypeStruct((bsz, t, w), bf16), jax.ShapeDtypeStruct((bsz, 1, w), f32)],
        scratch_shapes=[pltpu.VMEM((tb + 8, w), f32), pltpu.VMEM((tb, w), f32), pltpu.VMEM((tb, w), f32),
                        pltpu.VMEM((tb, w), f32), pltpu.VMEM((1, w), f32)],
        compiler_params=_cparams(("parallel", "arbitrary")),
        name="rglru",
    )(rec, gate, conv_w.astype(f32), row(conv_b), wa.astype(f32), wx.astype(f32), row(ba), row(bx), row(lam),
      cinit, h0)


def _lru_step_kernel(rec_ref, gate_ref, b0_ref, b1_ref, b2_ref, cw_ref, cb_ref, wa_ref, wx_ref, ba_ref, bx_ref,
                     lam_ref, h0_ref, y_ref, h_ref):
    xc = b0_ref[...] * cw_ref[0:1, :]
    xc = xc + b1_ref[...] * cw_ref[1:2, :]
    xc = xc + b2_ref[...] * cw_ref[2:3, :]
    xc = xc + rec_ref[...] * cw_ref[3:4, :]
    xc = xc + cb_ref[...]
    a, b = _lru_gates(xc, wa_ref, wx_ref, ba_ref, bx_ref, lam_ref, 3)
    h = a * h0_ref[...] + b
    h_ref[...] = h
    y_ref[...] = _gelu_tanh(gate_ref[...]) * h


def _lru_sample(rec, gate, conv_buf, h0, conv_w, conv_b, wa, wx, ba, bx, lam):
    db, w = rec.shape
    row = lambda a: a.reshape(1, w).astype(f32)
    buf = conv_buf.astype(f32)
    return pl.pallas_call(
        _lru_step_kernel,
        out_shape=[jax.ShapeDtypeStruct((db, w), f32), jax.ShapeDtypeStruct((db, w), f32)],
        compiler_params=pltpu.CompilerParams(vmem_limit_bytes=VMEM_LIMIT),
        name="rglru_step",
    )(rec, gate, buf[:, 0], buf[:, 1], buf[:, 2], conv_w.astype(f32), row(conv_b), wa.astype(f32), wx.astype(f32),
      row(ba), row(bx), row(lam), h0.astype(f32))


def _top2_kernel(l_ref, e_ref, g_ref, cnt_ref, cnt_scr):
    @pl.when(pl.program_id(0) == 0)
    def _():
        cnt_scr[...] = jnp.zeros_like(cnt_scr)

    lg = l_ref[...]
    tm, w = lg.shape
    lane = lax.broadcasted_iota(i32, lg.shape, 1)
    lg = jnp.where(lane < N_EXPERTS, lg, -jnp.inf)
    m1 = jnp.max(lg, axis=-1, keepdims=True)
    i1 = jnp.min(jnp.where(lg == m1, lane, w), axis=-1, keepdims=True)
    lg2 = jnp.where(lane == i1, -jnp.inf, lg)
    m2 = jnp.max(lg2, axis=-1, keepdims=True)
    i2 = jnp.min(jnp.where(lg2 == m2, lane, w), axis=-1, keepdims=True)
    e2 = jnp.exp(m2 - m1)
    den = 1.0 + e2
    g_ref[...] = jnp.where(lane == 0, 1.0 / den, jnp.where(lane == 1, e2 / den, 0.0))
    hit1, hit2 = lane == i1, lane == i2
    routed = (hit1 | hit2).astype(bf16)
    r_i = lax.broadcasted_iota(i32, (tm, tm), 0)
    c_i = lax.broadcasted_iota(i32, (tm, tm), 1)
    before = jnp.dot((c_i < r_i).astype(bf16), routed, preferred_element_type=f32) + cnt_scr[...]
    rank1 = jnp.sum(jnp.where(hit1, before, 0.0), axis=-1, keepdims=True).astype(i32)
    rank2 = jnp.sum(jnp.where(hit2, before, 0.0), axis=-1, keepdims=True).astype(i32)
    e_ref[...] = jnp.where(lane == 0, i1, jnp.where(lane == 1, i2, jnp.where(lane == 2, rank1,
                                                                               jnp.where(lane == 3, rank2, 0))))
    cnt_scr[...] += jnp.sum(routed.astype(f32), axis=0, keepdims=True)
    cnt_ref[...] = cnt_scr[...].astype(i32)


def _top2(logits, *, tm):
    n, w = logits.shape
    return pl.pallas_call(
        _top2_kernel,
        grid=(n // tm,),
        in_specs=[pl.BlockSpec((tm, w), lambda i: (i, 0))],
        out_specs=[pl.BlockSpec((tm, w), lambda i: (i, 0)), pl.BlockSpec((tm, w), lambda i: (i, 0)),
                   pl.BlockSpec((1, w), lambda i: (0, 0))],
        out_shape=[jax.ShapeDtypeStruct((n, w), i32), jax.ShapeDtypeStruct((n, w), f32),
                   jax.ShapeDtypeStruct((1, w), i32)],
        scratch_shapes=[pltpu.VMEM((1, w), f32)],
        compiler_params=_cparams(("arbitrary",)),
        name="moe_top2",
    )(logits)


def _moe_kernel(be_ref, nu_ref, nv_ref, xs_ref, w1_ref, w3_ref, w2_ref, o_ref, acc_scr):
    del be_ref
    i = pl.program_id(0)
    f = pl.program_id(1)
    last = pl.num_programs(1) - 1
    used = i < nu_ref[0]
    half = xs_ref.shape[0] // 2
    top_only = nv_ref[i] <= half

    @pl.when(used & (f == 0))
    def _():
        acc_scr[...] = jnp.zeros_like(acc_scr)

    def update(rows):
        x = xs_ref[rows, :]
        act = _silu(_mm(x, w1_ref[0])) * _mm(x, w3_ref[0])
        acc_scr[rows, :] += _mm(act, w2_ref[0])

    @pl.when(used & jnp.logical_not(top_only))
    def _():
        update(slice(None))

    @pl.when(used & top_only)
    def _():
        update(slice(0, half))

    @pl.when(used & (f == last))
    def _():
        o_ref[...] = acc_scr[...]

    @pl.when(jnp.logical_not(used) & (f == last))
    def _():
        o_ref[...] = jnp.zeros_like(o_ref)


def _moe_experts(xs, blk_e, n_used, n_valid, w1, w3, w2, *, tm, tf):
    rows, d = xs.shape
    fdim = w1.shape[2]
    nf = fdim // tf

    def wcol(i, f, be, nu, nv):
        return (be[i], 0, jnp.where(i < nu[0], f, nf - 1))

    def wrow(i, f, be, nu, nv):
        return (be[i], jnp.where(i < nu[0], f, nf - 1), 0)

    return pl.pallas_call(
        _moe_kernel,
        grid_spec=pltpu.PrefetchScalarGridSpec(
            num_scalar_prefetch=3,
            grid=(rows // tm, nf),
            in_specs=[pl.BlockSpec((tm, d), lambda i, f, be, nu, nv: (i, 0)),
                      pl.BlockSpec((1, d, tf), wcol),
                      pl.BlockSpec((1, d, tf), wcol),
                      pl.BlockSpec((1, tf, d), wrow)],
            out_specs=pl.BlockSpec((tm, d), lambda i, f, be, nu, nv: (i, 0)),
            scratch_shapes=[pltpu.VMEM((tm, d), f32)]),
        out_shape=jax.ShapeDtypeStruct((rows, d), f32),
        compiler_params=_cparams(("arbitrary", "arbitrary")),
        name="moe_experts",
    )(blk_e, n_used, n_valid, xs, w1, w3, w2)


def _combine_kernel(x_ref, gate_ref, y0_ref, y1_ref, g_ref, nw_ref, o_ref):
    gw = g_ref[...]
    y = y0_ref[...] * gw[:, 0:1] + y1_ref[...] * gw[:, 1:2]
    x = x_ref[0] + gate_ref[0] * y
    o_ref[0] = x * lax.rsqrt(jnp.mean(x * x, axis=-1, keepdims=True) + EPS) * nw_ref[...]


def _moe_combine(x, gate, y0, y1, gw, norm_w, *, tm, row0):
    bsz, t, d = x.shape
    assert row0 % tm == 0
    tok = pl.BlockSpec((1, tm, d), lambda b, i: (b, i, 0))
    flat = lambda w: pl.BlockSpec((tm, w), lambda b, i: (row0 // tm + b * (t // tm) + i, 0))
    return pl.pallas_call(
        _combine_kernel,
        grid=(bsz, t // tm),
        in_specs=[tok, _mod_spec(gate, tm), flat(d), flat(d), flat(gw.shape[-1]),
                  pl.BlockSpec((1, d), lambda b, i: (0, 0))],
        out_specs=tok,
        out_shape=jax.ShapeDtypeStruct((bsz, t, d), f32),
        compiler_params=_cparams(("parallel", "parallel")),
        name="moe_combine",
    )(x, gate, y0, y1, gw, norm_w.reshape(1, d).astype(f32))


MOE_TM = 1024
MOE_TF = 512


def _moe_dispatch(e_idx, rank, counts, n_tok):
    tm = MOE_TM
    n_assign = n_tok * TOP_K
    padded = (counts + tm - 1) // tm * tm
    pend = jnp.cumsum(padded)
    pstart = pend - padded
    experts = jnp.arange(N_EXPERTS, dtype=i32)
    dest = jnp.sum(jnp.where(e_idx[:, :, None] == experts, pstart, 0), axis=-1) + rank
    n_blocks = -(-n_assign // tm) + N_EXPERTS
    rows = n_blocks * tm
    tok = jnp.broadcast_to(jnp.arange(n_tok, dtype=i32)[:, None], (n_tok, TOP_K))
    row_tok = (jnp.arange(rows, dtype=i32) % n_tok).at[dest.reshape(-1)].set(
        tok.reshape(-1), unique_indices=True, mode='promise_in_bounds')
    n_used = (pend[-1] // tm).astype(i32)
    blk = jnp.minimum(jnp.arange(n_blocks, dtype=i32), n_used - 1) * tm
    blk_e = jnp.minimum(jnp.sum((blk[:, None] >= pend[None, :]).astype(i32), axis=1), N_EXPERTS - 1)
    is_e = blk_e[:, None] == experts[None, :]
    first_row = jnp.sum(jnp.where(is_e, pstart, 0), axis=1)
    n_valid = jnp.clip(jnp.sum(jnp.where(is_e, counts, 0), axis=1) - (blk - first_row), 0, tm)
    return dest, row_tok, blk_e, n_used.reshape(1), n_valid.astype(i32)


def _w_in0_layout(w_in0):
    c_qkv = GDN_CONV_CH
    c_z = c_qkv + GDN_HEADS * GDN_DV
    c_ab = c_z + 2 * GDN_HEADS
    c_q = c_ab + NSA_HEADS * NSA_DH
    c_kv = c_q + 6 * NSA_KV_GROUPS * NSA_DH
    c_g = c_kv + 3 * NSA_HEADS
    assert c_g == w_in0.shape[1]
    n_misc = 2 * GDN_HEADS + 3 * NSA_HEADS
    w = jnp.concatenate([w_in0[:, :c_z], w_in0[:, c_ab:c_kv], w_in0[:, c_z:c_ab], w_in0[:, c_kv:c_g],
                         jnp.zeros((w_in0.shape[0], MISC_W - n_misc), w_in0.dtype)], axis=1)
    widths = (GDN_CONV_CH, GDN_HEADS * GDN_DV, NSA_HEADS * NSA_DH, 4 * NSA_KV_GROUPS * NSA_DH,
              2 * NSA_KV_GROUPS * NSA_DH, MISC_W)
    splits, s = [], 0
    for wd in widths:
        splits.append((s, s + wd))
        s += wd
    return w, tuple(splits)


def kernel(x_prompt, x_sample, c_prompt, c_sample, cache_nsa_kv, cache_nsa_win, state_gdn, state_gdn_conv, state_lru, state_lru_conv, page_table, rel_bias, w_ada, b_ada, norm_mix, norm_ffn, norm_final, w_in0, gdn_conv_w, gdn_a_log, gdn_dt_bias, gdn_norm_w, w_out0, ffn_w_gate, ffn_w_up, ffn_w_down, w_in1, lru_conv_w, lru_conv_b, lru_wa, lru_ba, lru_wx, lru_bx, lru_lambda, w_out1, moe_router, moe_w1, moe_w3, moe_w2):
    bsz, seq, d = x_prompt.shape
    db = x_sample.shape[0]
    assert x_sample.shape[1] == 1
    dh = NSA_DH

    n_c = bsz + db
    n_c_pad = -(-n_c // 8) * 8
    c_all = jnp.concatenate([c_prompt, c_sample, jnp.zeros((n_c_pad - n_c, d), f32)], axis=0)
    mods = _adaln(c_all, w_ada, b_ada).reshape(2, n_c_pad, N_MOD, d)
    mod_p = [[mods[l, :bsz, k].reshape(bsz, 1, d) for k in range(N_MOD)] for l in range(2)]
    mod_s = [[mods[l, bsz:n_c, k].reshape(1, db, d) for k in range(N_MOD)] for l in range(2)]

    w0, splits0 = _w_in0_layout(w_in0)
    splits1 = ((0, RNN_WIDTH), (RNN_WIDTH, 2 * RNN_WIDTH))
    router = jnp.concatenate([moe_router, jnp.zeros((d, MISC_W - N_EXPERTS), f32)], axis=1)
    bf = lambda w: w.astype(bf16)

    tm = 512
    xp = x_prompt
    qkv, z, q_b, kv4, kvwin, misc = _mod_matmul(xp, norm_mix[0], mod_p[0][0], mod_p[0][1], bf(w0), splits0,
                                                tm=tm, prec=1)
    o_a, p_gdn = _gdn(qkv, z, misc, gdn_conv_w, jnp.zeros((bsz, GDN_CONV - 1, GDN_CONV_CH), f32), gdn_a_log,
                      gdn_dt_bias, gdn_norm_w, jnp.zeros((bsz, GDN_HEADS, GDN_DK, GDN_DV), f32), tb=256, n_valid=seq,
                      out_dtype=bf16)
    p_gdn_conv = qkv[:, seq - (GDN_CONV - 1):]
    o_b = _nsa_prompt(q_b, misc, kv4, kvwin, rel_bias)
    p_nsa_kv = kv4.reshape(bsz, seq, 4, NSA_KV_GROUPS, dh)
    keep = min(NSA_WINDOW, seq)
    p_nsa_win = kvwin[:, seq - keep:].reshape(bsz, keep, 2, NSA_KV_GROUPS, dh)
    xp = _proj_residual([o_a, o_b], bf(w_out0), xp, mod_p[0][2], tm=tm, prec=1)
    xp = _ffn(xp, norm_ffn[0], mod_p[0][3], mod_p[0][4], mod_p[0][5], bf(ffn_w_gate), bf(ffn_w_up), bf(ffn_w_down),
              tm=tm, tf=FFN_DIM // 2, prec=1)
    gate_br, rec_br = _mod_matmul(xp, norm_mix[1], mod_p[1][0], mod_p[1][1], bf(w_in1), splits1, tm=tm, prec=1)
    y_in, p_lru = _lru_prompt(rec_br, gate_br, lru_conv_w, lru_conv_b, lru_wa, lru_wx, lru_ba, lru_bx, lru_lambda,
                              tb=256)
    p_lru_conv = rec_br[:, seq - (RNN_CONV - 1):]
    xp = _proj_residual([y_in], bf(w_out1), xp, mod_p[1][2], tm=tm, prec=1)
    logit_p, h_p = _mod_matmul(xp, norm_ffn[1], mod_p[1][3], mod_p[1][4], router, ((0, MISC_W),), tm=tm, prec=3,
                               emit_h=f32)

    xs = x_sample.reshape(1, db, d)
    qkv_s, z_s, q_s, kv4_s, kvwin_s, misc_s = _mod_matmul(xs, norm_mix[0], mod_s[0][0], mod_s[0][1], w0, splits0,
                                                          tm=db, prec=3)
    c = GDN_CHUNK
    tpad = lambda a: jnp.pad(a.reshape(db, 1, a.shape[-1]), ((0, 0), (0, c - 1), (0, 0)))
    o_a_s, s_gdn = _gdn(tpad(qkv_s), tpad(z_s), tpad(misc_s), gdn_conv_w, state_gdn_conv, gdn_a_log, gdn_dt_bias,
                        gdn_norm_w, state_gdn, tb=c, n_valid=1, out_dtype=f32)
    o_a_s = o_a_s[:, 0].reshape(1, db, GDN_HEADS * GDN_DV)
    s_gdn_conv = jnp.concatenate([state_gdn_conv[:, 1:], qkv_s.reshape(db, 1, GDN_CONV_CH)], axis=1)
    gates_s = jax.nn.sigmoid(misc_s[0, :, 2 * GDN_HEADS:2 * GDN_HEADS + 3 * NSA_HEADS]).reshape(db, NSA_HEADS, 3)
    o_b_s = _nsa_sample(q_s[0], gates_s, kv4_s[0], kvwin_s[0], cache_nsa_kv, cache_nsa_win, page_table, rel_bias)
    s_nsa_kv = kv4_s.reshape(db, 1, 4, NSA_KV_GROUPS, dh)
    s_nsa_win = jnp.concatenate([cache_nsa_win[:, 1:],
                                 kvwin_s.reshape(db, 1, 2, NSA_KV_GROUPS, dh).astype(cache_nsa_win.dtype)], axis=1)
    xs = _proj_residual([o_a_s, o_b_s.reshape(1, db, NSA_HEADS * dh)], w_out0, xs, mod_s[0][2], tm=db, prec=3)
    xs = _ffn(xs, norm_ffn[0], mod_s[0][3], mod_s[0][4], mod_s[0][5], ffn_w_gate, ffn_w_up, ffn_w_down,
              tm=db, tf=256, prec=3)
    gate_s, rec_s = _mod_matmul(xs, norm_mix[1], mod_s[1][0], mod_s[1][1], w_in1, splits1, tm=db, prec=3)
    y_in_s, s_lru = _lru_sample(rec_s[0], gate_s[0], state_lru_conv, state_lru, lru_conv_w, lru_conv_b, lru_wa, lru_wx,
                                lru_ba, lru_bx, lru_lambda)
    s_lru_conv = jnp.concatenate([state_lru_conv[:, 1:], rec_s.reshape(db, 1, RNN_WIDTH)], axis=1)
    xs = _proj_residual([y_in_s.reshape(1, db, RNN_WIDTH)], w_out1, xs, mod_s[1][2], tm=db, prec=3)
    logit_s, h_s = _mod_matmul(xs, norm_ffn[1], mod_s[1][3], mod_s[1][4], router, ((0, MISC_W),), tm=db, prec=3,
                               emit_h=f32)

    n_p = bsz * seq
    n_tok = n_p + db
    logits = jnp.concatenate([logit_p.reshape(n_p, MISC_W), logit_s.reshape(db, MISC_W)], axis=0)
    h_all = jnp.concatenate([h_p.reshape(n_p, d), h_s.reshape(db, d)], axis=0)
    route, gw, cnt = _top2(logits, tm=max(t for t in range(8, 1025, 8) if n_tok % t == 0))
    dest, row_tok, blk_e, n_used, n_valid = _moe_dispatch(route[:, :TOP_K], route[:, TOP_K:2 * TOP_K],
                                                          cnt[0, :N_EXPERTS], n_tok)
    yb = _moe_experts(h_all[row_tok], blk_e, n_used, n_valid, moe_w1, moe_w3, moe_w2, tm=MOE_TM, tf=MOE_TF)
    y0, y1 = yb[dest[:, 0]], yb[dest[:, 1]]
    y_prompt = _moe_combine(xp, mod_p[1][5], y0, y1, gw, norm_final, tm=tm, row0=0)
    y_sample = _moe_combine(xs, mod_s[1][5], y0, y1, gw, norm_final, tm=db, row0=n_p)

    return (y_prompt, y_sample.reshape(db, 1, d),
            p_nsa_kv, p_nsa_win, p_gdn, p_gdn_conv, p_lru.reshape(bsz, RNN_WIDTH), p_lru_conv,
            s_nsa_kv, s_nsa_win, s_gdn, s_gdn_conv, s_lru, s_lru_conv)
```

```python
import functools
import math

import numpy as np
import jax
import jax.numpy as jnp
from jax import lax
from jax.experimental import pallas as pl
from jax.experimental.pallas import tpu as pltpu

f32 = jnp.float32
bf16 = jnp.bfloat16
i32 = jnp.int32

D_MODEL = 1024
EPS = 1e-6
N_MOD = 6
GDN_HEADS = 4
GDN_DK = 128
GDN_DV = 128
GDN_CONV = 4
GDN_CHUNK = 64
GDN_CONV_CH = GDN_HEADS * (2 * GDN_DK + GDN_DV)
NSA_HEADS = 8
NSA_KV_GROUPS = 2
NSA_HPG = NSA_HEADS // NSA_KV_GROUPS
NSA_DH = 64
NSA_BLOCK = 64
NSA_TOPN = 16
NSA_LOCAL = 2
NSA_WINDOW = 512
NSA_QBLOCK = 128
REL_BUCKETS = 32
REL_MAX_DIST = 2048
RNN_WIDTH = D_MODEL
RNN_BLOCKS = 8
RNN_BW = RNN_WIDTH // RNN_BLOCKS
RNN_CONV = 4
RG_C = 8.0
FFN_DIM = 2816
N_EXPERTS = 8
TOP_K = 2
EXPERT_DIM = 3584

NEG = -1e30
PAIR = 2 * NSA_BLOCK
NEAR_PAIRS = 13
MISC_W = 128
VMEM_LIMIT = 56 * 1024 * 1024
PROMPT_TM = 512
SCAN_TB = 256
FFN_TF_PROMPT = FFN_DIM // 2
FFN_TF_SAMPLE = 256
MEANS_TB = 512


def _cparams(sem, vmem=VMEM_LIMIT):
    return pltpu.CompilerParams(dimension_semantics=sem, vmem_limit_bytes=vmem)


def _mm(a, b):
    return jnp.dot(a.astype(bf16), b.astype(bf16), preferred_element_type=f32)


def _mm_nt(a, b):
    return lax.dot_general(a.astype(bf16), b.astype(bf16), (((1,), (1,)), ((), ())),
                           preferred_element_type=f32)


def _split2(a):
    hi = a.astype(bf16)
    lo = (a - hi.astype(f32)).astype(bf16)
    return hi, lo


def _mm3(a, b):
    ah, al = _split2(a)
    bh, bl = _split2(b)
    return (jnp.dot(ah, bh, preferred_element_type=f32) + jnp.dot(ah, bl, preferred_element_type=f32)
            + jnp.dot(al, bh, preferred_element_type=f32))


def _mm3_nt(a, b):
    ah, al = _split2(a)
    bh, bl = _split2(b)
    dn = (((1,), (1,)), ((), ()))
    return (lax.dot_general(ah, bh, dn, preferred_element_type=f32)
            + lax.dot_general(ah, bl, dn, preferred_element_type=f32)
            + lax.dot_general(al, bh, dn, preferred_element_type=f32))


def _mm_01(m01, a):
    hi = a.astype(bf16)
    r1 = a - hi.astype(f32)
    mid = r1.astype(bf16)
    lo = (r1 - mid.astype(f32)).astype(bf16)
    return (jnp.dot(m01, hi, preferred_element_type=f32) + jnp.dot(m01, mid, preferred_element_type=f32)
            + jnp.dot(m01, lo, preferred_element_type=f32))


def _dotp(a, b, prec):
    return _mm3(a, b) if prec == 3 else _mm(a, b)


def _silu(x):
    return x * jax.nn.sigmoid(x)


def _softplus(x):
    return jnp.maximum(x, 0.0) + jnp.log1p(jnp.exp(-jnp.abs(x)))


def _modulate(x, gain, shift, scale):
    r = lax.rsqrt(jnp.mean(x * x, axis=-1, keepdims=True) + EPS)
    return x * r * gain * (1.0 + scale) + shift


def _ada_kernel(c_ref, w_ref, b_ref, o_ref):
    o_ref[0] = _mm3(_silu(c_ref[...]), w_ref[0]) + b_ref[0]


def _adaln(c_all, w_ada, b_ada):
    rows = c_all.shape[0]
    depth, d, n = w_ada.shape
    tn = 1536
    return pl.pallas_call(
        _ada_kernel,
        grid=(depth, n // tn),
        in_specs=[pl.BlockSpec((rows, d), lambda l, j: (0, 0)),
                  pl.BlockSpec((1, d, tn), lambda l, j: (l, 0, j)),
                  pl.BlockSpec((1, 1, tn), lambda l, j: (l, 0, j))],
        out_specs=pl.BlockSpec((1, rows, tn), lambda l, j: (l, 0, j)),
        out_shape=jax.ShapeDtypeStruct((depth, rows, n), f32),
        compiler_params=_cparams(("arbitrary", "arbitrary")),
        name="adaln",
    )(c_all, w_ada, b_ada.reshape(depth, 1, n))


def _mod_spec(mod, tm):
    r = mod.shape[1]
    if r == 1:
        return pl.BlockSpec((1, 1, mod.shape[2]), lambda b, i: (b, 0, 0))
    return pl.BlockSpec((1, tm, mod.shape[2]), lambda b, i: (b, i, 0))


def _modmm_kernel(x_ref, gain_ref, shift_ref, scale_ref, w_ref, *o_refs, splits, prec, emit_h):
    h = _modulate(x_ref[0], gain_ref[...], shift_ref[0], scale_ref[0])
    if emit_h:
        o_refs[-1][0] = h.astype(o_refs[-1].dtype)
    hh = _split2(h) if prec == 3 else h.astype(bf16)
    for o_ref, (a, b) in zip(o_refs, splits):
        w = w_ref[:, a:b]
        if prec == 3:
            wh, wl = _split2(w)
            acc = (jnp.dot(hh[0], wh, preferred_element_type=f32) + jnp.dot(hh[0], wl, preferred_element_type=f32)
                   + jnp.dot(hh[1], wh, preferred_element_type=f32))
        else:
            acc = jnp.dot(hh, w, preferred_element_type=f32)
        o_ref[0] = acc


def _mod_matmul(x, gain, shift, scale, w, splits, *, tm, prec, emit_h=None):
    bsz, t, d = x.shape
    out_shape = [jax.ShapeDtypeStruct((bsz, t, b - a), f32) for a, b in splits]
    out_specs = [pl.BlockSpec((1, tm, b - a), lambda bi, i: (bi, i, 0)) for a, b in splits]
    if emit_h is not None:
        out_shape.append(jax.ShapeDtypeStruct((bsz, t, d), emit_h))
        out_specs.append(pl.BlockSpec((1, tm, d), lambda bi, i: (bi, i, 0)))
    return pl.pallas_call(
        functools.partial(_modmm_kernel, splits=tuple(splits), prec=prec, emit_h=emit_h is not None),
        grid=(bsz, t // tm),
        in_specs=[pl.BlockSpec((1, tm, d), lambda bi, i: (bi, i, 0)),
                  pl.BlockSpec((1, d), lambda bi, i: (0, 0)),
                  _mod_spec(shift, tm), _mod_spec(scale, tm),
                  pl.BlockSpec(w.shape, lambda bi, i: (0, 0))],
        out_specs=out_specs,
        out_shape=out_shape,
        compiler_params=_cparams(("parallel", "parallel")),
        name="mod_matmul",
    )(x, gain.reshape(1, d), shift, scale, w)


def _projres_kernel(*refs, n_lhs, ksplits, prec, final_norm):
    lhs = refs[:n_lhs]
    w_ref, x_ref, gate_ref = refs[n_lhs:n_lhs + 3]
    o_ref = refs[-1]
    acc = None
    for l_ref, (a, b) in zip(lhs, ksplits):
        part = _dotp(l_ref[0], w_ref[a:b, :], prec)
        acc = part if acc is None else acc + part
    y = x_ref[0] + gate_ref[0] * acc
    if final_norm:
        nw_ref = refs[n_lhs + 3]
        y = y * lax.rsqrt(jnp.mean(y * y, axis=-1, keepdims=True) + EPS) * nw_ref[...]
    o_ref[0] = y


def _proj_residual(lhs_list, w, x, gate, *, tm, prec, norm_w=None):
    bsz, t, d = x.shape
    ksplits, k0 = [], 0
    for l in lhs_list:
        ksplits.append((k0, k0 + l.shape[-1]))
        k0 += l.shape[-1]
    in_specs = [pl.BlockSpec((1, tm, l.shape[-1]), lambda bi, i: (bi, i, 0)) for l in lhs_list]
    in_specs += [pl.BlockSpec(w.shape, lambda bi, i: (0, 0)),
                 pl.BlockSpec((1, tm, d), lambda bi, i: (bi, i, 0)),
                 _mod_spec(gate, tm)]
    args = list(lhs_list) + [w, x, gate]
    if norm_w is not None:
        in_specs.append(pl.BlockSpec((1, d), lambda bi, i: (0, 0)))
        args.append(norm_w.reshape(1, d))
    return pl.pallas_call(
        functools.partial(_projres_kernel, n_lhs=len(lhs_list), ksplits=tuple(ksplits), prec=prec,
                          final_norm=norm_w is not None),
        grid=(bsz, t // tm),
        in_specs=in_specs,
        out_specs=pl.BlockSpec((1, tm, d), lambda bi, i: (bi, i, 0)),
        out_shape=jax.ShapeDtypeStruct((bsz, t, d), f32),
        compiler_params=_cparams(("parallel", "parallel")),
        name="proj_residual",
    )(*args)


def _ffn_kernel(x_ref, gain_ref, shift_ref, scale_ref, gate_ref, wg_ref, wu_ref, wd_ref, o_ref,
                h_scr, acc_scr, *, prec):
    f = pl.program_id(2)

    @pl.when(f == 0)
    def _():
        h_scr[...] = _modulate(x_ref[0], gain_ref[...], shift_ref[0], scale_ref[0]).astype(h_scr.dtype)
        acc_scr[...] = jnp.zeros_like(acc_scr)

    h = h_scr[...]
    act = _silu(_dotp(h, wg_ref[...], prec)) * _dotp(h, wu_ref[...], prec)
    acc_scr[...] += _dotp(act, wd_ref[...], prec)

    @pl.when(f == pl.num_programs(2) - 1)
    def _():
        o_ref[0] = x_ref[0] + gate_ref[0] * acc_scr[...]


def _ffn(x, gain, shift, scale, gate, wg, wu, wd, *, tm, tf, prec):
    bsz, t, d = x.shape
    fdim = wg.shape[1]
    mod_specs = []
    for mod in (shift, scale, gate):
        if mod.shape[1] == 1:
            mod_specs.append(pl.BlockSpec((1, 1, d), lambda b, i, f: (b, 0, 0)))
        else:
            mod_specs.append(pl.BlockSpec((1, tm, d), lambda b, i, f: (b, i, 0)))
    return pl.pallas_call(
        functools.partial(_ffn_kernel, prec=prec),
        grid=(bsz, t // tm, fdim // tf),
        in_specs=[pl.BlockSpec((1, tm, d), lambda b, i, f: (b, i, 0)),
                  pl.BlockSpec((1, d), lambda b, i, f: (0, 0)),
                  mod_specs[0], mod_specs[1], mod_specs[2],
                  pl.BlockSpec((d, tf), lambda b, i, f: (0, f)),
                  pl.BlockSpec((d, tf), lambda b, i, f: (0, f)),
                  pl.BlockSpec((tf, d), lambda b, i, f: (f, 0))],
        out_specs=pl.BlockSpec((1, tm, d), lambda b, i, f: (b, i, 0)),
        out_shape=jax.ShapeDtypeStruct((bsz, t, d), f32),
        scratch_shapes=[pltpu.VMEM((tm, d), f32 if prec == 3 else bf16), pltpu.VMEM((tm, d), f32)],
        compiler_params=_cparams(("parallel", "parallel", "arbitrary")),
        name="ffn",
    )(x, gain.reshape(1, d), shift, scale, gate, wg, wu, wd)


def _gdn_kernel(qkv_ref, z_ref, misc_ref, cw_ref, cinit_ref, hp_ref, nw_ref, s0_ref,
                o_ref, sfin_ref, xc_scr, s_scr, *, tb, n_valid):
    j = pl.program_id(1)
    c = GDN_CHUNK
    nh = GDN_HEADS

    @pl.when(j == 0)
    def _():
        xc_scr[0:8, :] = cinit_ref[0]
        s_scr[...] = s0_ref[0]

    x = qkv_ref[0]
    xc_scr[8:8 + tb, :] = x
    y = xc_scr[5:5 + tb, :] * cw_ref[0:1, :]
    y = y + xc_scr[6:6 + tb, :] * cw_ref[1:2, :]
    y = y + xc_scr[7:7 + tb, :] * cw_ref[2:3, :]
    y = y + x * cw_ref[3:4, :]
    xc_scr[0:8, :] = xc_scr[tb:tb + 8, :]
    y = _silu(y)

    misc = misc_ref[0]
    row = lax.broadcasted_iota(i32, (tb, MISC_W), 0) + j * tb
    live = row < n_valid
    log_a = jnp.where(live, hp_ref[0:1, :] * _softplus(misc + hp_ref[1:2, :]), 0.0)
    beta = jnp.where(live, jax.nn.sigmoid(misc), 0.0)

    r_i = lax.broadcasted_iota(i32, (tb, tb), 0)
    c_i = lax.broadcasted_iota(i32, (tb, tb), 1)
    sh = int(math.log2(c))
    ltri = (((r_i >> sh) == (c_i >> sh)) & (c_i <= r_i)).astype(bf16)
    g = _mm_01(ltri, log_a)
    g_t = g.T

    ri = lax.broadcasted_iota(i32, (c, c), 0)
    ci = lax.broadcasted_iota(i32, (c, c), 1)
    incl = ci <= ri
    strict = ci < ri
    eye = (ci == ri).astype(f32)
    quad = []
    for lvl in range(int(math.log2(c))):
        quad.append(((ri >> (lvl + 1)) == (ci >> (lvl + 1))) & (((ri >> lvl) & 1) == 1) & (((ci >> lvl) & 1) == 0))

    nchunk = tb // c
    units = [(n, h) for n in range(nchunk) for h in range(nh)]

    def stack(fn):
        return jnp.stack([fn(n * c, h) for n, h in units], axis=0)

    def bmm(eq, a_, b_):
        return jnp.einsum(eq, a_.astype(bf16), b_.astype(bf16), preferred_element_type=f32)

    q = stack(lambda r0, h: y[r0:r0 + c, h * GDN_DK:(h + 1) * GDN_DK])
    k = stack(lambda r0, h: y[r0:r0 + c, (nh + h) * GDN_DK:(nh + h + 1) * GDN_DK])
    v = stack(lambda r0, h: y[r0:r0 + c, 2 * nh * GDN_DK + h * GDN_DV:2 * nh * GDN_DK + (h + 1) * GDN_DV])
    q = q * lax.rsqrt(jnp.sum(q * q, axis=-1, keepdims=True) + EPS) * (GDN_DK ** -0.5)
    k = k * lax.rsqrt(jnp.sum(k * k, axis=-1, keepdims=True) + EPS)
    g_col = stack(lambda r0, h: g[r0:r0 + c, h:h + 1])
    g_row = stack(lambda r0, h: g_t[h:h + 1, r0:r0 + c])
    b_col = stack(lambda r0, h: beta[r0:r0 + c, nh + h:nh + h + 1])
    gam = jnp.where(incl, jnp.exp(jnp.where(incl, g_col - g_row, 0.0)), 0.0)
    kk = bmm('uid,ujd->uij', k, k)
    a = jnp.where(strict, b_col * gam * kk, 0.0)
    p = eye - jnp.where(quad[0], a, 0.0)
    for lvl in range(1, len(quad)):
        m = bmm('uij,ujk->uik', jnp.where(quad[lvl], a, 0.0), p)
        p = p - bmm('uij,ujk->uik', p, m)
    e_g = jnp.exp(g_col)
    sol = bmm('uij,ujd->uid', p, jnp.concatenate([b_col * v, (b_col * e_g) * k], axis=-1))
    vb, w = sol[:, :, :GDN_DV], sol[:, :, GDN_DV:]
    aqk = bmm('uid,ujd->uij', q, k) * gam
    qg = q * e_g
    g_last = g_col[:, c - 1:c, :]
    kd_t = jnp.swapaxes(k * jnp.exp(g_last - g_col), 1, 2)
    gc = jnp.exp(g_last)

    s = s_scr[...]
    for n in range(nchunk):
        sl = slice(n * nh, (n + 1) * nh)
        u = vb[sl] - bmm('hcd,hde->hce', w[sl], s)
        o = bmm('hcd,hde->hce', qg[sl], s) + bmm('hij,hje->hie', aqk[sl], u)
        s = gc[sl] * s + bmm('hdc,hce->hde', kd_t[sl], u)
        o = o * lax.rsqrt(jnp.mean(o * o, axis=-1, keepdims=True) + EPS) * nw_ref[...]
        for h in range(nh):
            zs = z_ref[0, n * c:(n + 1) * c, h * GDN_DV:(h + 1) * GDN_DV]
            o_ref[0, n * c:(n + 1) * c, h * GDN_DV:(h + 1) * GDN_DV] = (o[h] * _silu(zs)).astype(o_ref.dtype)
    s_scr[...] = s

    @pl.when(j == pl.num_programs(1) - 1)
    def _():
        sfin_ref[0] = s_scr[...]


def _gdn(qkv_raw, z, misc, conv_w, conv_buf, a_log, dt_bias, norm_w, s0, *, tb, n_valid, out_dtype):
    bsz, tp, ch = qkv_raw.shape
    cinit = jnp.concatenate([jnp.zeros((bsz, 5, ch), f32), conv_buf.astype(f32)], axis=1)
    hp = jnp.zeros((8, MISC_W), f32)
    hp = hp.at[0, :GDN_HEADS].set(-jnp.exp(a_log.astype(f32))).at[1, :GDN_HEADS].set(dt_bias.astype(f32))
    zw = GDN_HEADS * GDN_DV
    return pl.pallas_call(
        functools.partial(_gdn_kernel, tb=tb, n_valid=n_valid),
        grid=(bsz, tp // tb),
        in_specs=[pl.BlockSpec((1, tb, ch), lambda b, j: (b, j, 0)),
                  pl.BlockSpec((1, tb, zw), lambda b, j: (b, j, 0)),
                  pl.BlockSpec((1, tb, MISC_W), lambda b, j: (b, j, 0)),
                  pl.BlockSpec((GDN_CONV, ch), lambda b, j: (0, 0)),
                  pl.BlockSpec((1, 8, ch), lambda b, j: (b, 0, 0)),
                  pl.BlockSpec((8, MISC_W), lambda b, j: (0, 0)),
                  pl.BlockSpec((1, GDN_DV), lambda b, j: (0, 0)),
                  pl.BlockSpec((1, GDN_HEADS, GDN_DK, GDN_DV), lambda b, j: (b, 0, 0, 0))],
        out_specs=[pl.BlockSpec((1, tb, zw), lambda b, j: (b, j, 0)),
                   pl.BlockSpec((1, GDN_HEADS, GDN_DK, GDN_DV), lambda b, j: (b, 0, 0, 0))],
        out_shape=[jax.ShapeDtypeStruct((bsz, tp, zw), out_dtype),
                   jax.ShapeDtypeStruct((bsz, GDN_HEADS, GDN_DK, GDN_DV), f32)],
        scratch_shapes=[pltpu.VMEM((tb + 8, ch), f32), pltpu.VMEM((GDN_HEADS, GDN_DK, GDN_DV), f32)],
        compiler_params=_cparams(("parallel", "arbitrary")),
        name="gdn",
    )(qkv_raw, z, misc, conv_w.astype(f32), cinit, hp, norm_w.reshape(1, GDN_DV).astype(f32), s0.astype(f32))


def _bucket_thresholds():
    exact = REL_BUCKETS // 2
    d = np.arange(0, 4 * REL_MAX_DIST, dtype=np.int64)

    def buckets(ft):
        nf = np.maximum(d, exact).astype(ft)
        large = exact + (np.log(nf / ft(exact)) / ft(math.log(REL_MAX_DIST / exact)) * ft(REL_BUCKETS - exact)).astype(np.int32)
        return np.where(d < exact, d, np.minimum(large, REL_BUCKETS - 1))

    b64, b32 = buckets(np.float64), buckets(np.float32)
    assert np.array_equal(b64, b32) and b64[-1] == REL_BUCKETS - 1 and np.all(np.diff(b64) >= 0)
    return [int(np.argmax(b64 >= k)) for k in range(1, REL_BUCKETS)]


_BUCKET_THR = _bucket_thresholds()
FAR_DIST = _BUCKET_THR[-1]


def _bias_kernel(rb_ref, d_ref, o_ref, *, slab, group_heads):
    rows = d_ref.shape[-2]
    for s in range(rows // slab):
        d = d_ref[0, 0, s * slab:(s + 1) * slab, :]
        h = pl.program_id(0) * group_heads + s
        val = jnp.full(d.shape, rb_ref[0, h], f32)
        for k in range(1, REL_BUCKETS):
            val = jnp.where(d >= _BUCKET_THR[k - 1], rb_ref[k, h], val)
        o_ref[0, 0, s * slab:(s + 1) * slab, :] = jnp.where(d < 0, NEG, val)


def _bias_table(rel_bias, dist, *, slab):
    g, nt, rows, cols = dist.shape
    return pl.pallas_call(
        functools.partial(_bias_kernel, slab=slab, group_heads=rows // slab),
        grid=(g, nt),
        in_specs=[pl.BlockSpec(memory_space=pltpu.SMEM),
                  pl.BlockSpec((1, 1, rows, cols), lambda a, b: (a, b, 0, 0))],
        out_specs=pl.BlockSpec((1, 1, rows, cols), lambda a, b: (a, b, 0, 0)),
        out_shape=jax.ShapeDtypeStruct(dist.shape, f32),
        compiler_params=_cparams(("arbitrary", "arbitrary")),
        name="bias_table",
    )(rel_bias.astype(f32), dist)


def _blockmean_kernel(x_ref, o_ref):
    x = x_ref[0]
    nb = x.shape[0] // NSA_BLOCK
    o_ref[0] = jnp.sum(x.reshape(nb, NSA_BLOCK, x.shape[1]), axis=1) * (1.0 / NSA_BLOCK)


def _block_means(kv4, *, tb):
    bsz, t, _ = kv4.shape
    w = 2 * NSA_KV_GROUPS * NSA_DH
    return pl.pallas_call(
        _blockmean_kernel,
        grid=(bsz, t // tb),
        in_specs=[pl.BlockSpec((1, tb, w), lambda b, i: (b, i, 0))],
        out_specs=pl.BlockSpec((1, tb // NSA_BLOCK, w), lambda b, i: (b, i, 0)),
        out_shape=jax.ShapeDtypeStruct((bsz, t // NSA_BLOCK, w), f32),
        compiler_params=_cparams(("parallel", "parallel")),
        name="nsa_block_means",
    )(kv4)


LOG2E = 1.4426950408889634
NSA_KEY_TILE = 4 * PAIR


def _nsa_prompt_kernel(q_ref, misc_ref, kcvc_ref, kaug_ref, vgt_ref, kw_ref, vw_ref,
                       tabs_ref, tabw_ref, tabc_ref, c31_ref, o_ref, qaug_scr, s_scr):
    g = pl.program_id(1)
    i = pl.program_id(2)
    qb, dh, hpg = NSA_QBLOCK, NSA_DH, NSA_HPG
    is_g0 = g == 0
    nn = (((1,), (1,)), ((), ()))

    def ghalf(a):
        return jnp.where(is_g0, a[:, :dh], a[:, dh:])

    qt = q_ref[0] * (dh ** -0.5)
    qh = [qt[:, h * dh:(h + 1) * dh] for h in range(hpg)]
    kcvc = kcvc_ref[0]
    nblk = kcvc.shape[0]
    kc = ghalf(kcvc[:, :2 * dh])
    vc = ghalf(kcvc[:, 2 * dh:])

    rows = hpg * qb
    qs = jnp.concatenate(qh, axis=0)
    zero = jnp.zeros_like(qs)
    q2f = jnp.concatenate([jnp.where(is_g0, qs, zero), jnp.where(is_g0, zero, qs)], axis=1) * LOG2E
    q2 = q2f.astype(bf16)

    blk4 = lax.broadcasted_iota(i32, (rows, nblk), 1)
    qrow4 = lax.broadcasted_iota(i32, (rows, nblk), 0) & (qb - 1)
    first_tab = 2 * i - 2 * NEAR_PAIRS + 1
    shift = (first_tab + 4 * nblk) % nblk
    readable = blk4 * NSA_BLOCK + (NSA_BLOCK - 1) <= i * qb + qrow4
    bias = jnp.where(blk4 < first_tab, c31_ref[0], pltpu.roll(tabc_ref[0], shift, 1))
    s_c = jnp.where(readable, _mm3_nt(qs, kc) + bias, NEG)
    m_c = jnp.max(s_c, axis=-1, keepdims=True)
    p_c = jnp.where(readable, jnp.exp(s_c - m_c), 0.0)
    p_c = p_c / jnp.maximum(jnp.sum(p_c, axis=-1, keepdims=True), 1e-30)
    o_cmp = _mm(p_c, vc)
    imp = p_c[0:qb]
    for h in range(1, hpg):
        imp = imp + p_c[h * qb:(h + 1) * qb]

    span = NSA_WINDOW + qb
    ws = pl.multiple_of(i * qb, qb)
    s_w = lax.dot_general(q2, kw_ref[0, pl.ds(ws, span), :], nn, preferred_element_type=f32) + tabw_ref[0]
    in_seq = lax.broadcasted_iota(i32, (rows, span), 1) + (i * qb - NSA_WINDOW) >= 0
    s_w = jnp.where(in_seq, s_w, NEG)
    m_w = jnp.max(s_w, axis=-1, keepdims=True)
    p_w = jnp.exp2(s_w - m_w)
    l_w = jnp.sum(p_w, axis=-1, keepdims=True)
    o_win = ghalf(jnp.dot(p_w.astype(bf16), vw_ref[0, pl.ds(ws, span), :], preferred_element_type=f32))
    o_win = o_win / jnp.maximum(l_w, 1e-30)

    blk = lax.broadcasted_iota(i32, (nblk, qb), 0)
    cur = (i * qb + lax.broadcasted_iota(i32, (nblk, qb), 1)) >> int(math.log2(NSA_BLOCK))
    valid = blk <= cur
    forced = valid & ((blk == 0) | (blk > cur - NSA_LOCAL))
    imp_t = jnp.where(forced, -jnp.inf, jnp.where(valid, imp.T, -1.0))
    picked = forced
    for _ in range(min(NSA_TOPN, nblk) - 1 - NSA_LOCAL):
        mx = jnp.max(imp_t, axis=0, keepdims=True)
        first = jnp.min(jnp.where(imp_t == mx, blk, nblk), axis=0, keepdims=True)
        hit = blk == first
        picked = picked | hit
        imp_t = jnp.where(hit, -jnp.inf, imp_t)
    msel_t = jnp.where(picked & valid, 0.0, NEG).astype(bf16)

    qaug_scr[...] = jnp.concatenate([q2f.T.astype(bf16), jnp.concatenate([msel_t] * hpg, axis=1)], axis=0)
    tk = s_scr.shape[0]
    ppt = tk // PAIR
    t_d = i // ppt

    def scores(t):
        ks = pl.multiple_of(t * tk, tk)
        s = jnp.dot(kaug_ref[0, pl.ds(ks, tk), :], qaug_scr[...], preferred_element_type=f32)
        tabs = [tabs_ref[0, jnp.clip(i - ppt * t - j, 0, NEAR_PAIRS)] for j in range(ppt)]
        return s + jnp.concatenate(tabs, axis=0)

    s_scr[...] = scores(t_d)

    def tile(k_it, carry):
        m, acc = carry
        t = t_d - k_it
        s = s_scr[...]
        s_next = scores(jnp.maximum(t - 1, 0))
        m_new = jnp.maximum(m, jnp.max(s, axis=0, keepdims=True))
        alpha = jnp.exp2(m - m_new)
        p = jnp.exp2(s - m_new).astype(bf16)
        ks = pl.multiple_of(t * tk, tk)
        acc = alpha * acc + jnp.dot(vgt_ref[0, 0, :, pl.ds(ks, tk)], p, preferred_element_type=f32)
        s_scr[...] = s_next
        return m_new, acc

    _, acc_t = lax.fori_loop(0, t_d + 1, tile, (jnp.full((1, rows), NEG, f32), jnp.zeros((2 * dh, rows), f32)))
    o_sel = (acc_t[:dh] / jnp.maximum(acc_t[dh:dh + 1], 1e-30)).T

    gates = jax.nn.sigmoid(misc_ref[0])
    outs = []
    for h in range(hpg):
        r = slice(h * qb, (h + 1) * qb)
        acc = None
        for jb, branch in enumerate((o_cmp[r], o_sel[r], o_win[r])):
            la = 8 + 3 * h + jb
            lb = 8 + 3 * (hpg + h) + jb
            term = jnp.where(is_g0, gates[:, la:la + 1], gates[:, lb:lb + 1]) * branch
            acc = term if acc is None else acc + term
        outs.append(acc)
    o_ref[0] = jnp.concatenate(outs, axis=1).astype(o_ref.dtype)


def _nsa_tables(rel_bias):
    qb, hpg = NSA_QBLOCK, NSA_HPG
    rows = hpg * qb
    q = (np.arange(rows) % qb)[:, None]
    c = np.arange(PAIR)[None, :]
    d_sel = [PAIR * idx + q - c for idx in range(NEAR_PAIRS + 1)]
    assert PAIR * NEAR_PAIRS - (PAIR - 1) >= FAR_DIST
    mm = 2 * NEAR_PAIRS - 1 - c
    d_cmp = np.maximum(np.where(mm >= -1, NSA_BLOCK * mm + q - (NSA_BLOCK - 1), 0), 0)
    assert NSA_BLOCK * (2 * NEAR_PAIRS) - (NSA_BLOCK - 1) >= FAR_DIST
    cw = np.arange(NSA_WINDOW + qb)[None, :]
    d_win = q + NSA_WINDOW - cw
    d_win = np.where((d_win >= 0) & (d_win < NSA_WINDOW), d_win, -1)
    n_win = (NSA_WINDOW + qb) // PAIR
    tiles = d_sel + [d_cmp] + [d_win[:, k * PAIR:(k + 1) * PAIR] for k in range(n_win)]
    dist = np.broadcast_to(np.stack(tiles)[None], (NSA_KV_GROUPS, len(tiles), rows, PAIR))
    tab = _bias_table(rel_bias, jnp.asarray(dist, i32), slab=qb)
    ns = NEAR_PAIRS + 1
    tabs = tab[:, :ns]
    tabc = tab[:, ns]
    tabw = jnp.concatenate([tab[:, ns + 1 + k] for k in range(n_win)], axis=-1)
    c31 = jnp.repeat(rel_bias.astype(f32)[REL_BUCKETS - 1].reshape(NSA_KV_GROUPS, hpg), qb, axis=1)
    return tabs, tabw, tabc, c31.reshape(NSA_KV_GROUPS, rows, 1)


def _nsa_prompt(q_b, misc, kv4, kvwin, rel_bias):
    bsz, t, _ = q_b.shape
    dh, qb = NSA_DH, NSA_QBLOCK
    nblk = t // NSA_BLOCK
    assert nblk == PAIR and t % NSA_KEY_TILE == 0
    kcvc = _block_means(kv4, tb=MEANS_TB)
    onehot = (jnp.arange(t, dtype=i32)[:, None] // NSA_BLOCK == jnp.arange(nblk, dtype=i32)[None, :]).astype(bf16)
    kaug = jnp.concatenate([kv4[:, :, 4 * dh:6 * dh].astype(bf16),
                            jnp.broadcast_to(onehot[None], (bsz, t, nblk))], axis=-1)
    ones = jnp.ones((bsz, t, dh), bf16)
    vgt = jnp.stack([jnp.swapaxes(jnp.concatenate([kv4[:, :, (6 + g) * dh:(7 + g) * dh].astype(bf16), ones],
                                                   axis=-1), 1, 2)
                     for g in range(NSA_KV_GROUPS)], axis=1)
    pad = ((0, 0), (NSA_WINDOW, 0), (0, 0))
    kw = jnp.pad(kvwin[:, :, :2 * dh].astype(bf16), pad)
    vw = jnp.pad(kvwin[:, :, 2 * dh:].astype(bf16), pad)
    tabs, tabw, tabc, c31 = _nsa_tables(rel_bias)
    tabs, tabw = jnp.swapaxes(tabs, 2, 3) * LOG2E, tabw * LOG2E
    rows = NSA_HPG * qb
    gw = NSA_HPG * dh
    span = NSA_WINDOW + qb
    return pl.pallas_call(
        _nsa_prompt_kernel,
        grid=(bsz, NSA_KV_GROUPS, t // qb),
        in_specs=[pl.BlockSpec((1, qb, gw), lambda b, g, i: (b, i, g)),
                  pl.BlockSpec((1, qb, MISC_W), lambda b, g, i: (b, i, 0)),
                  pl.BlockSpec((1, nblk, 4 * dh), lambda b, g, i: (b, 0, 0)),
                  pl.BlockSpec((1, t, 2 * dh + nblk), lambda b, g, i: (b, 0, 0)),
                  pl.BlockSpec((1, 1, 2 * dh, t), lambda b, g, i: (b, g, 0, 0)),
                  pl.BlockSpec((1, t + NSA_WINDOW, 2 * dh), lambda b, g, i: (b, 0, 0)),
                  pl.BlockSpec((1, t + NSA_WINDOW, 2 * dh), lambda b, g, i: (b, 0, 0)),
                  pl.BlockSpec((1, NEAR_PAIRS + 1, PAIR, rows), lambda b, g, i: (g, 0, 0, 0)),
                  pl.BlockSpec((1, rows, span), lambda b, g, i: (g, 0, 0)),
                  pl.BlockSpec((1, rows, PAIR), lambda b, g, i: (g, 0, 0)),
                  pl.BlockSpec((1, rows, 1), lambda b, g, i: (g, 0, 0))],
        out_specs=pl.BlockSpec((1, qb, gw), lambda b, g, i: (b, i, g)),
        out_shape=jax.ShapeDtypeStruct((bsz, t, NSA_HEADS * dh), bf16),
        scratch_shapes=[pltpu.VMEM((2 * dh + nblk, rows), bf16), pltpu.VMEM((NSA_KEY_TILE, rows), f32)],
        compiler_params=_cparams(("parallel", "parallel", "arbitrary")),
        name="nsa_prompt",
    )(q_b, misc, kcvc, kaug, vgt, kw, vw, tabs, tabw, tabc, c31)


MEANS_PAGES = 16


def _nsa_s_means_kernel(pt_ref, *refs):
    del pt_ref
    x_refs, o_ref = refs[:-1], refs[-1]
    p = pl.program_id(1)

    @pl.when(p == 0)
    def _():
        o_ref[...] = jnp.zeros_like(o_ref)

    page = x_refs[0].shape[-1]
    per_page = page // NSA_BLOCK
    nblk = o_ref.shape[-1]
    assert page == nblk
    planes = 2 * NSA_KV_GROUPS
    xs = jnp.concatenate([x_ref[0, j, g] for x_ref in x_refs for j in range(2) for g in range(NSA_KV_GROUPS)],
                         axis=0)
    tok = lax.broadcasted_iota(i32, (page, nblk), 0)
    col = lax.broadcasted_iota(i32, (page, nblk), 1)
    pool = (col == (tok >> int(math.log2(NSA_BLOCK)))).astype(bf16)
    hi, lo = _split2(xs)
    sums = jnp.dot(hi, pool, preferred_element_type=f32) + jnp.dot(lo, pool, preferred_element_type=f32)
    rows = planes * NSA_DH
    acc = None
    for k in range(len(x_refs)):
        shift = (p * len(x_refs) + k) * per_page
        part = pltpu.roll(sums[k * rows:(k + 1) * rows], shift, 1)
        acc = part if acc is None else acc + part
    acc = acc * (1.0 / NSA_BLOCK)
    for pl_i in range(planes):
        o_ref[0, pl_i] = o_ref[0, pl_i] + acc[pl_i * NSA_DH:(pl_i + 1) * NSA_DH]


def _nsa_s_means(cache_t, page_table):
    db, n_pages = page_table.shape
    page = cache_t.shape[-1]
    nblk = n_pages * page // NSA_BLOCK
    kp = MEANS_PAGES
    assert n_pages % kp == 0

    def page_spec(k):
        return pl.BlockSpec((1, 2, NSA_KV_GROUPS, NSA_DH, page),
                            lambda b, p, pt: (pt[b * n_pages + p * kp + k], 0, 0, 0, 0))

    return pl.pallas_call(
        _nsa_s_means_kernel,
        grid_spec=pltpu.PrefetchScalarGridSpec(
            num_scalar_prefetch=1,
            grid=(db, n_pages // kp),
            in_specs=[page_spec(k) for k in range(kp)],
            out_specs=pl.BlockSpec((1, 2 * NSA_KV_GROUPS, NSA_DH, nblk), lambda b, p, pt: (b, 0, 0, 0))),
        out_shape=jax.ShapeDtypeStruct((db, 2 * NSA_KV_GROUPS, NSA_DH, nblk), f32),
        compiler_params=_cparams(("parallel", "arbitrary")),
        name="nsa_sample_means",
    )(page_table.reshape(-1), *([cache_t] * kp))


def _nsa_s_scores_kernel(q_ref, kcvc_ref, win_ref, kvn_ref, tc_ref, tw_ref, b0_ref, oc_ref, ow_ref, sel_ref):
    dh, hpg, ng = NSA_DH, NSA_HPG, NSA_KV_GROUPS
    sb = q_ref.shape[0]
    nblk = kcvc_ref.shape[-1]
    row = lax.broadcasted_iota(i32, (NSA_HEADS, 1), 0)
    in_g = [(row >= g * hpg) & (row < (g + 1) * hpg) for g in range(ng)]
    pairs = [(b, g) for b in range(sb) for g in range(ng)]
    q = [q_ref[b] * (dh ** -0.5) for b in range(sb)]

    s_c = [_mm3(q[b], kcvc_ref[b, g]) + tc_ref[...] for b, g in pairs]
    p_c = []
    for s in s_c:
        p = jnp.exp(s - jnp.max(s, axis=-1, keepdims=True))
        p_c.append(p / jnp.maximum(jnp.sum(p, axis=-1, keepdims=True), 1e-30))
    o_c = [_mm_nt(p_c[k], kcvc_ref[b, ng + g]) for k, (b, g) in enumerate(pairs)]
    imp = jnp.concatenate([jnp.sum(jnp.where(in_g[g], p_c[k], 0.0), axis=0, keepdims=True)
                           for k, (b, g) in enumerate(pairs)], axis=0)
    for b in range(sb):
        oc_ref[b] = jnp.where(in_g[0], o_c[b * ng], o_c[b * ng + 1])

    s_w = [_mm3(q[b], win_ref[b, 0, g]) + tw_ref[...] for b, g in pairs]
    o_w = []
    for k, (b, g) in enumerate(pairs):
        kvn = kvn_ref[b]
        s_n = jnp.sum(q[b] * kvn[:, g * dh:(g + 1) * dh], axis=-1, keepdims=True) + b0_ref[...]
        m_w = jnp.maximum(jnp.max(s_w[k], axis=-1, keepdims=True), s_n)
        p_w = jnp.exp(s_w[k] - m_w)
        p_n = jnp.exp(s_n - m_w)
        l_w = jnp.sum(p_w, axis=-1, keepdims=True) + p_n
        v_n = kvn[:, (ng + g) * dh:(ng + g + 1) * dh]
        o_w.append((_mm_nt(p_w, win_ref[b, 1, g]) + p_n * v_n) / jnp.maximum(l_w, 1e-30))
    for b in range(sb):
        ow_ref[b] = jnp.where(in_g[0], o_w[b * ng], o_w[b * ng + 1])

    lane = lax.broadcasted_iota(i32, imp.shape, 1)
    n_sel = NSA_TOPN - 1
    forced_blocks = [0] + [nblk - k for k in range(1, NSA_LOCAL)]
    imp = jnp.where((lane == 0) | (lane > nblk - NSA_LOCAL), -jnp.inf, imp)
    picks = jnp.zeros(imp.shape, i32)
    for it, fb in enumerate(forced_blocks):
        picks = jnp.where(lane == it, fb, picks)
    for it in range(len(forced_blocks), n_sel):
        mx = jnp.max(imp, axis=-1, keepdims=True)
        first = jnp.min(jnp.where(imp == mx, lane, nblk), axis=-1, keepdims=True)
        picks = jnp.where(lane == it, first, picks)
        imp = jnp.where(lane == first, -jnp.inf, imp)
    for b in range(sb):
        sel_ref[b] = picks[b * ng:(b + 1) * ng]


def _nsa_s_sel_kernel(sel_ref, pt_ref, q_ref, rb_ref, kvn_ref, oc_ref, ow_ref, gt_ref, *refs, n_sel, past):
    del pt_ref
    k_refs, v_refs, o_ref = refs[:n_sel], refs[n_sel:2 * n_sel], refs[2 * n_sel]
    b = pl.program_id(0)
    g = pl.program_id(1)
    dh, hpg = NSA_DH, NSA_HPG
    is_g0 = g == 0
    page = k_refs[0].shape[-1]
    per_page = page // NSA_BLOCK
    q = q_ref[0] * (dh ** -0.5)
    kvn = kvn_ref[0]
    lane = lax.broadcasted_iota(i32, (NSA_HEADS, page), 1)
    s_parts, d_parts = [], []
    for j in range(n_sel):
        nb = sel_ref[(b * NSA_KV_GROUPS + g) * n_sel + j]
        in_blk = (lane >> int(math.log2(NSA_BLOCK))) == (nb % per_page)
        d_parts.append(jnp.where(in_blk, past - ((nb // per_page) * page + lane), -1))
        s_parts.append(_mm3(q, k_refs[j][0, 0, 0]))
    d = jnp.concatenate(d_parts, axis=1)
    bias = jnp.broadcast_to(rb_ref[0], d.shape)
    for k in range(1, REL_BUCKETS):
        bias = jnp.where(d >= _BUCKET_THR[k - 1], rb_ref[k], bias)
    s = jnp.concatenate(s_parts, axis=1) + jnp.where(d < 0, NEG, bias)
    k_n = jnp.where(is_g0, kvn[:, 4 * dh:5 * dh], kvn[:, 5 * dh:6 * dh])
    v_n = jnp.where(is_g0, kvn[:, 6 * dh:7 * dh], kvn[:, 7 * dh:8 * dh])
    s_n = jnp.sum(q * k_n, axis=-1, keepdims=True) + rb_ref[0]
    m = jnp.maximum(jnp.max(s, axis=-1, keepdims=True), s_n)
    p = jnp.exp(s - m)
    p_n = jnp.exp(s_n - m)
    l = jnp.sum(p, axis=-1, keepdims=True) + p_n
    acc = p_n * v_n
    for j in range(n_sel):
        acc = acc + _mm_nt(p[:, j * page:(j + 1) * page], v_refs[j][0, 0, 0])
    o_s = acc / jnp.maximum(l, 1e-30)
    gt = gt_ref[0]
    o = gt[:, 0:1] * oc_ref[0] + gt[:, 1:2] * o_s + gt[:, 2:3] * ow_ref[0]
    row = lax.broadcasted_iota(i32, o.shape, 0)
    in_g = (row >= g * hpg) & (row < (g + 1) * hpg)

    @pl.when(is_g0)
    def _():
        o_ref[0] = o

    @pl.when(jnp.logical_not(is_g0))
    def _():
        o_ref[0] = jnp.where(in_g, o, o_ref[0])


def _nsa_sample(q_b, gates, kv4_new, kvwin_new, cache_kv, cache_win, page_table, rel_bias):
    db = q_b.shape[0]
    dh = NSA_DH
    n_pool, page = cache_kv.shape[:2]
    n_pages = page_table.shape[1]
    past = n_pages * page
    nblk = past // NSA_BLOCK
    wb = cache_win.shape[1]
    assert nblk == PAIR and page % NSA_BLOCK == 0 and wb == NSA_WINDOW and nblk > NSA_LOCAL
    cache_t = jnp.transpose(cache_kv, (0, 2, 3, 4, 1)).astype(f32)
    win_t = jnp.transpose(cache_win, (0, 2, 3, 4, 1)).astype(f32)
    kcvc = _nsa_s_means(cache_t, page_table)

    n = np.arange(nblk)
    d_cmp = past - (n * NSA_BLOCK + NSA_BLOCK - 1)
    jw = np.arange(wb)
    d_win = np.where(jw >= 1, wb - jw, -1)
    dist = np.broadcast_to(np.concatenate([d_cmp, d_win])[None, :], (NSA_HEADS, nblk + wb))
    tab = _bias_table(rel_bias, jnp.asarray(dist[None, None], i32), slab=1)[0, 0]
    t_cmp, t_win = tab[:, :nblk], tab[:, nblk:]
    rb = rel_bias.astype(f32)
    b0 = rb[0].reshape(NSA_HEADS, 1)

    q3 = q_b.reshape(db, NSA_HEADS, dh)
    sb = max(s for s in (8, 4, 2, 1) if db % s == 0)
    o_c, o_w, sel = pl.pallas_call(
        _nsa_s_scores_kernel,
        grid=(db // sb,),
        in_specs=[pl.BlockSpec((sb, NSA_HEADS, dh), lambda b: (b, 0, 0)),
                  pl.BlockSpec((sb, 2 * NSA_KV_GROUPS, dh, nblk), lambda b: (b, 0, 0, 0)),
                  pl.BlockSpec((sb, 2, NSA_KV_GROUPS, dh, wb), lambda b: (b, 0, 0, 0, 0)),
                  pl.BlockSpec((sb, 1, 4 * dh), lambda b: (b, 0, 0)),
                  pl.BlockSpec((NSA_HEADS, nblk), lambda b: (0, 0)),
                  pl.BlockSpec((NSA_HEADS, wb), lambda b: (0, 0)),
                  pl.BlockSpec((NSA_HEADS, 1), lambda b: (0, 0))],
        out_specs=[pl.BlockSpec((sb, NSA_HEADS, dh), lambda b: (b, 0, 0)),
                   pl.BlockSpec((sb, NSA_HEADS, dh), lambda b: (b, 0, 0)),
                   pl.BlockSpec((sb, NSA_KV_GROUPS, nblk), lambda b: (b, 0, 0))],
        out_shape=[jax.ShapeDtypeStruct((db, NSA_HEADS, dh), f32),
                   jax.ShapeDtypeStruct((db, NSA_HEADS, dh), f32),
                   jax.ShapeDtypeStruct((db, NSA_KV_GROUPS, nblk), i32)],
        compiler_params=_cparams(("parallel",)),
        name="nsa_sample_scores",
    )(q3, kcvc, win_t, kvwin_new.reshape(db, 1, 4 * dh), t_cmp, t_win, b0)

    n_sel = NSA_TOPN - 1
    sel_flat = sel[:, :, :n_sel].reshape(-1)
    per_page = page // NSA_BLOCK

    def page_spec(j, plane):
        def imap(b, g, s, p):
            nb = s[(b * NSA_KV_GROUPS + g) * n_sel + j]
            return (p[b * n_pages + nb // per_page], plane, g, 0, 0)
        return pl.BlockSpec((1, 1, 1, dh, page), imap)

    const = lambda shape: pl.BlockSpec(shape, lambda b, g, s, p: (0,) * len(shape))
    per_b = lambda shape: pl.BlockSpec((1,) + shape, lambda b, g, s, p: (b,) + (0,) * len(shape))
    o = pl.pallas_call(
        functools.partial(_nsa_s_sel_kernel, n_sel=n_sel, past=past),
        grid_spec=pltpu.PrefetchScalarGridSpec(
            num_scalar_prefetch=2,
            grid=(db, NSA_KV_GROUPS),
            in_specs=[per_b((NSA_HEADS, dh)), const((REL_BUCKETS, NSA_HEADS, 1)), per_b((1, 8 * dh)),
                      per_b((NSA_HEADS, dh)), per_b((NSA_HEADS, dh)), per_b((NSA_HEADS, 3))]
                     + [page_spec(j, 2) for j in range(n_sel)] + [page_spec(j, 3) for j in range(n_sel)],
            out_specs=per_b((NSA_HEADS, dh))),
        out_shape=jax.ShapeDtypeStruct((db, NSA_HEADS, dh), f32),
        compiler_params=_cparams(("parallel", "arbitrary")),
        name="nsa_sample_selected",
    )(sel_flat, page_table.reshape(-1), q3, rb.reshape(REL_BUCKETS, NSA_HEADS, 1), kv4_new.reshape(db, 1, 8 * dh),
      o_c, o_w, gates, *([cache_t] * (2 * n_sel)))
    return o.reshape(db, NSA_HEADS * dh)


def _gelu_tanh(x):
    return 0.5 * x * (1.0 + jnp.tanh(math.sqrt(2.0 / math.pi) * (x + 0.044715 * (x * x * x))))


def _lru_gates(xc, wa_ref, wx_ref, ba_ref, bx_ref, lam_ref, prec):
    r_parts, i_parts = [], []
    for n in range(RNN_BLOCKS):
        xb = xc[:, n * RNN_BW:(n + 1) * RNN_BW]
        r_parts.append(_dotp(xb, wa_ref[n], prec))
        i_parts.append(_dotp(xb, wx_ref[n], prec))
    r = jax.nn.sigmoid(jnp.concatenate(r_parts, axis=1) + ba_ref[...])
    i = jax.nn.sigmoid(jnp.concatenate(i_parts, axis=1) + bx_ref[...])
    log_a = -RG_C * r * _softplus(-lam_ref[...])
    a = jnp.exp(log_a)
    t = jnp.tanh(log_a)
    b = jnp.sqrt(jnp.maximum(-2.0 * t / (1.0 - t), 0.0)) * (i * xc)
    return a, b


def _lru_kernel(rec_ref, gate_ref, cw_ref, cb_ref, wa_ref, wx_ref, ba_ref, bx_ref, lam_ref, cinit_ref, h0_ref,
                y_ref, hfin_ref, xc_scr, a_scr, b_scr, hs_scr, h_scr, *, tb):
    j = pl.program_id(1)

    @pl.when(j == 0)
    def _():
        xc_scr[0:8, :] = cinit_ref[0]
        h_scr[...] = h0_ref[0]

    x = rec_ref[0]
    xc_scr[8:8 + tb, :] = x
    xc = xc_scr[5:5 + tb, :] * cw_ref[0:1, :]
    xc = xc + xc_scr[6:6 + tb, :] * cw_ref[1:2, :]
    xc = xc + xc_scr[7:7 + tb, :] * cw_ref[2:3, :]
    xc = xc + x * cw_ref[3:4, :]
    xc = xc + cb_ref[...]
    xc_scr[0:8, :] = xc_scr[tb:tb + 8, :]
    a, b = _lru_gates(xc, wa_ref, wx_ref, ba_ref, bx_ref, lam_ref, 1)
    a_scr[...] = a
    b_scr[...] = b

    def step(t, h):
        h = a_scr[pl.ds(t, 1), :] * h + b_scr[pl.ds(t, 1), :]
        hs_scr[pl.ds(t, 1), :] = h
        return h

    h = lax.fori_loop(0, tb, step, h_scr[...], unroll=8)
    h_scr[...] = h
    y_ref[0] = (_gelu_tanh(gate_ref[0]) * hs_scr[...]).astype(y_ref.dtype)

    @pl.when(j == pl.num_programs(1) - 1)
    def _():
        hfin_ref[0] = h


def _lru_prompt(rec, gate, conv_w, conv_b, wa, wx, ba, bx, lam, *, tb):
    bsz, t, w = rec.shape
    row = lambda a: a.reshape(1, w).astype(f32)
    cinit = jnp.zeros((bsz, 8, w), f32)
    h0 = jnp.zeros((bsz, 1, w), f32)
    full = lambda shape: pl.BlockSpec(shape, lambda b, j: (0,) * len(shape))
    return pl.pallas_call(
        functools.partial(_lru_kernel, tb=tb),
        grid=(bsz, t // tb),
        in_specs=[pl.BlockSpec((1, tb, w), lambda b, j: (b, j, 0)),
                  pl.BlockSpec((1, tb, w), lambda b, j: (b, j, 0)),
                  full((RNN_CONV, w)), full((1, w)),
                  full((RNN_BLOCKS, RNN_BW, RNN_BW)), full((RNN_BLOCKS, RNN_BW, RNN_BW)),
                  full((1, w)), full((1, w)), full((1, w)),
                  pl.BlockSpec((1, 8, w), lambda b, j: (b, 0, 0)),
                  pl.BlockSpec((1, 1, w), lambda b, j: (b, 0, 0))],
        out_specs=[pl.BlockSpec((1, tb, w), lambda b, j: (b, j, 0)),
                   pl.BlockSpec((1, 1, w), lambda b, j: (b, 0, 0))],
        out_shape=[jax.ShapeDtypeStruct((bsz, t, w), bf16), jax.ShapeDtypeStruct((bsz, 1, w), f32)],
        scratch_shapes=[pltpu.VMEM((tb + 8, w), f32), pltpu.VMEM((tb, w), f32), pltpu.VMEM((tb, w), f32),
                        pltpu.VMEM((tb, w), f32), pltpu.VMEM((1, w), f32)],
        compiler_params=_cparams(("parallel", "arbitrary")),
        name="rglru",
    )(rec, gate, conv_w.astype(f32), row(conv_b), wa.astype(f32), wx.astype(f32), row(ba), row(bx), row(lam),
      cinit, h0)


def _lru_step_kernel(rec_ref, gate_ref, b0_ref, b1_ref, b2_ref, cw_ref, cb_ref, wa_ref, wx_ref, ba_ref, bx_ref,
                     lam_ref, h0_ref, y_ref, h_ref):
    xc = b0_ref[...] * cw_ref[0:1, :]
    xc = xc + b1_ref[...] * cw_ref[1:2, :]
    xc = xc + b2_ref[...] * cw_ref[2:3, :]
    xc = xc + rec_ref[...] * cw_ref[3:4, :]
    xc = xc + cb_ref[...]
    a, b = _lru_gates(xc, wa_ref, wx_ref, ba_ref, bx_ref, lam_ref, 3)
    h = a * h0_ref[...] + b
    h_ref[...] = h
    y_ref[...] = _gelu_tanh(gate_ref[...]) * h


def _lru_sample(rec, gate, conv_buf, h0, conv_w, conv_b, wa, wx, ba, bx, lam):
    db, w = rec.shape
    row = lambda a: a.reshape(1, w).astype(f32)
    buf = conv_buf.astype(f32)
    return pl.pallas_call(
        _lru_step_kernel,
        out_shape=[jax.ShapeDtypeStruct((db, w), f32), jax.ShapeDtypeStruct((db, w), f32)],
        compiler_params=pltpu.CompilerParams(vmem_limit_bytes=VMEM_LIMIT),
        name="rglru_step",
    )(rec, gate, buf[:, 0], buf[:, 1], buf[:, 2], conv_w.astype(f32), row(conv_b), wa.astype(f32), wx.astype(f32),
      row(ba), row(bx), row(lam), h0.astype(f32))


def _top2_kernel(l_ref, e_ref, g_ref, cnt_ref, cnt_scr):
    @pl.when(pl.program_id(0) == 0)
    def _():
        cnt_scr[...] = jnp.zeros_like(cnt_scr)

    lg = l_ref[...]
    tm, w = lg.shape
    lane = lax.broadcasted_iota(i32, lg.shape, 1)
    lg = jnp.where(lane < N_EXPERTS, lg, -jnp.inf)
    m1 = jnp.max(lg, axis=-1, keepdims=True)
    i1 = jnp.min(jnp.where(lg == m1, lane, w), axis=-1, keepdims=True)
    lg2 = jnp.where(lane == i1, -jnp.inf, lg)
    m2 = jnp.max(lg2, axis=-1, keepdims=True)
    i2 = jnp.min(jnp.where(lg2 == m2, lane, w), axis=-1, keepdims=True)
    e2 = jnp.exp(m2 - m1)
    den = 1.0 + e2
    g_ref[...] = jnp.where(lane == 0, 1.0 / den, jnp.where(lane == 1, e2 / den, 0.0))
    hit1, hit2 = lane == i1, lane == i2
    routed = (hit1 | hit2).astype(bf16)
    r_i = lax.broadcasted_iota(i32, (tm, tm), 0)
    c_i = lax.broadcasted_iota(i32, (tm, tm), 1)
    before = jnp.dot((c_i < r_i).astype(bf16), routed, preferred_element_type=f32) + cnt_scr[...]
    rank1 = jnp.sum(jnp.where(hit1, before, 0.0), axis=-1, keepdims=True).astype(i32)
    rank2 = jnp.sum(jnp.where(hit2, before, 0.0), axis=-1, keepdims=True).astype(i32)
    e_ref[...] = jnp.where(lane == 0, i1, jnp.where(lane == 1, i2, jnp.where(lane == 2, rank1,
                                                                               jnp.where(lane == 3, rank2, 0))))
    cnt_scr[...] += jnp.sum(routed.astype(f32), axis=0, keepdims=True)
    cnt_ref[...] = cnt_scr[...].astype(i32)


def _top2(logits, *, tm):
    n, w = logits.shape
    return pl.pallas_call(
        _top2_kernel,
        grid=(n // tm,),
        in_specs=[pl.BlockSpec((tm, w), lambda i: (i, 0))],
        out_specs=[pl.BlockSpec((tm, w), lambda i: (i, 0)), pl.BlockSpec((tm, w), lambda i: (i, 0)),
                   pl.BlockSpec((1, w), lambda i: (0, 0))],
        out_shape=[jax.ShapeDtypeStruct((n, w), i32), jax.ShapeDtypeStruct((n, w), f32),
                   jax.ShapeDtypeStruct((1, w), i32)],
        scratch_shapes=[pltpu.VMEM((1, w), f32)],
        compiler_params=_cparams(("arbitrary",)),
        name="moe_top2",
    )(logits)


def _moe_kernel(be_ref, nu_ref, nv_ref, xs_ref, w1_ref, w3_ref, w2_ref, o_ref, acc_scr):
    del be_ref
    i = pl.program_id(0)
    f = pl.program_id(1)
    last = pl.num_programs(1) - 1
    used = i < nu_ref[0]
    half = xs_ref.shape[0] // 2
    top_only = nv_ref[i] <= half

    @pl.when(used & (f == 0))
    def _():
        acc_scr[...] = jnp.zeros_like(acc_scr)

    def update(rows):
        x = xs_ref[rows, :]
        act = _silu(_mm(x, w1_ref[0])) * _mm(x, w3_ref[0])
        acc_scr[rows, :] += _mm(act, w2_ref[0])

    @pl.when(used & jnp.logical_not(top_only))
    def _():
        update(slice(None))

    @pl.when(used & top_only)
    def _():
        update(slice(0, half))

    @pl.when(used & (f == last))
    def _():
        o_ref[...] = acc_scr[...]

    @pl.when(jnp.logical_not(used) & (f == last))
    def _():
        o_ref[...] = jnp.zeros_like(o_ref)


def _moe_experts(xs, blk_e, n_used, n_valid, w1, w3, w2, *, tm, tf):
    rows, d = xs.shape
    fdim = w1.shape[2]
    nf = fdim // tf

    def wcol(i, f, be, nu, nv):
        return (be[i], 0, jnp.where(i < nu[0], f, nf - 1))

    def wrow(i, f, be, nu, nv):
        return (be[i], jnp.where(i < nu[0], f, nf - 1), 0)

    return pl.pallas_call(
        _moe_kernel,
        grid_spec=pltpu.PrefetchScalarGridSpec(
            num_scalar_prefetch=3,
            grid=(rows // tm, nf),
            in_specs=[pl.BlockSpec((tm, d), lambda i, f, be, nu, nv: (i, 0)),
                      pl.BlockSpec((1, d, tf), wcol),
                      pl.BlockSpec((1, d, tf), wcol),
                      pl.BlockSpec((1, tf, d), wrow)],
            out_specs=pl.BlockSpec((tm, d), lambda i, f, be, nu, nv: (i, 0)),
            scratch_shapes=[pltpu.VMEM((tm, d), f32)]),
        out_shape=jax.ShapeDtypeStruct((rows, d), f32),
        compiler_params=_cparams(("arbitrary", "arbitrary")),
        name="moe_experts",
    )(blk_e, n_used, n_valid, xs, w1, w3, w2)


def _combine_kernel(x_ref, gate_ref, y0_ref, y1_ref, g_ref, nw_ref, o_ref):
    gw = g_ref[...]
    y = y0_ref[...] * gw[:, 0:1] + y1_ref[...] * gw[:, 1:2]
    x = x_ref[0] + gate_ref[0] * y
    o_ref[0] = x * lax.rsqrt(jnp.mean(x * x, axis=-1, keepdims=True) + EPS) * nw_ref[...]


def _moe_combine(x, gate, y0, y1, gw, norm_w, *, tm, row0):
    bsz, t, d = x.shape
    assert row0 % tm == 0
    tok = pl.BlockSpec((1, tm, d), lambda b, i: (b, i, 0))
    flat = lambda w: pl.BlockSpec((tm, w), lambda b, i: (row0 // tm + b * (t // tm) + i, 0))
    return pl.pallas_call(
        _combine_kernel,
        grid=(bsz, t // tm),
        in_specs=[tok, _mod_spec(gate, tm), flat(d), flat(d), flat(gw.shape[-1]),
                  pl.BlockSpec((1, d), lambda b, i: (0, 0))],
        out_specs=tok,
        out_shape=jax.ShapeDtypeStruct((bsz, t, d), f32),
        compiler_params=_cparams(("parallel", "parallel")),
        name="moe_combine",
    )(x, gate, y0, y1, gw, norm_w.reshape(1, d).astype(f32))


MOE_TM = 1024
MOE_TF = 512


def _moe_dispatch(e_idx, rank, counts, n_tok):
    tm = MOE_TM
    n_assign = n_tok * TOP_K
    padded = (counts + tm - 1) // tm * tm
    pend = jnp.cumsum(padded)
    pstart = pend - padded
    experts = jnp.arange(N_EXPERTS, dtype=i32)
    dest = jnp.sum(jnp.where(e_idx[:, :, None] == experts, pstart, 0), axis=-1) + rank
    n_blocks = -(-n_assign // tm) + N_EXPERTS
    rows = n_blocks * tm
    tok = jnp.broadcast_to(jnp.arange(n_tok, dtype=i32)[:, None], (n_tok, TOP_K))
    row_tok = (jnp.arange(rows, dtype=i32) % n_tok).at[dest.reshape(-1)].set(
        tok.reshape(-1), unique_indices=True, mode='promise_in_bounds')
    n_used = (pend[-1] // tm).astype(i32)
    blk = jnp.minimum(jnp.arange(n_blocks, dtype=i32), n_used - 1) * tm
    blk_e = jnp.minimum(jnp.sum((blk[:, None] >= pend[None, :]).astype(i32), axis=1), N_EXPERTS - 1)
    is_e = blk_e[:, None] == experts[None, :]
    first_row = jnp.sum(jnp.where(is_e, pstart, 0), axis=1)
    n_valid = jnp.clip(jnp.sum(jnp.where(is_e, counts, 0), axis=1) - (blk - first_row), 0, tm)
    return dest, row_tok, blk_e, n_used.reshape(1), n_valid.astype(i32)


def _w_in0_layout(w_in0):
    c_qkv = GDN_CONV_CH
    c_z = c_qkv + GDN_HEADS * GDN_DV
    c_ab = c_z + 2 * GDN_HEADS
    c_q = c_ab + NSA_HEADS * NSA_DH
    c_kv = c_q + 6 * NSA_KV_GROUPS * NSA_DH
    c_g = c_kv + 3 * NSA_HEADS
    assert c_g == w_in0.shape[1]
    n_misc = 2 * GDN_HEADS + 3 * NSA_HEADS
    w = jnp.concatenate([w_in0[:, :c_z], w_in0[:, c_ab:c_kv], w_in0[:, c_z:c_ab], w_in0[:, c_kv:c_g],
                         jnp.zeros((w_in0.shape[0], MISC_W - n_misc), w_in0.dtype)], axis=1)
    widths = (GDN_CONV_CH, GDN_HEADS * GDN_DV, NSA_HEADS * NSA_DH, 4 * NSA_KV_GROUPS * NSA_DH,
              2 * NSA_KV_GROUPS * NSA_DH, MISC_W)
    splits, s = [], 0
    for wd in widths:
        splits.append((s, s + wd))
        s += wd
    return w, tuple(splits)


def kernel(x_prompt, x_sample, c_prompt, c_sample, cache_nsa_kv, cache_nsa_win, state_gdn, state_gdn_conv, state_lru, state_lru_conv, page_table, rel_bias, w_ada, b_ada, norm_mix, norm_ffn, norm_final, w_in0, gdn_conv_w, gdn_a_log, gdn_dt_bias, gdn_norm_w, w_out0, ffn_w_gate, ffn_w_up, ffn_w_down, w_in1, lru_conv_w, lru_conv_b, lru_wa, lru_ba, lru_wx, lru_bx, lru_lambda, w_out1, moe_router, moe_w1, moe_w3, moe_w2):
    bsz, seq, d = x_prompt.shape
    db = x_sample.shape[0]
    assert x_sample.shape[1] == 1
    dh = NSA_DH

    n_c = bsz + db
    n_c_pad = -(-n_c // 8) * 8
    c_all = jnp.concatenate([c_prompt, c_sample, jnp.zeros((n_c_pad - n_c, d), f32)], axis=0)
    mods = _adaln(c_all, w_ada, b_ada).reshape(2, n_c_pad, N_MOD, d)
    mod_p = [[mods[l, :bsz, k].reshape(bsz, 1, d) for k in range(N_MOD)] for l in range(2)]
    mod_s = [[mods[l, bsz:n_c, k].reshape(1, db, d) for k in range(N_MOD)] for l in range(2)]

    w0, splits0 = _w_in0_layout(w_in0)
    splits1 = ((0, RNN_WIDTH), (RNN_WIDTH, 2 * RNN_WIDTH))
    router = jnp.concatenate([moe_router, jnp.zeros((d, MISC_W - N_EXPERTS), f32)], axis=1)
    bf = lambda w: w.astype(bf16)

    tm = PROMPT_TM
    xp = x_prompt
    qkv, z, q_b, kv4, kvwin, misc = _mod_matmul(xp, norm_mix[0], mod_p[0][0], mod_p[0][1], bf(w0), splits0,
                                                tm=tm, prec=1)
    o_a, p_gdn = _gdn(qkv, z, misc, gdn_conv_w, jnp.zeros((bsz, GDN_CONV - 1, GDN_CONV_CH), f32), gdn_a_log,
                      gdn_dt_bias, gdn_norm_w, jnp.zeros((bsz, GDN_HEADS, GDN_DK, GDN_DV), f32), tb=SCAN_TB, n_valid=seq,
                      out_dtype=bf16)
    p_gdn_conv = qkv[:, seq - (GDN_CONV - 1):]
    o_b = _nsa_prompt(q_b, misc, kv4, kvwin, rel_bias)
    p_nsa_kv = kv4.reshape(bsz, seq, 4, NSA_KV_GROUPS, dh)
    keep = min(NSA_WINDOW, seq)
    p_nsa_win = kvwin[:, seq - keep:].reshape(bsz, keep, 2, NSA_KV_GROUPS, dh)
    xp = _proj_residual([o_a, o_b], bf(w_out0), xp, mod_p[0][2], tm=tm, prec=1)
    xp = _ffn(xp, norm_ffn[0], mod_p[0][3], mod_p[0][4], mod_p[0][5], bf(ffn_w_gate), bf(ffn_w_up), bf(ffn_w_down),
              tm=tm, tf=FFN_TF_PROMPT, prec=1)
    gate_br, rec_br = _mod_matmul(xp, norm_mix[1], mod_p[1][0], mod_p[1][1], bf(w_in1), splits1, tm=tm, prec=1)
    y_in, p_lru = _lru_prompt(rec_br, gate_br, lru_conv_w, lru_conv_b, lru_wa, lru_wx, lru_ba, lru_bx, lru_lambda,
                              tb=SCAN_TB)
    p_lru_conv = rec_br[:, seq - (RNN_CONV - 1):]
    xp = _proj_residual([y_in], bf(w_out1), xp, mod_p[1][2], tm=tm, prec=1)
    logit_p, h_p = _mod_matmul(xp, norm_ffn[1], mod_p[1][3], mod_p[1][4], router, ((0, MISC_W),), tm=tm, prec=3,
                               emit_h=f32)

    xs = x_sample.reshape(1, db, d)
    qkv_s, z_s, q_s, kv4_s, kvwin_s, misc_s = _mod_matmul(xs, norm_mix[0], mod_s[0][0], mod_s[0][1], w0, splits0,
                                                          tm=db, prec=3)
    c = GDN_CHUNK
    tpad = lambda a: jnp.pad(a.reshape(db, 1, a.shape[-1]), ((0, 0), (0, c - 1), (0, 0)))
    o_a_s, s_gdn = _gdn(tpad(qkv_s), tpad(z_s), tpad(misc_s), gdn_conv_w, state_gdn_conv, gdn_a_log, gdn_dt_bias,
                        gdn_norm_w, state_gdn, tb=c, n_valid=1, out_dtype=f32)
    o_a_s = o_a_s[:, 0].reshape(1, db, GDN_HEADS * GDN_DV)
    s_gdn_conv = jnp.concatenate([state_gdn_conv[:, 1:], qkv_s.reshape(db, 1, GDN_CONV_CH)], axis=1)
    gates_s = jax.nn.sigmoid(misc_s[0, :, 2 * GDN_HEADS:2 * GDN_HEADS + 3 * NSA_HEADS]).reshape(db, NSA_HEADS, 3)
    o_b_s = _nsa_sample(q_s[0], gates_s, kv4_s[0], kvwin_s[0], cache_nsa_kv, cache_nsa_win, page_table, rel_bias)
    s_nsa_kv = kv4_s.reshape(db, 1, 4, NSA_KV_GROUPS, dh)
    s_nsa_win = jnp.concatenate([cache_nsa_win[:, 1:],
                                 kvwin_s.reshape(db, 1, 2, NSA_KV_GROUPS, dh).astype(cache_nsa_win.dtype)], axis=1)
    xs = _proj_residual([o_a_s, o_b_s.reshape(1, db, NSA_HEADS * dh)], w_out0, xs, mod_s[0][2], tm=db, prec=3)
    xs = _ffn(xs, norm_ffn[0], mod_s[0][3], mod_s[0][4], mod_s[0][5], ffn_w_gate, ffn_w_up, ffn_w_down,
              tm=db, tf=FFN_TF_SAMPLE, prec=3)
    gate_s, rec_s = _mod_matmul(xs, norm_mix[1], mod_s[1][0], mod_s[1][1], w_in1, splits1, tm=db, prec=3)
    y_in_s, s_lru = _lru_sample(rec_s[0], gate_s[0], state_lru_conv, state_lru, lru_conv_w, lru_conv_b, lru_wa, lru_wx,
                                lru_ba, lru_bx, lru_lambda)
    s_lru_conv = jnp.concatenate([state_lru_conv[:, 1:], rec_s.reshape(db, 1, RNN_WIDTH)], axis=1)
    xs = _proj_residual([y_in_s.reshape(1, db, RNN_WIDTH)], w_out1, xs, mod_s[1][2], tm=db, prec=3)
    logit_s, h_s = _mod_matmul(xs, norm_ffn[1], mod_s[1][3], mod_s[1][4], router, ((0, MISC_W),), tm=db, prec=3,
                               emit_h=f32)

    n_p = bsz * seq
    n_tok = n_p + db
    logits = jnp.concatenate([logit_p.reshape(n_p, MISC_W), logit_s.reshape(db, MISC_W)], axis=0)
    h_all = jnp.concatenate([h_p.reshape(n_p, d), h_s.reshape(db, d)], axis=0)
    route, gw, cnt = _top2(logits, tm=max(t for t in range(8, 1025, 8) if n_tok % t == 0))
    dest, row_tok, blk_e, n_used, n_valid = _moe_dispatch(route[:, :TOP_K], route[:, TOP_K:2 * TOP_K],
                                                          cnt[0, :N_EXPERTS], n_tok)
    yb = _moe_experts(h_all[row_tok], blk_e, n_used, n_valid, moe_w1, moe_w3, moe_w2, tm=MOE_TM, tf=MOE_TF)
    y0, y1 = yb[dest[:, 0]], yb[dest[:, 1]]
    y_prompt = _moe_combine(xp, mod_p[1][5], y0, y1, gw, norm_final, tm=tm, row0=0)
    y_sample = _moe_combine(xs, mod_s[1][5], y0, y1, gw, norm_final, tm=db, row0=n_p)

    return (y_prompt, y_sample.reshape(db, 1, d),
            p_nsa_kv, p_nsa_win, p_gdn, p_gdn_conv, p_lru.reshape(bsz, RNN_WIDTH), p_lru_conv,
            s_nsa_kv, s_nsa_win, s_gdn, s_gdn_conv, s_lru, s_lru_conv)
```

```python
import functools
import math

import numpy as np
import jax
import jax.numpy as jnp
from jax import lax
from jax.experimental import pallas as pl
from jax.experimental.pallas import tpu as pltpu

f32 = jnp.float32
bf16 = jnp.bfloat16
i32 = jnp.int32

D_MODEL = 1024
EPS = 1e-6
N_MOD = 6
GDN_HEADS = 4
GDN_DK = 128
GDN_DV = 128
GDN_CONV = 4
GDN_CHUNK = 64
GDN_CONV_CH = GDN_HEADS * (2 * GDN_DK + GDN_DV)
NSA_HEADS = 8
NSA_KV_GROUPS = 2
NSA_HPG = NSA_HEADS // NSA_KV_GROUPS
NSA_DH = 64
NSA_BLOCK = 64
NSA_TOPN = 16
NSA_LOCAL = 2
NSA_WINDOW = 512
NSA_QBLOCK = 128
REL_BUCKETS = 32
REL_MAX_DIST = 2048
RNN_WIDTH = D_MODEL
RNN_BLOCKS = 8
RNN_BW = RNN_WIDTH // RNN_BLOCKS
RNN_CONV = 4
RG_C = 8.0
FFN_DIM = 2816
N_EXPERTS = 8
TOP_K = 2
EXPERT_DIM = 3584

NEG = -1e30
PAIR = 2 * NSA_BLOCK
NEAR_PAIRS = 13
MISC_W = 128
VMEM_LIMIT = 56 * 1024 * 1024
PROMPT_TM = 512
SCAN_TB = 256
FFN_TF_PROMPT = FFN_DIM // 2
FFN_TF_SAMPLE = 256
MEANS_TB = 512


def _cparams(sem, vmem=VMEM_LIMIT):
    return pltpu.CompilerParams(dimension_semantics=sem, vmem_limit_bytes=vmem)


def _mm(a, b):
    return jnp.dot(a.astype(bf16), b.astype(bf16), preferred_element_type=f32)


def _mm_nt(a, b):
    return lax.dot_general(a.astype(bf16), b.astype(bf16), (((1,), (1,)), ((), ())),
                           preferred_element_type=f32)


def _split2(a):
    hi = a.astype(bf16)
    lo = (a - hi.astype(f32)).astype(bf16)
    return hi, lo


def _mm3(a, b):
    ah, al = _split2(a)
    bh, bl = _split2(b)
    return (jnp.dot(ah, bh, preferred_element_type=f32) + jnp.dot(ah, bl, preferred_element_type=f32)
            + jnp.dot(al, bh, preferred_element_type=f32))


def _mm3_nt(a, b):
    ah, al = _split2(a)
    bh, bl = _split2(b)
    dn = (((1,), (1,)), ((), ()))
    return (lax.dot_general(ah, bh, dn, preferred_element_type=f32)
            + lax.dot_general(ah, bl, dn, preferred_element_type=f32)
            + lax.dot_general(al, bh, dn, preferred_element_type=f32))


def _mm_01(m01, a):
    hi = a.astype(bf16)
    r1 = a - hi.astype(f32)
    mid = r1.astype(bf16)
    lo = (r1 - mid.astype(f32)).astype(bf16)
    return (jnp.dot(m01, hi, preferred_element_type=f32) + jnp.dot(m01, mid, preferred_element_type=f32)
            + jnp.dot(m01, lo, preferred_element_type=f32))


def _dotp(a, b, prec):
    return _mm3(a, b) if prec == 3 else _mm(a, b)


def _silu(x):
    return x * jax.nn.sigmoid(x)


def _softplus(x):
    return jnp.maximum(x, 0.0) + jnp.log1p(jnp.exp(-jnp.abs(x)))


def _modulate(x, gain, shift, scale):
    r = lax.rsqrt(jnp.mean(x * x, axis=-1, keepdims=True) + EPS)
    return x * r * gain * (1.0 + scale) + shift


def _ada_kernel(c_ref, w_ref, b_ref, o_ref):
    o_ref[0] = _mm3(_silu(c_ref[...]), w_ref[0]) + b_ref[0]


def _adaln(c_all, w_ada, b_ada):
    rows = c_all.shape[0]
    depth, d, n = w_ada.shape
    tn = 1536
    return pl.pallas_call(
        _ada_kernel,
        grid=(depth, n // tn),
        in_specs=[pl.BlockSpec((rows, d), lambda l, j: (0, 0)),
                  pl.BlockSpec((1, d, tn), lambda l, j: (l, 0, j)),
                  pl.BlockSpec((1, 1, tn), lambda l, j: (l, 0, j))],
        out_specs=pl.BlockSpec((1, rows, tn), lambda l, j: (l, 0, j)),
        out_shape=jax.ShapeDtypeStruct((depth, rows, n), f32),
        compiler_params=_cparams(("arbitrary", "arbitrary")),
        name="adaln",
    )(c_all, w_ada, b_ada.reshape(depth, 1, n))


def _mod_spec(mod, tm):
    r = mod.shape[1]
    if r == 1:
        return pl.BlockSpec((1, 1, mod.shape[2]), lambda b, i: (b, 0, 0))
    return pl.BlockSpec((1, tm, mod.shape[2]), lambda b, i: (b, i, 0))


def _modmm_kernel(x_ref, gain_ref, shift_ref, scale_ref, w_ref, *o_refs, splits, prec, emit_h):
    h = _modulate(x_ref[0], gain_ref[...], shift_ref[0], scale_ref[0])
    if emit_h:
        o_refs[-1][0] = h.astype(o_refs[-1].dtype)
    hh = _split2(h) if prec == 3 else h.astype(bf16)
    for o_ref, (a, b) in zip(o_refs, splits):
        w = w_ref[:, a:b]
        if prec == 3:
            wh, wl = _split2(w)
            acc = (jnp.dot(hh[0], wh, preferred_element_type=f32) + jnp.dot(hh[0], wl, preferred_element_type=f32)
                   + jnp.dot(hh[1], wh, preferred_element_type=f32))
        else:
            acc = jnp.dot(hh, w, preferred_element_type=f32)
        o_ref[0] = acc


def _mod_matmul(x, gain, shift, scale, w, splits, *, tm, prec, emit_h=None):
    bsz, t, d = x.shape
    out_shape = [jax.ShapeDtypeStruct((bsz, t, b - a), f32) for a, b in splits]
    out_specs = [pl.BlockSpec((1, tm, b - a), lambda bi, i: (bi, i, 0)) for a, b in splits]
    if emit_h is not None:
        out_shape.append(jax.ShapeDtypeStruct((bsz, t, d), emit_h))
        out_specs.append(pl.BlockSpec((1, tm, d), lambda bi, i: (bi, i, 0)))
    return pl.pallas_call(
        functools.partial(_modmm_kernel, splits=tuple(splits), prec=prec, emit_h=emit_h is not None),
        grid=(bsz, t // tm),
        in_specs=[pl.BlockSpec((1, tm, d), lambda bi, i: (bi, i, 0)),
                  pl.BlockSpec((1, d), lambda bi, i: (0, 0)),
                  _mod_spec(shift, tm), _mod_spec(scale, tm),
                  pl.BlockSpec(w.shape, lambda bi, i: (0, 0))],
        out_specs=out_specs,
        out_shape=out_shape,
        compiler_params=_cparams(("parallel", "parallel")),
        name="mod_matmul",
    )(x, gain.reshape(1, d), shift, scale, w)


def _projres_kernel(*refs, n_lhs, ksplits, prec, final_norm):
    lhs = refs[:n_lhs]
    w_ref, x_ref, gate_ref = refs[n_lhs:n_lhs + 3]
    o_ref = refs[-1]
    acc = None
    for l_ref, (a, b) in zip(lhs, ksplits):
        part = _dotp(l_ref[0], w_ref[a:b, :], prec)
        acc = part if acc is None else acc + part
    y = x_ref[0] + gate_ref[0] * acc
    if final_norm:
        nw_ref = refs[n_lhs + 3]
        y = y * lax.rsqrt(jnp.mean(y * y, axis=-1, keepdims=True) + EPS) * nw_ref[...]
    o_ref[0] = y


def _proj_residual(lhs_list, w, x, gate, *, tm, prec, norm_w=None):
    bsz, t, d = x.shape
    ksplits, k0 = [], 0
    for l in lhs_list:
        ksplits.append((k0, k0 + l.shape[-1]))
        k0 += l.shape[-1]
    in_specs = [pl.BlockSpec((1, tm, l.shape[-1]), lambda bi, i: (bi, i, 0)) for l in lhs_list]
    in_specs += [pl.BlockSpec(w.shape, lambda bi, i: (0, 0)),
                 pl.BlockSpec((1, tm, d), lambda bi, i: (bi, i, 0)),
                 _mod_spec(gate, tm)]
    args = list(lhs_list) + [w, x, gate]
    if norm_w is not None:
        in_specs.append(pl.BlockSpec((1, d), lambda bi, i: (0, 0)))
        args.append(norm_w.reshape(1, d))
    return pl.pallas_call(
        functools.partial(_projres_kernel, n_lhs=len(lhs_list), ksplits=tuple(ksplits), prec=prec,
                          final_norm=norm_w is not None),
        grid=(bsz, t // tm),
        in_specs=in_specs,
        out_specs=pl.BlockSpec((1, tm, d), lambda bi, i: (bi, i, 0)),
        out_shape=jax.ShapeDtypeStruct((bsz, t, d), f32),
        compiler_params=_cparams(("parallel", "parallel")),
        name="proj_residual",
    )(*args)


def _ffn_kernel(x_ref, gain_ref, shift_ref, scale_ref, gate_ref, wg_ref, wu_ref, wd_ref, o_ref,
                h_scr, acc_scr, *, prec):
    f = pl.program_id(2)

    @pl.when(f == 0)
    def _():
        h_scr[...] = _modulate(x_ref[0], gain_ref[...], shift_ref[0], scale_ref[0]).astype(h_scr.dtype)
        acc_scr[...] = jnp.zeros_like(acc_scr)

    h = h_scr[...]
    act = _silu(_dotp(h, wg_ref[...], prec)) * _dotp(h, wu_ref[...], prec)
    acc_scr[...] += _dotp(act, wd_ref[...], prec)

    @pl.when(f == pl.num_programs(2) - 1)
    def _():
        o_ref[0] = x_ref[0] + gate_ref[0] * acc_scr[...]


def _ffn(x, gain, shift, scale, gate, wg, wu, wd, *, tm, tf, prec):
    bsz, t, d = x.shape
    fdim = wg.shape[1]
    mod_specs = []
    for mod in (shift, scale, gate):
        if mod.shape[1] == 1:
            mod_specs.append(pl.BlockSpec((1, 1, d), lambda b, i, f: (b, 0, 0)))
        else:
            mod_specs.append(pl.BlockSpec((1, tm, d), lambda b, i, f: (b, i, 0)))
    return pl.pallas_call(
        functools.partial(_ffn_kernel, prec=prec),
        grid=(bsz, t // tm, fdim // tf),
        in_specs=[pl.BlockSpec((1, tm, d), lambda b, i, f: (b, i, 0)),
                  pl.BlockSpec((1, d), lambda b, i, f: (0, 0)),
                  mod_specs[0], mod_specs[1], mod_specs[2],
                  pl.BlockSpec((d, tf), lambda b, i, f: (0, f)),
                  pl.BlockSpec((d, tf), lambda b, i, f: (0, f)),
                  pl.BlockSpec((tf, d), lambda b, i, f: (f, 0))],
        out_specs=pl.BlockSpec((1, tm, d), lambda b, i, f: (b, i, 0)),
        out_shape=jax.ShapeDtypeStruct((bsz, t, d), f32),
        scratch_shapes=[pltpu.VMEM((tm, d), f32 if prec == 3 else bf16), pltpu.VMEM((tm, d), f32)],
        compiler_params=_cparams(("parallel", "parallel", "arbitrary")),
        name="ffn",
    )(x, gain.reshape(1, d), shift, scale, gate, wg, wu, wd)


def _gdn_kernel(qkv_ref, z_ref, misc_ref, cw_ref, cinit_ref, hp_ref, nw_ref, s0_ref,
                o_ref, sfin_ref, xc_scr, s_scr, *, tb, n_valid):
    j = pl.program_id(1)
    c = GDN_CHUNK
    nh = GDN_HEADS

    @pl.when(j == 0)
    def _():
        xc_scr[0:8, :] = cinit_ref[0]
        s_scr[...] = s0_ref[0]

    x = qkv_ref[0]
    xc_scr[8:8 + tb, :] = x
    y = xc_scr[5:5 + tb, :] * cw_ref[0:1, :]
    y = y + xc_scr[6:6 + tb, :] * cw_ref[1:2, :]
    y = y + xc_scr[7:7 + tb, :] * cw_ref[2:3, :]
    y = y + x * cw_ref[3:4, :]
    xc_scr[0:8, :] = xc_scr[tb:tb + 8, :]
    y = _silu(y)

    misc = misc_ref[0]
    row = lax.broadcasted_iota(i32, (tb, MISC_W), 0) + j * tb
    live = row < n_valid
    log_a = jnp.where(live, hp_ref[0:1, :] * _softplus(misc + hp_ref[1:2, :]), 0.0)
    beta = jnp.where(live, jax.nn.sigmoid(misc), 0.0)

    r_i = lax.broadcasted_iota(i32, (tb, tb), 0)
    c_i = lax.broadcasted_iota(i32, (tb, tb), 1)
    sh = int(math.log2(c))
    ltri = (((r_i >> sh) == (c_i >> sh)) & (c_i <= r_i)).astype(bf16)
    g = _mm_01(ltri, log_a)
    g_t = g.T

    ri = lax.broadcasted_iota(i32, (c, c), 0)
    ci = lax.broadcasted_iota(i32, (c, c), 1)
    incl = ci <= ri
    strict = ci < ri
    eye = (ci == ri).astype(f32)
    quad = []
    for lvl in range(int(math.log2(c))):
        quad.append(((ri >> (lvl + 1)) == (ci >> (lvl + 1))) & (((ri >> lvl) & 1) == 1) & (((ci >> lvl) & 1) == 0))

    nchunk = tb // c
    units = [(n, h) for n in range(nchunk) for h in range(nh)]

    def stack(fn):
        return jnp.stack([fn(n * c, h) for n, h in units], axis=0)

    def bmm(eq, a_, b_):
        return jnp.einsum(eq, a_.astype(bf16), b_.astype(bf16), preferred_element_type=f32)

    q = stack(lambda r0, h: y[r0:r0 + c, h * GDN_DK:(h + 1) * GDN_DK])
    k = stack(lambda r0, h: y[r0:r0 + c, (nh + h) * GDN_DK:(nh + h + 1) * GDN_DK])
    v = stack(lambda r0, h: y[r0:r0 + c, 2 * nh * GDN_DK + h * GDN_DV:2 * nh * GDN_DK + (h + 1) * GDN_DV])
    q = q * lax.rsqrt(jnp.sum(q * q, axis=-1, keepdims=True) + EPS) * (GDN_DK ** -0.5)
    k = k * lax.rsqrt(jnp.sum(k * k, axis=-1, keepdims=True) + EPS)
    g_col = stack(lambda r0, h: g[r0:r0 + c, h:h + 1])
    g_row = stack(lambda r0, h: g_t[h:h + 1, r0:r0 + c])
    b_col = stack(lambda r0, h: beta[r0:r0 + c, nh + h:nh + h + 1])
    gam = jnp.where(incl, jnp.exp(jnp.where(incl, g_col - g_row, 0.0)), 0.0)
    kk = bmm('uid,ujd->uij', k, k)
    a = jnp.where(strict, b_col * gam * kk, 0.0)
    p = eye - jnp.where(quad[0], a, 0.0)
    for lvl in range(1, len(quad)):
        m = bmm('uij,ujk->uik', jnp.where(quad[lvl], a, 0.0), p)
        p = p - bmm('uij,ujk->uik', p, m)
    e_g = jnp.exp(g_col)
    sol = bmm('uij,ujd->uid', p, jnp.concatenate([b_col * v, (b_col * e_g) * k], axis=-1))
    vb, w = sol[:, :, :GDN_DV], sol[:, :, GDN_DV:]
    aqk = bmm('uid,ujd->uij', q, k) * gam
    qg = q * e_g
    g_last = g_col[:, c - 1:c, :]
    kd_t = jnp.swapaxes(k * jnp.exp(g_last - g_col), 1, 2)
    gc = jnp.exp(g_last)

    s = s_scr[...]
    for n in range(nchunk):
        sl = slice(n * nh, (n + 1) * nh)
        u = vb[sl] - bmm('hcd,hde->hce', w[sl], s)
        o = bmm('hcd,hde->hce', qg[sl], s) + bmm('hij,hje->hie', aqk[sl], u)
        s = gc[sl] * s + bmm('hdc,hce->hde', kd_t[sl], u)
        o = o * lax.rsqrt(jnp.mean(o * o, axis=-1, keepdims=True) + EPS) * nw_ref[...]
        for h in range(nh):
            zs = z_ref[0, n * c:(n + 1) * c, h * GDN_DV:(h + 1) * GDN_DV]
            o_ref[0, n * c:(n + 1) * c, h * GDN_DV:(h + 1) * GDN_DV] = (o[h] * _silu(zs)).astype(o_ref.dtype)
    s_scr[...] = s

    @pl.when(j == pl.num_programs(1) - 1)
    def _():
        sfin_ref[0] = s_scr[...]


def _gdn(qkv_raw, z, misc, conv_w, conv_buf, a_log, dt_bias, norm_w, s0, *, tb, n_valid):
    bsz, tp, ch = qkv_raw.shape
    cinit = jnp.concatenate([jnp.zeros((bsz, 5, ch), f32), conv_buf.astype(f32)], axis=1)
    hp = jnp.zeros((8, MISC_W), f32)
    hp = hp.at[0, :GDN_HEADS].set(-jnp.exp(a_log.astype(f32))).at[1, :GDN_HEADS].set(dt_bias.astype(f32))
    zw = GDN_HEADS * GDN_DV
    return pl.pallas_call(
        functools.partial(_gdn_kernel, tb=tb, n_valid=n_valid),
        grid=(bsz, tp // tb),
        in_specs=[pl.BlockSpec((1, tb, ch), lambda b, j: (b, j, 0)),
                  pl.BlockSpec((1, tb, zw), lambda b, j: (b, j, 0)),
                  pl.BlockSpec((1, tb, MISC_W), lambda b, j: (b, j, 0)),
                  pl.BlockSpec((GDN_CONV, ch), lambda b, j: (0, 0)),
                  pl.BlockSpec((1, 8, ch), lambda b, j: (b, 0, 0)),
                  pl.BlockSpec((8, MISC_W), lambda b, j: (0, 0)),
                  pl.BlockSpec((1, GDN_DV), lambda b, j: (0, 0)),
                  pl.BlockSpec((1, GDN_HEADS, GDN_DK, GDN_DV), lambda b, j: (b, 0, 0, 0))],
        out_specs=[pl.BlockSpec((1, tb, zw), lambda b, j: (b, j, 0)),
                   pl.BlockSpec((1, GDN_HEADS, GDN_DK, GDN_DV), lambda b, j: (b, 0, 0, 0))],
        out_shape=[jax.ShapeDtypeStruct((bsz, tp, zw), bf16),
                   jax.ShapeDtypeStruct((bsz, GDN_HEADS, GDN_DK, GDN_DV), f32)],
        scratch_shapes=[pltpu.VMEM((tb + 8, ch), f32), pltpu.VMEM((GDN_HEADS, GDN_DK, GDN_DV), f32)],
        compiler_params=_cparams(("parallel", "arbitrary")),
        name="gdn",
    )(qkv_raw, z, misc, conv_w.astype(f32), cinit, hp, norm_w.reshape(1, GDN_DV).astype(f32), s0.astype(f32))


def _gdn_step_kernel(x_ref, buf_ref, cw_ref, z_ref, misc_ref, hp_ref, nw_ref, s0_ref, o_ref, s_ref):
    nh = GDN_HEADS
    buf = buf_ref[0]
    y = buf[0:1] * cw_ref[0:1, :]
    y = y + buf[1:2] * cw_ref[1:2, :]
    y = y + buf[2:3] * cw_ref[2:3, :]
    y = _silu(y + x_ref[0] * cw_ref[3:4, :])
    misc = misc_ref[0]
    a_all = jnp.exp(hp_ref[0:1, :] * _softplus(misc + hp_ref[1:2, :]))
    b_all = jax.nn.sigmoid(misc)
    eye = lax.broadcasted_iota(i32, (GDN_DK, GDN_DK), 0) == lax.broadcasted_iota(i32, (GDN_DK, GDN_DK), 1)
    pad = jnp.zeros((6, GDN_DK), f32)
    outs = []
    for h in range(nh):
        q = y[:, h * GDN_DK:(h + 1) * GDN_DK]
        k = y[:, (nh + h) * GDN_DK:(nh + h + 1) * GDN_DK]
        v = y[:, 2 * nh * GDN_DK + h * GDN_DV:2 * nh * GDN_DK + (h + 1) * GDN_DV]
        q = q * lax.rsqrt(jnp.sum(q * q, axis=-1, keepdims=True) + EPS) * (GDN_DK ** -0.5)
        k = k * lax.rsqrt(jnp.sum(k * k, axis=-1, keepdims=True) + EPS)
        a = a_all[:, h:h + 1]
        b = b_all[:, nh + h:nh + h + 1]
        s = s0_ref[0, h]
        ks_qs = _mm3(jnp.concatenate([k, q, pad], axis=0), s)
        u = b * v - (b * a) * ks_qs[0:1]
        o = a * ks_qs[1:2] + jnp.sum(q * k, axis=-1, keepdims=True) * u
        k_col = jnp.sum(jnp.where(eye, k, 0.0), axis=1, keepdims=True)
        s_ref[0, h] = a * s + k_col * u
        o = o * lax.rsqrt(jnp.mean(o * o, axis=-1, keepdims=True) + EPS) * nw_ref[...]
        outs.append(o * _silu(z_ref[0][:, h * GDN_DV:(h + 1) * GDN_DV]))
    o_ref[0] = jnp.concatenate(outs, axis=1)


def _gdn_step(qkv_raw, z, misc, conv_w, conv_buf, a_log, dt_bias, norm_w, s0):
    db, _, ch = qkv_raw.shape
    hp = jnp.zeros((8, MISC_W), f32)
    hp = hp.at[0, :GDN_HEADS].set(-jnp.exp(a_log.astype(f32))).at[1, :GDN_HEADS].set(dt_bias.astype(f32))
    zw = GDN_HEADS * GDN_DV
    per_b = lambda shape: pl.BlockSpec((1,) + shape, lambda b: (b,) + (0,) * len(shape))
    const = lambda shape: pl.BlockSpec(shape, lambda b: (0,) * len(shape))
    return pl.pallas_call(
        _gdn_step_kernel,
        grid=(db,),
        in_specs=[per_b((1, ch)), per_b((GDN_CONV - 1, ch)), const((GDN_CONV, ch)), per_b((1, zw)),
                  per_b((1, MISC_W)), const((8, MISC_W)), const((1, GDN_DV)), per_b((GDN_HEADS, GDN_DK, GDN_DV))],
        out_specs=[per_b((1, zw)), per_b((GDN_HEADS, GDN_DK, GDN_DV))],
        out_shape=[jax.ShapeDtypeStruct((db, 1, zw), f32),
                   jax.ShapeDtypeStruct((db, GDN_HEADS, GDN_DK, GDN_DV), f32)],
        compiler_params=_cparams(("parallel",)),
        name="gdn_step",
    )(qkv_raw, conv_buf.astype(f32), conv_w.astype(f32), z, misc, hp, norm_w.reshape(1, GDN_DV).astype(f32),
      s0.astype(f32))


def _bucket_thresholds():
    exact = REL_BUCKETS // 2
    d = np.arange(0, 4 * REL_MAX_DIST, dtype=np.int64)

    def buckets(ft):
        nf = np.maximum(d, exact).astype(ft)
        large = exact + (np.log(nf / ft(exact)) / ft(math.log(REL_MAX_DIST / exact)) * ft(REL_BUCKETS - exact)).astype(np.int32)
        return np.where(d < exact, d, np.minimum(large, REL_BUCKETS - 1))

    b64, b32 = buckets(np.float64), buckets(np.float32)
    assert np.array_equal(b64, b32) and b64[-1] == REL_BUCKETS - 1 and np.all(np.diff(b64) >= 0)
    return [int(np.argmax(b64 >= k)) for k in range(1, REL_BUCKETS)]


_BUCKET_THR = _bucket_thresholds()
FAR_DIST = _BUCKET_THR[-1]


def _bias_kernel(rb_ref, d_ref, o_ref, *, slab, group_heads):
    rows = d_ref.shape[-2]
    for s in range(rows // slab):
        d = d_ref[0, 0, s * slab:(s + 1) * slab, :]
        h = pl.program_id(0) * group_heads + s
        val = jnp.full(d.shape, rb_ref[0, h], f32)
        for k in range(1, REL_BUCKETS):
            val = jnp.where(d >= _BUCKET_THR[k - 1], rb_ref[k, h], val)
        o_ref[0, 0, s * slab:(s + 1) * slab, :] = jnp.where(d < 0, NEG, val)


def _bias_table(rel_bias, dist, *, slab):
    g, nt, rows, cols = dist.shape
    return pl.pallas_call(
        functools.partial(_bias_kernel, slab=slab, group_heads=rows // slab),
        grid=(g, nt),
        in_specs=[pl.BlockSpec(memory_space=pltpu.SMEM),
                  pl.BlockSpec((1, 1, rows, cols), lambda a, b: (a, b, 0, 0))],
        out_specs=pl.BlockSpec((1, 1, rows, cols), lambda a, b: (a, b, 0, 0)),
        out_shape=jax.ShapeDtypeStruct(dist.shape, f32),
        compiler_params=_cparams(("arbitrary", "arbitrary")),
        name="bias_table",
    )(rel_bias.astype(f32), dist)


def _blockmean_kernel(x_ref, o_ref):
    x = x_ref[0]
    nb = x.shape[0] // NSA_BLOCK
    o_ref[0] = jnp.sum(x.reshape(nb, NSA_BLOCK, x.shape[1]), axis=1) * (1.0 / NSA_BLOCK)


def _block_means(kv4, *, tb):
    bsz, t, _ = kv4.shape
    w = 2 * NSA_KV_GROUPS * NSA_DH
    return pl.pallas_call(
        _blockmean_kernel,
        grid=(bsz, t // tb),
        in_specs=[pl.BlockSpec((1, tb, w), lambda b, i: (b, i, 0))],
        out_specs=pl.BlockSpec((1, tb // NSA_BLOCK, w), lambda b, i: (b, i, 0)),
        out_shape=jax.ShapeDtypeStruct((bsz, t // NSA_BLOCK, w), f32),
        compiler_params=_cparams(("parallel", "parallel")),
        name="nsa_block_means",
    )(kv4)


LOG2E = 1.4426950408889634
NSA_KEY_TILE = 4 * PAIR


def _nsa_prompt_kernel(q_ref, misc_ref, kcvc_ref, kaug_ref, vgt_ref, kw_ref, vw_ref,
                       tabs_ref, tabw_ref, tabc_ref, c31_ref, o_ref, qaug_scr, s_scr):
    g = pl.program_id(1)
    i = pl.program_id(2)
    qb, dh, hpg = NSA_QBLOCK, NSA_DH, NSA_HPG
    is_g0 = g == 0
    nn = (((1,), (1,)), ((), ()))

    def ghalf(a):
        return jnp.where(is_g0, a[:, :dh], a[:, dh:])

    qt = q_ref[0] * (dh ** -0.5)
    qh = [qt[:, h * dh:(h + 1) * dh] for h in range(hpg)]
    kcvc = kcvc_ref[0]
    nblk = kcvc.shape[0]
    kc = ghalf(kcvc[:, :2 * dh])
    vc = ghalf(kcvc[:, 2 * dh:])

    rows = hpg * qb
    qs = jnp.concatenate(qh, axis=0)
    zero = jnp.zeros_like(qs)
    q2f = jnp.concatenate([jnp.where(is_g0, qs, zero), jnp.where(is_g0, zero, qs)], axis=1) * LOG2E
    q2 = q2f.astype(bf16)

    blk4 = lax.broadcasted_iota(i32, (rows, nblk), 1)
    qrow4 = lax.broadcasted_iota(i32, (rows, nblk), 0) & (qb - 1)
    first_tab = 2 * i - 2 * NEAR_PAIRS + 1
    shift = (first_tab + 4 * nblk) % nblk
    readable = blk4 * NSA_BLOCK + (NSA_BLOCK - 1) <= i * qb + qrow4
    bias = jnp.where(blk4 < first_tab, c31_ref[0], pltpu.roll(tabc_ref[0], shift, 1))
    s_c = jnp.where(readable, _mm3_nt(qs, kc) + bias, NEG)
    m_c = jnp.max(s_c, axis=-1, keepdims=True)
    p_c = jnp.where(readable, jnp.exp(s_c - m_c), 0.0)
    p_c = p_c / jnp.maximum(jnp.sum(p_c, axis=-1, keepdims=True), 1e-30)
    o_cmp = _mm(p_c, vc)
    imp = p_c[0:qb]
    for h in range(1, hpg):
        imp = imp + p_c[h * qb:(h + 1) * qb]

    span = NSA_WINDOW + qb
    ws = pl.multiple_of(i * qb, qb)
    s_w = lax.dot_general(q2, kw_ref[0, pl.ds(ws, span), :], nn, preferred_element_type=f32) + tabw_ref[0]
    in_seq = lax.broadcasted_iota(i32, (rows, span), 1) + (i * qb - NSA_WINDOW) >= 0
    s_w = jnp.where(in_seq, s_w, NEG)
    m_w = jnp.max(s_w, axis=-1, keepdims=True)
    p_w = jnp.exp2(s_w - m_w)
    l_w = jnp.sum(p_w, axis=-1, keepdims=True)
    o_win = ghalf(jnp.dot(p_w.astype(bf16), vw_ref[0, pl.ds(ws, span), :], preferred_element_type=f32))
    o_win = o_win / jnp.maximum(l_w, 1e-30)

    blk = lax.broadcasted_iota(i32, (nblk, qb), 0)
    cur = (i * qb + lax.broadcasted_iota(i32, (nblk, qb), 1)) >> int(math.log2(NSA_BLOCK))
    valid = blk <= cur
    forced = valid & ((blk == 0) | (blk > cur - NSA_LOCAL))
    imp_t = jnp.where(forced, -jnp.inf, jnp.where(valid, imp.T, -1.0))
    picked = forced
    for _ in range(min(NSA_TOPN, nblk) - 1 - NSA_LOCAL):
        mx = jnp.max(imp_t, axis=0, keepdims=True)
        first = jnp.min(jnp.where(imp_t == mx, blk, nblk), axis=0, keepdims=True)
        hit = blk == first
        picked = picked | hit
        imp_t = jnp.where(hit, -jnp.inf, imp_t)
    msel_t = jnp.where(picked & valid, 0.0, NEG).astype(bf16)

    qaug_scr[...] = jnp.concatenate([q2f.T.astype(bf16), jnp.concatenate([msel_t] * hpg, axis=1)], axis=0)
    tk = s_scr.shape[0]
    ppt = tk // PAIR
    t_d = i // ppt

    def scores(t):
        ks = pl.multiple_of(t * tk, tk)
        s = jnp.dot(kaug_ref[0, pl.ds(ks, tk), :], qaug_scr[...], preferred_element_type=f32)
        tabs = [tabs_ref[0, jnp.clip(i - ppt * t - j, 0, NEAR_PAIRS)] for j in range(ppt)]
        return s + jnp.concatenate(tabs, axis=0)

    s_scr[...] = scores(t_d)

    def tile(k_it, carry):
        m, acc = carry
        t = t_d - k_it
        s = s_scr[...]
        s_next = scores(jnp.maximum(t - 1, 0))
        m_new = jnp.maximum(m, jnp.max(s, axis=0, keepdims=True))
        alpha = jnp.exp2(m - m_new)
        p = jnp.exp2(s - m_new).astype(bf16)
        ks = pl.multiple_of(t * tk, tk)
        acc = alpha * acc + jnp.dot(vgt_ref[0, 0, :, pl.ds(ks, tk)], p, preferred_element_type=f32)
        s_scr[...] = s_next
        return m_new, acc

    _, acc_t = lax.fori_loop(0, t_d + 1, tile, (jnp.full((1, rows), NEG, f32), jnp.zeros((2 * dh, rows), f32)))
    o_sel = (acc_t[:dh] / jnp.maximum(acc_t[dh:dh + 1], 1e-30)).T

    gates = jax.nn.sigmoid(misc_ref[0])
    outs = []
    for h in range(hpg):
        r = slice(h * qb, (h + 1) * qb)
        acc = None
        for jb, branch in enumerate((o_cmp[r], o_sel[r], o_win[r])):
            la = 8 + 3 * h + jb
            lb = 8 + 3 * (hpg + h) + jb
            term = jnp.where(is_g0, gates[:, la:la + 1], gates[:, lb:lb + 1]) * branch
            acc = term if acc is None else acc + term
        outs.append(acc)
    o_ref[0] = jnp.concatenate(outs, axis=1).astype(o_ref.dtype)


def _nsa_tables(rel_bias):
    qb, hpg = NSA_QBLOCK, NSA_HPG
    rows = hpg * qb
    q = (np.arange(rows) % qb)[:, None]
    c = np.arange(PAIR)[None, :]
    d_sel = [PAIR * idx + q - c for idx in range(NEAR_PAIRS + 1)]
    assert PAIR * NEAR_PAIRS - (PAIR - 1) >= FAR_DIST
    mm = 2 * NEAR_PAIRS - 1 - c
    d_cmp = np.maximum(np.where(mm >= -1, NSA_BLOCK * mm + q - (NSA_BLOCK - 1), 0), 0)
    assert NSA_BLOCK * (2 * NEAR_PAIRS) - (NSA_BLOCK - 1) >= FAR_DIST
    cw = np.arange(NSA_WINDOW + qb)[None, :]
    d_win = q + NSA_WINDOW - cw
    d_win = np.where((d_win >= 0) & (d_win < NSA_WINDOW), d_win, -1)
    n_win = (NSA_WINDOW + qb) // PAIR
    tiles = d_sel + [d_cmp] + [d_win[:, k * PAIR:(k + 1) * PAIR] for k in range(n_win)]
    dist = np.broadcast_to(np.stack(tiles)[None], (NSA_KV_GROUPS, len(tiles), rows, PAIR))
    tab = _bias_table(rel_bias, jnp.asarray(dist, i32), slab=qb)
    ns = NEAR_PAIRS + 1
    tabs = tab[:, :ns]
    tabc = tab[:, ns]
    tabw = jnp.concatenate([tab[:, ns + 1 + k] for k in range(n_win)], axis=-1)
    c31 = jnp.repeat(rel_bias.astype(f32)[REL_BUCKETS - 1].reshape(NSA_KV_GROUPS, hpg), qb, axis=1)
    return tabs, tabw, tabc, c31.reshape(NSA_KV_GROUPS, rows, 1)


def _nsa_prompt(q_b, misc, kv4, kvwin, rel_bias):
    bsz, t, _ = q_b.shape
    dh, qb = NSA_DH, NSA_QBLOCK
    nblk = t // NSA_BLOCK
    assert nblk == PAIR and t % NSA_KEY_TILE == 0
    kcvc = _block_means(kv4, tb=MEANS_TB)
    onehot = (jnp.arange(t, dtype=i32)[:, None] // NSA_BLOCK == jnp.arange(nblk, dtype=i32)[None, :]).astype(bf16)
    kaug = jnp.concatenate([kv4[:, :, 4 * dh:6 * dh].astype(bf16),
                            jnp.broadcast_to(onehot[None], (bsz, t, nblk))], axis=-1)
    ones = jnp.ones((bsz, t, dh), bf16)
    vgt = jnp.stack([jnp.swapaxes(jnp.concatenate([kv4[:, :, (6 + g) * dh:(7 + g) * dh].astype(bf16), ones],
                                                   axis=-1), 1, 2)
                     for g in range(NSA_KV_GROUPS)], axis=1)
    pad = ((0, 0), (NSA_WINDOW, 0), (0, 0))
    kw = jnp.pad(kvwin[:, :, :2 * dh].astype(bf16), pad)
    vw = jnp.pad(kvwin[:, :, 2 * dh:].astype(bf16), pad)
    tabs, tabw, tabc, c31 = _nsa_tables(rel_bias)
    tabs, tabw = jnp.swapaxes(tabs, 2, 3) * LOG2E, tabw * LOG2E
    rows = NSA_HPG * qb
    gw = NSA_HPG * dh
    span = NSA_WINDOW + qb
    return pl.pallas_call(
        _nsa_prompt_kernel,
        grid=(bsz, NSA_KV_GROUPS, t // qb),
        in_specs=[pl.BlockSpec((1, qb, gw), lambda b, g, i: (b, i, g)),
                  pl.BlockSpec((1, qb, MISC_W), lambda b, g, i: (b, i, 0)),
                  pl.BlockSpec((1, nblk, 4 * dh), lambda b, g, i: (b, 0, 0)),
                  pl.BlockSpec((1, t, 2 * dh + nblk), lambda b, g, i: (b, 0, 0)),
                  pl.BlockSpec((1, 1, 2 * dh, t), lambda b, g, i: (b, g, 0, 0)),
                  pl.BlockSpec((1, t + NSA_WINDOW, 2 * dh), lambda b, g, i: (b, 0, 0)),
                  pl.BlockSpec((1, t + NSA_WINDOW, 2 * dh), lambda b, g, i: (b, 0, 0)),
                  pl.BlockSpec((1, NEAR_PAIRS + 1, PAIR, rows), lambda b, g, i: (g, 0, 0, 0)),
                  pl.BlockSpec((1, rows, span), lambda b, g, i: (g, 0, 0)),
                  pl.BlockSpec((1, rows, PAIR), lambda b, g, i: (g, 0, 0)),
                  pl.BlockSpec((1, rows, 1), lambda b, g, i: (g, 0, 0))],
        out_specs=pl.BlockSpec((1, qb, gw), lambda b, g, i: (b, i, g)),
        out_shape=jax.ShapeDtypeStruct((bsz, t, NSA_HEADS * dh), bf16),
        scratch_shapes=[pltpu.VMEM((2 * dh + nblk, rows), bf16), pltpu.VMEM((NSA_KEY_TILE, rows), f32)],
        compiler_params=_cparams(("parallel", "parallel", "arbitrary")),
        name="nsa_prompt",
    )(q_b, misc, kcvc, kaug, vgt, kw, vw, tabs, tabw, tabc, c31)


MEANS_PAGES = 16


def _nsa_s_means_kernel(pt_ref, *refs):
    del pt_ref
    x_refs, o_ref = refs[:-1], refs[-1]
    p = pl.program_id(1)

    @pl.when(p == 0)
    def _():
        o_ref[...] = jnp.zeros_like(o_ref)

    page = x_refs[0].shape[-1]
    per_page = page // NSA_BLOCK
    nblk = o_ref.shape[-1]
    assert page == nblk
    planes = 2 * NSA_KV_GROUPS
    xs = jnp.concatenate([x_ref[0, j, g] for x_ref in x_refs for j in range(2) for g in range(NSA_KV_GROUPS)],
                         axis=0)
    tok = lax.broadcasted_iota(i32, (page, nblk), 0)
    col = lax.broadcasted_iota(i32, (page, nblk), 1)
    pool = (col == (tok >> int(math.log2(NSA_BLOCK)))).astype(bf16)
    hi, lo = _split2(xs)
    sums = jnp.dot(hi, pool, preferred_element_type=f32) + jnp.dot(lo, pool, preferred_element_type=f32)
    rows = planes * NSA_DH
    acc = None
    for k in range(len(x_refs)):
        shift = (p * len(x_refs) + k) * per_page
        part = pltpu.roll(sums[k * rows:(k + 1) * rows], shift, 1)
        acc = part if acc is None else acc + part
    acc = acc * (1.0 / NSA_BLOCK)
    for pl_i in range(planes):
        o_ref[0, pl_i] = o_ref[0, pl_i] + acc[pl_i * NSA_DH:(pl_i + 1) * NSA_DH]


def _nsa_s_means(cache_t, page_table):
    db, n_pages = page_table.shape
    page = cache_t.shape[-1]
    nblk = n_pages * page // NSA_BLOCK
    kp = MEANS_PAGES
    assert n_pages % kp == 0

    def page_spec(k):
        return pl.BlockSpec((1, 2, NSA_KV_GROUPS, NSA_DH, page),
                            lambda b, p, pt: (pt[b * n_pages + p * kp + k], 0, 0, 0, 0))

    return pl.pallas_call(
        _nsa_s_means_kernel,
        grid_spec=pltpu.PrefetchScalarGridSpec(
            num_scalar_prefetch=1,
            grid=(db, n_pages // kp),
            in_specs=[page_spec(k) for k in range(kp)],
            out_specs=pl.BlockSpec((1, 2 * NSA_KV_GROUPS, NSA_DH, nblk), lambda b, p, pt: (b, 0, 0, 0))),
        out_shape=jax.ShapeDtypeStruct((db, 2 * NSA_KV_GROUPS, NSA_DH, nblk), f32),
        compiler_params=_cparams(("parallel", "arbitrary")),
        name="nsa_sample_means",
    )(page_table.reshape(-1), *([cache_t] * kp))


def _nsa_s_scores_kernel(q_ref, kcvc_ref, win_ref, kvn_ref, tc_ref, tw_ref, b0_ref, oc_ref, ow_ref, sel_ref):
    dh, hpg, ng = NSA_DH, NSA_HPG, NSA_KV_GROUPS
    sb = q_ref.shape[0]
    nblk = kcvc_ref.shape[-1]
    row = lax.broadcasted_iota(i32, (NSA_HEADS, 1), 0)
    in_g = [(row >= g * hpg) & (row < (g + 1) * hpg) for g in range(ng)]
    pairs = [(b, g) for b in range(sb) for g in range(ng)]
    q = [q_ref[b] * (dh ** -0.5) for b in range(sb)]

    s_c = [_mm3(q[b], kcvc_ref[b, g]) + tc_ref[...] for b, g in pairs]
    p_c = []
    for s in s_c:
        p = jnp.exp(s - jnp.max(s, axis=-1, keepdims=True))
        p_c.append(p / jnp.maximum(jnp.sum(p, axis=-1, keepdims=True), 1e-30))
    o_c = [_mm_nt(p_c[k], kcvc_ref[b, ng + g]) for k, (b, g) in enumerate(pairs)]
    imp = jnp.concatenate([jnp.sum(jnp.where(in_g[g], p_c[k], 0.0), axis=0, keepdims=True)
                           for k, (b, g) in enumerate(pairs)], axis=0)
    for b in range(sb):
        oc_ref[b] = jnp.where(in_g[0], o_c[b * ng], o_c[b * ng + 1])

    s_w = [_mm3(q[b], win_ref[b, 0, g]) + tw_ref[...] for b, g in pairs]
    o_w = []
    for k, (b, g) in enumerate(pairs):
        kvn = kvn_ref[b]
        s_n = jnp.sum(q[b] * kvn[:, g * dh:(g + 1) * dh], axis=-1, keepdims=True) + b0_ref[...]
        m_w = jnp.maximum(jnp.max(s_w[k], axis=-1, keepdims=True), s_n)
        p_w = jnp.exp(s_w[k] - m_w)
        p_n = jnp.exp(s_n - m_w)
        l_w = jnp.sum(p_w, axis=-1, keepdims=True) + p_n
        v_n = kvn[:, (ng + g) * dh:(ng + g + 1) * dh]
        o_w.append((_mm_nt(p_w, win_ref[b, 1, g]) + p_n * v_n) / jnp.maximum(l_w, 1e-30))
    for b in range(sb):
        ow_ref[b] = jnp.where(in_g[0], o_w[b * ng], o_w[b * ng + 1])

    lane = lax.broadcasted_iota(i32, imp.shape, 1)
    n_sel = NSA_TOPN - 1
    forced_blocks = [0] + [nblk - k for k in range(1, NSA_LOCAL)]
    imp = jnp.where((lane == 0) | (lane > nblk - NSA_LOCAL), -jnp.inf, imp)
    picks = jnp.zeros(imp.shape, i32)
    for it, fb in enumerate(forced_blocks):
        picks = jnp.where(lane == it, fb, picks)
    for it in range(len(forced_blocks), n_sel):
        mx = jnp.max(imp, axis=-1, keepdims=True)
        first = jnp.min(jnp.where(imp == mx, lane, nblk), axis=-1, keepdims=True)
        picks = jnp.where(lane == it, first, picks)
        imp = jnp.where(lane == first, -jnp.inf, imp)
    for b in range(sb):
        sel_ref[b] = picks[b * ng:(b + 1) * ng]


def _nsa_s_sel_kernel(sel_ref, pt_ref, q_ref, rb_ref, kvn_ref, oc_ref, ow_ref, gt_ref, *refs, n_sel, past):
    del pt_ref
    k_refs, v_refs, o_ref = refs[:n_sel], refs[n_sel:2 * n_sel], refs[2 * n_sel]
    b = pl.program_id(0)
    g = pl.program_id(1)
    dh, hpg = NSA_DH, NSA_HPG
    is_g0 = g == 0
    page = k_refs[0].shape[-1]
    per_page = page // NSA_BLOCK
    q = q_ref[0] * (dh ** -0.5)
    kvn = kvn_ref[0]
    lane = lax.broadcasted_iota(i32, (NSA_HEADS, page), 1)
    s_parts, d_parts = [], []
    for j in range(n_sel):
        nb = sel_ref[(b * NSA_KV_GROUPS + g) * n_sel + j]
        in_blk = (lane >> int(math.log2(NSA_BLOCK))) == (nb % per_page)
        d_parts.append(jnp.where(in_blk, past - ((nb // per_page) * page + lane), -1))
        s_parts.append(_mm(q, k_refs[j][0, 0, 0]))
    d = jnp.concatenate(d_parts, axis=1)
    bias = jnp.broadcast_to(rb_ref[0], d.shape)
    for k in range(1, REL_BUCKETS):
        bias = jnp.where(d >= _BUCKET_THR[k - 1], rb_ref[k], bias)
    s = jnp.concatenate(s_parts, axis=1) + jnp.where(d < 0, NEG, bias)
    k_n = jnp.where(is_g0, kvn[:, 4 * dh:5 * dh], kvn[:, 5 * dh:6 * dh])
    v_n = jnp.where(is_g0, kvn[:, 6 * dh:7 * dh], kvn[:, 7 * dh:8 * dh])
    s_n = jnp.sum(q * k_n, axis=-1, keepdims=True) + rb_ref[0]
    m = jnp.maximum(jnp.max(s, axis=-1, keepdims=True), s_n)
    p = jnp.exp(s - m)
    p_n = jnp.exp(s_n - m)
    l = jnp.sum(p, axis=-1, keepdims=True) + p_n
    acc = p_n * v_n
    for j in range(n_sel):
        acc = acc + _mm_nt(p[:, j * page:(j + 1) * page], v_refs[j][0, 0, 0])
    o_s = acc / jnp.maximum(l, 1e-30)
    gt = gt_ref[0]
    o = gt[:, 0:1] * oc_ref[0] + gt[:, 1:2] * o_s + gt[:, 2:3] * ow_ref[0]
    row = lax.broadcasted_iota(i32, o.shape, 0)
    in_g = (row >= g * hpg) & (row < (g + 1) * hpg)

    @pl.when(is_g0)
    def _():
        o_ref[0] = o

    @pl.when(jnp.logical_not(is_g0))
    def _():
        o_ref[0] = jnp.where(in_g, o, o_ref[0])


def _nsa_sample(q_b, gates, kv4_new, kvwin_new, cache_kv, cache_win, page_table, rel_bias):
    db = q_b.shape[0]
    dh = NSA_DH
    n_pool, page = cache_kv.shape[:2]
    n_pages = page_table.shape[1]
    past = n_pages * page
    nblk = past // NSA_BLOCK
    wb = cache_win.shape[1]
    assert nblk == PAIR and page % NSA_BLOCK == 0 and wb == NSA_WINDOW and nblk > NSA_LOCAL
    cache_t = jnp.transpose(cache_kv, (0, 2, 3, 4, 1)).astype(f32)
    win_t = jnp.transpose(cache_win, (0, 2, 3, 4, 1)).astype(f32)
    kcvc = _nsa_s_means(cache_t, page_table)

    n = np.arange(nblk)
    d_cmp = past - (n * NSA_BLOCK + NSA_BLOCK - 1)
    jw = np.arange(wb)
    d_win = np.where(jw >= 1, wb - jw, -1)
    dist = np.broadcast_to(np.concatenate([d_cmp, d_win])[None, :], (NSA_HEADS, nblk + wb))
    tab = _bias_table(rel_bias, jnp.asarray(dist[None, None], i32), slab=1)[0, 0]
    t_cmp, t_win = tab[:, :nblk], tab[:, nblk:]
    rb = rel_bias.astype(f32)
    b0 = rb[0].reshape(NSA_HEADS, 1)

    q3 = q_b.reshape(db, NSA_HEADS, dh)
    sb = max(s for s in (8, 4, 2, 1) if db % s == 0)
    o_c, o_w, sel = pl.pallas_call(
        _nsa_s_scores_kernel,
        grid=(db // sb,),
        in_specs=[pl.BlockSpec((sb, NSA_HEADS, dh), lambda b: (b, 0, 0)),
                  pl.BlockSpec((sb, 2 * NSA_KV_GROUPS, dh, nblk), lambda b: (b, 0, 0, 0)),
                  pl.BlockSpec((sb, 2, NSA_KV_GROUPS, dh, wb), lambda b: (b, 0, 0, 0, 0)),
                  pl.BlockSpec((sb, 1, 4 * dh), lambda b: (b, 0, 0)),
                  pl.BlockSpec((NSA_HEADS, nblk), lambda b: (0, 0)),
                  pl.BlockSpec((NSA_HEADS, wb), lambda b: (0, 0)),
                  pl.BlockSpec((NSA_HEADS, 1), lambda b: (0, 0))],
        out_specs=[pl.BlockSpec((sb, NSA_HEADS, dh), lambda b: (b, 0, 0)),
                   pl.BlockSpec((sb, NSA_HEADS, dh), lambda b: (b, 0, 0)),
                   pl.BlockSpec((sb, NSA_KV_GROUPS, nblk), lambda b: (b, 0, 0))],
        out_shape=[jax.ShapeDtypeStruct((db, NSA_HEADS, dh), f32),
                   jax.ShapeDtypeStruct((db, NSA_HEADS, dh), f32),
                   jax.ShapeDtypeStruct((db, NSA_KV_GROUPS, nblk), i32)],
        compiler_params=_cparams(("parallel",)),
        name="nsa_sample_scores",
    )(q3, kcvc, win_t, kvwin_new.reshape(db, 1, 4 * dh), t_cmp, t_win, b0)

    n_sel = NSA_TOPN - 1
    sel_flat = sel[:, :, :n_sel].reshape(-1)
    per_page = page // NSA_BLOCK

    def page_spec(j, plane):
        def imap(b, g, s, p):
            nb = s[(b * NSA_KV_GROUPS + g) * n_sel + j]
            return (p[b * n_pages + nb // per_page], plane, g, 0, 0)
        return pl.BlockSpec((1, 1, 1, dh, page), imap)

    const = lambda shape: pl.BlockSpec(shape, lambda b, g, s, p: (0,) * len(shape))
    per_b = lambda shape: pl.BlockSpec((1,) + shape, lambda b, g, s, p: (b,) + (0,) * len(shape))
    o = pl.pallas_call(
        functools.partial(_nsa_s_sel_kernel, n_sel=n_sel, past=past),
        grid_spec=pltpu.PrefetchScalarGridSpec(
            num_scalar_prefetch=2,
            grid=(db, NSA_KV_GROUPS),
            in_specs=[per_b((NSA_HEADS, dh)), const((REL_BUCKETS, NSA_HEADS, 1)), per_b((1, 8 * dh)),
                      per_b((NSA_HEADS, dh)), per_b((NSA_HEADS, dh)), per_b((NSA_HEADS, 3))]
                     + [page_spec(j, 2) for j in range(n_sel)] + [page_spec(j, 3) for j in range(n_sel)],
            out_specs=per_b((NSA_HEADS, dh))),
        out_shape=jax.ShapeDtypeStruct((db, NSA_HEADS, dh), f32),
        compiler_params=_cparams(("parallel", "arbitrary")),
        name="nsa_sample_selected",
    )(sel_flat, page_table.reshape(-1), q3, rb.reshape(REL_BUCKETS, NSA_HEADS, 1), kv4_new.reshape(db, 1, 8 * dh),
      o_c, o_w, gates, *([cache_t] * (2 * n_sel)))
    return o.reshape(db, NSA_HEADS * dh)


def _gelu_tanh(x):
    return 0.5 * x * (1.0 + jnp.tanh(math.sqrt(2.0 / math.pi) * (x + 0.044715 * (x * x * x))))


def _lru_gates(xc, wa_ref, wx_ref, ba_ref, bx_ref, lam_ref, prec):
    r_parts, i_parts = [], []
    for n in range(RNN_BLOCKS):
        xb = xc[:, n * RNN_BW:(n + 1) * RNN_BW]
        r_parts.append(_dotp(xb, wa_ref[n], prec))
        i_parts.append(_dotp(xb, wx_ref[n], prec))
    r = jax.nn.sigmoid(jnp.concatenate(r_parts, axis=1) + ba_ref[...])
    i = jax.nn.sigmoid(jnp.concatenate(i_parts, axis=1) + bx_ref[...])
    log_a = -RG_C * r * _softplus(-lam_ref[...])
    a = jnp.exp(log_a)
    t = jnp.tanh(log_a)
    b = jnp.sqrt(jnp.maximum(-2.0 * t / (1.0 - t), 0.0)) * (i * xc)
    return a, b


def _lru_kernel(rec_ref, gate_ref, cw_ref, cb_ref, wa_ref, wx_ref, ba_ref, bx_ref, lam_ref, cinit_ref, h0_ref,
                y_ref, hfin_ref, xc_scr, a_scr, b_scr, hs_scr, h_scr, *, tb):
    j = pl.program_id(1)

    @pl.when(j == 0)
    def _():
        xc_scr[0:8, :] = cinit_ref[0]
        h_scr[...] = h0_ref[0]

    x = rec_ref[0]
    xc_scr[8:8 + tb, :] = x
    xc = xc_scr[5:5 + tb, :] * cw_ref[0:1, :]
    xc = xc + xc_scr[6:6 + tb, :] * cw_ref[1:2, :]
    xc = xc + xc_scr[7:7 + tb, :] * cw_ref[2:3, :]
    xc = xc + x * cw_ref[3:4, :]
    xc = xc + cb_ref[...]
    xc_scr[0:8, :] = xc_scr[tb:tb + 8, :]
    a, b = _lru_gates(xc, wa_ref, wx_ref, ba_ref, bx_ref, lam_ref, 1)
    a_scr[...] = a
    b_scr[...] = b

    def step(t, h):
        h = a_scr[pl.ds(t, 1), :] * h + b_scr[pl.ds(t, 1), :]
        hs_scr[pl.ds(t, 1), :] = h
        return h

    h = lax.fori_loop(0, tb, step, h_scr[...], unroll=8)
    h_scr[...] = h
    y_ref[0] = (_gelu_tanh(gate_ref[0]) * hs_scr[...]).astype(y_ref.dtype)

    @pl.when(j == pl.num_programs(1) - 1)
    def _():
        hfin_ref[0] = h


def _lru_prompt(rec, gate, conv_w, conv_b, wa, wx, ba, bx, lam, *, tb):
    bsz, t, w = rec.shape
    row = lambda a: a.reshape(1, w).astype(f32)
    cinit = jnp.zeros((bsz, 8, w), f32)
    h0 = jnp.zeros((bsz, 1, w), f32)
    full = lambda shape: pl.BlockSpec(shape, lambda b, j: (0,) * len(shape))
    return pl.pallas_call(
        functools.partial(_lru_kernel, tb=tb),
        grid=(bsz, t // tb),
        in_specs=[pl.BlockSpec((1, tb, w), lambda b, j: (b, j, 0)),
                  pl.BlockSpec((1, tb, w), lambda b, j: (b, j, 0)),
                  full((RNN_CONV, w)), full((1, w)),
                  full((RNN_BLOCKS, RNN_BW, RNN_BW)), full((RNN_BLOCKS, RNN_BW, RNN_BW)),
                  full((1, w)), full((1, w)), full((1, w)),
                  pl.BlockSpec((1, 8, w), lambda b, j: (b, 0, 0)),
                  pl.BlockSpec((1, 1, w), lambda b, j: (b, 0, 0))],
        out_specs=[pl.BlockSpec((1, tb, w), lambda b, j: (b, j, 0)),
                   pl.BlockSpec((1, 1, w), lambda b, j: (b, 0, 0))],
        out_shape=[jax.ShapeDtypeStruct((bsz, t, w), bf16), jax.ShapeDtypeStruct((bsz, 1, w), f32)],
        scratch_shapes=[pltpu.VMEM((tb + 8, w), f32), pltpu.VMEM((tb, w), f32), pltpu.VMEM((tb, w), f32),
                        pltpu.VMEM((tb, w), f32), pltpu.VMEM((1, w), f32)],
        compiler_params=_cparams(("parallel", "arbitrary")),
        name="rglru",
    )(rec, gate, conv_w.astype(f32), row(conv_b), wa.astype(f32), wx.astype(f32), row(ba), row(bx), row(lam),
      cinit, h0)


def _lru_step_kernel(rec_ref, gate_ref, b0_ref, b1_ref, b2_ref, cw_ref, cb_ref, wa_ref, wx_ref, ba_ref, bx_ref,
                     lam_ref, h0_ref, y_ref, h_ref):
    xc = b0_ref[...] * cw_ref[0:1, :]
    xc = xc + b1_ref[...] * cw_ref[1:2, :]
    xc = xc + b2_ref[...] * cw_ref[2:3, :]
    xc = xc + rec_ref[...] * cw_ref[3:4, :]
    xc = xc + cb_ref[...]
    a, b = _lru_gates(xc, wa_ref, wx_ref, ba_ref, bx_ref, lam_ref, 3)
    h = a * h0_ref[...] + b
    h_ref[...] = h
    y_ref[...] = _gelu_tanh(gate_ref[...]) * h


def _lru_sample(rec, gate, conv_buf, h0, conv_w, conv_b, wa, wx, ba, bx, lam):
    db, w = rec.shape
    row = lambda a: a.reshape(1, w).astype(f32)
    buf = conv_buf.astype(f32)
    return pl.pallas_call(
        _lru_step_kernel,
        out_shape=[jax.ShapeDtypeStruct((db, w), f32), jax.ShapeDtypeStruct((db, w), f32)],
        compiler_params=pltpu.CompilerParams(vmem_limit_bytes=VMEM_LIMIT),
        name="rglru_step",
    )(rec, gate, buf[:, 0], buf[:, 1], buf[:, 2], conv_w.astype(f32), row(conv_b), wa.astype(f32), wx.astype(f32),
      row(ba), row(bx), row(lam), h0.astype(f32))


def _top2_kernel(l_ref, e_ref, g_ref, cnt_ref, cnt_scr):
    @pl.when(pl.program_id(0) == 0)
    def _():
        cnt_scr[...] = jnp.zeros_like(cnt_scr)

    lg = l_ref[...]
    tm, w = lg.shape
    lane = lax.broadcasted_iota(i32, lg.shape, 1)
    lg = jnp.where(lane < N_EXPERTS, lg, -jnp.inf)
    m1 = jnp.max(lg, axis=-1, keepdims=True)
    i1 = jnp.min(jnp.where(lg == m1, lane, w), axis=-1, keepdims=True)
    lg2 = jnp.where(lane == i1, -jnp.inf, lg)
    m2 = jnp.max(lg2, axis=-1, keepdims=True)
    i2 = jnp.min(jnp.where(lg2 == m2, lane, w), axis=-1, keepdims=True)
    e2 = jnp.exp(m2 - m1)
    den = 1.0 + e2
    g_ref[...] = jnp.where(lane == 0, 1.0 / den, jnp.where(lane == 1, e2 / den, 0.0))
    hit1, hit2 = lane == i1, lane == i2
    routed = (hit1 | hit2).astype(bf16)
    r_i = lax.broadcasted_iota(i32, (tm, tm), 0)
    c_i = lax.broadcasted_iota(i32, (tm, tm), 1)
    before = jnp.dot((c_i < r_i).astype(bf16), routed, preferred_element_type=f32) + cnt_scr[...]
    rank1 = jnp.sum(jnp.where(hit1, before, 0.0), axis=-1, keepdims=True).astype(i32)
    rank2 = jnp.sum(jnp.where(hit2, before, 0.0), axis=-1, keepdims=True).astype(i32)
    e_ref[...] = jnp.where(lane == 0, i1, jnp.where(lane == 1, i2, jnp.where(lane == 2, rank1,
                                                                               jnp.where(lane == 3, rank2, 0))))
    cnt_scr[...] += jnp.sum(routed.astype(f32), axis=0, keepdims=True)
    cnt_ref[...] = cnt_scr[...].astype(i32)


def _top2(logits, *, tm):
    n, w = logits.shape
    return pl.pallas_call(
        _top2_kernel,
        grid=(n // tm,),
        in_specs=[pl.BlockSpec((tm, w), lambda i: (i, 0))],
        out_specs=[pl.BlockSpec((tm, w), lambda i: (i, 0)), pl.BlockSpec((tm, w), lambda i: (i, 0)),
                   pl.BlockSpec((1, w), lambda i: (0, 0))],
        out_shape=[jax.ShapeDtypeStruct((n, w), i32), jax.ShapeDtypeStruct((n, w), f32),
                   jax.ShapeDtypeStruct((1, w), i32)],
        scratch_shapes=[pltpu.VMEM((1, w), f32)],
        compiler_params=_cparams(("arbitrary",)),
        name="moe_top2",
    )(logits)


def _moe_kernel(be_ref, nu_ref, nv_ref, xs_ref, w1_ref, w3_ref, w2_ref, o_ref, acc_scr):
    del be_ref
    i = pl.program_id(0)
    f = pl.program_id(1)
    last = pl.num_programs(1) - 1
    used = i < nu_ref[0]
    half = xs_ref.shape[0] // 2
    top_only = nv_ref[i] <= half

    @pl.when(used & (f == 0))
    def _():
        acc_scr[...] = jnp.zeros_like(acc_scr)

    def update(rows):
        x = xs_ref[rows, :]
        act = _silu(_mm(x, w1_ref[0])) * _mm(x, w3_ref[0])
        acc_scr[rows, :] += _mm(act, w2_ref[0])

    @pl.when(used & jnp.logical_not(top_only))
    def _():
        update(slice(None))

    @pl.when(used & top_only)
    def _():
        update(slice(0, half))

    @pl.when(used & (f == last))
    def _():
        o_ref[...] = acc_scr[...]

    @pl.when(jnp.logical_not(used) & (f == last))
    def _():
        o_ref[...] = jnp.zeros_like(o_ref)


def _moe_experts(xs, blk_e, n_used, n_valid, w1, w3, w2, *, tm, tf):
    rows, d = xs.shape
    fdim = w1.shape[2]
    nf = fdim // tf

    def wcol(i, f, be, nu, nv):
        return (be[i], 0, jnp.where(i < nu[0], f, nf - 1))

    def wrow(i, f, be, nu, nv):
        return (be[i], jnp.where(i < nu[0], f, nf - 1), 0)

    return pl.pallas_call(
        _moe_kernel,
        grid_spec=pltpu.PrefetchScalarGridSpec(
            num_scalar_prefetch=3,
            grid=(rows // tm, nf),
            in_specs=[pl.BlockSpec((tm, d), lambda i, f, be, nu, nv: (i, 0)),
                      pl.BlockSpec((1, d, tf), wcol),
                      pl.BlockSpec((1, d, tf), wcol),
                      pl.BlockSpec((1, tf, d), wrow)],
            out_specs=pl.BlockSpec((tm, d), lambda i, f, be, nu, nv: (i, 0)),
            scratch_shapes=[pltpu.VMEM((tm, d), f32)]),
        out_shape=jax.ShapeDtypeStruct((rows, d), f32),
        compiler_params=_cparams(("arbitrary", "arbitrary")),
        name="moe_experts",
    )(blk_e, n_used, n_valid, xs, w1, w3, w2)


def _combine_kernel(x_ref, gate_ref, y0_ref, y1_ref, g_ref, nw_ref, o_ref):
    gw = g_ref[...]
    y = y0_ref[...] * gw[:, 0:1] + y1_ref[...] * gw[:, 1:2]
    x = x_ref[0] + gate_ref[0] * y
    o_ref[0] = x * lax.rsqrt(jnp.mean(x * x, axis=-1, keepdims=True) + EPS) * nw_ref[...]


def _moe_combine(x, gate, y0, y1, gw, norm_w, *, tm, row0):
    bsz, t, d = x.shape
    assert row0 % tm == 0
    tok = pl.BlockSpec((1, tm, d), lambda b, i: (b, i, 0))
    flat = lambda w: pl.BlockSpec((tm, w), lambda b, i: (row0 // tm + b * (t // tm) + i, 0))
    return pl.pallas_call(
        _combine_kernel,
        grid=(bsz, t // tm),
        in_specs=[tok, _mod_spec(gate, tm), flat(d), flat(d), flat(gw.shape[-1]),
                  pl.BlockSpec((1, d), lambda b, i: (0, 0))],
        out_specs=tok,
        out_shape=jax.ShapeDtypeStruct((bsz, t, d), f32),
        compiler_params=_cparams(("parallel", "parallel")),
        name="moe_combine",
    )(x, gate, y0, y1, gw, norm_w.reshape(1, d).astype(f32))


MOE_TM = 1024
MOE_TF = 512


def _moe_dispatch(e_idx, rank, counts, n_tok):
    tm = MOE_TM
    n_assign = n_tok * TOP_K
    padded = (counts + tm - 1) // tm * tm
    pend = jnp.cumsum(padded)
    pstart = pend - padded
    experts = jnp.arange(N_EXPERTS, dtype=i32)
    dest = jnp.sum(jnp.where(e_idx[:, :, None] == experts, pstart, 0), axis=-1) + rank
    n_blocks = -(-n_assign // tm) + N_EXPERTS
    rows = n_blocks * tm
    tok = jnp.broadcast_to(jnp.arange(n_tok, dtype=i32)[:, None], (n_tok, TOP_K))
    row_tok = (jnp.arange(rows, dtype=i32) % n_tok).at[dest.reshape(-1)].set(
        tok.reshape(-1), unique_indices=True, mode='promise_in_bounds')
    n_used = (pend[-1] // tm).astype(i32)
    blk = jnp.minimum(jnp.arange(n_blocks, dtype=i32), n_used - 1) * tm
    blk_e = jnp.minimum(jnp.sum((blk[:, None] >= pend[None, :]).astype(i32), axis=1), N_EXPERTS - 1)
    is_e = blk_e[:, None] == experts[None, :]
    first_row = jnp.sum(jnp.where(is_e, pstart, 0), axis=1)
    n_valid = jnp.clip(jnp.sum(jnp.where(is_e, counts, 0), axis=1) - (blk - first_row), 0, tm)
    return dest, row_tok, blk_e, n_used.reshape(1), n_valid.astype(i32)


def _w_in0_layout(w_in0):
    c_qkv = GDN_CONV_CH
    c_z = c_qkv + GDN_HEADS * GDN_DV
    c_ab = c_z + 2 * GDN_HEADS
    c_q = c_ab + NSA_HEADS * NSA_DH
    c_kv = c_q + 6 * NSA_KV_GROUPS * NSA_DH
    c_g = c_kv + 3 * NSA_HEADS
    assert c_g == w_in0.shape[1]
    n_misc = 2 * GDN_HEADS + 3 * NSA_HEADS
    w = jnp.concatenate([w_in0[:, :c_z], w_in0[:, c_ab:c_kv], w_in0[:, c_z:c_ab], w_in0[:, c_kv:c_g],
                         jnp.zeros((w_in0.shape[0], MISC_W - n_misc), w_in0.dtype)], axis=1)
    widths = (GDN_CONV_CH, GDN_HEADS * GDN_DV, NSA_HEADS * NSA_DH, 4 * NSA_KV_GROUPS * NSA_DH,
              2 * NSA_KV_GROUPS * NSA_DH, MISC_W)
    splits, s = [], 0
    for wd in widths:
        splits.append((s, s + wd))
        s += wd
    return w, tuple(splits)


def kernel(x_prompt, x_sample, c_prompt, c_sample, cache_nsa_kv, cache_nsa_win, state_gdn, state_gdn_conv, state_lru, state_lru_conv, page_table, rel_bias, w_ada, b_ada, norm_mix, norm_ffn, norm_final, w_in0, gdn_conv_w, gdn_a_log, gdn_dt_bias, gdn_norm_w, w_out0, ffn_w_gate, ffn_w_up, ffn_w_down, w_in1, lru_conv_w, lru_conv_b, lru_wa, lru_ba, lru_wx, lru_bx, lru_lambda, w_out1, moe_router, moe_w1, moe_w3, moe_w2):
    bsz, seq, d = x_prompt.shape
    db = x_sample.shape[0]
    assert x_sample.shape[1] == 1
    dh = NSA_DH

    n_c = bsz + db
    n_c_pad = -(-n_c // 8) * 8
    c_all = jnp.concatenate([c_prompt, c_sample, jnp.zeros((n_c_pad - n_c, d), f32)], axis=0)
    mods = _adaln(c_all, w_ada, b_ada).reshape(2, n_c_pad, N_MOD, d)
    mod_p = [[mods[l, :bsz, k].reshape(bsz, 1, d) for k in range(N_MOD)] for l in range(2)]
    mod_s = [[mods[l, bsz:n_c, k].reshape(1, db, d) for k in range(N_MOD)] for l in range(2)]

    w0, splits0 = _w_in0_layout(w_in0)
    splits1 = ((0, RNN_WIDTH), (RNN_WIDTH, 2 * RNN_WIDTH))
    router = jnp.concatenate([moe_router, jnp.zeros((d, MISC_W - N_EXPERTS), f32)], axis=1)
    bf = lambda w: w.astype(bf16)

    tm = PROMPT_TM
    xp = x_prompt
    qkv, z, q_b, kv4, kvwin, misc = _mod_matmul(xp, norm_mix[0], mod_p[0][0], mod_p[0][1], bf(w0), splits0,
                                                tm=tm, prec=1)
    o_a, p_gdn = _gdn(qkv, z, misc, gdn_conv_w, jnp.zeros((bsz, GDN_CONV - 1, GDN_CONV_CH), f32), gdn_a_log,
                      gdn_dt_bias, gdn_norm_w, jnp.zeros((bsz, GDN_HEADS, GDN_DK, GDN_DV), f32), tb=SCAN_TB, n_valid=seq)
    p_gdn_conv = qkv[:, seq - (GDN_CONV - 1):]
    o_b = _nsa_prompt(q_b, misc, kv4, kvwin, rel_bias)
    p_nsa_kv = kv4.reshape(bsz, seq, 4, NSA_KV_GROUPS, dh)
    keep = min(NSA_WINDOW, seq)
    p_nsa_win = kvwin[:, seq - keep:].reshape(bsz, keep, 2, NSA_KV_GROUPS, dh)
    xp = _proj_residual([o_a, o_b], bf(w_out0), xp, mod_p[0][2], tm=tm, prec=1)
    xp = _ffn(xp, norm_ffn[0], mod_p[0][3], mod_p[0][4], mod_p[0][5], bf(ffn_w_gate), bf(ffn_w_up), bf(ffn_w_down),
              tm=tm, tf=FFN_TF_PROMPT, prec=1)
    gate_br, rec_br = _mod_matmul(xp, norm_mix[1], mod_p[1][0], mod_p[1][1], bf(w_in1), splits1, tm=tm, prec=1)
    y_in, p_lru = _lru_prompt(rec_br, gate_br, lru_conv_w, lru_conv_b, lru_wa, lru_wx, lru_ba, lru_bx, lru_lambda,
                              tb=SCAN_TB)
    p_lru_conv = rec_br[:, seq - (RNN_CONV - 1):]
    xp = _proj_residual([y_in], bf(w_out1), xp, mod_p[1][2], tm=tm, prec=1)
    logit_p, h_p = _mod_matmul(xp, norm_ffn[1], mod_p[1][3], mod_p[1][4], router, ((0, MISC_W),), tm=tm, prec=3,
                               emit_h=f32)

    xs = x_sample.reshape(1, db, d)
    qkv_s, z_s, q_s, kv4_s, kvwin_s, misc_s = _mod_matmul(xs, norm_mix[0], mod_s[0][0], mod_s[0][1], w0, splits0,
                                                          tm=db, prec=3)
    per_seq = lambda a: a.reshape(db, 1, a.shape[-1])
    o_a_s, s_gdn = _gdn_step(per_seq(qkv_s), per_seq(z_s), per_seq(misc_s), gdn_conv_w, state_gdn_conv, gdn_a_log,
                             gdn_dt_bias, gdn_norm_w, state_gdn)
    o_a_s = o_a_s.reshape(1, db, GDN_HEADS * GDN_DV)
    s_gdn_conv = jnp.concatenate([state_gdn_conv[:, 1:], qkv_s.reshape(db, 1, GDN_CONV_CH)], axis=1)
    gates_s = jax.nn.sigmoid(misc_s[0, :, 2 * GDN_HEADS:2 * GDN_HEADS + 3 * NSA_HEADS]).reshape(db, NSA_HEADS, 3)
    o_b_s = _nsa_sample(q_s[0], gates_s, kv4_s[0], kvwin_s[0], cache_nsa_kv, cache_nsa_win, page_table, rel_bias)
    s_nsa_kv = kv4_s.reshape(db, 1, 4, NSA_KV_GROUPS, dh)
    s_nsa_win = jnp.concatenate([cache_nsa_win[:, 1:],
                                 kvwin_s.reshape(db, 1, 2, NSA_KV_GROUPS, dh).astype(cache_nsa_win.dtype)], axis=1)
    xs = _proj_residual([o_a_s, o_b_s.reshape(1, db, NSA_HEADS * dh)], w_out0, xs, mod_s[0][2], tm=db, prec=3)
    xs = _ffn(xs, norm_ffn[0], mod_s[0][3], mod_s[0][4], mod_s[0][5], ffn_w_gate, ffn_w_up, ffn_w_down,
              tm=db, tf=FFN_TF_SAMPLE, prec=3)
    gate_s, rec_s = _mod_matmul(xs, norm_mix[1], mod_s[1][0], mod_s[1][1], w_in1, splits1, tm=db, prec=3)
    y_in_s, s_lru = _lru_sample(rec_s[0], gate_s[0], state_lru_conv, state_lru, lru_conv_w, lru_conv_b, lru_wa, lru_wx,
                                lru_ba, lru_bx, lru_lambda)
    s_lru_conv = jnp.concatenate([state_lru_conv[:, 1:], rec_s.reshape(db, 1, RNN_WIDTH)], axis=1)
    xs = _proj_residual([y_in_s.reshape(1, db, RNN_WIDTH)], w_out1, xs, mod_s[1][2], tm=db, prec=3)
    logit_s, h_s = _mod_matmul(xs, norm_ffn[1], mod_s[1][3], mod_s[1][4], router, ((0, MISC_W),), tm=db, prec=3,
                               emit_h=f32)

    n_p = bsz * seq
    n_tok = n_p + db
    logits = jnp.concatenate([logit_p.reshape(n_p, MISC_W), logit_s.reshape(db, MISC_W)], axis=0)
    h_all = jnp.concatenate([h_p.reshape(n_p, d), h_s.reshape(db, d)], axis=0)
    route, gw, cnt = _top2(logits, tm=max(t for t in range(8, 1025, 8) if n_tok % t == 0))
    dest, row_tok, blk_e, n_used, n_valid = _moe_dispatch(route[:, :TOP_K], route[:, TOP_K:2 * TOP_K],
                                                          cnt[0, :N_EXPERTS], n_tok)
    yb = _moe_experts(h_all[row_tok], blk_e, n_used, n_valid, moe_w1, moe_w3, moe_w2, tm=MOE_TM, tf=MOE_TF)
    y0, y1 = yb[dest[:, 0]], yb[dest[:, 1]]
    y_prompt = _moe_combine(xp, mod_p[1][5], y0, y1, gw, norm_final, tm=tm, row0=0)
    y_sample = _moe_combine(xs, mod_s[1][5], y0, y1, gw, norm_final, tm=db, row0=n_p)

    return (y_prompt, y_sample.reshape(db, 1, d),
            p_nsa_kv, p_nsa_win, p_gdn, p_gdn_conv, p_lru.reshape(bsz, RNN_WIDTH), p_lru_conv,
            s_nsa_kv, s_nsa_win, s_gdn, s_gdn_conv, s_lru, s_lru_conv)
```

```python
import functools
import math

import numpy as np
import jax
import jax.numpy as jnp
from jax import lax
from jax.experimental import pallas as pl
from jax.experimental.pallas import tpu as pltpu

f32 = jnp.float32
bf16 = jnp.bfloat16
i32 = jnp.int32

D_MODEL = 1024
EPS = 1e-6
N_MOD = 6
GDN_HEADS = 4
GDN_DK = 128
GDN_DV = 128
GDN_CONV = 4
GDN_CHUNK = 64
GDN_CONV_CH = GDN_HEADS * (2 * GDN_DK + GDN_DV)
NSA_HEADS = 8
NSA_KV_GROUPS = 2
NSA_HPG = NSA_HEADS // NSA_KV_GROUPS
NSA_DH = 64
NSA_BLOCK = 64
NSA_TOPN = 16
NSA_LOCAL = 2
NSA_WINDOW = 512
NSA_QBLOCK = 128
REL_BUCKETS = 32
REL_MAX_DIST = 2048
RNN_WIDTH = D_MODEL
RNN_BLOCKS = 8
RNN_BW = RNN_WIDTH // RNN_BLOCKS
RNN_CONV = 4
RG_C = 8.0
FFN_DIM = 2816
N_EXPERTS = 8
TOP_K = 2
EXPERT_DIM = 3584

NEG = -1e30
PAIR = 2 * NSA_BLOCK
NEAR_PAIRS = 13
MISC_W = 128
VMEM_LIMIT = 56 * 1024 * 1024
PROMPT_TM = 512
SCAN_TB = 256
FFN_TF_PROMPT = FFN_DIM // 2
FFN_TF_SAMPLE = 256
MEANS_TB = 512


def _cparams(sem, vmem=VMEM_LIMIT):
    return pltpu.CompilerParams(dimension_semantics=sem, vmem_limit_bytes=vmem)


def _mm(a, b):
    return jnp.dot(a.astype(bf16), b.astype(bf16), preferred_element_type=f32)


def _mm_nt(a, b):
    return lax.dot_general(a.astype(bf16), b.astype(bf16), (((1,), (1,)), ((), ())),
                           preferred_element_type=f32)


def _split2(a):
    hi = a.astype(bf16)
    lo = (a - hi.astype(f32)).astype(bf16)
    return hi, lo


def _mm3(a, b):
    ah, al = _split2(a)
    bh, bl = _split2(b)
    return (jnp.dot(ah, bh, preferred_element_type=f32) + jnp.dot(ah, bl, preferred_element_type=f32)
            + jnp.dot(al, bh, preferred_element_type=f32))


def _mm3_nt(a, b):
    ah, al = _split2(a)
    bh, bl = _split2(b)
    dn = (((1,), (1,)), ((), ()))
    return (lax.dot_general(ah, bh, dn, preferred_element_type=f32)
            + lax.dot_general(ah, bl, dn, preferred_element_type=f32)
            + lax.dot_general(al, bh, dn, preferred_element_type=f32))


def _mm_01(m01, a):
    hi = a.astype(bf16)
    r1 = a - hi.astype(f32)
    mid = r1.astype(bf16)
    lo = (r1 - mid.astype(f32)).astype(bf16)
    return (jnp.dot(m01, hi, preferred_element_type=f32) + jnp.dot(m01, mid, preferred_element_type=f32)
            + jnp.dot(m01, lo, preferred_element_type=f32))


def _dotp(a, b, prec):
    return _mm3(a, b) if prec == 3 else _mm(a, b)


def _silu(x):
    return x * jax.nn.sigmoid(x)


def _softplus(x):
    return jnp.maximum(x, 0.0) + jnp.log1p(jnp.exp(-jnp.abs(x)))


def _modulate(x, gain, shift, scale):
    r = lax.rsqrt(jnp.mean(x * x, axis=-1, keepdims=True) + EPS)
    return x * r * gain * (1.0 + scale) + shift


def _ada_kernel(c_ref, w_ref, b_ref, o_ref):
    o_ref[0] = _mm3(_silu(c_ref[...]), w_ref[0]) + b_ref[0]


def _adaln(c_all, w_ada, b_ada):
    rows = c_all.shape[0]
    depth, d, n = w_ada.shape
    tn = 1536
    return pl.pallas_call(
        _ada_kernel,
        grid=(depth, n // tn),
        in_specs=[pl.BlockSpec((rows, d), lambda l, j: (0, 0)),
                  pl.BlockSpec((1, d, tn), lambda l, j: (l, 0, j)),
                  pl.BlockSpec((1, 1, tn), lambda l, j: (l, 0, j))],
        out_specs=pl.BlockSpec((1, rows, tn), lambda l, j: (l, 0, j)),
        out_shape=jax.ShapeDtypeStruct((depth, rows, n), f32),
        compiler_params=_cparams(("arbitrary", "arbitrary")),
        name="adaln",
    )(c_all, w_ada, b_ada.reshape(depth, 1, n))


def _mod_spec(mod, tm):
    r = mod.shape[1]
    if r == 1:
        return pl.BlockSpec((1, 1, mod.shape[2]), lambda b, i: (b, 0, 0))
    return pl.BlockSpec((1, tm, mod.shape[2]), lambda b, i: (b, i, 0))


def _modmm_kernel(x_ref, gain_ref, shift_ref, scale_ref, w_ref, *o_refs, splits, prec, emit_h):
    h = _modulate(x_ref[0], gain_ref[...], shift_ref[0], scale_ref[0])
    if emit_h:
        o_refs[-1][0] = h.astype(o_refs[-1].dtype)
    hh = _split2(h) if prec == 3 else h.astype(bf16)
    for o_ref, (a, b) in zip(o_refs, splits):
        w = w_ref[:, a:b]
        if prec == 3:
            wh, wl = _split2(w)
            acc = (jnp.dot(hh[0], wh, preferred_element_type=f32) + jnp.dot(hh[0], wl, preferred_element_type=f32)
                   + jnp.dot(hh[1], wh, preferred_element_type=f32))
        else:
            acc = jnp.dot(hh, w, preferred_element_type=f32)
        o_ref[0] = acc


def _mod_matmul(x, gain, shift, scale, w, splits, *, tm, prec, emit_h=None):
    bsz, t, d = x.shape
    out_shape = [jax.ShapeDtypeStruct((bsz, t, b - a), f32) for a, b in splits]
    out_specs = [pl.BlockSpec((1, tm, b - a), lambda bi, i: (bi, i, 0)) for a, b in splits]
    if emit_h is not None:
        out_shape.append(jax.ShapeDtypeStruct((bsz, t, d), emit_h))
        out_specs.append(pl.BlockSpec((1, tm, d), lambda bi, i: (bi, i, 0)))
    return pl.pallas_call(
        functools.partial(_modmm_kernel, splits=tuple(splits), prec=prec, emit_h=emit_h is not None),
        grid=(bsz, t // tm),
        in_specs=[pl.BlockSpec((1, tm, d), lambda bi, i: (bi, i, 0)),
                  pl.BlockSpec((1, d), lambda bi, i: (0, 0)),
                  _mod_spec(shift, tm), _mod_spec(scale, tm),
                  pl.BlockSpec(w.shape, lambda bi, i: (0, 0))],
        out_specs=out_specs,
        out_shape=out_shape,
        compiler_params=_cparams(("parallel", "parallel")),
        name="mod_matmul",
    )(x, gain.reshape(1, d), shift, scale, w)


def _projres_kernel(*refs, n_lhs, ksplits, prec, final_norm):
    lhs = refs[:n_lhs]
    w_ref, x_ref, gate_ref = refs[n_lhs:n_lhs + 3]
    o_ref = refs[-1]
    acc = None
    for l_ref, (a, b) in zip(lhs, ksplits):
        part = _dotp(l_ref[0], w_ref[a:b, :], prec)
        acc = part if acc is None else acc + part
    y = x_ref[0] + gate_ref[0] * acc
    if final_norm:
        nw_ref = refs[n_lhs + 3]
        y = y * lax.rsqrt(jnp.mean(y * y, axis=-1, keepdims=True) + EPS) * nw_ref[...]
    o_ref[0] = y


def _proj_residual(lhs_list, w, x, gate, *, tm, prec, norm_w=None):
    bsz, t, d = x.shape
    ksplits, k0 = [], 0
    for l in lhs_list:
        ksplits.append((k0, k0 + l.shape[-1]))
        k0 += l.shape[-1]
    in_specs = [pl.BlockSpec((1, tm, l.shape[-1]), lambda bi, i: (bi, i, 0)) for l in lhs_list]
    in_specs += [pl.BlockSpec(w.shape, lambda bi, i: (0, 0)),
                 pl.BlockSpec((1, tm, d), lambda bi, i: (bi, i, 0)),
                 _mod_spec(gate, tm)]
    args = list(lhs_list) + [w, x, gate]
    if norm_w is not None:
        in_specs.append(pl.BlockSpec((1, d), lambda bi, i: (0, 0)))
        args.append(norm_w.reshape(1, d))
    return pl.pallas_call(
        functools.partial(_projres_kernel, n_lhs=len(lhs_list), ksplits=tuple(ksplits), prec=prec,
                          final_norm=norm_w is not None),
        grid=(bsz, t // tm),
        in_specs=in_specs,
        out_specs=pl.BlockSpec((1, tm, d), lambda bi, i: (bi, i, 0)),
        out_shape=jax.ShapeDtypeStruct((bsz, t, d), f32),
        compiler_params=_cparams(("parallel", "parallel")),
        name="proj_residual",
    )(*args)


def _ffn_kernel(x_ref, gain_ref, shift_ref, scale_ref, gate_ref, wg_ref, wu_ref, wd_ref, o_ref,
                h_scr, acc_scr, *, prec):
    f = pl.program_id(2)

    @pl.when(f == 0)
    def _():
        h_scr[...] = _modulate(x_ref[0], gain_ref[...], shift_ref[0], scale_ref[0]).astype(h_scr.dtype)
        acc_scr[...] = jnp.zeros_like(acc_scr)

    h = h_scr[...]
    act = _silu(_dotp(h, wg_ref[...], prec)) * _dotp(h, wu_ref[...], prec)
    acc_scr[...] += _dotp(act, wd_ref[...], prec)

    @pl.when(f == pl.num_programs(2) - 1)
    def _():
        o_ref[0] = x_ref[0] + gate_ref[0] * acc_scr[...]


def _ffn(x, gain, shift, scale, gate, wg, wu, wd, *, tm, tf, prec):
    bsz, t, d = x.shape
    fdim = wg.shape[1]
    mod_specs = []
    for mod in (shift, scale, gate):
        if mod.shape[1] == 1:
            mod_specs.append(pl.BlockSpec((1, 1, d), lambda b, i, f: (b, 0, 0)))
        else:
            mod_specs.append(pl.BlockSpec((1, tm, d), lambda b, i, f: (b, i, 0)))
    return pl.pallas_call(
        functools.partial(_ffn_kernel, prec=prec),
        grid=(bsz, t // tm, fdim // tf),
        in_specs=[pl.BlockSpec((1, tm, d), lambda b, i, f: (b, i, 0)),
                  pl.BlockSpec((1, d), lambda b, i, f: (0, 0)),
                  mod_specs[0], mod_specs[1], mod_specs[2],
                  pl.BlockSpec((d, tf), lambda b, i, f: (0, f)),
                  pl.BlockSpec((d, tf), lambda b, i, f: (0, f)),
                  pl.BlockSpec((tf, d), lambda b, i, f: (f, 0))],
        out_specs=pl.BlockSpec((1, tm, d), lambda b, i, f: (b, i, 0)),
        out_shape=jax.ShapeDtypeStruct((bsz, t, d), f32),
        scratch_shapes=[pltpu.VMEM((tm, d), f32 if prec == 3 else bf16), pltpu.VMEM((tm, d), f32)],
        compiler_params=_cparams(("parallel", "parallel", "arbitrary")),
        name="ffn",
    )(x, gain.reshape(1, d), shift, scale, gate, wg, wu, wd)


def _gdn_kernel(qkv_ref, z_ref, misc_ref, cw_ref, cinit_ref, hp_ref, nw_ref, s0_ref,
                o_ref, sfin_ref, xc_scr, s_scr, *, tb, n_valid):
    j = pl.program_id(1)
    c = GDN_CHUNK
    nh = GDN_HEADS

    @pl.when(j == 0)
    def _():
        xc_scr[0:8, :] = cinit_ref[0]
        s_scr[...] = s0_ref[0]

    x = qkv_ref[0]
    xc_scr[8:8 + tb, :] = x
    y = xc_scr[5:5 + tb, :] * cw_ref[0:1, :]
    y = y + xc_scr[6:6 + tb, :] * cw_ref[1:2, :]
    y = y + xc_scr[7:7 + tb, :] * cw_ref[2:3, :]
    y = y + x * cw_ref[3:4, :]
    xc_scr[0:8, :] = xc_scr[tb:tb + 8, :]
    y = _silu(y)

    misc = misc_ref[0]
    row = lax.broadcasted_iota(i32, (tb, MISC_W), 0) + j * tb
    live = row < n_valid
    log_a = jnp.where(live, hp_ref[0:1, :] * _softplus(misc + hp_ref[1:2, :]), 0.0)
    beta = jnp.where(live, jax.nn.sigmoid(misc), 0.0)

    r_i = lax.broadcasted_iota(i32, (tb, tb), 0)
    c_i = lax.broadcasted_iota(i32, (tb, tb), 1)
    sh = int(math.log2(c))
    ltri = (((r_i >> sh) == (c_i >> sh)) & (c_i <= r_i)).astype(bf16)
    g = _mm_01(ltri, log_a)
    g_t = g.T

    ri = lax.broadcasted_iota(i32, (c, c), 0)
    ci = lax.broadcasted_iota(i32, (c, c), 1)
    incl = ci <= ri
    strict = ci < ri
    eye = (ci == ri).astype(f32)
    quad = []
    for lvl in range(int(math.log2(c))):
        quad.append(((ri >> (lvl + 1)) == (ci >> (lvl + 1))) & (((ri >> lvl) & 1) == 1) & (((ci >> lvl) & 1) == 0))

    nchunk = tb // c
    units = [(n, h) for n in range(nchunk) for h in range(nh)]

    def stack(fn):
        return jnp.stack([fn(n * c, h) for n, h in units], axis=0)

    def bmm(eq, a_, b_):
        return jnp.einsum(eq, a_.astype(bf16), b_.astype(bf16), preferred_element_type=f32)

    q = stack(lambda r0, h: y[r0:r0 + c, h * GDN_DK:(h + 1) * GDN_DK])
    k = stack(lambda r0, h: y[r0:r0 + c, (nh + h) * GDN_DK:(nh + h + 1) * GDN_DK])
    v = stack(lambda r0, h: y[r0:r0 + c, 2 * nh * GDN_DK + h * GDN_DV:2 * nh * GDN_DK + (h + 1) * GDN_DV])
    q = q * lax.rsqrt(jnp.sum(q * q, axis=-1, keepdims=True) + EPS) * (GDN_DK ** -0.5)
    k = k * lax.rsqrt(jnp.sum(k * k, axis=-1, keepdims=True) + EPS)
    g_col = stack(lambda r0, h: g[r0:r0 + c, h:h + 1])
    g_row = stack(lambda r0, h: g_t[h:h + 1, r0:r0 + c])
    b_col = stack(lambda r0, h: beta[r0:r0 + c, nh + h:nh + h + 1])
    gam = jnp.where(incl, jnp.exp(jnp.where(incl, g_col - g_row, 0.0)), 0.0)
    kk = bmm('uid,ujd->uij', k, k)
    a = jnp.where(strict, b_col * gam * kk, 0.0)
    p = eye - jnp.where(quad[0], a, 0.0)
    for lvl in range(1, len(quad)):
        m = bmm('uij,ujk->uik', jnp.where(quad[lvl], a, 0.0), p)
        p = p - bmm('uij,ujk->uik', p, m)
    e_g = jnp.exp(g_col)
    sol = bmm('uij,ujd->uid', p, jnp.concatenate([b_col * v, (b_col * e_g) * k], axis=-1))
    vb, w = sol[:, :, :GDN_DV], sol[:, :, GDN_DV:]
    aqk = bmm('uid,ujd->uij', q, k) * gam
    qg = q * e_g
    g_last = g_col[:, c - 1:c, :]
    kd_t = jnp.swapaxes(k * jnp.exp(g_last - g_col), 1, 2)
    gc = jnp.exp(g_last)

    s = s_scr[...]
    for n in range(nchunk):
        sl = slice(n * nh, (n + 1) * nh)
        u = vb[sl] - bmm('hcd,hde->hce', w[sl], s)
        o = bmm('hcd,hde->hce', qg[sl], s) + bmm('hij,hje->hie', aqk[sl], u)
        s = gc[sl] * s + bmm('hdc,hce->hde', kd_t[sl], u)
        o = o * lax.rsqrt(jnp.mean(o * o, axis=-1, keepdims=True) + EPS) * nw_ref[...]
        for h in range(nh):
            zs = z_ref[0, n * c:(n + 1) * c, h * GDN_DV:(h + 1) * GDN_DV]
            o_ref[0, n * c:(n + 1) * c, h * GDN_DV:(h + 1) * GDN_DV] = (o[h] * _silu(zs)).astype(o_ref.dtype)
    s_scr[...] = s

    @pl.when(j == pl.num_programs(1) - 1)
    def _():
        sfin_ref[0] = s_scr[...]


def _gdn(qkv_raw, z, misc, conv_w, conv_buf, a_log, dt_bias, norm_w, s0, *, tb, n_valid):
    bsz, tp, ch = qkv_raw.shape
    cinit = jnp.concatenate([jnp.zeros((bsz, 5, ch), f32), conv_buf.astype(f32)], axis=1)
    hp = jnp.zeros((8, MISC_W), f32)
    hp = hp.at[0, :GDN_HEADS].set(-jnp.exp(a_log.astype(f32))).at[1, :GDN_HEADS].set(dt_bias.astype(f32))
    zw = GDN_HEADS * GDN_DV
    return pl.pallas_call(
        functools.partial(_gdn_kernel, tb=tb, n_valid=n_valid),
        grid=(bsz, tp // tb),
        in_specs=[pl.BlockSpec((1, tb, ch), lambda b, j: (b, j, 0)),
                  pl.BlockSpec((1, tb, zw), lambda b, j: (b, j, 0)),
                  pl.BlockSpec((1, tb, MISC_W), lambda b, j: (b, j, 0)),
                  pl.BlockSpec((GDN_CONV, ch), lambda b, j: (0, 0)),
                  pl.BlockSpec((1, 8, ch), lambda b, j: (b, 0, 0)),
                  pl.BlockSpec((8, MISC_W), lambda b, j: (0, 0)),
                  pl.BlockSpec((1, GDN_DV), lambda b, j: (0, 0)),
                  pl.BlockSpec((1, GDN_HEADS, GDN_DK, GDN_DV), lambda b, j: (b, 0, 0, 0))],
        out_specs=[pl.BlockSpec((1, tb, zw), lambda b, j: (b, j, 0)),
                   pl.BlockSpec((1, GDN_HEADS, GDN_DK, GDN_DV), lambda b, j: (b, 0, 0, 0))],
        out_shape=[jax.ShapeDtypeStruct((bsz, tp, zw), bf16),
                   jax.ShapeDtypeStruct((bsz, GDN_HEADS, GDN_DK, GDN_DV), f32)],
        scratch_shapes=[pltpu.VMEM((tb + 8, ch), f32), pltpu.VMEM((GDN_HEADS, GDN_DK, GDN_DV), f32)],
        compiler_params=_cparams(("parallel", "arbitrary")),
        name="gdn",
    )(qkv_raw, z, misc, conv_w.astype(f32), cinit, hp, norm_w.reshape(1, GDN_DV).astype(f32), s0.astype(f32))


def _gdn_step_kernel(x_ref, buf_ref, cw_ref, z_ref, misc_ref, hp_ref, nw_ref, s0_ref, o_ref, s_ref):
    nh = GDN_HEADS
    buf = buf_ref[0]
    y = buf[0:1] * cw_ref[0:1, :]
    y = y + buf[1:2] * cw_ref[1:2, :]
    y = y + buf[2:3] * cw_ref[2:3, :]
    y = _silu(y + x_ref[0] * cw_ref[3:4, :])
    misc = misc_ref[0]
    a_all = jnp.exp(hp_ref[0:1, :] * _softplus(misc + hp_ref[1:2, :]))
    b_all = jax.nn.sigmoid(misc)
    eye = lax.broadcasted_iota(i32, (GDN_DK, GDN_DK), 0) == lax.broadcasted_iota(i32, (GDN_DK, GDN_DK), 1)
    pad = jnp.zeros((6, GDN_DK), f32)
    outs = []
    for h in range(nh):
        q = y[:, h * GDN_DK:(h + 1) * GDN_DK]
        k = y[:, (nh + h) * GDN_DK:(nh + h + 1) * GDN_DK]
        v = y[:, 2 * nh * GDN_DK + h * GDN_DV:2 * nh * GDN_DK + (h + 1) * GDN_DV]
        q = q * lax.rsqrt(jnp.sum(q * q, axis=-1, keepdims=True) + EPS) * (GDN_DK ** -0.5)
        k = k * lax.rsqrt(jnp.sum(k * k, axis=-1, keepdims=True) + EPS)
        a = a_all[:, h:h + 1]
        b = b_all[:, nh + h:nh + h + 1]
        s = s0_ref[0, h]
        ks_qs = _mm3(jnp.concatenate([k, q, pad], axis=0), s)
        u = b * v - (b * a) * ks_qs[0:1]
        o = a * ks_qs[1:2] + jnp.sum(q * k, axis=-1, keepdims=True) * u
        k_col = jnp.sum(jnp.where(eye, k, 0.0), axis=1, keepdims=True)
        s_ref[0, h] = a * s + k_col * u
        o = o * lax.rsqrt(jnp.mean(o * o, axis=-1, keepdims=True) + EPS) * nw_ref[...]
        outs.append(o * _silu(z_ref[0][:, h * GDN_DV:(h + 1) * GDN_DV]))
    o_ref[0] = jnp.concatenate(outs, axis=1)


def _gdn_step(qkv_raw, z, misc, conv_w, conv_buf, a_log, dt_bias, norm_w, s0):
    db, _, ch = qkv_raw.shape
    hp = jnp.zeros((8, MISC_W), f32)
    hp = hp.at[0, :GDN_HEADS].set(-jnp.exp(a_log.astype(f32))).at[1, :GDN_HEADS].set(dt_bias.astype(f32))
    zw = GDN_HEADS * GDN_DV
    per_b = lambda shape: pl.BlockSpec((1,) + shape, lambda b: (b,) + (0,) * len(shape))
    const = lambda shape: pl.BlockSpec(shape, lambda b: (0,) * len(shape))
    return pl.pallas_call(
        _gdn_step_kernel,
        grid=(db,),
        in_specs=[per_b((1, ch)), per_b((GDN_CONV - 1, ch)), const((GDN_CONV, ch)), per_b((1, zw)),
                  per_b((1, MISC_W)), const((8, MISC_W)), const((1, GDN_DV)), per_b((GDN_HEADS, GDN_DK, GDN_DV))],
        out_specs=[per_b((1, zw)), per_b((GDN_HEADS, GDN_DK, GDN_DV))],
        out_shape=[jax.ShapeDtypeStruct((db, 1, zw), f32),
                   jax.ShapeDtypeStruct((db, GDN_HEADS, GDN_DK, GDN_DV), f32)],
        compiler_params=_cparams(("parallel",)),
        name="gdn_step",
    )(qkv_raw, conv_buf.astype(f32), conv_w.astype(f32), z, misc, hp, norm_w.reshape(1, GDN_DV).astype(f32),
      s0.astype(f32))


def _bucket_thresholds():
    exact = REL_BUCKETS // 2
    d = np.arange(0, 4 * REL_MAX_DIST, dtype=np.int64)

    def buckets(ft):
        nf = np.maximum(d, exact).astype(ft)
        large = exact + (np.log(nf / ft(exact)) / ft(math.log(REL_MAX_DIST / exact)) * ft(REL_BUCKETS - exact)).astype(np.int32)
        return np.where(d < exact, d, np.minimum(large, REL_BUCKETS - 1))

    b64, b32 = buckets(np.float64), buckets(np.float32)
    assert np.array_equal(b64, b32) and b64[-1] == REL_BUCKETS - 1 and np.all(np.diff(b64) >= 0)
    return [int(np.argmax(b64 >= k)) for k in range(1, REL_BUCKETS)]


_BUCKET_THR = _bucket_thresholds()
FAR_DIST = _BUCKET_THR[-1]


def _bias_kernel(rb_ref, d_ref, o_ref, *, slab, group_heads):
    rows = d_ref.shape[-2]
    for s in range(rows // slab):
        d = d_ref[0, 0, s * slab:(s + 1) * slab, :]
        h = pl.program_id(0) * group_heads + s
        val = jnp.full(d.shape, rb_ref[0, h], f32)
        for k in range(1, REL_BUCKETS):
            val = jnp.where(d >= _BUCKET_THR[k - 1], rb_ref[k, h], val)
        o_ref[0, 0, s * slab:(s + 1) * slab, :] = jnp.where(d < 0, NEG, val)


def _bias_table(rel_bias, dist, *, slab):
    g, nt, rows, cols = dist.shape
    return pl.pallas_call(
        functools.partial(_bias_kernel, slab=slab, group_heads=rows // slab),
        grid=(g, nt),
        in_specs=[pl.BlockSpec(memory_space=pltpu.SMEM),
                  pl.BlockSpec((1, 1, rows, cols), lambda a, b: (a, b, 0, 0))],
        out_specs=pl.BlockSpec((1, 1, rows, cols), lambda a, b: (a, b, 0, 0)),
        out_shape=jax.ShapeDtypeStruct(dist.shape, f32),
        compiler_params=_cparams(("arbitrary", "arbitrary")),
        name="bias_table",
    )(rel_bias.astype(f32), dist)


def _blockmean_kernel(x_ref, o_ref):
    x = x_ref[0]
    nb = x.shape[0] // NSA_BLOCK
    o_ref[0] = jnp.sum(x.reshape(nb, NSA_BLOCK, x.shape[1]), axis=1) * (1.0 / NSA_BLOCK)


def _block_means(kv4, *, tb):
    bsz, t, _ = kv4.shape
    w = 2 * NSA_KV_GROUPS * NSA_DH
    return pl.pallas_call(
        _blockmean_kernel,
        grid=(bsz, t // tb),
        in_specs=[pl.BlockSpec((1, tb, w), lambda b, i: (b, i, 0))],
        out_specs=pl.BlockSpec((1, tb // NSA_BLOCK, w), lambda b, i: (b, i, 0)),
        out_shape=jax.ShapeDtypeStruct((bsz, t // NSA_BLOCK, w), f32),
        compiler_params=_cparams(("parallel", "parallel")),
        name="nsa_block_means",
    )(kv4)


LOG2E = 1.4426950408889634
NSA_KEY_TILE = 4 * PAIR


def _nsa_prompt_kernel(q_ref, misc_ref, kcvc_ref, kaug_ref, vgt_ref, kw_ref, vw_ref,
                       tabs_ref, tabw_ref, tabc_ref, c31_ref, o_ref, qaug_scr, s_scr):
    g = pl.program_id(1)
    i = pl.program_id(2)
    qb, dh, hpg = NSA_QBLOCK, NSA_DH, NSA_HPG
    is_g0 = g == 0
    nn = (((1,), (1,)), ((), ()))

    def ghalf(a):
        return jnp.where(is_g0, a[:, :dh], a[:, dh:])

    qt = q_ref[0] * (dh ** -0.5)
    qh = [qt[:, h * dh:(h + 1) * dh] for h in range(hpg)]
    kcvc = kcvc_ref[0]
    nblk = kcvc.shape[0]
    kc = ghalf(kcvc[:, :2 * dh])
    vc = ghalf(kcvc[:, 2 * dh:])

    rows = hpg * qb
    qs = jnp.concatenate(qh, axis=0)
    zero = jnp.zeros_like(qs)
    q2f = jnp.concatenate([jnp.where(is_g0, qs, zero), jnp.where(is_g0, zero, qs)], axis=1) * LOG2E
    q2 = q2f.astype(bf16)

    blk4 = lax.broadcasted_iota(i32, (rows, nblk), 1)
    qrow4 = lax.broadcasted_iota(i32, (rows, nblk), 0) & (qb - 1)
    first_tab = 2 * i - 2 * NEAR_PAIRS + 1
    shift = (first_tab + 4 * nblk) % nblk
    readable = blk4 * NSA_BLOCK + (NSA_BLOCK - 1) <= i * qb + qrow4
    bias = jnp.where(blk4 < first_tab, c31_ref[0], pltpu.roll(tabc_ref[0], shift, 1))
    s_c = jnp.where(readable, _mm3_nt(qs, kc) + bias, NEG)
    m_c = jnp.max(s_c, axis=-1, keepdims=True)
    p_c = jnp.where(readable, jnp.exp(s_c - m_c), 0.0)
    p_c = p_c / jnp.maximum(jnp.sum(p_c, axis=-1, keepdims=True), 1e-30)
    o_cmp = _mm(p_c, vc)
    imp = p_c[0:qb]
    for h in range(1, hpg):
        imp = imp + p_c[h * qb:(h + 1) * qb]

    span = NSA_WINDOW + qb
    ws = pl.multiple_of(i * qb, qb)
    s_w = lax.dot_general(q2, kw_ref[0, pl.ds(ws, span), :], nn, preferred_element_type=f32) + tabw_ref[0]
    in_seq = lax.broadcasted_iota(i32, (rows, span), 1) + (i * qb - NSA_WINDOW) >= 0
    s_w = jnp.where(in_seq, s_w, NEG)
    m_w = jnp.max(s_w, axis=-1, keepdims=True)
    p_w = jnp.exp2(s_w - m_w)
    l_w = jnp.sum(p_w, axis=-1, keepdims=True)
    o_win = ghalf(jnp.dot(p_w.astype(bf16), vw_ref[0, pl.ds(ws, span), :], preferred_element_type=f32))
    o_win = o_win / jnp.maximum(l_w, 1e-30)

    blk = lax.broadcasted_iota(i32, (nblk, qb), 0)
    cur = (i * qb + lax.broadcasted_iota(i32, (nblk, qb), 1)) >> int(math.log2(NSA_BLOCK))
    valid = blk <= cur
    forced = valid & ((blk == 0) | (blk > cur - NSA_LOCAL))
    imp_t = jnp.where(forced, -jnp.inf, jnp.where(valid, imp.T, -1.0))
    picked = forced
    for _ in range(min(NSA_TOPN, nblk) - 1 - NSA_LOCAL):
        mx = jnp.max(imp_t, axis=0, keepdims=True)
        first = jnp.min(jnp.where(imp_t == mx, blk, nblk), axis=0, keepdims=True)
        hit = blk == first
        picked = picked | hit
        imp_t = jnp.where(hit, -jnp.inf, imp_t)
    msel_t = jnp.where(picked & valid, 0.0, NEG).astype(bf16)

    qaug_scr[...] = jnp.concatenate([q2f.T.astype(bf16), jnp.concatenate([msel_t] * hpg, axis=1)], axis=0)
    tk = s_scr.shape[0]
    ppt = tk // PAIR
    t_d = i // ppt

    def scores(t):
        ks = pl.multiple_of(t * tk, tk)
        s = jnp.dot(kaug_ref[0, pl.ds(ks, tk), :], qaug_scr[...], preferred_element_type=f32)
        tabs = [tabs_ref[0, jnp.clip(i - ppt * t - j, 0, NEAR_PAIRS)] for j in range(ppt)]
        return s + jnp.concatenate(tabs, axis=0)

    s_scr[...] = scores(t_d)

    def tile(k_it, carry):
        m, acc = carry
        t = t_d - k_it
        s = s_scr[...]
        s_next = scores(jnp.maximum(t - 1, 0))
        m_new = jnp.maximum(m, jnp.max(s, axis=0, keepdims=True))
        alpha = jnp.exp2(m - m_new)
        p = jnp.exp2(s - m_new).astype(bf16)
        ks = pl.multiple_of(t * tk, tk)
        acc = alpha * acc + jnp.dot(vgt_ref[0, 0, :, pl.ds(ks, tk)], p, preferred_element_type=f32)
        s_scr[...] = s_next
        return m_new, acc

    _, acc_t = lax.fori_loop(0, t_d + 1, tile, (jnp.full((1, rows), NEG, f32), jnp.zeros((2 * dh, rows), f32)))
    o_sel = (acc_t[:dh] / jnp.maximum(acc_t[dh:dh + 1], 1e-30)).T

    gates = jax.nn.sigmoid(misc_ref[0])
    outs = []
    for h in range(hpg):
        r = slice(h * qb, (h + 1) * qb)
        acc = None
        for jb, branch in enumerate((o_cmp[r], o_sel[r], o_win[r])):
            la = 8 + 3 * h + jb
            lb = 8 + 3 * (hpg + h) + jb
            term = jnp.where(is_g0, gates[:, la:la + 1], gates[:, lb:lb + 1]) * branch
            acc = term if acc is None else acc + term
        outs.append(acc)
    o_ref[0] = jnp.concatenate(outs, axis=1).astype(o_ref.dtype)


def _nsa_tables(rel_bias):
    qb, hpg = NSA_QBLOCK, NSA_HPG
    rows = hpg * qb
    q = (np.arange(rows) % qb)[:, None]
    c = np.arange(PAIR)[None, :]
    d_sel = [PAIR * idx + q - c for idx in range(NEAR_PAIRS + 1)]
    assert PAIR * NEAR_PAIRS - (PAIR - 1) >= FAR_DIST
    mm = 2 * NEAR_PAIRS - 1 - c
    d_cmp = np.maximum(np.where(mm >= -1, NSA_BLOCK * mm + q - (NSA_BLOCK - 1), 0), 0)
    assert NSA_BLOCK * (2 * NEAR_PAIRS) - (NSA_BLOCK - 1) >= FAR_DIST
    cw = np.arange(NSA_WINDOW + qb)[None, :]
    d_win = q + NSA_WINDOW - cw
    d_win = np.where((d_win >= 0) & (d_win < NSA_WINDOW), d_win, -1)
    n_win = (NSA_WINDOW + qb) // PAIR
    tiles = d_sel + [d_cmp] + [d_win[:, k * PAIR:(k + 1) * PAIR] for k in range(n_win)]
    dist = np.broadcast_to(np.stack(tiles)[None], (NSA_KV_GROUPS, len(tiles), rows, PAIR))
    tab = _bias_table(rel_bias, jnp.asarray(dist, i32), slab=qb)
    ns = NEAR_PAIRS + 1
    tabs = tab[:, :ns]
    tabc = tab[:, ns]
    tabw = jnp.concatenate([tab[:, ns + 1 + k] for k in range(n_win)], axis=-1)
    c31 = jnp.repeat(rel_bias.astype(f32)[REL_BUCKETS - 1].reshape(NSA_KV_GROUPS, hpg), qb, axis=1)
    return tabs, tabw, tabc, c31.reshape(NSA_KV_GROUPS, rows, 1)


def _nsa_prompt(q_b, misc, kv4, kvwin, rel_bias):
    bsz, t, _ = q_b.shape
    dh, qb = NSA_DH, NSA_QBLOCK
    nblk = t // NSA_BLOCK
    assert nblk == PAIR and t % NSA_KEY_TILE == 0
    kcvc = _block_means(kv4, tb=MEANS_TB)
    onehot = (jnp.arange(t, dtype=i32)[:, None] // NSA_BLOCK == jnp.arange(nblk, dtype=i32)[None, :]).astype(bf16)
    kaug = jnp.concatenate([kv4[:, :, 4 * dh:6 * dh].astype(bf16),
                            jnp.broadcast_to(onehot[None], (bsz, t, nblk))], axis=-1)
    ones = jnp.ones((bsz, t, dh), bf16)
    vgt = jnp.stack([jnp.swapaxes(jnp.concatenate([kv4[:, :, (6 + g) * dh:(7 + g) * dh].astype(bf16), ones],
                                                   axis=-1), 1, 2)
                     for g in range(NSA_KV_GROUPS)], axis=1)
    pad = ((0, 0), (NSA_WINDOW, 0), (0, 0))
    kw = jnp.pad(kvwin[:, :, :2 * dh].astype(bf16), pad)
    vw = jnp.pad(kvwin[:, :, 2 * dh:].astype(bf16), pad)
    tabs, tabw, tabc, c31 = _nsa_tables(rel_bias)
    tabs, tabw = jnp.swapaxes(tabs, 2, 3) * LOG2E, tabw * LOG2E
    rows = NSA_HPG * qb
    gw = NSA_HPG * dh
    span = NSA_WINDOW + qb
    return pl.pallas_call(
        _nsa_prompt_kernel,
        grid=(bsz, NSA_KV_GROUPS, t // qb),
        in_specs=[pl.BlockSpec((1, qb, gw), lambda b, g, i: (b, i, g)),
                  pl.BlockSpec((1, qb, MISC_W), lambda b, g, i: (b, i, 0)),
                  pl.BlockSpec((1, nblk, 4 * dh), lambda b, g, i: (b, 0, 0)),
                  pl.BlockSpec((1, t, 2 * dh + nblk), lambda b, g, i: (b, 0, 0)),
                  pl.BlockSpec((1, 1, 2 * dh, t), lambda b, g, i: (b, g, 0, 0)),
                  pl.BlockSpec((1, t + NSA_WINDOW, 2 * dh), lambda b, g, i: (b, 0, 0)),
                  pl.BlockSpec((1, t + NSA_WINDOW, 2 * dh), lambda b, g, i: (b, 0, 0)),
                  pl.BlockSpec((1, NEAR_PAIRS + 1, PAIR, rows), lambda b, g, i: (g, 0, 0, 0)),
                  pl.BlockSpec((1, rows, span), lambda b, g, i: (g, 0, 0)),
                  pl.BlockSpec((1, rows, PAIR), lambda b, g, i: (g, 0, 0)),
                  pl.BlockSpec((1, rows, 1), lambda b, g, i: (g, 0, 0))],
        out_specs=pl.BlockSpec((1, qb, gw), lambda b, g, i: (b, i, g)),
        out_shape=jax.ShapeDtypeStruct((bsz, t, NSA_HEADS * dh), bf16),
        scratch_shapes=[pltpu.VMEM((2 * dh + nblk, rows), bf16), pltpu.VMEM((NSA_KEY_TILE, rows), f32)],
        compiler_params=_cparams(("parallel", "parallel", "arbitrary")),
        name="nsa_prompt",
    )(q_b, misc, kcvc, kaug, vgt, kw, vw, tabs, tabw, tabc, c31)


MEANS_PAGES = 16


def _nsa_s_means_kernel(pt_ref, *refs):
    del pt_ref
    x_refs, o_ref = refs[:-1], refs[-1]
    p = pl.program_id(1)

    @pl.when(p == 0)
    def _():
        o_ref[...] = jnp.zeros_like(o_ref)

    page = x_refs[0].shape[-1]
    per_page = page // NSA_BLOCK
    nblk = o_ref.shape[-1]
    assert page == nblk
    planes = 2 * NSA_KV_GROUPS
    xs = jnp.concatenate([x_ref[0, j, g] for x_ref in x_refs for j in range(2) for g in range(NSA_KV_GROUPS)],
                         axis=0)
    tok = lax.broadcasted_iota(i32, (page, nblk), 0)
    col = lax.broadcasted_iota(i32, (page, nblk), 1)
    pool = (col == (tok >> int(math.log2(NSA_BLOCK)))).astype(bf16)
    hi, lo = _split2(xs)
    sums = jnp.dot(hi, pool, preferred_element_type=f32) + jnp.dot(lo, pool, preferred_element_type=f32)
    rows = planes * NSA_DH
    acc = None
    for k in range(len(x_refs)):
        shift = (p * len(x_refs) + k) * per_page
        part = pltpu.roll(sums[k * rows:(k + 1) * rows], shift, 1)
        acc = part if acc is None else acc + part
    acc = acc * (1.0 / NSA_BLOCK)
    for pl_i in range(planes):
        o_ref[0, pl_i] = o_ref[0, pl_i] + acc[pl_i * NSA_DH:(pl_i + 1) * NSA_DH]


def _nsa_s_means(cache_t, page_table):
    db, n_pages = page_table.shape
    page = cache_t.shape[-1]
    nblk = n_pages * page // NSA_BLOCK
    kp = MEANS_PAGES
    assert n_pages % kp == 0

    def page_spec(k):
        return pl.BlockSpec((1, 2, NSA_KV_GROUPS, NSA_DH, page),
                            lambda b, p, pt: (pt[b * n_pages + p * kp + k], 0, 0, 0, 0))

    return pl.pallas_call(
        _nsa_s_means_kernel,
        grid_spec=pltpu.PrefetchScalarGridSpec(
            num_scalar_prefetch=1,
            grid=(db, n_pages // kp),
            in_specs=[page_spec(k) for k in range(kp)],
            out_specs=pl.BlockSpec((1, 2 * NSA_KV_GROUPS, NSA_DH, nblk), lambda b, p, pt: (b, 0, 0, 0))),
        out_shape=jax.ShapeDtypeStruct((db, 2 * NSA_KV_GROUPS, NSA_DH, nblk), f32),
        compiler_params=_cparams(("parallel", "arbitrary")),
        name="nsa_sample_means",
    )(page_table.reshape(-1), *([cache_t] * kp))


def _nsa_s_scores_kernel(q_ref, kcvc_ref, win_ref, kvn_ref, tc_ref, tw_ref, b0_ref, oc_ref, ow_ref, sel_ref):
    dh, hpg, ng = NSA_DH, NSA_HPG, NSA_KV_GROUPS
    sb = q_ref.shape[0]
    nblk = kcvc_ref.shape[-1]
    row = lax.broadcasted_iota(i32, (NSA_HEADS, 1), 0)
    in_g = [(row >= g * hpg) & (row < (g + 1) * hpg) for g in range(ng)]
    pairs = [(b, g) for b in range(sb) for g in range(ng)]
    q = [q_ref[b] * (dh ** -0.5) for b in range(sb)]

    s_c = [_mm3(q[b], kcvc_ref[b, g]) + tc_ref[...] for b, g in pairs]
    p_c = []
    for s in s_c:
        p = jnp.exp(s - jnp.max(s, axis=-1, keepdims=True))
        p_c.append(p / jnp.maximum(jnp.sum(p, axis=-1, keepdims=True), 1e-30))
    o_c = [_mm_nt(p_c[k], kcvc_ref[b, ng + g]) for k, (b, g) in enumerate(pairs)]
    imp = jnp.concatenate([jnp.sum(jnp.where(in_g[g], p_c[k], 0.0), axis=0, keepdims=True)
                           for k, (b, g) in enumerate(pairs)], axis=0)
    for b in range(sb):
        oc_ref[b] = jnp.where(in_g[0], o_c[b * ng], o_c[b * ng + 1])

    s_w = [_mm3(q[b], win_ref[b, 0, g]) + tw_ref[...] for b, g in pairs]
    o_w = []
    for k, (b, g) in enumerate(pairs):
        kvn = kvn_ref[b]
        s_n = jnp.sum(q[b] * kvn[:, g * dh:(g + 1) * dh], axis=-1, keepdims=True) + b0_ref[...]
        m_w = jnp.maximum(jnp.max(s_w[k], axis=-1, keepdims=True), s_n)
        p_w = jnp.exp(s_w[k] - m_w)
        p_n = jnp.exp(s_n - m_w)
        l_w = jnp.sum(p_w, axis=-1, keepdims=True) + p_n
        v_n = kvn[:, (ng + g) * dh:(ng + g + 1) * dh]
        o_w.append((_mm_nt(p_w, win_ref[b, 1, g]) + p_n * v_n) / jnp.maximum(l_w, 1e-30))
    for b in range(sb):
        ow_ref[b] = jnp.where(in_g[0], o_w[b * ng], o_w[b * ng + 1])

    lane = lax.broadcasted_iota(i32, imp.shape, 1)
    n_sel = NSA_TOPN - 1
    forced_blocks = [0] + [nblk - k for k in range(1, NSA_LOCAL)]
    imp = jnp.where((lane == 0) | (lane > nblk - NSA_LOCAL), -jnp.inf, imp)
    picks = jnp.zeros(imp.shape, i32)
    for it, fb in enumerate(forced_blocks):
        picks = jnp.where(lane == it, fb, picks)
    for it in range(len(forced_blocks), n_sel):
        mx = jnp.max(imp, axis=-1, keepdims=True)
        first = jnp.min(jnp.where(imp == mx, lane, nblk), axis=-1, keepdims=True)
        picks = jnp.where(lane == it, first, picks)
        imp = jnp.where(lane == first, -jnp.inf, imp)
    for b in range(sb):
        sel_ref[b] = picks[b * ng:(b + 1) * ng]


def _nsa_s_sel_kernel(sel_ref, pt_ref, q_ref, rb_ref, kvn_ref, oc_ref, ow_ref, gt_ref, *refs, n_sel, past):
    del pt_ref
    k_refs, v_refs, o_ref = refs[:n_sel], refs[n_sel:2 * n_sel], refs[2 * n_sel]
    b = pl.program_id(0)
    g = pl.program_id(1)
    dh, hpg = NSA_DH, NSA_HPG
    is_g0 = g == 0
    page = k_refs[0].shape[-1]
    per_page = page // NSA_BLOCK
    q = q_ref[0] * (dh ** -0.5)
    kvn = kvn_ref[0]
    lane = lax.broadcasted_iota(i32, (NSA_HEADS, page), 1)
    s_parts, d_parts = [], []
    for j in range(n_sel):
        nb = sel_ref[(b * NSA_KV_GROUPS + g) * n_sel + j]
        in_blk = (lane >> int(math.log2(NSA_BLOCK))) == (nb % per_page)
        d_parts.append(jnp.where(in_blk, past - ((nb // per_page) * page + lane), -1))
        s_parts.append(_mm(q, k_refs[j][0, 0, 0]))
    d = jnp.concatenate(d_parts, axis=1)
    bias = jnp.broadcast_to(rb_ref[0], d.shape)
    for k in range(1, REL_BUCKETS):
        bias = jnp.where(d >= _BUCKET_THR[k - 1], rb_ref[k], bias)
    s = jnp.concatenate(s_parts, axis=1) + jnp.where(d < 0, NEG, bias)
    k_n = jnp.where(is_g0, kvn[:, 4 * dh:5 * dh], kvn[:, 5 * dh:6 * dh])
    v_n = jnp.where(is_g0, kvn[:, 6 * dh:7 * dh], kvn[:, 7 * dh:8 * dh])
    s_n = jnp.sum(q * k_n, axis=-1, keepdims=True) + rb_ref[0]
    m = jnp.maximum(jnp.max(s, axis=-1, keepdims=True), s_n)
    p = jnp.exp(s - m)
    p_n = jnp.exp(s_n - m)
    l = jnp.sum(p, axis=-1, keepdims=True) + p_n
    acc = p_n * v_n
    for j in range(n_sel):
        acc = acc + _mm_nt(p[:, j * page:(j + 1) * page], v_refs[j][0, 0, 0])
    o_s = acc / jnp.maximum(l, 1e-30)
    gt = gt_ref[0]
    o = gt[:, 0:1] * oc_ref[0] + gt[:, 1:2] * o_s + gt[:, 2:3] * ow_ref[0]
    row = lax.broadcasted_iota(i32, o.shape, 0)
    in_g = (row >= g * hpg) & (row < (g + 1) * hpg)

    @pl.when(is_g0)
    def _():
        o_ref[0] = o

    @pl.when(jnp.logical_not(is_g0))
    def _():
        o_ref[0] = jnp.where(in_g, o, o_ref[0])


def _nsa_sample(q_b, gates, kv4_new, kvwin_new, cache_kv, cache_win, page_table, rel_bias):
    db = q_b.shape[0]
    dh = NSA_DH
    n_pool, page = cache_kv.shape[:2]
    n_pages = page_table.shape[1]
    past = n_pages * page
    nblk = past // NSA_BLOCK
    wb = cache_win.shape[1]
    assert nblk == PAIR and page % NSA_BLOCK == 0 and wb == NSA_WINDOW and nblk > NSA_LOCAL
    cache_t = jnp.transpose(cache_kv, (0, 2, 3, 4, 1)).astype(f32)
    win_t = jnp.transpose(cache_win, (0, 2, 3, 4, 1)).astype(f32)
    kcvc = _nsa_s_means(cache_t, page_table)

    n = np.arange(nblk)
    d_cmp = past - (n * NSA_BLOCK + NSA_BLOCK - 1)
    jw = np.arange(wb)
    d_win = np.where(jw >= 1, wb - jw, -1)
    dist = np.broadcast_to(np.concatenate([d_cmp, d_win])[None, :], (NSA_HEADS, nblk + wb))
    tab = _bias_table(rel_bias, jnp.asarray(dist[None, None], i32), slab=1)[0, 0]
    t_cmp, t_win = tab[:, :nblk], tab[:, nblk:]
    rb = rel_bias.astype(f32)
    b0 = rb[0].reshape(NSA_HEADS, 1)

    q3 = q_b.reshape(db, NSA_HEADS, dh)
    sb = max(s for s in (8, 4, 2, 1) if db % s == 0)
    o_c, o_w, sel = pl.pallas_call(
        _nsa_s_scores_kernel,
        grid=(db // sb,),
        in_specs=[pl.BlockSpec((sb, NSA_HEADS, dh), lambda b: (b, 0, 0)),
                  pl.BlockSpec((sb, 2 * NSA_KV_GROUPS, dh, nblk), lambda b: (b, 0, 0, 0)),
                  pl.BlockSpec((sb, 2, NSA_KV_GROUPS, dh, wb), lambda b: (b, 0, 0, 0, 0)),
                  pl.BlockSpec((sb, 1, 4 * dh), lambda b: (b, 0, 0)),
                  pl.BlockSpec((NSA_HEADS, nblk), lambda b: (0, 0)),
                  pl.BlockSpec((NSA_HEADS, wb), lambda b: (0, 0)),
                  pl.BlockSpec((NSA_HEADS, 1), lambda b: (0, 0))],
        out_specs=[pl.BlockSpec((sb, NSA_HEADS, dh), lambda b: (b, 0, 0)),
                   pl.BlockSpec((sb, NSA_HEADS, dh), lambda b: (b, 0, 0)),
                   pl.BlockSpec((sb, NSA_KV_GROUPS, nblk), lambda b: (b, 0, 0))],
        out_shape=[jax.ShapeDtypeStruct((db, NSA_HEADS, dh), f32),
                   jax.ShapeDtypeStruct((db, NSA_HEADS, dh), f32),
                   jax.ShapeDtypeStruct((db, NSA_KV_GROUPS, nblk), i32)],
        compiler_params=_cparams(("parallel",)),
        name="nsa_sample_scores",
    )(q3, kcvc, win_t, kvwin_new.reshape(db, 1, 4 * dh), t_cmp, t_win, b0)

    n_sel = NSA_TOPN - 1
    sel_flat = sel[:, :, :n_sel].reshape(-1)
    per_page = page // NSA_BLOCK

    def page_spec(j, plane):
        def imap(b, g, s, p):
            nb = s[(b * NSA_KV_GROUPS + g) * n_sel + j]
            return (p[b * n_pages + nb // per_page], plane, g, 0, 0)
        return pl.BlockSpec((1, 1, 1, dh, page), imap)

    const = lambda shape: pl.BlockSpec(shape, lambda b, g, s, p: (0,) * len(shape))
    per_b = lambda shape: pl.BlockSpec((1,) + shape, lambda b, g, s, p: (b,) + (0,) * len(shape))
    o = pl.pallas_call(
        functools.partial(_nsa_s_sel_kernel, n_sel=n_sel, past=past),
        grid_spec=pltpu.PrefetchScalarGridSpec(
            num_scalar_prefetch=2,
            grid=(db, NSA_KV_GROUPS),
            in_specs=[per_b((NSA_HEADS, dh)), const((REL_BUCKETS, NSA_HEADS, 1)), per_b((1, 8 * dh)),
                      per_b((NSA_HEADS, dh)), per_b((NSA_HEADS, dh)), per_b((NSA_HEADS, 3))]
                     + [page_spec(j, 2) for j in range(n_sel)] + [page_spec(j, 3) for j in range(n_sel)],
            out_specs=per_b((NSA_HEADS, dh))),
        out_shape=jax.ShapeDtypeStruct((db, NSA_HEADS, dh), f32),
        compiler_params=_cparams(("parallel", "arbitrary")),
        name="nsa_sample_selected",
    )(sel_flat, page_table.reshape(-1), q3, rb.reshape(REL_BUCKETS, NSA_HEADS, 1), kv4_new.reshape(db, 1, 8 * dh),
      o_c, o_w, gates, *([cache_t] * (2 * n_sel)))
    return o.reshape(db, NSA_HEADS * dh)


def _gelu_tanh(x):
    return 0.5 * x * (1.0 + jnp.tanh(math.sqrt(2.0 / math.pi) * (x + 0.044715 * (x * x * x))))


def _lru_gates(xc, wa_ref, wx_ref, ba_ref, bx_ref, lam_ref, prec):
    r_parts, i_parts = [], []
    for n in range(RNN_BLOCKS):
        xb = xc[:, n * RNN_BW:(n + 1) * RNN_BW]
        r_parts.append(_dotp(xb, wa_ref[n], prec))
        i_parts.append(_dotp(xb, wx_ref[n], prec))
    r = jax.nn.sigmoid(jnp.concatenate(r_parts, axis=1) + ba_ref[...])
    i = jax.nn.sigmoid(jnp.concatenate(i_parts, axis=1) + bx_ref[...])
    log_a = -RG_C * r * _softplus(-lam_ref[...])
    a = jnp.exp(log_a)
    t = jnp.tanh(log_a)
    b = jnp.sqrt(jnp.maximum(-2.0 * t / (1.0 - t), 0.0)) * (i * xc)
    return a, b


def _lru_kernel(rec_ref, gate_ref, cw_ref, cb_ref, wa_ref, wx_ref, ba_ref, bx_ref, lam_ref, cinit_ref, h0_ref,
                y_ref, hfin_ref, xc_scr, a_scr, b_scr, hs_scr, h_scr, *, tb):
    j = pl.program_id(1)

    @pl.when(j == 0)
    def _():
        xc_scr[0:8, :] = cinit_ref[0]
        h_scr[...] = h0_ref[0]

    x = rec_ref[0]
    xc_scr[8:8 + tb, :] = x
    xc = xc_scr[5:5 + tb, :] * cw_ref[0:1, :]
    xc = xc + xc_scr[6:6 + tb, :] * cw_ref[1:2, :]
    xc = xc + xc_scr[7:7 + tb, :] * cw_ref[2:3, :]
    xc = xc + x * cw_ref[3:4, :]
    xc = xc + cb_ref[...]
    xc_scr[0:8, :] = xc_scr[tb:tb + 8, :]
    a, b = _lru_gates(xc, wa_ref, wx_ref, ba_ref, bx_ref, lam_ref, 1)
    a_scr[...] = a
    b_scr[...] = b

    def step(t, h):
        h = a_scr[pl.ds(t, 1), :] * h + b_scr[pl.ds(t, 1), :]
        hs_scr[pl.ds(t, 1), :] = h
        return h

    h = lax.fori_loop(0, tb, step, h_scr[...], unroll=8)
    h_scr[...] = h
    y_ref[0] = (_gelu_tanh(gate_ref[0]) * hs_scr[...]).astype(y_ref.dtype)

    @pl.when(j == pl.num_programs(1) - 1)
    def _():
        hfin_ref[0] = h


def _lru_prompt(rec, gate, conv_w, conv_b, wa, wx, ba, bx, lam, *, tb):
    bsz, t, w = rec.shape
    row = lambda a: a.reshape(1, w).astype(f32)
    cinit = jnp.zeros((bsz, 8, w), f32)
    h0 = jnp.zeros((bsz, 1, w), f32)
    full = lambda shape: pl.BlockSpec(shape, lambda b, j: (0,) * len(shape))
    return pl.pallas_call(
        functools.partial(_lru_kernel, tb=tb),
        grid=(bsz, t // tb),
        in_specs=[pl.BlockSpec((1, tb, w), lambda b, j: (b, j, 0)),
                  pl.BlockSpec((1, tb, w), lambda b, j: (b, j, 0)),
                  full((RNN_CONV, w)), full((1, w)),
                  full((RNN_BLOCKS, RNN_BW, RNN_BW)), full((RNN_BLOCKS, RNN_BW, RNN_BW)),
                  full((1, w)), full((1, w)), full((1, w)),
                  pl.BlockSpec((1, 8, w), lambda b, j: (b, 0, 0)),
                  pl.BlockSpec((1, 1, w), lambda b, j: (b, 0, 0))],
        out_specs=[pl.BlockSpec((1, tb, w), lambda b, j: (b, j, 0)),
                   pl.BlockSpec((1, 1, w), lambda b, j: (b, 0, 0))],
        out_shape=[jax.ShapeDtypeStruct((bsz, t, w), bf16), jax.ShapeDtypeStruct((bsz, 1, w), f32)],
        scratch_shapes=[pltpu.VMEM((tb + 8, w), f32), pltpu.VMEM((tb, w), f32), pltpu.VMEM((tb, w), f32),
                        pltpu.VMEM((tb, w), f32), pltpu.VMEM((1, w), f32)],
        compiler_params=_cparams(("parallel", "arbitrary")),
        name="rglru",
    )(rec, gate, conv_w.astype(f32), row(conv_b), wa.astype(f32), wx.astype(f32), row(ba), row(bx), row(lam),
      cinit, h0)


def _lru_step_kernel(rec_ref, gate_ref, b0_ref, b1_ref, b2_ref, cw_ref, cb_ref, wa_ref, wx_ref, ba_ref, bx_ref,
                     lam_ref, h0_ref, y_ref, h_ref):
    xc = b0_ref[...] * cw_ref[0:1, :]
    xc = xc + b1_ref[...] * cw_ref[1:2, :]
    xc = xc + b2_ref[...] * cw_ref[2:3, :]
    xc = xc + rec_ref[...] * cw_ref[3:4, :]
    xc = xc + cb_ref[...]
    a, b = _lru_gates(xc, wa_ref, wx_ref, ba_ref, bx_ref, lam_ref, 3)
    h = a * h0_ref[...] + b
    h_ref[...] = h
    y_ref[...] = _gelu_tanh(gate_ref[...]) * h


def _lru_sample(rec, gate, conv_buf, h0, conv_w, conv_b, wa, wx, ba, bx, lam):
    db, w = rec.shape
    row = lambda a: a.reshape(1, w).astype(f32)
    buf = conv_buf.astype(f32)
    return pl.pallas_call(
        _lru_step_kernel,
        out_shape=[jax.ShapeDtypeStruct((db, w), f32), jax.ShapeDtypeStruct((db, w), f32)],
        compiler_params=pltpu.CompilerParams(vmem_limit_bytes=VMEM_LIMIT),
        name="rglru_step",
    )(rec, gate, buf[:, 0], buf[:, 1], buf[:, 2], conv_w.astype(f32), row(conv_b), wa.astype(f32), wx.astype(f32),
      row(ba), row(bx), row(lam), h0.astype(f32))


def _top2_kernel(l_ref, e_ref, g_ref, cnt_ref, cnt_scr):
    @pl.when(pl.program_id(0) == 0)
    def _():
        cnt_scr[...] = jnp.zeros_like(cnt_scr)

    lg = l_ref[...]
    tm, w = lg.shape
    lane = lax.broadcasted_iota(i32, lg.shape, 1)
    lg = jnp.where(lane < N_EXPERTS, lg, -jnp.inf)
    m1 = jnp.max(lg, axis=-1, keepdims=True)
    i1 = jnp.min(jnp.where(lg == m1, lane, w), axis=-1, keepdims=True)
    lg2 = jnp.where(lane == i1, -jnp.inf, lg)
    m2 = jnp.max(lg2, axis=-1, keepdims=True)
    i2 = jnp.min(jnp.where(lg2 == m2, lane, w), axis=-1, keepdims=True)
    e2 = jnp.exp(m2 - m1)
    den = 1.0 + e2
    g_ref[...] = jnp.where(lane == 0, 1.0 / den, jnp.where(lane == 1, e2 / den, 0.0))
    hit1, hit2 = lane == i1, lane == i2
    routed = (hit1 | hit2).astype(bf16)
    r_i = lax.broadcasted_iota(i32, (tm, tm), 0)
    c_i = lax.broadcasted_iota(i32, (tm, tm), 1)
    before = jnp.dot((c_i < r_i).astype(bf16), routed, preferred_element_type=f32) + cnt_scr[...]
    rank1 = jnp.sum(jnp.where(hit1, before, 0.0), axis=-1, keepdims=True).astype(i32)
    rank2 = jnp.sum(jnp.where(hit2, before, 0.0), axis=-1, keepdims=True).astype(i32)
    e_ref[...] = jnp.where(lane == 0, i1, jnp.where(lane == 1, i2, jnp.where(lane == 2, rank1,
                                                                               jnp.where(lane == 3, rank2, 0))))
    cnt_scr[...] += jnp.sum(routed.astype(f32), axis=0, keepdims=True)
    cnt_ref[...] = cnt_scr[...].astype(i32)


def _top2(logits, *, tm):
    n, w = logits.shape
    return pl.pallas_call(
        _top2_kernel,
        grid=(n // tm,),
        in_specs=[pl.BlockSpec((tm, w), lambda i: (i, 0))],
        out_specs=[pl.BlockSpec((tm, w), lambda i: (i, 0)), pl.BlockSpec((tm, w), lambda i: (i, 0)),
                   pl.BlockSpec((1, w), lambda i: (0, 0))],
        out_shape=[jax.ShapeDtypeStruct((n, w), i32), jax.ShapeDtypeStruct((n, w), f32),
                   jax.ShapeDtypeStruct((1, w), i32)],
        scratch_shapes=[pltpu.VMEM((1, w), f32)],
        compiler_params=_cparams(("arbitrary",)),
        name="moe_top2",
    )(logits)


def _moe_kernel(be_ref, nu_ref, nv_ref, xs_ref, w1_ref, w3_ref, w2_ref, o_ref, acc_scr):
    del be_ref
    i = pl.program_id(0)
    f = pl.program_id(1)
    last = pl.num_programs(1) - 1
    used = i < nu_ref[0]
    parts = 4
    quarter = xs_ref.shape[0] // parts
    n_q = (nv_ref[i] + quarter - 1) // quarter

    @pl.when(used & (f == 0))
    def _():
        acc_scr[...] = jnp.zeros_like(acc_scr)

    def update(rows):
        x = xs_ref[rows, :]
        act = _silu(_mm(x, w1_ref[0])) * _mm(x, w3_ref[0])
        acc_scr[rows, :] += _mm(act, w2_ref[0])

    for q in range(1, parts + 1):
        @pl.when(used & (n_q == q))
        def _(q=q):
            update(slice(0, q * quarter))

    @pl.when(used & (f == last))
    def _():
        o_ref[...] = acc_scr[...]

    @pl.when(jnp.logical_not(used) & (f == last))
    def _():
        o_ref[...] = jnp.zeros_like(o_ref)


def _moe_experts(xs, blk_e, n_used, n_valid, w1, w3, w2, *, tm, tf):
    rows, d = xs.shape
    fdim = w1.shape[2]
    nf = fdim // tf

    def wcol(i, f, be, nu, nv):
        return (be[i], 0, jnp.where(i < nu[0], f, nf - 1))

    def wrow(i, f, be, nu, nv):
        return (be[i], jnp.where(i < nu[0], f, nf - 1), 0)

    return pl.pallas_call(
        _moe_kernel,
        grid_spec=pltpu.PrefetchScalarGridSpec(
            num_scalar_prefetch=3,
            grid=(rows // tm, nf),
            in_specs=[pl.BlockSpec((tm, d), lambda i, f, be, nu, nv: (i, 0)),
                      pl.BlockSpec((1, d, tf), wcol),
                      pl.BlockSpec((1, d, tf), wcol),
                      pl.BlockSpec((1, tf, d), wrow)],
            out_specs=pl.BlockSpec((tm, d), lambda i, f, be, nu, nv: (i, 0)),
            scratch_shapes=[pltpu.VMEM((tm, d), f32)]),
        out_shape=jax.ShapeDtypeStruct((rows, d), f32),
        compiler_params=_cparams(("arbitrary", "arbitrary")),
        name="moe_experts",
    )(blk_e, n_used, n_valid, xs, w1, w3, w2)


def _combine_kernel(x_ref, gate_ref, y0_ref, y1_ref, g_ref, nw_ref, o_ref):
    gw = g_ref[...]
    y = y0_ref[...] * gw[:, 0:1] + y1_ref[...] * gw[:, 1:2]
    x = x_ref[0] + gate_ref[0] * y
    o_ref[0] = x * lax.rsqrt(jnp.mean(x * x, axis=-1, keepdims=True) + EPS) * nw_ref[...]


def _moe_combine(x, gate, y0, y1, gw, norm_w, *, tm, row0):
    bsz, t, d = x.shape
    assert row0 % tm == 0
    tok = pl.BlockSpec((1, tm, d), lambda b, i: (b, i, 0))
    flat = lambda w: pl.BlockSpec((tm, w), lambda b, i: (row0 // tm + b * (t // tm) + i, 0))
    return pl.pallas_call(
        _combine_kernel,
        grid=(bsz, t // tm),
        in_specs=[tok, _mod_spec(gate, tm), flat(d), flat(d), flat(gw.shape[-1]),
                  pl.BlockSpec((1, d), lambda b, i: (0, 0))],
        out_specs=tok,
        out_shape=jax.ShapeDtypeStruct((bsz, t, d), f32),
        compiler_params=_cparams(("parallel", "parallel")),
        name="moe_combine",
    )(x, gate, y0, y1, gw, norm_w.reshape(1, d).astype(f32))


MOE_TM = 1024
MOE_TF = 512


def _moe_dispatch(e_idx, rank, counts, n_tok):
    tm = MOE_TM
    n_assign = n_tok * TOP_K
    padded = (counts + tm - 1) // tm * tm
    pend = jnp.cumsum(padded)
    pstart = pend - padded
    experts = jnp.arange(N_EXPERTS, dtype=i32)
    dest = jnp.sum(jnp.where(e_idx[:, :, None] == experts, pstart, 0), axis=-1) + rank
    n_blocks = (n_assign + N_EXPERTS * (tm - 1)) // tm
    rows = n_blocks * tm
    tok = jnp.broadcast_to(jnp.arange(n_tok, dtype=i32)[:, None], (n_tok, TOP_K))
    row_tok = (jnp.arange(rows, dtype=i32) % n_tok).at[dest.reshape(-1)].set(
        tok.reshape(-1), unique_indices=True, mode='promise_in_bounds')
    n_used = (pend[-1] // tm).astype(i32)
    blk = jnp.minimum(jnp.arange(n_blocks, dtype=i32), n_used - 1) * tm
    blk_e = jnp.minimum(jnp.sum((blk[:, None] >= pend[None, :]).astype(i32), axis=1), N_EXPERTS - 1)
    is_e = blk_e[:, None] == experts[None, :]
    first_row = jnp.sum(jnp.where(is_e, pstart, 0), axis=1)
    n_valid = jnp.clip(jnp.sum(jnp.where(is_e, counts, 0), axis=1) - (blk - first_row), 0, tm)
    return dest, row_tok, blk_e, n_used.reshape(1), n_valid.astype(i32)


def _w_in0_layout(w_in0):
    c_qkv = GDN_CONV_CH
    c_z = c_qkv + GDN_HEADS * GDN_DV
    c_ab = c_z + 2 * GDN_HEADS
    c_q = c_ab + NSA_HEADS * NSA_DH
    c_kv = c_q + 6 * NSA_KV_GROUPS * NSA_DH
    c_g = c_kv + 3 * NSA_HEADS
    assert c_g == w_in0.shape[1]
    n_misc = 2 * GDN_HEADS + 3 * NSA_HEADS
    w = jnp.concatenate([w_in0[:, :c_z], w_in0[:, c_ab:c_kv], w_in0[:, c_z:c_ab], w_in0[:, c_kv:c_g],
                         jnp.zeros((w_in0.shape[0], MISC_W - n_misc), w_in0.dtype)], axis=1)
    widths = (GDN_CONV_CH, GDN_HEADS * GDN_DV, NSA_HEADS * NSA_DH, 4 * NSA_KV_GROUPS * NSA_DH,
              2 * NSA_KV_GROUPS * NSA_DH, MISC_W)
    splits, s = [], 0
    for wd in widths:
        splits.append((s, s + wd))
        s += wd
    return w, tuple(splits)


def kernel(x_prompt, x_sample, c_prompt, c_sample, cache_nsa_kv, cache_nsa_win, state_gdn, state_gdn_conv, state_lru, state_lru_conv, page_table, rel_bias, w_ada, b_ada, norm_mix, norm_ffn, norm_final, w_in0, gdn_conv_w, gdn_a_log, gdn_dt_bias, gdn_norm_w, w_out0, ffn_w_gate, ffn_w_up, ffn_w_down, w_in1, lru_conv_w, lru_conv_b, lru_wa, lru_ba, lru_wx, lru_bx, lru_lambda, w_out1, moe_router, moe_w1, moe_w3, moe_w2):
    bsz, seq, d = x_prompt.shape
    db = x_sample.shape[0]
    assert x_sample.shape[1] == 1
    dh = NSA_DH

    n_c = bsz + db
    n_c_pad = -(-n_c // 8) * 8
    c_all = jnp.concatenate([c_prompt, c_sample, jnp.zeros((n_c_pad - n_c, d), f32)], axis=0)
    mods = _adaln(c_all, w_ada, b_ada).reshape(2, n_c_pad, N_MOD, d)
    mod_p = [[mods[l, :bsz, k].reshape(bsz, 1, d) for k in range(N_MOD)] for l in range(2)]
    mod_s = [[mods[l, bsz:n_c, k].reshape(1, db, d) for k in range(N_MOD)] for l in range(2)]

    w0, splits0 = _w_in0_layout(w_in0)
    splits1 = ((0, RNN_WIDTH), (RNN_WIDTH, 2 * RNN_WIDTH))
    router = jnp.concatenate([moe_router, jnp.zeros((d, MISC_W - N_EXPERTS), f32)], axis=1)
    bf = lambda w: w.astype(bf16)

    tm = PROMPT_TM
    xp = x_prompt
    qkv, z, q_b, kv4, kvwin, misc = _mod_matmul(xp, norm_mix[0], mod_p[0][0], mod_p[0][1], bf(w0), splits0,
                                                tm=tm, prec=1)
    o_a, p_gdn = _gdn(qkv, z, misc, gdn_conv_w, jnp.zeros((bsz, GDN_CONV - 1, GDN_CONV_CH), f32), gdn_a_log,
                      gdn_dt_bias, gdn_norm_w, jnp.zeros((bsz, GDN_HEADS, GDN_DK, GDN_DV), f32), tb=SCAN_TB, n_valid=seq)
    p_gdn_conv = qkv[:, seq - (GDN_CONV - 1):]
    o_b = _nsa_prompt(q_b, misc, kv4, kvwin, rel_bias)
    p_nsa_kv = kv4.reshape(bsz, seq, 4, NSA_KV_GROUPS, dh)
    keep = min(NSA_WINDOW, seq)
    p_nsa_win = kvwin[:, seq - keep:].reshape(bsz, keep, 2, NSA_KV_GROUPS, dh)
    xp = _proj_residual([o_a, o_b], bf(w_out0), xp, mod_p[0][2], tm=tm, prec=1)
    xp = _ffn(xp, norm_ffn[0], mod_p[0][3], mod_p[0][4], mod_p[0][5], bf(ffn_w_gate), bf(ffn_w_up), bf(ffn_w_down),
              tm=tm, tf=FFN_TF_PROMPT, prec=1)
    gate_br, rec_br = _mod_matmul(xp, norm_mix[1], mod_p[1][0], mod_p[1][1], bf(w_in1), splits1, tm=tm, prec=1)
    y_in, p_lru = _lru_prompt(rec_br, gate_br, lru_conv_w, lru_conv_b, lru_wa, lru_wx, lru_ba, lru_bx, lru_lambda,
                              tb=SCAN_TB)
    p_lru_conv = rec_br[:, seq - (RNN_CONV - 1):]
    xp = _proj_residual([y_in], bf(w_out1), xp, mod_p[1][2], tm=tm, prec=1)
    logit_p, h_p = _mod_matmul(xp, norm_ffn[1], mod_p[1][3], mod_p[1][4], router, ((0, MISC_W),), tm=tm, prec=3,
                               emit_h=f32)

    xs = x_sample.reshape(1, db, d)
    qkv_s, z_s, q_s, kv4_s, kvwin_s, misc_s = _mod_matmul(xs, norm_mix[0], mod_s[0][0], mod_s[0][1], w0, splits0,
                                                          tm=db, prec=3)
    per_seq = lambda a: a.reshape(db, 1, a.shape[-1])
    o_a_s, s_gdn = _gdn_step(per_seq(qkv_s), per_seq(z_s), per_seq(misc_s), gdn_conv_w, state_gdn_conv, gdn_a_log,
                             gdn_dt_bias, gdn_norm_w, state_gdn)
    o_a_s = o_a_s.reshape(1, db, GDN_HEADS * GDN_DV)
    s_gdn_conv = jnp.concatenate([state_gdn_conv[:, 1:], qkv_s.reshape(db, 1, GDN_CONV_CH)], axis=1)
    gates_s = jax.nn.sigmoid(misc_s[0, :, 2 * GDN_HEADS:2 * GDN_HEADS + 3 * NSA_HEADS]).reshape(db, NSA_HEADS, 3)
    o_b_s = _nsa_sample(q_s[0], gates_s, kv4_s[0], kvwin_s[0], cache_nsa_kv, cache_nsa_win, page_table, rel_bias)
    s_nsa_kv = kv4_s.reshape(db, 1, 4, NSA_KV_GROUPS, dh)
    s_nsa_win = jnp.concatenate([cache_nsa_win[:, 1:],
                                 kvwin_s.reshape(db, 1, 2, NSA_KV_GROUPS, dh).astype(cache_nsa_win.dtype)], axis=1)
    xs = _proj_residual([o_a_s, o_b_s.reshape(1, db, NSA_HEADS * dh)], w_out0, xs, mod_s[0][2], tm=db, prec=3)
    xs = _ffn(xs, norm_ffn[0], mod_s[0][3], mod_s[0][4], mod_s[0][5], ffn_w_gate, ffn_w_up, ffn_w_down,
              tm=db, tf=FFN_TF_SAMPLE, prec=3)
    gate_s, rec_s = _mod_matmul(xs, norm_mix[1], mod_s[1][0], mod_s[1][1], w_in1, splits1, tm=db, prec=3)
    y_in_s, s_lru = _lru_sample(rec_s[0], gate_s[0], state_lru_conv, state_lru, lru_conv_w, lru_conv_b, lru_wa, lru_wx,
                                lru_ba, lru_bx, lru_lambda)
    s_lru_conv = jnp.concatenate([state_lru_conv[:, 1:], rec_s.reshape(db, 1, RNN_WIDTH)], axis=1)
    xs = _proj_residual([y_in_s.reshape(1, db, RNN_WIDTH)], w_out1, xs, mod_s[1][2], tm=db, prec=3)
    logit_s, h_s = _mod_matmul(xs, norm_ffn[1], mod_s[1][3], mod_s[1][4], router, ((0, MISC_W),), tm=db, prec=3,
                               emit_h=f32)

    n_p = bsz * seq
    n_tok = n_p + db
    logits = jnp.concatenate([logit_p.reshape(n_p, MISC_W), logit_s.reshape(db, MISC_W)], axis=0)
    h_all = jnp.concatenate([h_p.reshape(n_p, d), h_s.reshape(db, d)], axis=0)
    route, gw, cnt = _top2(logits, tm=max(t for t in range(8, 1025, 8) if n_tok % t == 0))
    dest, row_tok, blk_e, n_used, n_valid = _moe_dispatch(route[:, :TOP_K], route[:, TOP_K:2 * TOP_K],
                                                          cnt[0, :N_EXPERTS], n_tok)
    yb = _moe_experts(h_all[row_tok], blk_e, n_used, n_valid, moe_w1, moe_w3, moe_w2, tm=MOE_TM, tf=MOE_TF)
    y0, y1 = yb[dest[:, 0]], yb[dest[:, 1]]
    y_prompt = _moe_combine(xp, mod_p[1][5], y0, y1, gw, norm_final, tm=tm, row0=0)
    y_sample = _moe_combine(xs, mod_s[1][5], y0, y1, gw, norm_final, tm=db, row0=n_p)

    return (y_prompt, y_sample.reshape(db, 1, d),
            p_nsa_kv, p_nsa_win, p_gdn, p_gdn_conv, p_lru.reshape(bsz, RNN_WIDTH), p_lru_conv,
            s_nsa_kv, s_nsa_win, s_gdn, s_gdn_conv, s_lru, s_lru_conv)
```

```python
import functools
import math

import numpy as np
import jax
import jax.numpy as jnp
from jax import lax
from jax.experimental import pallas as pl
from jax.experimental.pallas import tpu as pltpu

f32 = jnp.float32
bf16 = jnp.bfloat16
i32 = jnp.int32

D_MODEL = 1024
EPS = 1e-6
N_MOD = 6
GDN_HEADS = 4
GDN_DK = 128
GDN_DV = 128
GDN_CONV = 4
GDN_CHUNK = 64
GDN_CONV_CH = GDN_HEADS * (2 * GDN_DK + GDN_DV)
NSA_HEADS = 8
NSA_KV_GROUPS = 2
NSA_HPG = NSA_HEADS // NSA_KV_GROUPS
NSA_DH = 64
NSA_BLOCK = 64
NSA_TOPN = 16
NSA_LOCAL = 2
NSA_WINDOW = 512
NSA_QBLOCK = 128
REL_BUCKETS = 32
REL_MAX_DIST = 2048
RNN_WIDTH = D_MODEL
RNN_BLOCKS = 8
RNN_BW = RNN_WIDTH // RNN_BLOCKS
RNN_CONV = 4
RG_C = 8.0
FFN_DIM = 2816
N_EXPERTS = 8
TOP_K = 2
EXPERT_DIM = 3584

NEG = -1e30
PAIR = 2 * NSA_BLOCK
NEAR_PAIRS = 13
MISC_W = 128
VMEM_LIMIT = 56 * 1024 * 1024
PROMPT_TM = 512
SCAN_TB = 256
FFN_TF_PROMPT = FFN_DIM // 2
FFN_TF_SAMPLE = 256
MEANS_TB = 512


def _cparams(sem, vmem=VMEM_LIMIT):
    return pltpu.CompilerParams(dimension_semantics=sem, vmem_limit_bytes=vmem)


def _mm(a, b):
    return jnp.dot(a.astype(bf16), b.astype(bf16), preferred_element_type=f32)


def _mm_nt(a, b):
    return lax.dot_general(a.astype(bf16), b.astype(bf16), (((1,), (1,)), ((), ())),
                           preferred_element_type=f32)


def _split2(a):
    hi = a.astype(bf16)
    lo = (a - hi.astype(f32)).astype(bf16)
    return hi, lo


def _mm3(a, b):
    ah, al = _split2(a)
    bh, bl = _split2(b)
    return (jnp.dot(ah, bh, preferred_element_type=f32) + jnp.dot(ah, bl, preferred_element_type=f32)
            + jnp.dot(al, bh, preferred_element_type=f32))


def _mm3_nt(a, b):
    ah, al = _split2(a)
    bh, bl = _split2(b)
    dn = (((1,), (1,)), ((), ()))
    return (lax.dot_general(ah, bh, dn, preferred_element_type=f32)
            + lax.dot_general(ah, bl, dn, preferred_element_type=f32)
            + lax.dot_general(al, bh, dn, preferred_element_type=f32))


def _mm_01(m01, a):
    hi = a.astype(bf16)
    r1 = a - hi.astype(f32)
    mid = r1.astype(bf16)
    lo = (r1 - mid.astype(f32)).astype(bf16)
    return (jnp.dot(m01, hi, preferred_element_type=f32) + jnp.dot(m01, mid, preferred_element_type=f32)
            + jnp.dot(m01, lo, preferred_element_type=f32))


def _dotp(a, b, prec):
    return _mm3(a, b) if prec == 3 else _mm(a, b)


def _silu(x):
    return x * jax.nn.sigmoid(x)


def _softplus(x):
    return jnp.maximum(x, 0.0) + jnp.log1p(jnp.exp(-jnp.abs(x)))


def _modulate(x, gain, shift, scale):
    r = lax.rsqrt(jnp.mean(x * x, axis=-1, keepdims=True) + EPS)
    return x * r * gain * (1.0 + scale) + shift


def _ada_kernel(c_ref, w_ref, b_ref, o_ref):
    o_ref[0] = _mm3(_silu(c_ref[...]), w_ref[0]) + b_ref[0]


def _adaln(c_all, w_ada, b_ada):
    rows = c_all.shape[0]
    depth, d, n = w_ada.shape
    tn = 1536
    return pl.pallas_call(
        _ada_kernel,
        grid=(depth, n // tn),
        in_specs=[pl.BlockSpec((rows, d), lambda l, j: (0, 0)),
                  pl.BlockSpec((1, d, tn), lambda l, j: (l, 0, j)),
                  pl.BlockSpec((1, 1, tn), lambda l, j: (l, 0, j))],
        out_specs=pl.BlockSpec((1, rows, tn), lambda l, j: (l, 0, j)),
        out_shape=jax.ShapeDtypeStruct((depth, rows, n), f32),
        compiler_params=_cparams(("arbitrary", "arbitrary")),
        name="adaln",
    )(c_all, w_ada, b_ada.reshape(depth, 1, n))


def _mod_spec(mod, tm):
    r = mod.shape[1]
    if r == 1:
        return pl.BlockSpec((1, 1, mod.shape[2]), lambda b, i: (b, 0, 0))
    return pl.BlockSpec((1, tm, mod.shape[2]), lambda b, i: (b, i, 0))


def _modmm_kernel(x_ref, gain_ref, shift_ref, scale_ref, w_ref, *o_refs, splits, prec, emit_h):
    h = _modulate(x_ref[0], gain_ref[...], shift_ref[0], scale_ref[0])
    if emit_h:
        o_refs[-1][0] = h.astype(o_refs[-1].dtype)
    hh = _split2(h) if prec == 3 else h.astype(bf16)
    for o_ref, (a, b) in zip(o_refs, splits):
        w = w_ref[:, a:b]
        if prec == 3:
            wh, wl = _split2(w)
            acc = (jnp.dot(hh[0], wh, preferred_element_type=f32) + jnp.dot(hh[0], wl, preferred_element_type=f32)
                   + jnp.dot(hh[1], wh, preferred_element_type=f32))
        else:
            acc = jnp.dot(hh, w, preferred_element_type=f32)
        o_ref[0] = acc


def _mod_matmul(x, gain, shift, scale, w, splits, *, tm, prec, emit_h=None):
    bsz, t, d = x.shape
    out_shape = [jax.ShapeDtypeStruct((bsz, t, b - a), f32) for a, b in splits]
    out_specs = [pl.BlockSpec((1, tm, b - a), lambda bi, i: (bi, i, 0)) for a, b in splits]
    if emit_h is not None:
        out_shape.append(jax.ShapeDtypeStruct((bsz, t, d), emit_h))
        out_specs.append(pl.BlockSpec((1, tm, d), lambda bi, i: (bi, i, 0)))
    return pl.pallas_call(
        functools.partial(_modmm_kernel, splits=tuple(splits), prec=prec, emit_h=emit_h is not None),
        grid=(bsz, t // tm),
        in_specs=[pl.BlockSpec((1, tm, d), lambda bi, i: (bi, i, 0)),
                  pl.BlockSpec((1, d), lambda bi, i: (0, 0)),
                  _mod_spec(shift, tm), _mod_spec(scale, tm),
                  pl.BlockSpec(w.shape, lambda bi, i: (0, 0))],
        out_specs=out_specs,
        out_shape=out_shape,
        compiler_params=_cparams(("parallel", "parallel")),
        name="mod_matmul",
    )(x, gain.reshape(1, d), shift, scale, w)


def _projres_kernel(*refs, n_lhs, ksplits, prec, final_norm):
    lhs = refs[:n_lhs]
    w_ref, x_ref, gate_ref = refs[n_lhs:n_lhs + 3]
    o_ref = refs[-1]
    acc = None
    for l_ref, (a, b) in zip(lhs, ksplits):
        part = _dotp(l_ref[0], w_ref[a:b, :], prec)
        acc = part if acc is None else acc + part
    y = x_ref[0] + gate_ref[0] * acc
    if final_norm:
        nw_ref = refs[n_lhs + 3]
        y = y * lax.rsqrt(jnp.mean(y * y, axis=-1, keepdims=True) + EPS) * nw_ref[...]
    o_ref[0] = y


def _proj_residual(lhs_list, w, x, gate, *, tm, prec, norm_w=None):
    bsz, t, d = x.shape
    ksplits, k0 = [], 0
    for l in lhs_list:
        ksplits.append((k0, k0 + l.shape[-1]))
        k0 += l.shape[-1]
    in_specs = [pl.BlockSpec((1, tm, l.shape[-1]), lambda bi, i: (bi, i, 0)) for l in lhs_list]
    in_specs += [pl.BlockSpec(w.shape, lambda bi, i: (0, 0)),
                 pl.BlockSpec((1, tm, d), lambda bi, i: (bi, i, 0)),
                 _mod_spec(gate, tm)]
    args = list(lhs_list) + [w, x, gate]
    if norm_w is not None:
        in_specs.append(pl.BlockSpec((1, d), lambda bi, i: (0, 0)))
        args.append(norm_w.reshape(1, d))
    return pl.pallas_call(
        functools.partial(_projres_kernel, n_lhs=len(lhs_list), ksplits=tuple(ksplits), prec=prec,
                          final_norm=norm_w is not None),
        grid=(bsz, t // tm),
        in_specs=in_specs,
        out_specs=pl.BlockSpec((1, tm, d), lambda bi, i: (bi, i, 0)),
        out_shape=jax.ShapeDtypeStruct((bsz, t, d), f32),
        compiler_params=_cparams(("parallel", "parallel")),
        name="proj_residual",
    )(*args)


def _ffn_kernel(x_ref, gain_ref, shift_ref, scale_ref, gate_ref, wg_ref, wu_ref, wd_ref, o_ref,
                h_scr, acc_scr, *, prec):
    f = pl.program_id(2)

    @pl.when(f == 0)
    def _():
        h_scr[...] = _modulate(x_ref[0], gain_ref[...], shift_ref[0], scale_ref[0]).astype(h_scr.dtype)
        acc_scr[...] = jnp.zeros_like(acc_scr)

    h = h_scr[...]
    act = _silu(_dotp(h, wg_ref[...], prec)) * _dotp(h, wu_ref[...], prec)
    acc_scr[...] += _dotp(act, wd_ref[...], prec)

    @pl.when(f == pl.num_programs(2) - 1)
    def _():
        o_ref[0] = x_ref[0] + gate_ref[0] * acc_scr[...]


def _ffn(x, gain, shift, scale, gate, wg, wu, wd, *, tm, tf, prec):
    bsz, t, d = x.shape
    fdim = wg.shape[1]
    mod_specs = []
    for mod in (shift, scale, gate):
        if mod.shape[1] == 1:
            mod_specs.append(pl.BlockSpec((1, 1, d), lambda b, i, f: (b, 0, 0)))
        else:
            mod_specs.append(pl.BlockSpec((1, tm, d), lambda b, i, f: (b, i, 0)))
    return pl.pallas_call(
        functools.partial(_ffn_kernel, prec=prec),
        grid=(bsz, t // tm, fdim // tf),
        in_specs=[pl.BlockSpec((1, tm, d), lambda b, i, f: (b, i, 0)),
                  pl.BlockSpec((1, d), lambda b, i, f: (0, 0)),
                  mod_specs[0], mod_specs[1], mod_specs[2],
                  pl.BlockSpec((d, tf), lambda b, i, f: (0, f)),
                  pl.BlockSpec((d, tf), lambda b, i, f: (0, f)),
                  pl.BlockSpec((tf, d), lambda b, i, f: (f, 0))],
        out_specs=pl.BlockSpec((1, tm, d), lambda b, i, f: (b, i, 0)),
        out_shape=jax.ShapeDtypeStruct((bsz, t, d), f32),
        scratch_shapes=[pltpu.VMEM((tm, d), f32 if prec == 3 else bf16), pltpu.VMEM((tm, d), f32)],
        compiler_params=_cparams(("parallel", "parallel", "arbitrary")),
        name="ffn",
    )(x, gain.reshape(1, d), shift, scale, gate, wg, wu, wd)


def _gdn_kernel(qkv_ref, z_ref, misc_ref, cw_ref, cinit_ref, hp_ref, nw_ref, s0_ref,
                o_ref, sfin_ref, xc_scr, s_scr, *, tb, n_valid):
    j = pl.program_id(1)
    c = GDN_CHUNK
    nh = GDN_HEADS

    @pl.when(j == 0)
    def _():
        xc_scr[0:8, :] = cinit_ref[0]
        s_scr[...] = s0_ref[0]

    x = qkv_ref[0]
    xc_scr[8:8 + tb, :] = x
    y = xc_scr[5:5 + tb, :] * cw_ref[0:1, :]
    y = y + xc_scr[6:6 + tb, :] * cw_ref[1:2, :]
    y = y + xc_scr[7:7 + tb, :] * cw_ref[2:3, :]
    y = y + x * cw_ref[3:4, :]
    xc_scr[0:8, :] = xc_scr[tb:tb + 8, :]
    y = _silu(y)

    misc = misc_ref[0]
    row = lax.broadcasted_iota(i32, (tb, MISC_W), 0) + j * tb
    live = row < n_valid
    log_a = jnp.where(live, hp_ref[0:1, :] * _softplus(misc + hp_ref[1:2, :]), 0.0)
    beta = jnp.where(live, jax.nn.sigmoid(misc), 0.0)

    r_i = lax.broadcasted_iota(i32, (tb, tb), 0)
    c_i = lax.broadcasted_iota(i32, (tb, tb), 1)
    sh = int(math.log2(c))
    ltri = (((r_i >> sh) == (c_i >> sh)) & (c_i <= r_i)).astype(bf16)
    g = _mm_01(ltri, log_a)
    g_t = g.T

    ri = lax.broadcasted_iota(i32, (c, c), 0)
    ci = lax.broadcasted_iota(i32, (c, c), 1)
    incl = ci <= ri
    strict = ci < ri
    eye = (ci == ri).astype(f32)
    quad = []
    for lvl in range(int(math.log2(c))):
        quad.append(((ri >> (lvl + 1)) == (ci >> (lvl + 1))) & (((ri >> lvl) & 1) == 1) & (((ci >> lvl) & 1) == 0))

    nchunk = tb // c
    units = [(n, h) for n in range(nchunk) for h in range(nh)]

    def stack(fn):
        return jnp.stack([fn(n * c, h) for n, h in units], axis=0)

    def bmm(eq, a_, b_):
        return jnp.einsum(eq, a_.astype(bf16), b_.astype(bf16), preferred_element_type=f32)

    q = stack(lambda r0, h: y[r0:r0 + c, h * GDN_DK:(h + 1) * GDN_DK])
    k = stack(lambda r0, h: y[r0:r0 + c, (nh + h) * GDN_DK:(nh + h + 1) * GDN_DK])
    v = stack(lambda r0, h: y[r0:r0 + c, 2 * nh * GDN_DK + h * GDN_DV:2 * nh * GDN_DK + (h + 1) * GDN_DV])
    q = q * lax.rsqrt(jnp.sum(q * q, axis=-1, keepdims=True) + EPS) * (GDN_DK ** -0.5)
    k = k * lax.rsqrt(jnp.sum(k * k, axis=-1, keepdims=True) + EPS)
    g_col = stack(lambda r0, h: g[r0:r0 + c, h:h + 1])
    g_row = stack(lambda r0, h: g_t[h:h + 1, r0:r0 + c])
    b_col = stack(lambda r0, h: beta[r0:r0 + c, nh + h:nh + h + 1])
    gam = jnp.where(incl, jnp.exp(jnp.where(incl, g_col - g_row, 0.0)), 0.0)
    kk = bmm('uid,ujd->uij', k, k)
    a = jnp.where(strict, b_col * gam * kk, 0.0)
    p = eye - jnp.where(quad[0], a, 0.0)
    for lvl in range(1, len(quad)):
        m = bmm('uij,ujk->uik', jnp.where(quad[lvl], a, 0.0), p)
        p = p - bmm('uij,ujk->uik', p, m)
    e_g = jnp.exp(g_col)
    sol = bmm('uij,ujd->uid', p, jnp.concatenate([b_col * v, (b_col * e_g) * k], axis=-1))
    vb, w = sol[:, :, :GDN_DV], sol[:, :, GDN_DV:]
    aqk = bmm('uid,ujd->uij', q, k) * gam
    qg = q * e_g
    g_last = g_col[:, c - 1:c, :]
    kd_t = jnp.swapaxes(k * jnp.exp(g_last - g_col), 1, 2)
    gc = jnp.exp(g_last)

    s = s_scr[...]
    for n in range(nchunk):
        sl = slice(n * nh, (n + 1) * nh)
        u = vb[sl] - bmm('hcd,hde->hce', w[sl], s)
        o = bmm('hcd,hde->hce', qg[sl], s) + bmm('hij,hje->hie', aqk[sl], u)
        s = gc[sl] * s + bmm('hdc,hce->hde', kd_t[sl], u)
        o = o * lax.rsqrt(jnp.mean(o * o, axis=-1, keepdims=True) + EPS) * nw_ref[...]
        for h in range(nh):
            zs = z_ref[0, n * c:(n + 1) * c, h * GDN_DV:(h + 1) * GDN_DV]
            o_ref[0, n * c:(n + 1) * c, h * GDN_DV:(h + 1) * GDN_DV] = (o[h] * _silu(zs)).astype(o_ref.dtype)
    s_scr[...] = s

    @pl.when(j == pl.num_programs(1) - 1)
    def _():
        sfin_ref[0] = s_scr[...]


def _gdn(qkv_raw, z, misc, conv_w, conv_buf, a_log, dt_bias, norm_w, s0, *, tb, n_valid):
    bsz, tp, ch = qkv_raw.shape
    cinit = jnp.concatenate([jnp.zeros((bsz, 5, ch), f32), conv_buf.astype(f32)], axis=1)
    hp = jnp.zeros((8, MISC_W), f32)
    hp = hp.at[0, :GDN_HEADS].set(-jnp.exp(a_log.astype(f32))).at[1, :GDN_HEADS].set(dt_bias.astype(f32))
    zw = GDN_HEADS * GDN_DV
    return pl.pallas_call(
        functools.partial(_gdn_kernel, tb=tb, n_valid=n_valid),
        grid=(bsz, tp // tb),
        in_specs=[pl.BlockSpec((1, tb, ch), lambda b, j: (b, j, 0)),
                  pl.BlockSpec((1, tb, zw), lambda b, j: (b, j, 0)),
                  pl.BlockSpec((1, tb, MISC_W), lambda b, j: (b, j, 0)),
                  pl.BlockSpec((GDN_CONV, ch), lambda b, j: (0, 0)),
                  pl.BlockSpec((1, 8, ch), lambda b, j: (b, 0, 0)),
                  pl.BlockSpec((8, MISC_W), lambda b, j: (0, 0)),
                  pl.BlockSpec((1, GDN_DV), lambda b, j: (0, 0)),
                  pl.BlockSpec((1, GDN_HEADS, GDN_DK, GDN_DV), lambda b, j: (b, 0, 0, 0))],
        out_specs=[pl.BlockSpec((1, tb, zw), lambda b, j: (b, j, 0)),
                   pl.BlockSpec((1, GDN_HEADS, GDN_DK, GDN_DV), lambda b, j: (b, 0, 0, 0))],
        out_shape=[jax.ShapeDtypeStruct((bsz, tp, zw), bf16),
                   jax.ShapeDtypeStruct((bsz, GDN_HEADS, GDN_DK, GDN_DV), f32)],
        scratch_shapes=[pltpu.VMEM((tb + 8, ch), f32), pltpu.VMEM((GDN_HEADS, GDN_DK, GDN_DV), f32)],
        compiler_params=_cparams(("parallel", "arbitrary")),
        name="gdn",
    )(qkv_raw, z, misc, conv_w.astype(f32), cinit, hp, norm_w.reshape(1, GDN_DV).astype(f32), s0.astype(f32))


def _gdn_step_kernel(x_ref, buf_ref, cw_ref, z_ref, misc_ref, hp_ref, nw_ref, s0_ref, o_ref, s_ref):
    nh = GDN_HEADS
    buf = buf_ref[0]
    y = buf[0:1] * cw_ref[0:1, :]
    y = y + buf[1:2] * cw_ref[1:2, :]
    y = y + buf[2:3] * cw_ref[2:3, :]
    y = _silu(y + x_ref[0] * cw_ref[3:4, :])
    misc = misc_ref[0]
    a_all = jnp.exp(hp_ref[0:1, :] * _softplus(misc + hp_ref[1:2, :]))
    b_all = jax.nn.sigmoid(misc)
    eye = lax.broadcasted_iota(i32, (GDN_DK, GDN_DK), 0) == lax.broadcasted_iota(i32, (GDN_DK, GDN_DK), 1)
    pad = jnp.zeros((6, GDN_DK), f32)
    outs = []
    for h in range(nh):
        q = y[:, h * GDN_DK:(h + 1) * GDN_DK]
        k = y[:, (nh + h) * GDN_DK:(nh + h + 1) * GDN_DK]
        v = y[:, 2 * nh * GDN_DK + h * GDN_DV:2 * nh * GDN_DK + (h + 1) * GDN_DV]
        q = q * lax.rsqrt(jnp.sum(q * q, axis=-1, keepdims=True) + EPS) * (GDN_DK ** -0.5)
        k = k * lax.rsqrt(jnp.sum(k * k, axis=-1, keepdims=True) + EPS)
        a = a_all[:, h:h + 1]
        b = b_all[:, nh + h:nh + h + 1]
        s = s0_ref[0, h]
        ks_qs = _mm3(jnp.concatenate([k, q, pad], axis=0), s)
        u = b * v - (b * a) * ks_qs[0:1]
        o = a * ks_qs[1:2] + jnp.sum(q * k, axis=-1, keepdims=True) * u
        k_col = jnp.sum(jnp.where(eye, k, 0.0), axis=1, keepdims=True)
        s_ref[0, h] = a * s + k_col * u
        o = o * lax.rsqrt(jnp.mean(o * o, axis=-1, keepdims=True) + EPS) * nw_ref[...]
        outs.append(o * _silu(z_ref[0][:, h * GDN_DV:(h + 1) * GDN_DV]))
    o_ref[0] = jnp.concatenate(outs, axis=1)


def _gdn_step(qkv_raw, z, misc, conv_w, conv_buf, a_log, dt_bias, norm_w, s0):
    db, _, ch = qkv_raw.shape
    hp = jnp.zeros((8, MISC_W), f32)
    hp = hp.at[0, :GDN_HEADS].set(-jnp.exp(a_log.astype(f32))).at[1, :GDN_HEADS].set(dt_bias.astype(f32))
    zw = GDN_HEADS * GDN_DV
    per_b = lambda shape: pl.BlockSpec((1,) + shape, lambda b: (b,) + (0,) * len(shape))
    const = lambda shape: pl.BlockSpec(shape, lambda b: (0,) * len(shape))
    return pl.pallas_call(
        _gdn_step_kernel,
        grid=(db,),
        in_specs=[per_b((1, ch)), per_b((GDN_CONV - 1, ch)), const((GDN_CONV, ch)), per_b((1, zw)),
                  per_b((1, MISC_W)), const((8, MISC_W)), const((1, GDN_DV)), per_b((GDN_HEADS, GDN_DK, GDN_DV))],
        out_specs=[per_b((1, zw)), per_b((GDN_HEADS, GDN_DK, GDN_DV))],
        out_shape=[jax.ShapeDtypeStruct((db, 1, zw), f32),
                   jax.ShapeDtypeStruct((db, GDN_HEADS, GDN_DK, GDN_DV), f32)],
        compiler_params=_cparams(("parallel",)),
        name="gdn_step",
    )(qkv_raw, conv_buf.astype(f32), conv_w.astype(f32), z, misc, hp, norm_w.reshape(1, GDN_DV).astype(f32),
      s0.astype(f32))


def _bucket_thresholds():
    exact = REL_BUCKETS // 2
    d = np.arange(0, 4 * REL_MAX_DIST, dtype=np.int64)

    def buckets(ft):
        nf = np.maximum(d, exact).astype(ft)
        large = exact + (np.log(nf / ft(exact)) / ft(math.log(REL_MAX_DIST / exact)) * ft(REL_BUCKETS - exact)).astype(np.int32)
        return np.where(d < exact, d, np.minimum(large, REL_BUCKETS - 1))

    b64, b32 = buckets(np.float64), buckets(np.float32)
    assert np.array_equal(b64, b32) and b64[-1] == REL_BUCKETS - 1 and np.all(np.diff(b64) >= 0)
    return [int(np.argmax(b64 >= k)) for k in range(1, REL_BUCKETS)]


_BUCKET_THR = _bucket_thresholds()
FAR_DIST = _BUCKET_THR[-1]


def _bias_kernel(rb_ref, d_ref, o_ref, *, slab, group_heads):
    rows = d_ref.shape[-2]
    for s in range(rows // slab):
        d = d_ref[0, 0, s * slab:(s + 1) * slab, :]
        h = pl.program_id(0) * group_heads + s
        val = jnp.full(d.shape, rb_ref[0, h], f32)
        for k in range(1, REL_BUCKETS):
            val = jnp.where(d >= _BUCKET_THR[k - 1], rb_ref[k, h], val)
        o_ref[0, 0, s * slab:(s + 1) * slab, :] = jnp.where(d < 0, NEG, val)


def _bias_table(rel_bias, dist, *, slab):
    g, nt, rows, cols = dist.shape
    return pl.pallas_call(
        functools.partial(_bias_kernel, slab=slab, group_heads=rows // slab),
        grid=(g, nt),
        in_specs=[pl.BlockSpec(memory_space=pltpu.SMEM),
                  pl.BlockSpec((1, 1, rows, cols), lambda a, b: (a, b, 0, 0))],
        out_specs=pl.BlockSpec((1, 1, rows, cols), lambda a, b: (a, b, 0, 0)),
        out_shape=jax.ShapeDtypeStruct(dist.shape, f32),
        compiler_params=_cparams(("arbitrary", "arbitrary")),
        name="bias_table",
    )(rel_bias.astype(f32), dist)


def _blockmean_kernel(x_ref, o_ref):
    x = x_ref[0]
    nb = x.shape[0] // NSA_BLOCK
    o_ref[0] = jnp.sum(x.reshape(nb, NSA_BLOCK, x.shape[1]), axis=1) * (1.0 / NSA_BLOCK)


def _block_means(kv4, *, tb):
    bsz, t, _ = kv4.shape
    w = 2 * NSA_KV_GROUPS * NSA_DH
    return pl.pallas_call(
        _blockmean_kernel,
        grid=(bsz, t // tb),
        in_specs=[pl.BlockSpec((1, tb, w), lambda b, i: (b, i, 0))],
        out_specs=pl.BlockSpec((1, tb // NSA_BLOCK, w), lambda b, i: (b, i, 0)),
        out_shape=jax.ShapeDtypeStruct((bsz, t // NSA_BLOCK, w), f32),
        compiler_params=_cparams(("parallel", "parallel")),
        name="nsa_block_means",
    )(kv4)


LOG2E = 1.4426950408889634
NSA_KEY_TILE = 4 * PAIR


def _nsa_prompt_kernel(q_ref, misc_ref, kcvc_ref, kaug_ref, vgt_ref, kw_ref, vw_ref,
                       tabs_ref, tabw_ref, tabc_ref, c31_ref, o_ref, qaug_scr, s_scr):
    g = pl.program_id(1)
    i = pl.program_id(2)
    qb, dh, hpg = NSA_QBLOCK, NSA_DH, NSA_HPG
    is_g0 = g == 0
    nn = (((1,), (1,)), ((), ()))

    def ghalf(a):
        return jnp.where(is_g0, a[:, :dh], a[:, dh:])

    qt = q_ref[0] * (dh ** -0.5)
    qh = [qt[:, h * dh:(h + 1) * dh] for h in range(hpg)]
    kcvc = kcvc_ref[0]
    nblk = kcvc.shape[0]
    kc = ghalf(kcvc[:, :2 * dh])
    vc = ghalf(kcvc[:, 2 * dh:])

    rows = hpg * qb
    qs = jnp.concatenate(qh, axis=0)
    zero = jnp.zeros_like(qs)
    q2f = jnp.concatenate([jnp.where(is_g0, qs, zero), jnp.where(is_g0, zero, qs)], axis=1) * LOG2E
    q2 = q2f.astype(bf16)

    blk4 = lax.broadcasted_iota(i32, (rows, nblk), 1)
    qrow4 = lax.broadcasted_iota(i32, (rows, nblk), 0) & (qb - 1)
    first_tab = 2 * i - 2 * NEAR_PAIRS + 1
    shift = (first_tab + 4 * nblk) % nblk
    readable = blk4 * NSA_BLOCK + (NSA_BLOCK - 1) <= i * qb + qrow4
    bias = jnp.where(blk4 < first_tab, c31_ref[0], pltpu.roll(tabc_ref[0], shift, 1))
    s_c = jnp.where(readable, _mm3_nt(qs, kc) + bias, NEG)
    m_c = jnp.max(s_c, axis=-1, keepdims=True)
    p_c = jnp.where(readable, jnp.exp(s_c - m_c), 0.0)
    p_c = p_c / jnp.maximum(jnp.sum(p_c, axis=-1, keepdims=True), 1e-30)
    o_cmp = _mm(p_c, vc)
    imp = p_c[0:qb]
    for h in range(1, hpg):
        imp = imp + p_c[h * qb:(h + 1) * qb]

    span = NSA_WINDOW + qb
    ws = pl.multiple_of(i * qb, qb)
    s_w = lax.dot_general(q2, kw_ref[0, pl.ds(ws, span), :], nn, preferred_element_type=f32) + tabw_ref[0]
    in_seq = lax.broadcasted_iota(i32, (rows, span), 1) + (i * qb - NSA_WINDOW) >= 0
    s_w = jnp.where(in_seq, s_w, NEG)
    m_w = jnp.max(s_w, axis=-1, keepdims=True)
    p_w = jnp.exp2(s_w - m_w)
    l_w = jnp.sum(p_w, axis=-1, keepdims=True)
    o_win = ghalf(jnp.dot(p_w.astype(bf16), vw_ref[0, pl.ds(ws, span), :], preferred_element_type=f32))
    o_win = o_win / jnp.maximum(l_w, 1e-30)

    blk = lax.broadcasted_iota(i32, (nblk, qb), 0)
    cur = (i * qb + lax.broadcasted_iota(i32, (nblk, qb), 1)) >> int(math.log2(NSA_BLOCK))
    valid = blk <= cur
    forced = valid & ((blk == 0) | (blk > cur - NSA_LOCAL))
    imp_t = jnp.where(forced, -jnp.inf, jnp.where(valid, imp.T, -1.0))
    picked = forced
    for _ in range(min(NSA_TOPN, nblk) - 1 - NSA_LOCAL):
        mx = jnp.max(imp_t, axis=0, keepdims=True)
        first = jnp.min(jnp.where(imp_t == mx, blk, nblk), axis=0, keepdims=True)
        hit = blk == first
        picked = picked | hit
        imp_t = jnp.where(hit, -jnp.inf, imp_t)
    msel_t = jnp.where(picked & valid, 0.0, NEG).astype(bf16)

    qaug_scr[...] = jnp.concatenate([q2f.T.astype(bf16), jnp.concatenate([msel_t] * hpg, axis=1)], axis=0)
    tk = s_scr.shape[0]
    ppt = tk // PAIR
    t_d = i // ppt

    def scores(t):
        ks = pl.multiple_of(t * tk, tk)
        s = jnp.dot(kaug_ref[0, pl.ds(ks, tk), :], qaug_scr[...], preferred_element_type=f32)
        tabs = [tabs_ref[0, jnp.clip(i - ppt * t - j, 0, NEAR_PAIRS)] for j in range(ppt)]
        return s + jnp.concatenate(tabs, axis=0)

    s_scr[...] = scores(t_d)

    def tile(k_it, carry):
        m, acc = carry
        t = t_d - k_it
        s = s_scr[...]
        s_next = scores(jnp.maximum(t - 1, 0))
        m_new = jnp.maximum(m, jnp.max(s, axis=0, keepdims=True))
        alpha = jnp.exp2(m - m_new)
        p = jnp.exp2(s - m_new).astype(bf16)
        ks = pl.multiple_of(t * tk, tk)
        acc = alpha * acc + jnp.dot(vgt_ref[0, 0, :, pl.ds(ks, tk)], p, preferred_element_type=f32)
        s_scr[...] = s_next
        return m_new, acc

    _, acc_t = lax.fori_loop(0, t_d + 1, tile, (jnp.full((1, rows), NEG, f32), jnp.zeros((2 * dh, rows), f32)))
    o_sel = (acc_t[:dh] / jnp.maximum(acc_t[dh:dh + 1], 1e-30)).T

    gates = jax.nn.sigmoid(misc_ref[0])
    outs = []
    for h in range(hpg):
        r = slice(h * qb, (h + 1) * qb)
        acc = None
        for jb, branch in enumerate((o_cmp[r], o_sel[r], o_win[r])):
            la = 8 + 3 * h + jb
            lb = 8 + 3 * (hpg + h) + jb
            term = jnp.where(is_g0, gates[:, la:la + 1], gates[:, lb:lb + 1]) * branch
            acc = term if acc is None else acc + term
        outs.append(acc)
    o_ref[0] = jnp.concatenate(outs, axis=1).astype(o_ref.dtype)


def _nsa_tables(rel_bias):
    qb, hpg = NSA_QBLOCK, NSA_HPG
    rows = hpg * qb
    q = (np.arange(rows) % qb)[:, None]
    c = np.arange(PAIR)[None, :]
    d_sel = [PAIR * idx + q - c for idx in range(NEAR_PAIRS + 1)]
    assert PAIR * NEAR_PAIRS - (PAIR - 1) >= FAR_DIST
    mm = 2 * NEAR_PAIRS - 1 - c
    d_cmp = np.maximum(np.where(mm >= -1, NSA_BLOCK * mm + q - (NSA_BLOCK - 1), 0), 0)
    assert NSA_BLOCK * (2 * NEAR_PAIRS) - (NSA_BLOCK - 1) >= FAR_DIST
    cw = np.arange(NSA_WINDOW + qb)[None, :]
    d_win = q + NSA_WINDOW - cw
    d_win = np.where((d_win >= 0) & (d_win < NSA_WINDOW), d_win, -1)
    n_win = (NSA_WINDOW + qb) // PAIR
    tiles = d_sel + [d_cmp] + [d_win[:, k * PAIR:(k + 1) * PAIR] for k in range(n_win)]
    dist = np.broadcast_to(np.stack(tiles)[None], (NSA_KV_GROUPS, len(tiles), rows, PAIR))
    tab = _bias_table(rel_bias, jnp.asarray(dist, i32), slab=qb)
    ns = NEAR_PAIRS + 1
    tabs = tab[:, :ns]
    tabc = tab[:, ns]
    tabw = jnp.concatenate([tab[:, ns + 1 + k] for k in range(n_win)], axis=-1)
    c31 = jnp.repeat(rel_bias.astype(f32)[REL_BUCKETS - 1].reshape(NSA_KV_GROUPS, hpg), qb, axis=1)
    return tabs, tabw, tabc, c31.reshape(NSA_KV_GROUPS, rows, 1)


def _nsa_prompt(q_b, misc, kv4, kvwin, rel_bias):
    bsz, t, _ = q_b.shape
    dh, qb = NSA_DH, NSA_QBLOCK
    nblk = t // NSA_BLOCK
    assert nblk == PAIR and t % NSA_KEY_TILE == 0
    kcvc = _block_means(kv4, tb=MEANS_TB)
    onehot = (jnp.arange(t, dtype=i32)[:, None] // NSA_BLOCK == jnp.arange(nblk, dtype=i32)[None, :]).astype(bf16)
    kaug = jnp.concatenate([kv4[:, :, 4 * dh:6 * dh].astype(bf16),
                            jnp.broadcast_to(onehot[None], (bsz, t, nblk))], axis=-1)
    ones = jnp.ones((bsz, t, dh), bf16)
    vgt = jnp.stack([jnp.swapaxes(jnp.concatenate([kv4[:, :, (6 + g) * dh:(7 + g) * dh].astype(bf16), ones],
                                                   axis=-1), 1, 2)
                     for g in range(NSA_KV_GROUPS)], axis=1)
    pad = ((0, 0), (NSA_WINDOW, 0), (0, 0))
    kw = jnp.pad(kvwin[:, :, :2 * dh].astype(bf16), pad)
    vw = jnp.pad(kvwin[:, :, 2 * dh:].astype(bf16), pad)
    tabs, tabw, tabc, c31 = _nsa_tables(rel_bias)
    tabs, tabw = jnp.swapaxes(tabs, 2, 3) * LOG2E, tabw * LOG2E
    rows = NSA_HPG * qb
    gw = NSA_HPG * dh
    span = NSA_WINDOW + qb
    return pl.pallas_call(
        _nsa_prompt_kernel,
        grid=(bsz, NSA_KV_GROUPS, t // qb),
        in_specs=[pl.BlockSpec((1, qb, gw), lambda b, g, i: (b, i, g)),
                  pl.BlockSpec((1, qb, MISC_W), lambda b, g, i: (b, i, 0)),
                  pl.BlockSpec((1, nblk, 4 * dh), lambda b, g, i: (b, 0, 0)),
                  pl.BlockSpec((1, t, 2 * dh + nblk), lambda b, g, i: (b, 0, 0)),
                  pl.BlockSpec((1, 1, 2 * dh, t), lambda b, g, i: (b, g, 0, 0)),
                  pl.BlockSpec((1, t + NSA_WINDOW, 2 * dh), lambda b, g, i: (b, 0, 0)),
                  pl.BlockSpec((1, t + NSA_WINDOW, 2 * dh), lambda b, g, i: (b, 0, 0)),
                  pl.BlockSpec((1, NEAR_PAIRS + 1, PAIR, rows), lambda b, g, i: (g, 0, 0, 0)),
                  pl.BlockSpec((1, rows, span), lambda b, g, i: (g, 0, 0)),
                  pl.BlockSpec((1, rows, PAIR), lambda b, g, i: (g, 0, 0)),
                  pl.BlockSpec((1, rows, 1), lambda b, g, i: (g, 0, 0))],
        out_specs=pl.BlockSpec((1, qb, gw), lambda b, g, i: (b, i, g)),
        out_shape=jax.ShapeDtypeStruct((bsz, t, NSA_HEADS * dh), bf16),
        scratch_shapes=[pltpu.VMEM((2 * dh + nblk, rows), bf16), pltpu.VMEM((NSA_KEY_TILE, rows), f32)],
        compiler_params=_cparams(("parallel", "parallel", "arbitrary")),
        name="nsa_prompt",
    )(q_b, misc, kcvc, kaug, vgt, kw, vw, tabs, tabw, tabc, c31)


MEANS_PAGES = 16


def _nsa_s_means_kernel(pt_ref, *refs):
    del pt_ref
    x_refs, o_ref = refs[:-1], refs[-1]
    p = pl.program_id(1)

    @pl.when(p == 0)
    def _():
        o_ref[...] = jnp.zeros_like(o_ref)

    page = x_refs[0].shape[-1]
    per_page = page // NSA_BLOCK
    nblk = o_ref.shape[-1]
    assert page == nblk
    planes = 2 * NSA_KV_GROUPS
    xs = jnp.concatenate([x_ref[0, j, g] for x_ref in x_refs for j in range(2) for g in range(NSA_KV_GROUPS)],
                         axis=0)
    tok = lax.broadcasted_iota(i32, (page, nblk), 0)
    col = lax.broadcasted_iota(i32, (page, nblk), 1)
    pool = (col == (tok >> int(math.log2(NSA_BLOCK)))).astype(bf16)
    hi, lo = _split2(xs)
    sums = jnp.dot(hi, pool, preferred_element_type=f32) + jnp.dot(lo, pool, preferred_element_type=f32)
    rows = planes * NSA_DH
    acc = None
    for k in range(len(x_refs)):
        shift = (p * len(x_refs) + k) * per_page
        part = pltpu.roll(sums[k * rows:(k + 1) * rows], shift, 1)
        acc = part if acc is None else acc + part
    acc = acc * (1.0 / NSA_BLOCK)
    for pl_i in range(planes):
        o_ref[0, pl_i] = o_ref[0, pl_i] + acc[pl_i * NSA_DH:(pl_i + 1) * NSA_DH]


def _nsa_s_means(cache_t, page_table):
    db, n_pages = page_table.shape
    page = cache_t.shape[-1]
    nblk = n_pages * page // NSA_BLOCK
    kp = MEANS_PAGES
    assert n_pages % kp == 0

    def page_spec(k):
        return pl.BlockSpec((1, 2, NSA_KV_GROUPS, NSA_DH, page),
                            lambda b, p, pt: (pt[b * n_pages + p * kp + k], 0, 0, 0, 0))

    return pl.pallas_call(
        _nsa_s_means_kernel,
        grid_spec=pltpu.PrefetchScalarGridSpec(
            num_scalar_prefetch=1,
            grid=(db, n_pages // kp),
            in_specs=[page_spec(k) for k in range(kp)],
            out_specs=pl.BlockSpec((1, 2 * NSA_KV_GROUPS, NSA_DH, nblk), lambda b, p, pt: (b, 0, 0, 0))),
        out_shape=jax.ShapeDtypeStruct((db, 2 * NSA_KV_GROUPS, NSA_DH, nblk), f32),
        compiler_params=_cparams(("parallel", "arbitrary")),
        name="nsa_sample_means",
    )(page_table.reshape(-1), *([cache_t] * kp))


def _nsa_s_scores_kernel(q_ref, kcvc_ref, win_ref, kvn_ref, tc_ref, tw_ref, b0_ref, oc_ref, ow_ref, sel_ref):
    dh, hpg, ng = NSA_DH, NSA_HPG, NSA_KV_GROUPS
    sb = q_ref.shape[0]
    nblk = kcvc_ref.shape[-1]
    row = lax.broadcasted_iota(i32, (NSA_HEADS, 1), 0)
    in_g = [(row >= g * hpg) & (row < (g + 1) * hpg) for g in range(ng)]
    pairs = [(b, g) for b in range(sb) for g in range(ng)]
    q = [q_ref[b] * (dh ** -0.5) for b in range(sb)]

    s_c = [_mm3(q[b], kcvc_ref[b, g]) + tc_ref[...] for b, g in pairs]
    p_c = []
    for s in s_c:
        p = jnp.exp(s - jnp.max(s, axis=-1, keepdims=True))
        p_c.append(p / jnp.maximum(jnp.sum(p, axis=-1, keepdims=True), 1e-30))
    o_c = [_mm_nt(p_c[k], kcvc_ref[b, ng + g]) for k, (b, g) in enumerate(pairs)]
    imp = jnp.concatenate([jnp.sum(jnp.where(in_g[g], p_c[k], 0.0), axis=0, keepdims=True)
                           for k, (b, g) in enumerate(pairs)], axis=0)
    for b in range(sb):
        oc_ref[b] = jnp.where(in_g[0], o_c[b * ng], o_c[b * ng + 1])

    s_w = [_mm3(q[b], win_ref[b, 0, g]) + tw_ref[...] for b, g in pairs]
    o_w = []
    for k, (b, g) in enumerate(pairs):
        kvn = kvn_ref[b]
        s_n = jnp.sum(q[b] * kvn[:, g * dh:(g + 1) * dh], axis=-1, keepdims=True) + b0_ref[...]
        m_w = jnp.maximum(jnp.max(s_w[k], axis=-1, keepdims=True), s_n)
        p_w = jnp.exp(s_w[k] - m_w)
        p_n = jnp.exp(s_n - m_w)
        l_w = jnp.sum(p_w, axis=-1, keepdims=True) + p_n
        v_n = kvn[:, (ng + g) * dh:(ng + g + 1) * dh]
        o_w.append((_mm_nt(p_w, win_ref[b, 1, g]) + p_n * v_n) / jnp.maximum(l_w, 1e-30))
    for b in range(sb):
        ow_ref[b] = jnp.where(in_g[0], o_w[b * ng], o_w[b * ng + 1])

    lane = lax.broadcasted_iota(i32, imp.shape, 1)
    n_sel = NSA_TOPN - 1
    forced_blocks = [0] + [nblk - k for k in range(1, NSA_LOCAL)]
    imp = jnp.where((lane == 0) | (lane > nblk - NSA_LOCAL), -jnp.inf, imp)
    picks = jnp.zeros(imp.shape, i32)
    for it, fb in enumerate(forced_blocks):
        picks = jnp.where(lane == it, fb, picks)
    for it in range(len(forced_blocks), n_sel):
        mx = jnp.max(imp, axis=-1, keepdims=True)
        first = jnp.min(jnp.where(imp == mx, lane, nblk), axis=-1, keepdims=True)
        picks = jnp.where(lane == it, first, picks)
        imp = jnp.where(lane == first, -jnp.inf, imp)
    for b in range(sb):
        sel_ref[b] = picks[b * ng:(b + 1) * ng]


def _nsa_s_sel_kernel(sel_ref, pt_ref, q_ref, rb_ref, kvn_ref, oc_ref, ow_ref, gt_ref, *refs, n_sel, past):
    del pt_ref
    k_refs, v_refs, o_ref = refs[:n_sel], refs[n_sel:2 * n_sel], refs[2 * n_sel]
    b = pl.program_id(0)
    g = pl.program_id(1)
    dh, hpg = NSA_DH, NSA_HPG
    is_g0 = g == 0
    page = k_refs[0].shape[-1]
    per_page = page // NSA_BLOCK
    q = q_ref[0] * (dh ** -0.5)
    kvn = kvn_ref[0]
    lane = lax.broadcasted_iota(i32, (NSA_HEADS, page), 1)
    s_parts, d_parts = [], []
    for j in range(n_sel):
        nb = sel_ref[(b * NSA_KV_GROUPS + g) * n_sel + j]
        in_blk = (lane >> int(math.log2(NSA_BLOCK))) == (nb % per_page)
        d_parts.append(jnp.where(in_blk, past - ((nb // per_page) * page + lane), -1))
        s_parts.append(_mm(q, k_refs[j][0, 0, 0]))
    d = jnp.concatenate(d_parts, axis=1)
    bias = jnp.broadcast_to(rb_ref[0], d.shape)
    for k in range(1, REL_BUCKETS):
        bias = jnp.where(d >= _BUCKET_THR[k - 1], rb_ref[k], bias)
    s = jnp.concatenate(s_parts, axis=1) + jnp.where(d < 0, NEG, bias)
    k_n = jnp.where(is_g0, kvn[:, 4 * dh:5 * dh], kvn[:, 5 * dh:6 * dh])
    v_n = jnp.where(is_g0, kvn[:, 6 * dh:7 * dh], kvn[:, 7 * dh:8 * dh])
    s_n = jnp.sum(q * k_n, axis=-1, keepdims=True) + rb_ref[0]
    m = jnp.maximum(jnp.max(s, axis=-1, keepdims=True), s_n)
    p = jnp.exp(s - m)
    p_n = jnp.exp(s_n - m)
    l = jnp.sum(p, axis=-1, keepdims=True) + p_n
    acc = p_n * v_n
    for j in range(n_sel):
        acc = acc + _mm_nt(p[:, j * page:(j + 1) * page], v_refs[j][0, 0, 0])
    o_s = acc / jnp.maximum(l, 1e-30)
    gt = gt_ref[0]
    o = gt[:, 0:1] * oc_ref[0] + gt[:, 1:2] * o_s + gt[:, 2:3] * ow_ref[0]
    row = lax.broadcasted_iota(i32, o.shape, 0)
    in_g = (row >= g * hpg) & (row < (g + 1) * hpg)

    @pl.when(is_g0)
    def _():
        o_ref[0] = o

    @pl.when(jnp.logical_not(is_g0))
    def _():
        o_ref[0] = jnp.where(in_g, o, o_ref[0])


def _nsa_sample(q_b, gates, kv4_new, kvwin_new, cache_kv, cache_win, page_table, rel_bias):
    db = q_b.shape[0]
    dh = NSA_DH
    n_pool, page = cache_kv.shape[:2]
    n_pages = page_table.shape[1]
    past = n_pages * page
    nblk = past // NSA_BLOCK
    wb = cache_win.shape[1]
    assert nblk == PAIR and page % NSA_BLOCK == 0 and wb == NSA_WINDOW and nblk > NSA_LOCAL
    cache_t = jnp.transpose(cache_kv, (0, 2, 3, 4, 1)).astype(f32)
    win_t = jnp.transpose(cache_win, (0, 2, 3, 4, 1)).astype(f32)
    kcvc = _nsa_s_means(cache_t, page_table)

    n = np.arange(nblk)
    d_cmp = past - (n * NSA_BLOCK + NSA_BLOCK - 1)
    jw = np.arange(wb)
    d_win = np.where(jw >= 1, wb - jw, -1)
    dist = np.broadcast_to(np.concatenate([d_cmp, d_win])[None, :], (NSA_HEADS, nblk + wb))
    tab = _bias_table(rel_bias, jnp.asarray(dist[None, None], i32), slab=1)[0, 0]
    t_cmp, t_win = tab[:, :nblk], tab[:, nblk:]
    rb = rel_bias.astype(f32)
    b0 = rb[0].reshape(NSA_HEADS, 1)

    q3 = q_b.reshape(db, NSA_HEADS, dh)
    sb = max(s for s in (8, 4, 2, 1) if db % s == 0)
    o_c, o_w, sel = pl.pallas_call(
        _nsa_s_scores_kernel,
        grid=(db // sb,),
        in_specs=[pl.BlockSpec((sb, NSA_HEADS, dh), lambda b: (b, 0, 0)),
                  pl.BlockSpec((sb, 2 * NSA_KV_GROUPS, dh, nblk), lambda b: (b, 0, 0, 0)),
                  pl.BlockSpec((sb, 2, NSA_KV_GROUPS, dh, wb), lambda b: (b, 0, 0, 0, 0)),
                  pl.BlockSpec((sb, 1, 4 * dh), lambda b: (b, 0, 0)),
                  pl.BlockSpec((NSA_HEADS, nblk), lambda b: (0, 0)),
                  pl.BlockSpec((NSA_HEADS, wb), lambda b: (0, 0)),
                  pl.BlockSpec((NSA_HEADS, 1), lambda b: (0, 0))],
        out_specs=[pl.BlockSpec((sb, NSA_HEADS, dh), lambda b: (b, 0, 0)),
                   pl.BlockSpec((sb, NSA_HEADS, dh), lambda b: (b, 0, 0)),
                   pl.BlockSpec((sb, NSA_KV_GROUPS, nblk), lambda b: (b, 0, 0))],
        out_shape=[jax.ShapeDtypeStruct((db, NSA_HEADS, dh), f32),
                   jax.ShapeDtypeStruct((db, NSA_HEADS, dh), f32),
                   jax.ShapeDtypeStruct((db, NSA_KV_GROUPS, nblk), i32)],
        compiler_params=_cparams(("parallel",)),
        name="nsa_sample_scores",
    )(q3, kcvc, win_t, kvwin_new.reshape(db, 1, 4 * dh), t_cmp, t_win, b0)

    n_sel = NSA_TOPN - 1
    sel_flat = sel[:, :, :n_sel].reshape(-1)
    per_page = page // NSA_BLOCK

    def page_spec(j, plane):
        def imap(b, g, s, p):
            nb = s[(b * NSA_KV_GROUPS + g) * n_sel + j]
            return (p[b * n_pages + nb // per_page], plane, g, 0, 0)
        return pl.BlockSpec((1, 1, 1, dh, page), imap)

    const = lambda shape: pl.BlockSpec(shape, lambda b, g, s, p: (0,) * len(shape))
    per_b = lambda shape: pl.BlockSpec((1,) + shape, lambda b, g, s, p: (b,) + (0,) * len(shape))
    o = pl.pallas_call(
        functools.partial(_nsa_s_sel_kernel, n_sel=n_sel, past=past),
        grid_spec=pltpu.PrefetchScalarGridSpec(
            num_scalar_prefetch=2,
            grid=(db, NSA_KV_GROUPS),
            in_specs=[per_b((NSA_HEADS, dh)), const((REL_BUCKETS, NSA_HEADS, 1)), per_b((1, 8 * dh)),
                      per_b((NSA_HEADS, dh)), per_b((NSA_HEADS, dh)), per_b((NSA_HEADS, 3))]
                     + [page_spec(j, 2) for j in range(n_sel)] + [page_spec(j, 3) for j in range(n_sel)],
            out_specs=per_b((NSA_HEADS, dh))),
        out_shape=jax.ShapeDtypeStruct((db, NSA_HEADS, dh), f32),
        compiler_params=_cparams(("parallel", "arbitrary")),
        name="nsa_sample_selected",
    )(sel_flat, page_table.reshape(-1), q3, rb.reshape(REL_BUCKETS, NSA_HEADS, 1), kv4_new.reshape(db, 1, 8 * dh),
      o_c, o_w, gates, *([cache_t] * (2 * n_sel)))
    return o.reshape(db, NSA_HEADS * dh)


def _gelu_tanh(x):
    return 0.5 * x * (1.0 + jnp.tanh(math.sqrt(2.0 / math.pi) * (x + 0.044715 * (x * x * x))))


def _lru_gates(xc, wa_ref, wx_ref, ba_ref, bx_ref, lam_ref, prec):
    r_parts, i_parts = [], []
    for n in range(RNN_BLOCKS):
        xb = xc[:, n * RNN_BW:(n + 1) * RNN_BW]
        r_parts.append(_dotp(xb, wa_ref[n], prec))
        i_parts.append(_dotp(xb, wx_ref[n], prec))
    r = jax.nn.sigmoid(jnp.concatenate(r_parts, axis=1) + ba_ref[...])
    i = jax.nn.sigmoid(jnp.concatenate(i_parts, axis=1) + bx_ref[...])
    log_a = -RG_C * r * _softplus(-lam_ref[...])
    a = jnp.exp(log_a)
    t = jnp.tanh(log_a)
    b = jnp.sqrt(jnp.maximum(-2.0 * t / (1.0 - t), 0.0)) * (i * xc)
    return a, b


def _lru_kernel(rec_ref, gate_ref, cw_ref, cb_ref, wa_ref, wx_ref, ba_ref, bx_ref, lam_ref, cinit_ref, h0_ref,
                y_ref, hfin_ref, xc_scr, a_scr, b_scr, hs_scr, h_scr, *, tb):
    j = pl.program_id(1)

    @pl.when(j == 0)
    def _():
        xc_scr[0:8, :] = cinit_ref[0]
        h_scr[...] = h0_ref[0]

    x = rec_ref[0]
    xc_scr[8:8 + tb, :] = x
    xc = xc_scr[5:5 + tb, :] * cw_ref[0:1, :]
    xc = xc + xc_scr[6:6 + tb, :] * cw_ref[1:2, :]
    xc = xc + xc_scr[7:7 + tb, :] * cw_ref[2:3, :]
    xc = xc + x * cw_ref[3:4, :]
    xc = xc + cb_ref[...]
    xc_scr[0:8, :] = xc_scr[tb:tb + 8, :]
    a, b = _lru_gates(xc, wa_ref, wx_ref, ba_ref, bx_ref, lam_ref, 1)
    a_scr[...] = a
    b_scr[...] = b

    def step(t, h):
        h = a_scr[pl.ds(t, 1), :] * h + b_scr[pl.ds(t, 1), :]
        hs_scr[pl.ds(t, 1), :] = h
        return h

    h = lax.fori_loop(0, tb, step, h_scr[...], unroll=8)
    h_scr[...] = h
    y_ref[0] = (_gelu_tanh(gate_ref[0]) * hs_scr[...]).astype(y_ref.dtype)

    @pl.when(j == pl.num_programs(1) - 1)
    def _():
        hfin_ref[0] = h


def _lru_prompt(rec, gate, conv_w, conv_b, wa, wx, ba, bx, lam, *, tb):
    bsz, t, w = rec.shape
    row = lambda a: a.reshape(1, w).astype(f32)
    cinit = jnp.zeros((bsz, 8, w), f32)
    h0 = jnp.zeros((bsz, 1, w), f32)
    full = lambda shape: pl.BlockSpec(shape, lambda b, j: (0,) * len(shape))
    return pl.pallas_call(
        functools.partial(_lru_kernel, tb=tb),
        grid=(bsz, t // tb),
        in_specs=[pl.BlockSpec((1, tb, w), lambda b, j: (b, j, 0)),
                  pl.BlockSpec((1, tb, w), lambda b, j: (b, j, 0)),
                  full((RNN_CONV, w)), full((1, w)),
                  full((RNN_BLOCKS, RNN_BW, RNN_BW)), full((RNN_BLOCKS, RNN_BW, RNN_BW)),
                  full((1, w)), full((1, w)), full((1, w)),
                  pl.BlockSpec((1, 8, w), lambda b, j: (b, 0, 0)),
                  pl.BlockSpec((1, 1, w), lambda b, j: (b, 0, 0))],
        out_specs=[pl.BlockSpec((1, tb, w), lambda b, j: (b, j, 0)),
                   pl.BlockSpec((1, 1, w), lambda b, j: (b, 0, 0))],
        out_shape=[jax.ShapeDtypeStruct((bsz, t, w), bf16), jax.ShapeDtypeStruct((bsz, 1, w), f32)],
        scratch_shapes=[pltpu.VMEM((tb + 8, w), f32), pltpu.VMEM((tb, w), f32), pltpu.VMEM((tb, w), f32),
                        pltpu.VMEM((tb, w), f32), pltpu.VMEM((1, w), f32)],
        compiler_params=_cparams(("parallel", "arbitrary")),
        name="rglru",
    )(rec, gate, conv_w.astype(f32), row(conv_b), wa.astype(f32), wx.astype(f32), row(ba), row(bx), row(lam),
      cinit, h0)


def _lru_step_kernel(rec_ref, gate_ref, b0_ref, b1_ref, b2_ref, cw_ref, cb_ref, wa_ref, wx_ref, ba_ref, bx_ref,
                     lam_ref, h0_ref, y_ref, h_ref):
    xc = b0_ref[...] * cw_ref[0:1, :]
    xc = xc + b1_ref[...] * cw_ref[1:2, :]
    xc = xc + b2_ref[...] * cw_ref[2:3, :]
    xc = xc + rec_ref[...] * cw_ref[3:4, :]
    xc = xc + cb_ref[...]
    a, b = _lru_gates(xc, wa_ref, wx_ref, ba_ref, bx_ref, lam_ref, 3)
    h = a * h0_ref[...] + b
    h_ref[...] = h
    y_ref[...] = _gelu_tanh(gate_ref[...]) * h


def _lru_sample(rec, gate, conv_buf, h0, conv_w, conv_b, wa, wx, ba, bx, lam):
    db, w = rec.shape
    row = lambda a: a.reshape(1, w).astype(f32)
    buf = conv_buf.astype(f32)
    return pl.pallas_call(
        _lru_step_kernel,
        out_shape=[jax.ShapeDtypeStruct((db, w), f32), jax.ShapeDtypeStruct((db, w), f32)],
        compiler_params=pltpu.CompilerParams(vmem_limit_bytes=VMEM_LIMIT),
        name="rglru_step",
    )(rec, gate, buf[:, 0], buf[:, 1], buf[:, 2], conv_w.astype(f32), row(conv_b), wa.astype(f32), wx.astype(f32),
      row(ba), row(bx), row(lam), h0.astype(f32))


def _top2_kernel(l_ref, e_ref, g_ref, cnt_ref, cnt_scr):
    @pl.when(pl.program_id(0) == 0)
    def _():
        cnt_scr[...] = jnp.zeros_like(cnt_scr)

    lg = l_ref[...]
    tm, w = lg.shape
    lane = lax.broadcasted_iota(i32, lg.shape, 1)
    lg = jnp.where(lane < N_EXPERTS, lg, -jnp.inf)
    m1 = jnp.max(lg, axis=-1, keepdims=True)
    i1 = jnp.min(jnp.where(lg == m1, lane, w), axis=-1, keepdims=True)
    lg2 = jnp.where(lane == i1, -jnp.inf, lg)
    m2 = jnp.max(lg2, axis=-1, keepdims=True)
    i2 = jnp.min(jnp.where(lg2 == m2, lane, w), axis=-1, keepdims=True)
    e2 = jnp.exp(m2 - m1)
    den = 1.0 + e2
    g_ref[...] = jnp.where(lane == 0, 1.0 / den, jnp.where(lane == 1, e2 / den, 0.0))
    hit1, hit2 = lane == i1, lane == i2
    routed = (hit1 | hit2).astype(bf16)
    r_i = lax.broadcasted_iota(i32, (tm, tm), 0)
    c_i = lax.broadcasted_iota(i32, (tm, tm), 1)
    before = jnp.dot((c_i < r_i).astype(bf16), routed, preferred_element_type=f32) + cnt_scr[...]
    rank1 = jnp.sum(jnp.where(hit1, before, 0.0), axis=-1, keepdims=True).astype(i32)
    rank2 = jnp.sum(jnp.where(hit2, before, 0.0), axis=-1, keepdims=True).astype(i32)
    e_ref[...] = jnp.where(lane == 0, i1, jnp.where(lane == 1, i2, jnp.where(lane == 2, rank1,
                                                                               jnp.where(lane == 3, rank2, 0))))
    cnt_scr[...] += jnp.sum(routed.astype(f32), axis=0, keepdims=True)
    cnt_ref[...] = cnt_scr[...].astype(i32)


def _top2(logits, *, tm):
    n, w = logits.shape
    return pl.pallas_call(
        _top2_kernel,
        grid=(n // tm,),
        in_specs=[pl.BlockSpec((tm, w), lambda i: (i, 0))],
        out_specs=[pl.BlockSpec((tm, w), lambda i: (i, 0)), pl.BlockSpec((tm, w), lambda i: (i, 0)),
                   pl.BlockSpec((1, w), lambda i: (0, 0))],
        out_shape=[jax.ShapeDtypeStruct((n, w), i32), jax.ShapeDtypeStruct((n, w), f32),
                   jax.ShapeDtypeStruct((1, w), i32)],
        scratch_shapes=[pltpu.VMEM((1, w), f32)],
        compiler_params=_cparams(("arbitrary",)),
        name="moe_top2",
    )(logits)


def _moe_kernel(be_ref, nu_ref, nv_ref, xs_ref, w1_ref, w3_ref, w2_ref, o_ref, acc_scr):
    del be_ref
    i = pl.program_id(0)
    f = pl.program_id(1)
    last = pl.num_programs(1) - 1
    used = i < nu_ref[0]
    parts = 4
    quarter = xs_ref.shape[0] // parts
    n_q = (nv_ref[i] + quarter - 1) // quarter

    @pl.when(used & (f == 0))
    def _():
        acc_scr[...] = jnp.zeros_like(acc_scr)

    def update(rows):
        x = xs_ref[rows, :]
        act = _silu(_mm(x, w1_ref[0])) * _mm(x, w3_ref[0])
        acc_scr[rows, :] += _mm(act, w2_ref[0])

    for q in range(1, parts + 1):
        @pl.when(used & (n_q == q))
        def _(q=q):
            update(slice(0, q * quarter))

    @pl.when(used & (f == last))
    def _():
        o_ref[...] = acc_scr[...]

    @pl.when(jnp.logical_not(used) & (f == last))
    def _():
        o_ref[...] = jnp.zeros_like(o_ref)


def _moe_experts(xs, blk_e, n_used, n_valid, w1, w3, w2, *, tm, tf):
    rows, d = xs.shape
    fdim = w1.shape[2]
    nf = fdim // tf

    def wcol(i, f, be, nu, nv):
        return (be[i], 0, jnp.where(i < nu[0], f, nf - 1))

    def wrow(i, f, be, nu, nv):
        return (be[i], jnp.where(i < nu[0], f, nf - 1), 0)

    return pl.pallas_call(
        _moe_kernel,
        grid_spec=pltpu.PrefetchScalarGridSpec(
            num_scalar_prefetch=3,
            grid=(rows // tm, nf),
            in_specs=[pl.BlockSpec((tm, d), lambda i, f, be, nu, nv: (i, 0)),
                      pl.BlockSpec((1, d, tf), wcol),
                      pl.BlockSpec((1, d, tf), wcol),
                      pl.BlockSpec((1, tf, d), wrow)],
            out_specs=pl.BlockSpec((tm, d), lambda i, f, be, nu, nv: (i, 0)),
            scratch_shapes=[pltpu.VMEM((tm, d), f32)]),
        out_shape=jax.ShapeDtypeStruct((rows, d), f32),
        compiler_params=_cparams(("arbitrary", "arbitrary")),
        name="moe_experts",
    )(blk_e, n_used, n_valid, xs, w1, w3, w2)


def _combine_kernel(x_ref, gate_ref, y0_ref, y1_ref, g_ref, nw_ref, o_ref):
    gw = g_ref[...]
    y = y0_ref[...] * gw[:, 0:1] + y1_ref[...] * gw[:, 1:2]
    x = x_ref[0] + gate_ref[0] * y
    o_ref[0] = x * lax.rsqrt(jnp.mean(x * x, axis=-1, keepdims=True) + EPS) * nw_ref[...]


def _moe_combine(x, gate, y0, y1, gw, norm_w, *, tm, row0):
    bsz, t, d = x.shape
    assert row0 % tm == 0
    tok = pl.BlockSpec((1, tm, d), lambda b, i: (b, i, 0))
    flat = lambda w: pl.BlockSpec((tm, w), lambda b, i: (row0 // tm + b * (t // tm) + i, 0))
    return pl.pallas_call(
        _combine_kernel,
        grid=(bsz, t // tm),
        in_specs=[tok, _mod_spec(gate, tm), flat(d), flat(d), flat(gw.shape[-1]),
                  pl.BlockSpec((1, d), lambda b, i: (0, 0))],
        out_specs=tok,
        out_shape=jax.ShapeDtypeStruct((bsz, t, d), f32),
        compiler_params=_cparams(("parallel", "parallel")),
        name="moe_combine",
    )(x, gate, y0, y1, gw, norm_w.reshape(1, d).astype(f32))


MOE_TM = 1024
MOE_TF = 512


def _moe_dispatch(e_idx, rank, counts, n_tok):
    tm = MOE_TM
    n_assign = n_tok * TOP_K
    padded = (counts + tm - 1) // tm * tm
    pend = jnp.cumsum(padded)
    pstart = pend - padded
    experts = jnp.arange(N_EXPERTS, dtype=i32)
    dest = jnp.sum(jnp.where(e_idx[:, :, None] == experts, pstart, 0), axis=-1) + rank
    n_blocks = (n_assign + N_EXPERTS * (tm - 1)) // tm
    rows = n_blocks * tm
    tok = jnp.broadcast_to(jnp.arange(n_tok, dtype=i32)[:, None], (n_tok, TOP_K))
    row_tok = (jnp.arange(rows, dtype=i32) % n_tok).at[dest.reshape(-1)].set(
        tok.reshape(-1), unique_indices=True, mode='promise_in_bounds')
    n_used = (pend[-1] // tm).astype(i32)
    blk = jnp.minimum(jnp.arange(n_blocks, dtype=i32), n_used - 1) * tm
    blk_e = jnp.minimum(jnp.sum((blk[:, None] >= pend[None, :]).astype(i32), axis=1), N_EXPERTS - 1)
    is_e = blk_e[:, None] == experts[None, :]
    first_row = jnp.sum(jnp.where(is_e, pstart, 0), axis=1)
    n_valid = jnp.clip(jnp.sum(jnp.where(is_e, counts, 0), axis=1) - (blk - first_row), 0, tm)
    return dest, row_tok, blk_e, n_used.reshape(1), n_valid.astype(i32)


def _w_in0_layout(w_in0):
    c_qkv = GDN_CONV_CH
    c_z = c_qkv + GDN_HEADS * GDN_DV
    c_ab = c_z + 2 * GDN_HEADS
    c_q = c_ab + NSA_HEADS * NSA_DH
    c_kv = c_q + 6 * NSA_KV_GROUPS * NSA_DH
    c_g = c_kv + 3 * NSA_HEADS
    assert c_g == w_in0.shape[1]
    n_misc = 2 * GDN_HEADS + 3 * NSA_HEADS
    w = jnp.concatenate([w_in0[:, :c_z], w_in0[:, c_ab:c_kv], w_in0[:, c_z:c_ab], w_in0[:, c_kv:c_g],
                         jnp.zeros((w_in0.shape[0], MISC_W - n_misc), w_in0.dtype)], axis=1)
    widths = (GDN_CONV_CH, GDN_HEADS * GDN_DV, NSA_HEADS * NSA_DH, 4 * NSA_KV_GROUPS * NSA_DH,
              2 * NSA_KV_GROUPS * NSA_DH, MISC_W)
    splits, s = [], 0
    for wd in widths:
        splits.append((s, s + wd))
        s += wd
    return w, tuple(splits)


def kernel(x_prompt, x_sample, c_prompt, c_sample, cache_nsa_kv, cache_nsa_win, state_gdn, state_gdn_conv, state_lru, state_lru_conv, page_table, rel_bias, w_ada, b_ada, norm_mix, norm_ffn, norm_final, w_in0, gdn_conv_w, gdn_a_log, gdn_dt_bias, gdn_norm_w, w_out0, ffn_w_gate, ffn_w_up, ffn_w_down, w_in1, lru_conv_w, lru_conv_b, lru_wa, lru_ba, lru_wx, lru_bx, lru_lambda, w_out1, moe_router, moe_w1, moe_w3, moe_w2):
    bsz, seq, d = x_prompt.shape
    db = x_sample.shape[0]
    assert x_sample.shape[1] == 1
    dh = NSA_DH

    n_c = bsz + db
    n_c_pad = -(-n_c // 8) * 8
    c_all = jnp.concatenate([c_prompt, c_sample, jnp.zeros((n_c_pad - n_c, d), f32)], axis=0)
    mods = _adaln(c_all, w_ada, b_ada).reshape(2, n_c_pad, N_MOD, d)
    mod_p = [[mods[l, :bsz, k].reshape(bsz, 1, d) for k in range(N_MOD)] for l in range(2)]
    mod_s = [[mods[l, bsz:n_c, k].reshape(1, db, d) for k in range(N_MOD)] for l in range(2)]

    w0, splits0 = _w_in0_layout(w_in0)
    splits1 = ((0, RNN_WIDTH), (RNN_WIDTH, 2 * RNN_WIDTH))
    router = jnp.concatenate([moe_router, jnp.zeros((d, MISC_W - N_EXPERTS), f32)], axis=1)
    bf = lambda w: w.astype(bf16)

    tm = PROMPT_TM
    xp = x_prompt
    qkv, z, q_b, kv4, kvwin, misc = _mod_matmul(xp, norm_mix[0], mod_p[0][0], mod_p[0][1], bf(w0), splits0,
                                                tm=tm, prec=1)
    o_a, p_gdn = _gdn(qkv, z, misc, gdn_conv_w, jnp.zeros((bsz, GDN_CONV - 1, GDN_CONV_CH), f32), gdn_a_log,
                      gdn_dt_bias, gdn_norm_w, jnp.zeros((bsz, GDN_HEADS, GDN_DK, GDN_DV), f32), tb=SCAN_TB, n_valid=seq)
    p_gdn_conv = qkv[:, seq - (GDN_CONV - 1):]
    o_b = _nsa_prompt(q_b, misc, kv4, kvwin, rel_bias)
    p_nsa_kv = kv4.reshape(bsz, seq, 4, NSA_KV_GROUPS, dh)
    keep = min(NSA_WINDOW, seq)
    p_nsa_win = kvwin[:, seq - keep:].reshape(bsz, keep, 2, NSA_KV_GROUPS, dh)
    xp = _proj_residual([o_a, o_b], bf(w_out0), xp, mod_p[0][2], tm=tm, prec=1)
    xp = _ffn(xp, norm_ffn[0], mod_p[0][3], mod_p[0][4], mod_p[0][5], bf(ffn_w_gate), bf(ffn_w_up), bf(ffn_w_down),
              tm=tm, tf=FFN_TF_PROMPT, prec=1)
    gate_br, rec_br = _mod_matmul(xp, norm_mix[1], mod_p[1][0], mod_p[1][1], bf(w_in1), splits1, tm=tm, prec=1)
    y_in, p_lru = _lru_prompt(rec_br, gate_br, lru_conv_w, lru_conv_b, lru_wa, lru_wx, lru_ba, lru_bx, lru_lambda,
                              tb=SCAN_TB)
    p_lru_conv = rec_br[:, seq - (RNN_CONV - 1):]
    xp = _proj_residual([y_in], bf(w_out1), xp, mod_p[1][2], tm=tm, prec=1)
    logit_p, h_p = _mod_matmul(xp, norm_ffn[1], mod_p[1][3], mod_p[1][4], router, ((0, MISC_W),), tm=tm, prec=3,
                               emit_h=f32)

    xs = x_sample.reshape(1, db, d)
    qkv_s, z_s, q_s, kv4_s, kvwin_s, misc_s = _mod_matmul(xs, norm_mix[0], mod_s[0][0], mod_s[0][1], w0, splits0,
                                                          tm=db, prec=3)
    per_seq = lambda a: a.reshape(db, 1, a.shape[-1])
    o_a_s, s_gdn = _gdn_step(per_seq(qkv_s), per_seq(z_s), per_seq(misc_s), gdn_conv_w, state_gdn_conv, gdn_a_log,
                             gdn_dt_bias, gdn_norm_w, state_gdn)
    o_a_s = o_a_s.reshape(1, db, GDN_HEADS * GDN_DV)
    s_gdn_conv = jnp.concatenate([state_gdn_conv[:, 1:], qkv_s.reshape(db, 1, GDN_CONV_CH)], axis=1)
    gates_s = jax.nn.sigmoid(misc_s[0, :, 2 * GDN_HEADS:2 * GDN_HEADS + 3 * NSA_HEADS]).reshape(db, NSA_HEADS, 3)
    o_b_s = _nsa_sample(q_s[0], gates_s, kv4_s[0], kvwin_s[0], cache_nsa_kv, cache_nsa_win, page_table, rel_bias)
    s_nsa_kv = kv4_s.reshape(db, 1, 4, NSA_KV_GROUPS, dh)
    s_nsa_win = jnp.concatenate([cache_nsa_win[:, 1:],
                                 kvwin_s.reshape(db, 1, 2, NSA_KV_GROUPS, dh).astype(cache_nsa_win.dtype)], axis=1)
    xs = _proj_residual([o_a_s, o_b_s.reshape(1, db, NSA_HEADS * dh)], w_out0, xs, mod_s[0][2], tm=db, prec=3)
    xs = _ffn(xs, norm_ffn[0], mod_s[0][3], mod_s[0][4], mod_s[0][5], ffn_w_gate, ffn_w_up, ffn_w_down,
              tm=db, tf=FFN_TF_SAMPLE, prec=3)
    gate_s, rec_s = _mod_matmul(xs, norm_mix[1], mod_s[1][0], mod_s[1][1], w_in1, splits1, tm=db, prec=3)
    y_in_s, s_lru = _lru_sample(rec_s[0], gate_s[0], state_lru_conv, state_lru, lru_conv_w, lru_conv_b, lru_wa, lru_wx,
                                lru_ba, lru_bx, lru_lambda)
    s_lru_conv = jnp.concatenate([state_lru_conv[:, 1:], rec_s.reshape(db, 1, RNN_WIDTH)], axis=1)
    xs = _proj_residual([y_in_s.reshape(1, db, RNN_WIDTH)], w_out1, xs, mod_s[1][2], tm=db, prec=3)
    logit_s, h_s = _mod_matmul(xs, norm_ffn[1], mod_s[1][3], mod_s[1][4], router, ((0, MISC_W),), tm=db, prec=3,
                               emit_h=f32)

    n_p = bsz * seq
    n_tok = n_p + db
    logits = jnp.concatenate([logit_p.reshape(n_p, MISC_W), logit_s.reshape(db, MISC_W)], axis=0)
    h_all = jnp.concatenate([h_p.reshape(n_p, d), h_s.reshape(db, d)], axis=0)
    route, gw, cnt = _top2(logits, tm=max(t for t in range(8, 1025, 8) if n_tok % t == 0))
    dest, row_tok, blk_e, n_used, n_valid = _moe_dispatch(route[:, :TOP_K], route[:, TOP_K:2 * TOP_K],
                                                          cnt[0, :N_EXPERTS], n_tok)
    xs_rows = jnp.zeros((row_tok.shape[0], d), f32)
    for slot in range(TOP_K):
        xs_rows = xs_rows.at[dest[:, slot]].set(h_all, unique_indices=True, mode='promise_in_bounds')
    yb = _moe_experts(xs_rows, blk_e, n_used, n_valid, moe_w1, moe_w3, moe_w2, tm=MOE_TM, tf=MOE_TF)
    y0, y1 = yb[dest[:, 0]], yb[dest[:, 1]]
    y_prompt = _moe_combine(xp, mod_p[1][5], y0, y1, gw, norm_final, tm=tm, row0=0)
    y_sample = _moe_combine(xs, mod_s[1][5], y0, y1, gw, norm_final, tm=db, row0=n_p)

    return (y_prompt, y_sample.reshape(db, 1, d),
            p_nsa_kv, p_nsa_win, p_gdn, p_gdn_conv, p_lru.reshape(bsz, RNN_WIDTH), p_lru_conv,
            s_nsa_kv, s_nsa_win, s_gdn, s_gdn_conv, s_lru, s_lru_conv)
```

```python
import functools
import math

import numpy as np
import jax
import jax.numpy as jnp
from jax import lax
from jax.experimental import pallas as pl
from jax.experimental.pallas import tpu as pltpu

f32 = jnp.float32
bf16 = jnp.bfloat16
i32 = jnp.int32

D_MODEL = 1024
EPS = 1e-6
N_MOD = 6
GDN_HEADS = 4
GDN_DK = 128
GDN_DV = 128
GDN_CONV = 4
GDN_CHUNK = 64
GDN_CONV_CH = GDN_HEADS * (2 * GDN_DK + GDN_DV)
NSA_HEADS = 8
NSA_KV_GROUPS = 2
NSA_HPG = NSA_HEADS // NSA_KV_GROUPS
NSA_DH = 64
NSA_BLOCK = 64
NSA_TOPN = 16
NSA_LOCAL = 2
NSA_WINDOW = 512
NSA_QBLOCK = 128
REL_BUCKETS = 32
REL_MAX_DIST = 2048
RNN_WIDTH = D_MODEL
RNN_BLOCKS = 8
RNN_BW = RNN_WIDTH // RNN_BLOCKS
RNN_CONV = 4
RG_C = 8.0
FFN_DIM = 2816
N_EXPERTS = 8
TOP_K = 2
EXPERT_DIM = 3584

NEG = -1e30
PAIR = 2 * NSA_BLOCK
NEAR_PAIRS = 13
MISC_W = 128
VMEM_LIMIT = 56 * 1024 * 1024
PROMPT_TM = 512
SCAN_TB = 256
FFN_TF_PROMPT = FFN_DIM // 2
FFN_TF_SAMPLE = 256
MEANS_TB = 512


def _cparams(sem, vmem=VMEM_LIMIT):
    return pltpu.CompilerParams(dimension_semantics=sem, vmem_limit_bytes=vmem)


def _mm(a, b):
    return jnp.dot(a.astype(bf16), b.astype(bf16), preferred_element_type=f32)


def _mm_nt(a, b):
    return lax.dot_general(a.astype(bf16), b.astype(bf16), (((1,), (1,)), ((), ())),
                           preferred_element_type=f32)


def _split2(a):
    hi = a.astype(bf16)
    lo = (a - hi.astype(f32)).astype(bf16)
    return hi, lo


def _mm3(a, b):
    ah, al = _split2(a)
    bh, bl = _split2(b)
    return (jnp.dot(ah, bh, preferred_element_type=f32) + jnp.dot(ah, bl, preferred_element_type=f32)
            + jnp.dot(al, bh, preferred_element_type=f32))


def _mm3_nt(a, b):
    ah, al = _split2(a)
    bh, bl = _split2(b)
    dn = (((1,), (1,)), ((), ()))
    return (lax.dot_general(ah, bh, dn, preferred_element_type=f32)
            + lax.dot_general(ah, bl, dn, preferred_element_type=f32)
            + lax.dot_general(al, bh, dn, preferred_element_type=f32))


def _mm_01(m01, a):
    hi = a.astype(bf16)
    r1 = a - hi.astype(f32)
    mid = r1.astype(bf16)
    lo = (r1 - mid.astype(f32)).astype(bf16)
    return (jnp.dot(m01, hi, preferred_element_type=f32) + jnp.dot(m01, mid, preferred_element_type=f32)
            + jnp.dot(m01, lo, preferred_element_type=f32))


def _dotp(a, b, prec):
    return _mm3(a, b) if prec == 3 else _mm(a, b)


def _silu(x):
    return x * jax.nn.sigmoid(x)


def _softplus(x):
    return jnp.maximum(x, 0.0) + jnp.log1p(jnp.exp(-jnp.abs(x)))


def _modulate(x, gain, shift, scale):
    r = lax.rsqrt(jnp.mean(x * x, axis=-1, keepdims=True) + EPS)
    return x * r * gain * (1.0 + scale) + shift


def _ada_kernel(c_ref, w_ref, b_ref, o_ref):
    o_ref[0] = _mm3(_silu(c_ref[...]), w_ref[0]) + b_ref[0]


def _adaln(c_all, w_ada, b_ada):
    rows = c_all.shape[0]
    depth, d, n = w_ada.shape
    tn = 1536
    return pl.pallas_call(
        _ada_kernel,
        grid=(depth, n // tn),
        in_specs=[pl.BlockSpec((rows, d), lambda l, j: (0, 0)),
                  pl.BlockSpec((1, d, tn), lambda l, j: (l, 0, j)),
                  pl.BlockSpec((1, 1, tn), lambda l, j: (l, 0, j))],
        out_specs=pl.BlockSpec((1, rows, tn), lambda l, j: (l, 0, j)),
        out_shape=jax.ShapeDtypeStruct((depth, rows, n), f32),
        compiler_params=_cparams(("arbitrary", "arbitrary")),
        name="adaln",
    )(c_all, w_ada, b_ada.reshape(depth, 1, n))


def _mod_spec(mod, tm):
    r = mod.shape[1]
    if r == 1:
        return pl.BlockSpec((1, 1, mod.shape[2]), lambda b, i: (b, 0, 0))
    return pl.BlockSpec((1, tm, mod.shape[2]), lambda b, i: (b, i, 0))


def _modmm_kernel(x_ref, gain_ref, shift_ref, scale_ref, w_ref, *o_refs, splits, prec, emit_h):
    h = _modulate(x_ref[0], gain_ref[...], shift_ref[0], scale_ref[0])
    if emit_h:
        o_refs[-1][0] = h.astype(o_refs[-1].dtype)
    hh = _split2(h) if prec == 3 else h.astype(bf16)
    for o_ref, (a, b) in zip(o_refs, splits):
        w = w_ref[:, a:b]
        if prec == 3:
            wh, wl = _split2(w)
            acc = (jnp.dot(hh[0], wh, preferred_element_type=f32) + jnp.dot(hh[0], wl, preferred_element_type=f32)
                   + jnp.dot(hh[1], wh, preferred_element_type=f32))
        else:
            acc = jnp.dot(hh, w, preferred_element_type=f32)
        o_ref[0] = acc


def _mod_matmul(x, gain, shift, scale, w, splits, *, tm, prec, emit_h=None):
    bsz, t, d = x.shape
    out_shape = [jax.ShapeDtypeStruct((bsz, t, b - a), f32) for a, b in splits]
    out_specs = [pl.BlockSpec((1, tm, b - a), lambda bi, i: (bi, i, 0)) for a, b in splits]
    if emit_h is not None:
        out_shape.append(jax.ShapeDtypeStruct((bsz, t, d), emit_h))
        out_specs.append(pl.BlockSpec((1, tm, d), lambda bi, i: (bi, i, 0)))
    return pl.pallas_call(
        functools.partial(_modmm_kernel, splits=tuple(splits), prec=prec, emit_h=emit_h is not None),
        grid=(bsz, t // tm),
        in_specs=[pl.BlockSpec((1, tm, d), lambda bi, i: (bi, i, 0)),
                  pl.BlockSpec((1, d), lambda bi, i: (0, 0)),
                  _mod_spec(shift, tm), _mod_spec(scale, tm),
                  pl.BlockSpec(w.shape, lambda bi, i: (0, 0))],
        out_specs=out_specs,
        out_shape=out_shape,
        compiler_params=_cparams(("parallel", "parallel")),
        name="mod_matmul",
    )(x, gain.reshape(1, d), shift, scale, w)


def _projres_kernel(*refs, n_lhs, ksplits, prec, final_norm):
    lhs = refs[:n_lhs]
    w_ref, x_ref, gate_ref = refs[n_lhs:n_lhs + 3]
    o_ref = refs[-1]
    acc = None
    for l_ref, (a, b) in zip(lhs, ksplits):
        part = _dotp(l_ref[0], w_ref[a:b, :], prec)
        acc = part if acc is None else acc + part
    y = x_ref[0] + gate_ref[0] * acc
    if final_norm:
        nw_ref = refs[n_lhs + 3]
        y = y * lax.rsqrt(jnp.mean(y * y, axis=-1, keepdims=True) + EPS) * nw_ref[...]
    o_ref[0] = y


def _proj_residual(lhs_list, w, x, gate, *, tm, prec, norm_w=None):
    bsz, t, d = x.shape
    ksplits, k0 = [], 0
    for l in lhs_list:
        ksplits.append((k0, k0 + l.shape[-1]))
        k0 += l.shape[-1]
    in_specs = [pl.BlockSpec((1, tm, l.shape[-1]), lambda bi, i: (bi, i, 0)) for l in lhs_list]
    in_specs += [pl.BlockSpec(w.shape, lambda bi, i: (0, 0)),
                 pl.BlockSpec((1, tm, d), lambda bi, i: (bi, i, 0)),
                 _mod_spec(gate, tm)]
    args = list(lhs_list) + [w, x, gate]
    if norm_w is not None:
        in_specs.append(pl.BlockSpec((1, d), lambda bi, i: (0, 0)))
        args.append(norm_w.reshape(1, d))
    return pl.pallas_call(
        functools.partial(_projres_kernel, n_lhs=len(lhs_list), ksplits=tuple(ksplits), prec=prec,
                          final_norm=norm_w is not None),
        grid=(bsz, t // tm),
        in_specs=in_specs,
        out_specs=pl.BlockSpec((1, tm, d), lambda bi, i: (bi, i, 0)),
        out_shape=jax.ShapeDtypeStruct((bsz, t, d), f32),
        compiler_params=_cparams(("parallel", "parallel")),
        name="proj_residual",
    )(*args)


def _ffn_kernel(x_ref, gain_ref, shift_ref, scale_ref, gate_ref, wg_ref, wu_ref, wd_ref, o_ref,
                h_scr, acc_scr, *, prec):
    f = pl.program_id(2)

    @pl.when(f == 0)
    def _():
        h_scr[...] = _modulate(x_ref[0], gain_ref[...], shift_ref[0], scale_ref[0]).astype(h_scr.dtype)
        acc_scr[...] = jnp.zeros_like(acc_scr)

    h = h_scr[...]
    act = _silu(_dotp(h, wg_ref[...], prec)) * _dotp(h, wu_ref[...], prec)
    acc_scr[...] += _dotp(act, wd_ref[...], prec)

    @pl.when(f == pl.num_programs(2) - 1)
    def _():
        o_ref[0] = x_ref[0] + gate_ref[0] * acc_scr[...]


def _ffn(x, gain, shift, scale, gate, wg, wu, wd, *, tm, tf, prec):
    bsz, t, d = x.shape
    fdim = wg.shape[1]
    mod_specs = []
    for mod in (shift, scale, gate):
        if mod.shape[1] == 1:
            mod_specs.append(pl.BlockSpec((1, 1, d), lambda b, i, f: (b, 0, 0)))
        else:
            mod_specs.append(pl.BlockSpec((1, tm, d), lambda b, i, f: (b, i, 0)))
    return pl.pallas_call(
        functools.partial(_ffn_kernel, prec=prec),
        grid=(bsz, t // tm, fdim // tf),
        in_specs=[pl.BlockSpec((1, tm, d), lambda b, i, f: (b, i, 0)),
                  pl.BlockSpec((1, d), lambda b, i, f: (0, 0)),
                  mod_specs[0], mod_specs[1], mod_specs[2],
                  pl.BlockSpec((d, tf), lambda b, i, f: (0, f)),
                  pl.BlockSpec((d, tf), lambda b, i, f: (0, f)),
                  pl.BlockSpec((tf, d), lambda b, i, f: (f, 0))],
        out_specs=pl.BlockSpec((1, tm, d), lambda b, i, f: (b, i, 0)),
        out_shape=jax.ShapeDtypeStruct((bsz, t, d), f32),
        scratch_shapes=[pltpu.VMEM((tm, d), f32 if prec == 3 else bf16), pltpu.VMEM((tm, d), f32)],
        compiler_params=_cparams(("parallel", "parallel", "arbitrary")),
        name="ffn",
    )(x, gain.reshape(1, d), shift, scale, gate, wg, wu, wd)


def _gdn_kernel(qkv_ref, z_ref, misc_ref, cw_ref, cinit_ref, hp_ref, nw_ref, s0_ref,
                o_ref, sfin_ref, xc_scr, s_scr, *, tb, n_valid):
    j = pl.program_id(1)
    c = GDN_CHUNK
    nh = GDN_HEADS

    @pl.when(j == 0)
    def _():
        xc_scr[0:8, :] = cinit_ref[0]
        s_scr[...] = s0_ref[0]

    x = qkv_ref[0]
    xc_scr[8:8 + tb, :] = x
    y = xc_scr[5:5 + tb, :] * cw_ref[0:1, :]
    y = y + xc_scr[6:6 + tb, :] * cw_ref[1:2, :]
    y = y + xc_scr[7:7 + tb, :] * cw_ref[2:3, :]
    y = y + x * cw_ref[3:4, :]
    xc_scr[0:8, :] = xc_scr[tb:tb + 8, :]
    y = _silu(y)

    misc = misc_ref[0]
    row = lax.broadcasted_iota(i32, (tb, MISC_W), 0) + j * tb
    live = row < n_valid
    log_a = jnp.where(live, hp_ref[0:1, :] * _softplus(misc + hp_ref[1:2, :]), 0.0)
    beta = jnp.where(live, jax.nn.sigmoid(misc), 0.0)

    r_i = lax.broadcasted_iota(i32, (tb, tb), 0)
    c_i = lax.broadcasted_iota(i32, (tb, tb), 1)
    sh = int(math.log2(c))
    ltri = (((r_i >> sh) == (c_i >> sh)) & (c_i <= r_i)).astype(bf16)
    g = _mm_01(ltri, log_a)
    g_t = g.T

    ri = lax.broadcasted_iota(i32, (c, c), 0)
    ci = lax.broadcasted_iota(i32, (c, c), 1)
    incl = ci <= ri
    strict = ci < ri
    eye = (ci == ri).astype(f32)
    quad = []
    for lvl in range(int(math.log2(c))):
        quad.append(((ri >> (lvl + 1)) == (ci >> (lvl + 1))) & (((ri >> lvl) & 1) == 1) & (((ci >> lvl) & 1) == 0))

    nchunk = tb // c
    units = [(n, h) for n in range(nchunk) for h in range(nh)]

    def stack(fn):
        return jnp.stack([fn(n * c, h) for n, h in units], axis=0)

    def bmm(eq, a_, b_):
        return jnp.einsum(eq, a_.astype(bf16), b_.astype(bf16), preferred_element_type=f32)

    q = stack(lambda r0, h: y[r0:r0 + c, h * GDN_DK:(h + 1) * GDN_DK])
    k = stack(lambda r0, h: y[r0:r0 + c, (nh + h) * GDN_DK:(nh + h + 1) * GDN_DK])
    v = stack(lambda r0, h: y[r0:r0 + c, 2 * nh * GDN_DK + h * GDN_DV:2 * nh * GDN_DK + (h + 1) * GDN_DV])
    q = q * lax.rsqrt(jnp.sum(q * q, axis=-1, keepdims=True) + EPS) * (GDN_DK ** -0.5)
    k = k * lax.rsqrt(jnp.sum(k * k, axis=-1, keepdims=True) + EPS)
    g_col = stack(lambda r0, h: g[r0:r0 + c, h:h + 1])
    g_row = stack(lambda r0, h: g_t[h:h + 1, r0:r0 + c])
    b_col = stack(lambda r0, h: beta[r0:r0 + c, nh + h:nh + h + 1])
    gam = jnp.where(incl, jnp.exp(jnp.where(incl, g_col - g_row, 0.0)), 0.0)
    kk = bmm('uid,ujd->uij', k, k)
    a = jnp.where(strict, b_col * gam * kk, 0.0)
    p = eye - jnp.where(quad[0], a, 0.0)
    for lvl in range(1, len(quad)):
        m = bmm('uij,ujk->uik', jnp.where(quad[lvl], a, 0.0), p)
        p = p - bmm('uij,ujk->uik', p, m)
    e_g = jnp.exp(g_col)
    sol = bmm('uij,ujd->uid', p, jnp.concatenate([b_col * v, (b_col * e_g) * k], axis=-1))
    vb, w = sol[:, :, :GDN_DV], sol[:, :, GDN_DV:]
    aqk = bmm('uid,ujd->uij', q, k) * gam
    qg = q * e_g
    g_last = g_col[:, c - 1:c, :]
    kd_t = jnp.swapaxes(k * jnp.exp(g_last - g_col), 1, 2)
    gc = jnp.exp(g_last)

    s = s_scr[...]
    for n in range(nchunk):
        sl = slice(n * nh, (n + 1) * nh)
        u = vb[sl] - bmm('hcd,hde->hce', w[sl], s)
        o = bmm('hcd,hde->hce', qg[sl], s) + bmm('hij,hje->hie', aqk[sl], u)
        s = gc[sl] * s + bmm('hdc,hce->hde', kd_t[sl], u)
        o = o * lax.rsqrt(jnp.mean(o * o, axis=-1, keepdims=True) + EPS) * nw_ref[...]
        for h in range(nh):
            zs = z_ref[0, n * c:(n + 1) * c, h * GDN_DV:(h + 1) * GDN_DV]
            o_ref[0, n * c:(n + 1) * c, h * GDN_DV:(h + 1) * GDN_DV] = (o[h] * _silu(zs)).astype(o_ref.dtype)
    s_scr[...] = s

    @pl.when(j == pl.num_programs(1) - 1)
    def _():
        sfin_ref[0] = s_scr[...]


def _gdn(qkv_raw, z, misc, conv_w, conv_buf, a_log, dt_bias, norm_w, s0, *, tb, n_valid):
    bsz, tp, ch = qkv_raw.shape
    cinit = jnp.concatenate([jnp.zeros((bsz, 5, ch), f32), conv_buf.astype(f32)], axis=1)
    hp = jnp.zeros((8, MISC_W), f32)
    hp = hp.at[0, :GDN_HEADS].set(-jnp.exp(a_log.astype(f32))).at[1, :GDN_HEADS].set(dt_bias.astype(f32))
    zw = GDN_HEADS * GDN_DV
    return pl.pallas_call(
        functools.partial(_gdn_kernel, tb=tb, n_valid=n_valid),
        grid=(bsz, tp // tb),
        in_specs=[pl.BlockSpec((1, tb, ch), lambda b, j: (b, j, 0)),
                  pl.BlockSpec((1, tb, zw), lambda b, j: (b, j, 0)),
                  pl.BlockSpec((1, tb, MISC_W), lambda b, j: (b, j, 0)),
                  pl.BlockSpec((GDN_CONV, ch), lambda b, j: (0, 0)),
                  pl.BlockSpec((1, 8, ch), lambda b, j: (b, 0, 0)),
                  pl.BlockSpec((8, MISC_W), lambda b, j: (0, 0)),
                  pl.BlockSpec((1, GDN_DV), lambda b, j: (0, 0)),
                  pl.BlockSpec((1, GDN_HEADS, GDN_DK, GDN_DV), lambda b, j: (b, 0, 0, 0))],
        out_specs=[pl.BlockSpec((1, tb, zw), lambda b, j: (b, j, 0)),
                   pl.BlockSpec((1, GDN_HEADS, GDN_DK, GDN_DV), lambda b, j: (b, 0, 0, 0))],
        out_shape=[jax.ShapeDtypeStruct((bsz, tp, zw), bf16),
                   jax.ShapeDtypeStruct((bsz, GDN_HEADS, GDN_DK, GDN_DV), f32)],
        scratch_shapes=[pltpu.VMEM((tb + 8, ch), f32), pltpu.VMEM((GDN_HEADS, GDN_DK, GDN_DV), f32)],
        compiler_params=_cparams(("parallel", "arbitrary")),
        name="gdn",
    )(qkv_raw, z, misc, conv_w.astype(f32), cinit, hp, norm_w.reshape(1, GDN_DV).astype(f32), s0.astype(f32))


def _gdn_step_kernel(x_ref, buf_ref, cw_ref, z_ref, misc_ref, hp_ref, nw_ref, s0_ref, o_ref, s_ref):
    nh = GDN_HEADS
    buf = buf_ref[0]
    y = buf[0:1] * cw_ref[0:1, :]
    y = y + buf[1:2] * cw_ref[1:2, :]
    y = y + buf[2:3] * cw_ref[2:3, :]
    y = _silu(y + x_ref[0] * cw_ref[3:4, :])
    misc = misc_ref[0]
    a_all = jnp.exp(hp_ref[0:1, :] * _softplus(misc + hp_ref[1:2, :]))
    b_all = jax.nn.sigmoid(misc)
    eye = lax.broadcasted_iota(i32, (GDN_DK, GDN_DK), 0) == lax.broadcasted_iota(i32, (GDN_DK, GDN_DK), 1)
    pad = jnp.zeros((6, GDN_DK), f32)
    outs = []
    for h in range(nh):
        q = y[:, h * GDN_DK:(h + 1) * GDN_DK]
        k = y[:, (nh + h) * GDN_DK:(nh + h + 1) * GDN_DK]
        v = y[:, 2 * nh * GDN_DK + h * GDN_DV:2 * nh * GDN_DK + (h + 1) * GDN_DV]
        q = q * lax.rsqrt(jnp.sum(q * q, axis=-1, keepdims=True) + EPS) * (GDN_DK ** -0.5)
        k = k * lax.rsqrt(jnp.sum(k * k, axis=-1, keepdims=True) + EPS)
        a = a_all[:, h:h + 1]
        b = b_all[:, nh + h:nh + h + 1]
        s = s0_ref[0, h]
        ks_qs = _mm3(jnp.concatenate([k, q, pad], axis=0), s)
        u = b * v - (b * a) * ks_qs[0:1]
        o = a * ks_qs[1:2] + jnp.sum(q * k, axis=-1, keepdims=True) * u
        k_col = jnp.sum(jnp.where(eye, k, 0.0), axis=1, keepdims=True)
        s_ref[0, h] = a * s + k_col * u
        o = o * lax.rsqrt(jnp.mean(o * o, axis=-1, keepdims=True) + EPS) * nw_ref[...]
        outs.append(o * _silu(z_ref[0][:, h * GDN_DV:(h + 1) * GDN_DV]))
    o_ref[0] = jnp.concatenate(outs, axis=1)


def _gdn_step(qkv_raw, z, misc, conv_w, conv_buf, a_log, dt_bias, norm_w, s0):
    db, _, ch = qkv_raw.shape
    hp = jnp.zeros((8, MISC_W), f32)
    hp = hp.at[0, :GDN_HEADS].set(-jnp.exp(a_log.astype(f32))).at[1, :GDN_HEADS].set(dt_bias.astype(f32))
    zw = GDN_HEADS * GDN_DV
    per_b = lambda shape: pl.BlockSpec((1,) + shape, lambda b: (b,) + (0,) * len(shape))
    const = lambda shape: pl.BlockSpec(shape, lambda b: (0,) * len(shape))
    return pl.pallas_call(
        _gdn_step_kernel,
        grid=(db,),
        in_specs=[per_b((1, ch)), per_b((GDN_CONV - 1, ch)), const((GDN_CONV, ch)), per_b((1, zw)),
                  per_b((1, MISC_W)), const((8, MISC_W)), const((1, GDN_DV)), per_b((GDN_HEADS, GDN_DK, GDN_DV))],
        out_specs=[per_b((1, zw)), per_b((GDN_HEADS, GDN_DK, GDN_DV))],
        out_shape=[jax.ShapeDtypeStruct((db, 1, zw), f32),
                   jax.ShapeDtypeStruct((db, GDN_HEADS, GDN_DK, GDN_DV), f32)],
        compiler_params=_cparams(("parallel",)),
        name="gdn_step",
    )(qkv_raw, conv_buf.astype(f32), conv_w.astype(f32), z, misc, hp, norm_w.reshape(1, GDN_DV).astype(f32),
      s0.astype(f32))


def _bucket_thresholds():
    exact = REL_BUCKETS // 2
    d = np.arange(0, 4 * REL_MAX_DIST, dtype=np.int64)

    def buckets(ft):
        nf = np.maximum(d, exact).astype(ft)
        large = exact + (np.log(nf / ft(exact)) / ft(math.log(REL_MAX_DIST / exact)) * ft(REL_BUCKETS - exact)).astype(np.int32)
        return np.where(d < exact, d, np.minimum(large, REL_BUCKETS - 1))

    b64, b32 = buckets(np.float64), buckets(np.float32)
    assert np.array_equal(b64, b32) and b64[-1] == REL_BUCKETS - 1 and np.all(np.diff(b64) >= 0)
    return [int(np.argmax(b64 >= k)) for k in range(1, REL_BUCKETS)]


_BUCKET_THR = _bucket_thresholds()
FAR_DIST = _BUCKET_THR[-1]


def _bias_kernel(rb_ref, d_ref, o_ref, *, slab, group_heads):
    rows = d_ref.shape[-2]
    for s in range(rows // slab):
        d = d_ref[0, 0, s * slab:(s + 1) * slab, :]
        h = pl.program_id(0) * group_heads + s
        val = jnp.full(d.shape, rb_ref[0, h], f32)
        for k in range(1, REL_BUCKETS):
            val = jnp.where(d >= _BUCKET_THR[k - 1], rb_ref[k, h], val)
        o_ref[0, 0, s * slab:(s + 1) * slab, :] = jnp.where(d < 0, NEG, val)


def _bias_table(rel_bias, dist, *, slab):
    g, nt, rows, cols = dist.shape
    return pl.pallas_call(
        functools.partial(_bias_kernel, slab=slab, group_heads=rows // slab),
        grid=(g, nt),
        in_specs=[pl.BlockSpec(memory_space=pltpu.SMEM),
                  pl.BlockSpec((1, 1, rows, cols), lambda a, b: (a, b, 0, 0))],
        out_specs=pl.BlockSpec((1, 1, rows, cols), lambda a, b: (a, b, 0, 0)),
        out_shape=jax.ShapeDtypeStruct(dist.shape, f32),
        compiler_params=_cparams(("arbitrary", "arbitrary")),
        name="bias_table",
    )(rel_bias.astype(f32), dist)


def _blockmean_kernel(x_ref, o_ref):
    x = x_ref[0]
    nb = x.shape[0] // NSA_BLOCK
    o_ref[0] = jnp.sum(x.reshape(nb, NSA_BLOCK, x.shape[1]), axis=1) * (1.0 / NSA_BLOCK)


def _block_means(kv4, *, tb):
    bsz, t, _ = kv4.shape
    w = 2 * NSA_KV_GROUPS * NSA_DH
    return pl.pallas_call(
        _blockmean_kernel,
        grid=(bsz, t // tb),
        in_specs=[pl.BlockSpec((1, tb, w), lambda b, i: (b, i, 0))],
        out_specs=pl.BlockSpec((1, tb // NSA_BLOCK, w), lambda b, i: (b, i, 0)),
        out_shape=jax.ShapeDtypeStruct((bsz, t // NSA_BLOCK, w), f32),
        compiler_params=_cparams(("parallel", "parallel")),
        name="nsa_block_means",
    )(kv4)


LOG2E = 1.4426950408889634
NSA_KEY_TILE = 4 * PAIR


def _nsa_prompt_kernel(q_ref, misc_ref, kcvc_ref, kaug_ref, vgt_ref, kw_ref, vw_ref,
                       tabs_ref, tabw_ref, tabc_ref, c31_ref, o_ref, qaug_scr, s_scr):
    g = pl.program_id(1)
    i = pl.program_id(2)
    qb, dh, hpg = NSA_QBLOCK, NSA_DH, NSA_HPG
    is_g0 = g == 0
    nn = (((1,), (1,)), ((), ()))

    def ghalf(a):
        return jnp.where(is_g0, a[:, :dh], a[:, dh:])

    qt = q_ref[0] * (dh ** -0.5)
    qh = [qt[:, h * dh:(h + 1) * dh] for h in range(hpg)]
    kcvc = kcvc_ref[0]
    nblk = kcvc.shape[0]
    kc = ghalf(kcvc[:, :2 * dh])
    vc = ghalf(kcvc[:, 2 * dh:])

    rows = hpg * qb
    qs = jnp.concatenate(qh, axis=0)
    zero = jnp.zeros_like(qs)
    q2f = jnp.concatenate([jnp.where(is_g0, qs, zero), jnp.where(is_g0, zero, qs)], axis=1) * LOG2E
    q2 = q2f.astype(bf16)

    blk4 = lax.broadcasted_iota(i32, (rows, nblk), 1)
    qrow4 = lax.broadcasted_iota(i32, (rows, nblk), 0) & (qb - 1)
    first_tab = 2 * i - 2 * NEAR_PAIRS + 1
    shift = (first_tab + 4 * nblk) % nblk
    readable = blk4 * NSA_BLOCK + (NSA_BLOCK - 1) <= i * qb + qrow4
    bias = jnp.where(blk4 < first_tab, c31_ref[0], pltpu.roll(tabc_ref[0], shift, 1))
    s_c = jnp.where(readable, _mm3_nt(qs, kc) + bias, NEG)
    m_c = jnp.max(s_c, axis=-1, keepdims=True)
    p_c = jnp.where(readable, jnp.exp(s_c - m_c), 0.0)
    p_c = p_c / jnp.maximum(jnp.sum(p_c, axis=-1, keepdims=True), 1e-30)
    o_cmp = _mm(p_c, vc)
    imp = p_c[0:qb]
    for h in range(1, hpg):
        imp = imp + p_c[h * qb:(h + 1) * qb]

    span = NSA_WINDOW + qb
    ws = pl.multiple_of(i * qb, qb)
    s_w = lax.dot_general(q2, kw_ref[0, pl.ds(ws, span), :], nn, preferred_element_type=f32) + tabw_ref[0]
    in_seq = lax.broadcasted_iota(i32, (rows, span), 1) + (i * qb - NSA_WINDOW) >= 0
    s_w = jnp.where(in_seq, s_w, NEG)
    m_w = jnp.max(s_w, axis=-1, keepdims=True)
    p_w = jnp.exp2(s_w - m_w)
    l_w = jnp.sum(p_w, axis=-1, keepdims=True)
    o_win = ghalf(jnp.dot(p_w.astype(bf16), vw_ref[0, pl.ds(ws, span), :], preferred_element_type=f32))
    o_win = o_win / jnp.maximum(l_w, 1e-30)

    blk = lax.broadcasted_iota(i32, (nblk, qb), 0)
    cur = (i * qb + lax.broadcasted_iota(i32, (nblk, qb), 1)) >> int(math.log2(NSA_BLOCK))
    valid = blk <= cur
    forced = valid & ((blk == 0) | (blk > cur - NSA_LOCAL))
    imp_t = jnp.where(forced, -jnp.inf, jnp.where(valid, imp.T, -1.0))
    picked = forced
    for _ in range(min(NSA_TOPN, nblk) - 1 - NSA_LOCAL):
        mx = jnp.max(imp_t, axis=0, keepdims=True)
        first = jnp.min(jnp.where(imp_t == mx, blk, nblk), axis=0, keepdims=True)
        hit = blk == first
        picked = picked | hit
        imp_t = jnp.where(hit, -jnp.inf, imp_t)
    msel_t = jnp.where(picked & valid, 0.0, NEG).astype(bf16)

    qaug_scr[...] = jnp.concatenate([q2f.T.astype(bf16), jnp.concatenate([msel_t] * hpg, axis=1)], axis=0)
    tk = s_scr.shape[0]
    ppt = tk // PAIR
    t_d = i // ppt

    def scores(t):
        ks = pl.multiple_of(t * tk, tk)
        s = jnp.dot(kaug_ref[0, pl.ds(ks, tk), :], qaug_scr[...], preferred_element_type=f32)
        tabs = [tabs_ref[0, jnp.clip(i - ppt * t - j, 0, NEAR_PAIRS)] for j in range(ppt)]
        return s + jnp.concatenate(tabs, axis=0)

    s_scr[...] = scores(t_d)

    def tile(k_it, carry):
        m, acc = carry
        t = t_d - k_it
        s = s_scr[...]
        s_next = scores(jnp.maximum(t - 1, 0))
        m_new = jnp.maximum(m, jnp.max(s, axis=0, keepdims=True))
        alpha = jnp.exp2(m - m_new)
        p = jnp.exp2(s - m_new).astype(bf16)
        ks = pl.multiple_of(t * tk, tk)
        acc = alpha * acc + jnp.dot(vgt_ref[0, 0, :, pl.ds(ks, tk)], p, preferred_element_type=f32)
        s_scr[...] = s_next
        return m_new, acc

    _, acc_t = lax.fori_loop(0, t_d + 1, tile, (jnp.full((1, rows), NEG, f32), jnp.zeros((2 * dh, rows), f32)))
    o_sel = (acc_t[:dh] / jnp.maximum(acc_t[dh:dh + 1], 1e-30)).T

    gates = jax.nn.sigmoid(misc_ref[0])
    outs = []
    for h in range(hpg):
        r = slice(h * qb, (h + 1) * qb)
        acc = None
        for jb, branch in enumerate((o_cmp[r], o_sel[r], o_win[r])):
            la = 8 + 3 * h + jb
            lb = 8 + 3 * (hpg + h) + jb
            term = jnp.where(is_g0, gates[:, la:la + 1], gates[:, lb:lb + 1]) * branch
            acc = term if acc is None else acc + term
        outs.append(acc)
    o_ref[0] = jnp.concatenate(outs, axis=1).astype(o_ref.dtype)


def _nsa_tables(rel_bias):
    qb, hpg = NSA_QBLOCK, NSA_HPG
    rows = hpg * qb
    q = (np.arange(rows) % qb)[:, None]
    c = np.arange(PAIR)[None, :]
    d_sel = [PAIR * idx + q - c for idx in range(NEAR_PAIRS + 1)]
    assert PAIR * NEAR_PAIRS - (PAIR - 1) >= FAR_DIST
    mm = 2 * NEAR_PAIRS - 1 - c
    d_cmp = np.maximum(np.where(mm >= -1, NSA_BLOCK * mm + q - (NSA_BLOCK - 1), 0), 0)
    assert NSA_BLOCK * (2 * NEAR_PAIRS) - (NSA_BLOCK - 1) >= FAR_DIST
    cw = np.arange(NSA_WINDOW + qb)[None, :]
    d_win = q + NSA_WINDOW - cw
    d_win = np.where((d_win >= 0) & (d_win < NSA_WINDOW), d_win, -1)
    n_win = (NSA_WINDOW + qb) // PAIR
    tiles = d_sel + [d_cmp] + [d_win[:, k * PAIR:(k + 1) * PAIR] for k in range(n_win)]
    dist = np.broadcast_to(np.stack(tiles)[None], (NSA_KV_GROUPS, len(tiles), rows, PAIR))
    tab = _bias_table(rel_bias, jnp.asarray(dist, i32), slab=qb)
    ns = NEAR_PAIRS + 1
    tabs = tab[:, :ns]
    tabc = tab[:, ns]
    tabw = jnp.concatenate([tab[:, ns + 1 + k] for k in range(n_win)], axis=-1)
    c31 = jnp.repeat(rel_bias.astype(f32)[REL_BUCKETS - 1].reshape(NSA_KV_GROUPS, hpg), qb, axis=1)
    return tabs, tabw, tabc, c31.reshape(NSA_KV_GROUPS, rows, 1)


def _nsa_prompt(q_b, misc, kv4, kvwin, rel_bias):
    bsz, t, _ = q_b.shape
    dh, qb = NSA_DH, NSA_QBLOCK
    nblk = t // NSA_BLOCK
    assert nblk == PAIR and t % NSA_KEY_TILE == 0
    kcvc = _block_means(kv4, tb=MEANS_TB)
    onehot = (jnp.arange(t, dtype=i32)[:, None] // NSA_BLOCK == jnp.arange(nblk, dtype=i32)[None, :]).astype(bf16)
    kaug = jnp.concatenate([kv4[:, :, 4 * dh:6 * dh].astype(bf16),
                            jnp.broadcast_to(onehot[None], (bsz, t, nblk))], axis=-1)
    ones = jnp.ones((bsz, t, dh), bf16)
    vgt = jnp.stack([jnp.swapaxes(jnp.concatenate([kv4[:, :, (6 + g) * dh:(7 + g) * dh].astype(bf16), ones],
                                                   axis=-1), 1, 2)
                     for g in range(NSA_KV_GROUPS)], axis=1)
    pad = ((0, 0), (NSA_WINDOW, 0), (0, 0))
    kw = jnp.pad(kvwin[:, :, :2 * dh].astype(bf16), pad)
    vw = jnp.pad(kvwin[:, :, 2 * dh:].astype(bf16), pad)
    tabs, tabw, tabc, c31 = _nsa_tables(rel_bias)
    tabs, tabw = jnp.swapaxes(tabs, 2, 3) * LOG2E, tabw * LOG2E
    rows = NSA_HPG * qb
    gw = NSA_HPG * dh
    span = NSA_WINDOW + qb
    return pl.pallas_call(
        _nsa_prompt_kernel,
        grid=(bsz, NSA_KV_GROUPS, t // qb),
        in_specs=[pl.BlockSpec((1, qb, gw), lambda b, g, i: (b, i, g)),
                  pl.BlockSpec((1, qb, MISC_W), lambda b, g, i: (b, i, 0)),
                  pl.BlockSpec((1, nblk, 4 * dh), lambda b, g, i: (b, 0, 0)),
                  pl.BlockSpec((1, t, 2 * dh + nblk), lambda b, g, i: (b, 0, 0)),
                  pl.BlockSpec((1, 1, 2 * dh, t), lambda b, g, i: (b, g, 0, 0)),
                  pl.BlockSpec((1, t + NSA_WINDOW, 2 * dh), lambda b, g, i: (b, 0, 0)),
                  pl.BlockSpec((1, t + NSA_WINDOW, 2 * dh), lambda b, g, i: (b, 0, 0)),
                  pl.BlockSpec((1, NEAR_PAIRS + 1, PAIR, rows), lambda b, g, i: (g, 0, 0, 0)),
                  pl.BlockSpec((1, rows, span), lambda b, g, i: (g, 0, 0)),
                  pl.BlockSpec((1, rows, PAIR), lambda b, g, i: (g, 0, 0)),
                  pl.BlockSpec((1, rows, 1), lambda b, g, i: (g, 0, 0))],
        out_specs=pl.BlockSpec((1, qb, gw), lambda b, g, i: (b, i, g)),
        out_shape=jax.ShapeDtypeStruct((bsz, t, NSA_HEADS * dh), bf16),
        scratch_shapes=[pltpu.VMEM((2 * dh + nblk, rows), bf16), pltpu.VMEM((NSA_KEY_TILE, rows), f32)],
        compiler_params=_cparams(("parallel", "parallel", "arbitrary")),
        name="nsa_prompt",
    )(q_b, misc, kcvc, kaug, vgt, kw, vw, tabs, tabw, tabc, c31)


MEANS_PAGES = 16


def _nsa_s_means_kernel(pt_ref, *refs):
    del pt_ref
    x_refs, o_ref = refs[:-1], refs[-1]
    p = pl.program_id(1)

    @pl.when(p == 0)
    def _():
        o_ref[...] = jnp.zeros_like(o_ref)

    page = x_refs[0].shape[-1]
    per_page = page // NSA_BLOCK
    nblk = o_ref.shape[-1]
    assert page == nblk
    planes = 2 * NSA_KV_GROUPS
    xs = jnp.concatenate([x_ref[0, j, g] for x_ref in x_refs for j in range(2) for g in range(NSA_KV_GROUPS)],
                         axis=0)
    tok = lax.broadcasted_iota(i32, (page, nblk), 0)
    col = lax.broadcasted_iota(i32, (page, nblk), 1)
    pool = (col == (tok >> int(math.log2(NSA_BLOCK)))).astype(bf16)
    hi, lo = _split2(xs)
    sums = jnp.dot(hi, pool, preferred_element_type=f32) + jnp.dot(lo, pool, preferred_element_type=f32)
    rows = planes * NSA_DH
    acc = None
    for k in range(len(x_refs)):
        shift = (p * len(x_refs) + k) * per_page
        part = pltpu.roll(sums[k * rows:(k + 1) * rows], shift, 1)
        acc = part if acc is None else acc + part
    acc = acc * (1.0 / NSA_BLOCK)
    for pl_i in range(planes):
        o_ref[0, pl_i] = o_ref[0, pl_i] + acc[pl_i * NSA_DH:(pl_i + 1) * NSA_DH]


def _nsa_s_means(cache_t, page_table):
    db, n_pages = page_table.shape
    page = cache_t.shape[-1]
    nblk = n_pages * page // NSA_BLOCK
    kp = MEANS_PAGES
    assert n_pages % kp == 0

    def page_spec(k):
        return pl.BlockSpec((1, 2, NSA_KV_GROUPS, NSA_DH, page),
                            lambda b, p, pt: (pt[b * n_pages + p * kp + k], 0, 0, 0, 0))

    return pl.pallas_call(
        _nsa_s_means_kernel,
        grid_spec=pltpu.PrefetchScalarGridSpec(
            num_scalar_prefetch=1,
            grid=(db, n_pages // kp),
            in_specs=[page_spec(k) for k in range(kp)],
            out_specs=pl.BlockSpec((1, 2 * NSA_KV_GROUPS, NSA_DH, nblk), lambda b, p, pt: (b, 0, 0, 0))),
        out_shape=jax.ShapeDtypeStruct((db, 2 * NSA_KV_GROUPS, NSA_DH, nblk), f32),
        compiler_params=_cparams(("parallel", "arbitrary")),
        name="nsa_sample_means",
    )(page_table.reshape(-1), *([cache_t] * kp))


def _nsa_s_scores_kernel(q_ref, kcvc_ref, win_ref, kvn_ref, tc_ref, tw_ref, b0_ref, oc_ref, ow_ref, sel_ref):
    dh, hpg, ng = NSA_DH, NSA_HPG, NSA_KV_GROUPS
    sb = q_ref.shape[0]
    nblk = kcvc_ref.shape[-1]
    row = lax.broadcasted_iota(i32, (NSA_HEADS, 1), 0)
    in_g = [(row >= g * hpg) & (row < (g + 1) * hpg) for g in range(ng)]
    pairs = [(b, g) for b in range(sb) for g in range(ng)]
    q = [q_ref[b] * (dh ** -0.5) for b in range(sb)]

    s_c = [_mm3(q[b], kcvc_ref[b, g]) + tc_ref[...] for b, g in pairs]
    p_c = []
    for s in s_c:
        p = jnp.exp(s - jnp.max(s, axis=-1, keepdims=True))
        p_c.append(p / jnp.maximum(jnp.sum(p, axis=-1, keepdims=True), 1e-30))
    o_c = [_mm_nt(p_c[k], kcvc_ref[b, ng + g]) for k, (b, g) in enumerate(pairs)]
    imp = jnp.concatenate([jnp.sum(jnp.where(in_g[g], p_c[k], 0.0), axis=0, keepdims=True)
                           for k, (b, g) in enumerate(pairs)], axis=0)
    for b in range(sb):
        oc_ref[b] = jnp.where(in_g[0], o_c[b * ng], o_c[b * ng + 1])

    s_w = [_mm3(q[b], win_ref[b, 0, g]) + tw_ref[...] for b, g in pairs]
    o_w = []
    for k, (b, g) in enumerate(pairs):
        kvn = kvn_ref[b]
        s_n = jnp.sum(q[b] * kvn[:, g * dh:(g + 1) * dh], axis=-1, keepdims=True) + b0_ref[...]
        m_w = jnp.maximum(jnp.max(s_w[k], axis=-1, keepdims=True), s_n)
        p_w = jnp.exp(s_w[k] - m_w)
        p_n = jnp.exp(s_n - m_w)
        l_w = jnp.sum(p_w, axis=-1, keepdims=True) + p_n
        v_n = kvn[:, (ng + g) * dh:(ng + g + 1) * dh]
        o_w.append((_mm_nt(p_w, win_ref[b, 1, g]) + p_n * v_n) / jnp.maximum(l_w, 1e-30))
    for b in range(sb):
        ow_ref[b] = jnp.where(in_g[0], o_w[b * ng], o_w[b * ng + 1])

    lane = lax.broadcasted_iota(i32, imp.shape, 1)
    n_sel = NSA_TOPN - 1
    forced_blocks = [0] + [nblk - k for k in range(1, NSA_LOCAL)]
    imp = jnp.where((lane == 0) | (lane > nblk - NSA_LOCAL), -jnp.inf, imp)
    picks = jnp.zeros(imp.shape, i32)
    for it, fb in enumerate(forced_blocks):
        picks = jnp.where(lane == it, fb, picks)
    for it in range(len(forced_blocks), n_sel):
        mx = jnp.max(imp, axis=-1, keepdims=True)
        first = jnp.min(jnp.where(imp == mx, lane, nblk), axis=-1, keepdims=True)
        picks = jnp.where(lane == it, first, picks)
        imp = jnp.where(lane == first, -jnp.inf, imp)
    for b in range(sb):
        sel_ref[b] = picks[b * ng:(b + 1) * ng]


def _nsa_s_sel_kernel(sel_ref, pt_ref, q_ref, rb_ref, kvn_ref, oc_ref, ow_ref, gt_ref, *refs, n_sel, past):
    del pt_ref
    k_refs, v_refs, o_ref = refs[:n_sel], refs[n_sel:2 * n_sel], refs[2 * n_sel]
    b = pl.program_id(0)
    g = pl.program_id(1)
    dh, hpg = NSA_DH, NSA_HPG
    is_g0 = g == 0
    page = k_refs[0].shape[-1]
    per_page = page // NSA_BLOCK
    q = q_ref[0] * (dh ** -0.5)
    kvn = kvn_ref[0]
    lane = lax.broadcasted_iota(i32, (NSA_HEADS, page), 1)
    s_parts, d_parts = [], []
    for j in range(n_sel):
        nb = sel_ref[(b * NSA_KV_GROUPS + g) * n_sel + j]
        in_blk = (lane >> int(math.log2(NSA_BLOCK))) == (nb % per_page)
        d_parts.append(jnp.where(in_blk, past - ((nb // per_page) * page + lane), -1))
        s_parts.append(_mm(q, k_refs[j][0, 0, 0]))
    d = jnp.concatenate(d_parts, axis=1)
    bias = jnp.broadcast_to(rb_ref[0], d.shape)
    for k in range(1, REL_BUCKETS):
        bias = jnp.where(d >= _BUCKET_THR[k - 1], rb_ref[k], bias)
    s = jnp.concatenate(s_parts, axis=1) + jnp.where(d < 0, NEG, bias)
    k_n = jnp.where(is_g0, kvn[:, 4 * dh:5 * dh], kvn[:, 5 * dh:6 * dh])
    v_n = jnp.where(is_g0, kvn[:, 6 * dh:7 * dh], kvn[:, 7 * dh:8 * dh])
    s_n = jnp.sum(q * k_n, axis=-1, keepdims=True) + rb_ref[0]
    m = jnp.maximum(jnp.max(s, axis=-1, keepdims=True), s_n)
    p = jnp.exp(s - m)
    p_n = jnp.exp(s_n - m)
    l = jnp.sum(p, axis=-1, keepdims=True) + p_n
    acc = p_n * v_n
    for j in range(n_sel):
        acc = acc + _mm_nt(p[:, j * page:(j + 1) * page], v_refs[j][0, 0, 0])
    o_s = acc / jnp.maximum(l, 1e-30)
    gt = gt_ref[0]
    o = gt[:, 0:1] * oc_ref[0] + gt[:, 1:2] * o_s + gt[:, 2:3] * ow_ref[0]
    row = lax.broadcasted_iota(i32, o.shape, 0)
    in_g = (row >= g * hpg) & (row < (g + 1) * hpg)

    @pl.when(is_g0)
    def _():
        o_ref[0] = o

    @pl.when(jnp.logical_not(is_g0))
    def _():
        o_ref[0] = jnp.where(in_g, o, o_ref[0])


def _nsa_sample(q_b, gates, kv4_new, kvwin_new, cache_kv, cache_win, page_table, rel_bias):
    db = q_b.shape[0]
    dh = NSA_DH
    n_pool, page = cache_kv.shape[:2]
    n_pages = page_table.shape[1]
    past = n_pages * page
    nblk = past // NSA_BLOCK
    wb = cache_win.shape[1]
    assert nblk == PAIR and page % NSA_BLOCK == 0 and wb == NSA_WINDOW and nblk > NSA_LOCAL
    cache_t = jnp.transpose(cache_kv, (0, 2, 3, 4, 1)).astype(f32)
    win_t = jnp.transpose(cache_win, (0, 2, 3, 4, 1)).astype(f32)
    kcvc = _nsa_s_means(cache_t, page_table)

    n = np.arange(nblk)
    d_cmp = past - (n * NSA_BLOCK + NSA_BLOCK - 1)
    jw = np.arange(wb)
    d_win = np.where(jw >= 1, wb - jw, -1)
    dist = np.broadcast_to(np.concatenate([d_cmp, d_win])[None, :], (NSA_HEADS, nblk + wb))
    tab = _bias_table(rel_bias, jnp.asarray(dist[None, None], i32), slab=1)[0, 0]
    t_cmp, t_win = tab[:, :nblk], tab[:, nblk:]
    rb = rel_bias.astype(f32)
    b0 = rb[0].reshape(NSA_HEADS, 1)

    q3 = q_b.reshape(db, NSA_HEADS, dh)
    sb = max(s for s in (8, 4, 2, 1) if db % s == 0)
    o_c, o_w, sel = pl.pallas_call(
        _nsa_s_scores_kernel,
        grid=(db // sb,),
        in_specs=[pl.BlockSpec((sb, NSA_HEADS, dh), lambda b: (b, 0, 0)),
                  pl.BlockSpec((sb, 2 * NSA_KV_GROUPS, dh, nblk), lambda b: (b, 0, 0, 0)),
                  pl.BlockSpec((sb, 2, NSA_KV_GROUPS, dh, wb), lambda b: (b, 0, 0, 0, 0)),
                  pl.BlockSpec((sb, 1, 4 * dh), lambda b: (b, 0, 0)),
                  pl.BlockSpec((NSA_HEADS, nblk), lambda b: (0, 0)),
                  pl.BlockSpec((NSA_HEADS, wb), lambda b: (0, 0)),
                  pl.BlockSpec((NSA_HEADS, 1), lambda b: (0, 0))],
        out_specs=[pl.BlockSpec((sb, NSA_HEADS, dh), lambda b: (b, 0, 0)),
                   pl.BlockSpec((sb, NSA_HEADS, dh), lambda b: (b, 0, 0)),
                   pl.BlockSpec((sb, NSA_KV_GROUPS, nblk), lambda b: (b, 0, 0))],
        out_shape=[jax.ShapeDtypeStruct((db, NSA_HEADS, dh), f32),
                   jax.ShapeDtypeStruct((db, NSA_HEADS, dh), f32),
                   jax.ShapeDtypeStruct((db, NSA_KV_GROUPS, nblk), i32)],
        compiler_params=_cparams(("parallel",)),
        name="nsa_sample_scores",
    )(q3, kcvc, win_t, kvwin_new.reshape(db, 1, 4 * dh), t_cmp, t_win, b0)

    n_sel = NSA_TOPN - 1
    sel_flat = sel[:, :, :n_sel].reshape(-1)
    per_page = page // NSA_BLOCK

    def page_spec(j, plane):
        def imap(b, g, s, p):
            nb = s[(b * NSA_KV_GROUPS + g) * n_sel + j]
            return (p[b * n_pages + nb // per_page], plane, g, 0, 0)
        return pl.BlockSpec((1, 1, 1, dh, page), imap)

    const = lambda shape: pl.BlockSpec(shape, lambda b, g, s, p: (0,) * len(shape))
    per_b = lambda shape: pl.BlockSpec((1,) + shape, lambda b, g, s, p: (b,) + (0,) * len(shape))
    o = pl.pallas_call(
        functools.partial(_nsa_s_sel_kernel, n_sel=n_sel, past=past),
        grid_spec=pltpu.PrefetchScalarGridSpec(
            num_scalar_prefetch=2,
            grid=(db, NSA_KV_GROUPS),
            in_specs=[per_b((NSA_HEADS, dh)), const((REL_BUCKETS, NSA_HEADS, 1)), per_b((1, 8 * dh)),
                      per_b((NSA_HEADS, dh)), per_b((NSA_HEADS, dh)), per_b((NSA_HEADS, 3))]
                     + [page_spec(j, 2) for j in range(n_sel)] + [page_spec(j, 3) for j in range(n_sel)],
            out_specs=per_b((NSA_HEADS, dh))),
        out_shape=jax.ShapeDtypeStruct((db, NSA_HEADS, dh), f32),
        compiler_params=_cparams(("parallel", "arbitrary")),
        name="nsa_sample_selected",
    )(sel_flat, page_table.reshape(-1), q3, rb.reshape(REL_BUCKETS, NSA_HEADS, 1), kv4_new.reshape(db, 1, 8 * dh),
      o_c, o_w, gates, *([cache_t] * (2 * n_sel)))
    return o.reshape(db, NSA_HEADS * dh)


def _gelu_tanh(x):
    return 0.5 * x * (1.0 + jnp.tanh(math.sqrt(2.0 / math.pi) * (x + 0.044715 * (x * x * x))))


def _lru_gates(xc, wa_ref, wx_ref, ba_ref, bx_ref, lam_ref, prec):
    r_parts, i_parts = [], []
    for n in range(RNN_BLOCKS):
        xb = xc[:, n * RNN_BW:(n + 1) * RNN_BW]
        r_parts.append(_dotp(xb, wa_ref[n], prec))
        i_parts.append(_dotp(xb, wx_ref[n], prec))
    r = jax.nn.sigmoid(jnp.concatenate(r_parts, axis=1) + ba_ref[...])
    i = jax.nn.sigmoid(jnp.concatenate(i_parts, axis=1) + bx_ref[...])
    log_a = -RG_C * r * _softplus(-lam_ref[...])
    a = jnp.exp(log_a)
    t = jnp.tanh(log_a)
    b = jnp.sqrt(jnp.maximum(-2.0 * t / (1.0 - t), 0.0)) * (i * xc)
    return a, b


def _lru_kernel(rec_ref, gate_ref, cw_ref, cb_ref, wa_ref, wx_ref, ba_ref, bx_ref, lam_ref, cinit_ref, h0_ref,
                y_ref, hfin_ref, xc_scr, a_scr, b_scr, hs_scr, h_scr, *, tb):
    j = pl.program_id(1)

    @pl.when(j == 0)
    def _():
        xc_scr[0:8, :] = cinit_ref[0]
        h_scr[...] = h0_ref[0]

    x = rec_ref[0]
    xc_scr[8:8 + tb, :] = x
    xc = xc_scr[5:5 + tb, :] * cw_ref[0:1, :]
    xc = xc + xc_scr[6:6 + tb, :] * cw_ref[1:2, :]
    xc = xc + xc_scr[7:7 + tb, :] * cw_ref[2:3, :]
    xc = xc + x * cw_ref[3:4, :]
    xc = xc + cb_ref[...]
    xc_scr[0:8, :] = xc_scr[tb:tb + 8, :]
    a, b = _lru_gates(xc, wa_ref, wx_ref, ba_ref, bx_ref, lam_ref, 1)
    a_scr[...] = a
    b_scr[...] = b

    def step(t, h):
        h = a_scr[pl.ds(t, 1), :] * h + b_scr[pl.ds(t, 1), :]
        hs_scr[pl.ds(t, 1), :] = h
        return h

    h = lax.fori_loop(0, tb, step, h_scr[...], unroll=8)
    h_scr[...] = h
    y_ref[0] = (_gelu_tanh(gate_ref[0]) * hs_scr[...]).astype(y_ref.dtype)

    @pl.when(j == pl.num_programs(1) - 1)
    def _():
        hfin_ref[0] = h


def _lru_prompt(rec, gate, conv_w, conv_b, wa, wx, ba, bx, lam, *, tb):
    bsz, t, w = rec.shape
    row = lambda a: a.reshape(1, w).astype(f32)
    cinit = jnp.zeros((bsz, 8, w), f32)
    h0 = jnp.zeros((bsz, 1, w), f32)
    full = lambda shape: pl.BlockSpec(shape, lambda b, j: (0,) * len(shape))
    return pl.pallas_call(
        functools.partial(_lru_kernel, tb=tb),
        grid=(bsz, t // tb),
        in_specs=[pl.BlockSpec((1, tb, w), lambda b, j: (b, j, 0)),
                  pl.BlockSpec((1, tb, w), lambda b, j: (b, j, 0)),
                  full((RNN_CONV, w)), full((1, w)),
                  full((RNN_BLOCKS, RNN_BW, RNN_BW)), full((RNN_BLOCKS, RNN_BW, RNN_BW)),
                  full((1, w)), full((1, w)), full((1, w)),
                  pl.BlockSpec((1, 8, w), lambda b, j: (b, 0, 0)),
                  pl.BlockSpec((1, 1, w), lambda b, j: (b, 0, 0))],
        out_specs=[pl.BlockSpec((1, tb, w), lambda b, j: (b, j, 0)),
                   pl.BlockSpec((1, 1, w), lambda b, j: (b, 0, 0))],
        out_shape=[jax.ShapeDtypeStruct((bsz, t, w), bf16), jax.ShapeDtypeStruct((bsz, 1, w), f32)],
        scratch_shapes=[pltpu.VMEM((tb + 8, w), f32), pltpu.VMEM((tb, w), f32), pltpu.VMEM((tb, w), f32),
                        pltpu.VMEM((tb, w), f32), pltpu.VMEM((1, w), f32)],
        compiler_params=_cparams(("parallel", "arbitrary")),
        name="rglru",
    )(rec, gate, conv_w.astype(f32), row(conv_b), wa.astype(f32), wx.astype(f32), row(ba), row(bx), row(lam),
      cinit, h0)


def _lru_step_kernel(rec_ref, gate_ref, b0_ref, b1_ref, b2_ref, cw_ref, cb_ref, wa_ref, wx_ref, ba_ref, bx_ref,
                     lam_ref, h0_ref, y_ref, h_ref):
    xc = b0_ref[...] * cw_ref[0:1, :]
    xc = xc + b1_ref[...] * cw_ref[1:2, :]
    xc = xc + b2_ref[...] * cw_ref[2:3, :]
    xc = xc + rec_ref[...] * cw_ref[3:4, :]
    xc = xc + cb_ref[...]
    a, b = _lru_gates(xc, wa_ref, wx_ref, ba_ref, bx_ref, lam_ref, 3)
    h = a * h0_ref[...] + b
    h_ref[...] = h
    y_ref[...] = _gelu_tanh(gate_ref[...]) * h


def _lru_sample(rec, gate, conv_buf, h0, conv_w, conv_b, wa, wx, ba, bx, lam):
    db, w = rec.shape
    row = lambda a: a.reshape(1, w).astype(f32)
    buf = conv_buf.astype(f32)
    return pl.pallas_call(
        _lru_step_kernel,
        out_shape=[jax.ShapeDtypeStruct((db, w), f32), jax.ShapeDtypeStruct((db, w), f32)],
        compiler_params=pltpu.CompilerParams(vmem_limit_bytes=VMEM_LIMIT),
        name="rglru_step",
    )(rec, gate, buf[:, 0], buf[:, 1], buf[:, 2], conv_w.astype(f32), row(conv_b), wa.astype(f32), wx.astype(f32),
      row(ba), row(bx), row(lam), h0.astype(f32))


def _top2_kernel(l_ref, e_ref, g_ref, cnt_ref, cnt_scr):
    @pl.when(pl.program_id(0) == 0)
    def _():
        cnt_scr[...] = jnp.zeros_like(cnt_scr)

    lg = l_ref[...]
    tm, w = lg.shape
    lane = lax.broadcasted_iota(i32, lg.shape, 1)
    lg = jnp.where(lane < N_EXPERTS, lg, -jnp.inf)
    m1 = jnp.max(lg, axis=-1, keepdims=True)
    i1 = jnp.min(jnp.where(lg == m1, lane, w), axis=-1, keepdims=True)
    lg2 = jnp.where(lane == i1, -jnp.inf, lg)
    m2 = jnp.max(lg2, axis=-1, keepdims=True)
    i2 = jnp.min(jnp.where(lg2 == m2, lane, w), axis=-1, keepdims=True)
    e2 = jnp.exp(m2 - m1)
    den = 1.0 + e2
    g_ref[...] = jnp.where(lane == 0, 1.0 / den, jnp.where(lane == 1, e2 / den, 0.0))
    hit1, hit2 = lane == i1, lane == i2
    routed = (hit1 | hit2).astype(bf16)
    r_i = lax.broadcasted_iota(i32, (tm, tm), 0)
    c_i = lax.broadcasted_iota(i32, (tm, tm), 1)
    before = jnp.dot((c_i < r_i).astype(bf16), routed, preferred_element_type=f32) + cnt_scr[...]
    rank1 = jnp.sum(jnp.where(hit1, before, 0.0), axis=-1, keepdims=True).astype(i32)
    rank2 = jnp.sum(jnp.where(hit2, before, 0.0), axis=-1, keepdims=True).astype(i32)
    e_ref[...] = jnp.where(lane == 0, i1, jnp.where(lane == 1, i2, jnp.where(lane == 2, rank1,
                                                                               jnp.where(lane == 3, rank2, 0))))
    cnt_scr[...] += jnp.sum(routed.astype(f32), axis=0, keepdims=True)
    cnt_ref[...] = cnt_scr[...].astype(i32)


def _top2(logits, *, tm):
    n, w = logits.shape
    return pl.pallas_call(
        _top2_kernel,
        grid=(n // tm,),
        in_specs=[pl.BlockSpec((tm, w), lambda i: (i, 0))],
        out_specs=[pl.BlockSpec((tm, w), lambda i: (i, 0)), pl.BlockSpec((tm, w), lambda i: (i, 0)),
                   pl.BlockSpec((1, w), lambda i: (0, 0))],
        out_shape=[jax.ShapeDtypeStruct((n, w), i32), jax.ShapeDtypeStruct((n, w), f32),
                   jax.ShapeDtypeStruct((1, w), i32)],
        scratch_shapes=[pltpu.VMEM((1, w), f32)],
        compiler_params=_cparams(("arbitrary",)),
        name="moe_top2",
    )(logits)


def _moe_kernel(be_ref, nu_ref, nv_ref, xs_ref, w1_ref, w3_ref, w2_ref, o_ref, acc_scr):
    del be_ref
    i = pl.program_id(0)
    f = pl.program_id(1)
    last = pl.num_programs(1) - 1
    used = i < nu_ref[0]
    parts = 4
    quarter = xs_ref.shape[0] // parts
    n_q = (nv_ref[i] + quarter - 1) // quarter

    @pl.when(used & (f == 0))
    def _():
        acc_scr[...] = jnp.zeros_like(acc_scr)

    def update(rows):
        x = xs_ref[rows, :]
        act = _silu(_mm(x, w1_ref[0])) * _mm(x, w3_ref[0])
        acc_scr[rows, :] += _mm(act, w2_ref[0])

    for q in range(1, parts + 1):
        @pl.when(used & (n_q == q))
        def _(q=q):
            update(slice(0, q * quarter))

    @pl.when(used & (f == last))
    def _():
        o_ref[...] = acc_scr[...]

    @pl.when(jnp.logical_not(used) & (f == last))
    def _():
        o_ref[...] = jnp.zeros_like(o_ref)


def _moe_experts(xs, blk_e, n_used, n_valid, w1, w3, w2, *, tm, tf):
    rows, d = xs.shape
    fdim = w1.shape[2]
    nf = fdim // tf

    def wcol(i, f, be, nu, nv):
        return (be[i], 0, jnp.where(i < nu[0], f, nf - 1))

    def wrow(i, f, be, nu, nv):
        return (be[i], jnp.where(i < nu[0], f, nf - 1), 0)

    return pl.pallas_call(
        _moe_kernel,
        grid_spec=pltpu.PrefetchScalarGridSpec(
            num_scalar_prefetch=3,
            grid=(rows // tm, nf),
            in_specs=[pl.BlockSpec((tm, d), lambda i, f, be, nu, nv: (i, 0)),
                      pl.BlockSpec((1, d, tf), wcol),
                      pl.BlockSpec((1, d, tf), wcol),
                      pl.BlockSpec((1, tf, d), wrow)],
            out_specs=pl.BlockSpec((tm, d), lambda i, f, be, nu, nv: (i, 0)),
            scratch_shapes=[pltpu.VMEM((tm, d), f32)]),
        out_shape=jax.ShapeDtypeStruct((rows, d), f32),
        compiler_params=_cparams(("arbitrary", "arbitrary")),
        name="moe_experts",
    )(blk_e, n_used, n_valid, xs, w1, w3, w2)


def _combine_kernel(x_ref, gate_ref, y0_ref, y1_ref, g_ref, nw_ref, o_ref):
    gw = g_ref[...]
    y = y0_ref[...] * gw[:, 0:1] + y1_ref[...] * gw[:, 1:2]
    x = x_ref[0] + gate_ref[0] * y
    o_ref[0] = x * lax.rsqrt(jnp.mean(x * x, axis=-1, keepdims=True) + EPS) * nw_ref[...]


def _moe_combine(x, gate, y0, y1, gw, norm_w, *, tm, row0):
    bsz, t, d = x.shape
    assert row0 % tm == 0
    tok = pl.BlockSpec((1, tm, d), lambda b, i: (b, i, 0))
    flat = lambda w: pl.BlockSpec((tm, w), lambda b, i: (row0 // tm + b * (t // tm) + i, 0))
    return pl.pallas_call(
        _combine_kernel,
        grid=(bsz, t // tm),
        in_specs=[tok, _mod_spec(gate, tm), flat(d), flat(d), flat(gw.shape[-1]),
                  pl.BlockSpec((1, d), lambda b, i: (0, 0))],
        out_specs=tok,
        out_shape=jax.ShapeDtypeStruct((bsz, t, d), f32),
        compiler_params=_cparams(("parallel", "parallel")),
        name="moe_combine",
    )(x, gate, y0, y1, gw, norm_w.reshape(1, d).astype(f32))


MOE_TM = 1024
MOE_TF = 512


def _moe_dispatch(e_idx, rank, counts, n_tok):
    tm = MOE_TM
    n_assign = n_tok * TOP_K
    padded = (counts + tm - 1) // tm * tm
    pend = jnp.cumsum(padded)
    pstart = pend - padded
    experts = jnp.arange(N_EXPERTS, dtype=i32)
    dest = jnp.sum(jnp.where(e_idx[:, :, None] == experts, pstart, 0), axis=-1) + rank
    n_blocks = (n_assign + N_EXPERTS * (tm - 1)) // tm
    rows = n_blocks * tm
    tok = jnp.broadcast_to(jnp.arange(n_tok, dtype=i32)[:, None], (n_tok, TOP_K))
    row_tok = (jnp.arange(rows, dtype=i32) % n_tok).at[dest.reshape(-1)].set(
        tok.reshape(-1), unique_indices=True, mode='promise_in_bounds')
    n_used = (pend[-1] // tm).astype(i32)
    blk = jnp.minimum(jnp.arange(n_blocks, dtype=i32), n_used - 1) * tm
    blk_e = jnp.minimum(jnp.sum((blk[:, None] >= pend[None, :]).astype(i32), axis=1), N_EXPERTS - 1)
    is_e = blk_e[:, None] == experts[None, :]
    first_row = jnp.sum(jnp.where(is_e, pstart, 0), axis=1)
    n_valid = jnp.clip(jnp.sum(jnp.where(is_e, counts, 0), axis=1) - (blk - first_row), 0, tm)
    return dest, row_tok, blk_e, n_used.reshape(1), n_valid.astype(i32)


def _w_in0_layout(w_in0):
    c_qkv = GDN_CONV_CH
    c_z = c_qkv + GDN_HEADS * GDN_DV
    c_ab = c_z + 2 * GDN_HEADS
    c_q = c_ab + NSA_HEADS * NSA_DH
    c_kv = c_q + 6 * NSA_KV_GROUPS * NSA_DH
    c_g = c_kv + 3 * NSA_HEADS
    assert c_g == w_in0.shape[1]
    n_misc = 2 * GDN_HEADS + 3 * NSA_HEADS
    w = jnp.concatenate([w_in0[:, :c_z], w_in0[:, c_ab:c_kv], w_in0[:, c_z:c_ab], w_in0[:, c_kv:c_g],
                         jnp.zeros((w_in0.shape[0], MISC_W - n_misc), w_in0.dtype)], axis=1)
    widths = (GDN_CONV_CH, GDN_HEADS * GDN_DV, NSA_HEADS * NSA_DH, 4 * NSA_KV_GROUPS * NSA_DH,
              2 * NSA_KV_GROUPS * NSA_DH, MISC_W)
    splits, s = [], 0
    for wd in widths:
        splits.append((s, s + wd))
        s += wd
    return w, tuple(splits)


def kernel(x_prompt, x_sample, c_prompt, c_sample, cache_nsa_kv, cache_nsa_win, state_gdn, state_gdn_conv, state_lru, state_lru_conv, page_table, rel_bias, w_ada, b_ada, norm_mix, norm_ffn, norm_final, w_in0, gdn_conv_w, gdn_a_log, gdn_dt_bias, gdn_norm_w, w_out0, ffn_w_gate, ffn_w_up, ffn_w_down, w_in1, lru_conv_w, lru_conv_b, lru_wa, lru_ba, lru_wx, lru_bx, lru_lambda, w_out1, moe_router, moe_w1, moe_w3, moe_w2):
    bsz, seq, d = x_prompt.shape
    db = x_sample.shape[0]
    assert x_sample.shape[1] == 1
    dh = NSA_DH

    n_c = bsz + db
    n_c_pad = -(-n_c // 8) * 8
    c_all = jnp.concatenate([c_prompt, c_sample, jnp.zeros((n_c_pad - n_c, d), f32)], axis=0)
    mods = _adaln(c_all, w_ada, b_ada).reshape(2, n_c_pad, N_MOD, d)
    mod_p = [[mods[l, :bsz, k].reshape(bsz, 1, d) for k in range(N_MOD)] for l in range(2)]
    mod_s = [[mods[l, bsz:n_c, k].reshape(1, db, d) for k in range(N_MOD)] for l in range(2)]

    w0, splits0 = _w_in0_layout(w_in0)
    splits1 = ((0, RNN_WIDTH), (RNN_WIDTH, 2 * RNN_WIDTH))
    router = jnp.concatenate([moe_router, jnp.zeros((d, MISC_W - N_EXPERTS), f32)], axis=1)
    bf = lambda w: w.astype(bf16)

    tm = PROMPT_TM
    xp = x_prompt
    qkv, z, q_b, kv4, kvwin, misc = _mod_matmul(xp, norm_mix[0], mod_p[0][0], mod_p[0][1], bf(w0), splits0,
                                                tm=tm, prec=1)
    o_a, p_gdn = _gdn(qkv, z, misc, gdn_conv_w, jnp.zeros((bsz, GDN_CONV - 1, GDN_CONV_CH), f32), gdn_a_log,
                      gdn_dt_bias, gdn_norm_w, jnp.zeros((bsz, GDN_HEADS, GDN_DK, GDN_DV), f32), tb=SCAN_TB, n_valid=seq)
    p_gdn_conv = qkv[:, seq - (GDN_CONV - 1):]
    o_b = _nsa_prompt(q_b, misc, kv4, kvwin, rel_bias)
    p_nsa_kv = kv4.reshape(bsz, seq, 4, NSA_KV_GROUPS, dh)
    keep = min(NSA_WINDOW, seq)
    p_nsa_win = kvwin[:, seq - keep:].reshape(bsz, keep, 2, NSA_KV_GROUPS, dh)
    xp = _proj_residual([o_a, o_b], bf(w_out0), xp, mod_p[0][2], tm=tm, prec=1)
    xp = _ffn(xp, norm_ffn[0], mod_p[0][3], mod_p[0][4], mod_p[0][5], bf(ffn_w_gate), bf(ffn_w_up), bf(ffn_w_down),
              tm=tm, tf=FFN_TF_PROMPT, prec=1)
    gate_br, rec_br = _mod_matmul(xp, norm_mix[1], mod_p[1][0], mod_p[1][1], bf(w_in1), splits1, tm=tm, prec=1)
    y_in, p_lru = _lru_prompt(rec_br, gate_br, lru_conv_w, lru_conv_b, lru_wa, lru_wx, lru_ba, lru_bx, lru_lambda,
                              tb=SCAN_TB)
    p_lru_conv = rec_br[:, seq - (RNN_CONV - 1):]
    xp = _proj_residual([y_in], bf(w_out1), xp, mod_p[1][2], tm=tm, prec=1)
    logit_p, h_p = _mod_matmul(xp, norm_ffn[1], mod_p[1][3], mod_p[1][4], router, ((0, MISC_W),), tm=tm, prec=3,
                               emit_h=f32)

    xs = x_sample.reshape(1, db, d)
    qkv_s, z_s, q_s, kv4_s, kvwin_s, misc_s = _mod_matmul(xs, norm_mix[0], mod_s[0][0], mod_s[0][1], w0, splits0,
                                                          tm=db, prec=3)
    per_seq = lambda a: a.reshape(db, 1, a.shape[-1])
    o_a_s, s_gdn = _gdn_step(per_seq(qkv_s), per_seq(z_s), per_seq(misc_s), gdn_conv_w, state_gdn_conv, gdn_a_log,
                             gdn_dt_bias, gdn_norm_w, state_gdn)
    o_a_s = o_a_s.reshape(1, db, GDN_HEADS * GDN_DV)
    s_gdn_conv = jnp.concatenate([state_gdn_conv[:, 1:], qkv_s.reshape(db, 1, GDN_CONV_CH)], axis=1)
    gates_s = jax.nn.sigmoid(misc_s[0, :, 2 * GDN_HEADS:2 * GDN_HEADS + 3 * NSA_HEADS]).reshape(db, NSA_HEADS, 3)
    o_b_s = _nsa_sample(q_s[0], gates_s, kv4_s[0], kvwin_s[0], cache_nsa_kv, cache_nsa_win, page_table, rel_bias)
    s_nsa_kv = kv4_s.reshape(db, 1, 4, NSA_KV_GROUPS, dh)
    s_nsa_win = jnp.concatenate([cache_nsa_win[:, 1:],
                                 kvwin_s.reshape(db, 1, 2, NSA_KV_GROUPS, dh).astype(cache_nsa_win.dtype)], axis=1)
    xs = _proj_residual([o_a_s, o_b_s.reshape(1, db, NSA_HEADS * dh)], w_out0, xs, mod_s[0][2], tm=db, prec=3)
    xs = _ffn(xs, norm_ffn[0], mod_s[0][3], mod_s[0][4], mod_s[0][5], ffn_w_gate, ffn_w_up, ffn_w_down,
              tm=db, tf=FFN_TF_SAMPLE, prec=3)
    gate_s, rec_s = _mod_matmul(xs, norm_mix[1], mod_s[1][0], mod_s[1][1], w_in1, splits1, tm=db, prec=3)
    y_in_s, s_lru = _lru_sample(rec_s[0], gate_s[0], state_lru_conv, state_lru, lru_conv_w, lru_conv_b, lru_wa, lru_wx,
                                lru_ba, lru_bx, lru_lambda)
    s_lru_conv = jnp.concatenate([state_lru_conv[:, 1:], rec_s.reshape(db, 1, RNN_WIDTH)], axis=1)
    xs = _proj_residual([y_in_s.reshape(1, db, RNN_WIDTH)], w_out1, xs, mod_s[1][2], tm=db, prec=3)
    logit_s, h_s = _mod_matmul(xs, norm_ffn[1], mod_s[1][3], mod_s[1][4], router, ((0, MISC_W),), tm=db, prec=3,
                               emit_h=f32)

    n_p = bsz * seq
    n_tok = n_p + db
    logits = jnp.concatenate([logit_p.reshape(n_p, MISC_W), logit_s.reshape(db, MISC_W)], axis=0)
    route, gw, cnt = _top2(logits, tm=max(t for t in range(8, 1025, 8) if n_tok % t == 0))
    dest, row_tok, blk_e, n_used, n_valid = _moe_dispatch(route[:, :TOP_K], route[:, TOP_K:2 * TOP_K],
                                                          cnt[0, :N_EXPERTS], n_tok)
    xs_rows = h_p.reshape(n_p, d)[jnp.minimum(row_tok, n_p - 1)]
    xs_rows = xs_rows.at[dest[n_p:].reshape(-1)].set(jnp.repeat(h_s.reshape(db, d), TOP_K, axis=0),
                                                     unique_indices=True, mode='promise_in_bounds')
    yb = _moe_experts(xs_rows, blk_e, n_used, n_valid, moe_w1, moe_w3, moe_w2, tm=MOE_TM, tf=MOE_TF)
    y0, y1 = yb[dest[:, 0]], yb[dest[:, 1]]
    y_prompt = _moe_combine(xp, mod_p[1][5], y0, y1, gw, norm_final, tm=tm, row0=0)
    y_sample = _moe_combine(xs, mod_s[1][5], y0, y1, gw, norm_final, tm=db, row0=n_p)

    return (y_prompt, y_sample.reshape(db, 1, d),
            p_nsa_kv, p_nsa_win, p_gdn, p_gdn_conv, p_lru.reshape(bsz, RNN_WIDTH), p_lru_conv,
            s_nsa_kv, s_nsa_win, s_gdn, s_gdn_conv, s_lru, s_lru_conv)
```
